```python
import math
import jax, jax.numpy as jnp
from jax import lax
import numpy as np

D_MODEL = 1024
BATCH = 8
SEQ = 4096
DEPTH = 2

D_A = D_MODEL
K_A = 3
D_B = D_MODEL
K_B = 31
N_MEM = 256
N_HEADS = 4
HEAD_DIM = D_MODEL // N_HEADS
D_ATT = N_HEADS * HEAD_DIM
N_BRANCH = 3
D_FF = 2816
K_F = 3
EPS = 1e-6
IN_SPLITS = (3 * D_A, 3 * D_A + 2 * D_B, 3 * D_A + 2 * D_B + D_ATT)
D_IN = 3 * D_A + 2 * D_B + D_ATT + N_BRANCH * D_MODEL

kernel_name = "hybrid_shortconv_conformer_memattn_block"


def _rmsnorm(x, g):
    xf = x.astype(jnp.float32)
    r = lax.rsqrt(jnp.mean(xf * xf, axis=-1, keepdims=True) + EPS)
    return (xf * r).astype(x.dtype) * g


def _layernorm(x, g, b):
    xf = x.astype(jnp.float32)
    mu = jnp.mean(xf, axis=-1, keepdims=True)
    var = jnp.mean(jnp.square(xf - mu), axis=-1, keepdims=True)
    return ((xf - mu) * lax.rsqrt(var + EPS)).astype(x.dtype) * g + b


def _causal_dwconv(x, w):
    k, c = w.shape
    return lax.conv_general_dilated(
        x, w[:, None, :].astype(x.dtype), window_strides=(1,), padding=[(k - 1, 0)],
        dimension_numbers=("NWC", "WIO", "NWC"), feature_group_count=c)


def _fwd_setup_inputs(seed: int = 0) -> dict:
    key = jax.random.key(seed)
    ks = jax.random.split(key, 24)
    L, D = DEPTH, D_MODEL

    def nrm(k, shape, fan_in):
        return jax.random.normal(k, shape, jnp.float32) * (fan_in ** -0.5)

    def gain(k, shape):
        return 1.0 + 0.02 * jax.random.normal(k, shape, jnp.float32)

    def small(k, shape):
        return 0.02 * jax.random.normal(k, shape, jnp.float32)

    return {
        "x": jax.random.normal(ks[0], (BATCH, SEQ, D), jnp.float32),
        "mem": jax.random.normal(ks[1], (BATCH, N_MEM, D), jnp.float32),
        "norm_mix_g": gain(ks[2], (L, D)),
        "norm_mem_g": gain(ks[3], (L, D)),
        "w_in": nrm(ks[4], (L, D, D_IN), D),
        "b_gate": small(ks[5], (L, N_BRANCH * D)),
        "conv_a_w": nrm(ks[6], (L, K_A, D_A), K_A),
        "w_a_out": nrm(ks[7], (L, D_A, D), D_A),
        "conv_b_w": nrm(ks[8], (L, K_B, D_B), K_B),
        "conv_b_bias": small(ks[9], (L, D_B)),
        "ln_b_g": gain(ks[10], (L, D_B)),
        "ln_b_b": small(ks[11], (L, D_B)),
        "w_b_out": nrm(ks[12], (L, D_B, D), D_B),
        "w_kv": nrm(ks[13], (L, D, 2 * D_ATT), D),
        "w_att_out": nrm(ks[14], (L, D_ATT, D), D_ATT),
        "w_o": nrm(ks[15], (L, D, D), D),
        "norm_ffn_g": gain(ks[16], (L, D)),
        "w_up": nrm(ks[17], (L, D, 2 * D_FF), D),
        "conv_ffn_w": nrm(ks[18], (L, K_F, 2 * D_FF), K_F),
        "w_down": nrm(ks[19], (L, D_FF, D), D_FF),
        "norm_final_g": gain(ks[20], (D,)),
    }


def _mixer(x, mem, norm_mix_g, norm_mem_g, w_in, b_gate, conv_a_w, w_a_out,
           conv_b_w, conv_b_bias, ln_b_g, ln_b_b, w_b_out, w_kv, w_att_out, w_o):
    bsz, seq, _ = x.shape
    h = _rmsnorm(x, norm_mix_g)
    proj = h @ w_in
    p_a, p_b, q, p_g = jnp.split(proj, IN_SPLITS, axis=-1)

    gb, gc, v = jnp.split(p_a, 3, axis=-1)
    y_a = (gb * _causal_dwconv(gc * v, conv_a_w)) @ w_a_out

    u, ug = jnp.split(p_b, 2, axis=-1)
    u = u * jax.nn.sigmoid(ug)
    u = _causal_dwconv(u, conv_b_w) + conv_b_bias
    u = jax.nn.silu(_layernorm(u, ln_b_g, ln_b_b))
    y_b = u @ w_b_out

    memn = _rmsnorm(mem, norm_mem_g)
    k, vm = jnp.split(memn @ w_kv, 2, axis=-1)
    qh = q.reshape(bsz, seq, N_HEADS, HEAD_DIM)
    kh = k.reshape(bsz, N_MEM, N_HEADS, HEAD_DIM)
    vh = vm.reshape(bsz, N_MEM, N_HEADS, HEAD_DIM)
    s = jnp.einsum("bshd,bmhd->bhsm", qh, kh).astype(jnp.float32) * (1.0 / math.sqrt(HEAD_DIM))
    pr = jax.nn.softmax(s, axis=-1).astype(x.dtype)
    o = jnp.einsum("bhsm,bmhd->bshd", pr, vh).reshape(bsz, seq, D_ATT)
    y_c = o @ w_att_out

    g = jax.nn.sigmoid((p_g + b_gate).reshape(bsz, seq, N_BRANCH, D_MODEL))
    merged = g[:, :, 0] * y_a + g[:, :, 1] * y_b + g[:, :, 2] * y_c
    return merged @ w_o


def _conv_ffn(x, norm_ffn_g, w_up, conv_ffn_w, w_down):
    h = _rmsnorm(x, norm_ffn_g)
    u = _causal_dwconv(h @ w_up, conv_ffn_w)
    gt, up = jnp.split(u, 2, axis=-1)
    return (jax.nn.silu(gt) * up) @ w_down


def _fwd_reference(x, mem, norm_mix_g, norm_mem_g, w_in, b_gate, conv_a_w, w_a_out,
              conv_b_w, conv_b_bias, ln_b_g, ln_b_b, w_b_out, w_kv, w_att_out, w_o,
              norm_ffn_g, w_up, conv_ffn_w, w_down, norm_final_g):
    for l in range(DEPTH):
        x = x + _mixer(x, mem, norm_mix_g[l], norm_mem_g[l], w_in[l], b_gate[l],
                       conv_a_w[l], w_a_out[l], conv_b_w[l], conv_b_bias[l],
                       ln_b_g[l], ln_b_b[l], w_b_out[l], w_kv[l], w_att_out[l], w_o[l])
        x = x + _conv_ffn(x, norm_ffn_g[l], w_up[l], conv_ffn_w[l], w_down[l])
    return _rmsnorm(x, norm_final_g)


import jax as _jax
import jax.numpy as _jnp

TWIN_FORMAT = 'train_step'
FWD_PARAMS = ['x', 'mem', 'norm_mix_g', 'norm_mem_g', 'w_in', 'b_gate', 'conv_a_w', 'w_a_out', 'conv_b_w', 'conv_b_bias', 'ln_b_g', 'ln_b_b', 'w_b_out', 'w_kv', 'w_att_out', 'w_o', 'norm_ffn_g', 'w_up', 'conv_ffn_w', 'w_down', 'norm_final_g']
TWIN_WEIGHTS = ['norm_mix_g', 'norm_mem_g', 'w_in', 'b_gate', 'conv_a_w', 'w_a_out', 'conv_b_w', 'conv_b_bias', 'ln_b_g', 'ln_b_b', 'w_b_out', 'w_kv', 'w_att_out', 'w_o', 'norm_ffn_g', 'w_up', 'conv_ffn_w', 'w_down', 'norm_final_g']
TWIN_DIFF_INPUT = 'x'
TWIN_INPUTS = ['x', 'mem', 'norm_mix_g', 'norm_mem_g', 'w_in', 'b_gate', 'conv_a_w', 'w_a_out', 'conv_b_w', 'conv_b_bias', 'ln_b_g', 'ln_b_b', 'w_b_out', 'w_kv', 'w_att_out', 'w_o', 'norm_ffn_g', 'w_up', 'conv_ffn_w', 'w_down', 'norm_final_g', 'loss_target', 'm_norm_mix_g', 'm_norm_mem_g', 'm_w_in', 'm_b_gate', 'm_conv_a_w', 'm_w_a_out', 'm_conv_b_w', 'm_conv_b_bias', 'm_ln_b_g', 'm_ln_b_b', 'm_w_b_out', 'm_w_kv', 'm_w_att_out', 'm_w_o', 'm_norm_ffn_g', 'm_w_up', 'm_conv_ffn_w', 'm_w_down', 'm_norm_final_g', 'v_norm_mix_g', 'v_norm_mem_g', 'v_w_in', 'v_b_gate', 'v_conv_a_w', 'v_w_a_out', 'v_conv_b_w', 'v_conv_b_bias', 'v_ln_b_g', 'v_ln_b_b', 'v_w_b_out', 'v_w_kv', 'v_w_att_out', 'v_w_o', 'v_norm_ffn_g', 'v_w_up', 'v_conv_ffn_w', 'v_w_down', 'v_norm_final_g']
TWIN_OUTPUTS = ['loss', 'grad_x', 'grad_norm_mix_g', 'grad_norm_mem_g', 'grad_w_in', 'grad_b_gate', 'grad_conv_a_w', 'grad_w_a_out', 'grad_conv_b_w', 'grad_conv_b_bias', 'grad_ln_b_g', 'grad_ln_b_b', 'grad_w_b_out', 'grad_w_kv', 'grad_w_att_out', 'grad_w_o', 'grad_norm_ffn_g', 'grad_w_up', 'grad_conv_ffn_w', 'grad_w_down', 'grad_norm_final_g', 'delta_norm_mix_g', 'delta_norm_mem_g', 'delta_w_in', 'delta_b_gate', 'delta_conv_a_w', 'delta_w_a_out', 'delta_conv_b_w', 'delta_conv_b_bias', 'delta_ln_b_g', 'delta_ln_b_b', 'delta_w_b_out', 'delta_w_kv', 'delta_w_att_out', 'delta_w_o', 'delta_norm_ffn_g', 'delta_w_up', 'delta_conv_ffn_w', 'delta_w_down', 'delta_norm_final_g', 'new_m_norm_mix_g', 'new_m_norm_mem_g', 'new_m_w_in', 'new_m_b_gate', 'new_m_conv_a_w', 'new_m_w_a_out', 'new_m_conv_b_w', 'new_m_conv_b_bias', 'new_m_ln_b_g', 'new_m_ln_b_b', 'new_m_w_b_out', 'new_m_w_kv', 'new_m_w_att_out', 'new_m_w_o', 'new_m_norm_ffn_g', 'new_m_w_up', 'new_m_conv_ffn_w', 'new_m_w_down', 'new_m_norm_final_g', 'new_v_norm_mix_g', 'new_v_norm_mem_g', 'new_v_w_in', 'new_v_b_gate', 'new_v_conv_a_w', 'new_v_w_a_out', 'new_v_conv_b_w', 'new_v_conv_b_bias', 'new_v_ln_b_g', 'new_v_ln_b_b', 'new_v_w_b_out', 'new_v_w_kv', 'new_v_w_att_out', 'new_v_w_o', 'new_v_norm_ffn_g', 'new_v_w_up', 'new_v_conv_ffn_w', 'new_v_w_down', 'new_v_norm_final_g']
TWIN_LEAF_KINDS = {'loss': 'loss', 'grad_x': 'grad_x', 'grad_norm_mix_g': 'grad_w', 'grad_norm_mem_g': 'grad_w', 'grad_w_in': 'grad_w', 'grad_b_gate': 'grad_w', 'grad_conv_a_w': 'grad_w', 'grad_w_a_out': 'grad_w', 'grad_conv_b_w': 'grad_w', 'grad_conv_b_bias': 'grad_w', 'grad_ln_b_g': 'grad_w', 'grad_ln_b_b': 'grad_w', 'grad_w_b_out': 'grad_w', 'grad_w_kv': 'grad_w', 'grad_w_att_out': 'grad_w', 'grad_w_o': 'grad_w', 'grad_norm_ffn_g': 'grad_w', 'grad_w_up': 'grad_w', 'grad_conv_ffn_w': 'grad_w', 'grad_w_down': 'grad_w', 'grad_norm_final_g': 'grad_w', 'delta_norm_mix_g': 'delta_w', 'delta_norm_mem_g': 'delta_w', 'delta_w_in': 'delta_w', 'delta_b_gate': 'delta_w', 'delta_conv_a_w': 'delta_w', 'delta_w_a_out': 'delta_w', 'delta_conv_b_w': 'delta_w', 'delta_conv_b_bias': 'delta_w', 'delta_ln_b_g': 'delta_w', 'delta_ln_b_b': 'delta_w', 'delta_w_b_out': 'delta_w', 'delta_w_kv': 'delta_w', 'delta_w_att_out': 'delta_w', 'delta_w_o': 'delta_w', 'delta_norm_ffn_g': 'delta_w', 'delta_w_up': 'delta_w', 'delta_conv_ffn_w': 'delta_w', 'delta_w_down': 'delta_w', 'delta_norm_final_g': 'delta_w', 'new_m_norm_mix_g': 'new_m', 'new_m_norm_mem_g': 'new_m', 'new_m_w_in': 'new_m', 'new_m_b_gate': 'new_m', 'new_m_conv_a_w': 'new_m', 'new_m_w_a_out': 'new_m', 'new_m_conv_b_w': 'new_m', 'new_m_conv_b_bias': 'new_m', 'new_m_ln_b_g': 'new_m', 'new_m_ln_b_b': 'new_m', 'new_m_w_b_out': 'new_m', 'new_m_w_kv': 'new_m', 'new_m_w_att_out': 'new_m', 'new_m_w_o': 'new_m', 'new_m_norm_ffn_g': 'new_m', 'new_m_w_up': 'new_m', 'new_m_conv_ffn_w': 'new_m', 'new_m_w_down': 'new_m', 'new_m_norm_final_g': 'new_m', 'new_v_norm_mix_g': 'new_v', 'new_v_norm_mem_g': 'new_v', 'new_v_w_in': 'new_v', 'new_v_b_gate': 'new_v', 'new_v_conv_a_w': 'new_v', 'new_v_w_a_out': 'new_v', 'new_v_conv_b_w': 'new_v', 'new_v_conv_b_bias': 'new_v', 'new_v_ln_b_g': 'new_v', 'new_v_ln_b_b': 'new_v', 'new_v_w_b_out': 'new_v', 'new_v_w_kv': 'new_v', 'new_v_w_att_out': 'new_v', 'new_v_w_o': 'new_v', 'new_v_norm_ffn_g': 'new_v', 'new_v_w_up': 'new_v', 'new_v_conv_ffn_w': 'new_v', 'new_v_w_down': 'new_v', 'new_v_norm_final_g': 'new_v'}


def _forward(args):
    return _fwd_reference(*[args[k] for k in FWD_PARAMS])


def _output_shape():
    out = _jax.eval_shape(lambda: _forward(_fwd_setup_inputs(0)))
    return out.shape, out.dtype

N_MICROBATCH = 1
ADAM_LR = 0.001
ADAM_B1 = 0.9
ADAM_B2 = 0.999
ADAM_EPS = 1e-08
ADAM_WD = 0.01
ADAM_STEP = 10
PER_EXAMPLE_BATCH_AXIS = {'x': 0, 'mem': 0, 'loss_target': 0}
SHARED_INPUTS = []
_WEIGHT_DTYPES = {'norm_mix_g': _jnp.float32, 'norm_mem_g': _jnp.float32, 'w_in': _jnp.float32, 'b_gate': _jnp.float32, 'conv_a_w': _jnp.float32, 'w_a_out': _jnp.float32, 'conv_b_w': _jnp.float32, 'conv_b_bias': _jnp.float32, 'ln_b_g': _jnp.float32, 'ln_b_b': _jnp.float32, 'w_b_out': _jnp.float32, 'w_kv': _jnp.float32, 'w_att_out': _jnp.float32, 'w_o': _jnp.float32, 'norm_ffn_g': _jnp.float32, 'w_up': _jnp.float32, 'conv_ffn_w': _jnp.float32, 'w_down': _jnp.float32, 'norm_final_g': _jnp.float32}
MOMENT_SCALE = {'norm_mix_g': 1.833239e-01, 'norm_mem_g': 1.438165e-02, 'w_in': 5.919509e-02, 'b_gate': 2.581123e-02, 'conv_a_w': 9.550713e-02, 'w_a_out': 9.311064e-02, 'conv_b_w': 5.792073e-02, 'conv_b_bias': 1.259172e-01, 'ln_b_g': 6.724201e-02, 'ln_b_b': 5.759757e-02, 'w_b_out': 5.576063e-02, 'w_kv': 9.807323e-03, 'w_att_out': 9.912886e-03, 'w_o': 1.088536e-01, 'norm_ffn_g': 1.197705e-01, 'w_up': 5.179526e-02, 'conv_ffn_w': 5.262684e-02, 'w_down': 8.456423e-02, 'norm_final_g': 3.199809e+01}


def _to_microbatches(a, axis):
    t = _jnp.moveaxis(a, axis, 0)
    t = t.reshape((N_MICROBATCH, t.shape[0] // N_MICROBATCH) + t.shape[1:])
    return _jnp.moveaxis(t, 1, axis + 1)


def setup_inputs(seed: int = 0) -> dict:
    inp = _fwd_setup_inputs(seed)
    key = _jax.random.fold_in(_jax.random.key(seed), 7919)
    shape, _ = _output_shape()
    out = dict(inp)
    out["loss_target"] = _jax.random.normal(_jax.random.fold_in(key, 0), shape, _jnp.float32)
    for i, name in enumerate(TWIN_WEIGHTS):
        w = inp[name].astype(_jnp.float32)
        if MOMENT_SCALE is None:
            s = _jnp.sqrt(_jnp.mean(_jnp.square(w)) + 1e-30)
        else:
            s = MOMENT_SCALE[name]
        km, kv = _jax.random.split(_jax.random.fold_in(key, i + 1))
        out[name] = w
        out["m_" + name] = s * _jax.random.normal(km, w.shape, _jnp.float32)
        out["v_" + name] = (s * s) * _jax.random.uniform(kv, w.shape, _jnp.float32, 0.5, 1.5)
    if N_MICROBATCH > 1:
        for name, axis in PER_EXAMPLE_BATCH_AXIS.items():
            out[name] = _to_microbatches(out[name], axis)
    return {'x': out['x'], 'mem': out['mem'], 'norm_mix_g': out['norm_mix_g'], 'norm_mem_g': out['norm_mem_g'], 'w_in': out['w_in'], 'b_gate': out['b_gate'], 'conv_a_w': out['conv_a_w'], 'w_a_out': out['w_a_out'], 'conv_b_w': out['conv_b_w'], 'conv_b_bias': out['conv_b_bias'], 'ln_b_g': out['ln_b_g'], 'ln_b_b': out['ln_b_b'], 'w_b_out': out['w_b_out'], 'w_kv': out['w_kv'], 'w_att_out': out['w_att_out'], 'w_o': out['w_o'], 'norm_ffn_g': out['norm_ffn_g'], 'w_up': out['w_up'], 'conv_ffn_w': out['conv_ffn_w'], 'w_down': out['w_down'], 'norm_final_g': out['norm_final_g'], 'loss_target': out['loss_target'], 'm_norm_mix_g': out['m_norm_mix_g'], 'm_norm_mem_g': out['m_norm_mem_g'], 'm_w_in': out['m_w_in'], 'm_b_gate': out['m_b_gate'], 'm_conv_a_w': out['m_conv_a_w'], 'm_w_a_out': out['m_w_a_out'], 'm_conv_b_w': out['m_conv_b_w'], 'm_conv_b_bias': out['m_conv_b_bias'], 'm_ln_b_g': out['m_ln_b_g'], 'm_ln_b_b': out['m_ln_b_b'], 'm_w_b_out': out['m_w_b_out'], 'm_w_kv': out['m_w_kv'], 'm_w_att_out': out['m_w_att_out'], 'm_w_o': out['m_w_o'], 'm_norm_ffn_g': out['m_norm_ffn_g'], 'm_w_up': out['m_w_up'], 'm_conv_ffn_w': out['m_conv_ffn_w'], 'm_w_down': out['m_w_down'], 'm_norm_final_g': out['m_norm_final_g'], 'v_norm_mix_g': out['v_norm_mix_g'], 'v_norm_mem_g': out['v_norm_mem_g'], 'v_w_in': out['v_w_in'], 'v_b_gate': out['v_b_gate'], 'v_conv_a_w': out['v_conv_a_w'], 'v_w_a_out': out['v_w_a_out'], 'v_conv_b_w': out['v_conv_b_w'], 'v_conv_b_bias': out['v_conv_b_bias'], 'v_ln_b_g': out['v_ln_b_g'], 'v_ln_b_b': out['v_ln_b_b'], 'v_w_b_out': out['v_w_b_out'], 'v_w_kv': out['v_w_kv'], 'v_w_att_out': out['v_w_att_out'], 'v_w_o': out['v_w_o'], 'v_norm_ffn_g': out['v_norm_ffn_g'], 'v_w_up': out['v_w_up'], 'v_conv_ffn_w': out['v_conv_ffn_w'], 'v_w_down': out['v_w_down'], 'v_norm_final_g': out['v_norm_final_g']}


def _loss(weights, diff, rest, loss_target):
    with _jax.named_scope("forward"):
        args = {**rest, TWIN_DIFF_INPUT: diff, **{k: w.astype(_WEIGHT_DTYPES[k]) for k, w in weights.items()}}
        y = _forward(args)
    with _jax.named_scope("loss_head"):
        err = _jnp.square(y.astype(_jnp.float32) - loss_target)
        return 0.5 * _jnp.sum(_jnp.mean(err, axis=-1)) if err.ndim else 0.5 * err


def _adamw(w, g, m, v):
    m = ADAM_B1 * m + (1.0 - ADAM_B1) * g
    v = ADAM_B2 * v + (1.0 - ADAM_B2) * _jnp.square(g)
    m_hat = m / (1.0 - ADAM_B1 ** ADAM_STEP)
    v_hat = v / (1.0 - ADAM_B2 ** ADAM_STEP)
    delta = -ADAM_LR * (m_hat / (_jnp.sqrt(v_hat) + ADAM_EPS) + ADAM_WD * w)
    return delta, m, v


def reference(x, mem, norm_mix_g, norm_mem_g, w_in, b_gate, conv_a_w, w_a_out, conv_b_w, conv_b_bias, ln_b_g, ln_b_b, w_b_out, w_kv, w_att_out, w_o, norm_ffn_g, w_up, conv_ffn_w, w_down, norm_final_g, loss_target, m_norm_mix_g, m_norm_mem_g, m_w_in, m_b_gate, m_conv_a_w, m_w_a_out, m_conv_b_w, m_conv_b_bias, m_ln_b_g, m_ln_b_b, m_w_b_out, m_w_kv, m_w_att_out, m_w_o, m_norm_ffn_g, m_w_up, m_conv_ffn_w, m_w_down, m_norm_final_g, v_norm_mix_g, v_norm_mem_g, v_w_in, v_b_gate, v_conv_a_w, v_w_a_out, v_conv_b_w, v_conv_b_bias, v_ln_b_g, v_ln_b_b, v_w_b_out, v_w_kv, v_w_att_out, v_w_o, v_norm_ffn_g, v_w_up, v_conv_ffn_w, v_w_down, v_norm_final_g):
    given = dict(x=x, mem=mem, norm_mix_g=norm_mix_g, norm_mem_g=norm_mem_g, w_in=w_in, b_gate=b_gate, conv_a_w=conv_a_w, w_a_out=w_a_out, conv_b_w=conv_b_w, conv_b_bias=conv_b_bias, ln_b_g=ln_b_g, ln_b_b=ln_b_b, w_b_out=w_b_out, w_kv=w_kv, w_att_out=w_att_out, w_o=w_o, norm_ffn_g=norm_ffn_g, w_up=w_up, conv_ffn_w=conv_ffn_w, w_down=w_down, norm_final_g=norm_final_g, loss_target=loss_target, m_norm_mix_g=m_norm_mix_g, m_norm_mem_g=m_norm_mem_g, m_w_in=m_w_in, m_b_gate=m_b_gate, m_conv_a_w=m_conv_a_w, m_w_a_out=m_w_a_out, m_conv_b_w=m_conv_b_w, m_conv_b_bias=m_conv_b_bias, m_ln_b_g=m_ln_b_g, m_ln_b_b=m_ln_b_b, m_w_b_out=m_w_b_out, m_w_kv=m_w_kv, m_w_att_out=m_w_att_out, m_w_o=m_w_o, m_norm_ffn_g=m_norm_ffn_g, m_w_up=m_w_up, m_conv_ffn_w=m_conv_ffn_w, m_w_down=m_w_down, m_norm_final_g=m_norm_final_g, v_norm_mix_g=v_norm_mix_g, v_norm_mem_g=v_norm_mem_g, v_w_in=v_w_in, v_b_gate=v_b_gate, v_conv_a_w=v_conv_a_w, v_w_a_out=v_w_a_out, v_conv_b_w=v_conv_b_w, v_conv_b_bias=v_conv_b_bias, v_ln_b_g=v_ln_b_g, v_ln_b_b=v_ln_b_b, v_w_b_out=v_w_b_out, v_w_kv=v_w_kv, v_w_att_out=v_w_att_out, v_w_o=v_w_o, v_norm_ffn_g=v_norm_ffn_g, v_w_up=v_w_up, v_conv_ffn_w=v_conv_ffn_w, v_w_down=v_w_down, v_norm_final_g=v_norm_final_g)
    weights = {n: given[n] for n in TWIN_WEIGHTS}
    shared = {n: given[n] for n in SHARED_INPUTS}
    per_example = {n: given[n] for n in ['x', 'mem']}
    grad_fn = _jax.value_and_grad(_loss, argnums=(0, 1))

    def one_microbatch(ex, loss_target):
        ex = dict(ex)
        diff = ex.pop(TWIN_DIFF_INPUT)
        return grad_fn(weights, diff, {**shared, **ex}, loss_target)

    if N_MICROBATCH == 1:
        loss, (grad_w, grad_x) = one_microbatch(per_example, given["loss_target"])
    else:
        def body(carry, xs):
            loss_sum, grad_sum = carry
            l_k, (gw_k, gx_k) = one_microbatch(xs[0], xs[1])
            with _jax.named_scope("update"):
                return (loss_sum + l_k, _jax.tree.map(_jnp.add, grad_sum, gw_k)), gx_k

        init = (_jnp.zeros((), _jnp.float32), _jax.tree.map(_jnp.zeros_like, weights))
        (loss, grad_w), grad_x = _jax.lax.scan(body, init, (per_example, given["loss_target"]))
    with _jax.named_scope("update"):
        delta_w, new_m, new_v = {}, {}, {}
        for n in TWIN_WEIGHTS:
            delta_w[n], new_m[n], new_v[n] = _adamw(weights[n], grad_w[n], given["m_" + n], given["v_" + n])
    return (loss, grad_x, *[grad_w[n] for n in TWIN_WEIGHTS], *[delta_w[n] for n in TWIN_WEIGHTS],
            *[new_m[n] for n in TWIN_WEIGHTS], *[new_v[n] for n in TWIN_WEIGHTS])
```

```python
import functools
import math

import jax
import jax.numpy as jnp
from jax import lax
from jax.experimental import pallas as pl
from jax.experimental.pallas import tpu as pltpu

F32 = jnp.float32
BF = jnp.bfloat16
EPS = 1e-6
N_HEADS = 4
K_A, K_B, K_F = 3, 31, 3
ADAM_LR, ADAM_B1, ADAM_B2, ADAM_EPS, ADAM_WD, ADAM_STEP = 0.001, 0.9, 0.999, 1e-08, 0.01, 10
N_CHIPS = 4
N_DEV = 8
HALO = 32
TM_ROW = 256
TM_MM = 1024
TT_DW = 512
TR_EW = 128
VMEM_LIMIT = 56 * 1024 * 1024
MESH = pl.DeviceIdType.MESH
_pallas_call = pl.pallas_call


def _params(n_axes):
    return pltpu.CompilerParams(dimension_semantics=("arbitrary",) * n_axes, vmem_limit_bytes=VMEM_LIMIT)


def _resident(shape, index):
    return pl.BlockSpec(shape, lambda *_: index, pipeline_mode=pl.Buffered(1))


def _sig(x):
    return 1.0 / (1.0 + jnp.exp(-x))


def _nt(a, b):
    return lax.dot_general(a, b, (((1,), (1,)), ((), ())), preferred_element_type=F32)


def _tn(a, b):
    return lax.dot_general(a, b, (((0,), (0,)), ((), ())), preferred_element_type=F32)


def _nn(a, b):
    return jnp.dot(a, b, preferred_element_type=F32)


def _causal_taps(xx_ref, w_ref, k_taps, tm):
    base = HALO - k_taps + 1
    acc = w_ref[0:1, :] * xx_ref[pl.ds(base, tm), :]
    for k in range(1, k_taps):
        acc = acc + w_ref[k:k + 1, :] * xx_ref[pl.ds(base + k, tm), :]
    return acc


def _anticausal_taps(yy_ref, w_ref, k_taps, tm):
    acc = w_ref[k_taps - 1:k_taps, :] * yy_ref[pl.ds(0, tm), :]
    for k in range(k_taps - 1):
        acc = acc + w_ref[k:k + 1, :] * yy_ref[pl.ds(k_taps - 1 - k, tm), :]
    return acc


def _tap_grads(dw_ref, dy, xx_ref, k_taps, tm):
    base = HALO - k_taps + 1
    for k in range(k_taps):
        dw_ref[k:k + 1, :] += jnp.sum(dy * xx_ref[pl.ds(base + k, tm), :], axis=0, keepdims=True)


def _prev_halo(tm, col):
    return lambda i: (jnp.maximum(i * (tm // HALO) - 1, 0), col)


def _next_halo(tm, n_rows, col):
    return lambda i: (jnp.minimum((i + 1) * (tm // HALO), n_rows // HALO - 1), col)


def _norm_matmul(x, g, w, l, tn, name):
    t, d = x.shape
    n = w.shape[2]
    tm = min(TM_MM, t)

    def body(x_ref, g_ref, w_ref, h_ref, y_ref):
        @pl.when(pl.program_id(1) == 0)
        def _():
            xf = x_ref[...]
            r = lax.rsqrt(jnp.mean(xf * xf, axis=-1, keepdims=True) + EPS)
            h_ref[...] = ((xf * r) * g_ref[...]).astype(BF)

        y_ref[...] = _nn(h_ref[...], w_ref[...]).astype(BF)

    return _pallas_call(
        body, name=name, grid=(t // tm, n // tn),
        in_specs=[pl.BlockSpec((tm, d), lambda i, j: (i, 0)),
                  pl.BlockSpec((1, d), lambda i, j: (0, 0)),
                  pl.BlockSpec((None, d, tn), lambda i, j: (l, 0, j))],
        out_specs=[pl.BlockSpec((tm, d), lambda i, j: (i, 0)),
                   pl.BlockSpec((tm, tn), lambda i, j: (i, j))],
        out_shape=[jax.ShapeDtypeStruct((t, d), BF), jax.ShapeDtypeStruct((t, n), BF)],
        compiler_params=_params(2),
    )(x, g, w)


def _mem_kv(mem, g, w_kv, l, name):
    m, d = mem.shape

    def body(mem_ref, g_ref, w_ref, memn_ref, kv_ref):
        xf = mem_ref[...]
        r = lax.rsqrt(jnp.mean(xf * xf, axis=-1, keepdims=True) + EPS)
        memn = ((xf * r) * g_ref[...]).astype(BF)
        memn_ref[...] = memn
        kv_ref[...] = _nn(memn, w_ref[...]).astype(BF)

    return _pallas_call(
        body, name=name, grid=(1,),
        in_specs=[pl.BlockSpec((m, d), lambda i: (0, 0)),
                  pl.BlockSpec((1, d), lambda i: (0, 0)),
                  pl.BlockSpec((None, d, 2 * d), lambda i: (l, 0, 0))],
        out_specs=[pl.BlockSpec((m, d), lambda i: (0, 0)),
                   pl.BlockSpec((m, 2 * d), lambda i: (0, 0))],
        out_shape=[jax.ShapeDtypeStruct((m, d), BF), jax.ShapeDtypeStruct((m, 2 * d), BF)],
        compiler_params=_params(1),
    )(mem, g, w_kv)


def _load_branch_inputs(i, proj_ref, gch_ref, vh_ref, u0h_ref, ugh_ref, xa_ref, xb_ref, d):
    gc = proj_ref[:, d:2 * d].astype(F32)
    v = proj_ref[:, 2 * d:3 * d].astype(F32)
    u0 = proj_ref[:, 3 * d:4 * d].astype(F32)
    ug = proj_ref[:, 4 * d:5 * d].astype(F32)
    keep = (i > 0).astype(F32)
    xa_ref[pl.ds(0, HALO), :] = gch_ref[...].astype(F32) * vh_ref[...].astype(F32) * keep
    xa_ref[pl.ds(HALO, gc.shape[0]), :] = gc * v
    xb_ref[pl.ds(0, HALO), :] = u0h_ref[...].astype(F32) * _sig(ugh_ref[...].astype(F32)) * keep
    xb_ref[pl.ds(HALO, gc.shape[0]), :] = u0 * _sig(ug)


def _softmax_rows(s):
    e = jnp.exp(s - jnp.max(s, axis=-1, keepdims=True))
    return e / jnp.sum(e, axis=-1, keepdims=True)


def _mixer_fwd(proj, x, kv, conv_a, conv_b, cbias, ln_g, ln_b, b_gate, wsq, l, name):
    t, d = x.shape
    m = kv.shape[0]
    tm = min(TM_ROW, t)
    hd = d // N_HEADS
    scale = 1.0 / math.sqrt(hd)

    def body(proj_ref, gch_ref, vh_ref, u0h_ref, ugh_ref, x_ref, kv_ref, ca_w, cb_w, cbias_ref, lng_ref, lnb_ref,
             bg_ref, wa_ref, wb_ref, wc_ref, wo_ref,
             x1_ref, za_ref, zb_ref, o_ref, ya_ref, yb_ref, yc_ref, mg_ref, cb_ref, xa_ref, xb_ref):
        i = pl.program_id(0)
        _load_branch_inputs(i, proj_ref, gch_ref, vh_ref, u0h_ref, ugh_ref, xa_ref, xb_ref, d)
        gb = proj_ref[:, 0:d].astype(F32)
        za = (gb * _causal_taps(xa_ref, ca_w, K_A, tm)).astype(BF)
        za_ref[...] = za
        ya = _nn(za, wa_ref[...])
        ya_ref[...] = ya.astype(BF)
        cb = _causal_taps(xb_ref, cb_w, K_B, tm) + cbias_ref[...]
        cb_ref[...] = cb
        mu = jnp.mean(cb, axis=-1, keepdims=True)
        dlt = cb - mu
        rstd = lax.rsqrt(jnp.mean(dlt * dlt, axis=-1, keepdims=True) + EPS)
        lnb = (dlt * rstd) * lng_ref[...] + lnb_ref[...]
        zb = (lnb * _sig(lnb)).astype(BF)
        zb_ref[...] = zb
        yb = _nn(zb, wb_ref[...])
        yb_ref[...] = yb.astype(BF)
        for h in range(N_HEADS):
            qh = proj_ref[:, 5 * d + h * hd:5 * d + (h + 1) * hd]
            kh = kv_ref[:, h * hd:(h + 1) * hd]
            vh = kv_ref[:, d + h * hd:d + (h + 1) * hd]
            p = _softmax_rows(_nt(qh, kh) * scale)
            o_ref[:, h * hd:(h + 1) * hd] = _nn(p.astype(BF), vh).astype(BF)
        yc = _nn(o_ref[...], wc_ref[...])
        yc_ref[...] = yc.astype(BF)
        g0 = _sig(proj_ref[:, 6 * d:7 * d].astype(F32) + bg_ref[:, 0:d])
        g1 = _sig(proj_ref[:, 7 * d:8 * d].astype(F32) + bg_ref[:, d:2 * d])
        g2 = _sig(proj_ref[:, 8 * d:9 * d].astype(F32) + bg_ref[:, 2 * d:3 * d])
        mg = (g0 * ya + g1 * yb + g2 * yc).astype(BF)
        mg_ref[...] = mg
        x1_ref[...] = x_ref[...] + _nn(mg, wo_ref[...])

    row = lambda w_: pl.BlockSpec((tm, w_), lambda i: (i, 0))
    halo = lambda col: pl.BlockSpec((HALO, d), _prev_halo(tm, col))
    sq = lambda which: _resident((None, None, d, d), (l, which, 0, 0))
    act = jax.ShapeDtypeStruct((t, d), BF)
    return _pallas_call(
        body, name=name, grid=(t // tm,),
        in_specs=[row(9 * d), halo(1), halo(2), halo(3), halo(4), row(d),
                  _resident((m, 2 * d), (0, 0)),
                  _resident((None, K_A, d), (l, 0, 0)), _resident((None, K_B, d), (l, 0, 0)),
                  _resident((1, d), (0, 0)), _resident((1, d), (0, 0)), _resident((1, d), (0, 0)),
                  _resident((1, 3 * d), (0, 0)), sq(0), sq(1), sq(2), sq(3)],
        out_specs=[row(d)] * 9,
        out_shape=[jax.ShapeDtypeStruct((t, d), F32)] + [act] * 7 + [jax.ShapeDtypeStruct((t, d), F32)],
        scratch_shapes=[pltpu.VMEM((HALO + tm, d), F32), pltpu.VMEM((HALO + tm, d), F32)],
        compiler_params=_params(1),
    )(proj, proj, proj, proj, proj, x, kv, conv_a, conv_b, cbias, ln_g, ln_b, b_gate, wsq, wsq, wsq, wsq)


def _ffn_down_fwd(up, x1, conv_f, w_down, l, name):
    t, d = x1.shape
    f2 = up.shape[1]
    f = f2 // 2
    tm = min(TM_ROW, t)

    def body(up_ref, uph_ref, x1_ref, cw_ref, wd_ref, x2_ref, zf_ref, xx_ref):
        i = pl.program_id(0)
        xx_ref[pl.ds(0, HALO), :] = uph_ref[...].astype(F32) * (i > 0).astype(F32)
        xx_ref[pl.ds(HALO, tm), :] = up_ref[...].astype(F32)
        uc = _causal_taps(xx_ref, cw_ref, K_F, tm)
        gt = uc[:, 0:f]
        zf = (gt * _sig(gt) * uc[:, f:f2]).astype(BF)
        zf_ref[...] = zf
        x2_ref[...] = x1_ref[...] + _nn(zf, wd_ref[...])

    return _pallas_call(
        body, name=name, grid=(t // tm,),
        in_specs=[pl.BlockSpec((tm, f2), lambda i: (i, 0)),
                  pl.BlockSpec((HALO, f2), _prev_halo(tm, 0)),
                  pl.BlockSpec((tm, d), lambda i: (i, 0)),
                  _resident((None, K_F, f2), (l, 0, 0)),
                  _resident((None, f, d), (l, 0, 0))],
        out_specs=[pl.BlockSpec((tm, d), lambda i: (i, 0)), pl.BlockSpec((tm, f), lambda i: (i, 0))],
        out_shape=[jax.ShapeDtypeStruct((t, d), F32), jax.ShapeDtypeStruct((t, f), BF)],
        scratch_shapes=[pltpu.VMEM((HALO + tm, f2), F32)],
        compiler_params=_params(1),
    )(up, up, x1, conv_f, w_down)


def _final_loss(x, g, target, name):
    t, d = x.shape
    tm = min(2 * TM_ROW, t)

    def body(x_ref, g_ref, t_ref, dx_ref, loss_ref, dg_ref):
        @pl.when(pl.program_id(0) == 0)
        def _():
            loss_ref[...] = jnp.zeros_like(loss_ref)
            dg_ref[...] = jnp.zeros_like(dg_ref)

        xf = x_ref[...]
        r = lax.rsqrt(jnp.mean(xf * xf, axis=-1, keepdims=True) + EPS)
        xhat = xf * r
        err = xhat * g_ref[...] - t_ref[...]
        loss_ref[...] += (0.5 / d) * jnp.sum(err * err)
        dy = err * (1.0 / d)
        dg_ref[...] += jnp.sum(dy * xhat, axis=0, keepdims=True)
        dxh = dy * g_ref[...]
        dx_ref[...] = r * (dxh - xhat * jnp.mean(dxh * xhat, axis=-1, keepdims=True))

    return _pallas_call(
        body, name=name, grid=(t // tm,),
        in_specs=[pl.BlockSpec((tm, d), lambda i: (i, 0)), pl.BlockSpec((1, d), lambda i: (0, 0)),
                  pl.BlockSpec((tm, d), lambda i: (i, 0))],
        out_specs=[pl.BlockSpec((tm, d), lambda i: (i, 0)), pl.BlockSpec((8, 128), lambda i: (0, 0)),
                   pl.BlockSpec((1, d), lambda i: (0, 0))],
        out_shape=[jax.ShapeDtypeStruct((t, d), F32), jax.ShapeDtypeStruct((8, 128), F32),
                   jax.ShapeDtypeStruct((1, d), F32)],
        compiler_params=_params(1),
    )(x, g, target)


def _ffn_down_bwd(dx2, up, conv_f, w_down, l, name):
    t, d = dx2.shape
    f2 = up.shape[1]
    f = f2 // 2
    tm = min(TM_ROW, t)

    def body(dx2_ref, up_ref, uph_ref, cw_ref, wd_ref, duc_ref, dx2b_ref, dcw_ref, xx_ref):
        i = pl.program_id(0)

        @pl.when(i == 0)
        def _():
            dcw_ref[...] = jnp.zeros_like(dcw_ref)

        xx_ref[pl.ds(0, HALO), :] = uph_ref[...].astype(F32) * (i > 0).astype(F32)
        xx_ref[pl.ds(HALO, tm), :] = up_ref[...].astype(F32)
        uc = _causal_taps(xx_ref, cw_ref, K_F, tm)
        gt = uc[:, 0:f]
        sg = _sig(gt)
        dx2b = dx2_ref[...].astype(BF)
        dx2b_ref[...] = dx2b
        dzf = _nt(dx2b, wd_ref[...])
        duc_ref[:, 0:f] = (dzf * uc[:, f:f2] * (sg * (1.0 + gt * (1.0 - sg)))).astype(BF)
        duc_ref[:, f:f2] = (dzf * (gt * sg)).astype(BF)
        _tap_grads(dcw_ref, duc_ref[...].astype(F32), xx_ref, K_F, tm)

    return _pallas_call(
        body, name=name, grid=(t // tm,),
        in_specs=[pl.BlockSpec((tm, d), lambda i: (i, 0)),
                  pl.BlockSpec((tm, f2), lambda i: (i, 0)),
                  pl.BlockSpec((HALO, f2), _prev_halo(tm, 0)),
                  _resident((None, K_F, f2), (l, 0, 0)),
                  _resident((None, f, d), (l, 0, 0))],
        out_specs=[pl.BlockSpec((tm, f2), lambda i: (i, 0)), pl.BlockSpec((tm, d), lambda i: (i, 0)),
                   pl.BlockSpec((K_F, f2), lambda i: (0, 0))],
        out_shape=[jax.ShapeDtypeStruct((t, f2), BF), jax.ShapeDtypeStruct((t, d), BF),
                   jax.ShapeDtypeStruct((K_F, f2), F32)],
        scratch_shapes=[pltpu.VMEM((HALO + tm, f2), F32)],
        compiler_params=_params(1),
    )(dx2, up, up, conv_f, w_down)


def _ffn_conv_bwd(duc, conv_f, l, name):
    t, f2 = duc.shape
    tm = min(TM_ROW, t)
    n_t = t // tm

    def body(duc_ref, nxt_ref, cw_ref, dup_ref, yy_ref):
        i = pl.program_id(0)
        yy_ref[pl.ds(0, tm), :] = duc_ref[...].astype(F32)
        yy_ref[pl.ds(tm, HALO), :] = nxt_ref[...].astype(F32) * (i < n_t - 1).astype(F32)
        dup_ref[...] = _anticausal_taps(yy_ref, cw_ref, K_F, tm).astype(BF)

    return _pallas_call(
        body, name=name, grid=(n_t,),
        in_specs=[pl.BlockSpec((tm, f2), lambda i: (i, 0)),
                  pl.BlockSpec((HALO, f2), _next_halo(tm, t, 0)),
                  _resident((None, K_F, f2), (l, 0, 0))],
        out_specs=pl.BlockSpec((tm, f2), lambda i: (i, 0)),
        out_shape=jax.ShapeDtypeStruct((t, f2), BF),
        scratch_shapes=[pltpu.VMEM((tm + HALO, f2), F32)],
        compiler_params=_params(1),
    )(duc, duc, conv_f)


def _nt_matmul_norm_bwd(dy, w, l, x, g, dres, tk, name):
    t, n = dy.shape
    d = x.shape[1]
    tm = min(TM_MM // 2, t)
    n_k = n // tk

    def body(dy_ref, w_ref, x_ref, g_ref, dres_ref, dx_ref, dxb_ref, dg_ref, acc_ref):
        i, k = pl.program_id(0), pl.program_id(1)

        @pl.when((i == 0) & (k == 0))
        def _():
            dg_ref[...] = jnp.zeros_like(dg_ref)

        @pl.when(k == 0)
        def _():
            acc_ref[...] = jnp.zeros_like(acc_ref)

        acc_ref[...] += _nt(dy_ref[...], w_ref[...])

        @pl.when(k == n_k - 1)
        def _():
            xf = x_ref[...]
            r = lax.rsqrt(jnp.mean(xf * xf, axis=-1, keepdims=True) + EPS)
            xhat = xf * r
            dh = acc_ref[...]
            dg_ref[...] += jnp.sum(dh * xhat, axis=0, keepdims=True)
            dxh = dh * g_ref[...]
            dx = dres_ref[...] + r * (dxh - xhat * jnp.mean(dxh * xhat, axis=-1, keepdims=True))
            dx_ref[...] = dx
            dxb_ref[...] = dx.astype(BF)

    return _pallas_call(
        body, name=name, grid=(t // tm, n_k),
        in_specs=[pl.BlockSpec((tm, tk), lambda i, k: (i, k)),
                  pl.BlockSpec((None, d, tk), lambda i, k: (l, 0, k)),
                  pl.BlockSpec((tm, d), lambda i, k: (i, 0)),
                  pl.BlockSpec((1, d), lambda i, k: (0, 0)),
                  pl.BlockSpec((tm, d), lambda i, k: (i, 0))],
        out_specs=[pl.BlockSpec((tm, d), lambda i, k: (i, 0)), pl.BlockSpec((tm, d), lambda i, k: (i, 0)),
                   pl.BlockSpec((1, d), lambda i, k: (0, 0))],
        out_shape=[jax.ShapeDtypeStruct((t, d), F32), jax.ShapeDtypeStruct((t, d), BF),
                   jax.ShapeDtypeStruct((1, d), F32)],
        scratch_shapes=[pltpu.VMEM((tm, d), F32)],
        compiler_params=_params(2),
    )(dy, w, x, g, dres)


def _mixer_bwd(dx1b, proj, ya, yb, yc, cb, kv, conv_a, conv_b, ln_g, ln_b, b_gate, wsq, l, name):
    t, d = cb.shape
    m = kv.shape[0]
    tm = min(TM_ROW, t)
    hd = d // N_HEADS
    scale = 1.0 / math.sqrt(hd)

    def body(dx1b_ref, proj_ref, gch_ref, vh_ref, u0h_ref, ugh_ref, ya_ref, yb_ref, yc_ref, cb_ref, kv_ref,
             ca_w, cb_w, lng_ref, lnb_ref, bg_ref, wa_ref, wb_ref, wc_ref, wo_ref,
             dpre_ref, dya_ref, dyb_ref, dyc_ref, dkv_ref, dbg_ref, dlng_ref, dlnb_ref, dcbias_ref, dcaw_ref,
             dcbw_ref, xa_ref, xb_ref):
        i = pl.program_id(0)

        @pl.when(i == 0)
        def _():
            for ref in (dkv_ref, dbg_ref, dlng_ref, dlnb_ref, dcbias_ref, dcaw_ref, dcbw_ref):
                ref[...] = jnp.zeros_like(ref)

        _load_branch_inputs(i, proj_ref, gch_ref, vh_ref, u0h_ref, ugh_ref, xa_ref, xb_ref, d)
        dmg = _nt(dx1b_ref[...], wo_ref[...])
        ys = (ya_ref, yb_ref, yc_ref)
        dys = (dya_ref, dyb_ref, dyc_ref)
        for b in range(3):
            gate = _sig(proj_ref[:, (6 + b) * d:(7 + b) * d].astype(F32) + bg_ref[:, b * d:(b + 1) * d])
            dys[b][...] = (gate * dmg).astype(BF)
            dpg = dmg * ys[b][...].astype(F32) * gate * (1.0 - gate)
            dpre_ref[:, (6 + b) * d:(7 + b) * d] = dpg.astype(BF)
            dbg_ref[:, b * d:(b + 1) * d] += jnp.sum(dpg, axis=0, keepdims=True)
        gb = proj_ref[:, 0:d].astype(F32)
        ca = _causal_taps(xa_ref, ca_w, K_A, tm)
        dza = _nt(dya_ref[...], wa_ref[...])
        dpre_ref[:, 0:d] = (dza * ca).astype(BF)
        dpre_ref[:, d:2 * d] = (dza * gb).astype(BF)
        _tap_grads(dcaw_ref, dpre_ref[:, d:2 * d].astype(F32), xa_ref, K_A, tm)
        dpre_ref[:, 2 * d:3 * d] = jnp.zeros((tm, d), BF)
        cbv = cb_ref[...]
        mu = jnp.mean(cbv, axis=-1, keepdims=True)
        dlt = cbv - mu
        rstd = lax.rsqrt(jnp.mean(dlt * dlt, axis=-1, keepdims=True) + EPS)
        xhat = dlt * rstd
        lnb = xhat * lng_ref[...] + lnb_ref[...]
        sg = _sig(lnb)
        dzb = _nt(dyb_ref[...], wb_ref[...])
        dl = dzb * (sg * (1.0 + lnb * (1.0 - sg)))
        dlng_ref[...] += jnp.sum(dl * xhat, axis=0, keepdims=True)
        dlnb_ref[...] += jnp.sum(dl, axis=0, keepdims=True)
        dxh = dl * lng_ref[...]
        dcb = rstd * (dxh - jnp.mean(dxh, axis=-1, keepdims=True)
                      - xhat * jnp.mean(dxh * xhat, axis=-1, keepdims=True))
        dcbias_ref[...] += jnp.sum(dcb, axis=0, keepdims=True)
        dpre_ref[:, 3 * d:4 * d] = dcb.astype(BF)
        _tap_grads(dcbw_ref, dpre_ref[:, 3 * d:4 * d].astype(F32), xb_ref, K_B, tm)
        dpre_ref[:, 4 * d:5 * d] = jnp.zeros((tm, d), BF)
        do = _nt(dyc_ref[...], wc_ref[...]).astype(BF)
        for h in range(N_HEADS):
            qh = proj_ref[:, 5 * d + h * hd:5 * d + (h + 1) * hd]
            kh = kv_ref[:, h * hd:(h + 1) * hd]
            vh = kv_ref[:, d + h * hd:d + (h + 1) * hd]
            doh = do[:, h * hd:(h + 1) * hd]
            p = _softmax_rows(_nt(qh, kh) * scale)
            dp = _nt(doh, vh)
            ds = (p * (dp - jnp.sum(dp * p, axis=-1, keepdims=True)) * scale).astype(BF)
            dpre_ref[:, 5 * d + h * hd:5 * d + (h + 1) * hd] = _nn(ds, kh).astype(BF)
            dkv_ref[:, h * hd:(h + 1) * hd] += _tn(ds, qh)
            dkv_ref[:, d + h * hd:d + (h + 1) * hd] += _tn(p.astype(BF), doh)

    row = lambda w_: pl.BlockSpec((tm, w_), lambda i: (i, 0))
    halo = lambda col: pl.BlockSpec((HALO, d), _prev_halo(tm, col))
    sq = lambda which: _resident((None, None, d, d), (l, which, 0, 0))
    acc = lambda r, c: pl.BlockSpec((r, c), lambda i: (0, 0))
    act = jax.ShapeDtypeStruct((t, d), BF)
    vec = lambda r, c: jax.ShapeDtypeStruct((r, c), F32)
    return _pallas_call(
        body, name=name, grid=(t // tm,),
        in_specs=[row(d), row(9 * d), halo(1), halo(2), halo(3), halo(4), row(d), row(d), row(d), row(d),
                  _resident((m, 2 * d), (0, 0)),
                  _resident((None, K_A, d), (l, 0, 0)), _resident((None, K_B, d), (l, 0, 0)),
                  _resident((1, d), (0, 0)), _resident((1, d), (0, 0)), _resident((1, 3 * d), (0, 0)),
                  sq(0), sq(1), sq(2), sq(3)],
        out_specs=[row(9 * d), row(d), row(d), row(d), acc(m, 2 * d), acc(1, 3 * d), acc(1, d), acc(1, d),
                   acc(1, d), acc(K_A, d), acc(K_B, d)],
        out_shape=[jax.ShapeDtypeStruct((t, 9 * d), BF), act, act, act, vec(m, 2 * d), vec(1, 3 * d), vec(1, d),
                   vec(1, d), vec(1, d), vec(K_A, d), vec(K_B, d)],
        scratch_shapes=[pltpu.VMEM((HALO + tm, d), F32), pltpu.VMEM((HALO + tm, d), F32)],
        compiler_params=_params(1),
    )(dx1b, proj, proj, proj, proj, proj, ya, yb, yc, cb, kv, conv_a, conv_b, ln_g, ln_b, b_gate,
      wsq, wsq, wsq, wsq)


def _inproj_conv_bwd(dpre, proj, conv_a, conv_b, l, name):
    t, d9 = dpre.shape
    d = d9 // 9
    tm = min(TM_ROW, t)
    n_t = t // tm

    def body(dpre_ref, nxa_ref, nxb_ref, proj_ref, ca_w, cb_w, dproj_ref, ya_ref, yb_ref):
        i = pl.program_id(0)
        keep = (i < n_t - 1).astype(F32)
        ya_ref[pl.ds(0, tm), :] = dpre_ref[:, d:2 * d].astype(F32)
        ya_ref[pl.ds(tm, HALO), :] = nxa_ref[...].astype(F32) * keep
        yb_ref[pl.ds(0, tm), :] = dpre_ref[:, 3 * d:4 * d].astype(F32)
        yb_ref[pl.ds(tm, HALO), :] = nxb_ref[...].astype(F32) * keep
        dproj_ref[:, 0:d] = dpre_ref[:, 0:d]
        dproj_ref[:, 5 * d:9 * d] = dpre_ref[:, 5 * d:9 * d]
        dcv = _anticausal_taps(ya_ref, ca_w, K_A, tm)
        dproj_ref[:, d:2 * d] = (dcv * proj_ref[:, 2 * d:3 * d].astype(F32)).astype(BF)
        dproj_ref[:, 2 * d:3 * d] = (dcv * proj_ref[:, d:2 * d].astype(F32)).astype(BF)
        dub = _anticausal_taps(yb_ref, cb_w, K_B, tm)
        sg = _sig(proj_ref[:, 4 * d:5 * d].astype(F32))
        dproj_ref[:, 3 * d:4 * d] = (dub * sg).astype(BF)
        dproj_ref[:, 4 * d:5 * d] = (dub * proj_ref[:, 3 * d:4 * d].astype(F32) * sg * (1.0 - sg)).astype(BF)

    return _pallas_call(
        body, name=name, grid=(n_t,),
        in_specs=[pl.BlockSpec((tm, d9), lambda i: (i, 0)),
                  pl.BlockSpec((HALO, d), _next_halo(tm, t, 1)),
                  pl.BlockSpec((HALO, d), _next_halo(tm, t, 3)),
                  pl.BlockSpec((tm, d9), lambda i: (i, 0)),
                  _resident((None, K_A, d), (l, 0, 0)), _resident((None, K_B, d), (l, 0, 0))],
        out_specs=pl.BlockSpec((tm, d9), lambda i: (i, 0)),
        out_shape=jax.ShapeDtypeStruct((t, d9), BF),
        scratch_shapes=[pltpu.VMEM((tm + HALO, d), F32), pltpu.VMEM((tm + HALO, d), F32)],
        compiler_params=_params(1),
    )(dpre, dpre, dpre, proj, conv_a, conv_b)


def _mem_kv_bwd(dkv, memn, mem, g, w_kv, l, name):
    m, d = mem.shape

    def body(dkv_ref, memn_ref, mem_ref, g_ref, w_ref, dw_ref, dg_ref):
        dkvb = dkv_ref[...].astype(BF)
        dw_ref[...] = _tn(memn_ref[...], dkvb)
        dmemn = _nt(dkvb, w_ref[...])
        xf = mem_ref[...]
        r = lax.rsqrt(jnp.mean(xf * xf, axis=-1, keepdims=True) + EPS)
        dg_ref[...] = jnp.sum(dmemn * (xf * r), axis=0, keepdims=True)

    return _pallas_call(
        body, name=name, grid=(1,),
        in_specs=[pl.BlockSpec((m, 2 * d), lambda i: (0, 0)), pl.BlockSpec((m, d), lambda i: (0, 0)),
                  pl.BlockSpec((m, d), lambda i: (0, 0)), pl.BlockSpec((1, d), lambda i: (0, 0)),
                  pl.BlockSpec((None, d, 2 * d), lambda i: (l, 0, 0))],
        out_specs=[pl.BlockSpec((d, 2 * d), lambda i: (0, 0)), pl.BlockSpec((1, d), lambda i: (0, 0))],
        out_shape=[jax.ShapeDtypeStruct((d, 2 * d), F32), jax.ShapeDtypeStruct((1, d), F32)],
        compiler_params=_params(1),
    )(dkv, memn, mem, g, w_kv)


def _dw_matmul(a, b, tn, name):
    t, k = a.shape
    n = b.shape[1]
    tt = min(TT_DW, t)

    def body(a_ref, b_ref, o_ref):
        @pl.when(pl.program_id(1) == 0)
        def _():
            o_ref[...] = jnp.zeros_like(o_ref)

        o_ref[...] += _tn(a_ref[...], b_ref[...])

    return _pallas_call(
        body, name=name, grid=(n // tn, t // tt),
        in_specs=[pl.BlockSpec((tt, k), lambda j, s: (s, 0)), pl.BlockSpec((tt, tn), lambda j, s: (s, j))],
        out_specs=pl.BlockSpec((k, tn), lambda j, s: (0, j)),
        out_shape=jax.ShapeDtypeStruct((k, n), F32),
        compiler_params=_params(2),
    )(a, b)


def _local_step(x, mem, target, big, conv, small):
    depth = big["w_in"].shape[0]
    d = x.shape[1]
    f2 = big["w_up"].shape[2]
    tn_in = min(1024, d)
    tn_up = f2 // 11 if f2 % (11 * 128) == 0 and f2 // 11 >= 128 else f2
    row = lambda v: v.reshape(1, -1)
    saved = []
    for l in range(depth):
        memn, kv = _mem_kv(mem, row(small["norm_mem_g"][l]), big["w_kv"], l, f"mem_kv_{l}")
        h, proj = _norm_matmul(x, row(small["norm_mix_g"][l]), big["w_in"], l, tn_in, f"in_proj_{l}")
        x1, za, zb, o, ya, yb, yc, mg, cb = _mixer_fwd(
            proj, x, kv, conv["a"], conv["b"], row(small["conv_b_bias"][l]), row(small["ln_b_g"][l]),
            row(small["ln_b_b"][l]), row(small["b_gate"][l]), big["wsq"], l, f"mixer_fwd_{l}")
        h2, up = _norm_matmul(x1, row(small["norm_ffn_g"][l]), big["w_up"], l, tn_up, f"up_proj_{l}")
        x2, zf = _ffn_down_fwd(up, x1, conv["f"], big["w_down"], l, f"ffn_down_fwd_{l}")
        saved.append(dict(x=x, x1=x1, memn=memn, kv=kv, h=h, proj=proj, za=za, zb=zb, o=o, ya=ya, yb=yb, yc=yc,
                          mg=mg, cb=cb, h2=h2, up=up, zf=zf))
        x = x2
    dx, loss, dg_final = _final_loss(x, row(small["norm_final_g"]), target, "final_loss")
    grads, sgrads = [None] * depth, [None] * depth
    for l in reversed(range(depth)):
        s = saved[l]
        duc, dx2b, dconv_f = _ffn_down_bwd(dx, s["up"], conv["f"], big["w_down"], l, f"ffn_down_bwd_{l}")
        dup = _ffn_conv_bwd(duc, conv["f"], l, f"ffn_conv_bwd_{l}")
        dx1, dx1b, dg_ffn = _nt_matmul_norm_bwd(dup, big["w_up"], l, s["x1"], row(small["norm_ffn_g"][l]), dx,
                                                tn_up, f"up_proj_bwd_{l}")
        (dpre, dya, dyb, dyc, dkv, dbg, dlng, dlnb, dcbias, dconv_a, dconv_b) = _mixer_bwd(
            dx1b, s["proj"], s["ya"], s["yb"], s["yc"], s["cb"], s["kv"], conv["a"], conv["b"],
            row(small["ln_b_g"][l]), row(small["ln_b_b"][l]), row(small["b_gate"][l]), big["wsq"], l,
            f"mixer_bwd_{l}")
        dproj = _inproj_conv_bwd(dpre, s["proj"], conv["a"], conv["b"], l, f"inproj_conv_bwd_{l}")
        dx0, _, dg_mix = _nt_matmul_norm_bwd(dproj, big["w_in"], l, s["x"], row(small["norm_mix_g"][l]), dx1,
                                             tn_in, f"in_proj_bwd_{l}")
        dw_kv, dg_mem = _mem_kv_bwd(dkv, s["memn"], mem, row(small["norm_mem_g"][l]), big["w_kv"], l,
                                    f"mem_kv_bwd_{l}")
        grads[l] = dict(
            w_in=_dw_matmul(s["h"], dproj, tn_in, f"dw_in_{l}"),
            w_a_out=_dw_matmul(s["za"], dya, d, f"dw_a_out_{l}"),
            w_b_out=_dw_matmul(s["zb"], dyb, d, f"dw_b_out_{l}"),
            w_att_out=_dw_matmul(s["o"], dyc, d, f"dw_att_out_{l}"),
            w_o=_dw_matmul(s["mg"], dx1b, d, f"dw_o_{l}"),
            w_kv=dw_kv,
            w_up=_dw_matmul(s["h2"], dup, tn_up, f"dw_up_{l}"),
            w_down=_dw_matmul(s["zf"], dx2b, d, f"dw_down_{l}"),
        )
        sgrads[l] = dict(norm_mix_g=dg_mix, norm_mem_g=dg_mem, b_gate=dbg, conv_b_bias=dcbias, ln_b_g=dlng,
                         ln_b_b=dlnb, norm_ffn_g=dg_ffn, conv_a_w=dconv_a, conv_b_w=dconv_b, conv_ffn_w=dconv_f)
        dx = dx0
    return loss, dx, grads, sgrads, dg_final


BIG = ("w_in", "w_a_out", "w_b_out", "w_att_out", "w_o", "w_kv", "w_up", "w_down")
COL = ("w_in", "w_kv", "w_up")
SQUARES = ("w_a_out", "w_b_out", "w_att_out", "w_o")
ANY = pl.BlockSpec(memory_space=pl.ANY)


def _place():
    x, y, c = lax.axis_index("x"), lax.axis_index("y"), lax.axis_index("c")
    chips = [(1 - x, y), (x, 1 - y), (1 - x, 1 - y)]
    return x, y, c, 2 * x + y, chips


def _remote(src, dst, send_sem, recv_sem, dev):
    return pltpu.make_async_remote_copy(src_ref=src, dst_ref=dst, send_sem=send_sem, recv_sem=recv_sem,
                                        device_id=dev, device_id_type=MESH)


def _window(ref, rows, cols):
    return ref.at[(slice(None),) * (len(ref.shape) - 2) + (rows, cols)]


def _row_tile(rows, cols, unit=16, limit=1 << 20):
    best = unit
    for tr in range(unit, rows + 1, unit):
        if rows % tr == 0 and tr * cols <= limit:
            best = tr
    return best


def _cast_shard(w, name):
    depth, k, n = w.shape

    def body(w_ref, o_ref):
        o_ref[...] = w_ref[...].astype(BF)

    return _pallas_call(
        body, name=name, grid=(depth,),
        in_specs=[pl.BlockSpec((None, k, n), lambda l: (l, 0, 0))],
        out_specs=pl.BlockSpec((None, k, n), lambda l: (l, 0, 0)),
        out_shape=jax.ShapeDtypeStruct((depth, k, n), BF),
        compiler_params=_params(1),
    )(w)


def _cast_squares(ws, name):
    depth, k, n = ws[0].shape

    def body(a_ref, b_ref, c_ref, d_ref, o_ref):
        for i, ref in enumerate((a_ref, b_ref, c_ref, d_ref)):
            o_ref[i] = ref[...].astype(BF)

    return _pallas_call(
        body, name=name, grid=(depth,),
        in_specs=[pl.BlockSpec((None, k, n), lambda l: (l, 0, 0))] * 4,
        out_specs=pl.BlockSpec((None, 4, k, n), lambda l: (l, 0, 0, 0)),
        out_shape=jax.ShapeDtypeStruct((depth, 4, k, n), BF),
        compiler_params=_params(1),
    )(*ws)


def _gather_weights(shards, kinds, name):
    n_t = len(shards)

    def full_shape(s, kind):
        shp = list(s.shape)
        shp[-2 if kind == "row" else -1] *= N_CHIPS
        return tuple(shp)

    def body(*refs):
        shard = refs[:n_t]
        full = refs[n_t:2 * n_t]
        ici_send, ici_recv, sib_send, sib_recv, loc_sem = refs[2 * n_t:]
        x, y, c, me, chips = _place()
        sibling = (x, y, 1 - c)

        def src_part(i, half):
            s, kind = shard[i], kinds[i]
            if kind == "small":
                return s
            r = s.shape[-2] // 2
            return _window(s, pl.ds(pl.multiple_of(half * r, 16), r), slice(None))

        def dst_part(i, chip, half):
            f, s, kind = full[i], shard[i], kinds[i]
            rows, cols = s.shape[-2], s.shape[-1]
            if kind == "small":
                return _window(f, slice(None), pl.ds(pl.multiple_of(chip * cols, 128), cols))
            r = rows // 2
            if kind == "col":
                return _window(f, pl.ds(pl.multiple_of(half * r, 16), r), pl.ds(pl.multiple_of(chip * cols, 128), cols))
            return _window(f, pl.ds(pl.multiple_of(chip * rows + half * r, 16), r), slice(None))

        def own_slot(i):
            f, s, kind = full[i], shard[i], kinds[i]
            rows, cols = s.shape[-2], s.shape[-1]
            if kind == "row":
                return _window(f, pl.ds(pl.multiple_of(me * rows, 16), rows), slice(None))
            return _window(f, slice(None), pl.ds(pl.multiple_of(me * cols, 128), cols))

        local = [pltpu.make_async_copy(shard[i], own_slot(i), loc_sem.at[i]) for i in range(n_t)]
        for cp in local:
            cp.start()
        sends = []
        for i in range(n_t):
            for j, chip in enumerate(chips):
                sends.append(_remote(src_part(i, c), dst_part(i, me, c), ici_send.at[3 * i + j],
                                     ici_recv.at[3 * i + j], (*chip, c)))
        for cp in sends:
            cp.start()
        passed = []
        for i in range(n_t):
            for j, chip in enumerate(chips):
                k = 2 * chip[0] + chip[1]
                landed = dst_part(i, k, c)
                _remote(landed, landed, ici_send.at[3 * i + j], ici_recv.at[3 * i + j], (*chip, c)).wait_recv()
                if kinds[i] != "small":
                    cp = _remote(landed, landed, sib_send.at[3 * i + j], sib_recv.at[3 * i + j], sibling)
                    cp.start()
                    passed.append(cp)
        for i in range(n_t):
            if kinds[i] == "small":
                continue
            for j, chip in enumerate(chips):
                k = 2 * chip[0] + chip[1]
                other = dst_part(i, k, 1 - c)
                _remote(other, other, sib_send.at[3 * i + j], sib_recv.at[3 * i + j], sibling).wait_recv()
        for cp in sends + passed:
            cp.wait_send()
        for cp in local:
            cp.wait()

    outs = [jax.ShapeDtypeStruct(full_shape(s, k), s.dtype) for s, k in zip(shards, kinds)]
    return _pallas_call(
        body, name=name, in_specs=[ANY] * n_t, out_specs=[ANY] * n_t, out_shape=outs,
        scratch_shapes=[pltpu.SemaphoreType.DMA((3 * n_t,)), pltpu.SemaphoreType.DMA((3 * n_t,)),
                        pltpu.SemaphoreType.DMA((3 * n_t,)), pltpu.SemaphoreType.DMA((3 * n_t,)),
                        pltpu.SemaphoreType.DMA((n_t,))],
    )(*shards)


def _all_reduce_small(part, name):
    r, n = part.shape

    def body(in_ref, out_ref, gather_ref, send_sems, recv_sems):
        x, y, c, _, _ = _place()
        me = 4 * x + 2 * y + c
        gather_ref[me] = in_ref[...]
        sends = []
        for k in range(1, N_DEV):
            peer = (me + k) % N_DEV
            sends.append(_remote(in_ref, gather_ref.at[me], send_sems.at[k - 1], recv_sems.at[k - 1],
                                 (peer // 4, (peer // 2) % 2, peer % 2)))
        for cp in sends:
            cp.start()
        for k in range(1, N_DEV):
            origin = (me + N_DEV - k) % N_DEV
            _remote(in_ref, gather_ref.at[origin], send_sems.at[k - 1], recv_sems.at[k - 1],
                    (x, y, c)).wait_recv()
        for cp in sends:
            cp.wait_send()
        total = gather_ref[0]
        for dev in range(1, N_DEV):
            total = total + gather_ref[dev]
        out_ref[...] = total

    vm = pl.BlockSpec(memory_space=pltpu.VMEM)
    return _pallas_call(
        body, name=name, in_specs=[vm], out_specs=vm, out_shape=jax.ShapeDtypeStruct((r, n), F32),
        scratch_shapes=[pltpu.VMEM((N_DEV, r, n), F32), pltpu.SemaphoreType.DMA((N_DEV - 1,)),
                        pltpu.SemaphoreType.DMA((N_DEV - 1,))],
        compiler_params=pltpu.CompilerParams(vmem_limit_bytes=VMEM_LIMIT),
    )(part)


def _halves_view(name, dw):
    k, n = dw.shape
    s = 1 if name in COL else N_CHIPS
    return dw.reshape(s, 2, k // (2 * s), n)


def _swap_halves(views, name):
    n_t = len(views)

    def body(*refs):
        src = refs[:n_t]
        dst = refs[n_t:2 * n_t]
        send_sems, recv_sems = refs[2 * n_t:]
        x, y, c, _, _ = _place()
        copies = [_remote(src[i].at[:, 1 - c], dst[i], send_sems.at[i], recv_sems.at[i], (x, y, 1 - c))
                  for i in range(n_t)]
        for cp in copies:
            cp.start()
        for cp in copies:
            cp.wait()

    outs = [jax.ShapeDtypeStruct((v.shape[0],) + v.shape[2:], F32) for v in views]
    return _pallas_call(
        body, name=name, in_specs=[ANY] * n_t, out_specs=[ANY] * n_t, out_shape=outs,
        scratch_shapes=[pltpu.SemaphoreType.DMA((n_t,)), pltpu.SemaphoreType.DMA((n_t,))],
    )(*views)


def _add_halves(view, got, core, name):
    s, _, r, n = view.shape
    tr = _row_tile(r, n)

    def body(c_ref, a_ref, b_ref, o_ref):
        o_ref[...] = (a_ref[...] + b_ref[...]).astype(BF)

    return _pallas_call(
        body, name=name,
        grid_spec=pltpu.PrefetchScalarGridSpec(
            num_scalar_prefetch=1, grid=(s, r // tr),
            in_specs=[pl.BlockSpec((None, None, tr, n), lambda i, j, c_ref: (i, c_ref[0], j, 0)),
                      pl.BlockSpec((None, tr, n), lambda i, j, c_ref: (i, j, 0))],
            out_specs=pl.BlockSpec((None, tr, n), lambda i, j, c_ref: (i, j, 0))),
        out_shape=jax.ShapeDtypeStruct((s, r, n), BF),
        compiler_params=_params(2),
    )(core, view, got)


def _scatter_pieces(sums, names, name):
    n_t = len(sums)

    def piece_shape(nm, p):
        s, r, n = p.shape
        return (r, n // N_CHIPS) if nm in COL else (r, n)

    def body(*refs):
        src = refs[:n_t]
        dst = refs[n_t:2 * n_t]
        ici_send, ici_recv, sib_send, sib_recv, loc_sem = refs[2 * n_t:]
        x, y, c, me, chips = _place()
        sibling = (x, y, 1 - c)

        def piece(i, chip):
            if names[i] in COL:
                cw = src[i].shape[2] // N_CHIPS
                return src[i].at[0, :, pl.ds(pl.multiple_of(chip * cw, 128), cw)]
            return src[i].at[chip]

        local = [pltpu.make_async_copy(piece(i, me), dst[i].at[c, me], loc_sem.at[i]) for i in range(n_t)]
        for cp in local:
            cp.start()
        sends = []
        for i in range(n_t):
            sends.append(_remote(piece(i, me), dst[i].at[c, me], sib_send.at[4 * i + 3], sib_recv.at[4 * i + 3],
                                 sibling))
            for j, chip in enumerate(chips):
                k = 2 * chip[0] + chip[1]
                sends.append(_remote(piece(i, k), dst[i].at[c, me], ici_send.at[3 * i + j], ici_recv.at[3 * i + j],
                                     (*chip, c)))
        for cp in sends:
            cp.start()
        passed = []
        for i in range(n_t):
            for j, chip in enumerate(chips):
                k = 2 * chip[0] + chip[1]
                landed = dst[i].at[c, k]
                _remote(landed, landed, ici_send.at[3 * i + j], ici_recv.at[3 * i + j], (*chip, c)).wait_recv()
                cp = _remote(landed, landed, sib_send.at[4 * i + j], sib_recv.at[4 * i + j], sibling)
                cp.start()
                passed.append(cp)
        for i in range(n_t):
            other = dst[i].at[1 - c, me]
            _remote(other, other, sib_send.at[4 * i + 3], sib_recv.at[4 * i + 3], sibling).wait_recv()
            for j, chip in enumerate(chips):
                k = 2 * chip[0] + chip[1]
                other = dst[i].at[1 - c, k]
                _remote(other, other, sib_send.at[4 * i + j], sib_recv.at[4 * i + j], sibling).wait_recv()
        for cp in sends + passed:
            cp.wait_send()
        for cp in local:
            cp.wait()

    outs = [jax.ShapeDtypeStruct((2, N_CHIPS) + piece_shape(nm, p), BF) for nm, p in zip(names, sums)]
    return _pallas_call(
        body, name=name, in_specs=[ANY] * n_t, out_specs=[ANY] * n_t, out_shape=outs,
        scratch_shapes=[pltpu.SemaphoreType.DMA((3 * n_t,)), pltpu.SemaphoreType.DMA((3 * n_t,)),
                        pltpu.SemaphoreType.DMA((4 * n_t,)), pltpu.SemaphoreType.DMA((4 * n_t,)),
                        pltpu.SemaphoreType.DMA((n_t,))],
    )(*sums)


def _adamw(w, g, m, v):
    m = ADAM_B1 * m + (1.0 - ADAM_B1) * g
    v = ADAM_B2 * v + (1.0 - ADAM_B2) * (g * g)
    m_hat = m / (1.0 - ADAM_B1 ** ADAM_STEP)
    v_hat = v / (1.0 - ADAM_B2 ** ADAM_STEP)
    return -ADAM_LR * (m_hat / (jnp.sqrt(v_hat) + ADAM_EPS) + ADAM_WD * w), m, v


def _adam_shard(pieces, w, m, v, l, prev, name):
    depth, rows, cw = w.shape
    hr = rows // 2
    tr = _row_tile(hr, cw, limit=1 << 18)
    n_i = hr // tr

    def body(*refs):
        z_ref, w_ref, m_ref, v_ref = refs[:4]
        g_ref, d_ref, nm_ref, nv_ref = refs[-4:]
        g = z_ref[0].astype(F32)
        for k in range(1, N_CHIPS):
            g = g + z_ref[k].astype(F32)
        g_ref[...] = g
        d_ref[...], nm_ref[...], nv_ref[...] = _adamw(w_ref[...], g, m_ref[...], v_ref[...])

    par = pl.BlockSpec((None, tr, cw), lambda h, i: (l, h * n_i + i, 0))
    out = jax.ShapeDtypeStruct((depth, rows, cw), F32)
    extra = [] if prev is None else list(prev)
    return _pallas_call(
        body, name=name, grid=(2, n_i),
        in_specs=[pl.BlockSpec((None, N_CHIPS, tr, cw), lambda h, i: (h, 0, i, 0)), par, par, par] + [ANY] * len(extra),
        out_specs=[par] * 4, out_shape=[out] * 4,
        input_output_aliases={4 + k: k for k in range(len(extra))},
        compiler_params=_params(2),
    )(pieces, w, m, v, *extra)


def _adam_small(g, w, m, v, name):
    def body(g_ref, w_ref, m_ref, v_ref, d_ref, nm_ref, nv_ref):
        d_ref[...], nm_ref[...], nv_ref[...] = _adamw(w_ref[...], g_ref[...], m_ref[...], v_ref[...])

    vm = pl.BlockSpec(memory_space=pltpu.VMEM)
    out = jax.ShapeDtypeStruct(g.shape, F32)
    return _pallas_call(body, name=name, in_specs=[vm] * 4, out_specs=[vm] * 3, out_shape=[out] * 3)(g, w, m, v)


WEIGHTS = ("norm_mix_g", "norm_mem_g", "w_in", "b_gate", "conv_a_w", "w_a_out", "conv_b_w", "conv_b_bias", "ln_b_g",
           "ln_b_b", "w_b_out", "w_kv", "w_att_out", "w_o", "norm_ffn_g", "w_up", "conv_ffn_w", "w_down",
           "norm_final_g")
REPLICATED = ("norm_mix_g", "norm_mem_g", "b_gate", "conv_b_bias", "ln_b_g", "ln_b_b", "norm_ffn_g")
CONVS = ("conv_a_w", "conv_b_w", "conv_ffn_w")
PACK_WIDTH = 1024


def _pack(arrays):
    flat = jnp.concatenate([a.reshape(-1) for a in arrays])
    size = -(-flat.shape[0] // (8 * PACK_WIDTH)) * (8 * PACK_WIDTH)
    return jnp.pad(flat, (0, size - flat.shape[0])).reshape(-1, PACK_WIDTH)


def _unpack(packed, shapes):
    flat = packed.reshape(-1)
    out, at = [], 0
    for shp in shapes:
        n = math.prod(shp)
        out.append(flat[at:at + n].reshape(shp))
        at += n
    return out


def kernel(x, mem, norm_mix_g, norm_mem_g, w_in, b_gate, conv_a_w, w_a_out, conv_b_w, conv_b_bias, ln_b_g, ln_b_b, w_b_out, w_kv, w_att_out, w_o, norm_ffn_g, w_up, conv_ffn_w, w_down, norm_final_g, loss_target, m_norm_mix_g, m_norm_mem_g, m_w_in, m_b_gate, m_conv_a_w, m_w_a_out, m_conv_b_w, m_conv_b_bias, m_ln_b_g, m_ln_b_b, m_w_b_out, m_w_kv, m_w_att_out, m_w_o, m_norm_ffn_g, m_w_up, m_conv_ffn_w, m_w_down, m_norm_final_g, v_norm_mix_g, v_norm_mem_g, v_w_in, v_b_gate, v_conv_a_w, v_w_a_out, v_conv_b_w, v_conv_b_bias, v_ln_b_g, v_ln_b_b, v_w_b_out, v_w_kv, v_w_att_out, v_w_o, v_norm_ffn_g, v_w_up, v_conv_ffn_w, v_w_down, v_norm_final_g):
    w = dict(norm_mix_g=norm_mix_g, norm_mem_g=norm_mem_g, w_in=w_in, b_gate=b_gate, conv_a_w=conv_a_w,
             w_a_out=w_a_out, conv_b_w=conv_b_w, conv_b_bias=conv_b_bias, ln_b_g=ln_b_g, ln_b_b=ln_b_b,
             w_b_out=w_b_out, w_kv=w_kv, w_att_out=w_att_out, w_o=w_o, norm_ffn_g=norm_ffn_g, w_up=w_up,
             conv_ffn_w=conv_ffn_w, w_down=w_down, norm_final_g=norm_final_g)
    mom = dict(norm_mix_g=m_norm_mix_g, norm_mem_g=m_norm_mem_g, w_in=m_w_in, b_gate=m_b_gate, conv_a_w=m_conv_a_w,
               w_a_out=m_w_a_out, conv_b_w=m_conv_b_w, conv_b_bias=m_conv_b_bias, ln_b_g=m_ln_b_g, ln_b_b=m_ln_b_b,
               w_b_out=m_w_b_out, w_kv=m_w_kv, w_att_out=m_w_att_out, w_o=m_w_o, norm_ffn_g=m_norm_ffn_g,
               w_up=m_w_up, conv_ffn_w=m_conv_ffn_w, w_down=m_w_down, norm_final_g=m_norm_final_g)
    var = dict(norm_mix_g=v_norm_mix_g, norm_mem_g=v_norm_mem_g, w_in=v_w_in, b_gate=v_b_gate, conv_a_w=v_conv_a_w,
               w_a_out=v_w_a_out, conv_b_w=v_conv_b_w, conv_b_bias=v_conv_b_bias, ln_b_g=v_ln_b_g, ln_b_b=v_ln_b_b,
               w_b_out=v_w_b_out, w_kv=v_w_kv, w_att_out=v_w_att_out, w_o=v_w_o, norm_ffn_g=v_norm_ffn_g,
               w_up=v_w_up, conv_ffn_w=v_conv_ffn_w, w_down=v_w_down, norm_final_g=v_norm_final_g)
    depth = w_in.shape[0]
    chip = 2 * lax.axis_index("x") + lax.axis_index("y")
    core = lax.axis_index("c").astype(jnp.int32).reshape(1)

    shards = [_cast_shard(w_in, "cast_w_in"), _cast_squares([w[n] for n in SQUARES], "cast_squares"),
              _cast_shard(w_kv, "cast_w_kv"), _cast_shard(w_up, "cast_w_up"), _cast_shard(w_down, "cast_w_down"),
              jnp.concatenate([conv_a_w, conv_b_w], axis=1), conv_ffn_w]
    g_in, g_sq, g_kv, g_up, g_down, g_cab, g_cf = _gather_weights(
        shards, ("col", "row", "col", "col", "row", "small", "small"), "gather_weights")
    big = dict(w_in=g_in, wsq=g_sq, w_kv=g_kv, w_up=g_up, w_down=g_down)
    conv = dict(a=g_cab[:, :K_A], b=g_cab[:, K_A:], f=g_cf)
    small = {n: w[n] for n in REPLICATED + ("norm_final_g",)}

    loss, dx, grads, sgrads, dg_final = _local_step(x[0], mem[0], loss_target[0], big, conv, small)

    res = {n: None for n in BIG}
    for l in reversed(range(depth)):
        views = [_halves_view(n, grads[l][n]) for n in BIG]
        got = _swap_halves(views, f"swap_halves_{l}")
        sums = [_add_halves(v, y, core, f"add_halves_{n}_{l}") for n, v, y in zip(BIG, views, got)]
        pieces = _scatter_pieces(sums, BIG, f"scatter_pieces_{l}")
        for n, z in zip(BIG, pieces):
            res[n] = _adam_shard(z, w[n], mom[n], var[n], l, res[n], f"adam_{n}_{l}")

    per_layer = REPLICATED + CONVS
    parts = [sgrads[l][n] for l in range(depth) for n in per_layer] + [dg_final]
    total = _all_reduce_small(_pack(parts), "all_reduce_small")
    shapes = [sgrads[l][n].shape for l in range(depth) for n in per_layer] + [dg_final.shape]
    summed = _unpack(total, shapes)
    g_small = {}
    for k, n in enumerate(per_layer):
        full = jnp.stack([summed[l * len(per_layer) + k] for l in range(depth)])
        if n in CONVS:
            cols = w[n].shape[-1]
            full = lax.dynamic_slice_in_dim(full, chip * cols, cols, axis=2)
        g_small[n] = full.reshape(w[n].shape)
    g_small["norm_final_g"] = summed[-1].reshape(norm_final_g.shape)
    names = per_layer + ("norm_final_g",)
    d_p, m_p, v_p = _adam_small(_pack([g_small[n] for n in names]), _pack([w[n] for n in names]),
                                _pack([mom[n] for n in names]), _pack([var[n] for n in names]), "adam_small")
    shp = [w[n].shape for n in names]
    for n, dl, nm, nv in zip(names, _unpack(d_p, shp), _unpack(m_p, shp), _unpack(v_p, shp)):
        res[n] = (g_small[n], dl, nm, nv)

    loss = lax.psum(loss[0, 0], ("x", "y", "c"))
    return (loss, dx.reshape(x.shape), *[res[n][0] for n in WEIGHTS], *[res[n][1] for n in WEIGHTS],
            *[res[n][2] for n in WEIGHTS], *[res[n][3] for n in WEIGHTS])
```

```python
import functools
import math

import jax
import jax.numpy as jnp
from jax import lax
from jax.experimental import pallas as pl
from jax.experimental.pallas import tpu as pltpu

F32 = jnp.float32
BF = jnp.bfloat16
EPS = 1e-6
N_HEADS = 4
K_A, K_B, K_F = 3, 31, 3
ADAM_LR, ADAM_B1, ADAM_B2, ADAM_EPS, ADAM_WD, ADAM_STEP = 0.001, 0.9, 0.999, 1e-08, 0.01, 10
N_CHIPS = 4
N_DEV = 8
HALO = 32
TM_ROW = 256
TM_MM = 1024
TT_DW = 512
TR_EW = 128
VMEM_LIMIT = 56 * 1024 * 1024
MESH = pl.DeviceIdType.MESH
_pallas_call = pl.pallas_call


def _params(n_axes):
    return pltpu.CompilerParams(dimension_semantics=("arbitrary",) * n_axes, vmem_limit_bytes=VMEM_LIMIT)


def _resident(shape, index):
    return pl.BlockSpec(shape, lambda *_: index, pipeline_mode=pl.Buffered(1))


def _sig(x):
    return 1.0 / (1.0 + jnp.exp(-x))


def _nt(a, b):
    return lax.dot_general(a, b, (((1,), (1,)), ((), ())), preferred_element_type=F32)


def _tn(a, b):
    return lax.dot_general(a, b, (((0,), (0,)), ((), ())), preferred_element_type=F32)


def _nn(a, b):
    return jnp.dot(a, b, preferred_element_type=F32)


class _Comm:
    def __init__(self, inputs, out_shape, scratch, start, finish):
        self.inputs, self.out_shape, self.scratch = list(inputs), list(out_shape), list(scratch)
        self.start, self.finish = start, finish


def _join(programs):
    programs = [p for p in programs if p is not None]
    if not programs:
        return None

    def split(seq, counts):
        parts, at = [], 0
        for n in counts:
            parts.append(seq[at:at + n])
            at += n
        return parts

    n_in = [len(p.inputs) for p in programs]
    n_out = [len(p.out_shape) for p in programs]
    n_s = [len(p.scratch) for p in programs]

    def phase(which):
        def run(ins, outs, sems):
            for p, i, o, s in zip(programs, split(ins, n_in), split(outs, n_out), split(sems, n_s)):
                getattr(p, which)(i, o, s)
        return run

    return _Comm([a for p in programs for a in p.inputs], [a for p in programs for a in p.out_shape],
                 [a for p in programs for a in p.scratch], phase("start"), phase("finish"))


def _run(body, comm, *, name, grid, in_specs, out_specs, out_shape, args, scratch_shapes=()):
    n_axes = len(grid)
    if comm is None:
        outs = _pallas_call(body, name=name, grid=grid, in_specs=list(in_specs), out_specs=list(out_specs),
                            out_shape=list(out_shape), scratch_shapes=list(scratch_shapes),
                            compiler_params=_params(n_axes))(*args)
        return list(outs), []
    counts = (len(in_specs), len(comm.inputs), len(out_specs), len(comm.out_shape), len(scratch_shapes),
              len(comm.scratch))

    def hosted(*refs):
        parts, at = [], 0
        for n in counts:
            parts.append(refs[at:at + n])
            at += n
        ins, c_ins, outs, c_outs, scr, c_sems = parts
        ids = [pl.program_id(a) for a in range(n_axes)]
        first = functools.reduce(jnp.logical_and, [i == 0 for i in ids])
        last = functools.reduce(jnp.logical_and, [i == g - 1 for i, g in zip(ids, grid)])

        @pl.when(first)
        def _():
            comm.start(c_ins, c_outs, c_sems)

        body(*ins, *outs, *scr)

        @pl.when(last)
        def _():
            comm.finish(c_ins, c_outs, c_sems)

    any_spec = pl.BlockSpec(memory_space=pl.ANY)
    res = _pallas_call(
        hosted, name=name, grid=grid, in_specs=list(in_specs) + [any_spec] * counts[1],
        out_specs=list(out_specs) + [any_spec] * counts[3], out_shape=list(out_shape) + comm.out_shape,
        scratch_shapes=list(scratch_shapes) + comm.scratch, compiler_params=_params(n_axes),
    )(*args, *comm.inputs)
    return list(res[:counts[2]]), list(res[counts[2]:])


def _run_comm(comm, name):
    n_in, n_out = len(comm.inputs), len(comm.out_shape)

    def body(*refs):
        ins, outs, sems = refs[:n_in], refs[n_in:n_in + n_out], refs[n_in + n_out:]
        comm.start(ins, outs, sems)
        comm.finish(ins, outs, sems)

    any_spec = pl.BlockSpec(memory_space=pl.ANY)
    return list(_pallas_call(body, name=name, in_specs=[any_spec] * n_in, out_specs=[any_spec] * n_out,
                             out_shape=comm.out_shape, scratch_shapes=comm.scratch)(*comm.inputs))


def _causal_taps(xx_ref, w_ref, k_taps, tm):
    base = HALO - k_taps + 1
    acc = w_ref[0:1, :] * xx_ref[pl.ds(base, tm), :]
    for k in range(1, k_taps):
        acc = acc + w_ref[k:k + 1, :] * xx_ref[pl.ds(base + k, tm), :]
    return acc


def _anticausal_taps(yy_ref, w_ref, k_taps, tm):
    acc = w_ref[k_taps - 1:k_taps, :] * yy_ref[pl.ds(0, tm), :]
    for k in range(k_taps - 1):
        acc = acc + w_ref[k:k + 1, :] * yy_ref[pl.ds(k_taps - 1 - k, tm), :]
    return acc


def _tap_grads(dw_ref, dy, xx_ref, k_taps, tm):
    base = HALO - k_taps + 1
    for k in range(k_taps):
        dw_ref[k:k + 1, :] += jnp.sum(dy * xx_ref[pl.ds(base + k, tm), :], axis=0, keepdims=True)


def _prev_halo(tm, col):
    return lambda i: (jnp.maximum(i * (tm // HALO) - 1, 0), col)


def _next_halo(tm, n_rows, col):
    return lambda i: (jnp.minimum((i + 1) * (tm // HALO), n_rows // HALO - 1), col)


def _norm_matmul(x, g, w, l, tn, name, comm=None):
    t, d = x.shape
    n = w.shape[2]
    tm = min(TM_MM, t)

    def body(x_ref, g_ref, w_ref, h_ref, y_ref):
        @pl.when(pl.program_id(1) == 0)
        def _():
            xf = x_ref[...]
            r = lax.rsqrt(jnp.mean(xf * xf, axis=-1, keepdims=True) + EPS)
            h_ref[...] = ((xf * r) * g_ref[...]).astype(BF)

        y_ref[...] = _nn(h_ref[...], w_ref[...]).astype(BF)

    return _run(
        body, comm, name=name, grid=(t // tm, n // tn),
        in_specs=[pl.BlockSpec((tm, d), lambda i, j: (i, 0)),
                  pl.BlockSpec((1, d), lambda i, j: (0, 0)),
                  pl.BlockSpec((None, d, tn), lambda i, j: (l, 0, j))],
        out_specs=[pl.BlockSpec((tm, d), lambda i, j: (i, 0)),
                   pl.BlockSpec((tm, tn), lambda i, j: (i, j))],
        out_shape=[jax.ShapeDtypeStruct((t, d), BF), jax.ShapeDtypeStruct((t, n), BF)],
        args=(x, g, w))


def _mem_kv(mem, g, w_kv, l, name):
    m, d = mem.shape

    def body(mem_ref, g_ref, w_ref, memn_ref, kv_ref):
        xf = mem_ref[...]
        r = lax.rsqrt(jnp.mean(xf * xf, axis=-1, keepdims=True) + EPS)
        memn = ((xf * r) * g_ref[...]).astype(BF)
        memn_ref[...] = memn
        kv_ref[...] = _nn(memn, w_ref[...]).astype(BF)

    return _pallas_call(
        body, name=name, grid=(1,),
        in_specs=[pl.BlockSpec((m, d), lambda i: (0, 0)),
                  pl.BlockSpec((1, d), lambda i: (0, 0)),
                  pl.BlockSpec((None, d, 2 * d), lambda i: (l, 0, 0))],
        out_specs=[pl.BlockSpec((m, d), lambda i: (0, 0)),
                   pl.BlockSpec((m, 2 * d), lambda i: (0, 0))],
        out_shape=[jax.ShapeDtypeStruct((m, d), BF), jax.ShapeDtypeStruct((m, 2 * d), BF)],
        compiler_params=_params(1),
    )(mem, g, w_kv)


def _load_branch_inputs(i, proj_ref, gch_ref, vh_ref, u0h_ref, ugh_ref, xa_ref, xb_ref, d):
    gc = proj_ref[:, d:2 * d].astype(F32)
    v = proj_ref[:, 2 * d:3 * d].astype(F32)
    u0 = proj_ref[:, 3 * d:4 * d].astype(F32)
    ug = proj_ref[:, 4 * d:5 * d].astype(F32)
    keep = (i > 0).astype(F32)
    xa_ref[pl.ds(0, HALO), :] = gch_ref[...].astype(F32) * vh_ref[...].astype(F32) * keep
    xa_ref[pl.ds(HALO, gc.shape[0]), :] = gc * v
    xb_ref[pl.ds(0, HALO), :] = u0h_ref[...].astype(F32) * _sig(ugh_ref[...].astype(F32)) * keep
    xb_ref[pl.ds(HALO, gc.shape[0]), :] = u0 * _sig(ug)


def _softmax_rows(s):
    e = jnp.exp(s - jnp.max(s, axis=-1, keepdims=True))
    return e / jnp.sum(e, axis=-1, keepdims=True)


def _mixer_fwd(proj, x, kv, conv_a, conv_b, cbias, ln_g, ln_b, b_gate, wsq, l, name, comm=None):
    t, d = x.shape
    m = kv.shape[0]
    tm = min(TM_ROW, t)
    hd = d // N_HEADS
    scale = 1.0 / math.sqrt(hd)

    def body(proj_ref, gch_ref, vh_ref, u0h_ref, ugh_ref, x_ref, kv_ref, ca_w, cb_w, cbias_ref, lng_ref, lnb_ref,
             bg_ref, wa_ref, wb_ref, wc_ref, wo_ref,
             x1_ref, za_ref, zb_ref, o_ref, ya_ref, yb_ref, yc_ref, mg_ref, cb_ref, xa_ref, xb_ref):
        i = pl.program_id(0)
        _load_branch_inputs(i, proj_ref, gch_ref, vh_ref, u0h_ref, ugh_ref, xa_ref, xb_ref, d)
        gb = proj_ref[:, 0:d].astype(F32)
        za = (gb * _causal_taps(xa_ref, ca_w, K_A, tm)).astype(BF)
        za_ref[...] = za
        ya = _nn(za, wa_ref[...])
        ya_ref[...] = ya.astype(BF)
        cb = _causal_taps(xb_ref, cb_w, K_B, tm) + cbias_ref[...]
        cb_ref[...] = cb
        mu = jnp.mean(cb, axis=-1, keepdims=True)
        dlt = cb - mu
        rstd = lax.rsqrt(jnp.mean(dlt * dlt, axis=-1, keepdims=True) + EPS)
        lnb = (dlt * rstd) * lng_ref[...] + lnb_ref[...]
        zb = (lnb * _sig(lnb)).astype(BF)
        zb_ref[...] = zb
        yb = _nn(zb, wb_ref[...])
        yb_ref[...] = yb.astype(BF)
        for h in range(N_HEADS):
            qh = proj_ref[:, 5 * d + h * hd:5 * d + (h + 1) * hd]
            kh = kv_ref[:, h * hd:(h + 1) * hd]
            vh = kv_ref[:, d + h * hd:d + (h + 1) * hd]
            p = _softmax_rows(_nt(qh, kh) * scale)
            o_ref[:, h * hd:(h + 1) * hd] = _nn(p.astype(BF), vh).astype(BF)
        yc = _nn(o_ref[...], wc_ref[...])
        yc_ref[...] = yc.astype(BF)
        g0 = _sig(proj_ref[:, 6 * d:7 * d].astype(F32) + bg_ref[:, 0:d])
        g1 = _sig(proj_ref[:, 7 * d:8 * d].astype(F32) + bg_ref[:, d:2 * d])
        g2 = _sig(proj_ref[:, 8 * d:9 * d].astype(F32) + bg_ref[:, 2 * d:3 * d])
        mg = (g0 * ya + g1 * yb + g2 * yc).astype(BF)
        mg_ref[...] = mg
        x1_ref[...] = x_ref[...] + _nn(mg, wo_ref[...])

    row = lambda w_: pl.BlockSpec((tm, w_), lambda i: (i, 0))
    halo = lambda col: pl.BlockSpec((HALO, d), _prev_halo(tm, col))
    sq = lambda which: _resident((None, None, d, d), (l, which, 0, 0))
    act = jax.ShapeDtypeStruct((t, d), BF)
    return _run(
        body, comm, name=name, grid=(t // tm,),
        in_specs=[row(9 * d), halo(1), halo(2), halo(3), halo(4), row(d),
                  _resident((m, 2 * d), (0, 0)),
                  _resident((None, K_A, d), (l, 0, 0)), _resident((None, K_B, d), (l, 0, 0)),
                  _resident((1, d), (0, 0)), _resident((1, d), (0, 0)), _resident((1, d), (0, 0)),
                  _resident((1, 3 * d), (0, 0)), sq(0), sq(1), sq(2), sq(3)],
        out_specs=[row(d)] * 9,
        out_shape=[jax.ShapeDtypeStruct((t, d), F32)] + [act] * 7 + [jax.ShapeDtypeStruct((t, d), F32)],
        scratch_shapes=[pltpu.VMEM((HALO + tm, d), F32), pltpu.VMEM((HALO + tm, d), F32)],
        args=(proj, proj, proj, proj, proj, x, kv, conv_a, conv_b, cbias, ln_g, ln_b, b_gate, wsq, wsq, wsq, wsq))


def _ffn_down_fwd(up, x1, conv_f, w_down, l, name, comm=None):
    t, d = x1.shape
    f2 = up.shape[1]
    f = f2 // 2
    tm = min(TM_ROW, t)

    def body(up_ref, uph_ref, x1_ref, cw_ref, wd_ref, x2_ref, zf_ref, xx_ref):
        i = pl.program_id(0)
        xx_ref[pl.ds(0, HALO), :] = uph_ref[...].astype(F32) * (i > 0).astype(F32)
        xx_ref[pl.ds(HALO, tm), :] = up_ref[...].astype(F32)
        uc = _causal_taps(xx_ref, cw_ref, K_F, tm)
        gt = uc[:, 0:f]
        zf = (gt * _sig(gt) * uc[:, f:f2]).astype(BF)
        zf_ref[...] = zf
        x2_ref[...] = x1_ref[...] + _nn(zf, wd_ref[...])

    return _run(
        body, comm, name=name, grid=(t // tm,),
        in_specs=[pl.BlockSpec((tm, f2), lambda i: (i, 0)),
                  pl.BlockSpec((HALO, f2), _prev_halo(tm, 0)),
                  pl.BlockSpec((tm, d), lambda i: (i, 0)),
                  _resident((None, K_F, f2), (l, 0, 0)),
                  _resident((None, f, d), (l, 0, 0))],
        out_specs=[pl.BlockSpec((tm, d), lambda i: (i, 0)), pl.BlockSpec((tm, f), lambda i: (i, 0))],
        out_shape=[jax.ShapeDtypeStruct((t, d), F32), jax.ShapeDtypeStruct((t, f), BF)],
        scratch_shapes=[pltpu.VMEM((HALO + tm, f2), F32)],
        args=(up, up, x1, conv_f, w_down))


def _final_loss(x, g, target, name):
    t, d = x.shape
    tm = min(2 * TM_ROW, t)

    def body(x_ref, g_ref, t_ref, dx_ref, loss_ref, dg_ref):
        @pl.when(pl.program_id(0) == 0)
        def _():
            loss_ref[...] = jnp.zeros_like(loss_ref)
            dg_ref[...] = jnp.zeros_like(dg_ref)

        xf = x_ref[...]
        r = lax.rsqrt(jnp.mean(xf * xf, axis=-1, keepdims=True) + EPS)
        xhat = xf * r
        err = xhat * g_ref[...] - t_ref[...]
        loss_ref[...] += (0.5 / d) * jnp.sum(err * err)
        dy = err * (1.0 / d)
        dg_ref[...] += jnp.sum(dy * xhat, axis=0, keepdims=True)
        dxh = dy * g_ref[...]
        dx_ref[...] = r * (dxh - xhat * jnp.mean(dxh * xhat, axis=-1, keepdims=True))

    return _pallas_call(
        body, name=name, grid=(t // tm,),
        in_specs=[pl.BlockSpec((tm, d), lambda i: (i, 0)), pl.BlockSpec((1, d), lambda i: (0, 0)),
                  pl.BlockSpec((tm, d), lambda i: (i, 0))],
        out_specs=[pl.BlockSpec((tm, d), lambda i: (i, 0)), pl.BlockSpec((8, 128), lambda i: (0, 0)),
                   pl.BlockSpec((1, d), lambda i: (0, 0))],
        out_shape=[jax.ShapeDtypeStruct((t, d), F32), jax.ShapeDtypeStruct((8, 128), F32),
                   jax.ShapeDtypeStruct((1, d), F32)],
        compiler_params=_params(1),
    )(x, g, target)


def _ffn_down_bwd(dx2, up, conv_f, w_down, l, name, comm=None):
    t, d = dx2.shape
    f2 = up.shape[1]
    f = f2 // 2
    tm = min(TM_ROW, t)

    def body(dx2_ref, up_ref, uph_ref, cw_ref, wd_ref, duc_ref, dx2b_ref, dcw_ref, xx_ref):
        i = pl.program_id(0)

        @pl.when(i == 0)
        def _():
            dcw_ref[...] = jnp.zeros_like(dcw_ref)

        xx_ref[pl.ds(0, HALO), :] = uph_ref[...].astype(F32) * (i > 0).astype(F32)
        xx_ref[pl.ds(HALO, tm), :] = up_ref[...].astype(F32)
        uc = _causal_taps(xx_ref, cw_ref, K_F, tm)
        gt = uc[:, 0:f]
        sg = _sig(gt)
        dx2b = dx2_ref[...].astype(BF)
        dx2b_ref[...] = dx2b
        dzf = _nt(dx2b, wd_ref[...])
        duc_ref[:, 0:f] = (dzf * uc[:, f:f2] * (sg * (1.0 + gt * (1.0 - sg)))).astype(BF)
        duc_ref[:, f:f2] = (dzf * (gt * sg)).astype(BF)
        _tap_grads(dcw_ref, duc_ref[...].astype(F32), xx_ref, K_F, tm)

    return _run(
        body, comm, name=name, grid=(t // tm,),
        in_specs=[pl.BlockSpec((tm, d), lambda i: (i, 0)),
                  pl.BlockSpec((tm, f2), lambda i: (i, 0)),
                  pl.BlockSpec((HALO, f2), _prev_halo(tm, 0)),
                  _resident((None, K_F, f2), (l, 0, 0)),
                  _resident((None, f, d), (l, 0, 0))],
        out_specs=[pl.BlockSpec((tm, f2), lambda i: (i, 0)), pl.BlockSpec((tm, d), lambda i: (i, 0)),
                   pl.BlockSpec((K_F, f2), lambda i: (0, 0))],
        out_shape=[jax.ShapeDtypeStruct((t, f2), BF), jax.ShapeDtypeStruct((t, d), BF),
                   jax.ShapeDtypeStruct((K_F, f2), F32)],
        scratch_shapes=[pltpu.VMEM((HALO + tm, f2), F32)],
        args=(dx2, up, up, conv_f, w_down))


def _ffn_conv_bwd(duc, conv_f, l, name, comm=None):
    t, f2 = duc.shape
    tm = min(TM_ROW, t)
    n_t = t // tm

    def body(duc_ref, nxt_ref, cw_ref, dup_ref, yy_ref):
        i = pl.program_id(0)
        yy_ref[pl.ds(0, tm), :] = duc_ref[...].astype(F32)
        yy_ref[pl.ds(tm, HALO), :] = nxt_ref[...].astype(F32) * (i < n_t - 1).astype(F32)
        dup_ref[...] = _anticausal_taps(yy_ref, cw_ref, K_F, tm).astype(BF)

    return _run(
        body, comm, name=name, grid=(n_t,),
        in_specs=[pl.BlockSpec((tm, f2), lambda i: (i, 0)),
                  pl.BlockSpec((HALO, f2), _next_halo(tm, t, 0)),
                  _resident((None, K_F, f2), (l, 0, 0))],
        out_specs=[pl.BlockSpec((tm, f2), lambda i: (i, 0))],
        out_shape=[jax.ShapeDtypeStruct((t, f2), BF)],
        scratch_shapes=[pltpu.VMEM((tm + HALO, f2), F32)],
        args=(duc, duc, conv_f))


def _nt_matmul_norm_bwd(dy, w, l, x, g, dres, tk, name, comm=None):
    t, n = dy.shape
    d = x.shape[1]
    tm = min(TM_MM // 2, t)
    n_k = n // tk

    def body(dy_ref, w_ref, x_ref, g_ref, dres_ref, dx_ref, dxb_ref, dg_ref, acc_ref):
        i, k = pl.program_id(0), pl.program_id(1)

        @pl.when((i == 0) & (k == 0))
        def _():
            dg_ref[...] = jnp.zeros_like(dg_ref)

        @pl.when(k == 0)
        def _():
            acc_ref[...] = jnp.zeros_like(acc_ref)

        acc_ref[...] += _nt(dy_ref[...], w_ref[...])

        @pl.when(k == n_k - 1)
        def _():
            xf = x_ref[...]
            r = lax.rsqrt(jnp.mean(xf * xf, axis=-1, keepdims=True) + EPS)
            xhat = xf * r
            dh = acc_ref[...]
            dg_ref[...] += jnp.sum(dh * xhat, axis=0, keepdims=True)
            dxh = dh * g_ref[...]
            dx = dres_ref[...] + r * (dxh - xhat * jnp.mean(dxh * xhat, axis=-1, keepdims=True))
            dx_ref[...] = dx
            dxb_ref[...] = dx.astype(BF)

    return _run(
        body, comm, name=name, grid=(t // tm, n_k),
        in_specs=[pl.BlockSpec((tm, tk), lambda i, k: (i, k)),
                  pl.BlockSpec((None, d, tk), lambda i, k: (l, 0, k)),
                  pl.BlockSpec((tm, d), lambda i, k: (i, 0)),
                  pl.BlockSpec((1, d), lambda i, k: (0, 0)),
                  pl.BlockSpec((tm, d), lambda i, k: (i, 0))],
        out_specs=[pl.BlockSpec((tm, d), lambda i, k: (i, 0)), pl.BlockSpec((tm, d), lambda i, k: (i, 0)),
                   pl.BlockSpec((1, d), lambda i, k: (0, 0))],
        out_shape=[jax.ShapeDtypeStruct((t, d), F32), jax.ShapeDtypeStruct((t, d), BF),
                   jax.ShapeDtypeStruct((1, d), F32)],
        scratch_shapes=[pltpu.VMEM((tm, d), F32)],
        args=(dy, w, x, g, dres))


def _mixer_bwd(dx1b, proj, ya, yb, yc, cb, kv, conv_a, conv_b, ln_g, ln_b, b_gate, wsq, l, name, comm=None):
    t, d = cb.shape
    m = kv.shape[0]
    tm = min(TM_ROW, t)
    hd = d // N_HEADS
    scale = 1.0 / math.sqrt(hd)

    def body(dx1b_ref, proj_ref, gch_ref, vh_ref, u0h_ref, ugh_ref, ya_ref, yb_ref, yc_ref, cb_ref, kv_ref,
             ca_w, cb_w, lng_ref, lnb_ref, bg_ref, wa_ref, wb_ref, wc_ref, wo_ref,
             dpre_ref, dya_ref, dyb_ref, dyc_ref, dkv_ref, dbg_ref, dlng_ref, dlnb_ref, dcbias_ref, dcaw_ref,
             dcbw_ref, xa_ref, xb_ref):
        i = pl.program_id(0)

        @pl.when(i == 0)
        def _():
            for ref in (dkv_ref, dbg_ref, dlng_ref, dlnb_ref, dcbias_ref, dcaw_ref, dcbw_ref):
                ref[...] = jnp.zeros_like(ref)

        _load_branch_inputs(i, proj_ref, gch_ref, vh_ref, u0h_ref, ugh_ref, xa_ref, xb_ref, d)
        dmg = _nt(dx1b_ref[...], wo_ref[...])
        ys = (ya_ref, yb_ref, yc_ref)
        dys = (dya_ref, dyb_ref, dyc_ref)
        for b in range(3):
            gate = _sig(proj_ref[:, (6 + b) * d:(7 + b) * d].astype(F32) + bg_ref[:, b * d:(b + 1) * d])
            dys[b][...] = (gate * dmg).astype(BF)
            dpg = dmg * ys[b][...].astype(F32) * gate * (1.0 - gate)
            dpre_ref[:, (6 + b) * d:(7 + b) * d] = dpg.astype(BF)
            dbg_ref[:, b * d:(b + 1) * d] += jnp.sum(dpg, axis=0, keepdims=True)
        gb = proj_ref[:, 0:d].astype(F32)
        ca = _causal_taps(xa_ref, ca_w, K_A, tm)
        dza = _nt(dya_ref[...], wa_ref[...])
        dpre_ref[:, 0:d] = (dza * ca).astype(BF)
        dpre_ref[:, d:2 * d] = (dza * gb).astype(BF)
        _tap_grads(dcaw_ref, dpre_ref[:, d:2 * d].astype(F32), xa_ref, K_A, tm)
        dpre_ref[:, 2 * d:3 * d] = jnp.zeros((tm, d), BF)
        cbv = cb_ref[...]
        mu = jnp.mean(cbv, axis=-1, keepdims=True)
        dlt = cbv - mu
        rstd = lax.rsqrt(jnp.mean(dlt * dlt, axis=-1, keepdims=True) + EPS)
        xhat = dlt * rstd
        lnb = xhat * lng_ref[...] + lnb_ref[...]
        sg = _sig(lnb)
        dzb = _nt(dyb_ref[...], wb_ref[...])
        dl = dzb * (sg * (1.0 + lnb * (1.0 - sg)))
        dlng_ref[...] += jnp.sum(dl * xhat, axis=0, keepdims=True)
        dlnb_ref[...] += jnp.sum(dl, axis=0, keepdims=True)
        dxh = dl * lng_ref[...]
        dcb = rstd * (dxh - jnp.mean(dxh, axis=-1, keepdims=True)
                      - xhat * jnp.mean(dxh * xhat, axis=-1, keepdims=True))
        dcbias_ref[...] += jnp.sum(dcb, axis=0, keepdims=True)
        dpre_ref[:, 3 * d:4 * d] = dcb.astype(BF)
        _tap_grads(dcbw_ref, dpre_ref[:, 3 * d:4 * d].astype(F32), xb_ref, K_B, tm)
        dpre_ref[:, 4 * d:5 * d] = jnp.zeros((tm, d), BF)
        do = _nt(dyc_ref[...], wc_ref[...]).astype(BF)
        for h in range(N_HEADS):
            qh = proj_ref[:, 5 * d + h * hd:5 * d + (h + 1) * hd]
            kh = kv_ref[:, h * hd:(h + 1) * hd]
            vh = kv_ref[:, d + h * hd:d + (h + 1) * hd]
            doh = do[:, h * hd:(h + 1) * hd]
            p = _softmax_rows(_nt(qh, kh) * scale)
            dp = _nt(doh, vh)
            ds = (p * (dp - jnp.sum(dp * p, axis=-1, keepdims=True)) * scale).astype(BF)
            dpre_ref[:, 5 * d + h * hd:5 * d + (h + 1) * hd] = _nn(ds, kh).astype(BF)
            dkv_ref[:, h * hd:(h + 1) * hd] += _tn(ds, qh)
            dkv_ref[:, d + h * hd:d + (h + 1) * hd] += _tn(p.astype(BF), doh)

    row = lambda w_: pl.BlockSpec((tm, w_), lambda i: (i, 0))
    halo = lambda col: pl.BlockSpec((HALO, d), _prev_halo(tm, col))
    sq = lambda which: _resident((None, None, d, d), (l, which, 0, 0))
    acc = lambda r, c: pl.BlockSpec((r, c), lambda i: (0, 0))
    act = jax.ShapeDtypeStruct((t, d), BF)
    vec = lambda r, c: jax.ShapeDtypeStruct((r, c), F32)
    return _run(
        body, comm, name=name, grid=(t // tm,),
        in_specs=[row(d), row(9 * d), halo(1), halo(2), halo(3), halo(4), row(d), row(d), row(d), row(d),
                  _resident((m, 2 * d), (0, 0)),
                  _resident((None, K_A, d), (l, 0, 0)), _resident((None, K_B, d), (l, 0, 0)),
                  _resident((1, d), (0, 0)), _resident((1, d), (0, 0)), _resident((1, 3 * d), (0, 0)),
                  sq(0), sq(1), sq(2), sq(3)],
        out_specs=[row(9 * d), row(d), row(d), row(d), acc(m, 2 * d), acc(1, 3 * d), acc(1, d), acc(1, d),
                   acc(1, d), acc(K_A, d), acc(K_B, d)],
        out_shape=[jax.ShapeDtypeStruct((t, 9 * d), BF), act, act, act, vec(m, 2 * d), vec(1, 3 * d), vec(1, d),
                   vec(1, d), vec(1, d), vec(K_A, d), vec(K_B, d)],
        scratch_shapes=[pltpu.VMEM((HALO + tm, d), F32), pltpu.VMEM((HALO + tm, d), F32)],
        args=(dx1b, proj, proj, proj, proj, proj, ya, yb, yc, cb, kv, conv_a, conv_b, ln_g, ln_b, b_gate,
              wsq, wsq, wsq, wsq))


def _inproj_conv_bwd(dpre, proj, conv_a, conv_b, l, name, comm=None):
    t, d9 = dpre.shape
    d = d9 // 9
    tm = min(TM_ROW, t)
    n_t = t // tm

    def body(dpre_ref, nxa_ref, nxb_ref, proj_ref, ca_w, cb_w, dproj_ref, ya_ref, yb_ref):
        i = pl.program_id(0)
        keep = (i < n_t - 1).astype(F32)
        ya_ref[pl.ds(0, tm), :] = dpre_ref[:, d:2 * d].astype(F32)
        ya_ref[pl.ds(tm, HALO), :] = nxa_ref[...].astype(F32) * keep
        yb_ref[pl.ds(0, tm), :] = dpre_ref[:, 3 * d:4 * d].astype(F32)
        yb_ref[pl.ds(tm, HALO), :] = nxb_ref[...].astype(F32) * keep
        dproj_ref[:, 0:d] = dpre_ref[:, 0:d]
        dproj_ref[:, 5 * d:9 * d] = dpre_ref[:, 5 * d:9 * d]
        dcv = _anticausal_taps(ya_ref, ca_w, K_A, tm)
        dproj_ref[:, d:2 * d] = (dcv * proj_ref[:, 2 * d:3 * d].astype(F32)).astype(BF)
        dproj_ref[:, 2 * d:3 * d] = (dcv * proj_ref[:, d:2 * d].astype(F32)).astype(BF)
        dub = _anticausal_taps(yb_ref, cb_w, K_B, tm)
        sg = _sig(proj_ref[:, 4 * d:5 * d].astype(F32))
        dproj_ref[:, 3 * d:4 * d] = (dub * sg).astype(BF)
        dproj_ref[:, 4 * d:5 * d] = (dub * proj_ref[:, 3 * d:4 * d].astype(F32) * sg * (1.0 - sg)).astype(BF)

    return _run(
        body, comm, name=name, grid=(n_t,),
        in_specs=[pl.BlockSpec((tm, d9), lambda i: (i, 0)),
                  pl.BlockSpec((HALO, d), _next_halo(tm, t, 1)),
                  pl.BlockSpec((HALO, d), _next_halo(tm, t, 3)),
                  pl.BlockSpec((tm, d9), lambda i: (i, 0)),
                  _resident((None, K_A, d), (l, 0, 0)), _resident((None, K_B, d), (l, 0, 0))],
        out_specs=[pl.BlockSpec((tm, d9), lambda i: (i, 0))],
        out_shape=[jax.ShapeDtypeStruct((t, d9), BF)],
        scratch_shapes=[pltpu.VMEM((tm + HALO, d), F32), pltpu.VMEM((tm + HALO, d), F32)],
        args=(dpre, dpre, dpre, proj, conv_a, conv_b))


def _mem_kv_bwd(dkv, memn, mem, g, w_kv, l, name):
    m, d = mem.shape

    def body(dkv_ref, memn_ref, mem_ref, g_ref, w_ref, dw_ref, dg_ref):
        dkvb = dkv_ref[...].astype(BF)
        dw_ref[...] = _tn(memn_ref[...], dkvb)
        dmemn = _nt(dkvb, w_ref[...])
        xf = mem_ref[...]
        r = lax.rsqrt(jnp.mean(xf * xf, axis=-1, keepdims=True) + EPS)
        dg_ref[...] = jnp.sum(dmemn * (xf * r), axis=0, keepdims=True)

    return _pallas_call(
        body, name=name, grid=(1,),
        in_specs=[pl.BlockSpec((m, 2 * d), lambda i: (0, 0)), pl.BlockSpec((m, d), lambda i: (0, 0)),
                  pl.BlockSpec((m, d), lambda i: (0, 0)), pl.BlockSpec((1, d), lambda i: (0, 0)),
                  pl.BlockSpec((None, d, 2 * d), lambda i: (l, 0, 0))],
        out_specs=[pl.BlockSpec((d, 2 * d), lambda i: (0, 0)), pl.BlockSpec((1, d), lambda i: (0, 0))],
        out_shape=[jax.ShapeDtypeStruct((d, 2 * d), F32), jax.ShapeDtypeStruct((1, d), F32)],
        compiler_params=_params(1),
    )(dkv, memn, mem, g, w_kv)


def _dw_matmul(a, b, tn, name, comm=None):
    t, k = a.shape
    n = b.shape[1]
    tt = min(TT_DW, t)

    def body(a_ref, b_ref, o_ref):
        @pl.when(pl.program_id(1) == 0)
        def _():
            o_ref[...] = jnp.zeros_like(o_ref)

        o_ref[...] += _tn(a_ref[...], b_ref[...])

    return _run(
        body, comm, name=name, grid=(n // tn, t // tt),
        in_specs=[pl.BlockSpec((tt, k), lambda j, s: (s, 0)), pl.BlockSpec((tt, tn), lambda j, s: (s, j))],
        out_specs=[pl.BlockSpec((k, tn), lambda j, s: (0, j))],
        out_shape=[jax.ShapeDtypeStruct((k, n), F32)],
        args=(a, b))


class _GradReduce:
    def __init__(self, l, grads, core):
        self.l, self.core = l, core
        self.views = {n: _halves_view(n, grads[n]) for n in BIG}
        self.got, self.sums, self.pieces = {}, {}, {}

    def swap_program(self):
        return _swap_program([self.views[n] for n in BIG])

    def swapped(self, outs):
        self.got = dict(zip(BIG, outs))

    def add(self):
        self.sums = {n: _add_halves(self.views[n], self.got[n], self.core, f"add_halves_{n}_{self.l}") for n in BIG}

    def scatter_program(self, names):
        return _scatter_program([self.sums[n] for n in names], names)

    def scattered(self, names, outs):
        self.pieces.update(zip(names, outs))

    def run_alone(self):
        self.swapped(_run_comm(self.swap_program(), f"swap_halves_{self.l}"))
        self.add()
        self.scattered(BIG, _run_comm(self.scatter_program(BIG), f"scatter_pieces_{self.l}"))


def _step(x, mem, target, sh, conv_sh, small, core):
    depth = sh["w_in"].shape[0]
    d = x.shape[1]
    f2 = sh["w_up"].shape[2] * N_CHIPS
    tn_in = min(1024, d)
    tn_up = f2 // 11 if f2 % (11 * 128) == 0 and f2 // 11 >= 128 else f2
    row = lambda v: v.reshape(1, -1)
    one = lambda a: a[None]

    w_in, cab, cf = _run_comm(_gather_program([sh["w_in"]] + conv_sh, ("col", "small", "small"), (0, None, None)),
                              "gather_first")
    saved = []
    for l in range(depth):
        conv = dict(a=cab[l:l + 1, :K_A], b=cab[l:l + 1, K_A:], f=cf[l:l + 1])
        (h, proj), (wsq, w_kv) = _norm_matmul(
            x, row(small["norm_mix_g"][l]), one(w_in), 0, tn_in, f"in_proj_{l}",
            _gather_program([sh["wsq"], sh["w_kv"]], ("row", "col"), (l, l)))
        memn, kv = _mem_kv(mem, row(small["norm_mem_g"][l]), one(w_kv), 0, f"mem_kv_{l}")
        (x1, za, zb, o, ya, yb, yc, mg, cb), (w_up, w_down) = _mixer_fwd(
            proj, x, kv, conv["a"], conv["b"], row(small["conv_b_bias"][l]), row(small["ln_b_g"][l]),
            row(small["ln_b_b"][l]), row(small["b_gate"][l]), one(wsq), 0, f"mixer_fwd_{l}",
            _gather_program([sh["w_up"], sh["w_down"]], ("col", "row"), (l, l)))
        nxt = _gather_program([sh["w_in"]], ("col",), (l + 1,)) if l + 1 < depth else None
        (h2, up), w_in_next = _norm_matmul(x1, row(small["norm_ffn_g"][l]), one(w_up), 0, tn_up, f"up_proj_{l}", nxt)
        (x2, zf), _ = _ffn_down_fwd(up, x1, conv["f"], one(w_down), 0, f"ffn_down_fwd_{l}")
        saved.append(dict(x=x, x1=x1, memn=memn, kv=kv, h=h, proj=proj, za=za, zb=zb, o=o, ya=ya, yb=yb, yc=yc,
                          mg=mg, cb=cb, h2=h2, up=up, zf=zf, w_in=one(w_in), wsq=one(wsq), w_kv=one(w_kv),
                          w_up=one(w_up), w_down=one(w_down), conv=conv))
        x = x2
        w_in = w_in_next[0] if w_in_next else None
    dx, loss, dg_final = _final_loss(x, row(small["norm_final_g"]), target, "final_loss")
    sgrads, pieces = [None] * depth, [None] * depth
    above = None
    rest = tuple(n for n in BIG if n != "w_in")
    for l in reversed(range(depth)):
        s = saved[l]
        conv = s["conv"]
        (duc, dx2b, dconv_f), got = _ffn_down_bwd(dx, s["up"], conv["f"], s["w_down"], 0, f"ffn_down_bwd_{l}",
                                                  above and above.swap_program())
        if above:
            above.swapped(got)
            above.add()
        (dup,), _ = _ffn_conv_bwd(duc, conv["f"], 0, f"ffn_conv_bwd_{l}")
        (dx1, dx1b, dg_ffn), got = _nt_matmul_norm_bwd(
            dup, s["w_up"], 0, s["x1"], row(small["norm_ffn_g"][l]), dx, tn_up, f"up_proj_bwd_{l}",
            above and above.scatter_program(("w_in",)))
        if above:
            above.scattered(("w_in",), got)
        (dpre, dya, dyb, dyc, dkv, dbg, dlng, dlnb, dcbias, dconv_a, dconv_b), got = _mixer_bwd(
            dx1b, s["proj"], s["ya"], s["yb"], s["yc"], s["cb"], s["kv"], conv["a"], conv["b"],
            row(small["ln_b_g"][l]), row(small["ln_b_b"][l]), row(small["b_gate"][l]), s["wsq"], 0,
            f"mixer_bwd_{l}", above and above.scatter_program(rest))
        if above:
            above.scattered(rest, got)
            pieces[above.l] = above.pieces
        (dproj,), _ = _inproj_conv_bwd(dpre, s["proj"], conv["a"], conv["b"], 0, f"inproj_conv_bwd_{l}")
        (dx0, _, dg_mix), _ = _nt_matmul_norm_bwd(dproj, s["w_in"], 0, s["x"], row(small["norm_mix_g"][l]), dx1,
                                                  tn_in, f"in_proj_bwd_{l}")
        dw_kv, dg_mem = _mem_kv_bwd(dkv, s["memn"], mem, row(small["norm_mem_g"][l]), s["w_kv"], 0,
                                    f"mem_kv_bwd_{l}")
        grads = dict(
            w_in=_dw_matmul(s["h"], dproj, tn_in, f"dw_in_{l}")[0][0],
            w_a_out=_dw_matmul(s["za"], dya, d, f"dw_a_out_{l}")[0][0],
            w_b_out=_dw_matmul(s["zb"], dyb, d, f"dw_b_out_{l}")[0][0],
            w_att_out=_dw_matmul(s["o"], dyc, d, f"dw_att_out_{l}")[0][0],
            w_o=_dw_matmul(s["mg"], dx1b, d, f"dw_o_{l}")[0][0],
            w_kv=dw_kv,
            w_up=_dw_matmul(s["h2"], dup, tn_up, f"dw_up_{l}")[0][0],
            w_down=_dw_matmul(s["zf"], dx2b, d, f"dw_down_{l}")[0][0],
        )
        sgrads[l] = dict(norm_mix_g=dg_mix, norm_mem_g=dg_mem, b_gate=dbg, conv_b_bias=dcbias, ln_b_g=dlng,
                         ln_b_b=dlnb, norm_ffn_g=dg_ffn, conv_a_w=dconv_a, conv_b_w=dconv_b, conv_ffn_w=dconv_f)
        above = _GradReduce(l, grads, core)
        dx = dx0
    above.run_alone()
    pieces[above.l] = above.pieces
    return loss, dx, sgrads, dg_final, pieces


BIG = ("w_in", "w_a_out", "w_b_out", "w_att_out", "w_o", "w_kv", "w_up", "w_down")
COL = ("w_in", "w_kv", "w_up")
SQUARES = ("w_a_out", "w_b_out", "w_att_out", "w_o")
ANY = pl.BlockSpec(memory_space=pl.ANY)


def _place():
    x, y, c = lax.axis_index("x"), lax.axis_index("y"), lax.axis_index("c")
    chips = [(1 - x, y), (x, 1 - y), (1 - x, 1 - y)]
    return x, y, c, 2 * x + y, chips


def _remote(src, dst, send_sem, recv_sem, dev):
    return pltpu.make_async_remote_copy(src_ref=src, dst_ref=dst, send_sem=send_sem, recv_sem=recv_sem,
                                        device_id=dev, device_id_type=MESH)


def _window(ref, rows, cols):
    return ref.at[(slice(None),) * (len(ref.shape) - 2) + (rows, cols)]


def _row_tile(rows, cols, unit=16, limit=1 << 20):
    best = unit
    for tr in range(unit, rows + 1, unit):
        if rows % tr == 0 and tr * cols <= limit:
            best = tr
    return best


def _cast_shard(w, name):
    depth, k, n = w.shape

    def body(w_ref, o_ref):
        o_ref[...] = w_ref[...].astype(BF)

    return _pallas_call(
        body, name=name, grid=(depth,),
        in_specs=[pl.BlockSpec((None, k, n), lambda l: (l, 0, 0))],
        out_specs=pl.BlockSpec((None, k, n), lambda l: (l, 0, 0)),
        out_shape=jax.ShapeDtypeStruct((depth, k, n), BF),
        compiler_params=_params(1),
    )(w)


def _cast_squares(ws, name):
    depth, k, n = ws[0].shape

    def body(a_ref, b_ref, c_ref, d_ref, o_ref):
        for i, ref in enumerate((a_ref, b_ref, c_ref, d_ref)):
            o_ref[i] = ref[...].astype(BF)

    return _pallas_call(
        body, name=name, grid=(depth,),
        in_specs=[pl.BlockSpec((None, k, n), lambda l: (l, 0, 0))] * 4,
        out_specs=pl.BlockSpec((None, 4, k, n), lambda l: (l, 0, 0, 0)),
        out_shape=jax.ShapeDtypeStruct((depth, 4, k, n), BF),
        compiler_params=_params(1),
    )(*ws)


def _gather_program(shards, kinds, layers):
    n_t = len(shards)

    def full_shape(s, kind, layer):
        shp = list(s.shape if layer is None else s.shape[1:])
        shp[-2 if kind == "row" else -1] *= N_CHIPS
        return tuple(shp)

    def run(phase, ins, full, sems):
        shard = [r if l is None else r.at[l] for r, l in zip(ins, layers)]
        ici_send, ici_recv, sib_send, sib_recv, loc_sem = sems
        x, y, c, me, chips = _place()
        sibling = (x, y, 1 - c)

        def src_part(i, half):
            s, kind = shard[i], kinds[i]
            if kind == "small":
                return s
            r = s.shape[-2] // 2
            return _window(s, pl.ds(pl.multiple_of(half * r, 16), r), slice(None))

        def dst_part(i, chip, half):
            f, s, kind = full[i], shard[i], kinds[i]
            rows, cols = s.shape[-2], s.shape[-1]
            if kind == "small":
                return _window(f, slice(None), pl.ds(pl.multiple_of(chip * cols, 128), cols))
            r = rows // 2
            if kind == "col":
                return _window(f, pl.ds(pl.multiple_of(half * r, 16), r), pl.ds(pl.multiple_of(chip * cols, 128), cols))
            return _window(f, pl.ds(pl.multiple_of(chip * rows + half * r, 16), r), slice(None))

        def own_slot(i):
            f, s, kind = full[i], shard[i], kinds[i]
            rows, cols = s.shape[-2], s.shape[-1]
            if kind == "row":
                return _window(f, pl.ds(pl.multiple_of(me * rows, 16), rows), slice(None))
            return _window(f, slice(None), pl.ds(pl.multiple_of(me * cols, 128), cols))

        local = [pltpu.make_async_copy(shard[i], own_slot(i), loc_sem.at[i]) for i in range(n_t)]
        sends = []
        for i in range(n_t):
            for j, chip in enumerate(chips):
                sends.append(_remote(src_part(i, c), dst_part(i, me, c), ici_send.at[3 * i + j],
                                     ici_recv.at[3 * i + j], (*chip, c)))
        if phase == "start":
            for cp in local + sends:
                cp.start()
            return
        passed = []
        for i in range(n_t):
            for j, chip in enumerate(chips):
                k = 2 * chip[0] + chip[1]
                landed = dst_part(i, k, c)
                _remote(landed, landed, ici_send.at[3 * i + j], ici_recv.at[3 * i + j], (*chip, c)).wait_recv()
                if kinds[i] != "small":
                    cp = _remote(landed, landed, sib_send.at[3 * i + j], sib_recv.at[3 * i + j], sibling)
                    cp.start()
                    passed.append(cp)
        for i in range(n_t):
            if kinds[i] == "small":
                continue
            for j, chip in enumerate(chips):
                k = 2 * chip[0] + chip[1]
                other = dst_part(i, k, 1 - c)
                _remote(other, other, sib_send.at[3 * i + j], sib_recv.at[3 * i + j], sibling).wait_recv()
        for cp in sends + passed:
            cp.wait_send()
        for cp in local:
            cp.wait()

    outs = [jax.ShapeDtypeStruct(full_shape(s, k, l), s.dtype) for s, k, l in zip(shards, kinds, layers)]
    sems = [pltpu.SemaphoreType.DMA((3 * n_t,))] * 4 + [pltpu.SemaphoreType.DMA((n_t,))]
    return _Comm(shards, outs, sems, functools.partial(run, "start"), functools.partial(run, "finish"))


def _all_reduce_small(part, name):
    r, n = part.shape

    def body(in_ref, out_ref, gather_ref, send_sems, recv_sems):
        x, y, c, _, _ = _place()
        me = 4 * x + 2 * y + c
        gather_ref[me] = in_ref[...]
        sends = []
        for k in range(1, N_DEV):
            peer = (me + k) % N_DEV
            sends.append(_remote(in_ref, gather_ref.at[me], send_sems.at[k - 1], recv_sems.at[k - 1],
                                 (peer // 4, (peer // 2) % 2, peer % 2)))
        for cp in sends:
            cp.start()
        for k in range(1, N_DEV):
            origin = (me + N_DEV - k) % N_DEV
            _remote(in_ref, gather_ref.at[origin], send_sems.at[k - 1], recv_sems.at[k - 1],
                    (x, y, c)).wait_recv()
        for cp in sends:
            cp.wait_send()
        total = gather_ref[0]
        for dev in range(1, N_DEV):
            total = total + gather_ref[dev]
        out_ref[...] = total

    vm = pl.BlockSpec(memory_space=pltpu.VMEM)
    return _pallas_call(
        body, name=name, in_specs=[vm], out_specs=vm, out_shape=jax.ShapeDtypeStruct((r, n), F32),
        scratch_shapes=[pltpu.VMEM((N_DEV, r, n), F32), pltpu.SemaphoreType.DMA((N_DEV - 1,)),
                        pltpu.SemaphoreType.DMA((N_DEV - 1,))],
        compiler_params=pltpu.CompilerParams(vmem_limit_bytes=VMEM_LIMIT),
    )(part)


def _halves_view(name, dw):
    k, n = dw.shape
    s = 1 if name in COL else N_CHIPS
    return dw.reshape(s, 2, k // (2 * s), n)


def _swap_program(views):
    n_t = len(views)

    def run(phase, src, dst, sems):
        send_sems, recv_sems = sems
        x, y, c, _, _ = _place()
        copies = [_remote(src[i].at[:, 1 - c], dst[i], send_sems.at[i], recv_sems.at[i], (x, y, 1 - c))
                  for i in range(n_t)]
        for cp in copies:
            if phase == "start":
                cp.start()
            else:
                cp.wait()

    outs = [jax.ShapeDtypeStruct((v.shape[0],) + v.shape[2:], F32) for v in views]
    sems = [pltpu.SemaphoreType.DMA((n_t,)), pltpu.SemaphoreType.DMA((n_t,))]
    return _Comm(views, outs, sems, functools.partial(run, "start"), functools.partial(run, "finish"))


def _add_halves(view, got, core, name):
    s, _, r, n = view.shape
    tr = _row_tile(r, n)

    def body(c_ref, a_ref, b_ref, o_ref):
        o_ref[...] = (a_ref[...] + b_ref[...]).astype(BF)

    return _pallas_call(
        body, name=name,
        grid_spec=pltpu.PrefetchScalarGridSpec(
            num_scalar_prefetch=1, grid=(s, r // tr),
            in_specs=[pl.BlockSpec((None, None, tr, n), lambda i, j, c_ref: (i, c_ref[0], j, 0)),
                      pl.BlockSpec((None, tr, n), lambda i, j, c_ref: (i, j, 0))],
            out_specs=pl.BlockSpec((None, tr, n), lambda i, j, c_ref: (i, j, 0))),
        out_shape=jax.ShapeDtypeStruct((s, r, n), BF),
        compiler_params=_params(2),
    )(core, view, got)


def _scatter_program(sums, names):
    n_t = len(sums)

    def piece_shape(nm, p):
        s, r, n = p.shape
        return (r, n // N_CHIPS) if nm in COL else (r, n)

    def run(phase, src, dst, sems):
        ici_send, ici_recv, sib_send, sib_recv, loc_sem = sems
        x, y, c, me, chips = _place()
        sibling = (x, y, 1 - c)

        def piece(i, chip):
            if names[i] in COL:
                cw = src[i].shape[2] // N_CHIPS
                return src[i].at[0, :, pl.ds(pl.multiple_of(chip * cw, 128), cw)]
            return src[i].at[chip]

        local = [pltpu.make_async_copy(piece(i, me), dst[i].at[c, me], loc_sem.at[i]) for i in range(n_t)]
        sends = []
        for i in range(n_t):
            sends.append(_remote(piece(i, me), dst[i].at[c, me], sib_send.at[4 * i + 3], sib_recv.at[4 * i + 3],
                                 sibling))
            for j, chip in enumerate(chips):
                k = 2 * chip[0] + chip[1]
                sends.append(_remote(piece(i, k), dst[i].at[c, me], ici_send.at[3 * i + j], ici_recv.at[3 * i + j],
                                     (*chip, c)))
        if phase == "start":
            for cp in local + sends:
                cp.start()
            return
        passed = []
        for i in range(n_t):
            for j, chip in enumerate(chips):
                k = 2 * chip[0] + chip[1]
                landed = dst[i].at[c, k]
                _remote(landed, landed, ici_send.at[3 * i + j], ici_recv.at[3 * i + j], (*chip, c)).wait_recv()
                cp = _remote(landed, landed, sib_send.at[4 * i + j], sib_recv.at[4 * i + j], sibling)
                cp.start()
                passed.append(cp)
        for i in range(n_t):
            other = dst[i].at[1 - c, me]
            _remote(other, other, sib_send.at[4 * i + 3], sib_recv.at[4 * i + 3], sibling).wait_recv()
            for j, chip in enumerate(chips):
                k = 2 * chip[0] + chip[1]
                other = dst[i].at[1 - c, k]
                _remote(other, other, sib_send.at[4 * i + j], sib_recv.at[4 * i + j], sibling).wait_recv()
        for cp in sends + passed:
            cp.wait_send()
        for cp in local:
            cp.wait()

    outs = [jax.ShapeDtypeStruct((2, N_CHIPS) + piece_shape(nm, p), BF) for nm, p in zip(names, sums)]
    sems = [pltpu.SemaphoreType.DMA((3 * n_t,))] * 2 + [pltpu.SemaphoreType.DMA((4 * n_t,))] * 2 + [
        pltpu.SemaphoreType.DMA((n_t,))]
    return _Comm(sums, outs, sems, functools.partial(run, "start"), functools.partial(run, "finish"))


def _adamw(w, g, m, v):
    m = ADAM_B1 * m + (1.0 - ADAM_B1) * g
    v = ADAM_B2 * v + (1.0 - ADAM_B2) * (g * g)
    m_hat = m / (1.0 - ADAM_B1 ** ADAM_STEP)
    v_hat = v / (1.0 - ADAM_B2 ** ADAM_STEP)
    return -ADAM_LR * (m_hat / (jnp.sqrt(v_hat) + ADAM_EPS) + ADAM_WD * w), m, v


def _adam_shard(pieces, w, m, v, l, prev, name):
    depth, rows, cw = w.shape
    hr = rows // 2
    tr = _row_tile(hr, cw, limit=1 << 18)
    n_i = hr // tr

    def body(*refs):
        z_ref, w_ref, m_ref, v_ref = refs[:4]
        g_ref, d_ref, nm_ref, nv_ref = refs[-4:]
        g = z_ref[0].astype(F32)
        for k in range(1, N_CHIPS):
            g = g + z_ref[k].astype(F32)
        g_ref[...] = g
        d_ref[...], nm_ref[...], nv_ref[...] = _adamw(w_ref[...], g, m_ref[...], v_ref[...])

    par = pl.BlockSpec((None, tr, cw), lambda h, i: (l, h * n_i + i, 0))
    out = jax.ShapeDtypeStruct((depth, rows, cw), F32)
    extra = [] if prev is None else list(prev)
    return _pallas_call(
        body, name=name, grid=(2, n_i),
        in_specs=[pl.BlockSpec((None, N_CHIPS, tr, cw), lambda h, i: (h, 0, i, 0)), par, par, par] + [ANY] * len(extra),
        out_specs=[par] * 4, out_shape=[out] * 4,
        input_output_aliases={4 + k: k for k in range(len(extra))},
        compiler_params=_params(2),
    )(pieces, w, m, v, *extra)


def _adam_small(g, w, m, v, name):
    def body(g_ref, w_ref, m_ref, v_ref, d_ref, nm_ref, nv_ref):
        d_ref[...], nm_ref[...], nv_ref[...] = _adamw(w_ref[...], g_ref[...], m_ref[...], v_ref[...])

    vm = pl.BlockSpec(memory_space=pltpu.VMEM)
    out = jax.ShapeDtypeStruct(g.shape, F32)
    return _pallas_call(body, name=name, in_specs=[vm] * 4, out_specs=[vm] * 3, out_shape=[out] * 3)(g, w, m, v)


WEIGHTS = ("norm_mix_g", "norm_mem_g", "w_in", "b_gate", "conv_a_w", "w_a_out", "conv_b_w", "conv_b_bias", "ln_b_g",
           "ln_b_b", "w_b_out", "w_kv", "w_att_out", "w_o", "norm_ffn_g", "w_up", "conv_ffn_w", "w_down",
           "norm_final_g")
REPLICATED = ("norm_mix_g", "norm_mem_g", "b_gate", "conv_b_bias", "ln_b_g", "ln_b_b", "norm_ffn_g")
CONVS = ("conv_a_w", "conv_b_w", "conv_ffn_w")
PACK_WIDTH = 1024


def _pack(arrays):
    flat = jnp.concatenate([a.reshape(-1) for a in arrays])
    size = -(-flat.shape[0] // (8 * PACK_WIDTH)) * (8 * PACK_WIDTH)
    return jnp.pad(flat, (0, size - flat.shape[0])).reshape(-1, PACK_WIDTH)


def _unpack(packed, shapes):
    flat = packed.reshape(-1)
    out, at = [], 0
    for shp in shapes:
        n = math.prod(shp)
        out.append(flat[at:at + n].reshape(shp))
        at += n
    return out


def kernel(x, mem, norm_mix_g, norm_mem_g, w_in, b_gate, conv_a_w, w_a_out, conv_b_w, conv_b_bias, ln_b_g, ln_b_b, w_b_out, w_kv, w_att_out, w_o, norm_ffn_g, w_up, conv_ffn_w, w_down, norm_final_g, loss_target, m_norm_mix_g, m_norm_mem_g, m_w_in, m_b_gate, m_conv_a_w, m_w_a_out, m_conv_b_w, m_conv_b_bias, m_ln_b_g, m_ln_b_b, m_w_b_out, m_w_kv, m_w_att_out, m_w_o, m_norm_ffn_g, m_w_up, m_conv_ffn_w, m_w_down, m_norm_final_g, v_norm_mix_g, v_norm_mem_g, v_w_in, v_b_gate, v_conv_a_w, v_w_a_out, v_conv_b_w, v_conv_b_bias, v_ln_b_g, v_ln_b_b, v_w_b_out, v_w_kv, v_w_att_out, v_w_o, v_norm_ffn_g, v_w_up, v_conv_ffn_w, v_w_down, v_norm_final_g):
    w = dict(norm_mix_g=norm_mix_g, norm_mem_g=norm_mem_g, w_in=w_in, b_gate=b_gate, conv_a_w=conv_a_w,
             w_a_out=w_a_out, conv_b_w=conv_b_w, conv_b_bias=conv_b_bias, ln_b_g=ln_b_g, ln_b_b=ln_b_b,
             w_b_out=w_b_out, w_kv=w_kv, w_att_out=w_att_out, w_o=w_o, norm_ffn_g=norm_ffn_g, w_up=w_up,
             conv_ffn_w=conv_ffn_w, w_down=w_down, norm_final_g=norm_final_g)
    mom = dict(norm_mix_g=m_norm_mix_g, norm_mem_g=m_norm_mem_g, w_in=m_w_in, b_gate=m_b_gate, conv_a_w=m_conv_a_w,
               w_a_out=m_w_a_out, conv_b_w=m_conv_b_w, conv_b_bias=m_conv_b_bias, ln_b_g=m_ln_b_g, ln_b_b=m_ln_b_b,
               w_b_out=m_w_b_out, w_kv=m_w_kv, w_att_out=m_w_att_out, w_o=m_w_o, norm_ffn_g=m_norm_ffn_g,
               w_up=m_w_up, conv_ffn_w=m_conv_ffn_w, w_down=m_w_down, norm_final_g=m_norm_final_g)
    var = dict(norm_mix_g=v_norm_mix_g, norm_mem_g=v_norm_mem_g, w_in=v_w_in, b_gate=v_b_gate, conv_a_w=v_conv_a_w,
               w_a_out=v_w_a_out, conv_b_w=v_conv_b_w, conv_b_bias=v_conv_b_bias, ln_b_g=v_ln_b_g, ln_b_b=v_ln_b_b,
               w_b_out=v_w_b_out, w_kv=v_w_kv, w_att_out=v_w_att_out, w_o=v_w_o, norm_ffn_g=v_norm_ffn_g,
               w_up=v_w_up, conv_ffn_w=v_conv_ffn_w, w_down=v_w_down, norm_final_g=v_norm_final_g)
    depth = w_in.shape[0]
    chip = 2 * lax.axis_index("x") + lax.axis_index("y")
    core = lax.axis_index("c").astype(jnp.int32).reshape(1)

    sh = dict(w_in=_cast_shard(w_in, "cast_w_in"), wsq=_cast_squares([w[n] for n in SQUARES], "cast_squares"),
              w_kv=_cast_shard(w_kv, "cast_w_kv"), w_up=_cast_shard(w_up, "cast_w_up"),
              w_down=_cast_shard(w_down, "cast_w_down"))
    conv_sh = [jnp.concatenate([conv_a_w, conv_b_w], axis=1), conv_ffn_w]
    small = {n: w[n] for n in REPLICATED + ("norm_final_g",)}

    loss, dx, sgrads, dg_final, pieces = _step(x[0], mem[0], loss_target[0], sh, conv_sh, small, core)

    res = {n: None for n in BIG}
    for l in reversed(range(depth)):
        for n in BIG:
            res[n] = _adam_shard(pieces[l][n], w[n], mom[n], var[n], l, res[n], f"adam_{n}_{l}")

    per_layer = REPLICATED + CONVS
    parts = [sgrads[l][n] for l in range(depth) for n in per_layer] + [dg_final]
    total = _all_reduce_small(_pack(parts), "all_reduce_small")
    shapes = [sgrads[l][n].shape for l in range(depth) for n in per_layer] + [dg_final.shape]
    summed = _unpack(total, shapes)
    g_small = {}
    for k, n in enumerate(per_layer):
        full = jnp.stack([summed[l * len(per_layer) + k] for l in range(depth)])
        if n in CONVS:
            cols = w[n].shape[-1]
            full = lax.dynamic_slice_in_dim(full, chip * cols, cols, axis=2)
        g_small[n] = full.reshape(w[n].shape)
    g_small["norm_final_g"] = summed[-1].reshape(norm_final_g.shape)
    names = per_layer + ("norm_final_g",)
    d_p, m_p, v_p = _adam_small(_pack([g_small[n] for n in names]), _pack([w[n] for n in names]),
                                _pack([mom[n] for n in names]), _pack([var[n] for n in names]), "adam_small")
    shp = [w[n].shape for n in names]
    for n, dl, nm, nv in zip(names, _unpack(d_p, shp), _unpack(m_p, shp), _unpack(v_p, shp)):
        res[n] = (g_small[n], dl, nm, nv)

    loss = lax.psum(loss[0, 0], ("x", "y", "c"))
    return (loss, dx.reshape(x.shape), *[res[n][0] for n in WEIGHTS], *[res[n][1] for n in WEIGHTS],
            *[res[n][2] for n in WEIGHTS], *[res[n][3] for n in WEIGHTS])
```

```python
import functools
import math

import jax
import jax.numpy as jnp
from jax import lax
from jax.experimental import pallas as pl
from jax.experimental.pallas import tpu as pltpu

F32 = jnp.float32
BF = jnp.bfloat16
EPS = 1e-6
N_HEADS = 4
K_A, K_B, K_F = 3, 31, 3
ADAM_LR, ADAM_B1, ADAM_B2, ADAM_EPS, ADAM_WD, ADAM_STEP = 0.001, 0.9, 0.999, 1e-08, 0.01, 10
N_CHIPS = 4
N_DEV = 8
HALO = 32
TM_ROW = 256
TM_MM = 1024
TT_DW = 4096
DW_LHS_ELEMS = 4 * 1024 * 1024
TR_EW = 128
VMEM_LIMIT = 56 * 1024 * 1024
MESH = pl.DeviceIdType.MESH
_pallas_call = pl.pallas_call


def _params(n_axes):
    return pltpu.CompilerParams(dimension_semantics=("arbitrary",) * n_axes, vmem_limit_bytes=VMEM_LIMIT)


def _resident(shape, index):
    return pl.BlockSpec(shape, lambda *_: index, pipeline_mode=pl.Buffered(1))


def _sig(x):
    return 1.0 / (1.0 + jnp.exp(-x))


def _nt(a, b):
    return lax.dot_general(a, b, (((1,), (1,)), ((), ())), preferred_element_type=F32)


def _tn(a, b):
    return lax.dot_general(a, b, (((0,), (0,)), ((), ())), preferred_element_type=F32)


def _nn(a, b):
    return jnp.dot(a, b, preferred_element_type=F32)


class _Comm:
    def __init__(self, inputs, out_shape, scratch, start, finish, aliases=None):
        self.inputs, self.out_shape, self.scratch = list(inputs), list(out_shape), list(scratch)
        self.start, self.finish = start, finish
        self.aliases = dict(aliases or {})


def _join(programs):
    programs = [p for p in programs if p is not None]
    if not programs:
        return None

    def split(seq, counts):
        parts, at = [], 0
        for n in counts:
            parts.append(seq[at:at + n])
            at += n
        return parts

    n_in = [len(p.inputs) for p in programs]
    n_out = [len(p.out_shape) for p in programs]
    n_s = [len(p.scratch) for p in programs]

    def phase(which):
        def run(ins, outs, sems):
            for p, i, o, s in zip(programs, split(ins, n_in), split(outs, n_out), split(sems, n_s)):
                getattr(p, which)(i, o, s)
        return run

    aliases, in_at, out_at = {}, 0, 0
    for p, i, o in zip(programs, n_in, n_out):
        aliases.update({in_at + a: out_at + b for a, b in p.aliases.items()})
        in_at, out_at = in_at + i, out_at + o
    return _Comm([a for p in programs for a in p.inputs], [a for p in programs for a in p.out_shape],
                 [a for p in programs for a in p.scratch], phase("start"), phase("finish"), aliases)


def _run(body, comm, *, name, grid, in_specs, out_specs, out_shape, args, scratch_shapes=()):
    n_axes = len(grid)
    if comm is None:
        outs = _pallas_call(body, name=name, grid=grid, in_specs=list(in_specs), out_specs=list(out_specs),
                            out_shape=list(out_shape), scratch_shapes=list(scratch_shapes),
                            compiler_params=_params(n_axes))(*args)
        return list(outs), []
    counts = (len(in_specs), len(comm.inputs), len(out_specs), len(comm.out_shape), len(scratch_shapes),
              len(comm.scratch))

    def hosted(*refs):
        parts, at = [], 0
        for n in counts:
            parts.append(refs[at:at + n])
            at += n
        ins, c_ins, outs, c_outs, scr, c_sems = parts
        ids = [pl.program_id(a) for a in range(n_axes)]
        first = functools.reduce(jnp.logical_and, [i == 0 for i in ids])
        last = functools.reduce(jnp.logical_and, [i == g - 1 for i, g in zip(ids, grid)])

        @pl.when(first)
        def _():
            comm.start(c_ins, c_outs, c_sems)

        body(*ins, *outs, *scr)

        @pl.when(last)
        def _():
            comm.finish(c_ins, c_outs, c_sems)

    any_spec = pl.BlockSpec(memory_space=pl.ANY)
    res = _pallas_call(
        hosted, name=name, grid=grid, in_specs=list(in_specs) + [any_spec] * counts[1],
        out_specs=list(out_specs) + [any_spec] * counts[3], out_shape=list(out_shape) + comm.out_shape,
        scratch_shapes=list(scratch_shapes) + comm.scratch, compiler_params=_params(n_axes),
        input_output_aliases={counts[0] + a: counts[2] + b for a, b in comm.aliases.items()},
    )(*args, *comm.inputs)
    return list(res[:counts[2]]), list(res[counts[2]:])


def _run_comm(comm, name):
    n_in, n_out = len(comm.inputs), len(comm.out_shape)

    def body(*refs):
        ins, outs, sems = refs[:n_in], refs[n_in:n_in + n_out], refs[n_in + n_out:]
        comm.start(ins, outs, sems)
        comm.finish(ins, outs, sems)

    any_spec = pl.BlockSpec(memory_space=pl.ANY)
    return list(_pallas_call(body, name=name, in_specs=[any_spec] * n_in, out_specs=[any_spec] * n_out,
                             out_shape=comm.out_shape, scratch_shapes=comm.scratch,
                             input_output_aliases=comm.aliases)(*comm.inputs))


def _causal_taps(xx_ref, w_ref, k_taps, tm):
    base = HALO - k_taps + 1
    acc = w_ref[0:1, :] * xx_ref[pl.ds(base, tm), :]
    for k in range(1, k_taps):
        acc = acc + w_ref[k:k + 1, :] * xx_ref[pl.ds(base + k, tm), :]
    return acc


def _anticausal_taps(yy_ref, w_ref, k_taps, tm):
    acc = w_ref[k_taps - 1:k_taps, :] * yy_ref[pl.ds(0, tm), :]
    for k in range(k_taps - 1):
        acc = acc + w_ref[k:k + 1, :] * yy_ref[pl.ds(k_taps - 1 - k, tm), :]
    return acc


def _tap_grads(dw_ref, dy, xx_ref, k_taps, tm):
    base = HALO - k_taps + 1
    for k in range(k_taps):
        dw_ref[k:k + 1, :] += jnp.sum(dy * xx_ref[pl.ds(base + k, tm), :], axis=0, keepdims=True)


def _prev_halo(tm, col):
    return lambda i: (jnp.maximum(i * (tm // HALO) - 1, 0), col)


def _next_halo(tm, n_rows, col):
    return lambda i: (jnp.minimum((i + 1) * (tm // HALO), n_rows // HALO - 1), col)


def _norm_matmul(x, g, w, l, tn, name, comm=None):
    t, d = x.shape
    n = w.shape[2]
    tm = min(TM_MM, t)

    def body(x_ref, g_ref, w_ref, h_ref, y_ref):
        @pl.when(pl.program_id(1) == 0)
        def _():
            xf = x_ref[...]
            r = lax.rsqrt(jnp.mean(xf * xf, axis=-1, keepdims=True) + EPS)
            h_ref[...] = ((xf * r) * g_ref[...]).astype(BF)

        y_ref[...] = _nn(h_ref[...], w_ref[...]).astype(BF)

    return _run(
        body, comm, name=name, grid=(t // tm, n // tn),
        in_specs=[pl.BlockSpec((tm, d), lambda i, j: (i, 0)),
                  pl.BlockSpec((1, d), lambda i, j: (0, 0)),
                  pl.BlockSpec((None, d, tn), lambda i, j: (l, 0, j))],
        out_specs=[pl.BlockSpec((tm, d), lambda i, j: (i, 0)),
                   pl.BlockSpec((tm, tn), lambda i, j: (i, j))],
        out_shape=[jax.ShapeDtypeStruct((t, d), BF), jax.ShapeDtypeStruct((t, n), BF)],
        args=(x, g, w))


def _mem_kv(mem, g, w_kv, l, name):
    m, d = mem.shape

    def body(mem_ref, g_ref, w_ref, memn_ref, kv_ref):
        xf = mem_ref[...]
        r = lax.rsqrt(jnp.mean(xf * xf, axis=-1, keepdims=True) + EPS)
        memn = ((xf * r) * g_ref[...]).astype(BF)
        memn_ref[...] = memn
        kv_ref[...] = _nn(memn, w_ref[...]).astype(BF)

    return _pallas_call(
        body, name=name, grid=(1,),
        in_specs=[pl.BlockSpec((m, d), lambda i: (0, 0)),
                  pl.BlockSpec((1, d), lambda i: (0, 0)),
                  pl.BlockSpec((None, d, 2 * d), lambda i: (l, 0, 0))],
        out_specs=[pl.BlockSpec((m, d), lambda i: (0, 0)),
                   pl.BlockSpec((m, 2 * d), lambda i: (0, 0))],
        out_shape=[jax.ShapeDtypeStruct((m, d), BF), jax.ShapeDtypeStruct((m, 2 * d), BF)],
        compiler_params=_params(1),
    )(mem, g, w_kv)


def _load_branch_inputs(i, proj_ref, gch_ref, vh_ref, u0h_ref, ugh_ref, xa_ref, xb_ref, d):
    gc = proj_ref[:, d:2 * d].astype(F32)
    v = proj_ref[:, 2 * d:3 * d].astype(F32)
    u0 = proj_ref[:, 3 * d:4 * d].astype(F32)
    ug = proj_ref[:, 4 * d:5 * d].astype(F32)
    keep = (i > 0).astype(F32)
    xa_ref[pl.ds(0, HALO), :] = gch_ref[...].astype(F32) * vh_ref[...].astype(F32) * keep
    xa_ref[pl.ds(HALO, gc.shape[0]), :] = gc * v
    xb_ref[pl.ds(0, HALO), :] = u0h_ref[...].astype(F32) * _sig(ugh_ref[...].astype(F32)) * keep
    xb_ref[pl.ds(HALO, gc.shape[0]), :] = u0 * _sig(ug)


def _softmax_rows(s):
    e = jnp.exp(s - jnp.max(s, axis=-1, keepdims=True))
    return e / jnp.sum(e, axis=-1, keepdims=True)


def _mixer_fwd(proj, x, kv, conv_a, conv_b, cbias, ln_g, ln_b, b_gate, wsq, l, name, comm=None):
    t, d = x.shape
    m = kv.shape[0]
    tm = min(TM_ROW, t)
    hd = d // N_HEADS
    scale = 1.0 / math.sqrt(hd)

    def body(proj_ref, gch_ref, vh_ref, u0h_ref, ugh_ref, x_ref, kv_ref, ca_w, cb_w, cbias_ref, lng_ref, lnb_ref,
             bg_ref, wa_ref, wb_ref, wc_ref, wo_ref,
             x1_ref, za_ref, zb_ref, o_ref, ya_ref, yb_ref, yc_ref, mg_ref, cb_ref, xa_ref, xb_ref):
        i = pl.program_id(0)
        _load_branch_inputs(i, proj_ref, gch_ref, vh_ref, u0h_ref, ugh_ref, xa_ref, xb_ref, d)
        gb = proj_ref[:, 0:d].astype(F32)
        za = (gb * _causal_taps(xa_ref, ca_w, K_A, tm)).astype(BF)
        za_ref[...] = za
        ya = _nn(za, wa_ref[...])
        ya_ref[...] = ya.astype(BF)
        cb = _causal_taps(xb_ref, cb_w, K_B, tm) + cbias_ref[...]
        cb_ref[...] = cb
        mu = jnp.mean(cb, axis=-1, keepdims=True)
        dlt = cb - mu
        rstd = lax.rsqrt(jnp.mean(dlt * dlt, axis=-1, keepdims=True) + EPS)
        lnb = (dlt * rstd) * lng_ref[...] + lnb_ref[...]
        zb = (lnb * _sig(lnb)).astype(BF)
        zb_ref[...] = zb
        yb = _nn(zb, wb_ref[...])
        yb_ref[...] = yb.astype(BF)
        for h in range(N_HEADS):
            qh = proj_ref[:, 5 * d + h * hd:5 * d + (h + 1) * hd]
            kh = kv_ref[:, h * hd:(h + 1) * hd]
            vh = kv_ref[:, d + h * hd:d + (h + 1) * hd]
            p = _softmax_rows(_nt(qh, kh) * scale)
            o_ref[:, h * hd:(h + 1) * hd] = _nn(p.astype(BF), vh).astype(BF)
        yc = _nn(o_ref[...], wc_ref[...])
        yc_ref[...] = yc.astype(BF)
        g0 = _sig(proj_ref[:, 6 * d:7 * d].astype(F32) + bg_ref[:, 0:d])
        g1 = _sig(proj_ref[:, 7 * d:8 * d].astype(F32) + bg_ref[:, d:2 * d])
        g2 = _sig(proj_ref[:, 8 * d:9 * d].astype(F32) + bg_ref[:, 2 * d:3 * d])
        mg = (g0 * ya + g1 * yb + g2 * yc).astype(BF)
        mg_ref[...] = mg
        x1_ref[...] = x_ref[...] + _nn(mg, wo_ref[...])

    row = lambda w_: pl.BlockSpec((tm, w_), lambda i: (i, 0))
    halo = lambda col: pl.BlockSpec((HALO, d), _prev_halo(tm, col))
    sq = lambda which: _resident((None, None, d, d), (l, which, 0, 0))
    act = jax.ShapeDtypeStruct((t, d), BF)
    return _run(
        body, comm, name=name, grid=(t // tm,),
        in_specs=[row(9 * d), halo(1), halo(2), halo(3), halo(4), row(d),
                  _resident((m, 2 * d), (0, 0)),
                  _resident((None, K_A, d), (l, 0, 0)), _resident((None, K_B, d), (l, 0, 0)),
                  _resident((1, d), (0, 0)), _resident((1, d), (0, 0)), _resident((1, d), (0, 0)),
                  _resident((1, 3 * d), (0, 0)), sq(0), sq(1), sq(2), sq(3)],
        out_specs=[row(d)] * 9,
        out_shape=[jax.ShapeDtypeStruct((t, d), F32)] + [act] * 7 + [jax.ShapeDtypeStruct((t, d), F32)],
        scratch_shapes=[pltpu.VMEM((HALO + tm, d), F32), pltpu.VMEM((HALO + tm, d), F32)],
        args=(proj, proj, proj, proj, proj, x, kv, conv_a, conv_b, cbias, ln_g, ln_b, b_gate, wsq, wsq, wsq, wsq))


def _ffn_down_fwd(up, x1, conv_f, w_down, l, name, comm=None):
    t, d = x1.shape
    f2 = up.shape[1]
    f = f2 // 2
    tm = min(TM_ROW, t)

    def body(up_ref, uph_ref, x1_ref, cw_ref, wd_ref, x2_ref, zf_ref, xx_ref):
        i = pl.program_id(0)
        xx_ref[pl.ds(0, HALO), :] = uph_ref[...].astype(F32) * (i > 0).astype(F32)
        xx_ref[pl.ds(HALO, tm), :] = up_ref[...].astype(F32)
        uc = _causal_taps(xx_ref, cw_ref, K_F, tm)
        gt = uc[:, 0:f]
        zf = (gt * _sig(gt) * uc[:, f:f2]).astype(BF)
        zf_ref[...] = zf
        x2_ref[...] = x1_ref[...] + _nn(zf, wd_ref[...])

    return _run(
        body, comm, name=name, grid=(t // tm,),
        in_specs=[pl.BlockSpec((tm, f2), lambda i: (i, 0)),
                  pl.BlockSpec((HALO, f2), _prev_halo(tm, 0)),
                  pl.BlockSpec((tm, d), lambda i: (i, 0)),
                  _resident((None, K_F, f2), (l, 0, 0)),
                  _resident((None, f, d), (l, 0, 0))],
        out_specs=[pl.BlockSpec((tm, d), lambda i: (i, 0)), pl.BlockSpec((tm, f), lambda i: (i, 0))],
        out_shape=[jax.ShapeDtypeStruct((t, d), F32), jax.ShapeDtypeStruct((t, f), BF)],
        scratch_shapes=[pltpu.VMEM((HALO + tm, f2), F32)],
        args=(up, up, x1, conv_f, w_down))


def _final_loss(x, g, target, name):
    t, d = x.shape
    tm = min(2 * TM_ROW, t)

    def body(x_ref, g_ref, t_ref, dx_ref, loss_ref, dg_ref):
        @pl.when(pl.program_id(0) == 0)
        def _():
            loss_ref[...] = jnp.zeros_like(loss_ref)
            dg_ref[...] = jnp.zeros_like(dg_ref)

        xf = x_ref[...]
        r = lax.rsqrt(jnp.mean(xf * xf, axis=-1, keepdims=True) + EPS)
        xhat = xf * r
        err = xhat * g_ref[...] - t_ref[...]
        loss_ref[...] += (0.5 / d) * jnp.sum(err * err)
        dy = err * (1.0 / d)
        dg_ref[...] += jnp.sum(dy * xhat, axis=0, keepdims=True)
        dxh = dy * g_ref[...]
        dx_ref[...] = r * (dxh - xhat * jnp.mean(dxh * xhat, axis=-1, keepdims=True))

    return _pallas_call(
        body, name=name, grid=(t // tm,),
        in_specs=[pl.BlockSpec((tm, d), lambda i: (i, 0)), pl.BlockSpec((1, d), lambda i: (0, 0)),
                  pl.BlockSpec((tm, d), lambda i: (i, 0))],
        out_specs=[pl.BlockSpec((tm, d), lambda i: (i, 0)), pl.BlockSpec((8, 128), lambda i: (0, 0)),
                   pl.BlockSpec((1, d), lambda i: (0, 0))],
        out_shape=[jax.ShapeDtypeStruct((t, d), F32), jax.ShapeDtypeStruct((8, 128), F32),
                   jax.ShapeDtypeStruct((1, d), F32)],
        compiler_params=_params(1),
    )(x, g, target)


def _ffn_down_bwd(dx2, up, conv_f, w_down, l, name, comm=None):
    t, d = dx2.shape
    f2 = up.shape[1]
    f = f2 // 2
    tm = min(TM_ROW, t)

    def body(dx2_ref, up_ref, uph_ref, cw_ref, wd_ref, duc_ref, dx2b_ref, dcw_ref, xx_ref):
        i = pl.program_id(0)

        @pl.when(i == 0)
        def _():
            dcw_ref[...] = jnp.zeros_like(dcw_ref)

        xx_ref[pl.ds(0, HALO), :] = uph_ref[...].astype(F32) * (i > 0).astype(F32)
        xx_ref[pl.ds(HALO, tm), :] = up_ref[...].astype(F32)
        uc = _causal_taps(xx_ref, cw_ref, K_F, tm)
        gt = uc[:, 0:f]
        sg = _sig(gt)
        dx2b = dx2_ref[...].astype(BF)
        dx2b_ref[...] = dx2b
        dzf = _nt(dx2b, wd_ref[...])
        duc_ref[:, 0:f] = (dzf * uc[:, f:f2] * (sg * (1.0 + gt * (1.0 - sg)))).astype(BF)
        duc_ref[:, f:f2] = (dzf * (gt * sg)).astype(BF)
        _tap_grads(dcw_ref, duc_ref[...].astype(F32), xx_ref, K_F, tm)

    return _run(
        body, comm, name=name, grid=(t // tm,),
        in_specs=[pl.BlockSpec((tm, d), lambda i: (i, 0)),
                  pl.BlockSpec((tm, f2), lambda i: (i, 0)),
                  pl.BlockSpec((HALO, f2), _prev_halo(tm, 0)),
                  _resident((None, K_F, f2), (l, 0, 0)),
                  _resident((None, f, d), (l, 0, 0))],
        out_specs=[pl.BlockSpec((tm, f2), lambda i: (i, 0)), pl.BlockSpec((tm, d), lambda i: (i, 0)),
                   pl.BlockSpec((K_F, f2), lambda i: (0, 0))],
        out_shape=[jax.ShapeDtypeStruct((t, f2), BF), jax.ShapeDtypeStruct((t, d), BF),
                   jax.ShapeDtypeStruct((K_F, f2), F32)],
        scratch_shapes=[pltpu.VMEM((HALO + tm, f2), F32)],
        args=(dx2, up, up, conv_f, w_down))


def _ffn_conv_bwd(duc, conv_f, l, name, comm=None):
    t, f2 = duc.shape
    tm = min(TM_ROW, t)
    n_t = t // tm

    def body(duc_ref, nxt_ref, cw_ref, dup_ref, yy_ref):
        i = pl.program_id(0)
        yy_ref[pl.ds(0, tm), :] = duc_ref[...].astype(F32)
        yy_ref[pl.ds(tm, HALO), :] = nxt_ref[...].astype(F32) * (i < n_t - 1).astype(F32)
        dup_ref[...] = _anticausal_taps(yy_ref, cw_ref, K_F, tm).astype(BF)

    return _run(
        body, comm, name=name, grid=(n_t,),
        in_specs=[pl.BlockSpec((tm, f2), lambda i: (i, 0)),
                  pl.BlockSpec((HALO, f2), _next_halo(tm, t, 0)),
                  _resident((None, K_F, f2), (l, 0, 0))],
        out_specs=[pl.BlockSpec((tm, f2), lambda i: (i, 0))],
        out_shape=[jax.ShapeDtypeStruct((t, f2), BF)],
        scratch_shapes=[pltpu.VMEM((tm + HALO, f2), F32)],
        args=(duc, duc, conv_f))


def _nt_matmul_norm_bwd(dy, w, l, x, g, dres, tk, name, comm=None):
    t, n = dy.shape
    d = x.shape[1]
    tm = min(TM_MM // 2, t)
    n_k = n // tk

    def body(dy_ref, w_ref, x_ref, g_ref, dres_ref, dx_ref, dxb_ref, dg_ref, acc_ref):
        i, k = pl.program_id(0), pl.program_id(1)

        @pl.when((i == 0) & (k == 0))
        def _():
            dg_ref[...] = jnp.zeros_like(dg_ref)

        @pl.when(k == 0)
        def _():
            acc_ref[...] = jnp.zeros_like(acc_ref)

        acc_ref[...] += _nt(dy_ref[...], w_ref[...])

        @pl.when(k == n_k - 1)
        def _():
            xf = x_ref[...]
            r = lax.rsqrt(jnp.mean(xf * xf, axis=-1, keepdims=True) + EPS)
            xhat = xf * r
            dh = acc_ref[...]
            dg_ref[...] += jnp.sum(dh * xhat, axis=0, keepdims=True)
            dxh = dh * g_ref[...]
            dx = dres_ref[...] + r * (dxh - xhat * jnp.mean(dxh * xhat, axis=-1, keepdims=True))
            dx_ref[...] = dx
            dxb_ref[...] = dx.astype(BF)

    return _run(
        body, comm, name=name, grid=(t // tm, n_k),
        in_specs=[pl.BlockSpec((tm, tk), lambda i, k: (i, k)),
                  pl.BlockSpec((None, d, tk), lambda i, k: (l, 0, k)),
                  pl.BlockSpec((tm, d), lambda i, k: (i, 0)),
                  pl.BlockSpec((1, d), lambda i, k: (0, 0)),
                  pl.BlockSpec((tm, d), lambda i, k: (i, 0))],
        out_specs=[pl.BlockSpec((tm, d), lambda i, k: (i, 0)), pl.BlockSpec((tm, d), lambda i, k: (i, 0)),
                   pl.BlockSpec((1, d), lambda i, k: (0, 0))],
        out_shape=[jax.ShapeDtypeStruct((t, d), F32), jax.ShapeDtypeStruct((t, d), BF),
                   jax.ShapeDtypeStruct((1, d), F32)],
        scratch_shapes=[pltpu.VMEM((tm, d), F32)],
        args=(dy, w, x, g, dres))


def _mixer_bwd(dx1b, proj, ya, yb, yc, cb, kv, conv_a, conv_b, ln_g, ln_b, b_gate, wsq, l, name, comm=None):
    t, d = cb.shape
    m = kv.shape[0]
    tm = min(TM_ROW, t)
    hd = d // N_HEADS
    scale = 1.0 / math.sqrt(hd)

    def body(dx1b_ref, proj_ref, gch_ref, vh_ref, u0h_ref, ugh_ref, ya_ref, yb_ref, yc_ref, cb_ref, kv_ref,
             ca_w, cb_w, lng_ref, lnb_ref, bg_ref, wa_ref, wb_ref, wc_ref, wo_ref,
             dpre_ref, dya_ref, dyb_ref, dyc_ref, dkv_ref, dbg_ref, dlng_ref, dlnb_ref, dcbias_ref, dcaw_ref,
             dcbw_ref, xa_ref, xb_ref):
        i = pl.program_id(0)

        @pl.when(i == 0)
        def _():
            for ref in (dkv_ref, dbg_ref, dlng_ref, dlnb_ref, dcbias_ref, dcaw_ref, dcbw_ref):
                ref[...] = jnp.zeros_like(ref)

        _load_branch_inputs(i, proj_ref, gch_ref, vh_ref, u0h_ref, ugh_ref, xa_ref, xb_ref, d)
        dmg = _nt(dx1b_ref[...], wo_ref[...])
        ys = (ya_ref, yb_ref, yc_ref)
        dys = (dya_ref, dyb_ref, dyc_ref)
        for b in range(3):
            gate = _sig(proj_ref[:, (6 + b) * d:(7 + b) * d].astype(F32) + bg_ref[:, b * d:(b + 1) * d])
            dys[b][...] = (gate * dmg).astype(BF)
            dpg = dmg * ys[b][...].astype(F32) * gate * (1.0 - gate)
            dpre_ref[:, (6 + b) * d:(7 + b) * d] = dpg.astype(BF)
            dbg_ref[:, b * d:(b + 1) * d] += jnp.sum(dpg, axis=0, keepdims=True)
        gb = proj_ref[:, 0:d].astype(F32)
        ca = _causal_taps(xa_ref, ca_w, K_A, tm)
        dza = _nt(dya_ref[...], wa_ref[...])
        dpre_ref[:, 0:d] = (dza * ca).astype(BF)
        dpre_ref[:, d:2 * d] = (dza * gb).astype(BF)
        _tap_grads(dcaw_ref, dpre_ref[:, d:2 * d].astype(F32), xa_ref, K_A, tm)
        dpre_ref[:, 2 * d:3 * d] = jnp.zeros((tm, d), BF)
        cbv = cb_ref[...]
        mu = jnp.mean(cbv, axis=-1, keepdims=True)
        dlt = cbv - mu
        rstd = lax.rsqrt(jnp.mean(dlt * dlt, axis=-1, keepdims=True) + EPS)
        xhat = dlt * rstd
        lnb = xhat * lng_ref[...] + lnb_ref[...]
        sg = _sig(lnb)
        dzb = _nt(dyb_ref[...], wb_ref[...])
        dl = dzb * (sg * (1.0 + lnb * (1.0 - sg)))
        dlng_ref[...] += jnp.sum(dl * xhat, axis=0, keepdims=True)
        dlnb_ref[...] += jnp.sum(dl, axis=0, keepdims=True)
        dxh = dl * lng_ref[...]
        dcb = rstd * (dxh - jnp.mean(dxh, axis=-1, keepdims=True)
                      - xhat * jnp.mean(dxh * xhat, axis=-1, keepdims=True))
        dcbias_ref[...] += jnp.sum(dcb, axis=0, keepdims=True)
        dpre_ref[:, 3 * d:4 * d] = dcb.astype(BF)
        _tap_grads(dcbw_ref, dpre_ref[:, 3 * d:4 * d].astype(F32), xb_ref, K_B, tm)
        dpre_ref[:, 4 * d:5 * d] = jnp.zeros((tm, d), BF)
        do = _nt(dyc_ref[...], wc_ref[...]).astype(BF)
        for h in range(N_HEADS):
            qh = proj_ref[:, 5 * d + h * hd:5 * d + (h + 1) * hd]
            kh = kv_ref[:, h * hd:(h + 1) * hd]
            vh = kv_ref[:, d + h * hd:d + (h + 1) * hd]
            doh = do[:, h * hd:(h + 1) * hd]
            p = _softmax_rows(_nt(qh, kh) * scale)
            dp = _nt(doh, vh)
            ds = (p * (dp - jnp.sum(dp * p, axis=-1, keepdims=True)) * scale).astype(BF)
            dpre_ref[:, 5 * d + h * hd:5 * d + (h + 1) * hd] = _nn(ds, kh).astype(BF)
            dkv_ref[:, h * hd:(h + 1) * hd] += _tn(ds, qh)
            dkv_ref[:, d + h * hd:d + (h + 1) * hd] += _tn(p.astype(BF), doh)

    row = lambda w_: pl.BlockSpec((tm, w_), lambda i: (i, 0))
    halo = lambda col: pl.BlockSpec((HALO, d), _prev_halo(tm, col))
    sq = lambda which: _resident((None, None, d, d), (l, which, 0, 0))
    acc = lambda r, c: pl.BlockSpec((r, c), lambda i: (0, 0))
    act = jax.ShapeDtypeStruct((t, d), BF)
    vec = lambda r, c: jax.ShapeDtypeStruct((r, c), F32)
    return _run(
        body, comm, name=name, grid=(t // tm,),
        in_specs=[row(d), row(9 * d), halo(1), halo(2), halo(3), halo(4), row(d), row(d), row(d), row(d),
                  _resident((m, 2 * d), (0, 0)),
                  _resident((None, K_A, d), (l, 0, 0)), _resident((None, K_B, d), (l, 0, 0)),
                  _resident((1, d), (0, 0)), _resident((1, d), (0, 0)), _resident((1, 3 * d), (0, 0)),
                  sq(0), sq(1), sq(2), sq(3)],
        out_specs=[row(9 * d), row(d), row(d), row(d), acc(m, 2 * d), acc(1, 3 * d), acc(1, d), acc(1, d),
                   acc(1, d), acc(K_A, d), acc(K_B, d)],
        out_shape=[jax.ShapeDtypeStruct((t, 9 * d), BF), act, act, act, vec(m, 2 * d), vec(1, 3 * d), vec(1, d),
                   vec(1, d), vec(1, d), vec(K_A, d), vec(K_B, d)],
        scratch_shapes=[pltpu.VMEM((HALO + tm, d), F32), pltpu.VMEM((HALO + tm, d), F32)],
        args=(dx1b, proj, proj, proj, proj, proj, ya, yb, yc, cb, kv, conv_a, conv_b, ln_g, ln_b, b_gate,
              wsq, wsq, wsq, wsq))


def _inproj_conv_bwd(dpre, proj, conv_a, conv_b, l, name, comm=None):
    t, d9 = dpre.shape
    d = d9 // 9
    tm = min(TM_ROW, t)
    n_t = t // tm

    def body(dpre_ref, nxa_ref, nxb_ref, proj_ref, ca_w, cb_w, dproj_ref, ya_ref, yb_ref):
        i = pl.program_id(0)
        keep = (i < n_t - 1).astype(F32)
        ya_ref[pl.ds(0, tm), :] = dpre_ref[:, d:2 * d].astype(F32)
        ya_ref[pl.ds(tm, HALO), :] = nxa_ref[...].astype(F32) * keep
        yb_ref[pl.ds(0, tm), :] = dpre_ref[:, 3 * d:4 * d].astype(F32)
        yb_ref[pl.ds(tm, HALO), :] = nxb_ref[...].astype(F32) * keep
        dproj_ref[:, 0:d] = dpre_ref[:, 0:d]
        dproj_ref[:, 5 * d:9 * d] = dpre_ref[:, 5 * d:9 * d]
        dcv = _anticausal_taps(ya_ref, ca_w, K_A, tm)
        dproj_ref[:, d:2 * d] = (dcv * proj_ref[:, 2 * d:3 * d].astype(F32)).astype(BF)
        dproj_ref[:, 2 * d:3 * d] = (dcv * proj_ref[:, d:2 * d].astype(F32)).astype(BF)
        dub = _anticausal_taps(yb_ref, cb_w, K_B, tm)
        sg = _sig(proj_ref[:, 4 * d:5 * d].astype(F32))
        dproj_ref[:, 3 * d:4 * d] = (dub * sg).astype(BF)
        dproj_ref[:, 4 * d:5 * d] = (dub * proj_ref[:, 3 * d:4 * d].astype(F32) * sg * (1.0 - sg)).astype(BF)

    return _run(
        body, comm, name=name, grid=(n_t,),
        in_specs=[pl.BlockSpec((tm, d9), lambda i: (i, 0)),
                  pl.BlockSpec((HALO, d), _next_halo(tm, t, 1)),
                  pl.BlockSpec((HALO, d), _next_halo(tm, t, 3)),
                  pl.BlockSpec((tm, d9), lambda i: (i, 0)),
                  _resident((None, K_A, d), (l, 0, 0)), _resident((None, K_B, d), (l, 0, 0))],
        out_specs=[pl.BlockSpec((tm, d9), lambda i: (i, 0))],
        out_shape=[jax.ShapeDtypeStruct((t, d9), BF)],
        scratch_shapes=[pltpu.VMEM((tm + HALO, d), F32), pltpu.VMEM((tm + HALO, d), F32)],
        args=(dpre, dpre, dpre, proj, conv_a, conv_b))


def _mem_kv_bwd(dkv, memn, mem, g, w_kv, l, name):
    m, d = mem.shape

    def body(dkv_ref, memn_ref, mem_ref, g_ref, w_ref, dw_ref, dg_ref):
        dkvb = dkv_ref[...].astype(BF)
        dw_ref[...] = _tn(memn_ref[...], dkvb)
        dmemn = _nt(dkvb, w_ref[...])
        xf = mem_ref[...]
        r = lax.rsqrt(jnp.mean(xf * xf, axis=-1, keepdims=True) + EPS)
        dg_ref[...] = jnp.sum(dmemn * (xf * r), axis=0, keepdims=True)

    return _pallas_call(
        body, name=name, grid=(1,),
        in_specs=[pl.BlockSpec((m, 2 * d), lambda i: (0, 0)), pl.BlockSpec((m, d), lambda i: (0, 0)),
                  pl.BlockSpec((m, d), lambda i: (0, 0)), pl.BlockSpec((1, d), lambda i: (0, 0)),
                  pl.BlockSpec((None, d, 2 * d), lambda i: (l, 0, 0))],
        out_specs=[pl.BlockSpec((d, 2 * d), lambda i: (0, 0)), pl.BlockSpec((1, d), lambda i: (0, 0))],
        out_shape=[jax.ShapeDtypeStruct((d, 2 * d), F32), jax.ShapeDtypeStruct((1, d), F32)],
        compiler_params=_params(1),
    )(dkv, memn, mem, g, w_kv)


def _dw_matmul(a, b, tn, name, comm=None):
    t, k = a.shape
    n = b.shape[1]
    tt = min(TT_DW, t)
    while tt * k > DW_LHS_ELEMS and tt % 2 == 0:
        tt //= 2

    def body(a_ref, b_ref, o_ref):
        @pl.when(pl.program_id(1) == 0)
        def _():
            o_ref[...] = jnp.zeros_like(o_ref)

        o_ref[...] += _tn(a_ref[...], b_ref[...])

    return _run(
        body, comm, name=name, grid=(n // tn, t // tt),
        in_specs=[pl.BlockSpec((tt, k), lambda j, s: (s, 0)), pl.BlockSpec((tt, tn), lambda j, s: (s, j))],
        out_specs=[pl.BlockSpec((k, tn), lambda j, s: (0, j))],
        out_shape=[jax.ShapeDtypeStruct((k, n), F32)],
        args=(a, b))


class _GradReduce:
    def __init__(self, l, grads, core):
        self.l, self.core, self.names = l, core, tuple(grads)
        self.views = {n: _halves_view(n, g) for n, g in grads.items()}
        self.got, self.sums, self.pieces = {}, {}, {}

    def swap_program(self):
        return _swap_program([self.views[n] for n in self.names])

    def swapped(self, outs):
        self.got = dict(zip(self.names, outs))
        self.sums = {n: _add_halves(self.views[n], self.got[n], self.core, f"add_halves_{n}_{self.l}")
                     for n in self.names}

    def scatter_program(self, names=None):
        names = self.names if names is None else names
        return _scatter_program([self.sums[n] for n in names], names)

    def scattered(self, outs, names=None):
        self.pieces.update(zip(self.names if names is None else names, outs))


def _step(x, mem, target, sh, conv_sh, small, core):
    depth = sh["w_in"].shape[0]
    d = x.shape[1]
    f2 = sh["w_up"].shape[2] * N_CHIPS
    tn_in, tn_up = 9 * d // 4, f2 // 2
    tk_in, tk_up = 9 * d // 2, f2 // 2
    tn_dw_in = min(1024, d)
    tn_dw_up = f2 // 11 if f2 % (11 * 128) == 0 and f2 // 11 >= 128 else f2
    row = lambda v: v.reshape(1, -1)
    one = lambda a: a[None]

    w_in, cab, cf = _run_comm(_gather_program([sh["w_in"]] + conv_sh, ("col", "small", "small"), (0, None, None)),
                              "gather_first")
    saved = []
    for l in range(depth):
        conv = dict(a=cab[l:l + 1, :K_A], b=cab[l:l + 1, K_A:], f=cf[l:l + 1])
        (h, proj), (wsq, w_kv) = _norm_matmul(
            x, row(small["norm_mix_g"][l]), one(w_in), 0, tn_in, f"in_proj_{l}",
            _gather_program([sh["wsq"], sh["w_kv"]], ("row", "col"), (l, l)))
        memn, kv = _mem_kv(mem, row(small["norm_mem_g"][l]), one(w_kv), 0, f"mem_kv_{l}")
        (x1, za, zb, o, ya, yb, yc, mg, cb), (w_up, w_down) = _mixer_fwd(
            proj, x, kv, conv["a"], conv["b"], row(small["conv_b_bias"][l]), row(small["ln_b_g"][l]),
            row(small["ln_b_b"][l]), row(small["b_gate"][l]), one(wsq), 0, f"mixer_fwd_{l}",
            _gather_program([sh["w_up"], sh["w_down"]], ("col", "row"), (l, l)))
        more = l + 1 < depth
        nxt = _gather_program([sh["w_in"]], ("col",), (l + 1,), (0, 2)) if more else None
        (h2, up), w_in_next = _norm_matmul(x1, row(small["norm_ffn_g"][l]), one(w_up), 0, tn_up, f"up_proj_{l}", nxt)
        nxt = _gather_program([sh["w_in"]], ("col",), (l + 1,), (1, 2), w_in_next) if more else None
        (x2, zf), w_in_next = _ffn_down_fwd(up, x1, conv["f"], one(w_down), 0, f"ffn_down_fwd_{l}", nxt)
        saved.append(dict(x=x, x1=x1, memn=memn, kv=kv, h=h, proj=proj, za=za, zb=zb, o=o, ya=ya, yb=yb, yc=yc,
                          mg=mg, cb=cb, h2=h2, up=up, zf=zf, w_in=one(w_in), wsq=one(wsq), w_kv=one(w_kv),
                          w_up=one(w_up), w_down=one(w_down), conv=conv))
        x = x2
        w_in = w_in_next[0] if w_in_next else None
    dx, loss, dg_final = _final_loss(x, row(small["norm_final_g"]), target, "final_loss")
    sgrads, pieces = [None] * depth, [None] * depth
    above = None
    rest = tuple(n for n in BIG if n != "w_in")
    for l in reversed(range(depth)):
        s = saved[l]
        conv = s["conv"]
        bottom = l == 0
        (duc, dx2b, dconv_f), got = _ffn_down_bwd(dx, s["up"], conv["f"], s["w_down"], 0, f"ffn_down_bwd_{l}",
                                                  above and above.swap_program())
        if above:
            above.swapped(got)
        (dup,), _ = _ffn_conv_bwd(duc, conv["f"], 0, f"ffn_conv_bwd_{l}")
        (dx1, dx1b, dg_ffn), got = _nt_matmul_norm_bwd(
            dup, s["w_up"], 0, s["x1"], row(small["norm_ffn_g"][l]), dx, tk_up, f"up_proj_bwd_{l}",
            above and above.scatter_program(("w_in",)))
        if above:
            above.scattered(got, ("w_in",))
        grads = dict(w_up=_dw_matmul(s["h2"], dup, tn_dw_up, f"dw_up_{l}")[0][0],
                     w_down=_dw_matmul(s["zf"], dx2b, d, f"dw_down_{l}")[0][0])
        ffn = _GradReduce(l, grads, core) if bottom else None
        (dpre, dya, dyb, dyc, dkv, dbg, dlng, dlnb, dcbias, dconv_a, dconv_b), got = _mixer_bwd(
            dx1b, s["proj"], s["ya"], s["yb"], s["yc"], s["cb"], s["kv"], conv["a"], conv["b"],
            row(small["ln_b_g"][l]), row(small["ln_b_b"][l]), row(small["b_gate"][l]), s["wsq"], 0,
            f"mixer_bwd_{l}", _join([above and above.scatter_program(rest), ffn and ffn.swap_program()]))
        if above:
            above.scattered(got[:len(rest)], rest)
            pieces[above.l] = above.pieces
            got = got[len(rest):]
        if ffn:
            ffn.swapped(got)
        dw_kv, dg_mem = _mem_kv_bwd(dkv, s["memn"], mem, row(small["norm_mem_g"][l]), s["w_kv"], 0,
                                    f"mem_kv_bwd_{l}")
        mix_grads = dict(w_a_out=_dw_matmul(s["za"], dya, d, f"dw_a_out_{l}")[0][0],
                         w_b_out=_dw_matmul(s["zb"], dyb, d, f"dw_b_out_{l}")[0][0],
                         w_att_out=_dw_matmul(s["o"], dyc, d, f"dw_att_out_{l}")[0][0],
                         w_o=_dw_matmul(s["mg"], dx1b, d, f"dw_o_{l}")[0][0], w_kv=dw_kv)
        mix = _GradReduce(l, mix_grads, core) if bottom else None
        (dproj,), got = _inproj_conv_bwd(dpre, s["proj"], conv["a"], conv["b"], 0, f"inproj_conv_bwd_{l}",
                                         _join([ffn and ffn.scatter_program(), mix and mix.swap_program()]))
        if bottom:
            ffn.scattered(got[:len(ffn.names)])
            mix.swapped(got[len(ffn.names):])
        in_grads = dict(w_in=_dw_matmul(s["h"], dproj, tn_dw_in, f"dw_in_{l}")[0][0])
        inp = _GradReduce(l, in_grads, core) if bottom else None
        (dx0, _, dg_mix), got = _nt_matmul_norm_bwd(
            dproj, s["w_in"], 0, s["x"], row(small["norm_mix_g"][l]), dx1, tk_in, f"in_proj_bwd_{l}",
            _join([mix and mix.scatter_program(), inp and inp.swap_program()]))
        if bottom:
            mix.scattered(got[:len(mix.names)])
            inp.swapped(got[len(mix.names):])
            inp.scattered(_run_comm(inp.scatter_program(), f"scatter_w_in_{l}"))
            pieces[l] = {**ffn.pieces, **mix.pieces, **inp.pieces}
        else:
            above = _GradReduce(l, {**grads, **mix_grads, **in_grads}, core)
        sgrads[l] = dict(norm_mix_g=dg_mix, norm_mem_g=dg_mem, b_gate=dbg, conv_b_bias=dcbias, ln_b_g=dlng,
                         ln_b_b=dlnb, norm_ffn_g=dg_ffn, conv_a_w=dconv_a, conv_b_w=dconv_b, conv_ffn_w=dconv_f)
        dx = dx0
    return loss, dx, sgrads, dg_final, pieces


BIG = ("w_in", "w_a_out", "w_b_out", "w_att_out", "w_o", "w_kv", "w_up", "w_down")
COL = ("w_in", "w_kv", "w_up")
SQUARES = ("w_a_out", "w_b_out", "w_att_out", "w_o")
ANY = pl.BlockSpec(memory_space=pl.ANY)


def _place():
    x, y, c = lax.axis_index("x"), lax.axis_index("y"), lax.axis_index("c")
    chips = [(1 - x, y), (x, 1 - y), (1 - x, 1 - y)]
    return x, y, c, 2 * x + y, chips


def _remote(src, dst, send_sem, recv_sem, dev):
    return pltpu.make_async_remote_copy(src_ref=src, dst_ref=dst, send_sem=send_sem, recv_sem=recv_sem,
                                        device_id=dev, device_id_type=MESH)


def _window(ref, rows, cols):
    return ref.at[(slice(None),) * (len(ref.shape) - 2) + (rows, cols)]


def _row_tile(rows, cols, unit=16, limit=1 << 20):
    best = unit
    for tr in range(unit, rows + 1, unit):
        if rows % tr == 0 and tr * cols <= limit:
            best = tr
    return best


def _cast_shard(w, name):
    depth, k, n = w.shape

    def body(w_ref, o_ref):
        o_ref[...] = w_ref[...].astype(BF)

    return _pallas_call(
        body, name=name, grid=(depth,),
        in_specs=[pl.BlockSpec((None, k, n), lambda l: (l, 0, 0))],
        out_specs=pl.BlockSpec((None, k, n), lambda l: (l, 0, 0)),
        out_shape=jax.ShapeDtypeStruct((depth, k, n), BF),
        compiler_params=_params(1),
    )(w)


def _cast_squares(ws, name):
    depth, k, n = ws[0].shape

    def body(a_ref, b_ref, c_ref, d_ref, o_ref):
        for i, ref in enumerate((a_ref, b_ref, c_ref, d_ref)):
            o_ref[i] = ref[...].astype(BF)

    return _pallas_call(
        body, name=name, grid=(depth,),
        in_specs=[pl.BlockSpec((None, k, n), lambda l: (l, 0, 0))] * 4,
        out_specs=pl.BlockSpec((None, 4, k, n), lambda l: (l, 0, 0, 0)),
        out_shape=jax.ShapeDtypeStruct((depth, 4, k, n), BF),
        compiler_params=_params(1),
    )(*ws)


def _gather_program(shards, kinds, layers, part=(0, 1), into=None):
    n_t = len(shards)
    index, count = part

    def full_shape(s, kind, layer):
        shp = list(s.shape if layer is None else s.shape[1:])
        shp[-2 if kind == "row" else -1] *= N_CHIPS
        return tuple(shp)

    def run(phase, ins, full, sems):
        shard = [r if l is None else r.at[l] for r, l in zip(ins, layers)]
        ici_send, ici_recv, sib_send, sib_recv, loc_sem = sems
        x, y, c, me, chips = _place()
        sibling = (x, y, 1 - c)

        def src_part(i, half):
            s, kind = shard[i], kinds[i]
            if kind == "small":
                return s
            r = s.shape[-2] // (2 * count)
            return _window(s, pl.ds(pl.multiple_of((half * count + index) * r, 16), r), slice(None))

        def dst_part(i, chip, half):
            f, s, kind = full[i], shard[i], kinds[i]
            rows, cols = s.shape[-2], s.shape[-1]
            if kind == "small":
                return _window(f, slice(None), pl.ds(pl.multiple_of(chip * cols, 128), cols))
            r = rows // (2 * count)
            at = (half * count + index) * r
            if kind == "col":
                return _window(f, pl.ds(pl.multiple_of(at, 16), r), pl.ds(pl.multiple_of(chip * cols, 128), cols))
            return _window(f, pl.ds(pl.multiple_of(chip * rows + at, 16), r), slice(None))

        def own_slot(i):
            f, s, kind = full[i], shard[i], kinds[i]
            rows, cols = s.shape[-2], s.shape[-1]
            if kind == "row":
                return _window(f, pl.ds(pl.multiple_of(me * rows, 16), rows), slice(None))
            return _window(f, slice(None), pl.ds(pl.multiple_of(me * cols, 128), cols))

        local = [pltpu.make_async_copy(shard[i], own_slot(i), loc_sem.at[i]) for i in range(n_t)] if index == 0 else []
        sends = []
        for i in range(n_t):
            for j, chip in enumerate(chips):
                sends.append(_remote(src_part(i, c), dst_part(i, me, c), ici_send.at[3 * i + j],
                                     ici_recv.at[3 * i + j], (*chip, c)))
        if phase == "start":
            for cp in local + sends:
                cp.start()
            return
        passed = []
        for i in range(n_t):
            for j, chip in enumerate(chips):
                k = 2 * chip[0] + chip[1]
                landed = dst_part(i, k, c)
                _remote(landed, landed, ici_send.at[3 * i + j], ici_recv.at[3 * i + j], (*chip, c)).wait_recv()
                if kinds[i] != "small":
                    cp = _remote(landed, landed, sib_send.at[3 * i + j], sib_recv.at[3 * i + j], sibling)
                    cp.start()
                    passed.append(cp)
        for i in range(n_t):
            if kinds[i] == "small":
                continue
            for j, chip in enumerate(chips):
                k = 2 * chip[0] + chip[1]
                other = dst_part(i, k, 1 - c)
                _remote(other, other, sib_send.at[3 * i + j], sib_recv.at[3 * i + j], sibling).wait_recv()
        for cp in sends + passed:
            cp.wait_send()
        for cp in local:
            cp.wait()

    outs = [jax.ShapeDtypeStruct(full_shape(s, k, l), s.dtype) for s, k, l in zip(shards, kinds, layers)]
    sems = [pltpu.SemaphoreType.DMA((3 * n_t,))] * 4 + [pltpu.SemaphoreType.DMA((n_t,))]
    return _Comm(list(shards) + list(into or []), outs, sems, functools.partial(run, "start"),
                 functools.partial(run, "finish"), {n_t + i: i for i in range(len(into or []))})


def _all_reduce_small(part, name):
    r, n = part.shape

    def body(in_ref, out_ref, gather_ref, send_sems, recv_sems):
        x, y, c, _, _ = _place()
        me = 4 * x + 2 * y + c
        gather_ref[me] = in_ref[...]
        sends = []
        for k in range(1, N_DEV):
            peer = (me + k) % N_DEV
            sends.append(_remote(in_ref, gather_ref.at[me], send_sems.at[k - 1], recv_sems.at[k - 1],
                                 (peer // 4, (peer // 2) % 2, peer % 2)))
        for cp in sends:
            cp.start()
        for k in range(1, N_DEV):
            origin = (me + N_DEV - k) % N_DEV
            _remote(in_ref, gather_ref.at[origin], send_sems.at[k - 1], recv_sems.at[k - 1],
                    (x, y, c)).wait_recv()
        for cp in sends:
            cp.wait_send()
        total = gather_ref[0]
        for dev in range(1, N_DEV):
            total = total + gather_ref[dev]
        out_ref[...] = total

    vm = pl.BlockSpec(memory_space=pltpu.VMEM)
    return _pallas_call(
        body, name=name, in_specs=[vm], out_specs=vm, out_shape=jax.ShapeDtypeStruct((r, n), F32),
        scratch_shapes=[pltpu.VMEM((N_DEV, r, n), F32), pltpu.SemaphoreType.DMA((N_DEV - 1,)),
                        pltpu.SemaphoreType.DMA((N_DEV - 1,))],
        compiler_params=pltpu.CompilerParams(vmem_limit_bytes=VMEM_LIMIT),
    )(part)


def _halves_view(name, dw):
    k, n = dw.shape
    s = 1 if name in COL else N_CHIPS
    return dw.reshape(s, 2, k // (2 * s), n)


def _swap_program(views):
    n_t = len(views)

    def run(phase, src, dst, sems):
        send_sems, recv_sems = sems
        x, y, c, _, _ = _place()
        copies = [_remote(src[i].at[:, 1 - c], dst[i], send_sems.at[i], recv_sems.at[i], (x, y, 1 - c))
                  for i in range(n_t)]
        for cp in copies:
            if phase == "start":
                cp.start()
            else:
                cp.wait()

    outs = [jax.ShapeDtypeStruct((v.shape[0],) + v.shape[2:], F32) for v in views]
    sems = [pltpu.SemaphoreType.DMA((n_t,)), pltpu.SemaphoreType.DMA((n_t,))]
    return _Comm(views, outs, sems, functools.partial(run, "start"), functools.partial(run, "finish"))


def _add_halves(view, got, core, name):
    s, _, r, n = view.shape
    tr = _row_tile(r, n)

    def body(c_ref, a_ref, b_ref, o_ref):
        o_ref[...] = (a_ref[...] + b_ref[...]).astype(BF)

    return _pallas_call(
        body, name=name,
        grid_spec=pltpu.PrefetchScalarGridSpec(
            num_scalar_prefetch=1, grid=(s, r // tr),
            in_specs=[pl.BlockSpec((None, None, tr, n), lambda i, j, c_ref: (i, c_ref[0], j, 0)),
                      pl.BlockSpec((None, tr, n), lambda i, j, c_ref: (i, j, 0))],
            out_specs=pl.BlockSpec((None, tr, n), lambda i, j, c_ref: (i, j, 0))),
        out_shape=jax.ShapeDtypeStruct((s, r, n), BF),
        compiler_params=_params(2),
    )(core, view, got)


def _scatter_program(sums, names):
    n_t = len(sums)

    def piece_shape(nm, p):
        s, r, n = p.shape
        return (r, n // N_CHIPS) if nm in COL else (r, n)

    def run(phase, src, dst, sems):
        ici_send, ici_recv, sib_send, sib_recv, loc_sem = sems
        x, y, c, me, chips = _place()
        sibling = (x, y, 1 - c)

        def piece(i, chip):
            if names[i] in COL:
                cw = src[i].shape[2] // N_CHIPS
                return src[i].at[0, :, pl.ds(pl.multiple_of(chip * cw, 128), cw)]
            return src[i].at[chip]

        local = [pltpu.make_async_copy(piece(i, me), dst[i].at[c, me], loc_sem.at[i]) for i in range(n_t)]
        sends = []
        for i in range(n_t):
            sends.append(_remote(piece(i, me), dst[i].at[c, me], sib_send.at[4 * i + 3], sib_recv.at[4 * i + 3],
                                 sibling))
            for j, chip in enumerate(chips):
                k = 2 * chip[0] + chip[1]
                sends.append(_remote(piece(i, k), dst[i].at[c, me], ici_send.at[3 * i + j], ici_recv.at[3 * i + j],
                                     (*chip, c)))
        if phase == "start":
            for cp in local + sends:
                cp.start()
            return
        passed = []
        for i in range(n_t):
            for j, chip in enumerate(chips):
                k = 2 * chip[0] + chip[1]
                landed = dst[i].at[c, k]
                _remote(landed, landed, ici_send.at[3 * i + j], ici_recv.at[3 * i + j], (*chip, c)).wait_recv()
                cp = _remote(landed, landed, sib_send.at[4 * i + j], sib_recv.at[4 * i + j], sibling)
                cp.start()
                passed.append(cp)
        for i in range(n_t):
            other = dst[i].at[1 - c, me]
            _remote(other, other, sib_send.at[4 * i + 3], sib_recv.at[4 * i + 3], sibling).wait_recv()
            for j, chip in enumerate(chips):
                k = 2 * chip[0] + chip[1]
                other = dst[i].at[1 - c, k]
                _remote(other, other, sib_send.at[4 * i + j], sib_recv.at[4 * i + j], sibling).wait_recv()
        for cp in sends + passed:
            cp.wait_send()
        for cp in local:
            cp.wait()

    outs = [jax.ShapeDtypeStruct((2, N_CHIPS) + piece_shape(nm, p), BF) for nm, p in zip(names, sums)]
    sems = [pltpu.SemaphoreType.DMA((3 * n_t,))] * 2 + [pltpu.SemaphoreType.DMA((4 * n_t,))] * 2 + [
        pltpu.SemaphoreType.DMA((n_t,))]
    return _Comm(sums, outs, sems, functools.partial(run, "start"), functools.partial(run, "finish"))


def _adamw(w, g, m, v):
    m = ADAM_B1 * m + (1.0 - ADAM_B1) * g
    v = ADAM_B2 * v + (1.0 - ADAM_B2) * (g * g)
    m_hat = m / (1.0 - ADAM_B1 ** ADAM_STEP)
    v_hat = v / (1.0 - ADAM_B2 ** ADAM_STEP)
    return -ADAM_LR * (m_hat / (jnp.sqrt(v_hat) + ADAM_EPS) + ADAM_WD * w), m, v


def _adam_shard(pieces, w, m, v, l, prev, name):
    depth, rows, cw = w.shape
    hr = rows // 2
    tr = _row_tile(hr, cw, limit=1 << 18)
    n_i = hr // tr

    def body(*refs):
        z_ref, w_ref, m_ref, v_ref = refs[:4]
        g_ref, d_ref, nm_ref, nv_ref = refs[-4:]
        g = z_ref[0].astype(F32)
        for k in range(1, N_CHIPS):
            g = g + z_ref[k].astype(F32)
        g_ref[...] = g
        d_ref[...], nm_ref[...], nv_ref[...] = _adamw(w_ref[...], g, m_ref[...], v_ref[...])

    par = pl.BlockSpec((None, tr, cw), lambda h, i: (l, h * n_i + i, 0))
    out = jax.ShapeDtypeStruct((depth, rows, cw), F32)
    extra = [] if prev is None else list(prev)
    return _pallas_call(
        body, name=name, grid=(2, n_i),
        in_specs=[pl.BlockSpec((None, N_CHIPS, tr, cw), lambda h, i: (h, 0, i, 0)), par, par, par] + [ANY] * len(extra),
        out_specs=[par] * 4, out_shape=[out] * 4,
        input_output_aliases={4 + k: k for k in range(len(extra))},
        compiler_params=_params(2),
    )(pieces, w, m, v, *extra)


def _adam_small(g, w, m, v, name):
    def body(g_ref, w_ref, m_ref, v_ref, d_ref, nm_ref, nv_ref):
        d_ref[...], nm_ref[...], nv_ref[...] = _adamw(w_ref[...], g_ref[...], m_ref[...], v_ref[...])

    vm = pl.BlockSpec(memory_space=pltpu.VMEM)
    out = jax.ShapeDtypeStruct(g.shape, F32)
    return _pallas_call(body, name=name, in_specs=[vm] * 4, out_specs=[vm] * 3, out_shape=[out] * 3)(g, w, m, v)


WEIGHTS = ("norm_mix_g", "norm_mem_g", "w_in", "b_gate", "conv_a_w", "w_a_out", "conv_b_w", "conv_b_bias", "ln_b_g",
           "ln_b_b", "w_b_out", "w_kv", "w_att_out", "w_o", "norm_ffn_g", "w_up", "conv_ffn_w", "w_down",
           "norm_final_g")
REPLICATED = ("norm_mix_g", "norm_mem_g", "b_gate", "conv_b_bias", "ln_b_g", "ln_b_b", "norm_ffn_g")
CONVS = ("conv_a_w", "conv_b_w", "conv_ffn_w")
PACK_WIDTH = 1024


def _pack(arrays):
    flat = jnp.concatenate([a.reshape(-1) for a in arrays])
    size = -(-flat.shape[0] // (8 * PACK_WIDTH)) * (8 * PACK_WIDTH)
    return jnp.pad(flat, (0, size - flat.shape[0])).reshape(-1, PACK_WIDTH)


def _unpack(packed, shapes):
    flat = packed.reshape(-1)
    out, at = [], 0
    for shp in shapes:
        n = math.prod(shp)
        out.append(flat[at:at + n].reshape(shp))
        at += n
    return out


def kernel(x, mem, norm_mix_g, norm_mem_g, w_in, b_gate, conv_a_w, w_a_out, conv_b_w, conv_b_bias, ln_b_g, ln_b_b, w_b_out, w_kv, w_att_out, w_o, norm_ffn_g, w_up, conv_ffn_w, w_down, norm_final_g, loss_target, m_norm_mix_g, m_norm_mem_g, m_w_in, m_b_gate, m_conv_a_w, m_w_a_out, m_conv_b_w, m_conv_b_bias, m_ln_b_g, m_ln_b_b, m_w_b_out, m_w_kv, m_w_att_out, m_w_o, m_norm_ffn_g, m_w_up, m_conv_ffn_w, m_w_down, m_norm_final_g, v_norm_mix_g, v_norm_mem_g, v_w_in, v_b_gate, v_conv_a_w, v_w_a_out, v_conv_b_w, v_conv_b_bias, v_ln_b_g, v_ln_b_b, v_w_b_out, v_w_kv, v_w_att_out, v_w_o, v_norm_ffn_g, v_w_up, v_conv_ffn_w, v_w_down, v_norm_final_g):
    w = dict(norm_mix_g=norm_mix_g, norm_mem_g=norm_mem_g, w_in=w_in, b_gate=b_gate, conv_a_w=conv_a_w,
             w_a_out=w_a_out, conv_b_w=conv_b_w, conv_b_bias=conv_b_bias, ln_b_g=ln_b_g, ln_b_b=ln_b_b,
             w_b_out=w_b_out, w_kv=w_kv, w_att_out=w_att_out, w_o=w_o, norm_ffn_g=norm_ffn_g, w_up=w_up,
             conv_ffn_w=conv_ffn_w, w_down=w_down, norm_final_g=norm_final_g)
    mom = dict(norm_mix_g=m_norm_mix_g, norm_mem_g=m_norm_mem_g, w_in=m_w_in, b_gate=m_b_gate, conv_a_w=m_conv_a_w,
               w_a_out=m_w_a_out, conv_b_w=m_conv_b_w, conv_b_bias=m_conv_b_bias, ln_b_g=m_ln_b_g, ln_b_b=m_ln_b_b,
               w_b_out=m_w_b_out, w_kv=m_w_kv, w_att_out=m_w_att_out, w_o=m_w_o, norm_ffn_g=m_norm_ffn_g,
               w_up=m_w_up, conv_ffn_w=m_conv_ffn_w, w_down=m_w_down, norm_final_g=m_norm_final_g)
    var = dict(norm_mix_g=v_norm_mix_g, norm_mem_g=v_norm_mem_g, w_in=v_w_in, b_gate=v_b_gate, conv_a_w=v_conv_a_w,
               w_a_out=v_w_a_out, conv_b_w=v_conv_b_w, conv_b_bias=v_conv_b_bias, ln_b_g=v_ln_b_g, ln_b_b=v_ln_b_b,
               w_b_out=v_w_b_out, w_kv=v_w_kv, w_att_out=v_w_att_out, w_o=v_w_o, norm_ffn_g=v_norm_ffn_g,
               w_up=v_w_up, conv_ffn_w=v_conv_ffn_w, w_down=v_w_down, norm_final_g=v_norm_final_g)
    depth = w_in.shape[0]
    chip = 2 * lax.axis_index("x") + lax.axis_index("y")
    core = lax.axis_index("c").astype(jnp.int32).reshape(1)

    sh = dict(w_in=_cast_shard(w_in, "cast_w_in"), wsq=_cast_squares([w[n] for n in SQUARES], "cast_squares"),
              w_kv=_cast_shard(w_kv, "cast_w_kv"), w_up=_cast_shard(w_up, "cast_w_up"),
              w_down=_cast_shard(w_down, "cast_w_down"))
    conv_sh = [jnp.concatenate([conv_a_w, conv_b_w], axis=1), conv_ffn_w]
    small = {n: w[n] for n in REPLICATED + ("norm_final_g",)}

    loss, dx, sgrads, dg_final, pieces = _step(x[0], mem[0], loss_target[0], sh, conv_sh, small, core)

    res = {n: None for n in BIG}
    for l in reversed(range(depth)):
        for n in BIG:
            res[n] = _adam_shard(pieces[l][n], w[n], mom[n], var[n], l, res[n], f"adam_{n}_{l}")

    per_layer = REPLICATED + CONVS
    parts = [sgrads[l][n] for l in range(depth) for n in per_layer] + [dg_final]
    total = _all_reduce_small(_pack(parts), "all_reduce_small")
    shapes = [sgrads[l][n].shape for l in range(depth) for n in per_layer] + [dg_final.shape]
    summed = _unpack(total, shapes)
    g_small = {}
    for k, n in enumerate(per_layer):
        full = jnp.stack([summed[l * len(per_layer) + k] for l in range(depth)])
        if n in CONVS:
            cols = w[n].shape[-1]
            full = lax.dynamic_slice_in_dim(full, chip * cols, cols, axis=2)
        g_small[n] = full.reshape(w[n].shape)
    g_small["norm_final_g"] = summed[-1].reshape(norm_final_g.shape)
    names = per_layer + ("norm_final_g",)
    d_p, m_p, v_p = _adam_small(_pack([g_small[n] for n in names]), _pack([w[n] for n in names]),
                                _pack([mom[n] for n in names]), _pack([var[n] for n in names]), "adam_small")
    shp = [w[n].shape for n in names]
    for n, dl, nm, nv in zip(names, _unpack(d_p, shp), _unpack(m_p, shp), _unpack(v_p, shp)):
        res[n] = (g_small[n], dl, nm, nv)

    loss = lax.psum(loss[0, 0], ("x", "y", "c"))
    return (loss, dx.reshape(x.shape), *[res[n][0] for n in WEIGHTS], *[res[n][1] for n in WEIGHTS],
            *[res[n][2] for n in WEIGHTS], *[res[n][3] for n in WEIGHTS])
```

```python
import functools
import math

import jax
import jax.numpy as jnp
from jax import lax
from jax.experimental import pallas as pl
from jax.experimental.pallas import tpu as pltpu

F32 = jnp.float32
BF = jnp.bfloat16
EPS = 1e-6
N_HEADS = 4
K_A, K_B, K_F = 3, 31, 3
ADAM_LR, ADAM_B1, ADAM_B2, ADAM_EPS, ADAM_WD, ADAM_STEP = 0.001, 0.9, 0.999, 1e-08, 0.01, 10
N_CHIPS = 4
N_DEV = 8
HALO = 32
TM_ROW = 256
TM_MM = 1024
TT_DW = 4096
DW_LHS_ELEMS = 4 * 1024 * 1024
TR_EW = 128
VMEM_LIMIT = 56 * 1024 * 1024
MESH = pl.DeviceIdType.MESH
_pallas_call = pl.pallas_call


def _params(n_axes):
    return pltpu.CompilerParams(dimension_semantics=("arbitrary",) * n_axes, vmem_limit_bytes=VMEM_LIMIT)


def _resident(shape, index):
    return pl.BlockSpec(shape, lambda *_: index, pipeline_mode=pl.Buffered(1))


def _sig(x):
    return 1.0 / (1.0 + jnp.exp(-x))


def _nt(a, b):
    return lax.dot_general(a, b, (((1,), (1,)), ((), ())), preferred_element_type=F32)


def _tn(a, b):
    return lax.dot_general(a, b, (((0,), (0,)), ((), ())), preferred_element_type=F32)


def _nn(a, b):
    return jnp.dot(a, b, preferred_element_type=F32)


class _Comm:
    def __init__(self, inputs, out_shape, scratch, start, finish, aliases=None):
        self.inputs, self.out_shape, self.scratch = list(inputs), list(out_shape), list(scratch)
        self.start, self.finish = start, finish
        self.aliases = dict(aliases or {})


def _join(programs):
    programs = [p for p in programs if p is not None]
    if not programs:
        return None

    def split(seq, counts):
        parts, at = [], 0
        for n in counts:
            parts.append(seq[at:at + n])
            at += n
        return parts

    n_in = [len(p.inputs) for p in programs]
    n_out = [len(p.out_shape) for p in programs]
    n_s = [len(p.scratch) for p in programs]

    def phase(which):
        def run(ins, outs, sems):
            for p, i, o, s in zip(programs, split(ins, n_in), split(outs, n_out), split(sems, n_s)):
                getattr(p, which)(i, o, s)
        return run

    aliases, in_at, out_at = {}, 0, 0
    for p, i, o in zip(programs, n_in, n_out):
        aliases.update({in_at + a: out_at + b for a, b in p.aliases.items()})
        in_at, out_at = in_at + i, out_at + o
    return _Comm([a for p in programs for a in p.inputs], [a for p in programs for a in p.out_shape],
                 [a for p in programs for a in p.scratch], phase("start"), phase("finish"), aliases)


def _run(body, comm, *, name, grid, in_specs, out_specs, out_shape, args, scratch_shapes=()):
    n_axes = len(grid)
    if comm is None:
        outs = _pallas_call(body, name=name, grid=grid, in_specs=list(in_specs), out_specs=list(out_specs),
                            out_shape=list(out_shape), scratch_shapes=list(scratch_shapes),
                            compiler_params=_params(n_axes))(*args)
        return list(outs), []
    counts = (len(in_specs), len(comm.inputs), len(out_specs), len(comm.out_shape), len(scratch_shapes),
              len(comm.scratch))

    def hosted(*refs):
        parts, at = [], 0
        for n in counts:
            parts.append(refs[at:at + n])
            at += n
        ins, c_ins, outs, c_outs, scr, c_sems = parts
        ids = [pl.program_id(a) for a in range(n_axes)]
        first = functools.reduce(jnp.logical_and, [i == 0 for i in ids])
        last = functools.reduce(jnp.logical_and, [i == g - 1 for i, g in zip(ids, grid)])

        @pl.when(first)
        def _():
            comm.start(c_ins, c_outs, c_sems)

        body(*ins, *outs, *scr)

        @pl.when(last)
        def _():
            comm.finish(c_ins, c_outs, c_sems)

    any_spec = pl.BlockSpec(memory_space=pl.ANY)
    res = _pallas_call(
        hosted, name=name, grid=grid, in_specs=list(in_specs) + [any_spec] * counts[1],
        out_specs=list(out_specs) + [any_spec] * counts[3], out_shape=list(out_shape) + comm.out_shape,
        scratch_shapes=list(scratch_shapes) + comm.scratch, compiler_params=_params(n_axes),
        input_output_aliases={counts[0] + a: counts[2] + b for a, b in comm.aliases.items()},
    )(*args, *comm.inputs)
    return list(res[:counts[2]]), list(res[counts[2]:])


def _run_comm(comm, name):
    n_in, n_out = len(comm.inputs), len(comm.out_shape)

    def body(*refs):
        ins, outs, sems = refs[:n_in], refs[n_in:n_in + n_out], refs[n_in + n_out:]
        comm.start(ins, outs, sems)
        comm.finish(ins, outs, sems)

    any_spec = pl.BlockSpec(memory_space=pl.ANY)
    return list(_pallas_call(body, name=name, in_specs=[any_spec] * n_in, out_specs=[any_spec] * n_out,
                             out_shape=comm.out_shape, scratch_shapes=comm.scratch,
                             input_output_aliases=comm.aliases)(*comm.inputs))


def _causal_taps(xx_ref, w_ref, k_taps, tm):
    base = HALO - k_taps + 1
    acc = w_ref[0:1, :] * xx_ref[pl.ds(base, tm), :]
    for k in range(1, k_taps):
        acc = acc + w_ref[k:k + 1, :] * xx_ref[pl.ds(base + k, tm), :]
    return acc


def _anticausal_taps(yy_ref, w_ref, k_taps, tm):
    acc = w_ref[k_taps - 1:k_taps, :] * yy_ref[pl.ds(0, tm), :]
    for k in range(k_taps - 1):
        acc = acc + w_ref[k:k + 1, :] * yy_ref[pl.ds(k_taps - 1 - k, tm), :]
    return acc


def _tap_grads(dw_ref, dy, xx_ref, k_taps, tm):
    base = HALO - k_taps + 1
    for k in range(k_taps):
        dw_ref[k:k + 1, :] += jnp.sum(dy * xx_ref[pl.ds(base + k, tm), :], axis=0, keepdims=True)


def _prev_halo(tm, col):
    return lambda i: (jnp.maximum(i * (tm // HALO) - 1, 0), col)


def _next_halo(tm, n_rows, col):
    return lambda i: (jnp.minimum((i + 1) * (tm // HALO), n_rows // HALO - 1), col)


def _norm_matmul(x, g, w, name, comm=None):
    t, d = x.shape
    n_s, _, tn = w.shape
    n = n_s * tn
    tm = min(TM_MM, t)

    def body(x_ref, g_ref, w_ref, h_ref, y_ref):
        @pl.when(pl.program_id(1) == 0)
        def _():
            xf = x_ref[...]
            r = lax.rsqrt(jnp.mean(xf * xf, axis=-1, keepdims=True) + EPS)
            h_ref[...] = ((xf * r) * g_ref[...]).astype(BF)

        y_ref[...] = _nn(h_ref[...], w_ref[...]).astype(BF)

    return _run(
        body, comm, name=name, grid=(t // tm, n // tn),
        in_specs=[pl.BlockSpec((tm, d), lambda i, j: (i, 0)),
                  pl.BlockSpec((1, d), lambda i, j: (0, 0)),
                  pl.BlockSpec((None, d, tn), lambda i, j: (j, 0, 0))],
        out_specs=[pl.BlockSpec((tm, d), lambda i, j: (i, 0)),
                   pl.BlockSpec((tm, tn), lambda i, j: (i, j))],
        out_shape=[jax.ShapeDtypeStruct((t, d), BF), jax.ShapeDtypeStruct((t, n), BF)],
        args=(x, g, w))


def _mem_kv(mem, g, w_kv, name):
    m, d = mem.shape
    n_s, _, ns = w_kv.shape

    def body(mem_ref, g_ref, w_ref, memn_ref, kv_ref):
        xf = mem_ref[...]
        r = lax.rsqrt(jnp.mean(xf * xf, axis=-1, keepdims=True) + EPS)
        memn = ((xf * r) * g_ref[...]).astype(BF)
        memn_ref[...] = memn
        for s in range(n_s):
            kv_ref[:, s * ns:(s + 1) * ns] = _nn(memn, w_ref[s]).astype(BF)

    return _pallas_call(
        body, name=name, grid=(1,),
        in_specs=[pl.BlockSpec((m, d), lambda i: (0, 0)),
                  pl.BlockSpec((1, d), lambda i: (0, 0)),
                  pl.BlockSpec((n_s, d, ns), lambda i: (0, 0, 0))],
        out_specs=[pl.BlockSpec((m, d), lambda i: (0, 0)),
                   pl.BlockSpec((m, 2 * d), lambda i: (0, 0))],
        out_shape=[jax.ShapeDtypeStruct((m, d), BF), jax.ShapeDtypeStruct((m, 2 * d), BF)],
        compiler_params=_params(1),
    )(mem, g, w_kv)


def _load_branch_inputs(i, proj_ref, gch_ref, vh_ref, u0h_ref, ugh_ref, xa_ref, xb_ref, d):
    gc = proj_ref[:, d:2 * d].astype(F32)
    v = proj_ref[:, 2 * d:3 * d].astype(F32)
    u0 = proj_ref[:, 3 * d:4 * d].astype(F32)
    ug = proj_ref[:, 4 * d:5 * d].astype(F32)
    keep = (i > 0).astype(F32)
    xa_ref[pl.ds(0, HALO), :] = gch_ref[...].astype(F32) * vh_ref[...].astype(F32) * keep
    xa_ref[pl.ds(HALO, gc.shape[0]), :] = gc * v
    xb_ref[pl.ds(0, HALO), :] = u0h_ref[...].astype(F32) * _sig(ugh_ref[...].astype(F32)) * keep
    xb_ref[pl.ds(HALO, gc.shape[0]), :] = u0 * _sig(ug)


def _softmax_rows(s):
    e = jnp.exp(s - jnp.max(s, axis=-1, keepdims=True))
    return e / jnp.sum(e, axis=-1, keepdims=True)


def _mixer_fwd(proj, x, kv, conv_a, conv_b, cbias, ln_g, ln_b, b_gate, wsq, l, name, comm=None):
    t, d = x.shape
    m = kv.shape[0]
    tm = min(TM_ROW, t)
    hd = d // N_HEADS
    scale = 1.0 / math.sqrt(hd)

    def body(proj_ref, gch_ref, vh_ref, u0h_ref, ugh_ref, x_ref, kv_ref, ca_w, cb_w, cbias_ref, lng_ref, lnb_ref,
             bg_ref, wa_ref, wb_ref, wc_ref, wo_ref,
             x1_ref, za_ref, zb_ref, o_ref, ya_ref, yb_ref, yc_ref, mg_ref, cb_ref, xa_ref, xb_ref):
        i = pl.program_id(0)
        _load_branch_inputs(i, proj_ref, gch_ref, vh_ref, u0h_ref, ugh_ref, xa_ref, xb_ref, d)
        gb = proj_ref[:, 0:d].astype(F32)
        za = (gb * _causal_taps(xa_ref, ca_w, K_A, tm)).astype(BF)
        za_ref[...] = za
        ya = _nn(za, wa_ref[...])
        ya_ref[...] = ya.astype(BF)
        cb = _causal_taps(xb_ref, cb_w, K_B, tm) + cbias_ref[...]
        cb_ref[...] = cb
        mu = jnp.mean(cb, axis=-1, keepdims=True)
        dlt = cb - mu
        rstd = lax.rsqrt(jnp.mean(dlt * dlt, axis=-1, keepdims=True) + EPS)
        lnb = (dlt * rstd) * lng_ref[...] + lnb_ref[...]
        zb = (lnb * _sig(lnb)).astype(BF)
        zb_ref[...] = zb
        yb = _nn(zb, wb_ref[...])
        yb_ref[...] = yb.astype(BF)
        for h in range(N_HEADS):
            qh = proj_ref[:, 5 * d + h * hd:5 * d + (h + 1) * hd]
            kh = kv_ref[:, h * hd:(h + 1) * hd]
            vh = kv_ref[:, d + h * hd:d + (h + 1) * hd]
            p = _softmax_rows(_nt(qh, kh) * scale)
            o_ref[:, h * hd:(h + 1) * hd] = _nn(p.astype(BF), vh).astype(BF)
        yc = _nn(o_ref[...], wc_ref[...])
        yc_ref[...] = yc.astype(BF)
        g0 = _sig(proj_ref[:, 6 * d:7 * d].astype(F32) + bg_ref[:, 0:d])
        g1 = _sig(proj_ref[:, 7 * d:8 * d].astype(F32) + bg_ref[:, d:2 * d])
        g2 = _sig(proj_ref[:, 8 * d:9 * d].astype(F32) + bg_ref[:, 2 * d:3 * d])
        mg = (g0 * ya + g1 * yb + g2 * yc).astype(BF)
        mg_ref[...] = mg
        x1_ref[...] = x_ref[...] + _nn(mg, wo_ref[...])

    row = lambda w_: pl.BlockSpec((tm, w_), lambda i: (i, 0))
    halo = lambda col: pl.BlockSpec((HALO, d), _prev_halo(tm, col))
    sq = lambda which: _resident((None, None, d, d), (l, which, 0, 0))
    act = jax.ShapeDtypeStruct((t, d), BF)
    return _run(
        body, comm, name=name, grid=(t // tm,),
        in_specs=[row(9 * d), halo(1), halo(2), halo(3), halo(4), row(d),
                  _resident((m, 2 * d), (0, 0)),
                  _resident((None, K_A, d), (l, 0, 0)), _resident((None, K_B, d), (l, 0, 0)),
                  _resident((1, d), (0, 0)), _resident((1, d), (0, 0)), _resident((1, d), (0, 0)),
                  _resident((1, 3 * d), (0, 0)), sq(0), sq(1), sq(2), sq(3)],
        out_specs=[row(d)] * 9,
        out_shape=[jax.ShapeDtypeStruct((t, d), F32)] + [act] * 7 + [jax.ShapeDtypeStruct((t, d), F32)],
        scratch_shapes=[pltpu.VMEM((HALO + tm, d), F32), pltpu.VMEM((HALO + tm, d), F32)],
        args=(proj, proj, proj, proj, proj, x, kv, conv_a, conv_b, cbias, ln_g, ln_b, b_gate, wsq, wsq, wsq, wsq))


def _ffn_down_fwd(up, x1, conv_f, w_down, l, name, comm=None):
    t, d = x1.shape
    f2 = up.shape[1]
    f = f2 // 2
    tm = min(TM_ROW, t)

    def body(up_ref, uph_ref, x1_ref, cw_ref, wd_ref, x2_ref, zf_ref, xx_ref):
        i = pl.program_id(0)
        xx_ref[pl.ds(0, HALO), :] = uph_ref[...].astype(F32) * (i > 0).astype(F32)
        xx_ref[pl.ds(HALO, tm), :] = up_ref[...].astype(F32)
        uc = _causal_taps(xx_ref, cw_ref, K_F, tm)
        gt = uc[:, 0:f]
        zf = (gt * _sig(gt) * uc[:, f:f2]).astype(BF)
        zf_ref[...] = zf
        x2_ref[...] = x1_ref[...] + _nn(zf, wd_ref[...])

    return _run(
        body, comm, name=name, grid=(t // tm,),
        in_specs=[pl.BlockSpec((tm, f2), lambda i: (i, 0)),
                  pl.BlockSpec((HALO, f2), _prev_halo(tm, 0)),
                  pl.BlockSpec((tm, d), lambda i: (i, 0)),
                  _resident((None, K_F, f2), (l, 0, 0)),
                  _resident((None, f, d), (l, 0, 0))],
        out_specs=[pl.BlockSpec((tm, d), lambda i: (i, 0)), pl.BlockSpec((tm, f), lambda i: (i, 0))],
        out_shape=[jax.ShapeDtypeStruct((t, d), F32), jax.ShapeDtypeStruct((t, f), BF)],
        scratch_shapes=[pltpu.VMEM((HALO + tm, f2), F32)],
        args=(up, up, x1, conv_f, w_down))


def _final_loss(x, g, target, name):
    t, d = x.shape
    tm = min(2 * TM_ROW, t)

    def body(x_ref, g_ref, t_ref, dx_ref, loss_ref, dg_ref):
        @pl.when(pl.program_id(0) == 0)
        def _():
            loss_ref[...] = jnp.zeros_like(loss_ref)
            dg_ref[...] = jnp.zeros_like(dg_ref)

        xf = x_ref[...]
        r = lax.rsqrt(jnp.mean(xf * xf, axis=-1, keepdims=True) + EPS)
        xhat = xf * r
        err = xhat * g_ref[...] - t_ref[...]
        loss_ref[...] += (0.5 / d) * jnp.sum(err * err)
        dy = err * (1.0 / d)
        dg_ref[...] += jnp.sum(dy * xhat, axis=0, keepdims=True)
        dxh = dy * g_ref[...]
        dx_ref[...] = r * (dxh - xhat * jnp.mean(dxh * xhat, axis=-1, keepdims=True))

    return _pallas_call(
        body, name=name, grid=(t // tm,),
        in_specs=[pl.BlockSpec((tm, d), lambda i: (i, 0)), pl.BlockSpec((1, d), lambda i: (0, 0)),
                  pl.BlockSpec((tm, d), lambda i: (i, 0))],
        out_specs=[pl.BlockSpec((tm, d), lambda i: (i, 0)), pl.BlockSpec((8, 128), lambda i: (0, 0)),
                   pl.BlockSpec((1, d), lambda i: (0, 0))],
        out_shape=[jax.ShapeDtypeStruct((t, d), F32), jax.ShapeDtypeStruct((8, 128), F32),
                   jax.ShapeDtypeStruct((1, d), F32)],
        compiler_params=_params(1),
    )(x, g, target)


def _ffn_down_bwd(dx2, up, conv_f, w_down, l, name, comm=None):
    t, d = dx2.shape
    f2 = up.shape[1]
    f = f2 // 2
    tm = min(TM_ROW, t)

    def body(dx2_ref, up_ref, uph_ref, cw_ref, wd_ref, duc_ref, dx2b_ref, dcw_ref, xx_ref):
        i = pl.program_id(0)

        @pl.when(i == 0)
        def _():
            dcw_ref[...] = jnp.zeros_like(dcw_ref)

        xx_ref[pl.ds(0, HALO), :] = uph_ref[...].astype(F32) * (i > 0).astype(F32)
        xx_ref[pl.ds(HALO, tm), :] = up_ref[...].astype(F32)
        uc = _causal_taps(xx_ref, cw_ref, K_F, tm)
        gt = uc[:, 0:f]
        sg = _sig(gt)
        dx2b = dx2_ref[...].astype(BF)
        dx2b_ref[...] = dx2b
        dzf = _nt(dx2b, wd_ref[...])
        duc_ref[:, 0:f] = (dzf * uc[:, f:f2] * (sg * (1.0 + gt * (1.0 - sg)))).astype(BF)
        duc_ref[:, f:f2] = (dzf * (gt * sg)).astype(BF)
        _tap_grads(dcw_ref, duc_ref[...].astype(F32), xx_ref, K_F, tm)

    return _run(
        body, comm, name=name, grid=(t // tm,),
        in_specs=[pl.BlockSpec((tm, d), lambda i: (i, 0)),
                  pl.BlockSpec((tm, f2), lambda i: (i, 0)),
                  pl.BlockSpec((HALO, f2), _prev_halo(tm, 0)),
                  _resident((None, K_F, f2), (l, 0, 0)),
                  _resident((None, f, d), (l, 0, 0))],
        out_specs=[pl.BlockSpec((tm, f2), lambda i: (i, 0)), pl.BlockSpec((tm, d), lambda i: (i, 0)),
                   pl.BlockSpec((K_F, f2), lambda i: (0, 0))],
        out_shape=[jax.ShapeDtypeStruct((t, f2), BF), jax.ShapeDtypeStruct((t, d), BF),
                   jax.ShapeDtypeStruct((K_F, f2), F32)],
        scratch_shapes=[pltpu.VMEM((HALO + tm, f2), F32)],
        args=(dx2, up, up, conv_f, w_down))


def _ffn_conv_bwd(duc, conv_f, l, name, comm=None):
    t, f2 = duc.shape
    tm = min(TM_ROW, t)
    n_t = t // tm

    def body(duc_ref, nxt_ref, cw_ref, dup_ref, yy_ref):
        i = pl.program_id(0)
        yy_ref[pl.ds(0, tm), :] = duc_ref[...].astype(F32)
        yy_ref[pl.ds(tm, HALO), :] = nxt_ref[...].astype(F32) * (i < n_t - 1).astype(F32)
        dup_ref[...] = _anticausal_taps(yy_ref, cw_ref, K_F, tm).astype(BF)

    return _run(
        body, comm, name=name, grid=(n_t,),
        in_specs=[pl.BlockSpec((tm, f2), lambda i: (i, 0)),
                  pl.BlockSpec((HALO, f2), _next_halo(tm, t, 0)),
                  _resident((None, K_F, f2), (l, 0, 0))],
        out_specs=[pl.BlockSpec((tm, f2), lambda i: (i, 0))],
        out_shape=[jax.ShapeDtypeStruct((t, f2), BF)],
        scratch_shapes=[pltpu.VMEM((tm + HALO, f2), F32)],
        args=(duc, duc, conv_f))


def _nt_matmul_norm_bwd(dy, w, x, g, dres, name, comm=None):
    t, n = dy.shape
    d = x.shape[1]
    tm = min(TM_MM // 2, t)
    n_k, _, tk = w.shape

    def body(dy_ref, w_ref, x_ref, g_ref, dres_ref, dx_ref, dxb_ref, dg_ref, acc_ref):
        i, k = pl.program_id(0), pl.program_id(1)

        @pl.when((i == 0) & (k == 0))
        def _():
            dg_ref[...] = jnp.zeros_like(dg_ref)

        @pl.when(k == 0)
        def _():
            acc_ref[...] = jnp.zeros_like(acc_ref)

        acc_ref[...] += _nt(dy_ref[...], w_ref[...])

        @pl.when(k == n_k - 1)
        def _():
            xf = x_ref[...]
            r = lax.rsqrt(jnp.mean(xf * xf, axis=-1, keepdims=True) + EPS)
            xhat = xf * r
            dh = acc_ref[...]
            dg_ref[...] += jnp.sum(dh * xhat, axis=0, keepdims=True)
            dxh = dh * g_ref[...]
            dx = dres_ref[...] + r * (dxh - xhat * jnp.mean(dxh * xhat, axis=-1, keepdims=True))
            dx_ref[...] = dx
            dxb_ref[...] = dx.astype(BF)

    return _run(
        body, comm, name=name, grid=(t // tm, n_k),
        in_specs=[pl.BlockSpec((tm, tk), lambda i, k: (i, k)),
                  pl.BlockSpec((None, d, tk), lambda i, k: (k, 0, 0)),
                  pl.BlockSpec((tm, d), lambda i, k: (i, 0)),
                  pl.BlockSpec((1, d), lambda i, k: (0, 0)),
                  pl.BlockSpec((tm, d), lambda i, k: (i, 0))],
        out_specs=[pl.BlockSpec((tm, d), lambda i, k: (i, 0)), pl.BlockSpec((tm, d), lambda i, k: (i, 0)),
                   pl.BlockSpec((1, d), lambda i, k: (0, 0))],
        out_shape=[jax.ShapeDtypeStruct((t, d), F32), jax.ShapeDtypeStruct((t, d), BF),
                   jax.ShapeDtypeStruct((1, d), F32)],
        scratch_shapes=[pltpu.VMEM((tm, d), F32)],
        args=(dy, w, x, g, dres))


def _mixer_bwd(dx1b, proj, ya, yb, yc, cb, kv, conv_a, conv_b, ln_g, ln_b, b_gate, wsq, l, name, comm=None):
    t, d = cb.shape
    m = kv.shape[0]
    tm = min(TM_ROW, t)
    hd = d // N_HEADS
    scale = 1.0 / math.sqrt(hd)

    def body(dx1b_ref, proj_ref, gch_ref, vh_ref, u0h_ref, ugh_ref, ya_ref, yb_ref, yc_ref, cb_ref, kv_ref,
             ca_w, cb_w, lng_ref, lnb_ref, bg_ref, wa_ref, wb_ref, wc_ref, wo_ref,
             dpre_ref, dya_ref, dyb_ref, dyc_ref, dkv_ref, dbg_ref, dlng_ref, dlnb_ref, dcbias_ref, dcaw_ref,
             dcbw_ref, xa_ref, xb_ref):
        i = pl.program_id(0)

        @pl.when(i == 0)
        def _():
            for ref in (dkv_ref, dbg_ref, dlng_ref, dlnb_ref, dcbias_ref, dcaw_ref, dcbw_ref):
                ref[...] = jnp.zeros_like(ref)

        _load_branch_inputs(i, proj_ref, gch_ref, vh_ref, u0h_ref, ugh_ref, xa_ref, xb_ref, d)
        dmg = _nt(dx1b_ref[...], wo_ref[...])
        ys = (ya_ref, yb_ref, yc_ref)
        dys = (dya_ref, dyb_ref, dyc_ref)
        for b in range(3):
            gate = _sig(proj_ref[:, (6 + b) * d:(7 + b) * d].astype(F32) + bg_ref[:, b * d:(b + 1) * d])
            dys[b][...] = (gate * dmg).astype(BF)
            dpg = dmg * ys[b][...].astype(F32) * gate * (1.0 - gate)
            dpre_ref[:, (6 + b) * d:(7 + b) * d] = dpg.astype(BF)
            dbg_ref[:, b * d:(b + 1) * d] += jnp.sum(dpg, axis=0, keepdims=True)
        gb = proj_ref[:, 0:d].astype(F32)
        ca = _causal_taps(xa_ref, ca_w, K_A, tm)
        dza = _nt(dya_ref[...], wa_ref[...])
        dpre_ref[:, 0:d] = (dza * ca).astype(BF)
        dpre_ref[:, d:2 * d] = (dza * gb).astype(BF)
        _tap_grads(dcaw_ref, dpre_ref[:, d:2 * d].astype(F32), xa_ref, K_A, tm)
        dpre_ref[:, 2 * d:3 * d] = jnp.zeros((tm, d), BF)
        cbv = cb_ref[...]
        mu = jnp.mean(cbv, axis=-1, keepdims=True)
        dlt = cbv - mu
        rstd = lax.rsqrt(jnp.mean(dlt * dlt, axis=-1, keepdims=True) + EPS)
        xhat = dlt * rstd
        lnb = xhat * lng_ref[...] + lnb_ref[...]
        sg = _sig(lnb)
        dzb = _nt(dyb_ref[...], wb_ref[...])
        dl = dzb * (sg * (1.0 + lnb * (1.0 - sg)))
        dlng_ref[...] += jnp.sum(dl * xhat, axis=0, keepdims=True)
        dlnb_ref[...] += jnp.sum(dl, axis=0, keepdims=True)
        dxh = dl * lng_ref[...]
        dcb = rstd * (dxh - jnp.mean(dxh, axis=-1, keepdims=True)
                      - xhat * jnp.mean(dxh * xhat, axis=-1, keepdims=True))
        dcbias_ref[...] += jnp.sum(dcb, axis=0, keepdims=True)
        dpre_ref[:, 3 * d:4 * d] = dcb.astype(BF)
        _tap_grads(dcbw_ref, dpre_ref[:, 3 * d:4 * d].astype(F32), xb_ref, K_B, tm)
        dpre_ref[:, 4 * d:5 * d] = jnp.zeros((tm, d), BF)
        do = _nt(dyc_ref[...], wc_ref[...]).astype(BF)
        for h in range(N_HEADS):
            qh = proj_ref[:, 5 * d + h * hd:5 * d + (h + 1) * hd]
            kh = kv_ref[:, h * hd:(h + 1) * hd]
            vh = kv_ref[:, d + h * hd:d + (h + 1) * hd]
            doh = do[:, h * hd:(h + 1) * hd]
            p = _softmax_rows(_nt(qh, kh) * scale)
            dp = _nt(doh, vh)
            ds = (p * (dp - jnp.sum(dp * p, axis=-1, keepdims=True)) * scale).astype(BF)
            dpre_ref[:, 5 * d + h * hd:5 * d + (h + 1) * hd] = _nn(ds, kh).astype(BF)
            dkv_ref[:, h * hd:(h + 1) * hd] += _tn(ds, qh)
            dkv_ref[:, d + h * hd:d + (h + 1) * hd] += _tn(p.astype(BF), doh)

    row = lambda w_: pl.BlockSpec((tm, w_), lambda i: (i, 0))
    halo = lambda col: pl.BlockSpec((HALO, d), _prev_halo(tm, col))
    sq = lambda which: _resident((None, None, d, d), (l, which, 0, 0))
    acc = lambda r, c: pl.BlockSpec((r, c), lambda i: (0, 0))
    act = jax.ShapeDtypeStruct((t, d), BF)
    vec = lambda r, c: jax.ShapeDtypeStruct((r, c), F32)
    return _run(
        body, comm, name=name, grid=(t // tm,),
        in_specs=[row(d), row(9 * d), halo(1), halo(2), halo(3), halo(4), row(d), row(d), row(d), row(d),
                  _resident((m, 2 * d), (0, 0)),
                  _resident((None, K_A, d), (l, 0, 0)), _resident((None, K_B, d), (l, 0, 0)),
                  _resident((1, d), (0, 0)), _resident((1, d), (0, 0)), _resident((1, 3 * d), (0, 0)),
                  sq(0), sq(1), sq(2), sq(3)],
        out_specs=[row(9 * d), row(d), row(d), row(d), acc(m, 2 * d), acc(1, 3 * d), acc(1, d), acc(1, d),
                   acc(1, d), acc(K_A, d), acc(K_B, d)],
        out_shape=[jax.ShapeDtypeStruct((t, 9 * d), BF), act, act, act, vec(m, 2 * d), vec(1, 3 * d), vec(1, d),
                   vec(1, d), vec(1, d), vec(K_A, d), vec(K_B, d)],
        scratch_shapes=[pltpu.VMEM((HALO + tm, d), F32), pltpu.VMEM((HALO + tm, d), F32)],
        args=(dx1b, proj, proj, proj, proj, proj, ya, yb, yc, cb, kv, conv_a, conv_b, ln_g, ln_b, b_gate,
              wsq, wsq, wsq, wsq))


def _inproj_conv_bwd(dpre, proj, conv_a, conv_b, l, name, comm=None):
    t, d9 = dpre.shape
    d = d9 // 9
    tm = min(TM_ROW, t)
    n_t = t // tm

    def body(dpre_ref, nxa_ref, nxb_ref, proj_ref, ca_w, cb_w, dproj_ref, ya_ref, yb_ref):
        i = pl.program_id(0)
        keep = (i < n_t - 1).astype(F32)
        ya_ref[pl.ds(0, tm), :] = dpre_ref[:, d:2 * d].astype(F32)
        ya_ref[pl.ds(tm, HALO), :] = nxa_ref[...].astype(F32) * keep
        yb_ref[pl.ds(0, tm), :] = dpre_ref[:, 3 * d:4 * d].astype(F32)
        yb_ref[pl.ds(tm, HALO), :] = nxb_ref[...].astype(F32) * keep
        dproj_ref[:, 0:d] = dpre_ref[:, 0:d]
        dproj_ref[:, 5 * d:9 * d] = dpre_ref[:, 5 * d:9 * d]
        dcv = _anticausal_taps(ya_ref, ca_w, K_A, tm)
        dproj_ref[:, d:2 * d] = (dcv * proj_ref[:, 2 * d:3 * d].astype(F32)).astype(BF)
        dproj_ref[:, 2 * d:3 * d] = (dcv * proj_ref[:, d:2 * d].astype(F32)).astype(BF)
        dub = _anticausal_taps(yb_ref, cb_w, K_B, tm)
        sg = _sig(proj_ref[:, 4 * d:5 * d].astype(F32))
        dproj_ref[:, 3 * d:4 * d] = (dub * sg).astype(BF)
        dproj_ref[:, 4 * d:5 * d] = (dub * proj_ref[:, 3 * d:4 * d].astype(F32) * sg * (1.0 - sg)).astype(BF)

    return _run(
        body, comm, name=name, grid=(n_t,),
        in_specs=[pl.BlockSpec((tm, d9), lambda i: (i, 0)),
                  pl.BlockSpec((HALO, d), _next_halo(tm, t, 1)),
                  pl.BlockSpec((HALO, d), _next_halo(tm, t, 3)),
                  pl.BlockSpec((tm, d9), lambda i: (i, 0)),
                  _resident((None, K_A, d), (l, 0, 0)), _resident((None, K_B, d), (l, 0, 0))],
        out_specs=[pl.BlockSpec((tm, d9), lambda i: (i, 0))],
        out_shape=[jax.ShapeDtypeStruct((t, d9), BF)],
        scratch_shapes=[pltpu.VMEM((tm + HALO, d), F32), pltpu.VMEM((tm + HALO, d), F32)],
        args=(dpre, dpre, dpre, proj, conv_a, conv_b))


def _mem_kv_bwd(dkv, memn, mem, g, w_kv, name):
    m, d = mem.shape
    n_s, _, ns = w_kv.shape

    def body(dkv_ref, memn_ref, mem_ref, g_ref, w_ref, dw_ref, dg_ref):
        dkvb = dkv_ref[...].astype(BF)
        dw_ref[...] = _tn(memn_ref[...], dkvb)
        dmemn = _nt(dkvb[:, 0:ns], w_ref[0])
        for s in range(1, n_s):
            dmemn = dmemn + _nt(dkvb[:, s * ns:(s + 1) * ns], w_ref[s])
        xf = mem_ref[...]
        r = lax.rsqrt(jnp.mean(xf * xf, axis=-1, keepdims=True) + EPS)
        dg_ref[...] = jnp.sum(dmemn * (xf * r), axis=0, keepdims=True)

    return _pallas_call(
        body, name=name, grid=(1,),
        in_specs=[pl.BlockSpec((m, 2 * d), lambda i: (0, 0)), pl.BlockSpec((m, d), lambda i: (0, 0)),
                  pl.BlockSpec((m, d), lambda i: (0, 0)), pl.BlockSpec((1, d), lambda i: (0, 0)),
                  pl.BlockSpec((n_s, d, ns), lambda i: (0, 0, 0))],
        out_specs=[pl.BlockSpec((d, 2 * d), lambda i: (0, 0)), pl.BlockSpec((1, d), lambda i: (0, 0))],
        out_shape=[jax.ShapeDtypeStruct((d, 2 * d), F32), jax.ShapeDtypeStruct((1, d), F32)],
        compiler_params=_params(1),
    )(dkv, memn, mem, g, w_kv)


def _dw_matmul(a, b, tn, name, comm=None):
    t, k = a.shape
    n = b.shape[1]
    tt = min(TT_DW, t)
    while tt * k > DW_LHS_ELEMS and tt % 2 == 0:
        tt //= 2

    def body(a_ref, b_ref, o_ref):
        @pl.when(pl.program_id(1) == 0)
        def _():
            o_ref[...] = jnp.zeros_like(o_ref)

        o_ref[...] += _tn(a_ref[...], b_ref[...])

    return _run(
        body, comm, name=name, grid=(n // tn, t // tt),
        in_specs=[pl.BlockSpec((tt, k), lambda j, s: (s, 0)), pl.BlockSpec((tt, tn), lambda j, s: (s, j))],
        out_specs=[pl.BlockSpec((k, tn), lambda j, s: (0, j))],
        out_shape=[jax.ShapeDtypeStruct((k, n), F32)],
        args=(a, b))


class _GradReduce:
    def __init__(self, l, grads, core):
        self.l, self.core, self.names = l, core, tuple(grads)
        self.views = {n: _halves_view(n, g) for n, g in grads.items()}
        self.got, self.sums, self.pieces = {}, {}, {}

    def swap_program(self):
        return _swap_program([self.views[n] for n in self.names])

    def swapped(self, outs):
        self.got = dict(zip(self.names, outs))
        self.sums = {n: _add_halves(self.views[n], self.got[n], self.core, f"add_halves_{n}_{self.l}")
                     for n in self.names}

    def scatter_program(self, names=None):
        names = self.names if names is None else names
        return _scatter_program([self.sums[n] for n in names], names)

    def scattered(self, outs, names=None):
        self.pieces.update(zip(self.names if names is None else names, outs))


def _step(x, mem, target, sh, conv_sh, small, core):
    depth = sh["w_in"].shape[0]
    d = x.shape[1]
    f2 = sh["w_up"].shape[2] * N_CHIPS
    tn_dw_in = min(1024, d)
    tn_dw_up = f2 // 11 if f2 % (11 * 128) == 0 and f2 // 11 >= 128 else f2
    row = lambda v: v.reshape(1, -1)
    one = lambda a: a[None]

    w_in, cab, cf = _run_comm(_gather_program([sh["w_in"]] + conv_sh, ("col", "small", "small"), (0, None, None)),
                              "gather_first")
    saved = []
    for l in range(depth):
        conv = dict(a=cab[l:l + 1, :K_A], b=cab[l:l + 1, K_A:], f=cf[l:l + 1])
        (h, proj), (wsq, w_kv) = _norm_matmul(
            x, row(small["norm_mix_g"][l]), w_in, f"in_proj_{l}",
            _gather_program([sh["wsq"], sh["w_kv"]], ("row", "col"), (l, l)))
        memn, kv = _mem_kv(mem, row(small["norm_mem_g"][l]), w_kv, f"mem_kv_{l}")
        (x1, za, zb, o, ya, yb, yc, mg, cb), (w_up, w_down) = _mixer_fwd(
            proj, x, kv, conv["a"], conv["b"], row(small["conv_b_bias"][l]), row(small["ln_b_g"][l]),
            row(small["ln_b_b"][l]), row(small["b_gate"][l]), one(wsq), 0, f"mixer_fwd_{l}",
            _gather_program([sh["w_up"], sh["w_down"]], ("col", "row"), (l, l)))
        more = l + 1 < depth
        nxt = _gather_program([sh["w_in"]], ("col",), (l + 1,), (0, 2)) if more else None
        (h2, up), w_in_next = _norm_matmul(x1, row(small["norm_ffn_g"][l]), w_up, f"up_proj_{l}", nxt)
        nxt = _gather_program([sh["w_in"]], ("col",), (l + 1,), (1, 2), w_in_next) if more else None
        (x2, zf), w_in_next = _ffn_down_fwd(up, x1, conv["f"], one(w_down), 0, f"ffn_down_fwd_{l}", nxt)
        saved.append(dict(x=x, x1=x1, memn=memn, kv=kv, h=h, proj=proj, za=za, zb=zb, o=o, ya=ya, yb=yb, yc=yc,
                          mg=mg, cb=cb, h2=h2, up=up, zf=zf, w_in=w_in, wsq=one(wsq), w_kv=w_kv,
                          w_up=w_up, w_down=one(w_down), conv=conv))
        x = x2
        w_in = w_in_next[0] if w_in_next else None
    dx, loss, dg_final = _final_loss(x, row(small["norm_final_g"]), target, "final_loss")
    sgrads, pieces = [None] * depth, [None] * depth
    above = None
    first, second, rest = ("w_up",), ("w_down",) + SQUARES, ("w_in", "w_kv")
    for l in reversed(range(depth)):
        s = saved[l]
        conv = s["conv"]
        bottom = l == 0
        (duc, dx2b, dconv_f), got = _ffn_down_bwd(dx, s["up"], conv["f"], s["w_down"], 0, f"ffn_down_bwd_{l}",
                                                  above and above.swap_program())
        if above:
            above.swapped(got)
        (dup,), got = _ffn_conv_bwd(duc, conv["f"], 0, f"ffn_conv_bwd_{l}", above and above.scatter_program(first))
        if above:
            above.scattered(got, first)
        (dx1, dx1b, dg_ffn), got = _nt_matmul_norm_bwd(
            dup, s["w_up"], s["x1"], row(small["norm_ffn_g"][l]), dx, f"up_proj_bwd_{l}",
            above and above.scatter_program(second))
        if above:
            above.scattered(got, second)
        grads = dict(w_up=_dw_matmul(s["h2"], dup, tn_dw_up, f"dw_up_{l}")[0][0],
                     w_down=_dw_matmul(s["zf"], dx2b, d, f"dw_down_{l}")[0][0])
        ffn = _GradReduce(l, grads, core) if bottom else None
        (dpre, dya, dyb, dyc, dkv, dbg, dlng, dlnb, dcbias, dconv_a, dconv_b), got = _mixer_bwd(
            dx1b, s["proj"], s["ya"], s["yb"], s["yc"], s["cb"], s["kv"], conv["a"], conv["b"],
            row(small["ln_b_g"][l]), row(small["ln_b_b"][l]), row(small["b_gate"][l]), s["wsq"], 0,
            f"mixer_bwd_{l}", _join([above and above.scatter_program(rest), ffn and ffn.swap_program()]))
        if above:
            above.scattered(got[:len(rest)], rest)
            pieces[above.l] = above.pieces
            got = got[len(rest):]
        if ffn:
            ffn.swapped(got)
        dw_kv, dg_mem = _mem_kv_bwd(dkv, s["memn"], mem, row(small["norm_mem_g"][l]), s["w_kv"], f"mem_kv_bwd_{l}")
        mix_grads = dict(w_a_out=_dw_matmul(s["za"], dya, d, f"dw_a_out_{l}")[0][0],
                         w_b_out=_dw_matmul(s["zb"], dyb, d, f"dw_b_out_{l}")[0][0],
                         w_att_out=_dw_matmul(s["o"], dyc, d, f"dw_att_out_{l}")[0][0],
                         w_o=_dw_matmul(s["mg"], dx1b, d, f"dw_o_{l}")[0][0], w_kv=dw_kv)
        mix = _GradReduce(l, mix_grads, core) if bottom else None
        (dproj,), got = _inproj_conv_bwd(dpre, s["proj"], conv["a"], conv["b"], 0, f"inproj_conv_bwd_{l}",
                                         _join([ffn and ffn.scatter_program(), mix and mix.swap_program()]))
        if bottom:
            ffn.scattered(got[:len(ffn.names)])
            mix.swapped(got[len(ffn.names):])
        in_grads = dict(w_in=_dw_matmul(s["h"], dproj, tn_dw_in, f"dw_in_{l}")[0][0])
        inp = _GradReduce(l, in_grads, core) if bottom else None
        (dx0, _, dg_mix), got = _nt_matmul_norm_bwd(
            dproj, s["w_in"], s["x"], row(small["norm_mix_g"][l]), dx1, f"in_proj_bwd_{l}",
            _join([mix and mix.scatter_program(), inp and inp.swap_program()]))
        if bottom:
            mix.scattered(got[:len(mix.names)])
            inp.swapped(got[len(mix.names):])
            inp.scattered(_run_comm(inp.scatter_program(), f"scatter_w_in_{l}"))
            pieces[l] = {**ffn.pieces, **mix.pieces, **inp.pieces}
        else:
            above = _GradReduce(l, {**grads, **mix_grads, **in_grads}, core)
        sgrads[l] = dict(norm_mix_g=dg_mix, norm_mem_g=dg_mem, b_gate=dbg, conv_b_bias=dcbias, ln_b_g=dlng,
                         ln_b_b=dlnb, norm_ffn_g=dg_ffn, conv_a_w=dconv_a, conv_b_w=dconv_b, conv_ffn_w=dconv_f)
        dx = dx0
    return loss, dx, sgrads, dg_final, pieces


BIG = ("w_in", "w_a_out", "w_b_out", "w_att_out", "w_o", "w_kv", "w_up", "w_down")
COL = ("w_in", "w_kv", "w_up")
SQUARES = ("w_a_out", "w_b_out", "w_att_out", "w_o")
ANY = pl.BlockSpec(memory_space=pl.ANY)


def _place():
    x, y, c = lax.axis_index("x"), lax.axis_index("y"), lax.axis_index("c")
    chips = [(1 - x, y), (x, 1 - y), (1 - x, 1 - y)]
    return x, y, c, 2 * x + y, chips


def _remote(src, dst, send_sem, recv_sem, dev):
    return pltpu.make_async_remote_copy(src_ref=src, dst_ref=dst, send_sem=send_sem, recv_sem=recv_sem,
                                        device_id=dev, device_id_type=MESH)


def _window(ref, rows, cols):
    return ref.at[(slice(None),) * (len(ref.shape) - 2) + (rows, cols)]


def _row_tile(rows, cols, unit=16, limit=1 << 20):
    best = unit
    for tr in range(unit, rows + 1, unit):
        if rows % tr == 0 and tr * cols <= limit:
            best = tr
    return best


def _cast_shard(w, name):
    depth, k, n = w.shape

    def body(w_ref, o_ref):
        o_ref[...] = w_ref[...].astype(BF)

    return _pallas_call(
        body, name=name, grid=(depth,),
        in_specs=[pl.BlockSpec((None, k, n), lambda l: (l, 0, 0))],
        out_specs=pl.BlockSpec((None, k, n), lambda l: (l, 0, 0)),
        out_shape=jax.ShapeDtypeStruct((depth, k, n), BF),
        compiler_params=_params(1),
    )(w)


def _cast_squares(ws, name):
    depth, k, n = ws[0].shape

    def body(a_ref, b_ref, c_ref, d_ref, o_ref):
        for i, ref in enumerate((a_ref, b_ref, c_ref, d_ref)):
            o_ref[i] = ref[...].astype(BF)

    return _pallas_call(
        body, name=name, grid=(depth,),
        in_specs=[pl.BlockSpec((None, k, n), lambda l: (l, 0, 0))] * 4,
        out_specs=pl.BlockSpec((None, 4, k, n), lambda l: (l, 0, 0, 0)),
        out_shape=jax.ShapeDtypeStruct((depth, 4, k, n), BF),
        compiler_params=_params(1),
    )(*ws)


def _gather_program(shards, kinds, layers, part=(0, 1), into=None):
    n_t = len(shards)
    index, count = part

    def full_shape(s, kind, layer):
        shp = list(s.shape if layer is None else s.shape[1:])
        if kind == "col":
            return (N_CHIPS,) + tuple(shp)
        shp[-2 if kind == "row" else -1] *= N_CHIPS
        return tuple(shp)

    def run(phase, ins, full, sems):
        shard = [r if l is None else r.at[l] for r, l in zip(ins, layers)]
        ici_send, ici_recv, sib_send, sib_recv, loc_sem = sems
        x, y, c, me, chips = _place()
        sibling = (x, y, 1 - c)

        def src_part(i, half):
            s, kind = shard[i], kinds[i]
            if kind == "small":
                return s
            r = s.shape[-2] // (2 * count)
            return _window(s, pl.ds(pl.multiple_of((half * count + index) * r, 16), r), slice(None))

        def dst_part(i, chip, half):
            f, s, kind = full[i], shard[i], kinds[i]
            rows, cols = s.shape[-2], s.shape[-1]
            if kind == "small":
                return _window(f, slice(None), pl.ds(pl.multiple_of(chip * cols, 128), cols))
            r = rows // (2 * count)
            at = (half * count + index) * r
            if kind == "col":
                return f.at[chip, pl.ds(pl.multiple_of(at, 16), r), :]
            return _window(f, pl.ds(pl.multiple_of(chip * rows + at, 16), r), slice(None))

        def own_slot(i):
            f, s, kind = full[i], shard[i], kinds[i]
            rows, cols = s.shape[-2], s.shape[-1]
            if kind == "row":
                return _window(f, pl.ds(pl.multiple_of(me * rows, 16), rows), slice(None))
            if kind == "col":
                return f.at[me]
            return _window(f, slice(None), pl.ds(pl.multiple_of(me * cols, 128), cols))

        local = [pltpu.make_async_copy(shard[i], own_slot(i), loc_sem.at[i]) for i in range(n_t)] if index == 0 else []
        sends = []
        for i in range(n_t):
            for j, chip in enumerate(chips):
                sends.append(_remote(src_part(i, c), dst_part(i, me, c), ici_send.at[3 * i + j],
                                     ici_recv.at[3 * i + j], (*chip, c)))
        if phase == "start":
            for cp in local + sends:
                cp.start()
            return
        passed = []
        for i in range(n_t):
            for j, chip in enumerate(chips):
                k = 2 * chip[0] + chip[1]
                landed = dst_part(i, k, c)
                _remote(landed, landed, ici_send.at[3 * i + j], ici_recv.at[3 * i + j], (*chip, c)).wait_recv()
                if kinds[i] != "small":
                    cp = _remote(landed, landed, sib_send.at[3 * i + j], sib_recv.at[3 * i + j], sibling)
                    cp.start()
                    passed.append(cp)
        for i in range(n_t):
            if kinds[i] == "small":
                continue
            for j, chip in enumerate(chips):
                k = 2 * chip[0] + chip[1]
                other = dst_part(i, k, 1 - c)
                _remote(other, other, sib_send.at[3 * i + j], sib_recv.at[3 * i + j], sibling).wait_recv()
        for cp in sends + passed:
            cp.wait_send()
        for cp in local:
            cp.wait()

    outs = [jax.ShapeDtypeStruct(full_shape(s, k, l), s.dtype) for s, k, l in zip(shards, kinds, layers)]
    sems = [pltpu.SemaphoreType.DMA((3 * n_t,))] * 4 + [pltpu.SemaphoreType.DMA((n_t,))]
    return _Comm(list(shards) + list(into or []), outs, sems, functools.partial(run, "start"),
                 functools.partial(run, "finish"), {n_t + i: i for i in range(len(into or []))})


def _all_reduce_small(part, name):
    r, n = part.shape

    def body(in_ref, out_ref, gather_ref, send_sems, recv_sems):
        x, y, c, _, _ = _place()
        me = 4 * x + 2 * y + c
        gather_ref[me] = in_ref[...]
        sends = []
        for k in range(1, N_DEV):
            peer = (me + k) % N_DEV
            sends.append(_remote(in_ref, gather_ref.at[me], send_sems.at[k - 1], recv_sems.at[k - 1],
                                 (peer // 4, (peer // 2) % 2, peer % 2)))
        for cp in sends:
            cp.start()
        for k in range(1, N_DEV):
            origin = (me + N_DEV - k) % N_DEV
            _remote(in_ref, gather_ref.at[origin], send_sems.at[k - 1], recv_sems.at[k - 1],
                    (x, y, c)).wait_recv()
        for cp in sends:
            cp.wait_send()
        total = gather_ref[0]
        for dev in range(1, N_DEV):
            total = total + gather_ref[dev]
        out_ref[...] = total

    vm = pl.BlockSpec(memory_space=pltpu.VMEM)
    return _pallas_call(
        body, name=name, in_specs=[vm], out_specs=vm, out_shape=jax.ShapeDtypeStruct((r, n), F32),
        scratch_shapes=[pltpu.VMEM((N_DEV, r, n), F32), pltpu.SemaphoreType.DMA((N_DEV - 1,)),
                        pltpu.SemaphoreType.DMA((N_DEV - 1,))],
        compiler_params=pltpu.CompilerParams(vmem_limit_bytes=VMEM_LIMIT),
    )(part)


def _halves_view(name, dw):
    k, n = dw.shape
    s = 1 if name in COL else N_CHIPS
    return dw.reshape(s, 2, k // (2 * s), n)


def _swap_program(views):
    n_t = len(views)

    def run(phase, src, dst, sems):
        send_sems, recv_sems = sems
        x, y, c, _, _ = _place()
        copies = [_remote(src[i].at[:, 1 - c], dst[i], send_sems.at[i], recv_sems.at[i], (x, y, 1 - c))
                  for i in range(n_t)]
        for cp in copies:
            if phase == "start":
                cp.start()
            else:
                cp.wait()

    outs = [jax.ShapeDtypeStruct((v.shape[0],) + v.shape[2:], F32) for v in views]
    sems = [pltpu.SemaphoreType.DMA((n_t,)), pltpu.SemaphoreType.DMA((n_t,))]
    return _Comm(views, outs, sems, functools.partial(run, "start"), functools.partial(run, "finish"))


def _add_halves(view, got, core, name):
    s, _, r, n = view.shape
    tr = _row_tile(r, n)

    def body(c_ref, a_ref, b_ref, o_ref):
        o_ref[...] = (a_ref[...] + b_ref[...]).astype(BF)

    return _pallas_call(
        body, name=name,
        grid_spec=pltpu.PrefetchScalarGridSpec(
            num_scalar_prefetch=1, grid=(s, r // tr),
            in_specs=[pl.BlockSpec((None, None, tr, n), lambda i, j, c_ref: (i, c_ref[0], j, 0)),
                      pl.BlockSpec((None, tr, n), lambda i, j, c_ref: (i, j, 0))],
            out_specs=pl.BlockSpec((None, tr, n), lambda i, j, c_ref: (i, j, 0))),
        out_shape=jax.ShapeDtypeStruct((s, r, n), BF),
        compiler_params=_params(2),
    )(core, view, got)


def _scatter_program(sums, names):
    n_t = len(sums)

    def piece_shape(nm, p):
        s, r, n = p.shape
        return (r, n // N_CHIPS) if nm in COL else (r, n)

    def run(phase, src, dst, sems):
        ici_send, ici_recv, sib_send, sib_recv, loc_sem = sems
        x, y, c, me, chips = _place()
        sibling = (x, y, 1 - c)

        def piece(i, chip):
            if names[i] in COL:
                cw = src[i].shape[2] // N_CHIPS
                return src[i].at[0, :, pl.ds(pl.multiple_of(chip * cw, 128), cw)]
            return src[i].at[chip]

        local = [pltpu.make_async_copy(piece(i, me), dst[i].at[c, me], loc_sem.at[i]) for i in range(n_t)]
        sends = []
        for i in range(n_t):
            sends.append(_remote(piece(i, me), dst[i].at[c, me], sib_send.at[4 * i + 3], sib_recv.at[4 * i + 3],
                                 sibling))
            for j, chip in enumerate(chips):
                k = 2 * chip[0] + chip[1]
                sends.append(_remote(piece(i, k), dst[i].at[c, me], ici_send.at[3 * i + j], ici_recv.at[3 * i + j],
                                     (*chip, c)))
        if phase == "start":
            for cp in local + sends:
                cp.start()
            return
        passed = []
        for i in range(n_t):
            for j, chip in enumerate(chips):
                k = 2 * chip[0] + chip[1]
                landed = dst[i].at[c, k]
                _remote(landed, landed, ici_send.at[3 * i + j], ici_recv.at[3 * i + j], (*chip, c)).wait_recv()
                cp = _remote(landed, landed, sib_send.at[4 * i + j], sib_recv.at[4 * i + j], sibling)
                cp.start()
                passed.append(cp)
        for i in range(n_t):
            other = dst[i].at[1 - c, me]
            _remote(other, other, sib_send.at[4 * i + 3], sib_recv.at[4 * i + 3], sibling).wait_recv()
            for j, chip in enumerate(chips):
                k = 2 * chip[0] + chip[1]
                other = dst[i].at[1 - c, k]
                _remote(other, other, sib_send.at[4 * i + j], sib_recv.at[4 * i + j], sibling).wait_recv()
        for cp in sends + passed:
            cp.wait_send()
        for cp in local:
            cp.wait()

    outs = [jax.ShapeDtypeStruct((2, N_CHIPS) + piece_shape(nm, p), BF) for nm, p in zip(names, sums)]
    sems = [pltpu.SemaphoreType.DMA((3 * n_t,))] * 2 + [pltpu.SemaphoreType.DMA((4 * n_t,))] * 2 + [
        pltpu.SemaphoreType.DMA((n_t,))]
    return _Comm(sums, outs, sems, functools.partial(run, "start"), functools.partial(run, "finish"))


def _adamw(w, g, m, v):
    m = ADAM_B1 * m + (1.0 - ADAM_B1) * g
    v = ADAM_B2 * v + (1.0 - ADAM_B2) * (g * g)
    m_hat = m / (1.0 - ADAM_B1 ** ADAM_STEP)
    v_hat = v / (1.0 - ADAM_B2 ** ADAM_STEP)
    return -ADAM_LR * (m_hat / (jnp.sqrt(v_hat) + ADAM_EPS) + ADAM_WD * w), m, v


def _adam_shard(pieces, w, m, v, l, prev, name):
    depth, rows, cw = w.shape
    hr = rows // 2
    tr = _row_tile(hr, cw, limit=1 << 18)
    n_i = hr // tr

    def body(*refs):
        z_ref, w_ref, m_ref, v_ref = refs[:4]
        g_ref, d_ref, nm_ref, nv_ref = refs[-4:]
        g = z_ref[0].astype(F32)
        for k in range(1, N_CHIPS):
            g = g + z_ref[k].astype(F32)
        g_ref[...] = g
        d_ref[...], nm_ref[...], nv_ref[...] = _adamw(w_ref[...], g, m_ref[...], v_ref[...])

    par = pl.BlockSpec((None, tr, cw), lambda h, i: (l, h * n_i + i, 0))
    out = jax.ShapeDtypeStruct((depth, rows, cw), F32)
    extra = [] if prev is None else list(prev)
    return _pallas_call(
        body, name=name, grid=(2, n_i),
        in_specs=[pl.BlockSpec((None, N_CHIPS, tr, cw), lambda h, i: (h, 0, i, 0)), par, par, par] + [ANY] * len(extra),
        out_specs=[par] * 4, out_shape=[out] * 4,
        input_output_aliases={4 + k: k for k in range(len(extra))},
        compiler_params=_params(2),
    )(pieces, w, m, v, *extra)


def _adam_small(g, w, m, v, name):
    def body(g_ref, w_ref, m_ref, v_ref, d_ref, nm_ref, nv_ref):
        d_ref[...], nm_ref[...], nv_ref[...] = _adamw(w_ref[...], g_ref[...], m_ref[...], v_ref[...])

    vm = pl.BlockSpec(memory_space=pltpu.VMEM)
    out = jax.ShapeDtypeStruct(g.shape, F32)
    return _pallas_call(body, name=name, in_specs=[vm] * 4, out_specs=[vm] * 3, out_shape=[out] * 3)(g, w, m, v)


WEIGHTS = ("norm_mix_g", "norm_mem_g", "w_in", "b_gate", "conv_a_w", "w_a_out", "conv_b_w", "conv_b_bias", "ln_b_g",
           "ln_b_b", "w_b_out", "w_kv", "w_att_out", "w_o", "norm_ffn_g", "w_up", "conv_ffn_w", "w_down",
           "norm_final_g")
REPLICATED = ("norm_mix_g", "norm_mem_g", "b_gate", "conv_b_bias", "ln_b_g", "ln_b_b", "norm_ffn_g")
CONVS = ("conv_a_w", "conv_b_w", "conv_ffn_w")
PACK_WIDTH = 1024


def _pack(arrays):
    flat = jnp.concatenate([a.reshape(-1) for a in arrays])
    size = -(-flat.shape[0] // (8 * PACK_WIDTH)) * (8 * PACK_WIDTH)
    return jnp.pad(flat, (0, size - flat.shape[0])).reshape(-1, PACK_WIDTH)


def _unpack(packed, shapes):
    flat = packed.reshape(-1)
    out, at = [], 0
    for shp in shapes:
        n = math.prod(shp)
        out.append(flat[at:at + n].reshape(shp))
        at += n
    return out


def kernel(x, mem, norm_mix_g, norm_mem_g, w_in, b_gate, conv_a_w, w_a_out, conv_b_w, conv_b_bias, ln_b_g, ln_b_b, w_b_out, w_kv, w_att_out, w_o, norm_ffn_g, w_up, conv_ffn_w, w_down, norm_final_g, loss_target, m_norm_mix_g, m_norm_mem_g, m_w_in, m_b_gate, m_conv_a_w, m_w_a_out, m_conv_b_w, m_conv_b_bias, m_ln_b_g, m_ln_b_b, m_w_b_out, m_w_kv, m_w_att_out, m_w_o, m_norm_ffn_g, m_w_up, m_conv_ffn_w, m_w_down, m_norm_final_g, v_norm_mix_g, v_norm_mem_g, v_w_in, v_b_gate, v_conv_a_w, v_w_a_out, v_conv_b_w, v_conv_b_bias, v_ln_b_g, v_ln_b_b, v_w_b_out, v_w_kv, v_w_att_out, v_w_o, v_norm_ffn_g, v_w_up, v_conv_ffn_w, v_w_down, v_norm_final_g):
    w = dict(norm_mix_g=norm_mix_g, norm_mem_g=norm_mem_g, w_in=w_in, b_gate=b_gate, conv_a_w=conv_a_w,
             w_a_out=w_a_out, conv_b_w=conv_b_w, conv_b_bias=conv_b_bias, ln_b_g=ln_b_g, ln_b_b=ln_b_b,
             w_b_out=w_b_out, w_kv=w_kv, w_att_out=w_att_out, w_o=w_o, norm_ffn_g=norm_ffn_g, w_up=w_up,
             conv_ffn_w=conv_ffn_w, w_down=w_down, norm_final_g=norm_final_g)
    mom = dict(norm_mix_g=m_norm_mix_g, norm_mem_g=m_norm_mem_g, w_in=m_w_in, b_gate=m_b_gate, conv_a_w=m_conv_a_w,
               w_a_out=m_w_a_out, conv_b_w=m_conv_b_w, conv_b_bias=m_conv_b_bias, ln_b_g=m_ln_b_g, ln_b_b=m_ln_b_b,
               w_b_out=m_w_b_out, w_kv=m_w_kv, w_att_out=m_w_att_out, w_o=m_w_o, norm_ffn_g=m_norm_ffn_g,
               w_up=m_w_up, conv_ffn_w=m_conv_ffn_w, w_down=m_w_down, norm_final_g=m_norm_final_g)
    var = dict(norm_mix_g=v_norm_mix_g, norm_mem_g=v_norm_mem_g, w_in=v_w_in, b_gate=v_b_gate, conv_a_w=v_conv_a_w,
               w_a_out=v_w_a_out, conv_b_w=v_conv_b_w, conv_b_bias=v_conv_b_bias, ln_b_g=v_ln_b_g, ln_b_b=v_ln_b_b,
               w_b_out=v_w_b_out, w_kv=v_w_kv, w_att_out=v_w_att_out, w_o=v_w_o, norm_ffn_g=v_norm_ffn_g,
               w_up=v_w_up, conv_ffn_w=v_conv_ffn_w, w_down=v_w_down, norm_final_g=v_norm_final_g)
    depth = w_in.shape[0]
    chip = 2 * lax.axis_index("x") + lax.axis_index("y")
    core = lax.axis_index("c").astype(jnp.int32).reshape(1)

    sh = dict(w_in=_cast_shard(w_in, "cast_w_in"), wsq=_cast_squares([w[n] for n in SQUARES], "cast_squares"),
              w_kv=_cast_shard(w_kv, "cast_w_kv"), w_up=_cast_shard(w_up, "cast_w_up"),
              w_down=_cast_shard(w_down, "cast_w_down"))
    conv_sh = [jnp.concatenate([conv_a_w, conv_b_w], axis=1), conv_ffn_w]
    small = {n: w[n] for n in REPLICATED + ("norm_final_g",)}

    loss, dx, sgrads, dg_final, pieces = _step(x[0], mem[0], loss_target[0], sh, conv_sh, small, core)

    res = {n: None for n in BIG}
    for l in reversed(range(depth)):
        for n in BIG:
            res[n] = _adam_shard(pieces[l][n], w[n], mom[n], var[n], l, res[n], f"adam_{n}_{l}")

    per_layer = REPLICATED + CONVS
    parts = [sgrads[l][n] for l in range(depth) for n in per_layer] + [dg_final]
    total = _all_reduce_small(_pack(parts), "all_reduce_small")
    shapes = [sgrads[l][n].shape for l in range(depth) for n in per_layer] + [dg_final.shape]
    summed = _unpack(total, shapes)
    g_small = {}
    for k, n in enumerate(per_layer):
        full = jnp.stack([summed[l * len(per_layer) + k] for l in range(depth)])
        if n in CONVS:
            cols = w[n].shape[-1]
            full = lax.dynamic_slice_in_dim(full, chip * cols, cols, axis=2)
        g_small[n] = full.reshape(w[n].shape)
    g_small["norm_final_g"] = summed[-1].reshape(norm_final_g.shape)
    names = per_layer + ("norm_final_g",)
    d_p, m_p, v_p = _adam_small(_pack([g_small[n] for n in names]), _pack([w[n] for n in names]),
                                _pack([mom[n] for n in names]), _pack([var[n] for n in names]), "adam_small")
    shp = [w[n].shape for n in names]
    for n, dl, nm, nv in zip(names, _unpack(d_p, shp), _unpack(m_p, shp), _unpack(v_p, shp)):
        res[n] = (g_small[n], dl, nm, nv)

    loss = lax.psum(loss[0, 0], ("x", "y", "c"))
    return (loss, dx.reshape(x.shape), *[res[n][0] for n in WEIGHTS], *[res[n][1] for n in WEIGHTS],
            *[res[n][2] for n in WEIGHTS], *[res[n][3] for n in WEIGHTS])
```

```python
import functools
import math

import jax
import jax.numpy as jnp
from jax import lax
from jax.experimental import pallas as pl
from jax.experimental.pallas import tpu as pltpu

F32 = jnp.float32
BF = jnp.bfloat16
EPS = 1e-6
N_HEADS = 4
K_A, K_B, K_F = 3, 31, 3
ADAM_LR, ADAM_B1, ADAM_B2, ADAM_EPS, ADAM_WD, ADAM_STEP = 0.001, 0.9, 0.999, 1e-08, 0.01, 10
N_CHIPS = 4
N_DEV = 8
HALO = 32
MAX_STRIPES = 8
TM_ROW = 256
TM_MM = 1024
TT_DW = 4096
DW_LHS_ELEMS = 4 * 1024 * 1024
TR_EW = 128
VMEM_LIMIT = 56 * 1024 * 1024
MESH = pl.DeviceIdType.MESH
_pallas_call = pl.pallas_call


def _params(n_axes):
    return pltpu.CompilerParams(dimension_semantics=("arbitrary",) * n_axes, vmem_limit_bytes=VMEM_LIMIT)


def _resident(shape, index):
    return pl.BlockSpec(shape, lambda *_: index, pipeline_mode=pl.Buffered(1))


def _sig(x):
    return 1.0 / (1.0 + jnp.exp(-x))


def _nt(a, b):
    return lax.dot_general(a, b, (((1,), (1,)), ((), ())), preferred_element_type=F32)


def _tn(a, b):
    return lax.dot_general(a, b, (((0,), (0,)), ((), ())), preferred_element_type=F32)


def _nn(a, b):
    return jnp.dot(a, b, preferred_element_type=F32)


class _Comm:
    def __init__(self, inputs, out_shape, scratch, start, finish, aliases=None):
        self.inputs, self.out_shape, self.scratch = list(inputs), list(out_shape), list(scratch)
        self.start, self.finish = start, finish
        self.aliases = dict(aliases or {})


def _join(programs):
    programs = [p for p in programs if p is not None]
    if not programs:
        return None

    def split(seq, counts):
        parts, at = [], 0
        for n in counts:
            parts.append(seq[at:at + n])
            at += n
        return parts

    n_in = [len(p.inputs) for p in programs]
    n_out = [len(p.out_shape) for p in programs]
    n_s = [len(p.scratch) for p in programs]

    def phase(which):
        def run(ins, outs, sems):
            for p, i, o, s in zip(programs, split(ins, n_in), split(outs, n_out), split(sems, n_s)):
                getattr(p, which)(i, o, s)
        return run

    aliases, in_at, out_at = {}, 0, 0
    for p, i, o in zip(programs, n_in, n_out):
        aliases.update({in_at + a: out_at + b for a, b in p.aliases.items()})
        in_at, out_at = in_at + i, out_at + o
    return _Comm([a for p in programs for a in p.inputs], [a for p in programs for a in p.out_shape],
                 [a for p in programs for a in p.scratch], phase("start"), phase("finish"), aliases)


def _run(body, comm, *, name, grid, in_specs, out_specs, out_shape, args, scratch_shapes=()):
    n_axes = len(grid)
    if comm is None:
        outs = _pallas_call(body, name=name, grid=grid, in_specs=list(in_specs), out_specs=list(out_specs),
                            out_shape=list(out_shape), scratch_shapes=list(scratch_shapes),
                            compiler_params=_params(n_axes))(*args)
        return list(outs), []
    counts = (len(in_specs), len(comm.inputs), len(out_specs), len(comm.out_shape), len(scratch_shapes),
              len(comm.scratch))

    def hosted(*refs):
        parts, at = [], 0
        for n in counts:
            parts.append(refs[at:at + n])
            at += n
        ins, c_ins, outs, c_outs, scr, c_sems = parts
        ids = [pl.program_id(a) for a in range(n_axes)]
        first = functools.reduce(jnp.logical_and, [i == 0 for i in ids])
        last = functools.reduce(jnp.logical_and, [i == g - 1 for i, g in zip(ids, grid)])

        @pl.when(first)
        def _():
            comm.start(c_ins, c_outs, c_sems)

        body(*ins, *outs, *scr)

        @pl.when(last)
        def _():
            comm.finish(c_ins, c_outs, c_sems)

    any_spec = pl.BlockSpec(memory_space=pl.ANY)
    res = _pallas_call(
        hosted, name=name, grid=grid, in_specs=list(in_specs) + [any_spec] * counts[1],
        out_specs=list(out_specs) + [any_spec] * counts[3], out_shape=list(out_shape) + comm.out_shape,
        scratch_shapes=list(scratch_shapes) + comm.scratch, compiler_params=_params(n_axes),
        input_output_aliases={counts[0] + a: counts[2] + b for a, b in comm.aliases.items()},
    )(*args, *comm.inputs)
    return list(res[:counts[2]]), list(res[counts[2]:])


def _run_comm(comm, name):
    n_in, n_out = len(comm.inputs), len(comm.out_shape)

    def body(*refs):
        ins, outs, sems = refs[:n_in], refs[n_in:n_in + n_out], refs[n_in + n_out:]
        comm.start(ins, outs, sems)
        comm.finish(ins, outs, sems)

    any_spec = pl.BlockSpec(memory_space=pl.ANY)
    return list(_pallas_call(body, name=name, in_specs=[any_spec] * n_in, out_specs=[any_spec] * n_out,
                             out_shape=comm.out_shape, scratch_shapes=comm.scratch,
                             input_output_aliases=comm.aliases)(*comm.inputs))


def _causal_taps(xx_ref, w_ref, k_taps, tm):
    base = HALO - k_taps + 1
    acc = w_ref[0:1, :] * xx_ref[pl.ds(base, tm), :]
    for k in range(1, k_taps):
        acc = acc + w_ref[k:k + 1, :] * xx_ref[pl.ds(base + k, tm), :]
    return acc


def _anticausal_taps(yy_ref, w_ref, k_taps, tm):
    acc = w_ref[k_taps - 1:k_taps, :] * yy_ref[pl.ds(0, tm), :]
    for k in range(k_taps - 1):
        acc = acc + w_ref[k:k + 1, :] * yy_ref[pl.ds(k_taps - 1 - k, tm), :]
    return acc


def _tap_grads(dw_ref, dy, xx_ref, k_taps, tm):
    base = HALO - k_taps + 1
    for k in range(k_taps):
        dw_ref[k:k + 1, :] += jnp.sum(dy * xx_ref[pl.ds(base + k, tm), :], axis=0, keepdims=True)


def _prev_halo(tm, col):
    return lambda i: (jnp.maximum(i * (tm // HALO) - 1, 0), col)


def _next_halo(tm, n_rows, col):
    return lambda i: (jnp.minimum((i + 1) * (tm // HALO), n_rows // HALO - 1), col)


def _norm_matmul(x, g, w, gs, name, comm=None):
    t, d = x.shape
    n_s, _, ns = w.shape
    n = n_s * ns
    tm = min(TM_MM, t)

    def body(x_ref, g_ref, w_ref, h_ref, y_ref):
        @pl.when(pl.program_id(1) == 0)
        def _():
            xf = x_ref[...]
            r = lax.rsqrt(jnp.mean(xf * xf, axis=-1, keepdims=True) + EPS)
            h_ref[...] = ((xf * r) * g_ref[...]).astype(BF)

        for s in range(gs):
            y_ref[:, s * ns:(s + 1) * ns] = _nn(h_ref[...], w_ref[s]).astype(BF)

    return _run(
        body, comm, name=name, grid=(t // tm, n_s // gs),
        in_specs=[pl.BlockSpec((tm, d), lambda i, j: (i, 0)),
                  pl.BlockSpec((1, d), lambda i, j: (0, 0)),
                  pl.BlockSpec((gs, d, ns), lambda i, j: (j, 0, 0))],
        out_specs=[pl.BlockSpec((tm, d), lambda i, j: (i, 0)),
                   pl.BlockSpec((tm, gs * ns), lambda i, j: (i, j))],
        out_shape=[jax.ShapeDtypeStruct((t, d), BF), jax.ShapeDtypeStruct((t, n), BF)],
        args=(x, g, w))


def _mem_kv(mem, g, w_kv, name):
    m, d = mem.shape
    n_s, _, ns = w_kv.shape

    def body(mem_ref, g_ref, w_ref, memn_ref, kv_ref):
        xf = mem_ref[...]
        r = lax.rsqrt(jnp.mean(xf * xf, axis=-1, keepdims=True) + EPS)
        memn = ((xf * r) * g_ref[...]).astype(BF)
        memn_ref[...] = memn
        for s in range(n_s):
            kv_ref[:, s * ns:(s + 1) * ns] = _nn(memn, w_ref[s]).astype(BF)

    return _pallas_call(
        body, name=name, grid=(1,),
        in_specs=[pl.BlockSpec((m, d), lambda i: (0, 0)),
                  pl.BlockSpec((1, d), lambda i: (0, 0)),
                  pl.BlockSpec((n_s, d, ns), lambda i: (0, 0, 0))],
        out_specs=[pl.BlockSpec((m, d), lambda i: (0, 0)),
                   pl.BlockSpec((m, 2 * d), lambda i: (0, 0))],
        out_shape=[jax.ShapeDtypeStruct((m, d), BF), jax.ShapeDtypeStruct((m, 2 * d), BF)],
        compiler_params=_params(1),
    )(mem, g, w_kv)


def _load_branch_inputs(i, proj_ref, gch_ref, vh_ref, u0h_ref, ugh_ref, xa_ref, xb_ref, d):
    gc = proj_ref[:, d:2 * d].astype(F32)
    v = proj_ref[:, 2 * d:3 * d].astype(F32)
    u0 = proj_ref[:, 3 * d:4 * d].astype(F32)
    ug = proj_ref[:, 4 * d:5 * d].astype(F32)
    keep = (i > 0).astype(F32)
    xa_ref[pl.ds(0, HALO), :] = gch_ref[...].astype(F32) * vh_ref[...].astype(F32) * keep
    xa_ref[pl.ds(HALO, gc.shape[0]), :] = gc * v
    xb_ref[pl.ds(0, HALO), :] = u0h_ref[...].astype(F32) * _sig(ugh_ref[...].astype(F32)) * keep
    xb_ref[pl.ds(HALO, gc.shape[0]), :] = u0 * _sig(ug)


def _softmax_rows(s):
    e = jnp.exp(s - jnp.max(s, axis=-1, keepdims=True))
    return e / jnp.sum(e, axis=-1, keepdims=True)


def _mixer_fwd(proj, x, kv, conv_a, conv_b, cbias, ln_g, ln_b, b_gate, wsq, l, name, comm=None):
    t, d = x.shape
    m = kv.shape[0]
    tm = min(TM_ROW, t)
    hd = d // N_HEADS
    scale = 1.0 / math.sqrt(hd)

    def body(proj_ref, gch_ref, vh_ref, u0h_ref, ugh_ref, x_ref, kv_ref, ca_w, cb_w, cbias_ref, lng_ref, lnb_ref,
             bg_ref, wa_ref, wb_ref, wc_ref, wo_ref,
             x1_ref, za_ref, zb_ref, o_ref, ya_ref, yb_ref, yc_ref, mg_ref, cb_ref, xa_ref, xb_ref):
        i = pl.program_id(0)
        _load_branch_inputs(i, proj_ref, gch_ref, vh_ref, u0h_ref, ugh_ref, xa_ref, xb_ref, d)
        gb = proj_ref[:, 0:d].astype(F32)
        za = (gb * _causal_taps(xa_ref, ca_w, K_A, tm)).astype(BF)
        za_ref[...] = za
        ya = _nn(za, wa_ref[...])
        ya_ref[...] = ya.astype(BF)
        cb = _causal_taps(xb_ref, cb_w, K_B, tm) + cbias_ref[...]
        cb_ref[...] = cb
        mu = jnp.mean(cb, axis=-1, keepdims=True)
        dlt = cb - mu
        rstd = lax.rsqrt(jnp.mean(dlt * dlt, axis=-1, keepdims=True) + EPS)
        lnb = (dlt * rstd) * lng_ref[...] + lnb_ref[...]
        zb = (lnb * _sig(lnb)).astype(BF)
        zb_ref[...] = zb
        yb = _nn(zb, wb_ref[...])
        yb_ref[...] = yb.astype(BF)
        for h in range(N_HEADS):
            qh = proj_ref[:, 5 * d + h * hd:5 * d + (h + 1) * hd]
            kh = kv_ref[:, h * hd:(h + 1) * hd]
            vh = kv_ref[:, d + h * hd:d + (h + 1) * hd]
            p = _softmax_rows(_nt(qh, kh) * scale)
            o_ref[:, h * hd:(h + 1) * hd] = _nn(p.astype(BF), vh).astype(BF)
        yc = _nn(o_ref[...], wc_ref[...])
        yc_ref[...] = yc.astype(BF)
        g0 = _sig(proj_ref[:, 6 * d:7 * d].astype(F32) + bg_ref[:, 0:d])
        g1 = _sig(proj_ref[:, 7 * d:8 * d].astype(F32) + bg_ref[:, d:2 * d])
        g2 = _sig(proj_ref[:, 8 * d:9 * d].astype(F32) + bg_ref[:, 2 * d:3 * d])
        mg = (g0 * ya + g1 * yb + g2 * yc).astype(BF)
        mg_ref[...] = mg
        x1_ref[...] = x_ref[...] + _nn(mg, wo_ref[...])

    row = lambda w_: pl.BlockSpec((tm, w_), lambda i: (i, 0))
    halo = lambda col: pl.BlockSpec((HALO, d), _prev_halo(tm, col))
    sq = lambda which: _resident((None, None, d, d), (l, which, 0, 0))
    act = jax.ShapeDtypeStruct((t, d), BF)
    return _run(
        body, comm, name=name, grid=(t // tm,),
        in_specs=[row(9 * d), halo(1), halo(2), halo(3), halo(4), row(d),
                  _resident((m, 2 * d), (0, 0)),
                  _resident((None, K_A, d), (l, 0, 0)), _resident((None, K_B, d), (l, 0, 0)),
                  _resident((1, d), (0, 0)), _resident((1, d), (0, 0)), _resident((1, d), (0, 0)),
                  _resident((1, 3 * d), (0, 0)), sq(0), sq(1), sq(2), sq(3)],
        out_specs=[row(d)] * 9,
        out_shape=[jax.ShapeDtypeStruct((t, d), F32)] + [act] * 7 + [jax.ShapeDtypeStruct((t, d), F32)],
        scratch_shapes=[pltpu.VMEM((HALO + tm, d), F32), pltpu.VMEM((HALO + tm, d), F32)],
        args=(proj, proj, proj, proj, proj, x, kv, conv_a, conv_b, cbias, ln_g, ln_b, b_gate, wsq, wsq, wsq, wsq))


def _ffn_down_fwd(up, x1, conv_f, w_down, l, name, comm=None):
    t, d = x1.shape
    f2 = up.shape[1]
    f = f2 // 2
    tm = min(TM_ROW, t)

    def body(up_ref, uph_ref, x1_ref, cw_ref, wd_ref, x2_ref, zf_ref, xx_ref):
        i = pl.program_id(0)
        xx_ref[pl.ds(0, HALO), :] = uph_ref[...].astype(F32) * (i > 0).astype(F32)
        xx_ref[pl.ds(HALO, tm), :] = up_ref[...].astype(F32)
        uc = _causal_taps(xx_ref, cw_ref, K_F, tm)
        gt = uc[:, 0:f]
        zf = (gt * _sig(gt) * uc[:, f:f2]).astype(BF)
        zf_ref[...] = zf
        x2_ref[...] = x1_ref[...] + _nn(zf, wd_ref[...])

    return _run(
        body, comm, name=name, grid=(t // tm,),
        in_specs=[pl.BlockSpec((tm, f2), lambda i: (i, 0)),
                  pl.BlockSpec((HALO, f2), _prev_halo(tm, 0)),
                  pl.BlockSpec((tm, d), lambda i: (i, 0)),
                  _resident((None, K_F, f2), (l, 0, 0)),
                  _resident((None, f, d), (l, 0, 0))],
        out_specs=[pl.BlockSpec((tm, d), lambda i: (i, 0)), pl.BlockSpec((tm, f), lambda i: (i, 0))],
        out_shape=[jax.ShapeDtypeStruct((t, d), F32), jax.ShapeDtypeStruct((t, f), BF)],
        scratch_shapes=[pltpu.VMEM((HALO + tm, f2), F32)],
        args=(up, up, x1, conv_f, w_down))


def _final_loss(x, g, target, name):
    t, d = x.shape
    tm = min(2 * TM_ROW, t)

    def body(x_ref, g_ref, t_ref, dx_ref, loss_ref, dg_ref):
        @pl.when(pl.program_id(0) == 0)
        def _():
            loss_ref[...] = jnp.zeros_like(loss_ref)
            dg_ref[...] = jnp.zeros_like(dg_ref)

        xf = x_ref[...]
        r = lax.rsqrt(jnp.mean(xf * xf, axis=-1, keepdims=True) + EPS)
        xhat = xf * r
        err = xhat * g_ref[...] - t_ref[...]
        loss_ref[...] += (0.5 / d) * jnp.sum(err * err)
        dy = err * (1.0 / d)
        dg_ref[...] += jnp.sum(dy * xhat, axis=0, keepdims=True)
        dxh = dy * g_ref[...]
        dx_ref[...] = r * (dxh - xhat * jnp.mean(dxh * xhat, axis=-1, keepdims=True))

    return _pallas_call(
        body, name=name, grid=(t // tm,),
        in_specs=[pl.BlockSpec((tm, d), lambda i: (i, 0)), pl.BlockSpec((1, d), lambda i: (0, 0)),
                  pl.BlockSpec((tm, d), lambda i: (i, 0))],
        out_specs=[pl.BlockSpec((tm, d), lambda i: (i, 0)), pl.BlockSpec((8, 128), lambda i: (0, 0)),
                   pl.BlockSpec((1, d), lambda i: (0, 0))],
        out_shape=[jax.ShapeDtypeStruct((t, d), F32), jax.ShapeDtypeStruct((8, 128), F32),
                   jax.ShapeDtypeStruct((1, d), F32)],
        compiler_params=_params(1),
    )(x, g, target)


def _ffn_down_bwd(dx2, up, conv_f, w_down, l, name, comm=None):
    t, d = dx2.shape
    f2 = up.shape[1]
    f = f2 // 2
    tm = min(TM_ROW, t)

    def body(dx2_ref, up_ref, uph_ref, cw_ref, wd_ref, duc_ref, dx2b_ref, dcw_ref, xx_ref):
        i = pl.program_id(0)

        @pl.when(i == 0)
        def _():
            dcw_ref[...] = jnp.zeros_like(dcw_ref)

        xx_ref[pl.ds(0, HALO), :] = uph_ref[...].astype(F32) * (i > 0).astype(F32)
        xx_ref[pl.ds(HALO, tm), :] = up_ref[...].astype(F32)
        uc = _causal_taps(xx_ref, cw_ref, K_F, tm)
        gt = uc[:, 0:f]
        sg = _sig(gt)
        dx2b = dx2_ref[...].astype(BF)
        dx2b_ref[...] = dx2b
        dzf = _nt(dx2b, wd_ref[...])
        duc_ref[:, 0:f] = (dzf * uc[:, f:f2] * (sg * (1.0 + gt * (1.0 - sg)))).astype(BF)
        duc_ref[:, f:f2] = (dzf * (gt * sg)).astype(BF)
        _tap_grads(dcw_ref, duc_ref[...].astype(F32), xx_ref, K_F, tm)

    return _run(
        body, comm, name=name, grid=(t // tm,),
        in_specs=[pl.BlockSpec((tm, d), lambda i: (i, 0)),
                  pl.BlockSpec((tm, f2), lambda i: (i, 0)),
                  pl.BlockSpec((HALO, f2), _prev_halo(tm, 0)),
                  _resident((None, K_F, f2), (l, 0, 0)),
                  _resident((None, f, d), (l, 0, 0))],
        out_specs=[pl.BlockSpec((tm, f2), lambda i: (i, 0)), pl.BlockSpec((tm, d), lambda i: (i, 0)),
                   pl.BlockSpec((K_F, f2), lambda i: (0, 0))],
        out_shape=[jax.ShapeDtypeStruct((t, f2), BF), jax.ShapeDtypeStruct((t, d), BF),
                   jax.ShapeDtypeStruct((K_F, f2), F32)],
        scratch_shapes=[pltpu.VMEM((HALO + tm, f2), F32)],
        args=(dx2, up, up, conv_f, w_down))


def _ffn_conv_bwd(duc, conv_f, l, name, comm=None):
    t, f2 = duc.shape
    tm = min(TM_ROW, t)
    n_t = t // tm

    def body(duc_ref, nxt_ref, cw_ref, dup_ref, yy_ref):
        i = pl.program_id(0)
        yy_ref[pl.ds(0, tm), :] = duc_ref[...].astype(F32)
        yy_ref[pl.ds(tm, HALO), :] = nxt_ref[...].astype(F32) * (i < n_t - 1).astype(F32)
        dup_ref[...] = _anticausal_taps(yy_ref, cw_ref, K_F, tm).astype(BF)

    return _run(
        body, comm, name=name, grid=(n_t,),
        in_specs=[pl.BlockSpec((tm, f2), lambda i: (i, 0)),
                  pl.BlockSpec((HALO, f2), _next_halo(tm, t, 0)),
                  _resident((None, K_F, f2), (l, 0, 0))],
        out_specs=[pl.BlockSpec((tm, f2), lambda i: (i, 0))],
        out_shape=[jax.ShapeDtypeStruct((t, f2), BF)],
        scratch_shapes=[pltpu.VMEM((tm + HALO, f2), F32)],
        args=(duc, duc, conv_f))


def _nt_matmul_norm_bwd(dy, w, gs, x, g, dres, name, comm=None):
    t, n = dy.shape
    d = x.shape[1]
    tm = min(TM_MM // 2, t)
    n_s, _, ns = w.shape
    n_k, tk = n_s // gs, gs * ns

    def body(dy_ref, w_ref, x_ref, g_ref, dres_ref, dx_ref, dxb_ref, dg_ref, acc_ref):
        i, k = pl.program_id(0), pl.program_id(1)

        @pl.when((i == 0) & (k == 0))
        def _():
            dg_ref[...] = jnp.zeros_like(dg_ref)

        @pl.when(k == 0)
        def _():
            acc_ref[...] = jnp.zeros_like(acc_ref)

        part = _nt(dy_ref[:, 0:ns], w_ref[0])
        for s in range(1, gs):
            part = part + _nt(dy_ref[:, s * ns:(s + 1) * ns], w_ref[s])
        acc_ref[...] += part

        @pl.when(k == n_k - 1)
        def _():
            xf = x_ref[...]
            r = lax.rsqrt(jnp.mean(xf * xf, axis=-1, keepdims=True) + EPS)
            xhat = xf * r
            dh = acc_ref[...]
            dg_ref[...] += jnp.sum(dh * xhat, axis=0, keepdims=True)
            dxh = dh * g_ref[...]
            dx = dres_ref[...] + r * (dxh - xhat * jnp.mean(dxh * xhat, axis=-1, keepdims=True))
            dx_ref[...] = dx
            dxb_ref[...] = dx.astype(BF)

    return _run(
        body, comm, name=name, grid=(t // tm, n_k),
        in_specs=[pl.BlockSpec((tm, tk), lambda i, k: (i, k)),
                  pl.BlockSpec((gs, d, ns), lambda i, k: (k, 0, 0)),
                  pl.BlockSpec((tm, d), lambda i, k: (i, 0)),
                  pl.BlockSpec((1, d), lambda i, k: (0, 0)),
                  pl.BlockSpec((tm, d), lambda i, k: (i, 0))],
        out_specs=[pl.BlockSpec((tm, d), lambda i, k: (i, 0)), pl.BlockSpec((tm, d), lambda i, k: (i, 0)),
                   pl.BlockSpec((1, d), lambda i, k: (0, 0))],
        out_shape=[jax.ShapeDtypeStruct((t, d), F32), jax.ShapeDtypeStruct((t, d), BF),
                   jax.ShapeDtypeStruct((1, d), F32)],
        scratch_shapes=[pltpu.VMEM((tm, d), F32)],
        args=(dy, w, x, g, dres))


def _mixer_bwd(dx1b, proj, ya, yb, yc, cb, kv, conv_a, conv_b, ln_g, ln_b, b_gate, wsq, l, name, comm=None):
    t, d = cb.shape
    m = kv.shape[0]
    tm = min(TM_ROW, t)
    hd = d // N_HEADS
    scale = 1.0 / math.sqrt(hd)

    def body(dx1b_ref, proj_ref, gch_ref, vh_ref, u0h_ref, ugh_ref, ya_ref, yb_ref, yc_ref, cb_ref, kv_ref,
             ca_w, cb_w, lng_ref, lnb_ref, bg_ref, wa_ref, wb_ref, wc_ref, wo_ref,
             dpre_ref, dya_ref, dyb_ref, dyc_ref, dkv_ref, dbg_ref, dlng_ref, dlnb_ref, dcbias_ref, dcaw_ref,
             dcbw_ref, xa_ref, xb_ref):
        i = pl.program_id(0)

        @pl.when(i == 0)
        def _():
            for ref in (dkv_ref, dbg_ref, dlng_ref, dlnb_ref, dcbias_ref, dcaw_ref, dcbw_ref):
                ref[...] = jnp.zeros_like(ref)

        _load_branch_inputs(i, proj_ref, gch_ref, vh_ref, u0h_ref, ugh_ref, xa_ref, xb_ref, d)
        dmg = _nt(dx1b_ref[...], wo_ref[...])
        ys = (ya_ref, yb_ref, yc_ref)
        dys = (dya_ref, dyb_ref, dyc_ref)
        for b in range(3):
            gate = _sig(proj_ref[:, (6 + b) * d:(7 + b) * d].astype(F32) + bg_ref[:, b * d:(b + 1) * d])
            dys[b][...] = (gate * dmg).astype(BF)
            dpg = dmg * ys[b][...].astype(F32) * gate * (1.0 - gate)
            dpre_ref[:, (6 + b) * d:(7 + b) * d] = dpg.astype(BF)
            dbg_ref[:, b * d:(b + 1) * d] += jnp.sum(dpg, axis=0, keepdims=True)
        gb = proj_ref[:, 0:d].astype(F32)
        ca = _causal_taps(xa_ref, ca_w, K_A, tm)
        dza = _nt(dya_ref[...], wa_ref[...])
        dpre_ref[:, 0:d] = (dza * ca).astype(BF)
        dpre_ref[:, d:2 * d] = (dza * gb).astype(BF)
        _tap_grads(dcaw_ref, dpre_ref[:, d:2 * d].astype(F32), xa_ref, K_A, tm)
        dpre_ref[:, 2 * d:3 * d] = jnp.zeros((tm, d), BF)
        cbv = cb_ref[...]
        mu = jnp.mean(cbv, axis=-1, keepdims=True)
        dlt = cbv - mu
        rstd = lax.rsqrt(jnp.mean(dlt * dlt, axis=-1, keepdims=True) + EPS)
        xhat = dlt * rstd
        lnb = xhat * lng_ref[...] + lnb_ref[...]
        sg = _sig(lnb)
        dzb = _nt(dyb_ref[...], wb_ref[...])
        dl = dzb * (sg * (1.0 + lnb * (1.0 - sg)))
        dlng_ref[...] += jnp.sum(dl * xhat, axis=0, keepdims=True)
        dlnb_ref[...] += jnp.sum(dl, axis=0, keepdims=True)
        dxh = dl * lng_ref[...]
        dcb = rstd * (dxh - jnp.mean(dxh, axis=-1, keepdims=True)
                      - xhat * jnp.mean(dxh * xhat, axis=-1, keepdims=True))
        dcbias_ref[...] += jnp.sum(dcb, axis=0, keepdims=True)
        dpre_ref[:, 3 * d:4 * d] = dcb.astype(BF)
        _tap_grads(dcbw_ref, dpre_ref[:, 3 * d:4 * d].astype(F32), xb_ref, K_B, tm)
        dpre_ref[:, 4 * d:5 * d] = jnp.zeros((tm, d), BF)
        do = _nt(dyc_ref[...], wc_ref[...]).astype(BF)
        for h in range(N_HEADS):
            qh = proj_ref[:, 5 * d + h * hd:5 * d + (h + 1) * hd]
            kh = kv_ref[:, h * hd:(h + 1) * hd]
            vh = kv_ref[:, d + h * hd:d + (h + 1) * hd]
            doh = do[:, h * hd:(h + 1) * hd]
            p = _softmax_rows(_nt(qh, kh) * scale)
            dp = _nt(doh, vh)
            ds = (p * (dp - jnp.sum(dp * p, axis=-1, keepdims=True)) * scale).astype(BF)
            dpre_ref[:, 5 * d + h * hd:5 * d + (h + 1) * hd] = _nn(ds, kh).astype(BF)
            dkv_ref[:, h * hd:(h + 1) * hd] += _tn(ds, qh)
            dkv_ref[:, d + h * hd:d + (h + 1) * hd] += _tn(p.astype(BF), doh)

    row = lambda w_: pl.BlockSpec((tm, w_), lambda i: (i, 0))
    halo = lambda col: pl.BlockSpec((HALO, d), _prev_halo(tm, col))
    sq = lambda which: _resident((None, None, d, d), (l, which, 0, 0))
    acc = lambda r, c: pl.BlockSpec((r, c), lambda i: (0, 0))
    act = jax.ShapeDtypeStruct((t, d), BF)
    vec = lambda r, c: jax.ShapeDtypeStruct((r, c), F32)
    return _run(
        body, comm, name=name, grid=(t // tm,),
        in_specs=[row(d), row(9 * d), halo(1), halo(2), halo(3), halo(4), row(d), row(d), row(d), row(d),
                  _resident((m, 2 * d), (0, 0)),
                  _resident((None, K_A, d), (l, 0, 0)), _resident((None, K_B, d), (l, 0, 0)),
                  _resident((1, d), (0, 0)), _resident((1, d), (0, 0)), _resident((1, 3 * d), (0, 0)),
                  sq(0), sq(1), sq(2), sq(3)],
        out_specs=[row(9 * d), row(d), row(d), row(d), acc(m, 2 * d), acc(1, 3 * d), acc(1, d), acc(1, d),
                   acc(1, d), acc(K_A, d), acc(K_B, d)],
        out_shape=[jax.ShapeDtypeStruct((t, 9 * d), BF), act, act, act, vec(m, 2 * d), vec(1, 3 * d), vec(1, d),
                   vec(1, d), vec(1, d), vec(K_A, d), vec(K_B, d)],
        scratch_shapes=[pltpu.VMEM((HALO + tm, d), F32), pltpu.VMEM((HALO + tm, d), F32)],
        args=(dx1b, proj, proj, proj, proj, proj, ya, yb, yc, cb, kv, conv_a, conv_b, ln_g, ln_b, b_gate,
              wsq, wsq, wsq, wsq))


def _inproj_conv_bwd(dpre, proj, conv_a, conv_b, l, name, comm=None):
    t, d9 = dpre.shape
    d = d9 // 9
    tm = min(TM_ROW, t)
    n_t = t // tm

    def body(dpre_ref, nxa_ref, nxb_ref, proj_ref, ca_w, cb_w, dproj_ref, ya_ref, yb_ref):
        i = pl.program_id(0)
        keep = (i < n_t - 1).astype(F32)
        ya_ref[pl.ds(0, tm), :] = dpre_ref[:, d:2 * d].astype(F32)
        ya_ref[pl.ds(tm, HALO), :] = nxa_ref[...].astype(F32) * keep
        yb_ref[pl.ds(0, tm), :] = dpre_ref[:, 3 * d:4 * d].astype(F32)
        yb_ref[pl.ds(tm, HALO), :] = nxb_ref[...].astype(F32) * keep
        dproj_ref[:, 0:d] = dpre_ref[:, 0:d]
        dproj_ref[:, 5 * d:9 * d] = dpre_ref[:, 5 * d:9 * d]
        dcv = _anticausal_taps(ya_ref, ca_w, K_A, tm)
        dproj_ref[:, d:2 * d] = (dcv * proj_ref[:, 2 * d:3 * d].astype(F32)).astype(BF)
        dproj_ref[:, 2 * d:3 * d] = (dcv * proj_ref[:, d:2 * d].astype(F32)).astype(BF)
        dub = _anticausal_taps(yb_ref, cb_w, K_B, tm)
        sg = _sig(proj_ref[:, 4 * d:5 * d].astype(F32))
        dproj_ref[:, 3 * d:4 * d] = (dub * sg).astype(BF)
        dproj_ref[:, 4 * d:5 * d] = (dub * proj_ref[:, 3 * d:4 * d].astype(F32) * sg * (1.0 - sg)).astype(BF)

    return _run(
        body, comm, name=name, grid=(n_t,),
        in_specs=[pl.BlockSpec((tm, d9), lambda i: (i, 0)),
                  pl.BlockSpec((HALO, d), _next_halo(tm, t, 1)),
                  pl.BlockSpec((HALO, d), _next_halo(tm, t, 3)),
                  pl.BlockSpec((tm, d9), lambda i: (i, 0)),
                  _resident((None, K_A, d), (l, 0, 0)), _resident((None, K_B, d), (l, 0, 0))],
        out_specs=[pl.BlockSpec((tm, d9), lambda i: (i, 0))],
        out_shape=[jax.ShapeDtypeStruct((t, d9), BF)],
        scratch_shapes=[pltpu.VMEM((tm + HALO, d), F32), pltpu.VMEM((tm + HALO, d), F32)],
        args=(dpre, dpre, dpre, proj, conv_a, conv_b))


def _mem_kv_bwd(dkv, memn, mem, g, w_kv, name):
    m, d = mem.shape
    n_s, _, ns = w_kv.shape

    def body(dkv_ref, memn_ref, mem_ref, g_ref, w_ref, dw_ref, dg_ref):
        dkvb = dkv_ref[...].astype(BF)
        dw_ref[...] = _tn(memn_ref[...], dkvb)
        dmemn = _nt(dkvb[:, 0:ns], w_ref[0])
        for s in range(1, n_s):
            dmemn = dmemn + _nt(dkvb[:, s * ns:(s + 1) * ns], w_ref[s])
        xf = mem_ref[...]
        r = lax.rsqrt(jnp.mean(xf * xf, axis=-1, keepdims=True) + EPS)
        dg_ref[...] = jnp.sum(dmemn * (xf * r), axis=0, keepdims=True)

    return _pallas_call(
        body, name=name, grid=(1,),
        in_specs=[pl.BlockSpec((m, 2 * d), lambda i: (0, 0)), pl.BlockSpec((m, d), lambda i: (0, 0)),
                  pl.BlockSpec((m, d), lambda i: (0, 0)), pl.BlockSpec((1, d), lambda i: (0, 0)),
                  pl.BlockSpec((n_s, d, ns), lambda i: (0, 0, 0))],
        out_specs=[pl.BlockSpec((d, 2 * d), lambda i: (0, 0)), pl.BlockSpec((1, d), lambda i: (0, 0))],
        out_shape=[jax.ShapeDtypeStruct((d, 2 * d), F32), jax.ShapeDtypeStruct((1, d), F32)],
        compiler_params=_params(1),
    )(dkv, memn, mem, g, w_kv)


def _dw_matmul(a, b, tn, name, comm=None):
    t, k = a.shape
    n = b.shape[1]
    tt = min(TT_DW, t)
    while tt * k > DW_LHS_ELEMS and tt % 2 == 0:
        tt //= 2

    def body(a_ref, b_ref, o_ref):
        @pl.when(pl.program_id(1) == 0)
        def _():
            o_ref[...] = jnp.zeros_like(o_ref)

        o_ref[...] += _tn(a_ref[...], b_ref[...])

    return _run(
        body, comm, name=name, grid=(n // tn, t // tt),
        in_specs=[pl.BlockSpec((tt, k), lambda j, s: (s, 0)), pl.BlockSpec((tt, tn), lambda j, s: (s, j))],
        out_specs=[pl.BlockSpec((k, tn), lambda j, s: (0, j))],
        out_shape=[jax.ShapeDtypeStruct((k, n), F32)],
        args=(a, b))


class _GradReduce:
    def __init__(self, l, grads, core):
        self.l, self.core, self.names = l, core, tuple(grads)
        self.views = {n: _halves_view(n, g) for n, g in grads.items()}
        self.got, self.sums, self.pieces = {}, {}, {}

    def swap_program(self):
        return _swap_program([self.views[n] for n in self.names])

    def swapped(self, outs):
        self.got = dict(zip(self.names, outs))
        self.sums = {n: _add_halves(self.views[n], self.got[n], self.core, f"add_halves_{n}_{self.l}")
                     for n in self.names}

    def scatter_program(self, names=None):
        names = self.names if names is None else names
        return _scatter_program([self.sums[n] for n in names], names)

    def scattered(self, outs, names=None):
        self.pieces.update(zip(self.names if names is None else names, outs))


def _step(x, mem, target, sh, conv_sh, small, core):
    depth = sh["w_in"].shape[0]
    d = x.shape[1]
    f2 = sh["w_up"].shape[2] * N_CHIPS
    tn_dw_in = min(1024, d)
    tn_dw_up = f2 // 11 if f2 % (11 * 128) == 0 and f2 // 11 >= 128 else f2
    row = lambda v: v.reshape(1, -1)
    one = lambda a: a[None]

    w_in, cab, cf = _run_comm(_gather_program([sh["w_in"]] + conv_sh, ("col", "small", "small"), (0, None, None)),
                              "gather_first")
    saved = []
    for l in range(depth):
        conv = dict(a=cab[l:l + 1, :K_A], b=cab[l:l + 1, K_A:], f=cf[l:l + 1])
        (h, proj), (wsq, w_kv) = _norm_matmul(
            x, row(small["norm_mix_g"][l]), w_in, 1, f"in_proj_{l}",
            _gather_program([sh["wsq"], sh["w_kv"]], ("row", "col"), (l, l)))
        memn, kv = _mem_kv(mem, row(small["norm_mem_g"][l]), w_kv, f"mem_kv_{l}")
        (x1, za, zb, o, ya, yb, yc, mg, cb), (w_up, w_down) = _mixer_fwd(
            proj, x, kv, conv["a"], conv["b"], row(small["conv_b_bias"][l]), row(small["ln_b_g"][l]),
            row(small["ln_b_b"][l]), row(small["b_gate"][l]), one(wsq), 0, f"mixer_fwd_{l}",
            _gather_program([sh["w_up"], sh["w_down"]], ("col", "row"), (l, l)))
        more = l + 1 < depth
        nxt = _gather_program([sh["w_in"]], ("col",), (l + 1,), (0, 2)) if more else None
        (h2, up), w_in_next = _norm_matmul(x1, row(small["norm_ffn_g"][l]), w_up, 2, f"up_proj_{l}", nxt)
        nxt = _gather_program([sh["w_in"]], ("col",), (l + 1,), (1, 2), w_in_next) if more else None
        (x2, zf), w_in_next = _ffn_down_fwd(up, x1, conv["f"], one(w_down), 0, f"ffn_down_fwd_{l}", nxt)
        saved.append(dict(x=x, x1=x1, memn=memn, kv=kv, h=h, proj=proj, za=za, zb=zb, o=o, ya=ya, yb=yb, yc=yc,
                          mg=mg, cb=cb, h2=h2, up=up, zf=zf, w_in=w_in, wsq=one(wsq), w_kv=w_kv,
                          w_up=w_up, w_down=one(w_down), conv=conv))
        x = x2
        w_in = w_in_next[0] if w_in_next else None
    dx, loss, dg_final = _final_loss(x, row(small["norm_final_g"]), target, "final_loss")
    sgrads, pieces = [None] * depth, [None] * depth
    above = None
    first, second, rest = ("w_up",), ("w_down",) + SQUARES, ("w_in", "w_kv")
    for l in reversed(range(depth)):
        s = saved[l]
        conv = s["conv"]
        bottom = l == 0
        (duc, dx2b, dconv_f), got = _ffn_down_bwd(dx, s["up"], conv["f"], s["w_down"], 0, f"ffn_down_bwd_{l}",
                                                  above and above.swap_program())
        if above:
            above.swapped(got)
        (dup,), got = _ffn_conv_bwd(duc, conv["f"], 0, f"ffn_conv_bwd_{l}", above and above.scatter_program(first))
        if above:
            above.scattered(got, first)
        (dx1, dx1b, dg_ffn), got = _nt_matmul_norm_bwd(
            dup, s["w_up"], 2, s["x1"], row(small["norm_ffn_g"][l]), dx, f"up_proj_bwd_{l}",
            above and above.scatter_program(second))
        if above:
            above.scattered(got, second)
        grads = dict(w_up=_dw_matmul(s["h2"], dup, tn_dw_up, f"dw_up_{l}")[0][0],
                     w_down=_dw_matmul(s["zf"], dx2b, d, f"dw_down_{l}")[0][0])
        ffn = _GradReduce(l, grads, core) if bottom else None
        (dpre, dya, dyb, dyc, dkv, dbg, dlng, dlnb, dcbias, dconv_a, dconv_b), got = _mixer_bwd(
            dx1b, s["proj"], s["ya"], s["yb"], s["yc"], s["cb"], s["kv"], conv["a"], conv["b"],
            row(small["ln_b_g"][l]), row(small["ln_b_b"][l]), row(small["b_gate"][l]), s["wsq"], 0,
            f"mixer_bwd_{l}", _join([above and above.scatter_program(rest), ffn and ffn.swap_program()]))
        if above:
            above.scattered(got[:len(rest)], rest)
            pieces[above.l] = above.pieces
            got = got[len(rest):]
        if ffn:
            ffn.swapped(got)
        dw_kv, dg_mem = _mem_kv_bwd(dkv, s["memn"], mem, row(small["norm_mem_g"][l]), s["w_kv"], f"mem_kv_bwd_{l}")
        mix_grads = dict(w_a_out=_dw_matmul(s["za"], dya, d, f"dw_a_out_{l}")[0][0],
                         w_b_out=_dw_matmul(s["zb"], dyb, d, f"dw_b_out_{l}")[0][0],
                         w_att_out=_dw_matmul(s["o"], dyc, d, f"dw_att_out_{l}")[0][0],
                         w_o=_dw_matmul(s["mg"], dx1b, d, f"dw_o_{l}")[0][0], w_kv=dw_kv)
        mix = _GradReduce(l, mix_grads, core) if bottom else None
        (dproj,), got = _inproj_conv_bwd(dpre, s["proj"], conv["a"], conv["b"], 0, f"inproj_conv_bwd_{l}",
                                         _join([ffn and ffn.scatter_program(), mix and mix.swap_program()]))
        if bottom:
            ffn.scattered(got[:len(ffn.names)])
            mix.swapped(got[len(ffn.names):])
        in_grads = dict(w_in=_dw_matmul(s["h"], dproj, tn_dw_in, f"dw_in_{l}")[0][0])
        inp = _GradReduce(l, in_grads, core) if bottom else None
        (dx0, _, dg_mix), got = _nt_matmul_norm_bwd(
            dproj, s["w_in"], 2, s["x"], row(small["norm_mix_g"][l]), dx1, f"in_proj_bwd_{l}",
            _join([mix and mix.scatter_program(), inp and inp.swap_program()]))
        if bottom:
            mix.scattered(got[:len(mix.names)])
            inp.swapped(got[len(mix.names):])
            inp.scattered(_run_comm(inp.scatter_program(), f"scatter_w_in_{l}"))
            pieces[l] = {**ffn.pieces, **mix.pieces, **inp.pieces}
        else:
            above = _GradReduce(l, {**grads, **mix_grads, **in_grads}, core)
        sgrads[l] = dict(norm_mix_g=dg_mix, norm_mem_g=dg_mem, b_gate=dbg, conv_b_bias=dcbias, ln_b_g=dlng,
                         ln_b_b=dlnb, norm_ffn_g=dg_ffn, conv_a_w=dconv_a, conv_b_w=dconv_b, conv_ffn_w=dconv_f)
        dx = dx0
    return loss, dx, sgrads, dg_final, pieces


BIG = ("w_in", "w_a_out", "w_b_out", "w_att_out", "w_o", "w_kv", "w_up", "w_down")
COL = ("w_in", "w_kv", "w_up")
SQUARES = ("w_a_out", "w_b_out", "w_att_out", "w_o")
ANY = pl.BlockSpec(memory_space=pl.ANY)


def _place():
    x, y, c = lax.axis_index("x"), lax.axis_index("y"), lax.axis_index("c")
    chips = [(1 - x, y), (x, 1 - y), (1 - x, 1 - y)]
    return x, y, c, 2 * x + y, chips


def _remote(src, dst, send_sem, recv_sem, dev):
    return pltpu.make_async_remote_copy(src_ref=src, dst_ref=dst, send_sem=send_sem, recv_sem=recv_sem,
                                        device_id=dev, device_id_type=MESH)


class _Striped:
    def __init__(self, src, dst, make):
        rows = src.shape[-2]
        unit = 8 * (4 // jnp.dtype(src.dtype).itemsize)
        n = max(k for k in range(1, MAX_STRIPES + 1) if rows % (unit * k) == 0) if rows % unit == 0 else 1
        q = rows // n
        self.parts = [make(_window(src, pl.ds(i * q, q), slice(None)), _window(dst, pl.ds(i * q, q), slice(None)))
                      for i in range(n)]
        self.whole = make(src, dst)

    def start(self):
        for p in self.parts:
            p.start()

    def wait(self):
        self.whole.wait()

    def wait_send(self):
        self.whole.wait_send()

    def wait_recv(self):
        self.whole.wait_recv()


def _far(src, dst, send_sem, recv_sem, dev):
    return _Striped(src, dst, lambda s, d: _remote(s, d, send_sem, recv_sem, dev))


def _near(src, dst, sem):
    return _Striped(src, dst, lambda s, d: pltpu.make_async_copy(s, d, sem))


def _window(ref, rows, cols):
    return ref.at[(slice(None),) * (len(ref.shape) - 2) + (rows, cols)]


def _row_tile(rows, cols, unit=16, limit=1 << 20):
    best = unit
    for tr in range(unit, rows + 1, unit):
        if rows % tr == 0 and tr * cols <= limit:
            best = tr
    return best


def _cast_shard(w, name):
    depth, k, n = w.shape

    def body(w_ref, o_ref):
        o_ref[...] = w_ref[...].astype(BF)

    return _pallas_call(
        body, name=name, grid=(depth,),
        in_specs=[pl.BlockSpec((None, k, n), lambda l: (l, 0, 0))],
        out_specs=pl.BlockSpec((None, k, n), lambda l: (l, 0, 0)),
        out_shape=jax.ShapeDtypeStruct((depth, k, n), BF),
        compiler_params=_params(1),
    )(w)


def _cast_squares(ws, name):
    depth, k, n = ws[0].shape

    def body(a_ref, b_ref, c_ref, d_ref, o_ref):
        for i, ref in enumerate((a_ref, b_ref, c_ref, d_ref)):
            o_ref[i] = ref[...].astype(BF)

    return _pallas_call(
        body, name=name, grid=(depth,),
        in_specs=[pl.BlockSpec((None, k, n), lambda l: (l, 0, 0))] * 4,
        out_specs=pl.BlockSpec((None, 4, k, n), lambda l: (l, 0, 0, 0)),
        out_shape=jax.ShapeDtypeStruct((depth, 4, k, n), BF),
        compiler_params=_params(1),
    )(*ws)


def _gather_program(shards, kinds, layers, part=(0, 1), into=None):
    n_t = len(shards)
    index, count = part

    def full_shape(s, kind, layer):
        shp = list(s.shape if layer is None else s.shape[1:])
        if kind == "col":
            return (N_CHIPS,) + tuple(shp)
        shp[-2 if kind == "row" else -1] *= N_CHIPS
        return tuple(shp)

    def run(phase, ins, full, sems):
        shard = [r if l is None else r.at[l] for r, l in zip(ins, layers)]
        ici_send, ici_recv, sib_send, sib_recv, loc_sem = sems
        x, y, c, me, chips = _place()
        sibling = (x, y, 1 - c)

        def src_part(i, half):
            s, kind = shard[i], kinds[i]
            if kind == "small":
                return s
            r = s.shape[-2] // (2 * count)
            return _window(s, pl.ds(pl.multiple_of((half * count + index) * r, 16), r), slice(None))

        def dst_part(i, chip, half):
            f, s, kind = full[i], shard[i], kinds[i]
            rows, cols = s.shape[-2], s.shape[-1]
            if kind == "small":
                return _window(f, slice(None), pl.ds(pl.multiple_of(chip * cols, 128), cols))
            r = rows // (2 * count)
            at = (half * count + index) * r
            if kind == "col":
                return f.at[chip, pl.ds(pl.multiple_of(at, 16), r), :]
            return _window(f, pl.ds(pl.multiple_of(chip * rows + at, 16), r), slice(None))

        def own_slot(i):
            f, s, kind = full[i], shard[i], kinds[i]
            rows, cols = s.shape[-2], s.shape[-1]
            if kind == "row":
                return _window(f, pl.ds(pl.multiple_of(me * rows, 16), rows), slice(None))
            if kind == "col":
                return f.at[me]
            return _window(f, slice(None), pl.ds(pl.multiple_of(me * cols, 128), cols))

        local = [_near(shard[i], own_slot(i), loc_sem.at[i]) for i in range(n_t)] if index == 0 else []
        sends = []
        for i in range(n_t):
            for j, chip in enumerate(chips):
                sends.append(_far(src_part(i, c), dst_part(i, me, c), ici_send.at[3 * i + j],
                                  ici_recv.at[3 * i + j], (*chip, c)))
        if phase == "start":
            for cp in local + sends:
                cp.start()
            return
        passed = []
        for i in range(n_t):
            for j, chip in enumerate(chips):
                k = 2 * chip[0] + chip[1]
                landed = dst_part(i, k, c)
                _remote(landed, landed, ici_send.at[3 * i + j], ici_recv.at[3 * i + j], (*chip, c)).wait_recv()
                if kinds[i] != "small":
                    cp = _far(landed, landed, sib_send.at[3 * i + j], sib_recv.at[3 * i + j], sibling)
                    cp.start()
                    passed.append(cp)
        for i in range(n_t):
            if kinds[i] == "small":
                continue
            for j, chip in enumerate(chips):
                k = 2 * chip[0] + chip[1]
                other = dst_part(i, k, 1 - c)
                _remote(other, other, sib_send.at[3 * i + j], sib_recv.at[3 * i + j], sibling).wait_recv()
        for cp in sends + passed:
            cp.wait_send()
        for cp in local:
            cp.wait()

    outs = [jax.ShapeDtypeStruct(full_shape(s, k, l), s.dtype) for s, k, l in zip(shards, kinds, layers)]
    sems = [pltpu.SemaphoreType.DMA((3 * n_t,))] * 4 + [pltpu.SemaphoreType.DMA((n_t,))]
    return _Comm(list(shards) + list(into or []), outs, sems, functools.partial(run, "start"),
                 functools.partial(run, "finish"), {n_t + i: i for i in range(len(into or []))})


def _all_reduce_small(part, name):
    r, n = part.shape

    def body(in_ref, out_ref, gather_ref, send_sems, recv_sems):
        x, y, c, _, _ = _place()
        me = 4 * x + 2 * y + c
        gather_ref[me] = in_ref[...]
        sends = []
        for k in range(1, N_DEV):
            peer = (me + k) % N_DEV
            sends.append(_remote(in_ref, gather_ref.at[me], send_sems.at[k - 1], recv_sems.at[k - 1],
                                 (peer // 4, (peer // 2) % 2, peer % 2)))
        for cp in sends:
            cp.start()
        for k in range(1, N_DEV):
            origin = (me + N_DEV - k) % N_DEV
            _remote(in_ref, gather_ref.at[origin], send_sems.at[k - 1], recv_sems.at[k - 1],
                    (x, y, c)).wait_recv()
        for cp in sends:
            cp.wait_send()
        total = gather_ref[0]
        for dev in range(1, N_DEV):
            total = total + gather_ref[dev]
        out_ref[...] = total

    vm = pl.BlockSpec(memory_space=pltpu.VMEM)
    return _pallas_call(
        body, name=name, in_specs=[vm], out_specs=vm, out_shape=jax.ShapeDtypeStruct((r, n), F32),
        scratch_shapes=[pltpu.VMEM((N_DEV, r, n), F32), pltpu.SemaphoreType.DMA((N_DEV - 1,)),
                        pltpu.SemaphoreType.DMA((N_DEV - 1,))],
        compiler_params=pltpu.CompilerParams(vmem_limit_bytes=VMEM_LIMIT),
    )(part)


def _halves_view(name, dw):
    k, n = dw.shape
    s = 1 if name in COL else N_CHIPS
    return dw.reshape(s, 2, k // (2 * s), n)


def _swap_program(views):
    n_t = len(views)

    def run(phase, src, dst, sems):
        send_sems, recv_sems = sems
        x, y, c, _, _ = _place()
        copies = [_far(src[i].at[:, 1 - c], dst[i], send_sems.at[i], recv_sems.at[i], (x, y, 1 - c))
                  for i in range(n_t)]
        for cp in copies:
            if phase == "start":
                cp.start()
            else:
                cp.wait()

    outs = [jax.ShapeDtypeStruct((v.shape[0],) + v.shape[2:], F32) for v in views]
    sems = [pltpu.SemaphoreType.DMA((n_t,)), pltpu.SemaphoreType.DMA((n_t,))]
    return _Comm(views, outs, sems, functools.partial(run, "start"), functools.partial(run, "finish"))


def _add_halves(view, got, core, name):
    s, _, r, n = view.shape
    tr = _row_tile(r, n)

    def body(c_ref, a_ref, b_ref, o_ref):
        o_ref[...] = (a_ref[...] + b_ref[...]).astype(BF)

    return _pallas_call(
        body, name=name,
        grid_spec=pltpu.PrefetchScalarGridSpec(
            num_scalar_prefetch=1, grid=(s, r // tr),
            in_specs=[pl.BlockSpec((None, None, tr, n), lambda i, j, c_ref: (i, c_ref[0], j, 0)),
                      pl.BlockSpec((None, tr, n), lambda i, j, c_ref: (i, j, 0))],
            out_specs=pl.BlockSpec((None, tr, n), lambda i, j, c_ref: (i, j, 0))),
        out_shape=jax.ShapeDtypeStruct((s, r, n), BF),
        compiler_params=_params(2),
    )(core, view, got)


def _scatter_program(sums, names):
    n_t = len(sums)

    def piece_shape(nm, p):
        s, r, n = p.shape
        return (r, n // N_CHIPS) if nm in COL else (r, n)

    def run(phase, src, dst, sems):
        ici_send, ici_recv, sib_send, sib_recv, loc_sem = sems
        x, y, c, me, chips = _place()
        sibling = (x, y, 1 - c)

        def piece(i, chip):
            if names[i] in COL:
                cw = src[i].shape[2] // N_CHIPS
                return src[i].at[0, :, pl.ds(pl.multiple_of(chip * cw, 128), cw)]
            return src[i].at[chip]

        local = [_near(piece(i, me), dst[i].at[c, me], loc_sem.at[i]) for i in range(n_t)]
        sends = []
        for i in range(n_t):
            sends.append(_far(piece(i, me), dst[i].at[c, me], sib_send.at[4 * i + 3], sib_recv.at[4 * i + 3],
                              sibling))
            for j, chip in enumerate(chips):
                k = 2 * chip[0] + chip[1]
                sends.append(_far(piece(i, k), dst[i].at[c, me], ici_send.at[3 * i + j], ici_recv.at[3 * i + j],
                                  (*chip, c)))
        if phase == "start":
            for cp in local + sends:
                cp.start()
            return
        passed = []
        for i in range(n_t):
            for j, chip in enumerate(chips):
                k = 2 * chip[0] + chip[1]
                landed = dst[i].at[c, k]
                _remote(landed, landed, ici_send.at[3 * i + j], ici_recv.at[3 * i + j], (*chip, c)).wait_recv()
                cp = _far(landed, landed, sib_send.at[4 * i + j], sib_recv.at[4 * i + j], sibling)
                cp.start()
                passed.append(cp)
        for i in range(n_t):
            other = dst[i].at[1 - c, me]
            _remote(other, other, sib_send.at[4 * i + 3], sib_recv.at[4 * i + 3], sibling).wait_recv()
            for j, chip in enumerate(chips):
                k = 2 * chip[0] + chip[1]
                other = dst[i].at[1 - c, k]
                _remote(other, other, sib_send.at[4 * i + j], sib_recv.at[4 * i + j], sibling).wait_recv()
        for cp in sends + passed:
            cp.wait_send()
        for cp in local:
            cp.wait()

    outs = [jax.ShapeDtypeStruct((2, N_CHIPS) + piece_shape(nm, p), BF) for nm, p in zip(names, sums)]
    sems = [pltpu.SemaphoreType.DMA((3 * n_t,))] * 2 + [pltpu.SemaphoreType.DMA((4 * n_t,))] * 2 + [
        pltpu.SemaphoreType.DMA((n_t,))]
    return _Comm(sums, outs, sems, functools.partial(run, "start"), functools.partial(run, "finish"))


def _adamw(w, g, m, v):
    m = ADAM_B1 * m + (1.0 - ADAM_B1) * g
    v = ADAM_B2 * v + (1.0 - ADAM_B2) * (g * g)
    m_hat = m / (1.0 - ADAM_B1 ** ADAM_STEP)
    v_hat = v / (1.0 - ADAM_B2 ** ADAM_STEP)
    return -ADAM_LR * (m_hat / (jnp.sqrt(v_hat) + ADAM_EPS) + ADAM_WD * w), m, v


def _adam_shard(pieces, w, m, v, l, prev, name):
    depth, rows, cw = w.shape
    hr = rows // 2
    tr = _row_tile(hr, cw, limit=1 << 18)
    n_i = hr // tr

    def body(*refs):
        z_ref, w_ref, m_ref, v_ref = refs[:4]
        g_ref, d_ref, nm_ref, nv_ref = refs[-4:]
        g = z_ref[0].astype(F32)
        for k in range(1, N_CHIPS):
            g = g + z_ref[k].astype(F32)
        g_ref[...] = g
        d_ref[...], nm_ref[...], nv_ref[...] = _adamw(w_ref[...], g, m_ref[...], v_ref[...])

    par = pl.BlockSpec((None, tr, cw), lambda h, i: (l, h * n_i + i, 0))
    out = jax.ShapeDtypeStruct((depth, rows, cw), F32)
    extra = [] if prev is None else list(prev)
    return _pallas_call(
        body, name=name, grid=(2, n_i),
        in_specs=[pl.BlockSpec((None, N_CHIPS, tr, cw), lambda h, i: (h, 0, i, 0)), par, par, par] + [ANY] * len(extra),
        out_specs=[par] * 4, out_shape=[out] * 4,
        input_output_aliases={4 + k: k for k in range(len(extra))},
        compiler_params=_params(2),
    )(pieces, w, m, v, *extra)


def _adam_small(g, w, m, v, name):
    def body(g_ref, w_ref, m_ref, v_ref, d_ref, nm_ref, nv_ref):
        d_ref[...], nm_ref[...], nv_ref[...] = _adamw(w_ref[...], g_ref[...], m_ref[...], v_ref[...])

    vm = pl.BlockSpec(memory_space=pltpu.VMEM)
    out = jax.ShapeDtypeStruct(g.shape, F32)
    return _pallas_call(body, name=name, in_specs=[vm] * 4, out_specs=[vm] * 3, out_shape=[out] * 3)(g, w, m, v)


WEIGHTS = ("norm_mix_g", "norm_mem_g", "w_in", "b_gate", "conv_a_w", "w_a_out", "conv_b_w", "conv_b_bias", "ln_b_g",
           "ln_b_b", "w_b_out", "w_kv", "w_att_out", "w_o", "norm_ffn_g", "w_up", "conv_ffn_w", "w_down",
           "norm_final_g")
REPLICATED = ("norm_mix_g", "norm_mem_g", "b_gate", "conv_b_bias", "ln_b_g", "ln_b_b", "norm_ffn_g")
CONVS = ("conv_a_w", "conv_b_w", "conv_ffn_w")
PACK_WIDTH = 1024


def _pack(arrays):
    flat = jnp.concatenate([a.reshape(-1) for a in arrays])
    size = -(-flat.shape[0] // (8 * PACK_WIDTH)) * (8 * PACK_WIDTH)
    return jnp.pad(flat, (0, size - flat.shape[0])).reshape(-1, PACK_WIDTH)


def _unpack(packed, shapes):
    flat = packed.reshape(-1)
    out, at = [], 0
    for shp in shapes:
        n = math.prod(shp)
        out.append(flat[at:at + n].reshape(shp))
        at += n
    return out


def kernel(x, mem, norm_mix_g, norm_mem_g, w_in, b_gate, conv_a_w, w_a_out, conv_b_w, conv_b_bias, ln_b_g, ln_b_b, w_b_out, w_kv, w_att_out, w_o, norm_ffn_g, w_up, conv_ffn_w, w_down, norm_final_g, loss_target, m_norm_mix_g, m_norm_mem_g, m_w_in, m_b_gate, m_conv_a_w, m_w_a_out, m_conv_b_w, m_conv_b_bias, m_ln_b_g, m_ln_b_b, m_w_b_out, m_w_kv, m_w_att_out, m_w_o, m_norm_ffn_g, m_w_up, m_conv_ffn_w, m_w_down, m_norm_final_g, v_norm_mix_g, v_norm_mem_g, v_w_in, v_b_gate, v_conv_a_w, v_w_a_out, v_conv_b_w, v_conv_b_bias, v_ln_b_g, v_ln_b_b, v_w_b_out, v_w_kv, v_w_att_out, v_w_o, v_norm_ffn_g, v_w_up, v_conv_ffn_w, v_w_down, v_norm_final_g):
    w = dict(norm_mix_g=norm_mix_g, norm_mem_g=norm_mem_g, w_in=w_in, b_gate=b_gate, conv_a_w=conv_a_w,
             w_a_out=w_a_out, conv_b_w=conv_b_w, conv_b_bias=conv_b_bias, ln_b_g=ln_b_g, ln_b_b=ln_b_b,
             w_b_out=w_b_out, w_kv=w_kv, w_att_out=w_att_out, w_o=w_o, norm_ffn_g=norm_ffn_g, w_up=w_up,
             conv_ffn_w=conv_ffn_w, w_down=w_down, norm_final_g=norm_final_g)
    mom = dict(norm_mix_g=m_norm_mix_g, norm_mem_g=m_norm_mem_g, w_in=m_w_in, b_gate=m_b_gate, conv_a_w=m_conv_a_w,
               w_a_out=m_w_a_out, conv_b_w=m_conv_b_w, conv_b_bias=m_conv_b_bias, ln_b_g=m_ln_b_g, ln_b_b=m_ln_b_b,
               w_b_out=m_w_b_out, w_kv=m_w_kv, w_att_out=m_w_att_out, w_o=m_w_o, norm_ffn_g=m_norm_ffn_g,
               w_up=m_w_up, conv_ffn_w=m_conv_ffn_w, w_down=m_w_down, norm_final_g=m_norm_final_g)
    var = dict(norm_mix_g=v_norm_mix_g, norm_mem_g=v_norm_mem_g, w_in=v_w_in, b_gate=v_b_gate, conv_a_w=v_conv_a_w,
               w_a_out=v_w_a_out, conv_b_w=v_conv_b_w, conv_b_bias=v_conv_b_bias, ln_b_g=v_ln_b_g, ln_b_b=v_ln_b_b,
               w_b_out=v_w_b_out, w_kv=v_w_kv, w_att_out=v_w_att_out, w_o=v_w_o, norm_ffn_g=v_norm_ffn_g,
               w_up=v_w_up, conv_ffn_w=v_conv_ffn_w, w_down=v_w_down, norm_final_g=v_norm_final_g)
    depth = w_in.shape[0]
    chip = 2 * lax.axis_index("x") + lax.axis_index("y")
    core = lax.axis_index("c").astype(jnp.int32).reshape(1)

    sh = dict(w_in=_cast_shard(w_in, "cast_w_in"), wsq=_cast_squares([w[n] for n in SQUARES], "cast_squares"),
              w_kv=_cast_shard(w_kv, "cast_w_kv"), w_up=_cast_shard(w_up, "cast_w_up"),
              w_down=_cast_shard(w_down, "cast_w_down"))
    conv_sh = [jnp.concatenate([conv_a_w, conv_b_w], axis=1), conv_ffn_w]
    small = {n: w[n] for n in REPLICATED + ("norm_final_g",)}

    loss, dx, sgrads, dg_final, pieces = _step(x[0], mem[0], loss_target[0], sh, conv_sh, small, core)

    res = {n: None for n in BIG}
    for l in reversed(range(depth)):
        for n in BIG:
            res[n] = _adam_shard(pieces[l][n], w[n], mom[n], var[n], l, res[n], f"adam_{n}_{l}")

    per_layer = REPLICATED + CONVS
    parts = [sgrads[l][n] for l in range(depth) for n in per_layer] + [dg_final]
    total = _all_reduce_small(_pack(parts), "all_reduce_small")
    shapes = [sgrads[l][n].shape for l in range(depth) for n in per_layer] + [dg_final.shape]
    summed = _unpack(total, shapes)
    g_small = {}
    for k, n in enumerate(per_layer):
        full = jnp.stack([summed[l * len(per_layer) + k] for l in range(depth)])
        if n in CONVS:
            cols = w[n].shape[-1]
            full = lax.dynamic_slice_in_dim(full, chip * cols, cols, axis=2)
        g_small[n] = full.reshape(w[n].shape)
    g_small["norm_final_g"] = summed[-1].reshape(norm_final_g.shape)
    names = per_layer + ("norm_final_g",)
    d_p, m_p, v_p = _adam_small(_pack([g_small[n] for n in names]), _pack([w[n] for n in names]),
                                _pack([mom[n] for n in names]), _pack([var[n] for n in names]), "adam_small")
    shp = [w[n].shape for n in names]
    for n, dl, nm, nv in zip(names, _unpack(d_p, shp), _unpack(m_p, shp), _unpack(v_p, shp)):
        res[n] = (g_small[n], dl, nm, nv)

    loss = lax.psum(loss[0, 0], ("x", "y", "c"))
    return (loss, dx.reshape(x.shape), *[res[n][0] for n in WEIGHTS], *[res[n][1] for n in WEIGHTS],
            *[res[n][2] for n in WEIGHTS], *[res[n][3] for n in WEIGHTS])
```

```python
import functools
import math

import jax
import jax.numpy as jnp
from jax import lax
from jax.experimental import pallas as pl
from jax.experimental.pallas import tpu as pltpu

F32 = jnp.float32
BF = jnp.bfloat16
EPS = 1e-6
N_HEADS = 4
K_A, K_B, K_F = 3, 31, 3
ADAM_LR, ADAM_B1, ADAM_B2, ADAM_EPS, ADAM_WD, ADAM_STEP = 0.001, 0.9, 0.999, 1e-08, 0.01, 10
N_CHIPS = 4
N_DEV = 8
HALO = 32
MAX_STRIPES = 8
TM_ROW = 256
TM_MM = 1024
TT_DW = 4096
DW_LHS_ELEMS = 4 * 1024 * 1024
TR_EW = 128
VMEM_LIMIT = 56 * 1024 * 1024
MESH = pl.DeviceIdType.MESH
_pallas_call = pl.pallas_call


def _params(n_axes):
    return pltpu.CompilerParams(dimension_semantics=("arbitrary",) * n_axes, vmem_limit_bytes=VMEM_LIMIT)


def _resident(shape, index):
    return pl.BlockSpec(shape, lambda *_: index, pipeline_mode=pl.Buffered(1))


def _sig(x):
    return 1.0 / (1.0 + jnp.exp(-x))


def _nt(a, b):
    return lax.dot_general(a, b, (((1,), (1,)), ((), ())), preferred_element_type=F32)


def _tn(a, b):
    return lax.dot_general(a, b, (((0,), (0,)), ((), ())), preferred_element_type=F32)


def _nn(a, b):
    return jnp.dot(a, b, preferred_element_type=F32)


class _Comm:
    def __init__(self, inputs, out_shape, scratch, start, finish, aliases=None):
        self.inputs, self.out_shape, self.scratch = list(inputs), list(out_shape), list(scratch)
        self.start, self.finish = start, finish
        self.aliases = dict(aliases or {})


def _join(programs):
    programs = [p for p in programs if p is not None]
    if not programs:
        return None

    def split(seq, counts):
        parts, at = [], 0
        for n in counts:
            parts.append(seq[at:at + n])
            at += n
        return parts

    n_in = [len(p.inputs) for p in programs]
    n_out = [len(p.out_shape) for p in programs]
    n_s = [len(p.scratch) for p in programs]

    def phase(which):
        def run(ins, outs, sems):
            for p, i, o, s in zip(programs, split(ins, n_in), split(outs, n_out), split(sems, n_s)):
                getattr(p, which)(i, o, s)
        return run

    aliases, in_at, out_at = {}, 0, 0
    for p, i, o in zip(programs, n_in, n_out):
        aliases.update({in_at + a: out_at + b for a, b in p.aliases.items()})
        in_at, out_at = in_at + i, out_at + o
    return _Comm([a for p in programs for a in p.inputs], [a for p in programs for a in p.out_shape],
                 [a for p in programs for a in p.scratch], phase("start"), phase("finish"), aliases)


def _run(body, comm, *, name, grid, in_specs, out_specs, out_shape, args, scratch_shapes=()):
    n_axes = len(grid)
    if comm is None:
        outs = _pallas_call(body, name=name, grid=grid, in_specs=list(in_specs), out_specs=list(out_specs),
                            out_shape=list(out_shape), scratch_shapes=list(scratch_shapes),
                            compiler_params=_params(n_axes))(*args)
        return list(outs), []
    counts = (len(in_specs), len(comm.inputs), len(out_specs), len(comm.out_shape), len(scratch_shapes),
              len(comm.scratch))

    def hosted(*refs):
        parts, at = [], 0
        for n in counts:
            parts.append(refs[at:at + n])
            at += n
        ins, c_ins, outs, c_outs, scr, c_sems = parts
        ids = [pl.program_id(a) for a in range(n_axes)]
        first = functools.reduce(jnp.logical_and, [i == 0 for i in ids])
        last = functools.reduce(jnp.logical_and, [i == g - 1 for i, g in zip(ids, grid)])

        @pl.when(first)
        def _():
            comm.start(c_ins, c_outs, c_sems)

        body(*ins, *outs, *scr)

        @pl.when(last)
        def _():
            comm.finish(c_ins, c_outs, c_sems)

    any_spec = pl.BlockSpec(memory_space=pl.ANY)
    res = _pallas_call(
        hosted, name=name, grid=grid, in_specs=list(in_specs) + [any_spec] * counts[1],
        out_specs=list(out_specs) + [any_spec] * counts[3], out_shape=list(out_shape) + comm.out_shape,
        scratch_shapes=list(scratch_shapes) + comm.scratch, compiler_params=_params(n_axes),
        input_output_aliases={counts[0] + a: counts[2] + b for a, b in comm.aliases.items()},
    )(*args, *comm.inputs)
    return list(res[:counts[2]]), list(res[counts[2]:])


def _run_comm(comm, name):
    n_in, n_out = len(comm.inputs), len(comm.out_shape)

    def body(*refs):
        ins, outs, sems = refs[:n_in], refs[n_in:n_in + n_out], refs[n_in + n_out:]
        comm.start(ins, outs, sems)
        comm.finish(ins, outs, sems)

    any_spec = pl.BlockSpec(memory_space=pl.ANY)
    return list(_pallas_call(body, name=name, in_specs=[any_spec] * n_in, out_specs=[any_spec] * n_out,
                             out_shape=comm.out_shape, scratch_shapes=comm.scratch,
                             input_output_aliases=comm.aliases)(*comm.inputs))


def _causal_taps(xx_ref, w_ref, k_taps, tm):
    base = HALO - k_taps + 1
    acc = w_ref[0:1, :] * xx_ref[pl.ds(base, tm), :]
    for k in range(1, k_taps):
        acc = acc + w_ref[k:k + 1, :] * xx_ref[pl.ds(base + k, tm), :]
    return acc


def _anticausal_taps(yy_ref, w_ref, k_taps, tm):
    acc = w_ref[k_taps - 1:k_taps, :] * yy_ref[pl.ds(0, tm), :]
    for k in range(k_taps - 1):
        acc = acc + w_ref[k:k + 1, :] * yy_ref[pl.ds(k_taps - 1 - k, tm), :]
    return acc


def _tap_grads(dw_ref, dy, xx_ref, k_taps, tm):
    base = HALO - k_taps + 1
    for k in range(k_taps):
        dw_ref[k:k + 1, :] += jnp.sum(dy * xx_ref[pl.ds(base + k, tm), :], axis=0, keepdims=True)


def _prev_halo(tm, col):
    return lambda i: (jnp.maximum(i * (tm // HALO) - 1, 0), col)


def _next_halo(tm, n_rows, col):
    return lambda i: (jnp.minimum((i + 1) * (tm // HALO), n_rows // HALO - 1), col)


def _norm_matmul(x, g, w, gs, name, comm=None):
    t, d = x.shape
    n_s, _, ns = w.shape
    n = n_s * ns
    tm = min(TM_MM, t)

    def body(x_ref, g_ref, w_ref, h_ref, y_ref):
        @pl.when(pl.program_id(1) == 0)
        def _():
            xf = x_ref[...]
            r = lax.rsqrt(jnp.mean(xf * xf, axis=-1, keepdims=True) + EPS)
            h_ref[...] = ((xf * r) * g_ref[...]).astype(BF)

        for s in range(gs):
            y_ref[:, s * ns:(s + 1) * ns] = _nn(h_ref[...], w_ref[s]).astype(BF)

    return _run(
        body, comm, name=name, grid=(t // tm, n_s // gs),
        in_specs=[pl.BlockSpec((tm, d), lambda i, j: (i, 0)),
                  pl.BlockSpec((1, d), lambda i, j: (0, 0)),
                  pl.BlockSpec((gs, d, ns), lambda i, j: (j, 0, 0))],
        out_specs=[pl.BlockSpec((tm, d), lambda i, j: (i, 0)),
                   pl.BlockSpec((tm, gs * ns), lambda i, j: (i, j))],
        out_shape=[jax.ShapeDtypeStruct((t, d), BF), jax.ShapeDtypeStruct((t, n), BF)],
        args=(x, g, w))


def _mem_kv(mem, g, w_kv, name):
    m, d = mem.shape
    n_s, _, ns = w_kv.shape

    def body(mem_ref, g_ref, w_ref, memn_ref, kv_ref):
        xf = mem_ref[...]
        r = lax.rsqrt(jnp.mean(xf * xf, axis=-1, keepdims=True) + EPS)
        memn = ((xf * r) * g_ref[...]).astype(BF)
        memn_ref[...] = memn
        for s in range(n_s):
            kv_ref[:, s * ns:(s + 1) * ns] = _nn(memn, w_ref[s]).astype(BF)

    return _pallas_call(
        body, name=name, grid=(1,),
        in_specs=[pl.BlockSpec((m, d), lambda i: (0, 0)),
                  pl.BlockSpec((1, d), lambda i: (0, 0)),
                  pl.BlockSpec((n_s, d, ns), lambda i: (0, 0, 0))],
        out_specs=[pl.BlockSpec((m, d), lambda i: (0, 0)),
                   pl.BlockSpec((m, 2 * d), lambda i: (0, 0))],
        out_shape=[jax.ShapeDtypeStruct((m, d), BF), jax.ShapeDtypeStruct((m, 2 * d), BF)],
        compiler_params=_params(1),
    )(mem, g, w_kv)


def _load_branch_inputs(i, proj_ref, gch_ref, vh_ref, u0h_ref, ugh_ref, xa_ref, xb_ref, d):
    gc = proj_ref[:, d:2 * d].astype(F32)
    v = proj_ref[:, 2 * d:3 * d].astype(F32)
    u0 = proj_ref[:, 3 * d:4 * d].astype(F32)
    ug = proj_ref[:, 4 * d:5 * d].astype(F32)
    keep = (i > 0).astype(F32)
    xa_ref[pl.ds(0, HALO), :] = gch_ref[...].astype(F32) * vh_ref[...].astype(F32) * keep
    xa_ref[pl.ds(HALO, gc.shape[0]), :] = gc * v
    xb_ref[pl.ds(0, HALO), :] = u0h_ref[...].astype(F32) * _sig(ugh_ref[...].astype(F32)) * keep
    xb_ref[pl.ds(HALO, gc.shape[0]), :] = u0 * _sig(ug)


def _softmax_rows(s):
    e = jnp.exp(s - jnp.max(s, axis=-1, keepdims=True))
    return e / jnp.sum(e, axis=-1, keepdims=True)


def _mixer_fwd(proj, x, kv, conv_a, conv_b, cbias, ln_g, ln_b, b_gate, wsq, l, name, comm=None):
    t, d = x.shape
    m = kv.shape[0]
    tm = min(TM_ROW, t)
    hd = d // N_HEADS
    scale = 1.0 / math.sqrt(hd)

    def body(proj_ref, gch_ref, vh_ref, u0h_ref, ugh_ref, x_ref, kv_ref, ca_w, cb_w, cbias_ref, lng_ref, lnb_ref,
             bg_ref, wa_ref, wb_ref, wc_ref, wo_ref,
             x1_ref, za_ref, zb_ref, o_ref, ya_ref, yb_ref, yc_ref, mg_ref, cb_ref, xa_ref, xb_ref):
        i = pl.program_id(0)
        _load_branch_inputs(i, proj_ref, gch_ref, vh_ref, u0h_ref, ugh_ref, xa_ref, xb_ref, d)
        gb = proj_ref[:, 0:d].astype(F32)
        za = (gb * _causal_taps(xa_ref, ca_w, K_A, tm)).astype(BF)
        za_ref[...] = za
        ya = _nn(za, wa_ref[...])
        ya_ref[...] = ya.astype(BF)
        cb = _causal_taps(xb_ref, cb_w, K_B, tm) + cbias_ref[...]
        cb_ref[...] = cb
        mu = jnp.mean(cb, axis=-1, keepdims=True)
        dlt = cb - mu
        rstd = lax.rsqrt(jnp.mean(dlt * dlt, axis=-1, keepdims=True) + EPS)
        lnb = (dlt * rstd) * lng_ref[...] + lnb_ref[...]
        zb = (lnb * _sig(lnb)).astype(BF)
        zb_ref[...] = zb
        yb = _nn(zb, wb_ref[...])
        yb_ref[...] = yb.astype(BF)
        for h in range(N_HEADS):
            qh = proj_ref[:, 5 * d + h * hd:5 * d + (h + 1) * hd]
            kh = kv_ref[:, h * hd:(h + 1) * hd]
            vh = kv_ref[:, d + h * hd:d + (h + 1) * hd]
            p = _softmax_rows(_nt(qh, kh) * scale)
            o_ref[:, h * hd:(h + 1) * hd] = _nn(p.astype(BF), vh).astype(BF)
        yc = _nn(o_ref[...], wc_ref[...])
        yc_ref[...] = yc.astype(BF)
        g0 = _sig(proj_ref[:, 6 * d:7 * d].astype(F32) + bg_ref[:, 0:d])
        g1 = _sig(proj_ref[:, 7 * d:8 * d].astype(F32) + bg_ref[:, d:2 * d])
        g2 = _sig(proj_ref[:, 8 * d:9 * d].astype(F32) + bg_ref[:, 2 * d:3 * d])
        mg = (g0 * ya + g1 * yb + g2 * yc).astype(BF)
        mg_ref[...] = mg
        x1_ref[...] = x_ref[...] + _nn(mg, wo_ref[...])

    row = lambda w_: pl.BlockSpec((tm, w_), lambda i: (i, 0))
    halo = lambda col: pl.BlockSpec((HALO, d), _prev_halo(tm, col))
    sq = lambda which: _resident((None, None, d, d), (l, which, 0, 0))
    act = jax.ShapeDtypeStruct((t, d), BF)
    return _run(
        body, comm, name=name, grid=(t // tm,),
        in_specs=[row(9 * d), halo(1), halo(2), halo(3), halo(4), row(d),
                  _resident((m, 2 * d), (0, 0)),
                  _resident((None, K_A, d), (l, 0, 0)), _resident((None, K_B, d), (l, 0, 0)),
                  _resident((1, d), (0, 0)), _resident((1, d), (0, 0)), _resident((1, d), (0, 0)),
                  _resident((1, 3 * d), (0, 0)), sq(0), sq(1), sq(2), sq(3)],
        out_specs=[row(d)] * 9,
        out_shape=[jax.ShapeDtypeStruct((t, d), F32)] + [act] * 7 + [jax.ShapeDtypeStruct((t, d), F32)],
        scratch_shapes=[pltpu.VMEM((HALO + tm, d), F32), pltpu.VMEM((HALO + tm, d), F32)],
        args=(proj, proj, proj, proj, proj, x, kv, conv_a, conv_b, cbias, ln_g, ln_b, b_gate, wsq, wsq, wsq, wsq))


def _ffn_down_fwd(up, x1, conv_f, w_down, l, name, comm=None):
    t, d = x1.shape
    f2 = up.shape[1]
    f = f2 // 2
    tm = min(TM_ROW, t)

    def body(up_ref, uph_ref, x1_ref, cw_ref, wd_ref, x2_ref, zf_ref, xx_ref):
        i = pl.program_id(0)
        xx_ref[pl.ds(0, HALO), :] = uph_ref[...].astype(F32) * (i > 0).astype(F32)
        xx_ref[pl.ds(HALO, tm), :] = up_ref[...].astype(F32)
        uc = _causal_taps(xx_ref, cw_ref, K_F, tm)
        gt = uc[:, 0:f]
        zf = (gt * _sig(gt) * uc[:, f:f2]).astype(BF)
        zf_ref[...] = zf
        x2_ref[...] = x1_ref[...] + _nn(zf, wd_ref[...])

    return _run(
        body, comm, name=name, grid=(t // tm,),
        in_specs=[pl.BlockSpec((tm, f2), lambda i: (i, 0)),
                  pl.BlockSpec((HALO, f2), _prev_halo(tm, 0)),
                  pl.BlockSpec((tm, d), lambda i: (i, 0)),
                  _resident((None, K_F, f2), (l, 0, 0)),
                  _resident((None, f, d), (l, 0, 0))],
        out_specs=[pl.BlockSpec((tm, d), lambda i: (i, 0)), pl.BlockSpec((tm, f), lambda i: (i, 0))],
        out_shape=[jax.ShapeDtypeStruct((t, d), F32), jax.ShapeDtypeStruct((t, f), BF)],
        scratch_shapes=[pltpu.VMEM((HALO + tm, f2), F32)],
        args=(up, up, x1, conv_f, w_down))


def _final_loss(x, g, target, name):
    t, d = x.shape
    tm = min(2 * TM_ROW, t)

    def body(x_ref, g_ref, t_ref, dx_ref, loss_ref, dg_ref):
        @pl.when(pl.program_id(0) == 0)
        def _():
            loss_ref[...] = jnp.zeros_like(loss_ref)
            dg_ref[...] = jnp.zeros_like(dg_ref)

        xf = x_ref[...]
        r = lax.rsqrt(jnp.mean(xf * xf, axis=-1, keepdims=True) + EPS)
        xhat = xf * r
        err = xhat * g_ref[...] - t_ref[...]
        loss_ref[...] += (0.5 / d) * jnp.sum(err * err)
        dy = err * (1.0 / d)
        dg_ref[...] += jnp.sum(dy * xhat, axis=0, keepdims=True)
        dxh = dy * g_ref[...]
        dx_ref[...] = r * (dxh - xhat * jnp.mean(dxh * xhat, axis=-1, keepdims=True))

    return _pallas_call(
        body, name=name, grid=(t // tm,),
        in_specs=[pl.BlockSpec((tm, d), lambda i: (i, 0)), pl.BlockSpec((1, d), lambda i: (0, 0)),
                  pl.BlockSpec((tm, d), lambda i: (i, 0))],
        out_specs=[pl.BlockSpec((tm, d), lambda i: (i, 0)), pl.BlockSpec((8, 128), lambda i: (0, 0)),
                   pl.BlockSpec((1, d), lambda i: (0, 0))],
        out_shape=[jax.ShapeDtypeStruct((t, d), F32), jax.ShapeDtypeStruct((8, 128), F32),
                   jax.ShapeDtypeStruct((1, d), F32)],
        compiler_params=_params(1),
    )(x, g, target)


def _ffn_down_bwd(dx2, up, conv_f, w_down, l, name, comm=None):
    t, d = dx2.shape
    f2 = up.shape[1]
    f = f2 // 2
    tm = min(TM_ROW, t)

    def body(dx2_ref, up_ref, uph_ref, cw_ref, wd_ref, duc_ref, dx2b_ref, dcw_ref, xx_ref):
        i = pl.program_id(0)

        @pl.when(i == 0)
        def _():
            dcw_ref[...] = jnp.zeros_like(dcw_ref)

        xx_ref[pl.ds(0, HALO), :] = uph_ref[...].astype(F32) * (i > 0).astype(F32)
        xx_ref[pl.ds(HALO, tm), :] = up_ref[...].astype(F32)
        uc = _causal_taps(xx_ref, cw_ref, K_F, tm)
        gt = uc[:, 0:f]
        sg = _sig(gt)
        dx2b = dx2_ref[...].astype(BF)
        dx2b_ref[...] = dx2b
        dzf = _nt(dx2b, wd_ref[...])
        duc_ref[:, 0:f] = (dzf * uc[:, f:f2] * (sg * (1.0 + gt * (1.0 - sg)))).astype(BF)
        duc_ref[:, f:f2] = (dzf * (gt * sg)).astype(BF)
        _tap_grads(dcw_ref, duc_ref[...].astype(F32), xx_ref, K_F, tm)

    return _run(
        body, comm, name=name, grid=(t // tm,),
        in_specs=[pl.BlockSpec((tm, d), lambda i: (i, 0)),
                  pl.BlockSpec((tm, f2), lambda i: (i, 0)),
                  pl.BlockSpec((HALO, f2), _prev_halo(tm, 0)),
                  _resident((None, K_F, f2), (l, 0, 0)),
                  _resident((None, f, d), (l, 0, 0))],
        out_specs=[pl.BlockSpec((tm, f2), lambda i: (i, 0)), pl.BlockSpec((tm, d), lambda i: (i, 0)),
                   pl.BlockSpec((K_F, f2), lambda i: (0, 0))],
        out_shape=[jax.ShapeDtypeStruct((t, f2), BF), jax.ShapeDtypeStruct((t, d), BF),
                   jax.ShapeDtypeStruct((K_F, f2), F32)],
        scratch_shapes=[pltpu.VMEM((HALO + tm, f2), F32)],
        args=(dx2, up, up, conv_f, w_down))


def _ffn_conv_bwd(duc, conv_f, l, name, comm=None):
    t, f2 = duc.shape
    tm = min(TM_ROW, t)
    n_t = t // tm

    def body(duc_ref, nxt_ref, cw_ref, dup_ref, yy_ref):
        i = pl.program_id(0)
        yy_ref[pl.ds(0, tm), :] = duc_ref[...].astype(F32)
        yy_ref[pl.ds(tm, HALO), :] = nxt_ref[...].astype(F32) * (i < n_t - 1).astype(F32)
        dup_ref[...] = _anticausal_taps(yy_ref, cw_ref, K_F, tm).astype(BF)

    return _run(
        body, comm, name=name, grid=(n_t,),
        in_specs=[pl.BlockSpec((tm, f2), lambda i: (i, 0)),
                  pl.BlockSpec((HALO, f2), _next_halo(tm, t, 0)),
                  _resident((None, K_F, f2), (l, 0, 0))],
        out_specs=[pl.BlockSpec((tm, f2), lambda i: (i, 0))],
        out_shape=[jax.ShapeDtypeStruct((t, f2), BF)],
        scratch_shapes=[pltpu.VMEM((tm + HALO, f2), F32)],
        args=(duc, duc, conv_f))


def _nt_matmul_norm_bwd(dy, w, gs, x, g, dres, name, comm=None):
    t, n = dy.shape
    d = x.shape[1]
    tm = min(TM_MM // 2, t)
    n_s, _, ns = w.shape
    n_k, tk = n_s // gs, gs * ns

    def body(dy_ref, w_ref, x_ref, g_ref, dres_ref, dx_ref, dxb_ref, dg_ref, acc_ref):
        i, k = pl.program_id(0), pl.program_id(1)

        @pl.when((i == 0) & (k == 0))
        def _():
            dg_ref[...] = jnp.zeros_like(dg_ref)

        @pl.when(k == 0)
        def _():
            acc_ref[...] = jnp.zeros_like(acc_ref)

        part = _nt(dy_ref[:, 0:ns], w_ref[0])
        for s in range(1, gs):
            part = part + _nt(dy_ref[:, s * ns:(s + 1) * ns], w_ref[s])
        acc_ref[...] += part

        @pl.when(k == n_k - 1)
        def _():
            xf = x_ref[...]
            r = lax.rsqrt(jnp.mean(xf * xf, axis=-1, keepdims=True) + EPS)
            xhat = xf * r
            dh = acc_ref[...]
            dg_ref[...] += jnp.sum(dh * xhat, axis=0, keepdims=True)
            dxh = dh * g_ref[...]
            dx = dres_ref[...] + r * (dxh - xhat * jnp.mean(dxh * xhat, axis=-1, keepdims=True))
            dx_ref[...] = dx
            dxb_ref[...] = dx.astype(BF)

    return _run(
        body, comm, name=name, grid=(t // tm, n_k),
        in_specs=[pl.BlockSpec((tm, tk), lambda i, k: (i, k)),
                  pl.BlockSpec((gs, d, ns), lambda i, k: (k, 0, 0)),
                  pl.BlockSpec((tm, d), lambda i, k: (i, 0)),
                  pl.BlockSpec((1, d), lambda i, k: (0, 0)),
                  pl.BlockSpec((tm, d), lambda i, k: (i, 0))],
        out_specs=[pl.BlockSpec((tm, d), lambda i, k: (i, 0)), pl.BlockSpec((tm, d), lambda i, k: (i, 0)),
                   pl.BlockSpec((1, d), lambda i, k: (0, 0))],
        out_shape=[jax.ShapeDtypeStruct((t, d), F32), jax.ShapeDtypeStruct((t, d), BF),
                   jax.ShapeDtypeStruct((1, d), F32)],
        scratch_shapes=[pltpu.VMEM((tm, d), F32)],
        args=(dy, w, x, g, dres))


def _mixer_bwd(dx1b, proj, ya, yb, yc, cb, kv, conv_a, conv_b, ln_g, ln_b, b_gate, wsq, l, name, comm=None):
    t, d = cb.shape
    m = kv.shape[0]
    tm = min(TM_ROW, t)
    hd = d // N_HEADS
    scale = 1.0 / math.sqrt(hd)

    def body(dx1b_ref, proj_ref, gch_ref, vh_ref, u0h_ref, ugh_ref, ya_ref, yb_ref, yc_ref, cb_ref, kv_ref,
             ca_w, cb_w, lng_ref, lnb_ref, bg_ref, wa_ref, wb_ref, wc_ref, wo_ref,
             dpre_ref, dya_ref, dyb_ref, dyc_ref, dkv_ref, dbg_ref, dlng_ref, dlnb_ref, dcbias_ref, dcaw_ref,
             dcbw_ref, xa_ref, xb_ref):
        i = pl.program_id(0)

        @pl.when(i == 0)
        def _():
            for ref in (dkv_ref, dbg_ref, dlng_ref, dlnb_ref, dcbias_ref, dcaw_ref, dcbw_ref):
                ref[...] = jnp.zeros_like(ref)

        _load_branch_inputs(i, proj_ref, gch_ref, vh_ref, u0h_ref, ugh_ref, xa_ref, xb_ref, d)
        dmg = _nt(dx1b_ref[...], wo_ref[...])
        ys = (ya_ref, yb_ref, yc_ref)
        dys = (dya_ref, dyb_ref, dyc_ref)
        for b in range(3):
            gate = _sig(proj_ref[:, (6 + b) * d:(7 + b) * d].astype(F32) + bg_ref[:, b * d:(b + 1) * d])
            dys[b][...] = (gate * dmg).astype(BF)
            dpg = dmg * ys[b][...].astype(F32) * gate * (1.0 - gate)
            dpre_ref[:, (6 + b) * d:(7 + b) * d] = dpg.astype(BF)
            dbg_ref[:, b * d:(b + 1) * d] += jnp.sum(dpg, axis=0, keepdims=True)
        gb = proj_ref[:, 0:d].astype(F32)
        ca = _causal_taps(xa_ref, ca_w, K_A, tm)
        dza = _nt(dya_ref[...], wa_ref[...])
        dpre_ref[:, 0:d] = (dza * ca).astype(BF)
        dpre_ref[:, d:2 * d] = (dza * gb).astype(BF)
        _tap_grads(dcaw_ref, dpre_ref[:, d:2 * d].astype(F32), xa_ref, K_A, tm)
        dpre_ref[:, 2 * d:3 * d] = jnp.zeros((tm, d), BF)
        cbv = cb_ref[...]
        mu = jnp.mean(cbv, axis=-1, keepdims=True)
        dlt = cbv - mu
        rstd = lax.rsqrt(jnp.mean(dlt * dlt, axis=-1, keepdims=True) + EPS)
        xhat = dlt * rstd
        lnb = xhat * lng_ref[...] + lnb_ref[...]
        sg = _sig(lnb)
        dzb = _nt(dyb_ref[...], wb_ref[...])
        dl = dzb * (sg * (1.0 + lnb * (1.0 - sg)))
        dlng_ref[...] += jnp.sum(dl * xhat, axis=0, keepdims=True)
        dlnb_ref[...] += jnp.sum(dl, axis=0, keepdims=True)
        dxh = dl * lng_ref[...]
        dcb = rstd * (dxh - jnp.mean(dxh, axis=-1, keepdims=True)
                      - xhat * jnp.mean(dxh * xhat, axis=-1, keepdims=True))
        dcbias_ref[...] += jnp.sum(dcb, axis=0, keepdims=True)
        dpre_ref[:, 3 * d:4 * d] = dcb.astype(BF)
        _tap_grads(dcbw_ref, dpre_ref[:, 3 * d:4 * d].astype(F32), xb_ref, K_B, tm)
        dpre_ref[:, 4 * d:5 * d] = jnp.zeros((tm, d), BF)
        do = _nt(dyc_ref[...], wc_ref[...]).astype(BF)
        for h in range(N_HEADS):
            qh = proj_ref[:, 5 * d + h * hd:5 * d + (h + 1) * hd]
            kh = kv_ref[:, h * hd:(h + 1) * hd]
            vh = kv_ref[:, d + h * hd:d + (h + 1) * hd]
            doh = do[:, h * hd:(h + 1) * hd]
            p = _softmax_rows(_nt(qh, kh) * scale)
            dp = _nt(doh, vh)
            ds = (p * (dp - jnp.sum(dp * p, axis=-1, keepdims=True)) * scale).astype(BF)
            dpre_ref[:, 5 * d + h * hd:5 * d + (h + 1) * hd] = _nn(ds, kh).astype(BF)
            dkv_ref[:, h * hd:(h + 1) * hd] += _tn(ds, qh)
            dkv_ref[:, d + h * hd:d + (h + 1) * hd] += _tn(p.astype(BF), doh)

    row = lambda w_: pl.BlockSpec((tm, w_), lambda i: (i, 0))
    halo = lambda col: pl.BlockSpec((HALO, d), _prev_halo(tm, col))
    sq = lambda which: _resident((None, None, d, d), (l, which, 0, 0))
    acc = lambda r, c: pl.BlockSpec((r, c), lambda i: (0, 0))
    act = jax.ShapeDtypeStruct((t, d), BF)
    vec = lambda r, c: jax.ShapeDtypeStruct((r, c), F32)
    return _run(
        body, comm, name=name, grid=(t // tm,),
        in_specs=[row(d), row(9 * d), halo(1), halo(2), halo(3), halo(4), row(d), row(d), row(d), row(d),
                  _resident((m, 2 * d), (0, 0)),
                  _resident((None, K_A, d), (l, 0, 0)), _resident((None, K_B, d), (l, 0, 0)),
                  _resident((1, d), (0, 0)), _resident((1, d), (0, 0)), _resident((1, 3 * d), (0, 0)),
                  sq(0), sq(1), sq(2), sq(3)],
        out_specs=[row(9 * d), row(d), row(d), row(d), acc(m, 2 * d), acc(1, 3 * d), acc(1, d), acc(1, d),
                   acc(1, d), acc(K_A, d), acc(K_B, d)],
        out_shape=[jax.ShapeDtypeStruct((t, 9 * d), BF), act, act, act, vec(m, 2 * d), vec(1, 3 * d), vec(1, d),
                   vec(1, d), vec(1, d), vec(K_A, d), vec(K_B, d)],
        scratch_shapes=[pltpu.VMEM((HALO + tm, d), F32), pltpu.VMEM((HALO + tm, d), F32)],
        args=(dx1b, proj, proj, proj, proj, proj, ya, yb, yc, cb, kv, conv_a, conv_b, ln_g, ln_b, b_gate,
              wsq, wsq, wsq, wsq))


def _inproj_conv_bwd(dpre, proj, conv_a, conv_b, l, name, comm=None):
    t, d9 = dpre.shape
    d = d9 // 9
    tm = min(TM_ROW, t)
    n_t = t // tm

    def body(dpre_ref, nxa_ref, nxb_ref, proj_ref, ca_w, cb_w, dproj_ref, ya_ref, yb_ref):
        i = pl.program_id(0)
        keep = (i < n_t - 1).astype(F32)
        ya_ref[pl.ds(0, tm), :] = dpre_ref[:, d:2 * d].astype(F32)
        ya_ref[pl.ds(tm, HALO), :] = nxa_ref[...].astype(F32) * keep
        yb_ref[pl.ds(0, tm), :] = dpre_ref[:, 3 * d:4 * d].astype(F32)
        yb_ref[pl.ds(tm, HALO), :] = nxb_ref[...].astype(F32) * keep
        dproj_ref[:, 0:d] = dpre_ref[:, 0:d]
        dproj_ref[:, 5 * d:9 * d] = dpre_ref[:, 5 * d:9 * d]
        dcv = _anticausal_taps(ya_ref, ca_w, K_A, tm)
        dproj_ref[:, d:2 * d] = (dcv * proj_ref[:, 2 * d:3 * d].astype(F32)).astype(BF)
        dproj_ref[:, 2 * d:3 * d] = (dcv * proj_ref[:, d:2 * d].astype(F32)).astype(BF)
        dub = _anticausal_taps(yb_ref, cb_w, K_B, tm)
        sg = _sig(proj_ref[:, 4 * d:5 * d].astype(F32))
        dproj_ref[:, 3 * d:4 * d] = (dub * sg).astype(BF)
        dproj_ref[:, 4 * d:5 * d] = (dub * proj_ref[:, 3 * d:4 * d].astype(F32) * sg * (1.0 - sg)).astype(BF)

    return _run(
        body, comm, name=name, grid=(n_t,),
        in_specs=[pl.BlockSpec((tm, d9), lambda i: (i, 0)),
                  pl.BlockSpec((HALO, d), _next_halo(tm, t, 1)),
                  pl.BlockSpec((HALO, d), _next_halo(tm, t, 3)),
                  pl.BlockSpec((tm, d9), lambda i: (i, 0)),
                  _resident((None, K_A, d), (l, 0, 0)), _resident((None, K_B, d), (l, 0, 0))],
        out_specs=[pl.BlockSpec((tm, d9), lambda i: (i, 0))],
        out_shape=[jax.ShapeDtypeStruct((t, d9), BF)],
        scratch_shapes=[pltpu.VMEM((tm + HALO, d), F32), pltpu.VMEM((tm + HALO, d), F32)],
        args=(dpre, dpre, dpre, proj, conv_a, conv_b))


def _mem_kv_bwd(dkv, memn, mem, g, w_kv, name):
    m, d = mem.shape
    n_s, _, ns = w_kv.shape

    def body(dkv_ref, memn_ref, mem_ref, g_ref, w_ref, dw_ref, dg_ref):
        dkvb = dkv_ref[...].astype(BF)
        dw_ref[...] = _tn(memn_ref[...], dkvb)
        dmemn = _nt(dkvb[:, 0:ns], w_ref[0])
        for s in range(1, n_s):
            dmemn = dmemn + _nt(dkvb[:, s * ns:(s + 1) * ns], w_ref[s])
        xf = mem_ref[...]
        r = lax.rsqrt(jnp.mean(xf * xf, axis=-1, keepdims=True) + EPS)
        dg_ref[...] = jnp.sum(dmemn * (xf * r), axis=0, keepdims=True)

    return _pallas_call(
        body, name=name, grid=(1,),
        in_specs=[pl.BlockSpec((m, 2 * d), lambda i: (0, 0)), pl.BlockSpec((m, d), lambda i: (0, 0)),
                  pl.BlockSpec((m, d), lambda i: (0, 0)), pl.BlockSpec((1, d), lambda i: (0, 0)),
                  pl.BlockSpec((n_s, d, ns), lambda i: (0, 0, 0))],
        out_specs=[pl.BlockSpec((d, 2 * d), lambda i: (0, 0)), pl.BlockSpec((1, d), lambda i: (0, 0))],
        out_shape=[jax.ShapeDtypeStruct((d, 2 * d), F32), jax.ShapeDtypeStruct((1, d), F32)],
        compiler_params=_params(1),
    )(dkv, memn, mem, g, w_kv)


def _dw_matmul(a, b, tn, name, comm=None):
    t, k = a.shape
    n = b.shape[1]
    tt = min(TT_DW, t)
    while tt * k > DW_LHS_ELEMS and tt % 2 == 0:
        tt //= 2

    def body(a_ref, b_ref, o_ref):
        @pl.when(pl.program_id(1) == 0)
        def _():
            o_ref[...] = jnp.zeros_like(o_ref)

        o_ref[...] += _tn(a_ref[...], b_ref[...])

    return _run(
        body, comm, name=name, grid=(n // tn, t // tt),
        in_specs=[pl.BlockSpec((tt, k), lambda j, s: (s, 0)), pl.BlockSpec((tt, tn), lambda j, s: (s, j))],
        out_specs=[pl.BlockSpec((k, tn), lambda j, s: (0, j))],
        out_shape=[jax.ShapeDtypeStruct((k, n), F32)],
        args=(a, b))


class _GradReduce:
    def __init__(self, l, grads, core):
        self.l, self.core, self.names = l, core, tuple(grads)
        self.views = {n: _halves_view(n, g) for n, g in grads.items()}
        self.got, self.sums, self.landing, self.pieces = {}, {}, {}, {}

    def swap_program(self):
        return _swap_program([self.views[n] for n in self.names])

    def swapped(self, outs):
        self.got = dict(zip(self.names, outs))
        for n in self.names:
            self.sums[n], self.landing[n] = _add_halves(self.views[n], self.got[n], self.core, n in COL,
                                                        f"add_halves_{n}_{self.l}")

    def scatter_program(self, names=None):
        names = self.names if names is None else names
        return _scatter_program([self.sums[n] for n in names], [self.landing[n] for n in names], names)

    def scattered(self, outs, names=None):
        self.pieces.update(zip(self.names if names is None else names, outs))


def _step(x, mem, target, sh, conv_sh, small, core):
    depth = len(sh["w_in"])
    d = x.shape[1]
    f2 = sh["w_up"][0].shape[2] * N_CHIPS
    tn_dw_in = min(1024, d)
    tn_dw_up = f2 // 11 if f2 % (11 * 128) == 0 and f2 // 11 >= 128 else f2
    row = lambda v: v.reshape(1, -1)
    one = lambda a: a[None]

    w_in, cab, cf = _run_comm(_gather_program([sh["w_in"][0]] + conv_sh, ("col", "small", "small")), "gather_first")
    saved = []
    for l in range(depth):
        conv = dict(a=cab[l:l + 1, :K_A], b=cab[l:l + 1, K_A:], f=cf[l:l + 1])
        (h, proj), (wsq, w_kv) = _norm_matmul(
            x, row(small["norm_mix_g"][l]), w_in, 1, f"in_proj_{l}",
            _gather_program([sh["wsq"][l], sh["w_kv"][l]], ("row", "col")))
        memn, kv = _mem_kv(mem, row(small["norm_mem_g"][l]), w_kv, f"mem_kv_{l}")
        (x1, za, zb, o, ya, yb, yc, mg, cb), (w_up, w_down) = _mixer_fwd(
            proj, x, kv, conv["a"], conv["b"], row(small["conv_b_bias"][l]), row(small["ln_b_g"][l]),
            row(small["ln_b_b"][l]), row(small["b_gate"][l]), one(wsq), 0, f"mixer_fwd_{l}",
            _gather_program([sh["w_up"][l], sh["w_down"][l]], ("col", "row")))
        more = l + 1 < depth
        nxt = _gather_program([sh["w_in"][l + 1]], ("col",), (0, 2)) if more else None
        (h2, up), w_in_next = _norm_matmul(x1, row(small["norm_ffn_g"][l]), w_up, 2, f"up_proj_{l}", nxt)
        nxt = _gather_program(w_in_next, ("col",), (1, 2)) if more else None
        (x2, zf), w_in_next = _ffn_down_fwd(up, x1, conv["f"], one(w_down), 0, f"ffn_down_fwd_{l}", nxt)
        saved.append(dict(x=x, x1=x1, memn=memn, kv=kv, h=h, proj=proj, za=za, zb=zb, o=o, ya=ya, yb=yb, yc=yc,
                          mg=mg, cb=cb, h2=h2, up=up, zf=zf, w_in=w_in, wsq=one(wsq), w_kv=w_kv,
                          w_up=w_up, w_down=one(w_down), conv=conv))
        x = x2
        w_in = w_in_next[0] if w_in_next else None
    dx, loss, dg_final = _final_loss(x, row(small["norm_final_g"]), target, "final_loss")
    sgrads, pieces = [None] * depth, [None] * depth
    above = None
    first, second, rest = ("w_up",), ("w_down",) + SQUARES, ("w_in", "w_kv")
    for l in reversed(range(depth)):
        s = saved[l]
        conv = s["conv"]
        bottom = l == 0
        (duc, dx2b, dconv_f), got = _ffn_down_bwd(dx, s["up"], conv["f"], s["w_down"], 0, f"ffn_down_bwd_{l}",
                                                  above and above.swap_program())
        if above:
            above.swapped(got)
        (dup,), got = _ffn_conv_bwd(duc, conv["f"], 0, f"ffn_conv_bwd_{l}", above and above.scatter_program(first))
        if above:
            above.scattered(got, first)
        (dx1, dx1b, dg_ffn), got = _nt_matmul_norm_bwd(
            dup, s["w_up"], 2, s["x1"], row(small["norm_ffn_g"][l]), dx, f"up_proj_bwd_{l}",
            above and above.scatter_program(second))
        if above:
            above.scattered(got, second)
        grads = dict(w_up=_dw_matmul(s["h2"], dup, tn_dw_up, f"dw_up_{l}")[0][0],
                     w_down=_dw_matmul(s["zf"], dx2b, d, f"dw_down_{l}")[0][0])
        ffn = _GradReduce(l, grads, core) if bottom else None
        (dpre, dya, dyb, dyc, dkv, dbg, dlng, dlnb, dcbias, dconv_a, dconv_b), got = _mixer_bwd(
            dx1b, s["proj"], s["ya"], s["yb"], s["yc"], s["cb"], s["kv"], conv["a"], conv["b"],
            row(small["ln_b_g"][l]), row(small["ln_b_b"][l]), row(small["b_gate"][l]), s["wsq"], 0,
            f"mixer_bwd_{l}", _join([above and above.scatter_program(rest), ffn and ffn.swap_program()]))
        if above:
            above.scattered(got[:len(rest)], rest)
            pieces[above.l] = above.pieces
            got = got[len(rest):]
        if ffn:
            ffn.swapped(got)
        dw_kv, dg_mem = _mem_kv_bwd(dkv, s["memn"], mem, row(small["norm_mem_g"][l]), s["w_kv"], f"mem_kv_bwd_{l}")
        mix_grads = dict(w_a_out=_dw_matmul(s["za"], dya, d, f"dw_a_out_{l}")[0][0],
                         w_b_out=_dw_matmul(s["zb"], dyb, d, f"dw_b_out_{l}")[0][0],
                         w_att_out=_dw_matmul(s["o"], dyc, d, f"dw_att_out_{l}")[0][0],
                         w_o=_dw_matmul(s["mg"], dx1b, d, f"dw_o_{l}")[0][0], w_kv=dw_kv)
        mix = _GradReduce(l, mix_grads, core) if bottom else None
        (dproj,), got = _inproj_conv_bwd(dpre, s["proj"], conv["a"], conv["b"], 0, f"inproj_conv_bwd_{l}",
                                         _join([ffn and ffn.scatter_program(), mix and mix.swap_program()]))
        if bottom:
            ffn.scattered(got[:len(ffn.names)])
            mix.swapped(got[len(ffn.names):])
        in_grads = dict(w_in=_dw_matmul(s["h"], dproj, tn_dw_in, f"dw_in_{l}")[0][0])
        inp = _GradReduce(l, in_grads, core) if bottom else None
        (dx0, _, dg_mix), got = _nt_matmul_norm_bwd(
            dproj, s["w_in"], 2, s["x"], row(small["norm_mix_g"][l]), dx1, f"in_proj_bwd_{l}",
            _join([mix and mix.scatter_program(), inp and inp.swap_program()]))
        if bottom:
            mix.scattered(got[:len(mix.names)])
            inp.swapped(got[len(mix.names):])
            inp.scattered(_run_comm(inp.scatter_program(), f"scatter_w_in_{l}"))
            pieces[l] = {**ffn.pieces, **mix.pieces, **inp.pieces}
        else:
            above = _GradReduce(l, {**grads, **mix_grads, **in_grads}, core)
        sgrads[l] = dict(norm_mix_g=dg_mix, norm_mem_g=dg_mem, b_gate=dbg, conv_b_bias=dcbias, ln_b_g=dlng,
                         ln_b_b=dlnb, norm_ffn_g=dg_ffn, conv_a_w=dconv_a, conv_b_w=dconv_b, conv_ffn_w=dconv_f)
        dx = dx0
    return loss, dx, sgrads, dg_final, pieces


BIG = ("w_in", "w_a_out", "w_b_out", "w_att_out", "w_o", "w_kv", "w_up", "w_down")
COL = ("w_in", "w_kv", "w_up")
SQUARES = ("w_a_out", "w_b_out", "w_att_out", "w_o")
ANY = pl.BlockSpec(memory_space=pl.ANY)


def _place():
    x, y, c = lax.axis_index("x"), lax.axis_index("y"), lax.axis_index("c")
    chips = [(1 - x, y), (x, 1 - y), (1 - x, 1 - y)]
    return x, y, c, 2 * x + y, chips


def _remote(src, dst, send_sem, recv_sem, dev):
    return pltpu.make_async_remote_copy(src_ref=src, dst_ref=dst, send_sem=send_sem, recv_sem=recv_sem,
                                        device_id=dev, device_id_type=MESH)


class _Striped:
    def __init__(self, src, dst, make):
        rows = src.shape[-2]
        unit = 8 * (4 // jnp.dtype(src.dtype).itemsize)
        n = max(k for k in range(1, MAX_STRIPES + 1) if rows % (unit * k) == 0) if rows % unit == 0 else 1
        q = rows // n
        self.parts = [make(_window(src, pl.ds(i * q, q), slice(None)), _window(dst, pl.ds(i * q, q), slice(None)))
                      for i in range(n)]
        self.whole = make(src, dst)

    def start(self):
        for p in self.parts:
            p.start()

    def wait(self):
        self.whole.wait()

    def wait_send(self):
        self.whole.wait_send()

    def wait_recv(self):
        self.whole.wait_recv()


def _far(src, dst, send_sem, recv_sem, dev):
    return _Striped(src, dst, lambda s, d: _remote(s, d, send_sem, recv_sem, dev))


def _near(src, dst, sem):
    return _Striped(src, dst, lambda s, d: pltpu.make_async_copy(s, d, sem))


def _window(ref, rows, cols):
    return ref.at[(slice(None),) * (len(ref.shape) - 2) + (rows, cols)]


def _row_tile(rows, cols, unit=16, limit=1 << 20):
    best = unit
    for tr in range(unit, rows + 1, unit):
        if rows % tr == 0 and tr * cols <= limit:
            best = tr
    return best


def _cast_place(ws, l, kind, chip, name):
    _, k, n = ws[0].shape
    if kind == "col":
        shape, spec = (N_CHIPS, k, n), pl.BlockSpec((None, k, n), lambda i, c: (c[0], 0, 0))
    elif len(ws) == 1:
        shape, spec = (N_CHIPS * k, n), pl.BlockSpec((k, n), lambda i, c: (c[0], 0))
    else:
        shape, spec = (len(ws), N_CHIPS * k, n), pl.BlockSpec((len(ws), k, n), lambda i, c: (0, c[0], 0))

    def body(c_ref, *refs):
        o_ref = refs[-1]
        if len(ws) == 1 or kind == "col":
            o_ref[...] = refs[0][...].astype(BF)
        else:
            for i in range(len(ws)):
                o_ref[i] = refs[i][...].astype(BF)

    return _pallas_call(
        body, name=name,
        grid_spec=pltpu.PrefetchScalarGridSpec(
            num_scalar_prefetch=1, grid=(1,),
            in_specs=[pl.BlockSpec((None, k, n), lambda i, c: (l, 0, 0))] * len(ws), out_specs=spec),
        out_shape=jax.ShapeDtypeStruct(shape, BF),
        compiler_params=_params(1),
    )(chip, *ws)


def _gather_program(arrays, kinds, part=(0, 1)):
    n_t = len(arrays)
    index, count = part

    def shard_rows(f, kind):
        return f.shape[-2] if kind == "col" else f.shape[-2] // N_CHIPS

    def run(phase, ins, full, sems):
        ici_send, ici_recv, sib_send, sib_recv, loc_sem = sems
        x, y, c, me, chips = _place()
        sibling = (x, y, 1 - c)

        def part_of(i, chip, half):
            f, kind = full[i], kinds[i]
            if kind == "small":
                cols = ins[i].shape[-1]
                return _window(f, slice(None), pl.ds(pl.multiple_of(chip * cols, 128), cols))
            rows = shard_rows(f, kind)
            r = rows // (2 * count)
            at = (half * count + index) * r
            if kind == "col":
                return f.at[chip, pl.ds(pl.multiple_of(at, 16), r), :]
            return _window(f, pl.ds(pl.multiple_of(chip * rows + at, 16), r), slice(None))

        src_part = lambda i, half: ins[i] if kinds[i] == "small" else part_of(i, me, half)
        dst_part = part_of
        local = [_near(ins[i], part_of(i, me, c), loc_sem.at[i]) for i in range(n_t) if kinds[i] == "small"]
        sends = []
        for i in range(n_t):
            for j, chip in enumerate(chips):
                sends.append(_far(src_part(i, c), dst_part(i, me, c), ici_send.at[3 * i + j],
                                  ici_recv.at[3 * i + j], (*chip, c)))
        if phase == "start":
            for cp in local + sends:
                cp.start()
            return
        passed = []
        for i in range(n_t):
            for j, chip in enumerate(chips):
                k = 2 * chip[0] + chip[1]
                landed = dst_part(i, k, c)
                _remote(landed, landed, ici_send.at[3 * i + j], ici_recv.at[3 * i + j], (*chip, c)).wait_recv()
                if kinds[i] != "small":
                    cp = _far(landed, landed, sib_send.at[3 * i + j], sib_recv.at[3 * i + j], sibling)
                    cp.start()
                    passed.append(cp)
        for i in range(n_t):
            if kinds[i] == "small":
                continue
            for j, chip in enumerate(chips):
                k = 2 * chip[0] + chip[1]
                other = dst_part(i, k, 1 - c)
                _remote(other, other, sib_send.at[3 * i + j], sib_recv.at[3 * i + j], sibling).wait_recv()
        for cp in sends + passed:
            cp.wait_send()
        for cp in local:
            cp.wait()

    def out_shape(a, kind):
        shp = a.shape[:-1] + (a.shape[-1] * N_CHIPS,) if kind == "small" else a.shape
        return jax.ShapeDtypeStruct(shp, a.dtype)

    outs = [out_shape(a, k) for a, k in zip(arrays, kinds)]
    sems = [pltpu.SemaphoreType.DMA((3 * n_t,))] * 4 + [pltpu.SemaphoreType.DMA((n_t,))]
    return _Comm(arrays, outs, sems, functools.partial(run, "start"), functools.partial(run, "finish"),
                 {i: i for i in range(n_t) if kinds[i] != "small"})


def _all_reduce_small(part, name):
    r, n = part.shape

    def body(in_ref, out_ref, gather_ref, send_sems, recv_sems):
        x, y, c, _, _ = _place()
        me = 4 * x + 2 * y + c
        gather_ref[me] = in_ref[...]
        sends = []
        for k in range(1, N_DEV):
            peer = (me + k) % N_DEV
            sends.append(_remote(in_ref, gather_ref.at[me], send_sems.at[k - 1], recv_sems.at[k - 1],
                                 (peer // 4, (peer // 2) % 2, peer % 2)))
        for cp in sends:
            cp.start()
        for k in range(1, N_DEV):
            origin = (me + N_DEV - k) % N_DEV
            _remote(in_ref, gather_ref.at[origin], send_sems.at[k - 1], recv_sems.at[k - 1],
                    (x, y, c)).wait_recv()
        for cp in sends:
            cp.wait_send()
        total = gather_ref[0]
        for dev in range(1, N_DEV):
            total = total + gather_ref[dev]
        out_ref[...] = total

    vm = pl.BlockSpec(memory_space=pltpu.VMEM)
    return _pallas_call(
        body, name=name, in_specs=[vm], out_specs=vm, out_shape=jax.ShapeDtypeStruct((r, n), F32),
        scratch_shapes=[pltpu.VMEM((N_DEV, r, n), F32), pltpu.SemaphoreType.DMA((N_DEV - 1,)),
                        pltpu.SemaphoreType.DMA((N_DEV - 1,))],
        compiler_params=pltpu.CompilerParams(vmem_limit_bytes=VMEM_LIMIT),
    )(part)


def _halves_view(name, dw):
    k, n = dw.shape
    s = 1 if name in COL else N_CHIPS
    return dw.reshape(s, 2, k // (2 * s), n)


def _swap_program(views):
    n_t = len(views)

    def run(phase, src, dst, sems):
        send_sems, recv_sems = sems
        x, y, c, _, _ = _place()
        copies = [_far(src[i].at[:, 1 - c], dst[i], send_sems.at[i], recv_sems.at[i], (x, y, 1 - c))
                  for i in range(n_t)]
        for cp in copies:
            if phase == "start":
                cp.start()
            else:
                cp.wait()

    outs = [jax.ShapeDtypeStruct((v.shape[0],) + v.shape[2:], F32) for v in views]
    sems = [pltpu.SemaphoreType.DMA((n_t,)), pltpu.SemaphoreType.DMA((n_t,))]
    return _Comm(views, outs, sems, functools.partial(run, "start"), functools.partial(run, "finish"))


def _add_halves(view, got, place, col, name):
    s, _, r, n = view.shape
    if col:
        cw = n // N_CHIPS
        tr = _row_tile(r, cw)
        grid = (r // tr, N_CHIPS)
        in_specs = [pl.BlockSpec((None, None, tr, cw), lambda j, q, p: (0, p[0], j, q)),
                    pl.BlockSpec((None, tr, cw), lambda j, q, p: (0, j, q))]
        out_specs = [pl.BlockSpec((None, tr, cw), lambda j, q, p: (0, j, q)),
                     pl.BlockSpec((None, None, tr, cw), lambda j, q, p: (p[0], p[1], j, 0))]
    else:
        cw = n
        tr = _row_tile(r, n)
        grid = (s, r // tr)
        in_specs = [pl.BlockSpec((None, None, tr, n), lambda i, j, p: (i, p[0], j, 0)),
                    pl.BlockSpec((None, tr, n), lambda i, j, p: (i, j, 0))]
        out_specs = [pl.BlockSpec((None, tr, n), lambda i, j, p: (i, j, 0)),
                     pl.BlockSpec((None, None, tr, n), lambda i, j, p: (p[0], i, j, 0))]

    def body(p_ref, a_ref, b_ref, o_ref, z_ref):
        total = (a_ref[...] + b_ref[...]).astype(BF)
        o_ref[...] = total
        if col:
            @pl.when(pl.program_id(1) == p_ref[1])
            def _():
                z_ref[...] = total
        else:
            z_ref[...] = total

    return _pallas_call(
        body, name=name,
        grid_spec=pltpu.PrefetchScalarGridSpec(num_scalar_prefetch=1, grid=grid, in_specs=in_specs,
                                               out_specs=out_specs),
        out_shape=[jax.ShapeDtypeStruct((s, r, n), BF), jax.ShapeDtypeStruct((2, N_CHIPS, r, cw), BF)],
        compiler_params=_params(2),
    )(place, view, got)


def _scatter_program(sums, landing, names):
    n_t = len(sums)

    def run(phase, src, dst, sems):
        ici_send, ici_recv, sib_send, sib_recv = sems
        x, y, c, me, chips = _place()
        sibling = (x, y, 1 - c)

        def piece(i, chip):
            if names[i] in COL:
                cw = src[i].shape[2] // N_CHIPS
                return src[i].at[0, :, pl.ds(pl.multiple_of(chip * cw, 128), cw)]
            return src[i].at[chip]

        local = []
        sends = []
        for i in range(n_t):
            sends.append(_far(piece(i, me), dst[i].at[c, me], sib_send.at[4 * i + 3], sib_recv.at[4 * i + 3],
                              sibling))
            for j, chip in enumerate(chips):
                k = 2 * chip[0] + chip[1]
                sends.append(_far(piece(i, k), dst[i].at[c, me], ici_send.at[3 * i + j], ici_recv.at[3 * i + j],
                                  (*chip, c)))
        if phase == "start":
            for cp in local + sends:
                cp.start()
            return
        passed = []
        for i in range(n_t):
            for j, chip in enumerate(chips):
                k = 2 * chip[0] + chip[1]
                landed = dst[i].at[c, k]
                _remote(landed, landed, ici_send.at[3 * i + j], ici_recv.at[3 * i + j], (*chip, c)).wait_recv()
                cp = _far(landed, landed, sib_send.at[4 * i + j], sib_recv.at[4 * i + j], sibling)
                cp.start()
                passed.append(cp)
        for i in range(n_t):
            other = dst[i].at[1 - c, me]
            _remote(other, other, sib_send.at[4 * i + 3], sib_recv.at[4 * i + 3], sibling).wait_recv()
            for j, chip in enumerate(chips):
                k = 2 * chip[0] + chip[1]
                other = dst[i].at[1 - c, k]
                _remote(other, other, sib_send.at[4 * i + j], sib_recv.at[4 * i + j], sibling).wait_recv()
        for cp in sends + passed:
            cp.wait_send()
        for cp in local:
            cp.wait()

    outs = [jax.ShapeDtypeStruct(z.shape, z.dtype) for z in landing]
    sems = [pltpu.SemaphoreType.DMA((3 * n_t,))] * 2 + [pltpu.SemaphoreType.DMA((4 * n_t,))] * 2
    return _Comm(list(sums) + list(landing), outs, sems, functools.partial(run, "start"),
                 functools.partial(run, "finish"), {n_t + i: i for i in range(n_t)})


def _adamw(w, g, m, v):
    m = ADAM_B1 * m + (1.0 - ADAM_B1) * g
    v = ADAM_B2 * v + (1.0 - ADAM_B2) * (g * g)
    m_hat = m / (1.0 - ADAM_B1 ** ADAM_STEP)
    v_hat = v / (1.0 - ADAM_B2 ** ADAM_STEP)
    return -ADAM_LR * (m_hat / (jnp.sqrt(v_hat) + ADAM_EPS) + ADAM_WD * w), m, v


def _adam_shard(pieces, w, m, v, l, prev, name):
    depth, rows, cw = w.shape
    hr = rows // 2
    tr = _row_tile(hr, cw, limit=1 << 18)
    n_i = hr // tr

    def body(*refs):
        z_ref, w_ref, m_ref, v_ref = refs[:4]
        g_ref, d_ref, nm_ref, nv_ref = refs[-4:]
        g = z_ref[0].astype(F32)
        for k in range(1, N_CHIPS):
            g = g + z_ref[k].astype(F32)
        g_ref[...] = g
        d_ref[...], nm_ref[...], nv_ref[...] = _adamw(w_ref[...], g, m_ref[...], v_ref[...])

    par = pl.BlockSpec((None, tr, cw), lambda h, i: (l, h * n_i + i, 0))
    out = jax.ShapeDtypeStruct((depth, rows, cw), F32)
    extra = [] if prev is None else list(prev)
    return _pallas_call(
        body, name=name, grid=(2, n_i),
        in_specs=[pl.BlockSpec((None, N_CHIPS, tr, cw), lambda h, i: (h, 0, i, 0)), par, par, par] + [ANY] * len(extra),
        out_specs=[par] * 4, out_shape=[out] * 4,
        input_output_aliases={4 + k: k for k in range(len(extra))},
        compiler_params=_params(2),
    )(pieces, w, m, v, *extra)


def _adam_small(g, w, m, v, name):
    def body(g_ref, w_ref, m_ref, v_ref, d_ref, nm_ref, nv_ref):
        d_ref[...], nm_ref[...], nv_ref[...] = _adamw(w_ref[...], g_ref[...], m_ref[...], v_ref[...])

    vm = pl.BlockSpec(memory_space=pltpu.VMEM)
    out = jax.ShapeDtypeStruct(g.shape, F32)
    return _pallas_call(body, name=name, in_specs=[vm] * 4, out_specs=[vm] * 3, out_shape=[out] * 3)(g, w, m, v)


WEIGHTS = ("norm_mix_g", "norm_mem_g", "w_in", "b_gate", "conv_a_w", "w_a_out", "conv_b_w", "conv_b_bias", "ln_b_g",
           "ln_b_b", "w_b_out", "w_kv", "w_att_out", "w_o", "norm_ffn_g", "w_up", "conv_ffn_w", "w_down",
           "norm_final_g")
REPLICATED = ("norm_mix_g", "norm_mem_g", "b_gate", "conv_b_bias", "ln_b_g", "ln_b_b", "norm_ffn_g")
CONVS = ("conv_a_w", "conv_b_w", "conv_ffn_w")
PACK_WIDTH = 1024


def _pack(arrays):
    flat = jnp.concatenate([a.reshape(-1) for a in arrays])
    size = -(-flat.shape[0] // (8 * PACK_WIDTH)) * (8 * PACK_WIDTH)
    return jnp.pad(flat, (0, size - flat.shape[0])).reshape(-1, PACK_WIDTH)


def _unpack(packed, shapes):
    flat = packed.reshape(-1)
    out, at = [], 0
    for shp in shapes:
        n = math.prod(shp)
        out.append(flat[at:at + n].reshape(shp))
        at += n
    return out


def kernel(x, mem, norm_mix_g, norm_mem_g, w_in, b_gate, conv_a_w, w_a_out, conv_b_w, conv_b_bias, ln_b_g, ln_b_b, w_b_out, w_kv, w_att_out, w_o, norm_ffn_g, w_up, conv_ffn_w, w_down, norm_final_g, loss_target, m_norm_mix_g, m_norm_mem_g, m_w_in, m_b_gate, m_conv_a_w, m_w_a_out, m_conv_b_w, m_conv_b_bias, m_ln_b_g, m_ln_b_b, m_w_b_out, m_w_kv, m_w_att_out, m_w_o, m_norm_ffn_g, m_w_up, m_conv_ffn_w, m_w_down, m_norm_final_g, v_norm_mix_g, v_norm_mem_g, v_w_in, v_b_gate, v_conv_a_w, v_w_a_out, v_conv_b_w, v_conv_b_bias, v_ln_b_g, v_ln_b_b, v_w_b_out, v_w_kv, v_w_att_out, v_w_o, v_norm_ffn_g, v_w_up, v_conv_ffn_w, v_w_down, v_norm_final_g):
    w = dict(norm_mix_g=norm_mix_g, norm_mem_g=norm_mem_g, w_in=w_in, b_gate=b_gate, conv_a_w=conv_a_w,
             w_a_out=w_a_out, conv_b_w=conv_b_w, conv_b_bias=conv_b_bias, ln_b_g=ln_b_g, ln_b_b=ln_b_b,
             w_b_out=w_b_out, w_kv=w_kv, w_att_out=w_att_out, w_o=w_o, norm_ffn_g=norm_ffn_g, w_up=w_up,
             conv_ffn_w=conv_ffn_w, w_down=w_down, norm_final_g=norm_final_g)
    mom = dict(norm_mix_g=m_norm_mix_g, norm_mem_g=m_norm_mem_g, w_in=m_w_in, b_gate=m_b_gate, conv_a_w=m_conv_a_w,
               w_a_out=m_w_a_out, conv_b_w=m_conv_b_w, conv_b_bias=m_conv_b_bias, ln_b_g=m_ln_b_g, ln_b_b=m_ln_b_b,
               w_b_out=m_w_b_out, w_kv=m_w_kv, w_att_out=m_w_att_out, w_o=m_w_o, norm_ffn_g=m_norm_ffn_g,
               w_up=m_w_up, conv_ffn_w=m_conv_ffn_w, w_down=m_w_down, norm_final_g=m_norm_final_g)
    var = dict(norm_mix_g=v_norm_mix_g, norm_mem_g=v_norm_mem_g, w_in=v_w_in, b_gate=v_b_gate, conv_a_w=v_conv_a_w,
               w_a_out=v_w_a_out, conv_b_w=v_conv_b_w, conv_b_bias=v_conv_b_bias, ln_b_g=v_ln_b_g, ln_b_b=v_ln_b_b,
               w_b_out=v_w_b_out, w_kv=v_w_kv, w_att_out=v_w_att_out, w_o=v_w_o, norm_ffn_g=v_norm_ffn_g,
               w_up=v_w_up, conv_ffn_w=v_conv_ffn_w, w_down=v_w_down, norm_final_g=v_norm_final_g)
    depth = w_in.shape[0]
    chip = 2 * lax.axis_index("x") + lax.axis_index("y")
    core = jnp.stack([lax.axis_index("c"), chip]).astype(jnp.int32)

    chip1 = chip.astype(jnp.int32).reshape(1)
    layers = range(depth)
    sh = dict(w_in=[_cast_place([w_in], l, "col", chip1, f"cast_w_in_{l}") for l in layers],
              wsq=[_cast_place([w[n] for n in SQUARES], l, "row", chip1, f"cast_squares_{l}") for l in layers],
              w_kv=[_cast_place([w_kv], l, "col", chip1, f"cast_w_kv_{l}") for l in layers],
              w_up=[_cast_place([w_up], l, "col", chip1, f"cast_w_up_{l}") for l in layers],
              w_down=[_cast_place([w_down], l, "row", chip1, f"cast_w_down_{l}") for l in layers])
    conv_sh = [jnp.concatenate([conv_a_w, conv_b_w], axis=1), conv_ffn_w]
    small = {n: w[n] for n in REPLICATED + ("norm_final_g",)}

    loss, dx, sgrads, dg_final, pieces = _step(x[0], mem[0], loss_target[0], sh, conv_sh, small, core)

    res = {n: None for n in BIG}
    for l in reversed(range(depth)):
        for n in BIG:
            res[n] = _adam_shard(pieces[l][n], w[n], mom[n], var[n], l, res[n], f"adam_{n}_{l}")

    per_layer = REPLICATED + CONVS
    parts = [sgrads[l][n] for l in range(depth) for n in per_layer] + [dg_final]
    total = _all_reduce_small(_pack(parts), "all_reduce_small")
    shapes = [sgrads[l][n].shape for l in range(depth) for n in per_layer] + [dg_final.shape]
    summed = _unpack(total, shapes)
    g_small = {}
    for k, n in enumerate(per_layer):
        full = jnp.stack([summed[l * len(per_layer) + k] for l in range(depth)])
        if n in CONVS:
            cols = w[n].shape[-1]
            full = lax.dynamic_slice_in_dim(full, chip * cols, cols, axis=2)
        g_small[n] = full.reshape(w[n].shape)
    g_small["norm_final_g"] = summed[-1].reshape(norm_final_g.shape)
    names = per_layer + ("norm_final_g",)
    d_p, m_p, v_p = _adam_small(_pack([g_small[n] for n in names]), _pack([w[n] for n in names]),
                                _pack([mom[n] for n in names]), _pack([var[n] for n in names]), "adam_small")
    shp = [w[n].shape for n in names]
    for n, dl, nm, nv in zip(names, _unpack(d_p, shp), _unpack(m_p, shp), _unpack(v_p, shp)):
        res[n] = (g_small[n], dl, nm, nv)

    loss = lax.psum(loss[0, 0], ("x", "y", "c"))
    return (loss, dx.reshape(x.shape), *[res[n][0] for n in WEIGHTS], *[res[n][1] for n in WEIGHTS],
            *[res[n][2] for n in WEIGHTS], *[res[n][3] for n in WEIGHTS])
```

```python
import functools
import math

import jax
import jax.numpy as jnp
from jax import lax
from jax.experimental import pallas as pl
from jax.experimental.pallas import tpu as pltpu

F32 = jnp.float32
BF = jnp.bfloat16
EPS = 1e-6
N_HEADS = 4
K_A, K_B, K_F = 3, 31, 3
ADAM_LR, ADAM_B1, ADAM_B2, ADAM_EPS, ADAM_WD, ADAM_STEP = 0.001, 0.9, 0.999, 1e-08, 0.01, 10
N_CHIPS = 4
N_DEV = 8
HALO = 32
MAX_STRIPES = 8
TM_ROW = 256
TM_MM = 1024
TT_DW = 4096
DW_LHS_ELEMS = 4 * 1024 * 1024
TR_EW = 128
VMEM_LIMIT = 56 * 1024 * 1024
MESH = pl.DeviceIdType.MESH
_pallas_call = pl.pallas_call


def _params(n_axes):
    return pltpu.CompilerParams(dimension_semantics=("arbitrary",) * n_axes, vmem_limit_bytes=VMEM_LIMIT)


def _resident(shape, index):
    return pl.BlockSpec(shape, lambda *_: index, pipeline_mode=pl.Buffered(1))


def _sig(x):
    return 1.0 / (1.0 + jnp.exp(-x))


def _nt(a, b):
    return lax.dot_general(a, b, (((1,), (1,)), ((), ())), preferred_element_type=F32)


def _tn(a, b):
    return lax.dot_general(a, b, (((0,), (0,)), ((), ())), preferred_element_type=F32)


def _nn(a, b):
    return jnp.dot(a, b, preferred_element_type=F32)


class _Comm:
    def __init__(self, inputs, out_shape, scratch, start, finish, aliases=None):
        self.inputs, self.out_shape, self.scratch = list(inputs), list(out_shape), list(scratch)
        self.start, self.finish = start, finish
        self.aliases = dict(aliases or {})


def _join(programs):
    programs = [p for p in programs if p is not None]
    if not programs:
        return None

    def split(seq, counts):
        parts, at = [], 0
        for n in counts:
            parts.append(seq[at:at + n])
            at += n
        return parts

    n_in = [len(p.inputs) for p in programs]
    n_out = [len(p.out_shape) for p in programs]
    n_s = [len(p.scratch) for p in programs]

    def phase(which):
        def run(ins, outs, sems):
            for p, i, o, s in zip(programs, split(ins, n_in), split(outs, n_out), split(sems, n_s)):
                getattr(p, which)(i, o, s)
        return run

    aliases, in_at, out_at = {}, 0, 0
    for p, i, o in zip(programs, n_in, n_out):
        aliases.update({in_at + a: out_at + b for a, b in p.aliases.items()})
        in_at, out_at = in_at + i, out_at + o
    return _Comm([a for p in programs for a in p.inputs], [a for p in programs for a in p.out_shape],
                 [a for p in programs for a in p.scratch], phase("start"), phase("finish"), aliases)


def _run(body, comm, *, name, grid, in_specs, out_specs, out_shape, args, scratch_shapes=()):
    n_axes = len(grid)
    if comm is None:
        outs = _pallas_call(body, name=name, grid=grid, in_specs=list(in_specs), out_specs=list(out_specs),
                            out_shape=list(out_shape), scratch_shapes=list(scratch_shapes),
                            compiler_params=_params(n_axes))(*args)
        return list(outs), []
    counts = (len(in_specs), len(comm.inputs), len(out_specs), len(comm.out_shape), len(scratch_shapes),
              len(comm.scratch))

    def hosted(*refs):
        parts, at = [], 0
        for n in counts:
            parts.append(refs[at:at + n])
            at += n
        ins, c_ins, outs, c_outs, scr, c_sems = parts
        ids = [pl.program_id(a) for a in range(n_axes)]
        first = functools.reduce(jnp.logical_and, [i == 0 for i in ids])
        last = functools.reduce(jnp.logical_and, [i == g - 1 for i, g in zip(ids, grid)])

        @pl.when(first)
        def _():
            comm.start(c_ins, c_outs, c_sems)

        body(*ins, *outs, *scr)

        @pl.when(last)
        def _():
            comm.finish(c_ins, c_outs, c_sems)

    any_spec = pl.BlockSpec(memory_space=pl.ANY)
    res = _pallas_call(
        hosted, name=name, grid=grid, in_specs=list(in_specs) + [any_spec] * counts[1],
        out_specs=list(out_specs) + [any_spec] * counts[3], out_shape=list(out_shape) + comm.out_shape,
        scratch_shapes=list(scratch_shapes) + comm.scratch, compiler_params=_params(n_axes),
        input_output_aliases={counts[0] + a: counts[2] + b for a, b in comm.aliases.items()},
    )(*args, *comm.inputs)
    return list(res[:counts[2]]), list(res[counts[2]:])


def _run_comm(comm, name):
    n_in, n_out = len(comm.inputs), len(comm.out_shape)

    def body(*refs):
        ins, outs, sems = refs[:n_in], refs[n_in:n_in + n_out], refs[n_in + n_out:]
        comm.start(ins, outs, sems)
        comm.finish(ins, outs, sems)

    any_spec = pl.BlockSpec(memory_space=pl.ANY)
    return list(_pallas_call(body, name=name, in_specs=[any_spec] * n_in, out_specs=[any_spec] * n_out,
                             out_shape=comm.out_shape, scratch_shapes=comm.scratch,
                             input_output_aliases=comm.aliases)(*comm.inputs))


SUBLANES = 8
ALL_RESIDUES = tuple(range(1, SUBLANES))
SHORT_RESIDUES = tuple(sorted({(HALO - K_A + 1 + k) % SUBLANES for k in range(K_A)} - {0}))


class _Rows:
    def __init__(self, ref, shifted_ref=None, residues=()):
        self.ref, self.shifted_ref, self.residues = ref, shifted_ref, tuple(residues)

    def shift(self):
        n = self.shifted_ref.shape[1]
        for j, b in enumerate(self.residues):
            self.shifted_ref[j] = self.ref[pl.ds(b, n), :]

    def at(self, offset, tm):
        b = offset % SUBLANES
        if b in self.residues:
            return self.shifted_ref[self.residues.index(b), pl.ds(offset - b, tm), :]
        return self.ref[pl.ds(offset, tm), :]


def _shifted_scratch(residues, tm, c):
    return pltpu.VMEM((len(residues), tm + HALO - SUBLANES, c), F32)


def _causal_taps(xx, w_ref, k_taps, tm):
    base = HALO - k_taps + 1
    acc = w_ref[0:1, :] * xx.at(base, tm)
    for k in range(1, k_taps):
        acc = acc + w_ref[k:k + 1, :] * xx.at(base + k, tm)
    return acc


def _anticausal_taps(yy, w_ref, k_taps, tm):
    acc = w_ref[k_taps - 1:k_taps, :] * yy.at(0, tm)
    for k in range(k_taps - 1):
        acc = acc + w_ref[k:k + 1, :] * yy.at(k_taps - 1 - k, tm)
    return acc


def _tap_grads(dw_ref, dy, xx, k_taps, tm):
    base = HALO - k_taps + 1
    for k in range(k_taps):
        dw_ref[k:k + 1, :] += jnp.sum(dy * xx.at(base + k, tm), axis=0, keepdims=True)


def _prev_halo(tm, col):
    return lambda i: (jnp.maximum(i * (tm // HALO) - 1, 0), col)


def _next_halo(tm, n_rows, col):
    return lambda i: (jnp.minimum((i + 1) * (tm // HALO), n_rows // HALO - 1), col)


def _norm_matmul(x, g, w, gs, name, comm=None):
    t, d = x.shape
    n_s, _, ns = w.shape
    n = n_s * ns
    tm = min(TM_MM, t)

    def body(x_ref, g_ref, w_ref, h_ref, y_ref):
        @pl.when(pl.program_id(1) == 0)
        def _():
            xf = x_ref[...]
            r = lax.rsqrt(jnp.mean(xf * xf, axis=-1, keepdims=True) + EPS)
            h_ref[...] = ((xf * r) * g_ref[...]).astype(BF)

        for s in range(gs):
            y_ref[:, s * ns:(s + 1) * ns] = _nn(h_ref[...], w_ref[s]).astype(BF)

    return _run(
        body, comm, name=name, grid=(t // tm, n_s // gs),
        in_specs=[pl.BlockSpec((tm, d), lambda i, j: (i, 0)),
                  pl.BlockSpec((1, d), lambda i, j: (0, 0)),
                  pl.BlockSpec((gs, d, ns), lambda i, j: (j, 0, 0))],
        out_specs=[pl.BlockSpec((tm, d), lambda i, j: (i, 0)),
                   pl.BlockSpec((tm, gs * ns), lambda i, j: (i, j))],
        out_shape=[jax.ShapeDtypeStruct((t, d), BF), jax.ShapeDtypeStruct((t, n), BF)],
        args=(x, g, w))


def _mem_kv(mem, g, w_kv, name):
    m, d = mem.shape
    n_s, _, ns = w_kv.shape

    def body(mem_ref, g_ref, w_ref, memn_ref, kv_ref):
        xf = mem_ref[...]
        r = lax.rsqrt(jnp.mean(xf * xf, axis=-1, keepdims=True) + EPS)
        memn = ((xf * r) * g_ref[...]).astype(BF)
        memn_ref[...] = memn
        for s in range(n_s):
            kv_ref[:, s * ns:(s + 1) * ns] = _nn(memn, w_ref[s]).astype(BF)

    return _pallas_call(
        body, name=name, grid=(1,),
        in_specs=[pl.BlockSpec((m, d), lambda i: (0, 0)),
                  pl.BlockSpec((1, d), lambda i: (0, 0)),
                  pl.BlockSpec((n_s, d, ns), lambda i: (0, 0, 0))],
        out_specs=[pl.BlockSpec((m, d), lambda i: (0, 0)),
                   pl.BlockSpec((m, 2 * d), lambda i: (0, 0))],
        out_shape=[jax.ShapeDtypeStruct((m, d), BF), jax.ShapeDtypeStruct((m, 2 * d), BF)],
        compiler_params=_params(1),
    )(mem, g, w_kv)


def _load_branch_inputs(i, proj_ref, gch_ref, vh_ref, u0h_ref, ugh_ref, xa_ref, xb_ref, d):
    gc = proj_ref[:, d:2 * d].astype(F32)
    v = proj_ref[:, 2 * d:3 * d].astype(F32)
    u0 = proj_ref[:, 3 * d:4 * d].astype(F32)
    ug = proj_ref[:, 4 * d:5 * d].astype(F32)
    keep = (i > 0).astype(F32)
    xa_ref[pl.ds(0, HALO), :] = gch_ref[...].astype(F32) * vh_ref[...].astype(F32) * keep
    xa_ref[pl.ds(HALO, gc.shape[0]), :] = gc * v
    xb_ref[pl.ds(0, HALO), :] = u0h_ref[...].astype(F32) * _sig(ugh_ref[...].astype(F32)) * keep
    xb_ref[pl.ds(HALO, gc.shape[0]), :] = u0 * _sig(ug)


def _softmax_rows(s):
    e = jnp.exp(s - jnp.max(s, axis=-1, keepdims=True))
    return e / jnp.sum(e, axis=-1, keepdims=True)


def _mixer_fwd(proj, x, kv, conv_a, conv_b, cbias, ln_g, ln_b, b_gate, wsq, l, name, comm=None):
    t, d = x.shape
    m = kv.shape[0]
    tm = min(TM_ROW, t)
    hd = d // N_HEADS
    scale = 1.0 / math.sqrt(hd)

    def body(proj_ref, gch_ref, vh_ref, u0h_ref, ugh_ref, x_ref, kv_ref, ca_w, cb_w, cbias_ref, lng_ref, lnb_ref,
             bg_ref, wa_ref, wb_ref, wc_ref, wo_ref,
             x1_ref, za_ref, zb_ref, o_ref, ya_ref, yb_ref, yc_ref, mg_ref, cb_ref, xa_ref, xb_ref, sb_ref):
        i = pl.program_id(0)
        _load_branch_inputs(i, proj_ref, gch_ref, vh_ref, u0h_ref, ugh_ref, xa_ref, xb_ref, d)
        xb = _Rows(xb_ref, sb_ref, ALL_RESIDUES)
        xb.shift()
        gb = proj_ref[:, 0:d].astype(F32)
        za = (gb * _causal_taps(_Rows(xa_ref), ca_w, K_A, tm)).astype(BF)
        za_ref[...] = za
        ya = _nn(za, wa_ref[...])
        ya_ref[...] = ya.astype(BF)
        cb = _causal_taps(xb, cb_w, K_B, tm) + cbias_ref[...]
        cb_ref[...] = cb
        mu = jnp.mean(cb, axis=-1, keepdims=True)
        dlt = cb - mu
        rstd = lax.rsqrt(jnp.mean(dlt * dlt, axis=-1, keepdims=True) + EPS)
        lnb = (dlt * rstd) * lng_ref[...] + lnb_ref[...]
        zb = (lnb * _sig(lnb)).astype(BF)
        zb_ref[...] = zb
        yb = _nn(zb, wb_ref[...])
        yb_ref[...] = yb.astype(BF)
        for h in range(N_HEADS):
            qh = proj_ref[:, 5 * d + h * hd:5 * d + (h + 1) * hd]
            kh = kv_ref[:, h * hd:(h + 1) * hd]
            vh = kv_ref[:, d + h * hd:d + (h + 1) * hd]
            p = _softmax_rows(_nt(qh, kh) * scale)
            o_ref[:, h * hd:(h + 1) * hd] = _nn(p.astype(BF), vh).astype(BF)
        yc = _nn(o_ref[...], wc_ref[...])
        yc_ref[...] = yc.astype(BF)
        g0 = _sig(proj_ref[:, 6 * d:7 * d].astype(F32) + bg_ref[:, 0:d])
        g1 = _sig(proj_ref[:, 7 * d:8 * d].astype(F32) + bg_ref[:, d:2 * d])
        g2 = _sig(proj_ref[:, 8 * d:9 * d].astype(F32) + bg_ref[:, 2 * d:3 * d])
        mg = (g0 * ya + g1 * yb + g2 * yc).astype(BF)
        mg_ref[...] = mg
        x1_ref[...] = x_ref[...] + _nn(mg, wo_ref[...])

    row = lambda w_: pl.BlockSpec((tm, w_), lambda i: (i, 0))
    halo = lambda col: pl.BlockSpec((HALO, d), _prev_halo(tm, col))
    sq = lambda which: _resident((None, None, d, d), (l, which, 0, 0))
    act = jax.ShapeDtypeStruct((t, d), BF)
    return _run(
        body, comm, name=name, grid=(t // tm,),
        in_specs=[row(9 * d), halo(1), halo(2), halo(3), halo(4), row(d),
                  _resident((m, 2 * d), (0, 0)),
                  _resident((None, K_A, d), (l, 0, 0)), _resident((None, K_B, d), (l, 0, 0)),
                  _resident((1, d), (0, 0)), _resident((1, d), (0, 0)), _resident((1, d), (0, 0)),
                  _resident((1, 3 * d), (0, 0)), sq(0), sq(1), sq(2), sq(3)],
        out_specs=[row(d)] * 9,
        out_shape=[jax.ShapeDtypeStruct((t, d), F32)] + [act] * 7 + [jax.ShapeDtypeStruct((t, d), F32)],
        scratch_shapes=[pltpu.VMEM((HALO + tm, d), F32), pltpu.VMEM((HALO + tm, d), F32),
                        _shifted_scratch(ALL_RESIDUES, tm, d)],
        args=(proj, proj, proj, proj, proj, x, kv, conv_a, conv_b, cbias, ln_g, ln_b, b_gate, wsq, wsq, wsq, wsq))


def _ffn_down_fwd(up, x1, conv_f, w_down, l, name, comm=None):
    t, d = x1.shape
    f2 = up.shape[1]
    f = f2 // 2
    tm = min(TM_ROW, t)

    def body(up_ref, uph_ref, x1_ref, cw_ref, wd_ref, x2_ref, zf_ref, xx_ref):
        i = pl.program_id(0)
        xx_ref[pl.ds(0, HALO), :] = uph_ref[...].astype(F32) * (i > 0).astype(F32)
        xx_ref[pl.ds(HALO, tm), :] = up_ref[...].astype(F32)
        uc = _causal_taps(_Rows(xx_ref), cw_ref, K_F, tm)
        gt = uc[:, 0:f]
        zf = (gt * _sig(gt) * uc[:, f:f2]).astype(BF)
        zf_ref[...] = zf
        x2_ref[...] = x1_ref[...] + _nn(zf, wd_ref[...])

    return _run(
        body, comm, name=name, grid=(t // tm,),
        in_specs=[pl.BlockSpec((tm, f2), lambda i: (i, 0)),
                  pl.BlockSpec((HALO, f2), _prev_halo(tm, 0)),
                  pl.BlockSpec((tm, d), lambda i: (i, 0)),
                  _resident((None, K_F, f2), (l, 0, 0)),
                  _resident((None, f, d), (l, 0, 0))],
        out_specs=[pl.BlockSpec((tm, d), lambda i: (i, 0)), pl.BlockSpec((tm, f), lambda i: (i, 0))],
        out_shape=[jax.ShapeDtypeStruct((t, d), F32), jax.ShapeDtypeStruct((t, f), BF)],
        scratch_shapes=[pltpu.VMEM((HALO + tm, f2), F32)],
        args=(up, up, x1, conv_f, w_down))


def _final_loss(x, g, target, name):
    t, d = x.shape
    tm = min(2 * TM_ROW, t)

    def body(x_ref, g_ref, t_ref, dx_ref, loss_ref, dg_ref):
        @pl.when(pl.program_id(0) == 0)
        def _():
            loss_ref[...] = jnp.zeros_like(loss_ref)
            dg_ref[...] = jnp.zeros_like(dg_ref)

        xf = x_ref[...]
        r = lax.rsqrt(jnp.mean(xf * xf, axis=-1, keepdims=True) + EPS)
        xhat = xf * r
        err = xhat * g_ref[...] - t_ref[...]
        loss_ref[...] += (0.5 / d) * jnp.sum(err * err)
        dy = err * (1.0 / d)
        dg_ref[...] += jnp.sum(dy * xhat, axis=0, keepdims=True)
        dxh = dy * g_ref[...]
        dx_ref[...] = r * (dxh - xhat * jnp.mean(dxh * xhat, axis=-1, keepdims=True))

    return _pallas_call(
        body, name=name, grid=(t // tm,),
        in_specs=[pl.BlockSpec((tm, d), lambda i: (i, 0)), pl.BlockSpec((1, d), lambda i: (0, 0)),
                  pl.BlockSpec((tm, d), lambda i: (i, 0))],
        out_specs=[pl.BlockSpec((tm, d), lambda i: (i, 0)), pl.BlockSpec((8, 128), lambda i: (0, 0)),
                   pl.BlockSpec((1, d), lambda i: (0, 0))],
        out_shape=[jax.ShapeDtypeStruct((t, d), F32), jax.ShapeDtypeStruct((8, 128), F32),
                   jax.ShapeDtypeStruct((1, d), F32)],
        compiler_params=_params(1),
    )(x, g, target)


def _ffn_down_bwd(dx2, up, conv_f, w_down, l, name, comm=None):
    t, d = dx2.shape
    f2 = up.shape[1]
    f = f2 // 2
    tm = min(TM_ROW, t)

    def body(dx2_ref, up_ref, uph_ref, cw_ref, wd_ref, duc_ref, dx2b_ref, dcw_ref, xx_ref):
        i = pl.program_id(0)

        @pl.when(i == 0)
        def _():
            dcw_ref[...] = jnp.zeros_like(dcw_ref)

        xx_ref[pl.ds(0, HALO), :] = uph_ref[...].astype(F32) * (i > 0).astype(F32)
        xx_ref[pl.ds(HALO, tm), :] = up_ref[...].astype(F32)
        uc = _causal_taps(_Rows(xx_ref), cw_ref, K_F, tm)
        gt = uc[:, 0:f]
        sg = _sig(gt)
        dx2b = dx2_ref[...].astype(BF)
        dx2b_ref[...] = dx2b
        dzf = _nt(dx2b, wd_ref[...])
        duc_ref[:, 0:f] = (dzf * uc[:, f:f2] * (sg * (1.0 + gt * (1.0 - sg)))).astype(BF)
        duc_ref[:, f:f2] = (dzf * (gt * sg)).astype(BF)
        _tap_grads(dcw_ref, duc_ref[...].astype(F32), _Rows(xx_ref), K_F, tm)

    return _run(
        body, comm, name=name, grid=(t // tm,),
        in_specs=[pl.BlockSpec((tm, d), lambda i: (i, 0)),
                  pl.BlockSpec((tm, f2), lambda i: (i, 0)),
                  pl.BlockSpec((HALO, f2), _prev_halo(tm, 0)),
                  _resident((None, K_F, f2), (l, 0, 0)),
                  _resident((None, f, d), (l, 0, 0))],
        out_specs=[pl.BlockSpec((tm, f2), lambda i: (i, 0)), pl.BlockSpec((tm, d), lambda i: (i, 0)),
                   pl.BlockSpec((K_F, f2), lambda i: (0, 0))],
        out_shape=[jax.ShapeDtypeStruct((t, f2), BF), jax.ShapeDtypeStruct((t, d), BF),
                   jax.ShapeDtypeStruct((K_F, f2), F32)],
        scratch_shapes=[pltpu.VMEM((HALO + tm, f2), F32)],
        args=(dx2, up, up, conv_f, w_down))


def _ffn_conv_bwd(duc, conv_f, l, name, comm=None):
    t, f2 = duc.shape
    tm = min(TM_ROW, t)
    n_t = t // tm

    def body(duc_ref, nxt_ref, cw_ref, dup_ref, yy_ref):
        i = pl.program_id(0)
        yy_ref[pl.ds(0, tm), :] = duc_ref[...].astype(F32)
        yy_ref[pl.ds(tm, HALO), :] = nxt_ref[...].astype(F32) * (i < n_t - 1).astype(F32)
        dup_ref[...] = _anticausal_taps(_Rows(yy_ref), cw_ref, K_F, tm).astype(BF)

    return _run(
        body, comm, name=name, grid=(n_t,),
        in_specs=[pl.BlockSpec((tm, f2), lambda i: (i, 0)),
                  pl.BlockSpec((HALO, f2), _next_halo(tm, t, 0)),
                  _resident((None, K_F, f2), (l, 0, 0))],
        out_specs=[pl.BlockSpec((tm, f2), lambda i: (i, 0))],
        out_shape=[jax.ShapeDtypeStruct((t, f2), BF)],
        scratch_shapes=[pltpu.VMEM((tm + HALO, f2), F32)],
        args=(duc, duc, conv_f))


def _nt_matmul_norm_bwd(dy, w, gs, x, g, dres, name, comm=None):
    t, n = dy.shape
    d = x.shape[1]
    tm = min(TM_MM // 2, t)
    n_s, _, ns = w.shape
    n_k, tk = n_s // gs, gs * ns

    def body(dy_ref, w_ref, x_ref, g_ref, dres_ref, dx_ref, dxb_ref, dg_ref, acc_ref):
        i, k = pl.program_id(0), pl.program_id(1)

        @pl.when((i == 0) & (k == 0))
        def _():
            dg_ref[...] = jnp.zeros_like(dg_ref)

        @pl.when(k == 0)
        def _():
            acc_ref[...] = jnp.zeros_like(acc_ref)

        part = _nt(dy_ref[:, 0:ns], w_ref[0])
        for s in range(1, gs):
            part = part + _nt(dy_ref[:, s * ns:(s + 1) * ns], w_ref[s])
        acc_ref[...] += part

        @pl.when(k == n_k - 1)
        def _():
            xf = x_ref[...]
            r = lax.rsqrt(jnp.mean(xf * xf, axis=-1, keepdims=True) + EPS)
            xhat = xf * r
            dh = acc_ref[...]
            dg_ref[...] += jnp.sum(dh * xhat, axis=0, keepdims=True)
            dxh = dh * g_ref[...]
            dx = dres_ref[...] + r * (dxh - xhat * jnp.mean(dxh * xhat, axis=-1, keepdims=True))
            dx_ref[...] = dx
            dxb_ref[...] = dx.astype(BF)

    return _run(
        body, comm, name=name, grid=(t // tm, n_k),
        in_specs=[pl.BlockSpec((tm, tk), lambda i, k: (i, k)),
                  pl.BlockSpec((gs, d, ns), lambda i, k: (k, 0, 0)),
                  pl.BlockSpec((tm, d), lambda i, k: (i, 0)),
                  pl.BlockSpec((1, d), lambda i, k: (0, 0)),
                  pl.BlockSpec((tm, d), lambda i, k: (i, 0))],
        out_specs=[pl.BlockSpec((tm, d), lambda i, k: (i, 0)), pl.BlockSpec((tm, d), lambda i, k: (i, 0)),
                   pl.BlockSpec((1, d), lambda i, k: (0, 0))],
        out_shape=[jax.ShapeDtypeStruct((t, d), F32), jax.ShapeDtypeStruct((t, d), BF),
                   jax.ShapeDtypeStruct((1, d), F32)],
        scratch_shapes=[pltpu.VMEM((tm, d), F32)],
        args=(dy, w, x, g, dres))


def _mixer_bwd(dx1b, proj, ya, yb, yc, cb, kv, conv_a, conv_b, ln_g, ln_b, b_gate, wsq, l, name, comm=None):
    t, d = cb.shape
    m = kv.shape[0]
    tm = min(TM_ROW, t)
    hd = d // N_HEADS
    scale = 1.0 / math.sqrt(hd)

    def body(dx1b_ref, proj_ref, gch_ref, vh_ref, u0h_ref, ugh_ref, ya_ref, yb_ref, yc_ref, cb_ref, kv_ref,
             ca_w, cb_w, lng_ref, lnb_ref, bg_ref, wa_ref, wb_ref, wc_ref, wo_ref,
             dpre_ref, dya_ref, dyb_ref, dyc_ref, dkv_ref, dbg_ref, dlng_ref, dlnb_ref, dcbias_ref, dcaw_ref,
             dcbw_ref, xa_ref, xb_ref, sa_ref, sb_ref):
        i = pl.program_id(0)

        @pl.when(i == 0)
        def _():
            for ref in (dkv_ref, dbg_ref, dlng_ref, dlnb_ref, dcbias_ref, dcaw_ref, dcbw_ref):
                ref[...] = jnp.zeros_like(ref)

        _load_branch_inputs(i, proj_ref, gch_ref, vh_ref, u0h_ref, ugh_ref, xa_ref, xb_ref, d)
        xa, xb = _Rows(xa_ref, sa_ref, SHORT_RESIDUES), _Rows(xb_ref, sb_ref, ALL_RESIDUES)
        xa.shift()
        xb.shift()
        dmg = _nt(dx1b_ref[...], wo_ref[...])
        ys = (ya_ref, yb_ref, yc_ref)
        dys = (dya_ref, dyb_ref, dyc_ref)
        for b in range(3):
            gate = _sig(proj_ref[:, (6 + b) * d:(7 + b) * d].astype(F32) + bg_ref[:, b * d:(b + 1) * d])
            dys[b][...] = (gate * dmg).astype(BF)
            dpg = dmg * ys[b][...].astype(F32) * gate * (1.0 - gate)
            dpre_ref[:, (6 + b) * d:(7 + b) * d] = dpg.astype(BF)
            dbg_ref[:, b * d:(b + 1) * d] += jnp.sum(dpg, axis=0, keepdims=True)
        gb = proj_ref[:, 0:d].astype(F32)
        ca = _causal_taps(xa, ca_w, K_A, tm)
        dza = _nt(dya_ref[...], wa_ref[...])
        dpre_ref[:, 0:d] = (dza * ca).astype(BF)
        dpre_ref[:, d:2 * d] = (dza * gb).astype(BF)
        _tap_grads(dcaw_ref, dpre_ref[:, d:2 * d].astype(F32), xa, K_A, tm)
        dpre_ref[:, 2 * d:3 * d] = jnp.zeros((tm, d), BF)
        cbv = cb_ref[...]
        mu = jnp.mean(cbv, axis=-1, keepdims=True)
        dlt = cbv - mu
        rstd = lax.rsqrt(jnp.mean(dlt * dlt, axis=-1, keepdims=True) + EPS)
        xhat = dlt * rstd
        lnb = xhat * lng_ref[...] + lnb_ref[...]
        sg = _sig(lnb)
        dzb = _nt(dyb_ref[...], wb_ref[...])
        dl = dzb * (sg * (1.0 + lnb * (1.0 - sg)))
        dlng_ref[...] += jnp.sum(dl * xhat, axis=0, keepdims=True)
        dlnb_ref[...] += jnp.sum(dl, axis=0, keepdims=True)
        dxh = dl * lng_ref[...]
        dcb = rstd * (dxh - jnp.mean(dxh, axis=-1, keepdims=True)
                      - xhat * jnp.mean(dxh * xhat, axis=-1, keepdims=True))
        dcbias_ref[...] += jnp.sum(dcb, axis=0, keepdims=True)
        dpre_ref[:, 3 * d:4 * d] = dcb.astype(BF)
        _tap_grads(dcbw_ref, dpre_ref[:, 3 * d:4 * d].astype(F32), xb, K_B, tm)
        dpre_ref[:, 4 * d:5 * d] = jnp.zeros((tm, d), BF)
        do = _nt(dyc_ref[...], wc_ref[...]).astype(BF)
        for h in range(N_HEADS):
            qh = proj_ref[:, 5 * d + h * hd:5 * d + (h + 1) * hd]
            kh = kv_ref[:, h * hd:(h + 1) * hd]
            vh = kv_ref[:, d + h * hd:d + (h + 1) * hd]
            doh = do[:, h * hd:(h + 1) * hd]
            p = _softmax_rows(_nt(qh, kh) * scale)
            dp = _nt(doh, vh)
            ds = (p * (dp - jnp.sum(dp * p, axis=-1, keepdims=True)) * scale).astype(BF)
            dpre_ref[:, 5 * d + h * hd:5 * d + (h + 1) * hd] = _nn(ds, kh).astype(BF)
            dkv_ref[:, h * hd:(h + 1) * hd] += _tn(ds, qh)
            dkv_ref[:, d + h * hd:d + (h + 1) * hd] += _tn(p.astype(BF), doh)

    row = lambda w_: pl.BlockSpec((tm, w_), lambda i: (i, 0))
    halo = lambda col: pl.BlockSpec((HALO, d), _prev_halo(tm, col))
    sq = lambda which: _resident((None, None, d, d), (l, which, 0, 0))
    acc = lambda r, c: pl.BlockSpec((r, c), lambda i: (0, 0))
    act = jax.ShapeDtypeStruct((t, d), BF)
    vec = lambda r, c: jax.ShapeDtypeStruct((r, c), F32)
    return _run(
        body, comm, name=name, grid=(t // tm,),
        in_specs=[row(d), row(9 * d), halo(1), halo(2), halo(3), halo(4), row(d), row(d), row(d), row(d),
                  _resident((m, 2 * d), (0, 0)),
                  _resident((None, K_A, d), (l, 0, 0)), _resident((None, K_B, d), (l, 0, 0)),
                  _resident((1, d), (0, 0)), _resident((1, d), (0, 0)), _resident((1, 3 * d), (0, 0)),
                  sq(0), sq(1), sq(2), sq(3)],
        out_specs=[row(9 * d), row(d), row(d), row(d), acc(m, 2 * d), acc(1, 3 * d), acc(1, d), acc(1, d),
                   acc(1, d), acc(K_A, d), acc(K_B, d)],
        out_shape=[jax.ShapeDtypeStruct((t, 9 * d), BF), act, act, act, vec(m, 2 * d), vec(1, 3 * d), vec(1, d),
                   vec(1, d), vec(1, d), vec(K_A, d), vec(K_B, d)],
        scratch_shapes=[pltpu.VMEM((HALO + tm, d), F32), pltpu.VMEM((HALO + tm, d), F32),
                        _shifted_scratch(SHORT_RESIDUES, tm, d), _shifted_scratch(ALL_RESIDUES, tm, d)],
        args=(dx1b, proj, proj, proj, proj, proj, ya, yb, yc, cb, kv, conv_a, conv_b, ln_g, ln_b, b_gate,
              wsq, wsq, wsq, wsq))


def _inproj_conv_bwd(dpre, proj, conv_a, conv_b, l, name, comm=None):
    t, d9 = dpre.shape
    d = d9 // 9
    tm = min(TM_ROW, t)
    n_t = t // tm

    def body(dpre_ref, nxa_ref, nxb_ref, proj_ref, ca_w, cb_w, dproj_ref, ya_ref, yb_ref, sb_ref):
        i = pl.program_id(0)
        keep = (i < n_t - 1).astype(F32)
        ya_ref[pl.ds(0, tm), :] = dpre_ref[:, d:2 * d].astype(F32)
        ya_ref[pl.ds(tm, HALO), :] = nxa_ref[...].astype(F32) * keep
        yb_ref[pl.ds(0, tm), :] = dpre_ref[:, 3 * d:4 * d].astype(F32)
        yb_ref[pl.ds(tm, HALO), :] = nxb_ref[...].astype(F32) * keep
        yb = _Rows(yb_ref, sb_ref, ALL_RESIDUES)
        yb.shift()
        dproj_ref[:, 0:d] = dpre_ref[:, 0:d]
        dproj_ref[:, 5 * d:9 * d] = dpre_ref[:, 5 * d:9 * d]
        dcv = _anticausal_taps(_Rows(ya_ref), ca_w, K_A, tm)
        dproj_ref[:, d:2 * d] = (dcv * proj_ref[:, 2 * d:3 * d].astype(F32)).astype(BF)
        dproj_ref[:, 2 * d:3 * d] = (dcv * proj_ref[:, d:2 * d].astype(F32)).astype(BF)
        dub = _anticausal_taps(yb, cb_w, K_B, tm)
        sg = _sig(proj_ref[:, 4 * d:5 * d].astype(F32))
        dproj_ref[:, 3 * d:4 * d] = (dub * sg).astype(BF)
        dproj_ref[:, 4 * d:5 * d] = (dub * proj_ref[:, 3 * d:4 * d].astype(F32) * sg * (1.0 - sg)).astype(BF)

    return _run(
        body, comm, name=name, grid=(n_t,),
        in_specs=[pl.BlockSpec((tm, d9), lambda i: (i, 0)),
                  pl.BlockSpec((HALO, d), _next_halo(tm, t, 1)),
                  pl.BlockSpec((HALO, d), _next_halo(tm, t, 3)),
                  pl.BlockSpec((tm, d9), lambda i: (i, 0)),
                  _resident((None, K_A, d), (l, 0, 0)), _resident((None, K_B, d), (l, 0, 0))],
        out_specs=[pl.BlockSpec((tm, d9), lambda i: (i, 0))],
        out_shape=[jax.ShapeDtypeStruct((t, d9), BF)],
        scratch_shapes=[pltpu.VMEM((tm + HALO, d), F32), pltpu.VMEM((tm + HALO, d), F32),
                        _shifted_scratch(ALL_RESIDUES, tm, d)],
        args=(dpre, dpre, dpre, proj, conv_a, conv_b))


def _mem_kv_bwd(dkv, memn, mem, g, w_kv, name):
    m, d = mem.shape
    n_s, _, ns = w_kv.shape

    def body(dkv_ref, memn_ref, mem_ref, g_ref, w_ref, dw_ref, dg_ref):
        dkvb = dkv_ref[...].astype(BF)
        dw_ref[...] = _tn(memn_ref[...], dkvb)
        dmemn = _nt(dkvb[:, 0:ns], w_ref[0])
        for s in range(1, n_s):
            dmemn = dmemn + _nt(dkvb[:, s * ns:(s + 1) * ns], w_ref[s])
        xf = mem_ref[...]
        r = lax.rsqrt(jnp.mean(xf * xf, axis=-1, keepdims=True) + EPS)
        dg_ref[...] = jnp.sum(dmemn * (xf * r), axis=0, keepdims=True)

    return _pallas_call(
        body, name=name, grid=(1,),
        in_specs=[pl.BlockSpec((m, 2 * d), lambda i: (0, 0)), pl.BlockSpec((m, d), lambda i: (0, 0)),
                  pl.BlockSpec((m, d), lambda i: (0, 0)), pl.BlockSpec((1, d), lambda i: (0, 0)),
                  pl.BlockSpec((n_s, d, ns), lambda i: (0, 0, 0))],
        out_specs=[pl.BlockSpec((d, 2 * d), lambda i: (0, 0)), pl.BlockSpec((1, d), lambda i: (0, 0))],
        out_shape=[jax.ShapeDtypeStruct((d, 2 * d), F32), jax.ShapeDtypeStruct((1, d), F32)],
        compiler_params=_params(1),
    )(dkv, memn, mem, g, w_kv)


def _dw_matmul(a, b, tn, name, comm=None):
    t, k = a.shape
    n = b.shape[1]
    tt = min(TT_DW, t)
    while tt * k > DW_LHS_ELEMS and tt % 2 == 0:
        tt //= 2

    def body(a_ref, b_ref, o_ref):
        @pl.when(pl.program_id(1) == 0)
        def _():
            o_ref[...] = jnp.zeros_like(o_ref)

        o_ref[...] += _tn(a_ref[...], b_ref[...])

    return _run(
        body, comm, name=name, grid=(n // tn, t // tt),
        in_specs=[pl.BlockSpec((tt, k), lambda j, s: (s, 0)), pl.BlockSpec((tt, tn), lambda j, s: (s, j))],
        out_specs=[pl.BlockSpec((k, tn), lambda j, s: (0, j))],
        out_shape=[jax.ShapeDtypeStruct((k, n), F32)],
        args=(a, b))


class _GradReduce:
    def __init__(self, l, grads, core):
        self.l, self.core, self.names = l, core, tuple(grads)
        self.views = {n: _halves_view(n, g) for n, g in grads.items()}
        self.got, self.sums, self.landing, self.pieces = {}, {}, {}, {}

    def swap_program(self):
        return _swap_program([self.views[n] for n in self.names])

    def swapped(self, outs):
        self.got = dict(zip(self.names, outs))
        for n in self.names:
            self.sums[n], self.landing[n] = _add_halves(self.views[n], self.got[n], self.core, n in COL,
                                                        f"add_halves_{n}_{self.l}")

    def scatter_program(self, names=None):
        names = self.names if names is None else names
        return _scatter_program([self.sums[n] for n in names], [self.landing[n] for n in names], names)

    def scattered(self, outs, names=None):
        self.pieces.update(zip(self.names if names is None else names, outs))


def _step(x, mem, target, sh, conv_sh, small, core):
    depth = len(sh["w_in"])
    d = x.shape[1]
    f2 = sh["w_up"][0].shape[2] * N_CHIPS
    tn_dw_in = min(1024, d)
    tn_dw_up = f2 // 11 if f2 % (11 * 128) == 0 and f2 // 11 >= 128 else f2
    row = lambda v: v.reshape(1, -1)
    one = lambda a: a[None]

    w_in, cab, cf = _run_comm(_gather_program([sh["w_in"][0]] + conv_sh, ("col", "small", "small")), "gather_first")
    saved = []
    for l in range(depth):
        conv = dict(a=cab[l:l + 1, :K_A], b=cab[l:l + 1, K_A:], f=cf[l:l + 1])
        (h, proj), (wsq, w_kv) = _norm_matmul(
            x, row(small["norm_mix_g"][l]), w_in, 1, f"in_proj_{l}",
            _gather_program([sh["wsq"][l], sh["w_kv"][l]], ("row", "col")))
        memn, kv = _mem_kv(mem, row(small["norm_mem_g"][l]), w_kv, f"mem_kv_{l}")
        (x1, za, zb, o, ya, yb, yc, mg, cb), (w_up, w_down) = _mixer_fwd(
            proj, x, kv, conv["a"], conv["b"], row(small["conv_b_bias"][l]), row(small["ln_b_g"][l]),
            row(small["ln_b_b"][l]), row(small["b_gate"][l]), one(wsq), 0, f"mixer_fwd_{l}",
            _gather_program([sh["w_up"][l], sh["w_down"][l]], ("col", "row")))
        more = l + 1 < depth
        nxt = _gather_program([sh["w_in"][l + 1]], ("col",), (0, 2)) if more else None
        (h2, up), w_in_next = _norm_matmul(x1, row(small["norm_ffn_g"][l]), w_up, 2, f"up_proj_{l}", nxt)
        nxt = _gather_program(w_in_next, ("col",), (1, 2)) if more else None
        (x2, zf), w_in_next = _ffn_down_fwd(up, x1, conv["f"], one(w_down), 0, f"ffn_down_fwd_{l}", nxt)
        saved.append(dict(x=x, x1=x1, memn=memn, kv=kv, h=h, proj=proj, za=za, zb=zb, o=o, ya=ya, yb=yb, yc=yc,
                          mg=mg, cb=cb, h2=h2, up=up, zf=zf, w_in=w_in, wsq=one(wsq), w_kv=w_kv,
                          w_up=w_up, w_down=one(w_down), conv=conv))
        x = x2
        w_in = w_in_next[0] if w_in_next else None
    dx, loss, dg_final = _final_loss(x, row(small["norm_final_g"]), target, "final_loss")
    sgrads, pieces = [None] * depth, [None] * depth
    above = None
    first, second, rest = ("w_up",), ("w_down",) + SQUARES, ("w_in", "w_kv")
    for l in reversed(range(depth)):
        s = saved[l]
        conv = s["conv"]
        bottom = l == 0
        (duc, dx2b, dconv_f), got = _ffn_down_bwd(dx, s["up"], conv["f"], s["w_down"], 0, f"ffn_down_bwd_{l}",
                                                  above and above.swap_program())
        if above:
            above.swapped(got)
        (dup,), got = _ffn_conv_bwd(duc, conv["f"], 0, f"ffn_conv_bwd_{l}", above and above.scatter_program(first))
        if above:
            above.scattered(got, first)
        (dx1, dx1b, dg_ffn), got = _nt_matmul_norm_bwd(
            dup, s["w_up"], 2, s["x1"], row(small["norm_ffn_g"][l]), dx, f"up_proj_bwd_{l}",
            above and above.scatter_program(second))
        if above:
            above.scattered(got, second)
        grads = dict(w_up=_dw_matmul(s["h2"], dup, tn_dw_up, f"dw_up_{l}")[0][0],
                     w_down=_dw_matmul(s["zf"], dx2b, d, f"dw_down_{l}")[0][0])
        ffn = _GradReduce(l, grads, core) if bottom else None
        (dpre, dya, dyb, dyc, dkv, dbg, dlng, dlnb, dcbias, dconv_a, dconv_b), got = _mixer_bwd(
            dx1b, s["proj"], s["ya"], s["yb"], s["yc"], s["cb"], s["kv"], conv["a"], conv["b"],
            row(small["ln_b_g"][l]), row(small["ln_b_b"][l]), row(small["b_gate"][l]), s["wsq"], 0,
            f"mixer_bwd_{l}", _join([above and above.scatter_program(rest), ffn and ffn.swap_program()]))
        if above:
            above.scattered(got[:len(rest)], rest)
            pieces[above.l] = above.pieces
            got = got[len(rest):]
        if ffn:
            ffn.swapped(got)
        dw_kv, dg_mem = _mem_kv_bwd(dkv, s["memn"], mem, row(small["norm_mem_g"][l]), s["w_kv"], f"mem_kv_bwd_{l}")
        mix_grads = dict(w_a_out=_dw_matmul(s["za"], dya, d, f"dw_a_out_{l}")[0][0],
                         w_b_out=_dw_matmul(s["zb"], dyb, d, f"dw_b_out_{l}")[0][0],
                         w_att_out=_dw_matmul(s["o"], dyc, d, f"dw_att_out_{l}")[0][0],
                         w_o=_dw_matmul(s["mg"], dx1b, d, f"dw_o_{l}")[0][0], w_kv=dw_kv)
        mix = _GradReduce(l, mix_grads, core) if bottom else None
        (dproj,), got = _inproj_conv_bwd(dpre, s["proj"], conv["a"], conv["b"], 0, f"inproj_conv_bwd_{l}",
                                         _join([ffn and ffn.scatter_program(), mix and mix.swap_program()]))
        if bottom:
            ffn.scattered(got[:len(ffn.names)])
            mix.swapped(got[len(ffn.names):])
        in_grads = dict(w_in=_dw_matmul(s["h"], dproj, tn_dw_in, f"dw_in_{l}")[0][0])
        inp = _GradReduce(l, in_grads, core) if bottom else None
        (dx0, _, dg_mix), got = _nt_matmul_norm_bwd(
            dproj, s["w_in"], 2, s["x"], row(small["norm_mix_g"][l]), dx1, f"in_proj_bwd_{l}",
            _join([mix and mix.scatter_program(), inp and inp.swap_program()]))
        if bottom:
            mix.scattered(got[:len(mix.names)])
            inp.swapped(got[len(mix.names):])
            inp.scattered(_run_comm(inp.scatter_program(), f"scatter_w_in_{l}"))
            pieces[l] = {**ffn.pieces, **mix.pieces, **inp.pieces}
        else:
            above = _GradReduce(l, {**grads, **mix_grads, **in_grads}, core)
        sgrads[l] = dict(norm_mix_g=dg_mix, norm_mem_g=dg_mem, b_gate=dbg, conv_b_bias=dcbias, ln_b_g=dlng,
                         ln_b_b=dlnb, norm_ffn_g=dg_ffn, conv_a_w=dconv_a, conv_b_w=dconv_b, conv_ffn_w=dconv_f)
        dx = dx0
    return loss, dx, sgrads, dg_final, pieces


BIG = ("w_in", "w_a_out", "w_b_out", "w_att_out", "w_o", "w_kv", "w_up", "w_down")
COL = ("w_in", "w_kv", "w_up")
SQUARES = ("w_a_out", "w_b_out", "w_att_out", "w_o")
ANY = pl.BlockSpec(memory_space=pl.ANY)


def _place():
    x, y, c = lax.axis_index("x"), lax.axis_index("y"), lax.axis_index("c")
    chips = [(1 - x, y), (x, 1 - y), (1 - x, 1 - y)]
    return x, y, c, 2 * x + y, chips


def _remote(src, dst, send_sem, recv_sem, dev):
    return pltpu.make_async_remote_copy(src_ref=src, dst_ref=dst, send_sem=send_sem, recv_sem=recv_sem,
                                        device_id=dev, device_id_type=MESH)


class _Striped:
    def __init__(self, src, dst, make):
        rows = src.shape[-2]
        unit = 8 * (4 // jnp.dtype(src.dtype).itemsize)
        n = max(k for k in range(1, MAX_STRIPES + 1) if rows % (unit * k) == 0) if rows % unit == 0 else 1
        q = rows // n
        self.parts = [make(_window(src, pl.ds(i * q, q), slice(None)), _window(dst, pl.ds(i * q, q), slice(None)))
                      for i in range(n)]
        self.whole = make(src, dst)

    def start(self):
        for p in self.parts:
            p.start()

    def wait(self):
        self.whole.wait()

    def wait_send(self):
        self.whole.wait_send()

    def wait_recv(self):
        self.whole.wait_recv()


def _far(src, dst, send_sem, recv_sem, dev):
    return _Striped(src, dst, lambda s, d: _remote(s, d, send_sem, recv_sem, dev))


def _near(src, dst, sem):
    return _Striped(src, dst, lambda s, d: pltpu.make_async_copy(s, d, sem))


def _window(ref, rows, cols):
    return ref.at[(slice(None),) * (len(ref.shape) - 2) + (rows, cols)]


def _row_tile(rows, cols, unit=16, limit=1 << 20):
    best = unit
    for tr in range(unit, rows + 1, unit):
        if rows % tr == 0 and tr * cols <= limit:
            best = tr
    return best


def _cast_place(ws, l, kind, chip, name):
    _, k, n = ws[0].shape
    if kind == "col":
        shape, spec = (N_CHIPS, k, n), pl.BlockSpec((None, k, n), lambda i, c: (c[0], 0, 0))
    elif len(ws) == 1:
        shape, spec = (N_CHIPS * k, n), pl.BlockSpec((k, n), lambda i, c: (c[0], 0))
    else:
        shape, spec = (len(ws), N_CHIPS * k, n), pl.BlockSpec((len(ws), k, n), lambda i, c: (0, c[0], 0))

    def body(c_ref, *refs):
        o_ref = refs[-1]
        if len(ws) == 1 or kind == "col":
            o_ref[...] = refs[0][...].astype(BF)
        else:
            for i in range(len(ws)):
                o_ref[i] = refs[i][...].astype(BF)

    return _pallas_call(
        body, name=name,
        grid_spec=pltpu.PrefetchScalarGridSpec(
            num_scalar_prefetch=1, grid=(1,),
            in_specs=[pl.BlockSpec((None, k, n), lambda i, c: (l, 0, 0))] * len(ws), out_specs=spec),
        out_shape=jax.ShapeDtypeStruct(shape, BF),
        compiler_params=_params(1),
    )(chip, *ws)


def _gather_program(arrays, kinds, part=(0, 1)):
    n_t = len(arrays)
    index, count = part

    def shard_rows(f, kind):
        return f.shape[-2] if kind == "col" else f.shape[-2] // N_CHIPS

    def run(phase, ins, full, sems):
        ici_send, ici_recv, sib_send, sib_recv, loc_sem = sems
        x, y, c, me, chips = _place()
        sibling = (x, y, 1 - c)

        def part_of(i, chip, half):
            f, kind = full[i], kinds[i]
            if kind == "small":
                cols = ins[i].shape[-1]
                return _window(f, slice(None), pl.ds(pl.multiple_of(chip * cols, 128), cols))
            rows = shard_rows(f, kind)
            r = rows // (2 * count)
            at = (half * count + index) * r
            if kind == "col":
                return f.at[chip, pl.ds(pl.multiple_of(at, 16), r), :]
            return _window(f, pl.ds(pl.multiple_of(chip * rows + at, 16), r), slice(None))

        src_part = lambda i, half: ins[i] if kinds[i] == "small" else part_of(i, me, half)
        dst_part = part_of
        local = [_near(ins[i], part_of(i, me, c), loc_sem.at[i]) for i in range(n_t) if kinds[i] == "small"]
        sends = []
        for i in range(n_t):
            for j, chip in enumerate(chips):
                sends.append(_far(src_part(i, c), dst_part(i, me, c), ici_send.at[3 * i + j],
                                  ici_recv.at[3 * i + j], (*chip, c)))
        if phase == "start":
            for cp in local + sends:
                cp.start()
            return
        passed = []
        for i in range(n_t):
            for j, chip in enumerate(chips):
                k = 2 * chip[0] + chip[1]
                landed = dst_part(i, k, c)
                _remote(landed, landed, ici_send.at[3 * i + j], ici_recv.at[3 * i + j], (*chip, c)).wait_recv()
                if kinds[i] != "small":
                    cp = _far(landed, landed, sib_send.at[3 * i + j], sib_recv.at[3 * i + j], sibling)
                    cp.start()
                    passed.append(cp)
        for i in range(n_t):
            if kinds[i] == "small":
                continue
            for j, chip in enumerate(chips):
                k = 2 * chip[0] + chip[1]
                other = dst_part(i, k, 1 - c)
                _remote(other, other, sib_send.at[3 * i + j], sib_recv.at[3 * i + j], sibling).wait_recv()
        for cp in sends + passed:
            cp.wait_send()
        for cp in local:
            cp.wait()

    def out_shape(a, kind):
        shp = a.shape[:-1] + (a.shape[-1] * N_CHIPS,) if kind == "small" else a.shape
        return jax.ShapeDtypeStruct(shp, a.dtype)

    outs = [out_shape(a, k) for a, k in zip(arrays, kinds)]
    sems = [pltpu.SemaphoreType.DMA((3 * n_t,))] * 4 + [pltpu.SemaphoreType.DMA((n_t,))]
    return _Comm(arrays, outs, sems, functools.partial(run, "start"), functools.partial(run, "finish"),
                 {i: i for i in range(n_t) if kinds[i] != "small"})


def _all_reduce_small(part, name):
    r, n = part.shape

    def body(in_ref, out_ref, gather_ref, send_sems, recv_sems):
        x, y, c, _, _ = _place()
        me = 4 * x + 2 * y + c
        gather_ref[me] = in_ref[...]
        sends = []
        for k in range(1, N_DEV):
            peer = (me + k) % N_DEV
            sends.append(_remote(in_ref, gather_ref.at[me], send_sems.at[k - 1], recv_sems.at[k - 1],
                                 (peer // 4, (peer // 2) % 2, peer % 2)))
        for cp in sends:
            cp.start()
        for k in range(1, N_DEV):
            origin = (me + N_DEV - k) % N_DEV
            _remote(in_ref, gather_ref.at[origin], send_sems.at[k - 1], recv_sems.at[k - 1],
                    (x, y, c)).wait_recv()
        for cp in sends:
            cp.wait_send()
        total = gather_ref[0]
        for dev in range(1, N_DEV):
            total = total + gather_ref[dev]
        out_ref[...] = total

    vm = pl.BlockSpec(memory_space=pltpu.VMEM)
    return _pallas_call(
        body, name=name, in_specs=[vm], out_specs=vm, out_shape=jax.ShapeDtypeStruct((r, n), F32),
        scratch_shapes=[pltpu.VMEM((N_DEV, r, n), F32), pltpu.SemaphoreType.DMA((N_DEV - 1,)),
                        pltpu.SemaphoreType.DMA((N_DEV - 1,))],
        compiler_params=pltpu.CompilerParams(vmem_limit_bytes=VMEM_LIMIT),
    )(part)


def _halves_view(name, dw):
    k, n = dw.shape
    s = 1 if name in COL else N_CHIPS
    return dw.reshape(s, 2, k // (2 * s), n)


def _swap_program(views):
    n_t = len(views)

    def run(phase, src, dst, sems):
        send_sems, recv_sems = sems
        x, y, c, _, _ = _place()
        copies = [_far(src[i].at[:, 1 - c], dst[i], send_sems.at[i], recv_sems.at[i], (x, y, 1 - c))
                  for i in range(n_t)]
        for cp in copies:
            if phase == "start":
                cp.start()
            else:
                cp.wait()

    outs = [jax.ShapeDtypeStruct((v.shape[0],) + v.shape[2:], F32) for v in views]
    sems = [pltpu.SemaphoreType.DMA((n_t,)), pltpu.SemaphoreType.DMA((n_t,))]
    return _Comm(views, outs, sems, functools.partial(run, "start"), functools.partial(run, "finish"))


def _add_halves(view, got, place, col, name):
    s, _, r, n = view.shape
    if col:
        cw = n // N_CHIPS
        tr = _row_tile(r, cw)
        grid = (r // tr, N_CHIPS)
        in_specs = [pl.BlockSpec((None, None, tr, cw), lambda j, q, p: (0, p[0], j, q)),
                    pl.BlockSpec((None, tr, cw), lambda j, q, p: (0, j, q))]
        out_specs = [pl.BlockSpec((None, tr, cw), lambda j, q, p: (0, j, q)),
                     pl.BlockSpec((None, None, tr, cw), lambda j, q, p: (p[0], p[1], j, 0))]
    else:
        cw = n
        tr = _row_tile(r, n)
        grid = (s, r // tr)
        in_specs = [pl.BlockSpec((None, None, tr, n), lambda i, j, p: (i, p[0], j, 0)),
                    pl.BlockSpec((None, tr, n), lambda i, j, p: (i, j, 0))]
        out_specs = [pl.BlockSpec((None, tr, n), lambda i, j, p: (i, j, 0)),
                     pl.BlockSpec((None, None, tr, n), lambda i, j, p: (p[0], i, j, 0))]

    def body(p_ref, a_ref, b_ref, o_ref, z_ref):
        total = (a_ref[...] + b_ref[...]).astype(BF)
        o_ref[...] = total
        if col:
            @pl.when(pl.program_id(1) == p_ref[1])
            def _():
                z_ref[...] = total
        else:
            z_ref[...] = total

    return _pallas_call(
        body, name=name,
        grid_spec=pltpu.PrefetchScalarGridSpec(num_scalar_prefetch=1, grid=grid, in_specs=in_specs,
                                               out_specs=out_specs),
        out_shape=[jax.ShapeDtypeStruct((s, r, n), BF), jax.ShapeDtypeStruct((2, N_CHIPS, r, cw), BF)],
        compiler_params=_params(2),
    )(place, view, got)


def _scatter_program(sums, landing, names):
    n_t = len(sums)

    def run(phase, src, dst, sems):
        ici_send, ici_recv, sib_send, sib_recv = sems
        x, y, c, me, chips = _place()
        sibling = (x, y, 1 - c)

        def piece(i, chip):
            if names[i] in COL:
                cw = src[i].shape[2] // N_CHIPS
                return src[i].at[0, :, pl.ds(pl.multiple_of(chip * cw, 128), cw)]
            return src[i].at[chip]

        local = []
        sends = []
        for i in range(n_t):
            sends.append(_far(piece(i, me), dst[i].at[c, me], sib_send.at[4 * i + 3], sib_recv.at[4 * i + 3],
                              sibling))
            for j, chip in enumerate(chips):
                k = 2 * chip[0] + chip[1]
                sends.append(_far(piece(i, k), dst[i].at[c, me], ici_send.at[3 * i + j], ici_recv.at[3 * i + j],
                                  (*chip, c)))
        if phase == "start":
            for cp in local + sends:
                cp.start()
            return
        passed = []
        for i in range(n_t):
            for j, chip in enumerate(chips):
                k = 2 * chip[0] + chip[1]
                landed = dst[i].at[c, k]
                _remote(landed, landed, ici_send.at[3 * i + j], ici_recv.at[3 * i + j], (*chip, c)).wait_recv()
                cp = _far(landed, landed, sib_send.at[4 * i + j], sib_recv.at[4 * i + j], sibling)
                cp.start()
                passed.append(cp)
        for i in range(n_t):
            other = dst[i].at[1 - c, me]
            _remote(other, other, sib_send.at[4 * i + 3], sib_recv.at[4 * i + 3], sibling).wait_recv()
            for j, chip in enumerate(chips):
                k = 2 * chip[0] + chip[1]
                other = dst[i].at[1 - c, k]
                _remote(other, other, sib_send.at[4 * i + j], sib_recv.at[4 * i + j], sibling).wait_recv()
        for cp in sends + passed:
            cp.wait_send()
        for cp in local:
            cp.wait()

    outs = [jax.ShapeDtypeStruct(z.shape, z.dtype) for z in landing]
    sems = [pltpu.SemaphoreType.DMA((3 * n_t,))] * 2 + [pltpu.SemaphoreType.DMA((4 * n_t,))] * 2
    return _Comm(list(sums) + list(landing), outs, sems, functools.partial(run, "start"),
                 functools.partial(run, "finish"), {n_t + i: i for i in range(n_t)})


def _adamw(w, g, m, v):
    m = ADAM_B1 * m + (1.0 - ADAM_B1) * g
    v = ADAM_B2 * v + (1.0 - ADAM_B2) * (g * g)
    m_hat = m / (1.0 - ADAM_B1 ** ADAM_STEP)
    v_hat = v / (1.0 - ADAM_B2 ** ADAM_STEP)
    return -ADAM_LR * (m_hat / (jnp.sqrt(v_hat) + ADAM_EPS) + ADAM_WD * w), m, v


def _adam_shard(pieces, w, m, v, l, prev, name):
    depth, rows, cw = w.shape
    hr = rows // 2
    tr = _row_tile(hr, cw, limit=1 << 18)
    n_i = hr // tr

    def body(*refs):
        z_ref, w_ref, m_ref, v_ref = refs[:4]
        g_ref, d_ref, nm_ref, nv_ref = refs[-4:]
        g = z_ref[0].astype(F32)
        for k in range(1, N_CHIPS):
            g = g + z_ref[k].astype(F32)
        g_ref[...] = g
        d_ref[...], nm_ref[...], nv_ref[...] = _adamw(w_ref[...], g, m_ref[...], v_ref[...])

    par = pl.BlockSpec((None, tr, cw), lambda h, i: (l, h * n_i + i, 0))
    out = jax.ShapeDtypeStruct((depth, rows, cw), F32)
    extra = [] if prev is None else list(prev)
    return _pallas_call(
        body, name=name, grid=(2, n_i),
        in_specs=[pl.BlockSpec((None, N_CHIPS, tr, cw), lambda h, i: (h, 0, i, 0)), par, par, par] + [ANY] * len(extra),
        out_specs=[par] * 4, out_shape=[out] * 4,
        input_output_aliases={4 + k: k for k in range(len(extra))},
        compiler_params=_params(2),
    )(pieces, w, m, v, *extra)


def _adam_small(g, w, m, v, name):
    def body(g_ref, w_ref, m_ref, v_ref, d_ref, nm_ref, nv_ref):
        d_ref[...], nm_ref[...], nv_ref[...] = _adamw(w_ref[...], g_ref[...], m_ref[...], v_ref[...])

    vm = pl.BlockSpec(memory_space=pltpu.VMEM)
    out = jax.ShapeDtypeStruct(g.shape, F32)
    return _pallas_call(body, name=name, in_specs=[vm] * 4, out_specs=[vm] * 3, out_shape=[out] * 3)(g, w, m, v)


WEIGHTS = ("norm_mix_g", "norm_mem_g", "w_in", "b_gate", "conv_a_w", "w_a_out", "conv_b_w", "conv_b_bias", "ln_b_g",
           "ln_b_b", "w_b_out", "w_kv", "w_att_out", "w_o", "norm_ffn_g", "w_up", "conv_ffn_w", "w_down",
           "norm_final_g")
REPLICATED = ("norm_mix_g", "norm_mem_g", "b_gate", "conv_b_bias", "ln_b_g", "ln_b_b", "norm_ffn_g")
CONVS = ("conv_a_w", "conv_b_w", "conv_ffn_w")
PACK_WIDTH = 1024


def _pack(arrays):
    flat = jnp.concatenate([a.reshape(-1) for a in arrays])
    size = -(-flat.shape[0] // (8 * PACK_WIDTH)) * (8 * PACK_WIDTH)
    return jnp.pad(flat, (0, size - flat.shape[0])).reshape(-1, PACK_WIDTH)


def _unpack(packed, shapes):
    flat = packed.reshape(-1)
    out, at = [], 0
    for shp in shapes:
        n = math.prod(shp)
        out.append(flat[at:at + n].reshape(shp))
        at += n
    return out


def kernel(x, mem, norm_mix_g, norm_mem_g, w_in, b_gate, conv_a_w, w_a_out, conv_b_w, conv_b_bias, ln_b_g, ln_b_b, w_b_out, w_kv, w_att_out, w_o, norm_ffn_g, w_up, conv_ffn_w, w_down, norm_final_g, loss_target, m_norm_mix_g, m_norm_mem_g, m_w_in, m_b_gate, m_conv_a_w, m_w_a_out, m_conv_b_w, m_conv_b_bias, m_ln_b_g, m_ln_b_b, m_w_b_out, m_w_kv, m_w_att_out, m_w_o, m_norm_ffn_g, m_w_up, m_conv_ffn_w, m_w_down, m_norm_final_g, v_norm_mix_g, v_norm_mem_g, v_w_in, v_b_gate, v_conv_a_w, v_w_a_out, v_conv_b_w, v_conv_b_bias, v_ln_b_g, v_ln_b_b, v_w_b_out, v_w_kv, v_w_att_out, v_w_o, v_norm_ffn_g, v_w_up, v_conv_ffn_w, v_w_down, v_norm_final_g):
    w = dict(norm_mix_g=norm_mix_g, norm_mem_g=norm_mem_g, w_in=w_in, b_gate=b_gate, conv_a_w=conv_a_w,
             w_a_out=w_a_out, conv_b_w=conv_b_w, conv_b_bias=conv_b_bias, ln_b_g=ln_b_g, ln_b_b=ln_b_b,
             w_b_out=w_b_out, w_kv=w_kv, w_att_out=w_att_out, w_o=w_o, norm_ffn_g=norm_ffn_g, w_up=w_up,
             conv_ffn_w=conv_ffn_w, w_down=w_down, norm_final_g=norm_final_g)
    mom = dict(norm_mix_g=m_norm_mix_g, norm_mem_g=m_norm_mem_g, w_in=m_w_in, b_gate=m_b_gate, conv_a_w=m_conv_a_w,
               w_a_out=m_w_a_out, conv_b_w=m_conv_b_w, conv_b_bias=m_conv_b_bias, ln_b_g=m_ln_b_g, ln_b_b=m_ln_b_b,
               w_b_out=m_w_b_out, w_kv=m_w_kv, w_att_out=m_w_att_out, w_o=m_w_o, norm_ffn_g=m_norm_ffn_g,
               w_up=m_w_up, conv_ffn_w=m_conv_ffn_w, w_down=m_w_down, norm_final_g=m_norm_final_g)
    var = dict(norm_mix_g=v_norm_mix_g, norm_mem_g=v_norm_mem_g, w_in=v_w_in, b_gate=v_b_gate, conv_a_w=v_conv_a_w,
               w_a_out=v_w_a_out, conv_b_w=v_conv_b_w, conv_b_bias=v_conv_b_bias, ln_b_g=v_ln_b_g, ln_b_b=v_ln_b_b,
               w_b_out=v_w_b_out, w_kv=v_w_kv, w_att_out=v_w_att_out, w_o=v_w_o, norm_ffn_g=v_norm_ffn_g,
               w_up=v_w_up, conv_ffn_w=v_conv_ffn_w, w_down=v_w_down, norm_final_g=v_norm_final_g)
    depth = w_in.shape[0]
    chip = 2 * lax.axis_index("x") + lax.axis_index("y")
    core = jnp.stack([lax.axis_index("c"), chip]).astype(jnp.int32)

    chip1 = chip.astype(jnp.int32).reshape(1)
    layers = range(depth)
    sh = dict(w_in=[_cast_place([w_in], l, "col", chip1, f"cast_w_in_{l}") for l in layers],
              wsq=[_cast_place([w[n] for n in SQUARES], l, "row", chip1, f"cast_squares_{l}") for l in layers],
              w_kv=[_cast_place([w_kv], l, "col", chip1, f"cast_w_kv_{l}") for l in layers],
              w_up=[_cast_place([w_up], l, "col", chip1, f"cast_w_up_{l}") for l in layers],
              w_down=[_cast_place([w_down], l, "row", chip1, f"cast_w_down_{l}") for l in layers])
    conv_sh = [jnp.concatenate([conv_a_w, conv_b_w], axis=1), conv_ffn_w]
    small = {n: w[n] for n in REPLICATED + ("norm_final_g",)}

    loss, dx, sgrads, dg_final, pieces = _step(x[0], mem[0], loss_target[0], sh, conv_sh, small, core)

    res = {n: None for n in BIG}
    for l in reversed(range(depth)):
        for n in BIG:
            res[n] = _adam_shard(pieces[l][n], w[n], mom[n], var[n], l, res[n], f"adam_{n}_{l}")

    per_layer = REPLICATED + CONVS
    parts = [sgrads[l][n] for l in range(depth) for n in per_layer] + [dg_final]
    total = _all_reduce_small(_pack(parts), "all_reduce_small")
    shapes = [sgrads[l][n].shape for l in range(depth) for n in per_layer] + [dg_final.shape]
    summed = _unpack(total, shapes)
    g_small = {}
    for k, n in enumerate(per_layer):
        full = jnp.stack([summed[l * len(per_layer) + k] for l in range(depth)])
        if n in CONVS:
            cols = w[n].shape[-1]
            full = lax.dynamic_slice_in_dim(full, chip * cols, cols, axis=2)
        g_small[n] = full.reshape(w[n].shape)
    g_small["norm_final_g"] = summed[-1].reshape(norm_final_g.shape)
    names = per_layer + ("norm_final_g",)
    d_p, m_p, v_p = _adam_small(_pack([g_small[n] for n in names]), _pack([w[n] for n in names]),
                                _pack([mom[n] for n in names]), _pack([var[n] for n in names]), "adam_small")
    shp = [w[n].shape for n in names]
    for n, dl, nm, nv in zip(names, _unpack(d_p, shp), _unpack(m_p, shp), _unpack(v_p, shp)):
        res[n] = (g_small[n], dl, nm, nv)

    loss = lax.psum(loss[0, 0], ("x", "y", "c"))
    return (loss, dx.reshape(x.shape), *[res[n][0] for n in WEIGHTS], *[res[n][1] for n in WEIGHTS],
            *[res[n][2] for n in WEIGHTS], *[res[n][3] for n in WEIGHTS])
```

```python
import functools
import math

import jax
import jax.numpy as jnp
from jax import lax
from jax.experimental import pallas as pl
from jax.experimental.pallas import tpu as pltpu

F32 = jnp.float32
BF = jnp.bfloat16
EPS = 1e-6
N_HEADS = 4
K_A, K_B, K_F = 3, 31, 3
ADAM_LR, ADAM_B1, ADAM_B2, ADAM_EPS, ADAM_WD, ADAM_STEP = 0.001, 0.9, 0.999, 1e-08, 0.01, 10
N_CHIPS = 4
N_DEV = 8
HALO = 32
MAX_STRIPES = 8
TM_ROW = 256
TM_MM = 1024
TT_DW = 4096
DW_LHS_ELEMS = 4 * 1024 * 1024
TR_EW = 128
VMEM_LIMIT = 56 * 1024 * 1024
MESH = pl.DeviceIdType.MESH
_pallas_call = pl.pallas_call


def _params(n_axes):
    return pltpu.CompilerParams(dimension_semantics=("arbitrary",) * n_axes, vmem_limit_bytes=VMEM_LIMIT)


def _resident(shape, index):
    return pl.BlockSpec(shape, lambda *_: index, pipeline_mode=pl.Buffered(1))


def _sig(x):
    return 1.0 / (1.0 + jnp.exp(-x))


def _nt(a, b):
    return lax.dot_general(a, b, (((1,), (1,)), ((), ())), preferred_element_type=F32)


def _tn(a, b):
    return lax.dot_general(a, b, (((0,), (0,)), ((), ())), preferred_element_type=F32)


def _nn(a, b):
    return jnp.dot(a, b, preferred_element_type=F32)


class _Comm:
    def __init__(self, inputs, out_shape, scratch, run, aliases=None):
        self.inputs, self.out_shape, self.scratch, self.run = list(inputs), list(out_shape), list(scratch), run
        self.aliases = dict(aliases or {})


def _join(programs):
    programs = [p for p in programs if p is not None]
    if not programs:
        return None

    def split(seq, counts):
        parts, at = [], 0
        for n in counts:
            parts.append(seq[at:at + n])
            at += n
        return parts

    n_in = [len(p.inputs) for p in programs]
    n_out = [len(p.out_shape) for p in programs]
    n_s = [len(p.scratch) for p in programs]

    def run(phase, ins, outs, sems):
        for p, i, o, s in zip(programs, split(ins, n_in), split(outs, n_out), split(sems, n_s)):
            p.run(phase, i, o, s)

    aliases, in_at, out_at = {}, 0, 0
    for p, i, o in zip(programs, n_in, n_out):
        aliases.update({in_at + a: out_at + b for a, b in p.aliases.items()})
        in_at, out_at = in_at + i, out_at + o
    return _Comm([a for p in programs for a in p.inputs], [a for p in programs for a in p.out_shape],
                 [a for p in programs for a in p.scratch], run, aliases)


def _run(body, comm, *, name, grid, in_specs, out_specs, out_shape, args, scratch_shapes=()):
    n_axes = len(grid)
    if comm is None:
        outs = _pallas_call(body, name=name, grid=grid, in_specs=list(in_specs), out_specs=list(out_specs),
                            out_shape=list(out_shape), scratch_shapes=list(scratch_shapes),
                            compiler_params=_params(n_axes))(*args)
        return list(outs), []
    counts = (len(in_specs), len(comm.inputs), len(out_specs), len(comm.out_shape), len(scratch_shapes),
              len(comm.scratch))

    def hosted(*refs):
        parts, at = [], 0
        for n in counts:
            parts.append(refs[at:at + n])
            at += n
        ins, c_ins, outs, c_outs, scr, c_sems = parts
        step = pl.program_id(0)
        for a in range(1, n_axes):
            step = step * grid[a] + pl.program_id(a)
        total = math.prod(grid)
        late = max(0, total - 1 - max(1, total // 4))

        @pl.when(step == 0)
        def _():
            comm.run("start", c_ins, c_outs, c_sems)

        body(*ins, *outs, *scr)

        @pl.when(step == late)
        def _():
            comm.run("forward", c_ins, c_outs, c_sems)

        @pl.when(step == total - 1)
        def _():
            comm.run("finish", c_ins, c_outs, c_sems)

    any_spec = pl.BlockSpec(memory_space=pl.ANY)
    res = _pallas_call(
        hosted, name=name, grid=grid, in_specs=list(in_specs) + [any_spec] * counts[1],
        out_specs=list(out_specs) + [any_spec] * counts[3], out_shape=list(out_shape) + comm.out_shape,
        scratch_shapes=list(scratch_shapes) + comm.scratch, compiler_params=_params(n_axes),
        input_output_aliases={counts[0] + a: counts[2] + b for a, b in comm.aliases.items()},
    )(*args, *comm.inputs)
    return list(res[:counts[2]]), list(res[counts[2]:])


def _run_comm(comm, name):
    n_in, n_out = len(comm.inputs), len(comm.out_shape)

    def body(*refs):
        ins, outs, sems = refs[:n_in], refs[n_in:n_in + n_out], refs[n_in + n_out:]
        for phase in ("start", "forward", "finish"):
            comm.run(phase, ins, outs, sems)

    any_spec = pl.BlockSpec(memory_space=pl.ANY)
    return list(_pallas_call(body, name=name, in_specs=[any_spec] * n_in, out_specs=[any_spec] * n_out,
                             out_shape=comm.out_shape, scratch_shapes=comm.scratch,
                             input_output_aliases=comm.aliases)(*comm.inputs))


SUBLANES = 8
ALL_RESIDUES = tuple(range(1, SUBLANES))
SHORT_RESIDUES = tuple(sorted({(HALO - K_A + 1 + k) % SUBLANES for k in range(K_A)} - {0}))


class _Rows:
    def __init__(self, ref, shifted_ref=None, residues=()):
        self.ref, self.shifted_ref, self.residues = ref, shifted_ref, tuple(residues)

    def shift(self):
        n = self.shifted_ref.shape[1]
        for j, b in enumerate(self.residues):
            self.shifted_ref[j] = self.ref[pl.ds(b, n), :]

    def at(self, offset, tm):
        b = offset % SUBLANES
        if b in self.residues:
            return self.shifted_ref[self.residues.index(b), pl.ds(offset - b, tm), :]
        return self.ref[pl.ds(offset, tm), :]


def _shifted_scratch(residues, tm, c):
    return pltpu.VMEM((len(residues), tm + HALO - SUBLANES, c), F32)


def _causal_taps(xx, w_ref, k_taps, tm):
    base = HALO - k_taps + 1
    acc = w_ref[0:1, :] * xx.at(base, tm)
    for k in range(1, k_taps):
        acc = acc + w_ref[k:k + 1, :] * xx.at(base + k, tm)
    return acc


def _anticausal_taps(yy, w_ref, k_taps, tm):
    acc = w_ref[k_taps - 1:k_taps, :] * yy.at(0, tm)
    for k in range(k_taps - 1):
        acc = acc + w_ref[k:k + 1, :] * yy.at(k_taps - 1 - k, tm)
    return acc


def _tap_grads(dw_ref, dy, xx, k_taps, tm):
    base = HALO - k_taps + 1
    for k in range(k_taps):
        dw_ref[k:k + 1, :] += jnp.sum(dy * xx.at(base + k, tm), axis=0, keepdims=True)


def _prev_halo(tm, col):
    return lambda i: (jnp.maximum(i * (tm // HALO) - 1, 0), col)


def _next_halo(tm, n_rows, col):
    return lambda i: (jnp.minimum((i + 1) * (tm // HALO), n_rows // HALO - 1), col)


def _norm_matmul(x, g, w, gs, name, comm=None):
    t, d = x.shape
    n_s, _, ns = w.shape
    n = n_s * ns
    tm = min(TM_MM, t)

    def body(x_ref, g_ref, w_ref, h_ref, y_ref):
        @pl.when(pl.program_id(1) == 0)
        def _():
            xf = x_ref[...]
            r = lax.rsqrt(jnp.mean(xf * xf, axis=-1, keepdims=True) + EPS)
            h_ref[...] = ((xf * r) * g_ref[...]).astype(BF)

        for s in range(gs):
            y_ref[:, s * ns:(s + 1) * ns] = _nn(h_ref[...], w_ref[s]).astype(BF)

    return _run(
        body, comm, name=name, grid=(t // tm, n_s // gs),
        in_specs=[pl.BlockSpec((tm, d), lambda i, j: (i, 0)),
                  pl.BlockSpec((1, d), lambda i, j: (0, 0)),
                  pl.BlockSpec((gs, d, ns), lambda i, j: (j, 0, 0))],
        out_specs=[pl.BlockSpec((tm, d), lambda i, j: (i, 0)),
                   pl.BlockSpec((tm, gs * ns), lambda i, j: (i, j))],
        out_shape=[jax.ShapeDtypeStruct((t, d), BF), jax.ShapeDtypeStruct((t, n), BF)],
        args=(x, g, w))


def _mem_kv(mem, g, w_kv, name):
    m, d = mem.shape
    n_s, _, ns = w_kv.shape

    def body(mem_ref, g_ref, w_ref, memn_ref, kv_ref):
        xf = mem_ref[...]
        r = lax.rsqrt(jnp.mean(xf * xf, axis=-1, keepdims=True) + EPS)
        memn = ((xf * r) * g_ref[...]).astype(BF)
        memn_ref[...] = memn
        for s in range(n_s):
            kv_ref[:, s * ns:(s + 1) * ns] = _nn(memn, w_ref[s]).astype(BF)

    return _pallas_call(
        body, name=name, grid=(1,),
        in_specs=[pl.BlockSpec((m, d), lambda i: (0, 0)),
                  pl.BlockSpec((1, d), lambda i: (0, 0)),
                  pl.BlockSpec((n_s, d, ns), lambda i: (0, 0, 0))],
        out_specs=[pl.BlockSpec((m, d), lambda i: (0, 0)),
                   pl.BlockSpec((m, 2 * d), lambda i: (0, 0))],
        out_shape=[jax.ShapeDtypeStruct((m, d), BF), jax.ShapeDtypeStruct((m, 2 * d), BF)],
        compiler_params=_params(1),
    )(mem, g, w_kv)


def _load_branch_inputs(i, proj_ref, gch_ref, vh_ref, u0h_ref, ugh_ref, xa_ref, xb_ref, d):
    gc = proj_ref[:, d:2 * d].astype(F32)
    v = proj_ref[:, 2 * d:3 * d].astype(F32)
    u0 = proj_ref[:, 3 * d:4 * d].astype(F32)
    ug = proj_ref[:, 4 * d:5 * d].astype(F32)
    keep = (i > 0).astype(F32)
    xa_ref[pl.ds(0, HALO), :] = gch_ref[...].astype(F32) * vh_ref[...].astype(F32) * keep
    xa_ref[pl.ds(HALO, gc.shape[0]), :] = gc * v
    xb_ref[pl.ds(0, HALO), :] = u0h_ref[...].astype(F32) * _sig(ugh_ref[...].astype(F32)) * keep
    xb_ref[pl.ds(HALO, gc.shape[0]), :] = u0 * _sig(ug)


def _softmax_rows(s):
    e = jnp.exp(s - jnp.max(s, axis=-1, keepdims=True))
    return e / jnp.sum(e, axis=-1, keepdims=True)


def _mixer_fwd(proj, x, kv, conv_a, conv_b, cbias, ln_g, ln_b, b_gate, wsq, l, name, comm=None):
    t, d = x.shape
    m = kv.shape[0]
    tm = min(TM_ROW, t)
    hd = d // N_HEADS
    scale = 1.0 / math.sqrt(hd)

    def body(proj_ref, gch_ref, vh_ref, u0h_ref, ugh_ref, x_ref, kv_ref, ca_w, cb_w, cbias_ref, lng_ref, lnb_ref,
             bg_ref, wa_ref, wb_ref, wc_ref, wo_ref,
             x1_ref, za_ref, zb_ref, o_ref, ya_ref, yb_ref, yc_ref, mg_ref, cb_ref, xa_ref, xb_ref, sb_ref):
        i = pl.program_id(0)
        _load_branch_inputs(i, proj_ref, gch_ref, vh_ref, u0h_ref, ugh_ref, xa_ref, xb_ref, d)
        xb = _Rows(xb_ref, sb_ref, ALL_RESIDUES)
        xb.shift()
        gb = proj_ref[:, 0:d].astype(F32)
        za = (gb * _causal_taps(_Rows(xa_ref), ca_w, K_A, tm)).astype(BF)
        za_ref[...] = za
        ya = _nn(za, wa_ref[...])
        ya_ref[...] = ya.astype(BF)
        cb = _causal_taps(xb, cb_w, K_B, tm) + cbias_ref[...]
        cb_ref[...] = cb
        mu = jnp.mean(cb, axis=-1, keepdims=True)
        dlt = cb - mu
        rstd = lax.rsqrt(jnp.mean(dlt * dlt, axis=-1, keepdims=True) + EPS)
        lnb = (dlt * rstd) * lng_ref[...] + lnb_ref[...]
        zb = (lnb * _sig(lnb)).astype(BF)
        zb_ref[...] = zb
        yb = _nn(zb, wb_ref[...])
        yb_ref[...] = yb.astype(BF)
        for h in range(N_HEADS):
            qh = proj_ref[:, 5 * d + h * hd:5 * d + (h + 1) * hd]
            kh = kv_ref[:, h * hd:(h + 1) * hd]
            vh = kv_ref[:, d + h * hd:d + (h + 1) * hd]
            p = _softmax_rows(_nt(qh, kh) * scale)
            o_ref[:, h * hd:(h + 1) * hd] = _nn(p.astype(BF), vh).astype(BF)
        yc = _nn(o_ref[...], wc_ref[...])
        yc_ref[...] = yc.astype(BF)
        g0 = _sig(proj_ref[:, 6 * d:7 * d].astype(F32) + bg_ref[:, 0:d])
        g1 = _sig(proj_ref[:, 7 * d:8 * d].astype(F32) + bg_ref[:, d:2 * d])
        g2 = _sig(proj_ref[:, 8 * d:9 * d].astype(F32) + bg_ref[:, 2 * d:3 * d])
        mg = (g0 * ya + g1 * yb + g2 * yc).astype(BF)
        mg_ref[...] = mg
        x1_ref[...] = x_ref[...] + _nn(mg, wo_ref[...])

    row = lambda w_: pl.BlockSpec((tm, w_), lambda i: (i, 0))
    halo = lambda col: pl.BlockSpec((HALO, d), _prev_halo(tm, col))
    sq = lambda which: _resident((None, None, d, d), (l, which, 0, 0))
    act = jax.ShapeDtypeStruct((t, d), BF)
    return _run(
        body, comm, name=name, grid=(t // tm,),
        in_specs=[row(9 * d), halo(1), halo(2), halo(3), halo(4), row(d),
                  _resident((m, 2 * d), (0, 0)),
                  _resident((None, K_A, d), (l, 0, 0)), _resident((None, K_B, d), (l, 0, 0)),
                  _resident((1, d), (0, 0)), _resident((1, d), (0, 0)), _resident((1, d), (0, 0)),
                  _resident((1, 3 * d), (0, 0)), sq(0), sq(1), sq(2), sq(3)],
        out_specs=[row(d)] * 9,
        out_shape=[jax.ShapeDtypeStruct((t, d), F32)] + [act] * 7 + [jax.ShapeDtypeStruct((t, d), F32)],
        scratch_shapes=[pltpu.VMEM((HALO + tm, d), F32), pltpu.VMEM((HALO + tm, d), F32),
                        _shifted_scratch(ALL_RESIDUES, tm, d)],
        args=(proj, proj, proj, proj, proj, x, kv, conv_a, conv_b, cbias, ln_g, ln_b, b_gate, wsq, wsq, wsq, wsq))


def _ffn_down_fwd(up, x1, conv_f, w_down, l, name, comm=None):
    t, d = x1.shape
    f2 = up.shape[1]
    f = f2 // 2
    tm = min(TM_ROW, t)

    def body(up_ref, uph_ref, x1_ref, cw_ref, wd_ref, x2_ref, zf_ref, uc_ref, xx_ref):
        i = pl.program_id(0)
        xx_ref[pl.ds(0, HALO), :] = uph_ref[...].astype(F32) * (i > 0).astype(F32)
        xx_ref[pl.ds(HALO, tm), :] = up_ref[...].astype(F32)
        uc = _causal_taps(_Rows(xx_ref), cw_ref, K_F, tm)
        uc_ref[...] = uc.astype(BF)
        gt = uc[:, 0:f]
        zf = (gt * _sig(gt) * uc[:, f:f2]).astype(BF)
        zf_ref[...] = zf
        x2_ref[...] = x1_ref[...] + _nn(zf, wd_ref[...])

    return _run(
        body, comm, name=name, grid=(t // tm,),
        in_specs=[pl.BlockSpec((tm, f2), lambda i: (i, 0)),
                  pl.BlockSpec((HALO, f2), _prev_halo(tm, 0)),
                  pl.BlockSpec((tm, d), lambda i: (i, 0)),
                  _resident((None, K_F, f2), (l, 0, 0)),
                  _resident((None, f, d), (l, 0, 0))],
        out_specs=[pl.BlockSpec((tm, d), lambda i: (i, 0)), pl.BlockSpec((tm, f), lambda i: (i, 0)),
                   pl.BlockSpec((tm, f2), lambda i: (i, 0))],
        out_shape=[jax.ShapeDtypeStruct((t, d), F32), jax.ShapeDtypeStruct((t, f), BF),
                   jax.ShapeDtypeStruct((t, f2), BF)],
        scratch_shapes=[pltpu.VMEM((HALO + tm, f2), F32)],
        args=(up, up, x1, conv_f, w_down))


def _final_loss(x, g, target, name):
    t, d = x.shape
    tm = min(2 * TM_ROW, t)

    def body(x_ref, g_ref, t_ref, dx_ref, loss_ref, dg_ref):
        @pl.when(pl.program_id(0) == 0)
        def _():
            loss_ref[...] = jnp.zeros_like(loss_ref)
            dg_ref[...] = jnp.zeros_like(dg_ref)

        xf = x_ref[...]
        r = lax.rsqrt(jnp.mean(xf * xf, axis=-1, keepdims=True) + EPS)
        xhat = xf * r
        err = xhat * g_ref[...] - t_ref[...]
        loss_ref[...] += (0.5 / d) * jnp.sum(err * err)
        dy = err * (1.0 / d)
        dg_ref[...] += jnp.sum(dy * xhat, axis=0, keepdims=True)
        dxh = dy * g_ref[...]
        dx_ref[...] = r * (dxh - xhat * jnp.mean(dxh * xhat, axis=-1, keepdims=True))

    return _pallas_call(
        body, name=name, grid=(t // tm,),
        in_specs=[pl.BlockSpec((tm, d), lambda i: (i, 0)), pl.BlockSpec((1, d), lambda i: (0, 0)),
                  pl.BlockSpec((tm, d), lambda i: (i, 0))],
        out_specs=[pl.BlockSpec((tm, d), lambda i: (i, 0)), pl.BlockSpec((8, 128), lambda i: (0, 0)),
                   pl.BlockSpec((1, d), lambda i: (0, 0))],
        out_shape=[jax.ShapeDtypeStruct((t, d), F32), jax.ShapeDtypeStruct((8, 128), F32),
                   jax.ShapeDtypeStruct((1, d), F32)],
        compiler_params=_params(1),
    )(x, g, target)


def _ffn_down_bwd(dx2, up, uc, w_down, l, name, comm=None):
    t, d = dx2.shape
    f2 = up.shape[1]
    f = f2 // 2
    tm = min(TM_ROW, t)

    def body(dx2_ref, up_ref, uph_ref, uc_ref, wd_ref, duc_ref, dx2b_ref, dcw_ref, xx_ref):
        i = pl.program_id(0)

        @pl.when(i == 0)
        def _():
            dcw_ref[...] = jnp.zeros_like(dcw_ref)

        xx_ref[pl.ds(0, HALO), :] = uph_ref[...].astype(F32) * (i > 0).astype(F32)
        xx_ref[pl.ds(HALO, tm), :] = up_ref[...].astype(F32)
        gt = uc_ref[:, 0:f].astype(F32)
        sg = _sig(gt)
        dx2b = dx2_ref[...].astype(BF)
        dx2b_ref[...] = dx2b
        dzf = _nt(dx2b, wd_ref[...])
        duc_ref[:, 0:f] = (dzf * uc_ref[:, f:f2].astype(F32) * (sg * (1.0 + gt * (1.0 - sg)))).astype(BF)
        duc_ref[:, f:f2] = (dzf * (gt * sg)).astype(BF)
        _tap_grads(dcw_ref, duc_ref[...].astype(F32), _Rows(xx_ref), K_F, tm)

    return _run(
        body, comm, name=name, grid=(t // tm,),
        in_specs=[pl.BlockSpec((tm, d), lambda i: (i, 0)),
                  pl.BlockSpec((tm, f2), lambda i: (i, 0)),
                  pl.BlockSpec((HALO, f2), _prev_halo(tm, 0)),
                  pl.BlockSpec((tm, f2), lambda i: (i, 0)),
                  _resident((None, f, d), (l, 0, 0))],
        out_specs=[pl.BlockSpec((tm, f2), lambda i: (i, 0)), pl.BlockSpec((tm, d), lambda i: (i, 0)),
                   pl.BlockSpec((K_F, f2), lambda i: (0, 0))],
        out_shape=[jax.ShapeDtypeStruct((t, f2), BF), jax.ShapeDtypeStruct((t, d), BF),
                   jax.ShapeDtypeStruct((K_F, f2), F32)],
        scratch_shapes=[pltpu.VMEM((HALO + tm, f2), F32)],
        args=(dx2, up, up, uc, w_down))


def _ffn_conv_bwd(duc, conv_f, l, name, comm=None):
    t, f2 = duc.shape
    tm = min(TM_ROW, t)
    n_t = t // tm

    def body(duc_ref, nxt_ref, cw_ref, dup_ref, yy_ref):
        i = pl.program_id(0)
        yy_ref[pl.ds(0, tm), :] = duc_ref[...].astype(F32)
        yy_ref[pl.ds(tm, HALO), :] = nxt_ref[...].astype(F32) * (i < n_t - 1).astype(F32)
        dup_ref[...] = _anticausal_taps(_Rows(yy_ref), cw_ref, K_F, tm).astype(BF)

    return _run(
        body, comm, name=name, grid=(n_t,),
        in_specs=[pl.BlockSpec((tm, f2), lambda i: (i, 0)),
                  pl.BlockSpec((HALO, f2), _next_halo(tm, t, 0)),
                  _resident((None, K_F, f2), (l, 0, 0))],
        out_specs=[pl.BlockSpec((tm, f2), lambda i: (i, 0))],
        out_shape=[jax.ShapeDtypeStruct((t, f2), BF)],
        scratch_shapes=[pltpu.VMEM((tm + HALO, f2), F32)],
        args=(duc, duc, conv_f))


def _nt_matmul_norm_bwd(dy, w, gs, x, g, dres, name, comm=None):
    t, n = dy.shape
    d = x.shape[1]
    tm = min(TM_MM // 2, t)
    n_s, _, ns = w.shape
    n_k, tk = n_s // gs, gs * ns

    def body(dy_ref, w_ref, x_ref, g_ref, dres_ref, dx_ref, dxb_ref, dg_ref, acc_ref):
        i, k = pl.program_id(0), pl.program_id(1)

        @pl.when((i == 0) & (k == 0))
        def _():
            dg_ref[...] = jnp.zeros_like(dg_ref)

        @pl.when(k == 0)
        def _():
            acc_ref[...] = jnp.zeros_like(acc_ref)

        part = _nt(dy_ref[:, 0:ns], w_ref[0])
        for s in range(1, gs):
            part = part + _nt(dy_ref[:, s * ns:(s + 1) * ns], w_ref[s])
        acc_ref[...] += part

        @pl.when(k == n_k - 1)
        def _():
            xf = x_ref[...]
            r = lax.rsqrt(jnp.mean(xf * xf, axis=-1, keepdims=True) + EPS)
            xhat = xf * r
            dh = acc_ref[...]
            dg_ref[...] += jnp.sum(dh * xhat, axis=0, keepdims=True)
            dxh = dh * g_ref[...]
            dx = dres_ref[...] + r * (dxh - xhat * jnp.mean(dxh * xhat, axis=-1, keepdims=True))
            dx_ref[...] = dx
            dxb_ref[...] = dx.astype(BF)

    return _run(
        body, comm, name=name, grid=(t // tm, n_k),
        in_specs=[pl.BlockSpec((tm, tk), lambda i, k: (i, k)),
                  pl.BlockSpec((gs, d, ns), lambda i, k: (k, 0, 0)),
                  pl.BlockSpec((tm, d), lambda i, k: (i, 0)),
                  pl.BlockSpec((1, d), lambda i, k: (0, 0)),
                  pl.BlockSpec((tm, d), lambda i, k: (i, 0))],
        out_specs=[pl.BlockSpec((tm, d), lambda i, k: (i, 0)), pl.BlockSpec((tm, d), lambda i, k: (i, 0)),
                   pl.BlockSpec((1, d), lambda i, k: (0, 0))],
        out_shape=[jax.ShapeDtypeStruct((t, d), F32), jax.ShapeDtypeStruct((t, d), BF),
                   jax.ShapeDtypeStruct((1, d), F32)],
        scratch_shapes=[pltpu.VMEM((tm, d), F32)],
        args=(dy, w, x, g, dres))


def _mixer_bwd(dx1b, proj, ya, yb, yc, cb, kv, conv_a, conv_b, ln_g, ln_b, b_gate, wsq, l, name, comm=None):
    t, d = cb.shape
    m = kv.shape[0]
    tm = min(TM_ROW, t)
    hd = d // N_HEADS
    scale = 1.0 / math.sqrt(hd)

    def body(dx1b_ref, proj_ref, gch_ref, vh_ref, u0h_ref, ugh_ref, ya_ref, yb_ref, yc_ref, cb_ref, kv_ref,
             ca_w, cb_w, lng_ref, lnb_ref, bg_ref, wa_ref, wb_ref, wc_ref, wo_ref,
             dpre_ref, dya_ref, dyb_ref, dyc_ref, dkv_ref, dbg_ref, dlng_ref, dlnb_ref, dcbias_ref, dcaw_ref,
             dcbw_ref, xa_ref, xb_ref, sa_ref, sb_ref):
        i = pl.program_id(0)

        @pl.when(i == 0)
        def _():
            for ref in (dkv_ref, dbg_ref, dlng_ref, dlnb_ref, dcbias_ref, dcaw_ref, dcbw_ref):
                ref[...] = jnp.zeros_like(ref)

        _load_branch_inputs(i, proj_ref, gch_ref, vh_ref, u0h_ref, ugh_ref, xa_ref, xb_ref, d)
        xa, xb = _Rows(xa_ref, sa_ref, SHORT_RESIDUES), _Rows(xb_ref, sb_ref, ALL_RESIDUES)
        xa.shift()
        xb.shift()
        dmg = _nt(dx1b_ref[...], wo_ref[...])
        ys = (ya_ref, yb_ref, yc_ref)
        dys = (dya_ref, dyb_ref, dyc_ref)
        for b in range(3):
            gate = _sig(proj_ref[:, (6 + b) * d:(7 + b) * d].astype(F32) + bg_ref[:, b * d:(b + 1) * d])
            dys[b][...] = (gate * dmg).astype(BF)
            dpg = dmg * ys[b][...].astype(F32) * gate * (1.0 - gate)
            dpre_ref[:, (6 + b) * d:(7 + b) * d] = dpg.astype(BF)
            dbg_ref[:, b * d:(b + 1) * d] += jnp.sum(dpg, axis=0, keepdims=True)
        gb = proj_ref[:, 0:d].astype(F32)
        ca = _causal_taps(xa, ca_w, K_A, tm)
        dza = _nt(dya_ref[...], wa_ref[...])
        dpre_ref[:, 0:d] = (dza * ca).astype(BF)
        dpre_ref[:, d:2 * d] = (dza * gb).astype(BF)
        _tap_grads(dcaw_ref, dpre_ref[:, d:2 * d].astype(F32), xa, K_A, tm)
        dpre_ref[:, 2 * d:3 * d] = jnp.zeros((tm, d), BF)
        cbv = cb_ref[...]
        mu = jnp.mean(cbv, axis=-1, keepdims=True)
        dlt = cbv - mu
        rstd = lax.rsqrt(jnp.mean(dlt * dlt, axis=-1, keepdims=True) + EPS)
        xhat = dlt * rstd
        lnb = xhat * lng_ref[...] + lnb_ref[...]
        sg = _sig(lnb)
        dzb = _nt(dyb_ref[...], wb_ref[...])
        dl = dzb * (sg * (1.0 + lnb * (1.0 - sg)))
        dlng_ref[...] += jnp.sum(dl * xhat, axis=0, keepdims=True)
        dlnb_ref[...] += jnp.sum(dl, axis=0, keepdims=True)
        dxh = dl * lng_ref[...]
        dcb = rstd * (dxh - jnp.mean(dxh, axis=-1, keepdims=True)
                      - xhat * jnp.mean(dxh * xhat, axis=-1, keepdims=True))
        dcbias_ref[...] += jnp.sum(dcb, axis=0, keepdims=True)
        dpre_ref[:, 3 * d:4 * d] = dcb.astype(BF)
        _tap_grads(dcbw_ref, dpre_ref[:, 3 * d:4 * d].astype(F32), xb, K_B, tm)
        dpre_ref[:, 4 * d:5 * d] = jnp.zeros((tm, d), BF)
        do = _nt(dyc_ref[...], wc_ref[...]).astype(BF)
        for h in range(N_HEADS):
            qh = proj_ref[:, 5 * d + h * hd:5 * d + (h + 1) * hd]
            kh = kv_ref[:, h * hd:(h + 1) * hd]
            vh = kv_ref[:, d + h * hd:d + (h + 1) * hd]
            doh = do[:, h * hd:(h + 1) * hd]
            p = _softmax_rows(_nt(qh, kh) * scale)
            dp = _nt(doh, vh)
            ds = (p * (dp - jnp.sum(dp * p, axis=-1, keepdims=True)) * scale).astype(BF)
            dpre_ref[:, 5 * d + h * hd:5 * d + (h + 1) * hd] = _nn(ds, kh).astype(BF)
            dkv_ref[:, h * hd:(h + 1) * hd] += _tn(ds, qh)
            dkv_ref[:, d + h * hd:d + (h + 1) * hd] += _tn(p.astype(BF), doh)

    row = lambda w_: pl.BlockSpec((tm, w_), lambda i: (i, 0))
    halo = lambda col: pl.BlockSpec((HALO, d), _prev_halo(tm, col))
    sq = lambda which: _resident((None, None, d, d), (l, which, 0, 0))
    acc = lambda r, c: pl.BlockSpec((r, c), lambda i: (0, 0))
    act = jax.ShapeDtypeStruct((t, d), BF)
    vec = lambda r, c: jax.ShapeDtypeStruct((r, c), F32)
    return _run(
        body, comm, name=name, grid=(t // tm,),
        in_specs=[row(d), row(9 * d), halo(1), halo(2), halo(3), halo(4), row(d), row(d), row(d), row(d),
                  _resident((m, 2 * d), (0, 0)),
                  _resident((None, K_A, d), (l, 0, 0)), _resident((None, K_B, d), (l, 0, 0)),
                  _resident((1, d), (0, 0)), _resident((1, d), (0, 0)), _resident((1, 3 * d), (0, 0)),
                  sq(0), sq(1), sq(2), sq(3)],
        out_specs=[row(9 * d), row(d), row(d), row(d), acc(m, 2 * d), acc(1, 3 * d), acc(1, d), acc(1, d),
                   acc(1, d), acc(K_A, d), acc(K_B, d)],
        out_shape=[jax.ShapeDtypeStruct((t, 9 * d), BF), act, act, act, vec(m, 2 * d), vec(1, 3 * d), vec(1, d),
                   vec(1, d), vec(1, d), vec(K_A, d), vec(K_B, d)],
        scratch_shapes=[pltpu.VMEM((HALO + tm, d), F32), pltpu.VMEM((HALO + tm, d), F32),
                        _shifted_scratch(SHORT_RESIDUES, tm, d), _shifted_scratch(ALL_RESIDUES, tm, d)],
        args=(dx1b, proj, proj, proj, proj, proj, ya, yb, yc, cb, kv, conv_a, conv_b, ln_g, ln_b, b_gate,
              wsq, wsq, wsq, wsq))


def _inproj_conv_bwd(dpre, proj, conv_a, conv_b, l, name, comm=None):
    t, d9 = dpre.shape
    d = d9 // 9
    tm = min(TM_ROW, t)
    n_t = t // tm

    def body(dpre_ref, nxa_ref, nxb_ref, proj_ref, ca_w, cb_w, dproj_ref, ya_ref, yb_ref, sb_ref):
        i = pl.program_id(0)
        keep = (i < n_t - 1).astype(F32)
        ya_ref[pl.ds(0, tm), :] = dpre_ref[:, d:2 * d].astype(F32)
        ya_ref[pl.ds(tm, HALO), :] = nxa_ref[...].astype(F32) * keep
        yb_ref[pl.ds(0, tm), :] = dpre_ref[:, 3 * d:4 * d].astype(F32)
        yb_ref[pl.ds(tm, HALO), :] = nxb_ref[...].astype(F32) * keep
        yb = _Rows(yb_ref, sb_ref, ALL_RESIDUES)
        yb.shift()
        dproj_ref[:, 0:d] = dpre_ref[:, 0:d]
        dproj_ref[:, 5 * d:9 * d] = dpre_ref[:, 5 * d:9 * d]
        dcv = _anticausal_taps(_Rows(ya_ref), ca_w, K_A, tm)
        dproj_ref[:, d:2 * d] = (dcv * proj_ref[:, 2 * d:3 * d].astype(F32)).astype(BF)
        dproj_ref[:, 2 * d:3 * d] = (dcv * proj_ref[:, d:2 * d].astype(F32)).astype(BF)
        dub = _anticausal_taps(yb, cb_w, K_B, tm)
        sg = _sig(proj_ref[:, 4 * d:5 * d].astype(F32))
        dproj_ref[:, 3 * d:4 * d] = (dub * sg).astype(BF)
        dproj_ref[:, 4 * d:5 * d] = (dub * proj_ref[:, 3 * d:4 * d].astype(F32) * sg * (1.0 - sg)).astype(BF)

    return _run(
        body, comm, name=name, grid=(n_t,),
        in_specs=[pl.BlockSpec((tm, d9), lambda i: (i, 0)),
                  pl.BlockSpec((HALO, d), _next_halo(tm, t, 1)),
                  pl.BlockSpec((HALO, d), _next_halo(tm, t, 3)),
                  pl.BlockSpec((tm, d9), lambda i: (i, 0)),
                  _resident((None, K_A, d), (l, 0, 0)), _resident((None, K_B, d), (l, 0, 0))],
        out_specs=[pl.BlockSpec((tm, d9), lambda i: (i, 0))],
        out_shape=[jax.ShapeDtypeStruct((t, d9), BF)],
        scratch_shapes=[pltpu.VMEM((tm + HALO, d), F32), pltpu.VMEM((tm + HALO, d), F32),
                        _shifted_scratch(ALL_RESIDUES, tm, d)],
        args=(dpre, dpre, dpre, proj, conv_a, conv_b))


def _mem_kv_bwd(dkv, memn, mem, g, w_kv, name):
    m, d = mem.shape
    n_s, _, ns = w_kv.shape

    def body(dkv_ref, memn_ref, mem_ref, g_ref, w_ref, dw_ref, dg_ref):
        dkvb = dkv_ref[...].astype(BF)
        dw_ref[...] = _tn(memn_ref[...], dkvb)
        dmemn = _nt(dkvb[:, 0:ns], w_ref[0])
        for s in range(1, n_s):
            dmemn = dmemn + _nt(dkvb[:, s * ns:(s + 1) * ns], w_ref[s])
        xf = mem_ref[...]
        r = lax.rsqrt(jnp.mean(xf * xf, axis=-1, keepdims=True) + EPS)
        dg_ref[...] = jnp.sum(dmemn * (xf * r), axis=0, keepdims=True)

    return _pallas_call(
        body, name=name, grid=(1,),
        in_specs=[pl.BlockSpec((m, 2 * d), lambda i: (0, 0)), pl.BlockSpec((m, d), lambda i: (0, 0)),
                  pl.BlockSpec((m, d), lambda i: (0, 0)), pl.BlockSpec((1, d), lambda i: (0, 0)),
                  pl.BlockSpec((n_s, d, ns), lambda i: (0, 0, 0))],
        out_specs=[pl.BlockSpec((d, 2 * d), lambda i: (0, 0)), pl.BlockSpec((1, d), lambda i: (0, 0))],
        out_shape=[jax.ShapeDtypeStruct((d, 2 * d), F32), jax.ShapeDtypeStruct((1, d), F32)],
        compiler_params=_params(1),
    )(dkv, memn, mem, g, w_kv)


def _dw_matmul(a, b, tn, name, comm=None):
    t, k = a.shape
    n = b.shape[1]
    tt = min(TT_DW, t)
    while tt * k > DW_LHS_ELEMS and tt % 2 == 0:
        tt //= 2

    def body(a_ref, b_ref, o_ref):
        @pl.when(pl.program_id(1) == 0)
        def _():
            o_ref[...] = jnp.zeros_like(o_ref)

        o_ref[...] += _tn(a_ref[...], b_ref[...])

    return _run(
        body, comm, name=name, grid=(n // tn, t // tt),
        in_specs=[pl.BlockSpec((tt, k), lambda j, s: (s, 0)), pl.BlockSpec((tt, tn), lambda j, s: (s, j))],
        out_specs=[pl.BlockSpec((k, tn), lambda j, s: (0, j))],
        out_shape=[jax.ShapeDtypeStruct((k, n), F32)],
        args=(a, b))


class _GradReduce:
    def __init__(self, l, grads, core):
        self.l, self.core, self.names = l, core, tuple(grads)
        self.views = {n: _halves_view(n, g) for n, g in grads.items()}
        self.got, self.sums, self.landing, self.pieces = {}, {}, {}, {}

    def swap_program(self):
        return _swap_program([self.views[n] for n in self.names])

    def swapped(self, outs):
        self.got = dict(zip(self.names, outs))
        for n in self.names:
            self.sums[n], self.landing[n] = _add_halves(self.views[n], self.got[n], self.core, n in COL,
                                                        f"add_halves_{n}_{self.l}")

    def scatter_program(self, names=None):
        names = self.names if names is None else names
        return _scatter_program([self.sums[n] for n in names], [self.landing[n] for n in names], names)

    def scattered(self, outs, names=None):
        self.pieces.update(zip(self.names if names is None else names, outs))


def _step(x, mem, target, sh, conv_sh, small, core):
    depth = len(sh["w_in"])
    d = x.shape[1]
    f2 = sh["w_up"][0].shape[2] * N_CHIPS
    tn_dw_in = min(1024, d)
    tn_dw_up = f2 // 11 if f2 % (11 * 128) == 0 and f2 // 11 >= 128 else f2
    row = lambda v: v.reshape(1, -1)
    one = lambda a: a[None]

    w_in, cab, cf = _run_comm(_gather_program([sh["w_in"][0]] + conv_sh, ("col", "small", "small")), "gather_first")
    saved = []
    for l in range(depth):
        conv = dict(a=cab[l:l + 1, :K_A], b=cab[l:l + 1, K_A:], f=cf[l:l + 1])
        (h, proj), (wsq, w_kv) = _norm_matmul(
            x, row(small["norm_mix_g"][l]), w_in, 1, f"in_proj_{l}",
            _gather_program([sh["wsq"][l], sh["w_kv"][l]], ("row", "col")))
        memn, kv = _mem_kv(mem, row(small["norm_mem_g"][l]), w_kv, f"mem_kv_{l}")
        (x1, za, zb, o, ya, yb, yc, mg, cb), (w_up, w_down) = _mixer_fwd(
            proj, x, kv, conv["a"], conv["b"], row(small["conv_b_bias"][l]), row(small["ln_b_g"][l]),
            row(small["ln_b_b"][l]), row(small["b_gate"][l]), one(wsq), 0, f"mixer_fwd_{l}",
            _gather_program([sh["w_up"][l], sh["w_down"][l]], ("col", "row")))
        more = l + 1 < depth
        nxt = _gather_program([sh["w_in"][l + 1]], ("col",), (0, 2)) if more else None
        (h2, up), w_in_next = _norm_matmul(x1, row(small["norm_ffn_g"][l]), w_up, 2, f"up_proj_{l}", nxt)
        nxt = _gather_program(w_in_next, ("col",), (1, 2)) if more else None
        (x2, zf, uc), w_in_next = _ffn_down_fwd(up, x1, conv["f"], one(w_down), 0, f"ffn_down_fwd_{l}", nxt)
        saved.append(dict(x=x, x1=x1, memn=memn, kv=kv, h=h, proj=proj, za=za, zb=zb, o=o, ya=ya, yb=yb, yc=yc,
                          mg=mg, cb=cb, h2=h2, up=up, zf=zf, uc=uc, w_in=w_in, wsq=one(wsq), w_kv=w_kv,
                          w_up=w_up, w_down=one(w_down), conv=conv))
        x = x2
        w_in = w_in_next[0] if w_in_next else None
    dx, loss, dg_final = _final_loss(x, row(small["norm_final_g"]), target, "final_loss")
    sgrads, pieces = [None] * depth, [None] * depth
    above = None
    first, second, rest = ("w_up",), ("w_down",) + SQUARES, ("w_in", "w_kv")
    for l in reversed(range(depth)):
        s = saved[l]
        conv = s["conv"]
        bottom = l == 0
        (duc, dx2b, dconv_f), got = _ffn_down_bwd(dx, s["up"], s["uc"], s["w_down"], 0, f"ffn_down_bwd_{l}",
                                                  above and above.swap_program())
        if above:
            above.swapped(got)
        (dup,), got = _ffn_conv_bwd(duc, conv["f"], 0, f"ffn_conv_bwd_{l}", above and above.scatter_program(first))
        if above:
            above.scattered(got, first)
        (dx1, dx1b, dg_ffn), got = _nt_matmul_norm_bwd(
            dup, s["w_up"], 2, s["x1"], row(small["norm_ffn_g"][l]), dx, f"up_proj_bwd_{l}",
            above and above.scatter_program(second))
        if above:
            above.scattered(got, second)
        grads = dict(w_up=_dw_matmul(s["h2"], dup, tn_dw_up, f"dw_up_{l}")[0][0],
                     w_down=_dw_matmul(s["zf"], dx2b, d, f"dw_down_{l}")[0][0])
        ffn = _GradReduce(l, grads, core) if bottom else None
        (dpre, dya, dyb, dyc, dkv, dbg, dlng, dlnb, dcbias, dconv_a, dconv_b), got = _mixer_bwd(
            dx1b, s["proj"], s["ya"], s["yb"], s["yc"], s["cb"], s["kv"], conv["a"], conv["b"],
            row(small["ln_b_g"][l]), row(small["ln_b_b"][l]), row(small["b_gate"][l]), s["wsq"], 0,
            f"mixer_bwd_{l}", _join([above and above.scatter_program(rest), ffn and ffn.swap_program()]))
        if above:
            above.scattered(got[:len(rest)], rest)
            pieces[above.l] = above.pieces
            got = got[len(rest):]
        if ffn:
            ffn.swapped(got)
        dw_kv, dg_mem = _mem_kv_bwd(dkv, s["memn"], mem, row(small["norm_mem_g"][l]), s["w_kv"], f"mem_kv_bwd_{l}")
        mix_grads = dict(w_a_out=_dw_matmul(s["za"], dya, d, f"dw_a_out_{l}")[0][0],
                         w_b_out=_dw_matmul(s["zb"], dyb, d, f"dw_b_out_{l}")[0][0],
                         w_att_out=_dw_matmul(s["o"], dyc, d, f"dw_att_out_{l}")[0][0],
                         w_o=_dw_matmul(s["mg"], dx1b, d, f"dw_o_{l}")[0][0], w_kv=dw_kv)
        mix = _GradReduce(l, mix_grads, core) if bottom else None
        (dproj,), got = _inproj_conv_bwd(dpre, s["proj"], conv["a"], conv["b"], 0, f"inproj_conv_bwd_{l}",
                                         _join([ffn and ffn.scatter_program(), mix and mix.swap_program()]))
        if bottom:
            ffn.scattered(got[:len(ffn.names)])
            mix.swapped(got[len(ffn.names):])
        in_grads = dict(w_in=_dw_matmul(s["h"], dproj, tn_dw_in, f"dw_in_{l}")[0][0])
        inp = _GradReduce(l, in_grads, core) if bottom else None
        (dx0, _, dg_mix), got = _nt_matmul_norm_bwd(
            dproj, s["w_in"], 2, s["x"], row(small["norm_mix_g"][l]), dx1, f"in_proj_bwd_{l}",
            _join([mix and mix.scatter_program(), inp and inp.swap_program()]))
        if bottom:
            mix.scattered(got[:len(mix.names)])
            inp.swapped(got[len(mix.names):])
            inp.scattered(_run_comm(inp.scatter_program(), f"scatter_w_in_{l}"))
            pieces[l] = {**ffn.pieces, **mix.pieces, **inp.pieces}
        else:
            above = _GradReduce(l, {**grads, **mix_grads, **in_grads}, core)
        sgrads[l] = dict(norm_mix_g=dg_mix, norm_mem_g=dg_mem, b_gate=dbg, conv_b_bias=dcbias, ln_b_g=dlng,
                         ln_b_b=dlnb, norm_ffn_g=dg_ffn, conv_a_w=dconv_a, conv_b_w=dconv_b, conv_ffn_w=dconv_f)
        dx = dx0
    return loss, dx, sgrads, dg_final, pieces


BIG = ("w_in", "w_a_out", "w_b_out", "w_att_out", "w_o", "w_kv", "w_up", "w_down")
COL = ("w_in", "w_kv", "w_up")
SQUARES = ("w_a_out", "w_b_out", "w_att_out", "w_o")
ANY = pl.BlockSpec(memory_space=pl.ANY)


def _place():
    x, y, c = lax.axis_index("x"), lax.axis_index("y"), lax.axis_index("c")
    chips = [(1 - x, y), (x, 1 - y), (1 - x, 1 - y)]
    return x, y, c, 2 * x + y, chips


def _remote(src, dst, send_sem, recv_sem, dev):
    return pltpu.make_async_remote_copy(src_ref=src, dst_ref=dst, send_sem=send_sem, recv_sem=recv_sem,
                                        device_id=dev, device_id_type=MESH)


class _Striped:
    def __init__(self, src, dst, make):
        rows = src.shape[-2]
        unit = 8 * (4 // jnp.dtype(src.dtype).itemsize)
        n = max(k for k in range(1, MAX_STRIPES + 1) if rows % (unit * k) == 0) if rows % unit == 0 else 1
        q = rows // n
        self.parts = [make(_window(src, pl.ds(i * q, q), slice(None)), _window(dst, pl.ds(i * q, q), slice(None)))
                      for i in range(n)]
        self.whole = make(src, dst)

    def start(self):
        for p in self.parts:
            p.start()

    def wait(self):
        self.whole.wait()

    def wait_send(self):
        self.whole.wait_send()

    def wait_recv(self):
        self.whole.wait_recv()


def _far(src, dst, send_sem, recv_sem, dev):
    return _Striped(src, dst, lambda s, d: _remote(s, d, send_sem, recv_sem, dev))


def _near(src, dst, sem):
    return _Striped(src, dst, lambda s, d: pltpu.make_async_copy(s, d, sem))


def _window(ref, rows, cols):
    return ref.at[(slice(None),) * (len(ref.shape) - 2) + (rows, cols)]


def _row_tile(rows, cols, unit=16, limit=1 << 20):
    best = unit
    for tr in range(unit, rows + 1, unit):
        if rows % tr == 0 and tr * cols <= limit:
            best = tr
    return best


def _cast_place(ws, l, kind, chip, name):
    _, k, n = ws[0].shape
    if kind == "col":
        shape, spec = (N_CHIPS, k, n), pl.BlockSpec((None, k, n), lambda i, c: (c[0], 0, 0))
    elif len(ws) == 1:
        shape, spec = (N_CHIPS * k, n), pl.BlockSpec((k, n), lambda i, c: (c[0], 0))
    else:
        shape, spec = (len(ws), N_CHIPS * k, n), pl.BlockSpec((len(ws), k, n), lambda i, c: (0, c[0], 0))

    def body(c_ref, *refs):
        o_ref = refs[-1]
        if len(ws) == 1 or kind == "col":
            o_ref[...] = refs[0][...].astype(BF)
        else:
            for i in range(len(ws)):
                o_ref[i] = refs[i][...].astype(BF)

    return _pallas_call(
        body, name=name,
        grid_spec=pltpu.PrefetchScalarGridSpec(
            num_scalar_prefetch=1, grid=(1,),
            in_specs=[pl.BlockSpec((None, k, n), lambda i, c: (l, 0, 0))] * len(ws), out_specs=spec),
        out_shape=jax.ShapeDtypeStruct(shape, BF),
        compiler_params=_params(1),
    )(chip, *ws)


def _gather_program(arrays, kinds, part=(0, 1)):
    n_t = len(arrays)
    index, count = part

    def shard_rows(f, kind):
        return f.shape[-2] if kind == "col" else f.shape[-2] // N_CHIPS

    def run(phase, ins, full, sems):
        ici_send, ici_recv, sib_send, sib_recv, loc_sem = sems
        x, y, c, me, chips = _place()
        sibling = (x, y, 1 - c)

        def part_of(i, chip, half):
            f, kind = full[i], kinds[i]
            if kind == "small":
                cols = ins[i].shape[-1]
                return _window(f, slice(None), pl.ds(pl.multiple_of(chip * cols, 128), cols))
            rows = shard_rows(f, kind)
            r = rows // (2 * count)
            at = (half * count + index) * r
            if kind == "col":
                return f.at[chip, pl.ds(pl.multiple_of(at, 16), r), :]
            return _window(f, pl.ds(pl.multiple_of(chip * rows + at, 16), r), slice(None))

        src_part = lambda i, half: ins[i] if kinds[i] == "small" else part_of(i, me, half)
        dst_part = part_of
        local = [_near(ins[i], part_of(i, me, c), loc_sem.at[i]) for i in range(n_t) if kinds[i] == "small"]
        sends = []
        for i in range(n_t):
            for j, chip in enumerate(chips):
                sends.append(_far(src_part(i, c), dst_part(i, me, c), ici_send.at[3 * i + j],
                                  ici_recv.at[3 * i + j], (*chip, c)))
        if phase == "start":
            for cp in local + sends:
                cp.start()
            return
        passed = []
        for i in range(n_t):
            for j, chip in enumerate(chips):
                k = 2 * chip[0] + chip[1]
                landed = dst_part(i, k, c)
                if phase == "forward":
                    _remote(landed, landed, ici_send.at[3 * i + j], ici_recv.at[3 * i + j], (*chip, c)).wait_recv()
                if kinds[i] != "small":
                    passed.append(_far(landed, landed, sib_send.at[3 * i + j], sib_recv.at[3 * i + j], sibling))
        if phase == "forward":
            for cp in passed:
                cp.start()
            return
        for i in range(n_t):
            if kinds[i] == "small":
                continue
            for j, chip in enumerate(chips):
                k = 2 * chip[0] + chip[1]
                other = dst_part(i, k, 1 - c)
                _remote(other, other, sib_send.at[3 * i + j], sib_recv.at[3 * i + j], sibling).wait_recv()
        for cp in sends + passed:
            cp.wait_send()
        for cp in local:
            cp.wait()

    def out_shape(a, kind):
        shp = a.shape[:-1] + (a.shape[-1] * N_CHIPS,) if kind == "small" else a.shape
        return jax.ShapeDtypeStruct(shp, a.dtype)

    outs = [out_shape(a, k) for a, k in zip(arrays, kinds)]
    sems = [pltpu.SemaphoreType.DMA((3 * n_t,))] * 4 + [pltpu.SemaphoreType.DMA((n_t,))]
    return _Comm(arrays, outs, sems, run, {i: i for i in range(n_t) if kinds[i] != "small"})


def _all_reduce_small(part, name):
    r, n = part.shape

    def body(in_ref, out_ref, gather_ref, send_sems, recv_sems):
        x, y, c, _, _ = _place()
        me = 4 * x + 2 * y + c
        gather_ref[me] = in_ref[...]
        sends = []
        for k in range(1, N_DEV):
            peer = (me + k) % N_DEV
            sends.append(_remote(in_ref, gather_ref.at[me], send_sems.at[k - 1], recv_sems.at[k - 1],
                                 (peer // 4, (peer // 2) % 2, peer % 2)))
        for cp in sends:
            cp.start()
        for k in range(1, N_DEV):
            origin = (me + N_DEV - k) % N_DEV
            _remote(in_ref, gather_ref.at[origin], send_sems.at[k - 1], recv_sems.at[k - 1],
                    (x, y, c)).wait_recv()
        for cp in sends:
            cp.wait_send()
        total = gather_ref[0]
        for dev in range(1, N_DEV):
            total = total + gather_ref[dev]
        out_ref[...] = total

    vm = pl.BlockSpec(memory_space=pltpu.VMEM)
    return _pallas_call(
        body, name=name, in_specs=[vm], out_specs=vm, out_shape=jax.ShapeDtypeStruct((r, n), F32),
        scratch_shapes=[pltpu.VMEM((N_DEV, r, n), F32), pltpu.SemaphoreType.DMA((N_DEV - 1,)),
                        pltpu.SemaphoreType.DMA((N_DEV - 1,))],
        compiler_params=pltpu.CompilerParams(vmem_limit_bytes=VMEM_LIMIT),
    )(part)


def _halves_view(name, dw):
    k, n = dw.shape
    s = 1 if name in COL else N_CHIPS
    return dw.reshape(s, 2, k // (2 * s), n)


def _swap_program(views):
    n_t = len(views)

    def run(phase, src, dst, sems):
        send_sems, recv_sems = sems
        x, y, c, _, _ = _place()
        copies = [_far(src[i].at[:, 1 - c], dst[i], send_sems.at[i], recv_sems.at[i], (x, y, 1 - c))
                  for i in range(n_t)]
        for cp in copies:
            if phase == "start":
                cp.start()
            elif phase == "finish":
                cp.wait()

    outs = [jax.ShapeDtypeStruct((v.shape[0],) + v.shape[2:], F32) for v in views]
    sems = [pltpu.SemaphoreType.DMA((n_t,)), pltpu.SemaphoreType.DMA((n_t,))]
    return _Comm(views, outs, sems, run)


def _add_halves(view, got, place, col, name):
    s, _, r, n = view.shape
    if col:
        cw = n // N_CHIPS
        tr = _row_tile(r, cw)
        grid = (r // tr, N_CHIPS)
        in_specs = [pl.BlockSpec((None, None, tr, cw), lambda j, q, p: (0, p[0], j, q)),
                    pl.BlockSpec((None, tr, cw), lambda j, q, p: (0, j, q))]
        out_specs = [pl.BlockSpec((None, tr, cw), lambda j, q, p: (0, j, q)),
                     pl.BlockSpec((None, None, tr, cw), lambda j, q, p: (p[0], p[1], j, 0))]
    else:
        cw = n
        tr = _row_tile(r, n)
        grid = (s, r // tr)
        in_specs = [pl.BlockSpec((None, None, tr, n), lambda i, j, p: (i, p[0], j, 0)),
                    pl.BlockSpec((None, tr, n), lambda i, j, p: (i, j, 0))]
        out_specs = [pl.BlockSpec((None, tr, n), lambda i, j, p: (i, j, 0)),
                     pl.BlockSpec((None, None, tr, n), lambda i, j, p: (p[0], i, j, 0))]

    def body(p_ref, a_ref, b_ref, o_ref, z_ref):
        total = (a_ref[...] + b_ref[...]).astype(BF)
        o_ref[...] = total
        if col:
            @pl.when(pl.program_id(1) == p_ref[1])
            def _():
                z_ref[...] = total
        else:
            z_ref[...] = total

    return _pallas_call(
        body, name=name,
        grid_spec=pltpu.PrefetchScalarGridSpec(num_scalar_prefetch=1, grid=grid, in_specs=in_specs,
                                               out_specs=out_specs),
        out_shape=[jax.ShapeDtypeStruct((s, r, n), BF), jax.ShapeDtypeStruct((2, N_CHIPS, r, cw), BF)],
        compiler_params=_params(2),
    )(place, view, got)


def _scatter_program(sums, landing, names):
    n_t = len(sums)

    def run(phase, src, dst, sems):
        ici_send, ici_recv, sib_send, sib_recv = sems
        x, y, c, me, chips = _place()
        sibling = (x, y, 1 - c)

        def piece(i, chip):
            if names[i] in COL:
                cw = src[i].shape[2] // N_CHIPS
                return src[i].at[0, :, pl.ds(pl.multiple_of(chip * cw, 128), cw)]
            return src[i].at[chip]

        local = []
        sends = []
        for i in range(n_t):
            sends.append(_far(piece(i, me), dst[i].at[c, me], sib_send.at[4 * i + 3], sib_recv.at[4 * i + 3],
                              sibling))
            for j, chip in enumerate(chips):
                k = 2 * chip[0] + chip[1]
                sends.append(_far(piece(i, k), dst[i].at[c, me], ici_send.at[3 * i + j], ici_recv.at[3 * i + j],
                                  (*chip, c)))
        if phase == "start":
            for cp in local + sends:
                cp.start()
            return
        passed = []
        for i in range(n_t):
            for j, chip in enumerate(chips):
                k = 2 * chip[0] + chip[1]
                landed = dst[i].at[c, k]
                if phase == "forward":
                    _remote(landed, landed, ici_send.at[3 * i + j], ici_recv.at[3 * i + j], (*chip, c)).wait_recv()
                passed.append(_far(landed, landed, sib_send.at[4 * i + j], sib_recv.at[4 * i + j], sibling))
        if phase == "forward":
            for cp in passed:
                cp.start()
            return
        for i in range(n_t):
            other = dst[i].at[1 - c, me]
            _remote(other, other, sib_send.at[4 * i + 3], sib_recv.at[4 * i + 3], sibling).wait_recv()
            for j, chip in enumerate(chips):
                k = 2 * chip[0] + chip[1]
                other = dst[i].at[1 - c, k]
                _remote(other, other, sib_send.at[4 * i + j], sib_recv.at[4 * i + j], sibling).wait_recv()
        for cp in sends + passed:
            cp.wait_send()
        for cp in local:
            cp.wait()

    outs = [jax.ShapeDtypeStruct(z.shape, z.dtype) for z in landing]
    sems = [pltpu.SemaphoreType.DMA((3 * n_t,))] * 2 + [pltpu.SemaphoreType.DMA((4 * n_t,))] * 2
    return _Comm(list(sums) + list(landing), outs, sems, run, {n_t + i: i for i in range(n_t)})


def _adamw(w, g, m, v):
    m = ADAM_B1 * m + (1.0 - ADAM_B1) * g
    v = ADAM_B2 * v + (1.0 - ADAM_B2) * (g * g)
    m_hat = m / (1.0 - ADAM_B1 ** ADAM_STEP)
    v_hat = v / (1.0 - ADAM_B2 ** ADAM_STEP)
    return -ADAM_LR * (m_hat / (jnp.sqrt(v_hat) + ADAM_EPS) + ADAM_WD * w), m, v


def _adam_shard(pieces, w, m, v, l, prev, name):
    depth, rows, cw = w.shape
    hr = rows // 2
    tr = _row_tile(hr, cw, limit=1 << 18)
    n_i = hr // tr

    def body(*refs):
        z_ref, w_ref, m_ref, v_ref = refs[:4]
        g_ref, d_ref, nm_ref, nv_ref = refs[-4:]
        g = z_ref[0].astype(F32)
        for k in range(1, N_CHIPS):
            g = g + z_ref[k].astype(F32)
        g_ref[...] = g
        d_ref[...], nm_ref[...], nv_ref[...] = _adamw(w_ref[...], g, m_ref[...], v_ref[...])

    par = pl.BlockSpec((None, tr, cw), lambda h, i: (l, h * n_i + i, 0))
    out = jax.ShapeDtypeStruct((depth, rows, cw), F32)
    extra = [] if prev is None else list(prev)
    return _pallas_call(
        body, name=name, grid=(2, n_i),
        in_specs=[pl.BlockSpec((None, N_CHIPS, tr, cw), lambda h, i: (h, 0, i, 0)), par, par, par] + [ANY] * len(extra),
        out_specs=[par] * 4, out_shape=[out] * 4,
        input_output_aliases={4 + k: k for k in range(len(extra))},
        compiler_params=_params(2),
    )(pieces, w, m, v, *extra)


def _adam_small(g, w, m, v, name):
    def body(g_ref, w_ref, m_ref, v_ref, d_ref, nm_ref, nv_ref):
        d_ref[...], nm_ref[...], nv_ref[...] = _adamw(w_ref[...], g_ref[...], m_ref[...], v_ref[...])

    vm = pl.BlockSpec(memory_space=pltpu.VMEM)
    out = jax.ShapeDtypeStruct(g.shape, F32)
    return _pallas_call(body, name=name, in_specs=[vm] * 4, out_specs=[vm] * 3, out_shape=[out] * 3)(g, w, m, v)


WEIGHTS = ("norm_mix_g", "norm_mem_g", "w_in", "b_gate", "conv_a_w", "w_a_out", "conv_b_w", "conv_b_bias", "ln_b_g",
           "ln_b_b", "w_b_out", "w_kv", "w_att_out", "w_o", "norm_ffn_g", "w_up", "conv_ffn_w", "w_down",
           "norm_final_g")
REPLICATED = ("norm_mix_g", "norm_mem_g", "b_gate", "conv_b_bias", "ln_b_g", "ln_b_b", "norm_ffn_g")
CONVS = ("conv_a_w", "conv_b_w", "conv_ffn_w")
PACK_WIDTH = 1024


def _pack(arrays):
    flat = jnp.concatenate([a.reshape(-1) for a in arrays])
    size = -(-flat.shape[0] // (8 * PACK_WIDTH)) * (8 * PACK_WIDTH)
    return jnp.pad(flat, (0, size - flat.shape[0])).reshape(-1, PACK_WIDTH)


def _unpack(packed, shapes):
    flat = packed.reshape(-1)
    out, at = [], 0
    for shp in shapes:
        n = math.prod(shp)
        out.append(flat[at:at + n].reshape(shp))
        at += n
    return out


def kernel(x, mem, norm_mix_g, norm_mem_g, w_in, b_gate, conv_a_w, w_a_out, conv_b_w, conv_b_bias, ln_b_g, ln_b_b, w_b_out, w_kv, w_att_out, w_o, norm_ffn_g, w_up, conv_ffn_w, w_down, norm_final_g, loss_target, m_norm_mix_g, m_norm_mem_g, m_w_in, m_b_gate, m_conv_a_w, m_w_a_out, m_conv_b_w, m_conv_b_bias, m_ln_b_g, m_ln_b_b, m_w_b_out, m_w_kv, m_w_att_out, m_w_o, m_norm_ffn_g, m_w_up, m_conv_ffn_w, m_w_down, m_norm_final_g, v_norm_mix_g, v_norm_mem_g, v_w_in, v_b_gate, v_conv_a_w, v_w_a_out, v_conv_b_w, v_conv_b_bias, v_ln_b_g, v_ln_b_b, v_w_b_out, v_w_kv, v_w_att_out, v_w_o, v_norm_ffn_g, v_w_up, v_conv_ffn_w, v_w_down, v_norm_final_g):
    w = dict(norm_mix_g=norm_mix_g, norm_mem_g=norm_mem_g, w_in=w_in, b_gate=b_gate, conv_a_w=conv_a_w,
             w_a_out=w_a_out, conv_b_w=conv_b_w, conv_b_bias=conv_b_bias, ln_b_g=ln_b_g, ln_b_b=ln_b_b,
             w_b_out=w_b_out, w_kv=w_kv, w_att_out=w_att_out, w_o=w_o, norm_ffn_g=norm_ffn_g, w_up=w_up,
             conv_ffn_w=conv_ffn_w, w_down=w_down, norm_final_g=norm_final_g)
    mom = dict(norm_mix_g=m_norm_mix_g, norm_mem_g=m_norm_mem_g, w_in=m_w_in, b_gate=m_b_gate, conv_a_w=m_conv_a_w,
               w_a_out=m_w_a_out, conv_b_w=m_conv_b_w, conv_b_bias=m_conv_b_bias, ln_b_g=m_ln_b_g, ln_b_b=m_ln_b_b,
               w_b_out=m_w_b_out, w_kv=m_w_kv, w_att_out=m_w_att_out, w_o=m_w_o, norm_ffn_g=m_norm_ffn_g,
               w_up=m_w_up, conv_ffn_w=m_conv_ffn_w, w_down=m_w_down, norm_final_g=m_norm_final_g)
    var = dict(norm_mix_g=v_norm_mix_g, norm_mem_g=v_norm_mem_g, w_in=v_w_in, b_gate=v_b_gate, conv_a_w=v_conv_a_w,
               w_a_out=v_w_a_out, conv_b_w=v_conv_b_w, conv_b_bias=v_conv_b_bias, ln_b_g=v_ln_b_g, ln_b_b=v_ln_b_b,
               w_b_out=v_w_b_out, w_kv=v_w_kv, w_att_out=v_w_att_out, w_o=v_w_o, norm_ffn_g=v_norm_ffn_g,
               w_up=v_w_up, conv_ffn_w=v_conv_ffn_w, w_down=v_w_down, norm_final_g=v_norm_final_g)
    depth = w_in.shape[0]
    chip = 2 * lax.axis_index("x") + lax.axis_index("y")
    core = jnp.stack([lax.axis_index("c"), chip]).astype(jnp.int32)

    chip1 = chip.astype(jnp.int32).reshape(1)
    layers = range(depth)
    sh = dict(w_in=[_cast_place([w_in], l, "col", chip1, f"cast_w_in_{l}") for l in layers],
              wsq=[_cast_place([w[n] for n in SQUARES], l, "row", chip1, f"cast_squares_{l}") for l in layers],
              w_kv=[_cast_place([w_kv], l, "col", chip1, f"cast_w_kv_{l}") for l in layers],
              w_up=[_cast_place([w_up], l, "col", chip1, f"cast_w_up_{l}") for l in layers],
              w_down=[_cast_place([w_down], l, "row", chip1, f"cast_w_down_{l}") for l in layers])
    conv_sh = [jnp.concatenate([conv_a_w, conv_b_w], axis=1), conv_ffn_w]
    small = {n: w[n] for n in REPLICATED + ("norm_final_g",)}

    loss, dx, sgrads, dg_final, pieces = _step(x[0], mem[0], loss_target[0], sh, conv_sh, small, core)

    res = {n: None for n in BIG}
    for l in reversed(range(depth)):
        for n in BIG:
            res[n] = _adam_shard(pieces[l][n], w[n], mom[n], var[n], l, res[n], f"adam_{n}_{l}")

    per_layer = REPLICATED + CONVS
    parts = [sgrads[l][n] for l in range(depth) for n in per_layer] + [dg_final]
    total = _all_reduce_small(_pack(parts), "all_reduce_small")
    shapes = [sgrads[l][n].shape for l in range(depth) for n in per_layer] + [dg_final.shape]
    summed = _unpack(total, shapes)
    g_small = {}
    for k, n in enumerate(per_layer):
        full = jnp.stack([summed[l * len(per_layer) + k] for l in range(depth)])
        if n in CONVS:
            cols = w[n].shape[-1]
            full = lax.dynamic_slice_in_dim(full, chip * cols, cols, axis=2)
        g_small[n] = full.reshape(w[n].shape)
    g_small["norm_final_g"] = summed[-1].reshape(norm_final_g.shape)
    names = per_layer + ("norm_final_g",)
    d_p, m_p, v_p = _adam_small(_pack([g_small[n] for n in names]), _pack([w[n] for n in names]),
                                _pack([mom[n] for n in names]), _pack([var[n] for n in names]), "adam_small")
    shp = [w[n].shape for n in names]
    for n, dl, nm, nv in zip(names, _unpack(d_p, shp), _unpack(m_p, shp), _unpack(v_p, shp)):
        res[n] = (g_small[n], dl, nm, nv)

    loss = lax.psum(loss[0, 0], ("x", "y", "c"))
    return (loss, dx.reshape(x.shape), *[res[n][0] for n in WEIGHTS], *[res[n][1] for n in WEIGHTS],
            *[res[n][2] for n in WEIGHTS], *[res[n][3] for n in WEIGHTS])
```

```python
import functools
import math

import jax
import jax.numpy as jnp
from jax import lax
from jax.experimental import pallas as pl
from jax.experimental.pallas import tpu as pltpu

F32 = jnp.float32
BF = jnp.bfloat16
EPS = 1e-6
N_HEADS = 4
K_A, K_B, K_F = 3, 31, 3
ADAM_LR, ADAM_B1, ADAM_B2, ADAM_EPS, ADAM_WD, ADAM_STEP = 0.001, 0.9, 0.999, 1e-08, 0.01, 10
N_CHIPS = 4
N_DEV = 8
HALO = 32
MAX_STRIPES = 8
TM_ROW = 256
TM_MM = 1024
TT_DW = 4096
DW_LHS_ELEMS = 4 * 1024 * 1024
TR_EW = 128
VMEM_LIMIT = 56 * 1024 * 1024
MESH = pl.DeviceIdType.MESH
_pallas_call = pl.pallas_call


def _params(n_axes):
    return pltpu.CompilerParams(dimension_semantics=("arbitrary",) * n_axes, vmem_limit_bytes=VMEM_LIMIT)


def _resident(shape, index):
    return pl.BlockSpec(shape, lambda *_: index, pipeline_mode=pl.Buffered(1))


def _sig(x):
    return 1.0 / (1.0 + jnp.exp(-x))


def _nt(a, b):
    return lax.dot_general(a, b, (((1,), (1,)), ((), ())), preferred_element_type=F32)


def _tn(a, b):
    return lax.dot_general(a, b, (((0,), (0,)), ((), ())), preferred_element_type=F32)


def _nn(a, b):
    return jnp.dot(a, b, preferred_element_type=F32)


class _Comm:
    def __init__(self, inputs, out_shape, scratch, run, aliases=None):
        self.inputs, self.out_shape, self.scratch, self.run = list(inputs), list(out_shape), list(scratch), run
        self.aliases = dict(aliases or {})


def _join(programs):
    programs = [p for p in programs if p is not None]
    if not programs:
        return None

    def split(seq, counts):
        parts, at = [], 0
        for n in counts:
            parts.append(seq[at:at + n])
            at += n
        return parts

    n_in = [len(p.inputs) for p in programs]
    n_out = [len(p.out_shape) for p in programs]
    n_s = [len(p.scratch) for p in programs]

    def run(phase, ins, outs, sems):
        for p, i, o, s in zip(programs, split(ins, n_in), split(outs, n_out), split(sems, n_s)):
            p.run(phase, i, o, s)

    aliases, in_at, out_at = {}, 0, 0
    for p, i, o in zip(programs, n_in, n_out):
        aliases.update({in_at + a: out_at + b for a, b in p.aliases.items()})
        in_at, out_at = in_at + i, out_at + o
    return _Comm([a for p in programs for a in p.inputs], [a for p in programs for a in p.out_shape],
                 [a for p in programs for a in p.scratch], run, aliases)


def _run(body, comm, *, name, grid, in_specs, out_specs, out_shape, args, scratch_shapes=()):
    n_axes = len(grid)
    if comm is None:
        outs = _pallas_call(body, name=name, grid=grid, in_specs=list(in_specs), out_specs=list(out_specs),
                            out_shape=list(out_shape), scratch_shapes=list(scratch_shapes),
                            compiler_params=_params(n_axes))(*args)
        return list(outs), []
    counts = (len(in_specs), len(comm.inputs), len(out_specs), len(comm.out_shape), len(scratch_shapes),
              len(comm.scratch))

    def hosted(*refs):
        parts, at = [], 0
        for n in counts:
            parts.append(refs[at:at + n])
            at += n
        ins, c_ins, outs, c_outs, scr, c_sems = parts
        step = pl.program_id(0)
        for a in range(1, n_axes):
            step = step * grid[a] + pl.program_id(a)
        total = math.prod(grid)
        late = max(0, total - 1 - max(1, total // 4))

        @pl.when(step == 0)
        def _():
            comm.run("start", c_ins, c_outs, c_sems)

        body(*ins, *outs, *scr)

        @pl.when(step == late)
        def _():
            comm.run("forward", c_ins, c_outs, c_sems)

        @pl.when(step == total - 1)
        def _():
            comm.run("finish", c_ins, c_outs, c_sems)

    any_spec = pl.BlockSpec(memory_space=pl.ANY)
    res = _pallas_call(
        hosted, name=name, grid=grid, in_specs=list(in_specs) + [any_spec] * counts[1],
        out_specs=list(out_specs) + [any_spec] * counts[3], out_shape=list(out_shape) + comm.out_shape,
        scratch_shapes=list(scratch_shapes) + comm.scratch, compiler_params=_params(n_axes),
        input_output_aliases={counts[0] + a: counts[2] + b for a, b in comm.aliases.items()},
    )(*args, *comm.inputs)
    return list(res[:counts[2]]), list(res[counts[2]:])


def _run_comm(comm, name):
    n_in, n_out = len(comm.inputs), len(comm.out_shape)

    def body(*refs):
        ins, outs, sems = refs[:n_in], refs[n_in:n_in + n_out], refs[n_in + n_out:]
        for phase in ("start", "forward", "finish"):
            comm.run(phase, ins, outs, sems)

    any_spec = pl.BlockSpec(memory_space=pl.ANY)
    return list(_pallas_call(body, name=name, in_specs=[any_spec] * n_in, out_specs=[any_spec] * n_out,
                             out_shape=comm.out_shape, scratch_shapes=comm.scratch,
                             input_output_aliases=comm.aliases)(*comm.inputs))


SUBLANES = 8
LANES = 128
ROW_CHUNK = 128
ALL_RESIDUES = tuple(range(1, SUBLANES))
SHORT_RESIDUES = tuple(sorted({(HALO - K_A + 1 + k) % SUBLANES for k in range(K_A)} - {0}))


class _Rows:
    def __init__(self, ref, shifted_ref=None, residues=()):
        self.ref, self.shifted_ref, self.residues = ref, shifted_ref, tuple(residues)

    def shift(self):
        n = self.shifted_ref.shape[1]
        for j, b in enumerate(self.residues):
            self.shifted_ref[j] = self.ref[pl.ds(b, n), :]

    def at(self, offset, r0, c0, rows):
        b = offset % SUBLANES
        if b in self.residues:
            return self.shifted_ref[self.residues.index(b), pl.ds(offset - b + r0, rows), pl.ds(c0, LANES)]
        return self.ref[pl.ds(offset + r0, rows), pl.ds(c0, LANES)]


def _shifted_scratch(residues, tm, c):
    return pltpu.VMEM((len(residues), tm + HALO - SUBLANES, c), F32)


def _causal_offsets(k_taps):
    return [HALO - k_taps + 1 + k for k in range(k_taps)]


def _anticausal_offsets(k_taps):
    return [k_taps - 1 - k for k in range(k_taps)]


def _tap_chunk(src, w_ref, offsets, r0, c0, rows):
    acc = w_ref[0:1, pl.ds(c0, LANES)] * src.at(offsets[0], r0, c0, rows)
    for k in range(1, len(offsets)):
        acc = acc + w_ref[k:k + 1, pl.ds(c0, LANES)] * src.at(offsets[k], r0, c0, rows)
    return acc


def _conv_taps(src, w_ref, offsets, tm, emit):
    rows = min(ROW_CHUNK, tm)
    for c0 in range(0, w_ref.shape[1], LANES):
        for r0 in range(0, tm, rows):
            emit(r0, rows, c0, _tap_chunk(src, w_ref, offsets, r0, c0, rows))


def _tap_grads(dw_ref, dy_at, src, offsets, tm):
    rows = min(ROW_CHUNK // 2, tm)
    for c0 in range(0, dw_ref.shape[1], LANES):
        acc = [None] * len(offsets)
        for r0 in range(0, tm, rows):
            dy = dy_at(r0, rows, c0)
            for k, off in enumerate(offsets):
                part = (dy * src.at(off, r0, c0, rows)).reshape(rows // SUBLANES, SUBLANES, LANES).sum(axis=0)
                acc[k] = part if acc[k] is None else acc[k] + part
        for k in range(len(offsets)):
            dw_ref[k:k + 1, pl.ds(c0, LANES)] += jnp.sum(acc[k], axis=0, keepdims=True)


def _prev_halo(tm, col):
    return lambda i: (jnp.maximum(i * (tm // HALO) - 1, 0), col)


def _next_halo(tm, n_rows, col):
    return lambda i: (jnp.minimum((i + 1) * (tm // HALO), n_rows // HALO - 1), col)


def _norm_matmul(x, g, w, gs, name, comm=None):
    t, d = x.shape
    n_s, _, ns = w.shape
    n = n_s * ns
    tm = min(TM_MM, t)

    def body(x_ref, g_ref, w_ref, h_ref, y_ref):
        @pl.when(pl.program_id(1) == 0)
        def _():
            xf = x_ref[...]
            r = lax.rsqrt(jnp.mean(xf * xf, axis=-1, keepdims=True) + EPS)
            h_ref[...] = ((xf * r) * g_ref[...]).astype(BF)

        for s in range(gs):
            y_ref[:, s * ns:(s + 1) * ns] = _nn(h_ref[...], w_ref[s]).astype(BF)

    return _run(
        body, comm, name=name, grid=(t // tm, n_s // gs),
        in_specs=[pl.BlockSpec((tm, d), lambda i, j: (i, 0)),
                  pl.BlockSpec((1, d), lambda i, j: (0, 0)),
                  pl.BlockSpec((gs, d, ns), lambda i, j: (j, 0, 0))],
        out_specs=[pl.BlockSpec((tm, d), lambda i, j: (i, 0)),
                   pl.BlockSpec((tm, gs * ns), lambda i, j: (i, j))],
        out_shape=[jax.ShapeDtypeStruct((t, d), BF), jax.ShapeDtypeStruct((t, n), BF)],
        args=(x, g, w))


def _mem_kv(mem, g, w_kv, name):
    m, d = mem.shape
    n_s, _, ns = w_kv.shape

    def body(mem_ref, g_ref, w_ref, memn_ref, kv_ref):
        xf = mem_ref[...]
        r = lax.rsqrt(jnp.mean(xf * xf, axis=-1, keepdims=True) + EPS)
        memn = ((xf * r) * g_ref[...]).astype(BF)
        memn_ref[...] = memn
        for s in range(n_s):
            kv_ref[:, s * ns:(s + 1) * ns] = _nn(memn, w_ref[s]).astype(BF)

    return _pallas_call(
        body, name=name, grid=(1,),
        in_specs=[pl.BlockSpec((m, d), lambda i: (0, 0)),
                  pl.BlockSpec((1, d), lambda i: (0, 0)),
                  pl.BlockSpec((n_s, d, ns), lambda i: (0, 0, 0))],
        out_specs=[pl.BlockSpec((m, d), lambda i: (0, 0)),
                   pl.BlockSpec((m, 2 * d), lambda i: (0, 0))],
        out_shape=[jax.ShapeDtypeStruct((m, d), BF), jax.ShapeDtypeStruct((m, 2 * d), BF)],
        compiler_params=_params(1),
    )(mem, g, w_kv)


def _load_branch_inputs(i, proj_ref, gch_ref, vh_ref, u0h_ref, ugh_ref, xa_ref, xb_ref, d):
    gc = proj_ref[:, d:2 * d].astype(F32)
    v = proj_ref[:, 2 * d:3 * d].astype(F32)
    u0 = proj_ref[:, 3 * d:4 * d].astype(F32)
    ug = proj_ref[:, 4 * d:5 * d].astype(F32)
    keep = (i > 0).astype(F32)
    xa_ref[pl.ds(0, HALO), :] = gch_ref[...].astype(F32) * vh_ref[...].astype(F32) * keep
    xa_ref[pl.ds(HALO, gc.shape[0]), :] = gc * v
    xb_ref[pl.ds(0, HALO), :] = u0h_ref[...].astype(F32) * _sig(ugh_ref[...].astype(F32)) * keep
    xb_ref[pl.ds(HALO, gc.shape[0]), :] = u0 * _sig(ug)


def _softmax_rows(s):
    e = jnp.exp(s - jnp.max(s, axis=-1, keepdims=True))
    return e / jnp.sum(e, axis=-1, keepdims=True)


def _mixer_fwd(proj, x, kv, conv_a, conv_b, cbias, ln_g, ln_b, b_gate, wsq, l, name, comm=None):
    t, d = x.shape
    m = kv.shape[0]
    tm = min(TM_ROW, t)
    hd = d // N_HEADS
    scale = 1.0 / math.sqrt(hd)

    def body(proj_ref, gch_ref, vh_ref, u0h_ref, ugh_ref, x_ref, kv_ref, ca_w, cb_w, cbias_ref, lng_ref, lnb_ref,
             bg_ref, wa_ref, wb_ref, wc_ref, wo_ref,
             x1_ref, za_ref, zb_ref, o_ref, ya_ref, yb_ref, yc_ref, mg_ref, cb_ref, xa_ref, xb_ref, sb_ref):
        i = pl.program_id(0)
        _load_branch_inputs(i, proj_ref, gch_ref, vh_ref, u0h_ref, ugh_ref, xa_ref, xb_ref, d)
        xb = _Rows(xb_ref, sb_ref, ALL_RESIDUES)
        xb.shift()
        def put_za(r0, rows, c0, ca):
            gb = proj_ref[pl.ds(r0, rows), pl.ds(c0, LANES)].astype(F32)
            za_ref[pl.ds(r0, rows), pl.ds(c0, LANES)] = (gb * ca).astype(BF)

        _conv_taps(_Rows(xa_ref), ca_w, _causal_offsets(K_A), tm, put_za)
        ya = _nn(za_ref[...], wa_ref[...])
        ya_ref[...] = ya.astype(BF)

        def put_cb(r0, rows, c0, conv):
            cb_ref[pl.ds(r0, rows), pl.ds(c0, LANES)] = conv + cbias_ref[:, pl.ds(c0, LANES)]

        _conv_taps(xb, cb_w, _causal_offsets(K_B), tm, put_cb)
        cb = cb_ref[...]
        mu = jnp.mean(cb, axis=-1, keepdims=True)
        dlt = cb - mu
        rstd = lax.rsqrt(jnp.mean(dlt * dlt, axis=-1, keepdims=True) + EPS)
        lnb = (dlt * rstd) * lng_ref[...] + lnb_ref[...]
        zb = (lnb * _sig(lnb)).astype(BF)
        zb_ref[...] = zb
        yb = _nn(zb, wb_ref[...])
        yb_ref[...] = yb.astype(BF)
        for h in range(N_HEADS):
            qh = proj_ref[:, 5 * d + h * hd:5 * d + (h + 1) * hd]
            kh = kv_ref[:, h * hd:(h + 1) * hd]
            vh = kv_ref[:, d + h * hd:d + (h + 1) * hd]
            p = _softmax_rows(_nt(qh, kh) * scale)
            o_ref[:, h * hd:(h + 1) * hd] = _nn(p.astype(BF), vh).astype(BF)
        yc = _nn(o_ref[...], wc_ref[...])
        yc_ref[...] = yc.astype(BF)
        g0 = _sig(proj_ref[:, 6 * d:7 * d].astype(F32) + bg_ref[:, 0:d])
        g1 = _sig(proj_ref[:, 7 * d:8 * d].astype(F32) + bg_ref[:, d:2 * d])
        g2 = _sig(proj_ref[:, 8 * d:9 * d].astype(F32) + bg_ref[:, 2 * d:3 * d])
        mg = (g0 * ya + g1 * yb + g2 * yc).astype(BF)
        mg_ref[...] = mg
        x1_ref[...] = x_ref[...] + _nn(mg, wo_ref[...])

    row = lambda w_: pl.BlockSpec((tm, w_), lambda i: (i, 0))
    halo = lambda col: pl.BlockSpec((HALO, d), _prev_halo(tm, col))
    sq = lambda which: _resident((None, None, d, d), (l, which, 0, 0))
    act = jax.ShapeDtypeStruct((t, d), BF)
    return _run(
        body, comm, name=name, grid=(t // tm,),
        in_specs=[row(9 * d), halo(1), halo(2), halo(3), halo(4), row(d),
                  _resident((m, 2 * d), (0, 0)),
                  _resident((None, K_A, d), (l, 0, 0)), _resident((None, K_B, d), (l, 0, 0)),
                  _resident((1, d), (0, 0)), _resident((1, d), (0, 0)), _resident((1, d), (0, 0)),
                  _resident((1, 3 * d), (0, 0)), sq(0), sq(1), sq(2), sq(3)],
        out_specs=[row(d)] * 9,
        out_shape=[jax.ShapeDtypeStruct((t, d), F32)] + [act] * 7 + [jax.ShapeDtypeStruct((t, d), F32)],
        scratch_shapes=[pltpu.VMEM((HALO + tm, d), F32), pltpu.VMEM((HALO + tm, d), F32),
                        _shifted_scratch(ALL_RESIDUES, tm, d)],
        args=(proj, proj, proj, proj, proj, x, kv, conv_a, conv_b, cbias, ln_g, ln_b, b_gate, wsq, wsq, wsq, wsq))


def _ffn_down_fwd(up, x1, conv_f, w_down, l, name, comm=None):
    t, d = x1.shape
    f2 = up.shape[1]
    f = f2 // 2
    tm = min(TM_ROW, t)

    def body(up_ref, uph_ref, x1_ref, cw_ref, wd_ref, x2_ref, zf_ref, uc_ref, xx_ref):
        i = pl.program_id(0)
        xx_ref[pl.ds(0, HALO), :] = uph_ref[...].astype(F32) * (i > 0).astype(F32)
        xx_ref[pl.ds(HALO, tm), :] = up_ref[...].astype(F32)
        xx, offsets, rows = _Rows(xx_ref), _causal_offsets(K_F), min(ROW_CHUNK, tm)
        for c0 in range(0, f, LANES):
            for r0 in range(0, tm, rows):
                gt = _tap_chunk(xx, cw_ref, offsets, r0, c0, rows)
                upp = _tap_chunk(xx, cw_ref, offsets, r0, f + c0, rows)
                uc_ref[pl.ds(r0, rows), pl.ds(c0, LANES)] = gt.astype(BF)
                uc_ref[pl.ds(r0, rows), pl.ds(f + c0, LANES)] = upp.astype(BF)
                zf_ref[pl.ds(r0, rows), pl.ds(c0, LANES)] = (gt * _sig(gt) * upp).astype(BF)
        x2_ref[...] = x1_ref[...] + _nn(zf_ref[...], wd_ref[...])

    return _run(
        body, comm, name=name, grid=(t // tm,),
        in_specs=[pl.BlockSpec((tm, f2), lambda i: (i, 0)),
                  pl.BlockSpec((HALO, f2), _prev_halo(tm, 0)),
                  pl.BlockSpec((tm, d), lambda i: (i, 0)),
                  _resident((None, K_F, f2), (l, 0, 0)),
                  _resident((None, f, d), (l, 0, 0))],
        out_specs=[pl.BlockSpec((tm, d), lambda i: (i, 0)), pl.BlockSpec((tm, f), lambda i: (i, 0)),
                   pl.BlockSpec((tm, f2), lambda i: (i, 0))],
        out_shape=[jax.ShapeDtypeStruct((t, d), F32), jax.ShapeDtypeStruct((t, f), BF),
                   jax.ShapeDtypeStruct((t, f2), BF)],
        scratch_shapes=[pltpu.VMEM((HALO + tm, f2), F32)],
        args=(up, up, x1, conv_f, w_down))


def _final_loss(x, g, target, name):
    t, d = x.shape
    tm = min(2 * TM_ROW, t)

    def body(x_ref, g_ref, t_ref, dx_ref, loss_ref, dg_ref):
        @pl.when(pl.program_id(0) == 0)
        def _():
            loss_ref[...] = jnp.zeros_like(loss_ref)
            dg_ref[...] = jnp.zeros_like(dg_ref)

        xf = x_ref[...]
        r = lax.rsqrt(jnp.mean(xf * xf, axis=-1, keepdims=True) + EPS)
        xhat = xf * r
        err = xhat * g_ref[...] - t_ref[...]
        loss_ref[...] += (0.5 / d) * jnp.sum(err * err)
        dy = err * (1.0 / d)
        dg_ref[...] += jnp.sum(dy * xhat, axis=0, keepdims=True)
        dxh = dy * g_ref[...]
        dx_ref[...] = r * (dxh - xhat * jnp.mean(dxh * xhat, axis=-1, keepdims=True))

    return _pallas_call(
        body, name=name, grid=(t // tm,),
        in_specs=[pl.BlockSpec((tm, d), lambda i: (i, 0)), pl.BlockSpec((1, d), lambda i: (0, 0)),
                  pl.BlockSpec((tm, d), lambda i: (i, 0))],
        out_specs=[pl.BlockSpec((tm, d), lambda i: (i, 0)), pl.BlockSpec((8, 128), lambda i: (0, 0)),
                   pl.BlockSpec((1, d), lambda i: (0, 0))],
        out_shape=[jax.ShapeDtypeStruct((t, d), F32), jax.ShapeDtypeStruct((8, 128), F32),
                   jax.ShapeDtypeStruct((1, d), F32)],
        compiler_params=_params(1),
    )(x, g, target)


def _ffn_down_bwd(dx2, up, uc, w_down, l, name, comm=None):
    t, d = dx2.shape
    f2 = up.shape[1]
    f = f2 // 2
    tm = min(TM_ROW, t)

    def body(dx2_ref, up_ref, uph_ref, uc_ref, wd_ref, duc_ref, dx2b_ref, dcw_ref, xx_ref):
        i = pl.program_id(0)

        @pl.when(i == 0)
        def _():
            dcw_ref[...] = jnp.zeros_like(dcw_ref)

        xx_ref[pl.ds(0, HALO), :] = uph_ref[...].astype(F32) * (i > 0).astype(F32)
        xx_ref[pl.ds(HALO, tm), :] = up_ref[...].astype(F32)
        gt = uc_ref[:, 0:f].astype(F32)
        sg = _sig(gt)
        dx2b = dx2_ref[...].astype(BF)
        dx2b_ref[...] = dx2b
        dzf = _nt(dx2b, wd_ref[...])
        duc_ref[:, 0:f] = (dzf * uc_ref[:, f:f2].astype(F32) * (sg * (1.0 + gt * (1.0 - sg)))).astype(BF)
        duc_ref[:, f:f2] = (dzf * (gt * sg)).astype(BF)
        _tap_grads(dcw_ref, lambda r0, rows, c0: duc_ref[pl.ds(r0, rows), pl.ds(c0, LANES)].astype(F32),
                   _Rows(xx_ref), _causal_offsets(K_F), tm)

    return _run(
        body, comm, name=name, grid=(t // tm,),
        in_specs=[pl.BlockSpec((tm, d), lambda i: (i, 0)),
                  pl.BlockSpec((tm, f2), lambda i: (i, 0)),
                  pl.BlockSpec((HALO, f2), _prev_halo(tm, 0)),
                  pl.BlockSpec((tm, f2), lambda i: (i, 0)),
                  _resident((None, f, d), (l, 0, 0))],
        out_specs=[pl.BlockSpec((tm, f2), lambda i: (i, 0)), pl.BlockSpec((tm, d), lambda i: (i, 0)),
                   pl.BlockSpec((K_F, f2), lambda i: (0, 0))],
        out_shape=[jax.ShapeDtypeStruct((t, f2), BF), jax.ShapeDtypeStruct((t, d), BF),
                   jax.ShapeDtypeStruct((K_F, f2), F32)],
        scratch_shapes=[pltpu.VMEM((HALO + tm, f2), F32)],
        args=(dx2, up, up, uc, w_down))


def _ffn_conv_bwd(duc, conv_f, l, name, comm=None):
    t, f2 = duc.shape
    tm = min(TM_ROW, t)
    n_t = t // tm

    def body(duc_ref, nxt_ref, cw_ref, dup_ref, yy_ref):
        i = pl.program_id(0)
        yy_ref[pl.ds(0, tm), :] = duc_ref[...].astype(F32)
        yy_ref[pl.ds(tm, HALO), :] = nxt_ref[...].astype(F32) * (i < n_t - 1).astype(F32)

        def put(r0, rows, c0, conv):
            dup_ref[pl.ds(r0, rows), pl.ds(c0, LANES)] = conv.astype(BF)

        _conv_taps(_Rows(yy_ref), cw_ref, _anticausal_offsets(K_F), tm, put)

    return _run(
        body, comm, name=name, grid=(n_t,),
        in_specs=[pl.BlockSpec((tm, f2), lambda i: (i, 0)),
                  pl.BlockSpec((HALO, f2), _next_halo(tm, t, 0)),
                  _resident((None, K_F, f2), (l, 0, 0))],
        out_specs=[pl.BlockSpec((tm, f2), lambda i: (i, 0))],
        out_shape=[jax.ShapeDtypeStruct((t, f2), BF)],
        scratch_shapes=[pltpu.VMEM((tm + HALO, f2), F32)],
        args=(duc, duc, conv_f))


def _nt_matmul_norm_bwd(dy, w, gs, x, g, dres, name, comm=None):
    t, n = dy.shape
    d = x.shape[1]
    tm = min(TM_MM // 2, t)
    n_s, _, ns = w.shape
    n_k, tk = n_s // gs, gs * ns

    def body(dy_ref, w_ref, x_ref, g_ref, dres_ref, dx_ref, dxb_ref, dg_ref, acc_ref):
        i, k = pl.program_id(0), pl.program_id(1)

        @pl.when((i == 0) & (k == 0))
        def _():
            dg_ref[...] = jnp.zeros_like(dg_ref)

        @pl.when(k == 0)
        def _():
            acc_ref[...] = jnp.zeros_like(acc_ref)

        part = _nt(dy_ref[:, 0:ns], w_ref[0])
        for s in range(1, gs):
            part = part + _nt(dy_ref[:, s * ns:(s + 1) * ns], w_ref[s])
        acc_ref[...] += part

        @pl.when(k == n_k - 1)
        def _():
            xf = x_ref[...]
            r = lax.rsqrt(jnp.mean(xf * xf, axis=-1, keepdims=True) + EPS)
            xhat = xf * r
            dh = acc_ref[...]
            dg_ref[...] += jnp.sum(dh * xhat, axis=0, keepdims=True)
            dxh = dh * g_ref[...]
            dx = dres_ref[...] + r * (dxh - xhat * jnp.mean(dxh * xhat, axis=-1, keepdims=True))
            dx_ref[...] = dx
            dxb_ref[...] = dx.astype(BF)

    return _run(
        body, comm, name=name, grid=(t // tm, n_k),
        in_specs=[pl.BlockSpec((tm, tk), lambda i, k: (i, k)),
                  pl.BlockSpec((gs, d, ns), lambda i, k: (k, 0, 0)),
                  pl.BlockSpec((tm, d), lambda i, k: (i, 0)),
                  pl.BlockSpec((1, d), lambda i, k: (0, 0)),
                  pl.BlockSpec((tm, d), lambda i, k: (i, 0))],
        out_specs=[pl.BlockSpec((tm, d), lambda i, k: (i, 0)), pl.BlockSpec((tm, d), lambda i, k: (i, 0)),
                   pl.BlockSpec((1, d), lambda i, k: (0, 0))],
        out_shape=[jax.ShapeDtypeStruct((t, d), F32), jax.ShapeDtypeStruct((t, d), BF),
                   jax.ShapeDtypeStruct((1, d), F32)],
        scratch_shapes=[pltpu.VMEM((tm, d), F32)],
        args=(dy, w, x, g, dres))


def _mixer_bwd(dx1b, proj, ya, yb, yc, cb, kv, conv_a, conv_b, ln_g, ln_b, b_gate, wsq, l, name, comm=None):
    t, d = cb.shape
    m = kv.shape[0]
    tm = min(TM_ROW, t)
    hd = d // N_HEADS
    scale = 1.0 / math.sqrt(hd)

    def body(dx1b_ref, proj_ref, gch_ref, vh_ref, u0h_ref, ugh_ref, ya_ref, yb_ref, yc_ref, cb_ref, kv_ref,
             ca_w, cb_w, lng_ref, lnb_ref, bg_ref, wa_ref, wb_ref, wc_ref, wo_ref,
             dpre_ref, dya_ref, dyb_ref, dyc_ref, dkv_ref, dbg_ref, dlng_ref, dlnb_ref, dcbias_ref, dcaw_ref,
             dcbw_ref, xa_ref, xb_ref, sa_ref, sb_ref):
        i = pl.program_id(0)

        @pl.when(i == 0)
        def _():
            for ref in (dkv_ref, dbg_ref, dlng_ref, dlnb_ref, dcbias_ref, dcaw_ref, dcbw_ref):
                ref[...] = jnp.zeros_like(ref)

        _load_branch_inputs(i, proj_ref, gch_ref, vh_ref, u0h_ref, ugh_ref, xa_ref, xb_ref, d)
        xa, xb = _Rows(xa_ref, sa_ref, SHORT_RESIDUES), _Rows(xb_ref, sb_ref, ALL_RESIDUES)
        xa.shift()
        xb.shift()
        dmg = _nt(dx1b_ref[...], wo_ref[...])
        ys = (ya_ref, yb_ref, yc_ref)
        dys = (dya_ref, dyb_ref, dyc_ref)
        for b in range(3):
            gate = _sig(proj_ref[:, (6 + b) * d:(7 + b) * d].astype(F32) + bg_ref[:, b * d:(b + 1) * d])
            dys[b][...] = (gate * dmg).astype(BF)
            dpg = dmg * ys[b][...].astype(F32) * gate * (1.0 - gate)
            dpre_ref[:, (6 + b) * d:(7 + b) * d] = dpg.astype(BF)
            dbg_ref[:, b * d:(b + 1) * d] += jnp.sum(dpg, axis=0, keepdims=True)
        gb = proj_ref[:, 0:d].astype(F32)
        dza = _nt(dya_ref[...], wa_ref[...])

        def put_dgb(r0, rows, c0, ca):
            dpre_ref[pl.ds(r0, rows), pl.ds(c0, LANES)] = (dza[r0:r0 + rows, c0:c0 + LANES] * ca).astype(BF)

        _conv_taps(xa, ca_w, _causal_offsets(K_A), tm, put_dgb)
        dpre_ref[:, d:2 * d] = (dza * gb).astype(BF)
        _tap_grads(dcaw_ref, lambda r0, rows, c0: dpre_ref[pl.ds(r0, rows), pl.ds(d + c0, LANES)].astype(F32),
                   xa, _causal_offsets(K_A), tm)
        dpre_ref[:, 2 * d:3 * d] = jnp.zeros((tm, d), BF)
        cbv = cb_ref[...]
        mu = jnp.mean(cbv, axis=-1, keepdims=True)
        dlt = cbv - mu
        rstd = lax.rsqrt(jnp.mean(dlt * dlt, axis=-1, keepdims=True) + EPS)
        xhat = dlt * rstd
        lnb = xhat * lng_ref[...] + lnb_ref[...]
        sg = _sig(lnb)
        dzb = _nt(dyb_ref[...], wb_ref[...])
        dl = dzb * (sg * (1.0 + lnb * (1.0 - sg)))
        dlng_ref[...] += jnp.sum(dl * xhat, axis=0, keepdims=True)
        dlnb_ref[...] += jnp.sum(dl, axis=0, keepdims=True)
        dxh = dl * lng_ref[...]
        dcb = rstd * (dxh - jnp.mean(dxh, axis=-1, keepdims=True)
                      - xhat * jnp.mean(dxh * xhat, axis=-1, keepdims=True))
        dcbias_ref[...] += jnp.sum(dcb, axis=0, keepdims=True)
        dpre_ref[:, 3 * d:4 * d] = dcb.astype(BF)
        _tap_grads(dcbw_ref, lambda r0, rows, c0: dpre_ref[pl.ds(r0, rows), pl.ds(3 * d + c0, LANES)].astype(F32),
                   xb, _causal_offsets(K_B), tm)
        dpre_ref[:, 4 * d:5 * d] = jnp.zeros((tm, d), BF)
        do = _nt(dyc_ref[...], wc_ref[...]).astype(BF)
        for h in range(N_HEADS):
            qh = proj_ref[:, 5 * d + h * hd:5 * d + (h + 1) * hd]
            kh = kv_ref[:, h * hd:(h + 1) * hd]
            vh = kv_ref[:, d + h * hd:d + (h + 1) * hd]
            doh = do[:, h * hd:(h + 1) * hd]
            p = _softmax_rows(_nt(qh, kh) * scale)
            dp = _nt(doh, vh)
            ds = (p * (dp - jnp.sum(dp * p, axis=-1, keepdims=True)) * scale).astype(BF)
            dpre_ref[:, 5 * d + h * hd:5 * d + (h + 1) * hd] = _nn(ds, kh).astype(BF)
            dkv_ref[:, h * hd:(h + 1) * hd] += _tn(ds, qh)
            dkv_ref[:, d + h * hd:d + (h + 1) * hd] += _tn(p.astype(BF), doh)

    row = lambda w_: pl.BlockSpec((tm, w_), lambda i: (i, 0))
    halo = lambda col: pl.BlockSpec((HALO, d), _prev_halo(tm, col))
    sq = lambda which: _resident((None, None, d, d), (l, which, 0, 0))
    acc = lambda r, c: pl.BlockSpec((r, c), lambda i: (0, 0))
    act = jax.ShapeDtypeStruct((t, d), BF)
    vec = lambda r, c: jax.ShapeDtypeStruct((r, c), F32)
    return _run(
        body, comm, name=name, grid=(t // tm,),
        in_specs=[row(d), row(9 * d), halo(1), halo(2), halo(3), halo(4), row(d), row(d), row(d), row(d),
                  _resident((m, 2 * d), (0, 0)),
                  _resident((None, K_A, d), (l, 0, 0)), _resident((None, K_B, d), (l, 0, 0)),
                  _resident((1, d), (0, 0)), _resident((1, d), (0, 0)), _resident((1, 3 * d), (0, 0)),
                  sq(0), sq(1), sq(2), sq(3)],
        out_specs=[row(9 * d), row(d), row(d), row(d), acc(m, 2 * d), acc(1, 3 * d), acc(1, d), acc(1, d),
                   acc(1, d), acc(K_A, d), acc(K_B, d)],
        out_shape=[jax.ShapeDtypeStruct((t, 9 * d), BF), act, act, act, vec(m, 2 * d), vec(1, 3 * d), vec(1, d),
                   vec(1, d), vec(1, d), vec(K_A, d), vec(K_B, d)],
        scratch_shapes=[pltpu.VMEM((HALO + tm, d), F32), pltpu.VMEM((HALO + tm, d), F32),
                        _shifted_scratch(SHORT_RESIDUES, tm, d), _shifted_scratch(ALL_RESIDUES, tm, d)],
        args=(dx1b, proj, proj, proj, proj, proj, ya, yb, yc, cb, kv, conv_a, conv_b, ln_g, ln_b, b_gate,
              wsq, wsq, wsq, wsq))


def _inproj_conv_bwd(dpre, proj, conv_a, conv_b, l, name, comm=None):
    t, d9 = dpre.shape
    d = d9 // 9
    tm = min(TM_ROW, t)
    n_t = t // tm

    def body(dpre_ref, nxa_ref, nxb_ref, proj_ref, ca_w, cb_w, dproj_ref, ya_ref, yb_ref, sb_ref):
        i = pl.program_id(0)
        keep = (i < n_t - 1).astype(F32)
        ya_ref[pl.ds(0, tm), :] = dpre_ref[:, d:2 * d].astype(F32)
        ya_ref[pl.ds(tm, HALO), :] = nxa_ref[...].astype(F32) * keep
        yb_ref[pl.ds(0, tm), :] = dpre_ref[:, 3 * d:4 * d].astype(F32)
        yb_ref[pl.ds(tm, HALO), :] = nxb_ref[...].astype(F32) * keep
        yb = _Rows(yb_ref, sb_ref, ALL_RESIDUES)
        yb.shift()
        dproj_ref[:, 0:d] = dpre_ref[:, 0:d]
        dproj_ref[:, 5 * d:9 * d] = dpre_ref[:, 5 * d:9 * d]
        def chunk(ref, block, r0, rows, c0):
            return ref.at[pl.ds(r0, rows), pl.ds(block * d + c0, LANES)]

        def put_a(r0, rows, c0, dcv):
            chunk(dproj_ref, 1, r0, rows, c0)[...] = (dcv * chunk(proj_ref, 2, r0, rows, c0)[...].astype(F32)).astype(BF)
            chunk(dproj_ref, 2, r0, rows, c0)[...] = (dcv * chunk(proj_ref, 1, r0, rows, c0)[...].astype(F32)).astype(BF)

        def put_b(r0, rows, c0, dub):
            sg = _sig(chunk(proj_ref, 4, r0, rows, c0)[...].astype(F32))
            u0 = chunk(proj_ref, 3, r0, rows, c0)[...].astype(F32)
            chunk(dproj_ref, 3, r0, rows, c0)[...] = (dub * sg).astype(BF)
            chunk(dproj_ref, 4, r0, rows, c0)[...] = (dub * u0 * sg * (1.0 - sg)).astype(BF)

        _conv_taps(_Rows(ya_ref), ca_w, _anticausal_offsets(K_A), tm, put_a)
        _conv_taps(yb, cb_w, _anticausal_offsets(K_B), tm, put_b)

    return _run(
        body, comm, name=name, grid=(n_t,),
        in_specs=[pl.BlockSpec((tm, d9), lambda i: (i, 0)),
                  pl.BlockSpec((HALO, d), _next_halo(tm, t, 1)),
                  pl.BlockSpec((HALO, d), _next_halo(tm, t, 3)),
                  pl.BlockSpec((tm, d9), lambda i: (i, 0)),
                  _resident((None, K_A, d), (l, 0, 0)), _resident((None, K_B, d), (l, 0, 0))],
        out_specs=[pl.BlockSpec((tm, d9), lambda i: (i, 0))],
        out_shape=[jax.ShapeDtypeStruct((t, d9), BF)],
        scratch_shapes=[pltpu.VMEM((tm + HALO, d), F32), pltpu.VMEM((tm + HALO, d), F32),
                        _shifted_scratch(ALL_RESIDUES, tm, d)],
        args=(dpre, dpre, dpre, proj, conv_a, conv_b))


def _mem_kv_bwd(dkv, memn, mem, g, w_kv, name):
    m, d = mem.shape
    n_s, _, ns = w_kv.shape

    def body(dkv_ref, memn_ref, mem_ref, g_ref, w_ref, dw_ref, dg_ref):
        dkvb = dkv_ref[...].astype(BF)
        dw_ref[...] = _tn(memn_ref[...], dkvb)
        dmemn = _nt(dkvb[:, 0:ns], w_ref[0])
        for s in range(1, n_s):
            dmemn = dmemn + _nt(dkvb[:, s * ns:(s + 1) * ns], w_ref[s])
        xf = mem_ref[...]
        r = lax.rsqrt(jnp.mean(xf * xf, axis=-1, keepdims=True) + EPS)
        dg_ref[...] = jnp.sum(dmemn * (xf * r), axis=0, keepdims=True)

    return _pallas_call(
        body, name=name, grid=(1,),
        in_specs=[pl.BlockSpec((m, 2 * d), lambda i: (0, 0)), pl.BlockSpec((m, d), lambda i: (0, 0)),
                  pl.BlockSpec((m, d), lambda i: (0, 0)), pl.BlockSpec((1, d), lambda i: (0, 0)),
                  pl.BlockSpec((n_s, d, ns), lambda i: (0, 0, 0))],
        out_specs=[pl.BlockSpec((d, 2 * d), lambda i: (0, 0)), pl.BlockSpec((1, d), lambda i: (0, 0))],
        out_shape=[jax.ShapeDtypeStruct((d, 2 * d), F32), jax.ShapeDtypeStruct((1, d), F32)],
        compiler_params=_params(1),
    )(dkv, memn, mem, g, w_kv)


def _dw_matmul(a, b, tn, name, comm=None):
    t, k = a.shape
    n = b.shape[1]
    tt = min(TT_DW, t)
    while tt * k > DW_LHS_ELEMS and tt % 2 == 0:
        tt //= 2

    def body(a_ref, b_ref, o_ref):
        @pl.when(pl.program_id(1) == 0)
        def _():
            o_ref[...] = jnp.zeros_like(o_ref)

        o_ref[...] += _tn(a_ref[...], b_ref[...])

    return _run(
        body, comm, name=name, grid=(n // tn, t // tt),
        in_specs=[pl.BlockSpec((tt, k), lambda j, s: (s, 0)), pl.BlockSpec((tt, tn), lambda j, s: (s, j))],
        out_specs=[pl.BlockSpec((k, tn), lambda j, s: (0, j))],
        out_shape=[jax.ShapeDtypeStruct((k, n), F32)],
        args=(a, b))


class _GradReduce:
    def __init__(self, l, grads, core):
        self.l, self.core, self.names = l, core, tuple(grads)
        self.views = {n: _halves_view(n, g) for n, g in grads.items()}
        self.got, self.sums, self.landing, self.pieces = {}, {}, {}, {}

    def swap_program(self):
        return _swap_program([self.views[n] for n in self.names])

    def swapped(self, outs):
        self.got = dict(zip(self.names, outs))
        for n in self.names:
            self.sums[n], self.landing[n] = _add_halves(self.views[n], self.got[n], self.core, n in COL,
                                                        f"add_halves_{n}_{self.l}")

    def scatter_program(self, names=None):
        names = self.names if names is None else names
        return _scatter_program([self.sums[n] for n in names], [self.landing[n] for n in names], names)

    def scattered(self, outs, names=None):
        self.pieces.update(zip(self.names if names is None else names, outs))


def _step(x, mem, target, sh, conv_sh, small, core):
    depth = len(sh["w_in"])
    d = x.shape[1]
    f2 = sh["w_up"][0].shape[2] * N_CHIPS
    tn_dw_in = min(1024, d)
    tn_dw_up = f2 // 11 if f2 % (11 * 128) == 0 and f2 // 11 >= 128 else f2
    row = lambda v: v.reshape(1, -1)
    one = lambda a: a[None]

    w_in, cab, cf = _run_comm(_gather_program([sh["w_in"][0]] + conv_sh, ("col", "small", "small")), "gather_first")
    saved = []
    for l in range(depth):
        conv = dict(a=cab[l:l + 1, :K_A], b=cab[l:l + 1, K_A:], f=cf[l:l + 1])
        (h, proj), (wsq, w_kv) = _norm_matmul(
            x, row(small["norm_mix_g"][l]), w_in, 1, f"in_proj_{l}",
            _gather_program([sh["wsq"][l], sh["w_kv"][l]], ("row", "col")))
        memn, kv = _mem_kv(mem, row(small["norm_mem_g"][l]), w_kv, f"mem_kv_{l}")
        (x1, za, zb, o, ya, yb, yc, mg, cb), (w_up, w_down) = _mixer_fwd(
            proj, x, kv, conv["a"], conv["b"], row(small["conv_b_bias"][l]), row(small["ln_b_g"][l]),
            row(small["ln_b_b"][l]), row(small["b_gate"][l]), one(wsq), 0, f"mixer_fwd_{l}",
            _gather_program([sh["w_up"][l], sh["w_down"][l]], ("col", "row")))
        more = l + 1 < depth
        nxt = _gather_program([sh["w_in"][l + 1]], ("col",), (0, 2)) if more else None
        (h2, up), w_in_next = _norm_matmul(x1, row(small["norm_ffn_g"][l]), w_up, 2, f"up_proj_{l}", nxt)
        nxt = _gather_program(w_in_next, ("col",), (1, 2)) if more else None
        (x2, zf, uc), w_in_next = _ffn_down_fwd(up, x1, conv["f"], one(w_down), 0, f"ffn_down_fwd_{l}", nxt)
        saved.append(dict(x=x, x1=x1, memn=memn, kv=kv, h=h, proj=proj, za=za, zb=zb, o=o, ya=ya, yb=yb, yc=yc,
                          mg=mg, cb=cb, h2=h2, up=up, zf=zf, uc=uc, w_in=w_in, wsq=one(wsq), w_kv=w_kv,
                          w_up=w_up, w_down=one(w_down), conv=conv))
        x = x2
        w_in = w_in_next[0] if w_in_next else None
    dx, loss, dg_final = _final_loss(x, row(small["norm_final_g"]), target, "final_loss")
    sgrads, pieces = [None] * depth, [None] * depth
    above = None
    first, second, rest = ("w_up",), ("w_down",) + SQUARES, ("w_in", "w_kv")
    for l in reversed(range(depth)):
        s = saved[l]
        conv = s["conv"]
        bottom = l == 0
        (duc, dx2b, dconv_f), got = _ffn_down_bwd(dx, s["up"], s["uc"], s["w_down"], 0, f"ffn_down_bwd_{l}",
                                                  above and above.swap_program())
        if above:
            above.swapped(got)
        (dup,), got = _ffn_conv_bwd(duc, conv["f"], 0, f"ffn_conv_bwd_{l}", above and above.scatter_program(first))
        if above:
            above.scattered(got, first)
        (dx1, dx1b, dg_ffn), got = _nt_matmul_norm_bwd(
            dup, s["w_up"], 2, s["x1"], row(small["norm_ffn_g"][l]), dx, f"up_proj_bwd_{l}",
            above and above.scatter_program(second))
        if above:
            above.scattered(got, second)
        grads = dict(w_up=_dw_matmul(s["h2"], dup, tn_dw_up, f"dw_up_{l}")[0][0],
                     w_down=_dw_matmul(s["zf"], dx2b, d, f"dw_down_{l}")[0][0])
        ffn = _GradReduce(l, grads, core) if bottom else None
        (dpre, dya, dyb, dyc, dkv, dbg, dlng, dlnb, dcbias, dconv_a, dconv_b), got = _mixer_bwd(
            dx1b, s["proj"], s["ya"], s["yb"], s["yc"], s["cb"], s["kv"], conv["a"], conv["b"],
            row(small["ln_b_g"][l]), row(small["ln_b_b"][l]), row(small["b_gate"][l]), s["wsq"], 0,
            f"mixer_bwd_{l}", _join([above and above.scatter_program(rest), ffn and ffn.swap_program()]))
        if above:
            above.scattered(got[:len(rest)], rest)
            pieces[above.l] = above.pieces
            got = got[len(rest):]
        if ffn:
            ffn.swapped(got)
        dw_kv, dg_mem = _mem_kv_bwd(dkv, s["memn"], mem, row(small["norm_mem_g"][l]), s["w_kv"], f"mem_kv_bwd_{l}")
        mix_grads = dict(w_a_out=_dw_matmul(s["za"], dya, d, f"dw_a_out_{l}")[0][0],
                         w_b_out=_dw_matmul(s["zb"], dyb, d, f"dw_b_out_{l}")[0][0],
                         w_att_out=_dw_matmul(s["o"], dyc, d, f"dw_att_out_{l}")[0][0],
                         w_o=_dw_matmul(s["mg"], dx1b, d, f"dw_o_{l}")[0][0], w_kv=dw_kv)
        mix = _GradReduce(l, mix_grads, core) if bottom else None
        (dproj,), got = _inproj_conv_bwd(dpre, s["proj"], conv["a"], conv["b"], 0, f"inproj_conv_bwd_{l}",
                                         _join([ffn and ffn.scatter_program(), mix and mix.swap_program()]))
        if bottom:
            ffn.scattered(got[:len(ffn.names)])
            mix.swapped(got[len(ffn.names):])
        in_grads = dict(w_in=_dw_matmul(s["h"], dproj, tn_dw_in, f"dw_in_{l}")[0][0])
        inp = _GradReduce(l, in_grads, core) if bottom else None
        (dx0, _, dg_mix), got = _nt_matmul_norm_bwd(
            dproj, s["w_in"], 2, s["x"], row(small["norm_mix_g"][l]), dx1, f"in_proj_bwd_{l}",
            _join([mix and mix.scatter_program(), inp and inp.swap_program()]))
        if bottom:
            mix.scattered(got[:len(mix.names)])
            inp.swapped(got[len(mix.names):])
            inp.scattered(_run_comm(inp.scatter_program(), f"scatter_w_in_{l}"))
            pieces[l] = {**ffn.pieces, **mix.pieces, **inp.pieces}
        else:
            above = _GradReduce(l, {**grads, **mix_grads, **in_grads}, core)
        sgrads[l] = dict(norm_mix_g=dg_mix, norm_mem_g=dg_mem, b_gate=dbg, conv_b_bias=dcbias, ln_b_g=dlng,
                         ln_b_b=dlnb, norm_ffn_g=dg_ffn, conv_a_w=dconv_a, conv_b_w=dconv_b, conv_ffn_w=dconv_f)
        dx = dx0
    return loss, dx, sgrads, dg_final, pieces


BIG = ("w_in", "w_a_out", "w_b_out", "w_att_out", "w_o", "w_kv", "w_up", "w_down")
COL = ("w_in", "w_kv", "w_up")
SQUARES = ("w_a_out", "w_b_out", "w_att_out", "w_o")
ANY = pl.BlockSpec(memory_space=pl.ANY)


def _place():
    x, y, c = lax.axis_index("x"), lax.axis_index("y"), lax.axis_index("c")
    chips = [(1 - x, y), (x, 1 - y), (1 - x, 1 - y)]
    return x, y, c, 2 * x + y, chips


def _remote(src, dst, send_sem, recv_sem, dev):
    return pltpu.make_async_remote_copy(src_ref=src, dst_ref=dst, send_sem=send_sem, recv_sem=recv_sem,
                                        device_id=dev, device_id_type=MESH)


class _Striped:
    def __init__(self, src, dst, make):
        rows = src.shape[-2]
        unit = 8 * (4 // jnp.dtype(src.dtype).itemsize)
        n = max(k for k in range(1, MAX_STRIPES + 1) if rows % (unit * k) == 0) if rows % unit == 0 else 1
        q = rows // n
        self.parts = [make(_window(src, pl.ds(i * q, q), slice(None)), _window(dst, pl.ds(i * q, q), slice(None)))
                      for i in range(n)]
        self.whole = make(src, dst)

    def start(self):
        for p in self.parts:
            p.start()

    def wait(self):
        self.whole.wait()

    def wait_send(self):
        self.whole.wait_send()

    def wait_recv(self):
        self.whole.wait_recv()


def _far(src, dst, send_sem, recv_sem, dev):
    return _Striped(src, dst, lambda s, d: _remote(s, d, send_sem, recv_sem, dev))


def _near(src, dst, sem):
    return _Striped(src, dst, lambda s, d: pltpu.make_async_copy(s, d, sem))


def _window(ref, rows, cols):
    return ref.at[(slice(None),) * (len(ref.shape) - 2) + (rows, cols)]


def _row_tile(rows, cols, unit=16, limit=1 << 20):
    best = unit
    for tr in range(unit, rows + 1, unit):
        if rows % tr == 0 and tr * cols <= limit:
            best = tr
    return best


def _cast_place(ws, l, kind, chip, name):
    _, k, n = ws[0].shape
    if kind == "col":
        shape, spec = (N_CHIPS, k, n), pl.BlockSpec((None, k, n), lambda i, c: (c[0], 0, 0))
    elif len(ws) == 1:
        shape, spec = (N_CHIPS * k, n), pl.BlockSpec((k, n), lambda i, c: (c[0], 0))
    else:
        shape, spec = (len(ws), N_CHIPS * k, n), pl.BlockSpec((len(ws), k, n), lambda i, c: (0, c[0], 0))

    def body(c_ref, *refs):
        o_ref = refs[-1]
        if len(ws) == 1 or kind == "col":
            o_ref[...] = refs[0][...].astype(BF)
        else:
            for i in range(len(ws)):
                o_ref[i] = refs[i][...].astype(BF)

    return _pallas_call(
        body, name=name,
        grid_spec=pltpu.PrefetchScalarGridSpec(
            num_scalar_prefetch=1, grid=(1,),
            in_specs=[pl.BlockSpec((None, k, n), lambda i, c: (l, 0, 0))] * len(ws), out_specs=spec),
        out_shape=jax.ShapeDtypeStruct(shape, BF),
        compiler_params=_params(1),
    )(chip, *ws)


def _gather_program(arrays, kinds, part=(0, 1)):
    n_t = len(arrays)
    index, count = part

    def shard_rows(f, kind):
        return f.shape[-2] if kind == "col" else f.shape[-2] // N_CHIPS

    def run(phase, ins, full, sems):
        ici_send, ici_recv, sib_send, sib_recv, loc_sem = sems
        x, y, c, me, chips = _place()
        sibling = (x, y, 1 - c)

        def part_of(i, chip, half):
            f, kind = full[i], kinds[i]
            if kind == "small":
                cols = ins[i].shape[-1]
                return _window(f, slice(None), pl.ds(pl.multiple_of(chip * cols, 128), cols))
            rows = shard_rows(f, kind)
            r = rows // (2 * count)
            at = (half * count + index) * r
            if kind == "col":
                return f.at[chip, pl.ds(pl.multiple_of(at, 16), r), :]
            return _window(f, pl.ds(pl.multiple_of(chip * rows + at, 16), r), slice(None))

        src_part = lambda i, half: ins[i] if kinds[i] == "small" else part_of(i, me, half)
        dst_part = part_of
        local = [_near(ins[i], part_of(i, me, c), loc_sem.at[i]) for i in range(n_t) if kinds[i] == "small"]
        sends = []
        for i in range(n_t):
            for j, chip in enumerate(chips):
                sends.append(_far(src_part(i, c), dst_part(i, me, c), ici_send.at[3 * i + j],
                                  ici_recv.at[3 * i + j], (*chip, c)))
        if phase == "start":
            for cp in local + sends:
                cp.start()
            return
        passed = []
        for i in range(n_t):
            for j, chip in enumerate(chips):
                k = 2 * chip[0] + chip[1]
                landed = dst_part(i, k, c)
                if phase == "forward":
                    _remote(landed, landed, ici_send.at[3 * i + j], ici_recv.at[3 * i + j], (*chip, c)).wait_recv()
                if kinds[i] != "small":
                    passed.append(_far(landed, landed, sib_send.at[3 * i + j], sib_recv.at[3 * i + j], sibling))
                    if phase == "forward":
                        passed[-1].start()
        if phase == "forward":
            return
        for i in range(n_t):
            if kinds[i] == "small":
                continue
            for j, chip in enumerate(chips):
                k = 2 * chip[0] + chip[1]
                other = dst_part(i, k, 1 - c)
                _remote(other, other, sib_send.at[3 * i + j], sib_recv.at[3 * i + j], sibling).wait_recv()
        for cp in sends + passed:
            cp.wait_send()
        for cp in local:
            cp.wait()

    def out_shape(a, kind):
        shp = a.shape[:-1] + (a.shape[-1] * N_CHIPS,) if kind == "small" else a.shape
        return jax.ShapeDtypeStruct(shp, a.dtype)

    outs = [out_shape(a, k) for a, k in zip(arrays, kinds)]
    sems = [pltpu.SemaphoreType.DMA((3 * n_t,))] * 4 + [pltpu.SemaphoreType.DMA((n_t,))]
    return _Comm(arrays, outs, sems, run, {i: i for i in range(n_t) if kinds[i] != "small"})


def _all_reduce_small(part, name):
    r, n = part.shape

    def body(in_ref, out_ref, gather_ref, send_sems, recv_sems):
        x, y, c, _, _ = _place()
        me = 4 * x + 2 * y + c
        gather_ref[me] = in_ref[...]
        sends = []
        for k in range(1, N_DEV):
            peer = (me + k) % N_DEV
            sends.append(_remote(in_ref, gather_ref.at[me], send_sems.at[k - 1], recv_sems.at[k - 1],
                                 (peer // 4, (peer // 2) % 2, peer % 2)))
        for cp in sends:
            cp.start()
        for k in range(1, N_DEV):
            origin = (me + N_DEV - k) % N_DEV
            _remote(in_ref, gather_ref.at[origin], send_sems.at[k - 1], recv_sems.at[k - 1],
                    (x, y, c)).wait_recv()
        for cp in sends:
            cp.wait_send()
        total = gather_ref[0]
        for dev in range(1, N_DEV):
            total = total + gather_ref[dev]
        out_ref[...] = total

    vm = pl.BlockSpec(memory_space=pltpu.VMEM)
    return _pallas_call(
        body, name=name, in_specs=[vm], out_specs=vm, out_shape=jax.ShapeDtypeStruct((r, n), F32),
        scratch_shapes=[pltpu.VMEM((N_DEV, r, n), F32), pltpu.SemaphoreType.DMA((N_DEV - 1,)),
                        pltpu.SemaphoreType.DMA((N_DEV - 1,))],
        compiler_params=pltpu.CompilerParams(vmem_limit_bytes=VMEM_LIMIT),
    )(part)


def _halves_view(name, dw):
    k, n = dw.shape
    s = 1 if name in COL else N_CHIPS
    return dw.reshape(s, 2, k // (2 * s), n)


def _swap_program(views):
    n_t = len(views)

    def run(phase, src, dst, sems):
        send_sems, recv_sems = sems
        x, y, c, _, _ = _place()
        copies = [_far(src[i].at[:, 1 - c], dst[i], send_sems.at[i], recv_sems.at[i], (x, y, 1 - c))
                  for i in range(n_t)]
        for cp in copies:
            if phase == "start":
                cp.start()
            elif phase == "finish":
                cp.wait()

    outs = [jax.ShapeDtypeStruct((v.shape[0],) + v.shape[2:], F32) for v in views]
    sems = [pltpu.SemaphoreType.DMA((n_t,)), pltpu.SemaphoreType.DMA((n_t,))]
    return _Comm(views, outs, sems, run)


def _add_halves(view, got, place, col, name):
    s, _, r, n = view.shape
    if col:
        cw = n // N_CHIPS
        tr = _row_tile(r, cw)
        grid = (r // tr, N_CHIPS)
        in_specs = [pl.BlockSpec((None, None, tr, cw), lambda j, q, p: (0, p[0], j, q)),
                    pl.BlockSpec((None, tr, cw), lambda j, q, p: (0, j, q))]
        out_specs = [pl.BlockSpec((None, tr, cw), lambda j, q, p: (0, j, q)),
                     pl.BlockSpec((None, None, tr, cw), lambda j, q, p: (p[0], p[1], j, 0))]
    else:
        cw = n
        tr = _row_tile(r, n)
        grid = (s, r // tr)
        in_specs = [pl.BlockSpec((None, None, tr, n), lambda i, j, p: (i, p[0], j, 0)),
                    pl.BlockSpec((None, tr, n), lambda i, j, p: (i, j, 0))]
        out_specs = [pl.BlockSpec((None, tr, n), lambda i, j, p: (i, j, 0)),
                     pl.BlockSpec((None, None, tr, n), lambda i, j, p: (p[0], i, j, 0))]

    def body(p_ref, a_ref, b_ref, o_ref, z_ref):
        total = (a_ref[...] + b_ref[...]).astype(BF)
        o_ref[...] = total
        if col:
            @pl.when(pl.program_id(1) == p_ref[1])
            def _():
                z_ref[...] = total
        else:
            z_ref[...] = total

    return _pallas_call(
        body, name=name,
        grid_spec=pltpu.PrefetchScalarGridSpec(num_scalar_prefetch=1, grid=grid, in_specs=in_specs,
                                               out_specs=out_specs),
        out_shape=[jax.ShapeDtypeStruct((s, r, n), BF), jax.ShapeDtypeStruct((2, N_CHIPS, r, cw), BF)],
        compiler_params=_params(2),
    )(place, view, got)


def _scatter_program(sums, landing, names):
    n_t = len(sums)

    def run(phase, src, dst, sems):
        ici_send, ici_recv, sib_send, sib_recv = sems
        x, y, c, me, chips = _place()
        sibling = (x, y, 1 - c)

        def piece(i, chip):
            if names[i] in COL:
                cw = src[i].shape[2] // N_CHIPS
                return src[i].at[0, :, pl.ds(pl.multiple_of(chip * cw, 128), cw)]
            return src[i].at[chip]

        local = []
        sends = []
        for i in range(n_t):
            sends.append(_far(piece(i, me), dst[i].at[c, me], sib_send.at[4 * i + 3], sib_recv.at[4 * i + 3],
                              sibling))
            for j, chip in enumerate(chips):
                k = 2 * chip[0] + chip[1]
                sends.append(_far(piece(i, k), dst[i].at[c, me], ici_send.at[3 * i + j], ici_recv.at[3 * i + j],
                                  (*chip, c)))
        if phase == "start":
            for cp in local + sends:
                cp.start()
            return
        passed = []
        for i in range(n_t):
            for j, chip in enumerate(chips):
                k = 2 * chip[0] + chip[1]
                landed = dst[i].at[c, k]
                if phase == "forward":
                    _remote(landed, landed, ici_send.at[3 * i + j], ici_recv.at[3 * i + j], (*chip, c)).wait_recv()
                passed.append(_far(landed, landed, sib_send.at[4 * i + j], sib_recv.at[4 * i + j], sibling))
                if phase == "forward":
                    passed[-1].start()
        if phase == "forward":
            return
        for i in range(n_t):
            other = dst[i].at[1 - c, me]
            _remote(other, other, sib_send.at[4 * i + 3], sib_recv.at[4 * i + 3], sibling).wait_recv()
            for j, chip in enumerate(chips):
                k = 2 * chip[0] + chip[1]
                other = dst[i].at[1 - c, k]
                _remote(other, other, sib_send.at[4 * i + j], sib_recv.at[4 * i + j], sibling).wait_recv()
        for cp in sends + passed:
            cp.wait_send()
        for cp in local:
            cp.wait()

    outs = [jax.ShapeDtypeStruct(z.shape, z.dtype) for z in landing]
    sems = [pltpu.SemaphoreType.DMA((3 * n_t,))] * 2 + [pltpu.SemaphoreType.DMA((4 * n_t,))] * 2
    return _Comm(list(sums) + list(landing), outs, sems, run, {n_t + i: i for i in range(n_t)})


def _adamw(w, g, m, v):
    m = ADAM_B1 * m + (1.0 - ADAM_B1) * g
    v = ADAM_B2 * v + (1.0 - ADAM_B2) * (g * g)
    m_hat = m / (1.0 - ADAM_B1 ** ADAM_STEP)
    v_hat = v / (1.0 - ADAM_B2 ** ADAM_STEP)
    return -ADAM_LR * (m_hat / (jnp.sqrt(v_hat) + ADAM_EPS) + ADAM_WD * w), m, v


def _adam_shard(pieces, w, m, v, l, prev, name):
    depth, rows, cw = w.shape
    hr = rows // 2
    tr = _row_tile(hr, cw, limit=1 << 18)
    n_i = hr // tr

    def body(*refs):
        z_ref, w_ref, m_ref, v_ref = refs[:4]
        g_ref, d_ref, nm_ref, nv_ref = refs[-4:]
        g = z_ref[0].astype(F32)
        for k in range(1, N_CHIPS):
            g = g + z_ref[k].astype(F32)
        g_ref[...] = g
        d_ref[...], nm_ref[...], nv_ref[...] = _adamw(w_ref[...], g, m_ref[...], v_ref[...])

    par = pl.BlockSpec((None, tr, cw), lambda h, i: (l, h * n_i + i, 0))
    out = jax.ShapeDtypeStruct((depth, rows, cw), F32)
    extra = [] if prev is None else list(prev)
    return _pallas_call(
        body, name=name, grid=(2, n_i),
        in_specs=[pl.BlockSpec((None, N_CHIPS, tr, cw), lambda h, i: (h, 0, i, 0)), par, par, par] + [ANY] * len(extra),
        out_specs=[par] * 4, out_shape=[out] * 4,
        input_output_aliases={4 + k: k for k in range(len(extra))},
        compiler_params=_params(2),
    )(pieces, w, m, v, *extra)


def _adam_small(g, w, m, v, name):
    def body(g_ref, w_ref, m_ref, v_ref, d_ref, nm_ref, nv_ref):
        d_ref[...], nm_ref[...], nv_ref[...] = _adamw(w_ref[...], g_ref[...], m_ref[...], v_ref[...])

    vm = pl.BlockSpec(memory_space=pltpu.VMEM)
    out = jax.ShapeDtypeStruct(g.shape, F32)
    return _pallas_call(body, name=name, in_specs=[vm] * 4, out_specs=[vm] * 3, out_shape=[out] * 3)(g, w, m, v)


WEIGHTS = ("norm_mix_g", "norm_mem_g", "w_in", "b_gate", "conv_a_w", "w_a_out", "conv_b_w", "conv_b_bias", "ln_b_g",
           "ln_b_b", "w_b_out", "w_kv", "w_att_out", "w_o", "norm_ffn_g", "w_up", "conv_ffn_w", "w_down",
           "norm_final_g")
REPLICATED = ("norm_mix_g", "norm_mem_g", "b_gate", "conv_b_bias", "ln_b_g", "ln_b_b", "norm_ffn_g")
CONVS = ("conv_a_w", "conv_b_w", "conv_ffn_w")
PACK_WIDTH = 1024


def _pack(arrays):
    flat = jnp.concatenate([a.reshape(-1) for a in arrays])
    size = -(-flat.shape[0] // (8 * PACK_WIDTH)) * (8 * PACK_WIDTH)
    return jnp.pad(flat, (0, size - flat.shape[0])).reshape(-1, PACK_WIDTH)


def _unpack(packed, shapes):
    flat = packed.reshape(-1)
    out, at = [], 0
    for shp in shapes:
        n = math.prod(shp)
        out.append(flat[at:at + n].reshape(shp))
        at += n
    return out


def kernel(x, mem, norm_mix_g, norm_mem_g, w_in, b_gate, conv_a_w, w_a_out, conv_b_w, conv_b_bias, ln_b_g, ln_b_b, w_b_out, w_kv, w_att_out, w_o, norm_ffn_g, w_up, conv_ffn_w, w_down, norm_final_g, loss_target, m_norm_mix_g, m_norm_mem_g, m_w_in, m_b_gate, m_conv_a_w, m_w_a_out, m_conv_b_w, m_conv_b_bias, m_ln_b_g, m_ln_b_b, m_w_b_out, m_w_kv, m_w_att_out, m_w_o, m_norm_ffn_g, m_w_up, m_conv_ffn_w, m_w_down, m_norm_final_g, v_norm_mix_g, v_norm_mem_g, v_w_in, v_b_gate, v_conv_a_w, v_w_a_out, v_conv_b_w, v_conv_b_bias, v_ln_b_g, v_ln_b_b, v_w_b_out, v_w_kv, v_w_att_out, v_w_o, v_norm_ffn_g, v_w_up, v_conv_ffn_w, v_w_down, v_norm_final_g):
    w = dict(norm_mix_g=norm_mix_g, norm_mem_g=norm_mem_g, w_in=w_in, b_gate=b_gate, conv_a_w=conv_a_w,
             w_a_out=w_a_out, conv_b_w=conv_b_w, conv_b_bias=conv_b_bias, ln_b_g=ln_b_g, ln_b_b=ln_b_b,
             w_b_out=w_b_out, w_kv=w_kv, w_att_out=w_att_out, w_o=w_o, norm_ffn_g=norm_ffn_g, w_up=w_up,
             conv_ffn_w=conv_ffn_w, w_down=w_down, norm_final_g=norm_final_g)
    mom = dict(norm_mix_g=m_norm_mix_g, norm_mem_g=m_norm_mem_g, w_in=m_w_in, b_gate=m_b_gate, conv_a_w=m_conv_a_w,
               w_a_out=m_w_a_out, conv_b_w=m_conv_b_w, conv_b_bias=m_conv_b_bias, ln_b_g=m_ln_b_g, ln_b_b=m_ln_b_b,
               w_b_out=m_w_b_out, w_kv=m_w_kv, w_att_out=m_w_att_out, w_o=m_w_o, norm_ffn_g=m_norm_ffn_g,
               w_up=m_w_up, conv_ffn_w=m_conv_ffn_w, w_down=m_w_down, norm_final_g=m_norm_final_g)
    var = dict(norm_mix_g=v_norm_mix_g, norm_mem_g=v_norm_mem_g, w_in=v_w_in, b_gate=v_b_gate, conv_a_w=v_conv_a_w,
               w_a_out=v_w_a_out, conv_b_w=v_conv_b_w, conv_b_bias=v_conv_b_bias, ln_b_g=v_ln_b_g, ln_b_b=v_ln_b_b,
               w_b_out=v_w_b_out, w_kv=v_w_kv, w_att_out=v_w_att_out, w_o=v_w_o, norm_ffn_g=v_norm_ffn_g,
               w_up=v_w_up, conv_ffn_w=v_conv_ffn_w, w_down=v_w_down, norm_final_g=v_norm_final_g)
    depth = w_in.shape[0]
    chip = 2 * lax.axis_index("x") + lax.axis_index("y")
    core = jnp.stack([lax.axis_index("c"), chip]).astype(jnp.int32)

    chip1 = chip.astype(jnp.int32).reshape(1)
    layers = range(depth)
    sh = dict(w_in=[_cast_place([w_in], l, "col", chip1, f"cast_w_in_{l}") for l in layers],
              wsq=[_cast_place([w[n] for n in SQUARES], l, "row", chip1, f"cast_squares_{l}") for l in layers],
              w_kv=[_cast_place([w_kv], l, "col", chip1, f"cast_w_kv_{l}") for l in layers],
              w_up=[_cast_place([w_up], l, "col", chip1, f"cast_w_up_{l}") for l in layers],
              w_down=[_cast_place([w_down], l, "row", chip1, f"cast_w_down_{l}") for l in layers])
    conv_sh = [jnp.concatenate([conv_a_w, conv_b_w], axis=1), conv_ffn_w]
    small = {n: w[n] for n in REPLICATED + ("norm_final_g",)}

    loss, dx, sgrads, dg_final, pieces = _step(x[0], mem[0], loss_target[0], sh, conv_sh, small, core)

    res = {n: None for n in BIG}
    for l in reversed(range(depth)):
        for n in BIG:
            res[n] = _adam_shard(pieces[l][n], w[n], mom[n], var[n], l, res[n], f"adam_{n}_{l}")

    per_layer = REPLICATED + CONVS
    parts = [sgrads[l][n] for l in range(depth) for n in per_layer] + [dg_final]
    total = _all_reduce_small(_pack(parts), "all_reduce_small")
    shapes = [sgrads[l][n].shape for l in range(depth) for n in per_layer] + [dg_final.shape]
    summed = _unpack(total, shapes)
    g_small = {}
    for k, n in enumerate(per_layer):
        full = jnp.stack([summed[l * len(per_layer) + k] for l in range(depth)])
        if n in CONVS:
            cols = w[n].shape[-1]
            full = lax.dynamic_slice_in_dim(full, chip * cols, cols, axis=2)
        g_small[n] = full.reshape(w[n].shape)
    g_small["norm_final_g"] = summed[-1].reshape(norm_final_g.shape)
    names = per_layer + ("norm_final_g",)
    d_p, m_p, v_p = _adam_small(_pack([g_small[n] for n in names]), _pack([w[n] for n in names]),
                                _pack([mom[n] for n in names]), _pack([var[n] for n in names]), "adam_small")
    shp = [w[n].shape for n in names]
    for n, dl, nm, nv in zip(names, _unpack(d_p, shp), _unpack(m_p, shp), _unpack(v_p, shp)):
        res[n] = (g_small[n], dl, nm, nv)

    loss = lax.psum(loss[0, 0], ("x", "y", "c"))
    return (loss, dx.reshape(x.shape), *[res[n][0] for n in WEIGHTS], *[res[n][1] for n in WEIGHTS],
            *[res[n][2] for n in WEIGHTS], *[res[n][3] for n in WEIGHTS])
```

```python
import functools
import math

import jax
import jax.numpy as jnp
from jax import lax
from jax.experimental import pallas as pl
from jax.experimental.pallas import tpu as pltpu

F32 = jnp.float32
BF = jnp.bfloat16
EPS = 1e-6
N_HEADS = 4
K_A, K_B, K_F = 3, 31, 3
ADAM_LR, ADAM_B1, ADAM_B2, ADAM_EPS, ADAM_WD, ADAM_STEP = 0.001, 0.9, 0.999, 1e-08, 0.01, 10
N_CHIPS = 4
N_DEV = 8
HALO = 32
MAX_STRIPES = 8
TM_ROW = 256
TM_MM = 1024
TT_DW = 4096
DW_LHS_ELEMS = 4 * 1024 * 1024
TR_EW = 128
VMEM_LIMIT = 56 * 1024 * 1024
MESH = pl.DeviceIdType.MESH
_pallas_call = pl.pallas_call


def _params(n_axes):
    return pltpu.CompilerParams(dimension_semantics=("arbitrary",) * n_axes, vmem_limit_bytes=VMEM_LIMIT)


def _resident(shape, index):
    return pl.BlockSpec(shape, lambda *_: index, pipeline_mode=pl.Buffered(1))


def _sig(x):
    return 1.0 / (1.0 + jnp.exp(-x))


def _nt(a, b):
    return lax.dot_general(a, b, (((1,), (1,)), ((), ())), preferred_element_type=F32)


def _tn(a, b):
    return lax.dot_general(a, b, (((0,), (0,)), ((), ())), preferred_element_type=F32)


def _nn(a, b):
    return jnp.dot(a, b, preferred_element_type=F32)


class _Comm:
    def __init__(self, inputs, out_shape, scratch, run, aliases=None):
        self.inputs, self.out_shape, self.scratch, self.run = list(inputs), list(out_shape), list(scratch), run
        self.aliases = dict(aliases or {})


def _join(programs):
    programs = [p for p in programs if p is not None]
    if not programs:
        return None

    def split(seq, counts):
        parts, at = [], 0
        for n in counts:
            parts.append(seq[at:at + n])
            at += n
        return parts

    n_in = [len(p.inputs) for p in programs]
    n_out = [len(p.out_shape) for p in programs]
    n_s = [len(p.scratch) for p in programs]

    def run(phase, ins, outs, sems):
        for p, i, o, s in zip(programs, split(ins, n_in), split(outs, n_out), split(sems, n_s)):
            p.run(phase, i, o, s)

    aliases, in_at, out_at = {}, 0, 0
    for p, i, o in zip(programs, n_in, n_out):
        aliases.update({in_at + a: out_at + b for a, b in p.aliases.items()})
        in_at, out_at = in_at + i, out_at + o
    return _Comm([a for p in programs for a in p.inputs], [a for p in programs for a in p.out_shape],
                 [a for p in programs for a in p.scratch], run, aliases)


def _run(body, comm, *, name, grid, in_specs, out_specs, out_shape, args, scratch_shapes=()):
    n_axes = len(grid)
    if comm is None:
        outs = _pallas_call(body, name=name, grid=grid, in_specs=list(in_specs), out_specs=list(out_specs),
                            out_shape=list(out_shape), scratch_shapes=list(scratch_shapes),
                            compiler_params=_params(n_axes))(*args)
        return list(outs), []
    counts = (len(in_specs), len(comm.inputs), len(out_specs), len(comm.out_shape), len(scratch_shapes),
              len(comm.scratch))

    def hosted(*refs):
        parts, at = [], 0
        for n in counts:
            parts.append(refs[at:at + n])
            at += n
        ins, c_ins, outs, c_outs, scr, c_sems = parts
        step = pl.program_id(0)
        for a in range(1, n_axes):
            step = step * grid[a] + pl.program_id(a)
        total = math.prod(grid)
        late = max(0, total - 1 - max(1, total // 4))

        @pl.when(step == 0)
        def _():
            comm.run("start", c_ins, c_outs, c_sems)

        body(*ins, *outs, *scr)

        @pl.when(step == late)
        def _():
            comm.run("forward", c_ins, c_outs, c_sems)

        @pl.when(step == total - 1)
        def _():
            comm.run("finish", c_ins, c_outs, c_sems)

    any_spec = pl.BlockSpec(memory_space=pl.ANY)
    res = _pallas_call(
        hosted, name=name, grid=grid, in_specs=list(in_specs) + [any_spec] * counts[1],
        out_specs=list(out_specs) + [any_spec] * counts[3], out_shape=list(out_shape) + comm.out_shape,
        scratch_shapes=list(scratch_shapes) + comm.scratch, compiler_params=_params(n_axes),
        input_output_aliases={counts[0] + a: counts[2] + b for a, b in comm.aliases.items()},
    )(*args, *comm.inputs)
    return list(res[:counts[2]]), list(res[counts[2]:])


def _run_comm(comm, name):
    n_in, n_out = len(comm.inputs), len(comm.out_shape)

    def body(*refs):
        ins, outs, sems = refs[:n_in], refs[n_in:n_in + n_out], refs[n_in + n_out:]
        for phase in ("start", "forward", "finish"):
            comm.run(phase, ins, outs, sems)

    any_spec = pl.BlockSpec(memory_space=pl.ANY)
    return list(_pallas_call(body, name=name, in_specs=[any_spec] * n_in, out_specs=[any_spec] * n_out,
                             out_shape=comm.out_shape, scratch_shapes=comm.scratch,
                             input_output_aliases=comm.aliases)(*comm.inputs))


SUBLANES = 8
LANES = 128
ROW_CHUNK = 128
ALL_RESIDUES = tuple(range(1, SUBLANES))
SHORT_RESIDUES = tuple(sorted({(HALO - K_A + 1 + k) % SUBLANES for k in range(K_A)} - {0}))


class _Rows:
    def __init__(self, ref, shifted_ref=None, residues=()):
        self.ref, self.shifted_ref, self.residues = ref, shifted_ref, tuple(residues)

    def shift(self):
        n = self.shifted_ref.shape[1]
        for j, b in enumerate(self.residues):
            self.shifted_ref[j] = self.ref[pl.ds(b, n), :]

    def at(self, offset, r0, c0, rows):
        b = offset % SUBLANES
        if b in self.residues:
            return self.shifted_ref[self.residues.index(b), pl.ds(offset - b + r0, rows), pl.ds(c0, LANES)]
        return self.ref[pl.ds(offset + r0, rows), pl.ds(c0, LANES)]


def _shifted_scratch(residues, tm, c):
    return pltpu.VMEM((len(residues), tm + HALO - SUBLANES, c), F32)


def _causal_offsets(k_taps):
    return [HALO - k_taps + 1 + k for k in range(k_taps)]


def _anticausal_offsets(k_taps):
    return [k_taps - 1 - k for k in range(k_taps)]


def _tap_chunk(src, w_ref, offsets, r0, c0, rows):
    acc = w_ref[0:1, pl.ds(c0, LANES)] * src.at(offsets[0], r0, c0, rows)
    for k in range(1, len(offsets)):
        acc = acc + w_ref[k:k + 1, pl.ds(c0, LANES)] * src.at(offsets[k], r0, c0, rows)
    return acc


def _conv_whole(ref, w_ref, offsets, tm):
    acc = w_ref[0:1, :] * ref[pl.ds(offsets[0], tm), :]
    for k in range(1, len(offsets)):
        acc = acc + w_ref[k:k + 1, :] * ref[pl.ds(offsets[k], tm), :]
    return acc


def _tap_grads_whole(dw_ref, dy, ref, offsets, tm):
    for k, off in enumerate(offsets):
        dw_ref[k:k + 1, :] += jnp.sum(dy * ref[pl.ds(off, tm), :], axis=0, keepdims=True)


def _conv_taps(src, w_ref, offsets, tm, emit):
    rows = min(ROW_CHUNK, tm)
    for c0 in range(0, w_ref.shape[1], LANES):
        for r0 in range(0, tm, rows):
            emit(r0, rows, c0, _tap_chunk(src, w_ref, offsets, r0, c0, rows))


def _tap_grads(dw_ref, dy_at, src, offsets, tm):
    rows = min(ROW_CHUNK // 2, tm)
    for c0 in range(0, dw_ref.shape[1], LANES):
        acc = [None] * len(offsets)
        for r0 in range(0, tm, rows):
            dy = dy_at(r0, rows, c0)
            for k, off in enumerate(offsets):
                part = (dy * src.at(off, r0, c0, rows)).reshape(rows // SUBLANES, SUBLANES, LANES).sum(axis=0)
                acc[k] = part if acc[k] is None else acc[k] + part
        for k in range(len(offsets)):
            dw_ref[k:k + 1, pl.ds(c0, LANES)] += jnp.sum(acc[k], axis=0, keepdims=True)


def _prev_halo(tm, col):
    return lambda i: (jnp.maximum(i * (tm // HALO) - 1, 0), col)


def _next_halo(tm, n_rows, col):
    return lambda i: (jnp.minimum((i + 1) * (tm // HALO), n_rows // HALO - 1), col)


def _norm_matmul(x, g, w, gs, name, comm=None):
    t, d = x.shape
    n_s, _, ns = w.shape
    n = n_s * ns
    tm = min(TM_MM, t)

    def body(x_ref, g_ref, w_ref, h_ref, y_ref):
        @pl.when(pl.program_id(1) == 0)
        def _():
            xf = x_ref[...]
            r = lax.rsqrt(jnp.mean(xf * xf, axis=-1, keepdims=True) + EPS)
            h_ref[...] = ((xf * r) * g_ref[...]).astype(BF)

        for s in range(gs):
            y_ref[:, s * ns:(s + 1) * ns] = _nn(h_ref[...], w_ref[s]).astype(BF)

    return _run(
        body, comm, name=name, grid=(t // tm, n_s // gs),
        in_specs=[pl.BlockSpec((tm, d), lambda i, j: (i, 0)),
                  pl.BlockSpec((1, d), lambda i, j: (0, 0)),
                  pl.BlockSpec((gs, d, ns), lambda i, j: (j, 0, 0))],
        out_specs=[pl.BlockSpec((tm, d), lambda i, j: (i, 0)),
                   pl.BlockSpec((tm, gs * ns), lambda i, j: (i, j))],
        out_shape=[jax.ShapeDtypeStruct((t, d), BF), jax.ShapeDtypeStruct((t, n), BF)],
        args=(x, g, w))


def _mem_kv(mem, g, w_kv, name):
    m, d = mem.shape
    n_s, _, ns = w_kv.shape

    def body(mem_ref, g_ref, w_ref, memn_ref, kv_ref):
        xf = mem_ref[...]
        r = lax.rsqrt(jnp.mean(xf * xf, axis=-1, keepdims=True) + EPS)
        memn = ((xf * r) * g_ref[...]).astype(BF)
        memn_ref[...] = memn
        for s in range(n_s):
            kv_ref[:, s * ns:(s + 1) * ns] = _nn(memn, w_ref[s]).astype(BF)

    return _pallas_call(
        body, name=name, grid=(1,),
        in_specs=[pl.BlockSpec((m, d), lambda i: (0, 0)),
                  pl.BlockSpec((1, d), lambda i: (0, 0)),
                  pl.BlockSpec((n_s, d, ns), lambda i: (0, 0, 0))],
        out_specs=[pl.BlockSpec((m, d), lambda i: (0, 0)),
                   pl.BlockSpec((m, 2 * d), lambda i: (0, 0))],
        out_shape=[jax.ShapeDtypeStruct((m, d), BF), jax.ShapeDtypeStruct((m, 2 * d), BF)],
        compiler_params=_params(1),
    )(mem, g, w_kv)


def _load_branch_inputs(i, proj_ref, gch_ref, vh_ref, u0h_ref, ugh_ref, xa_ref, xb_ref, d):
    gc = proj_ref[:, d:2 * d].astype(F32)
    v = proj_ref[:, 2 * d:3 * d].astype(F32)
    u0 = proj_ref[:, 3 * d:4 * d].astype(F32)
    ug = proj_ref[:, 4 * d:5 * d].astype(F32)
    keep = (i > 0).astype(F32)
    xa_ref[pl.ds(0, HALO), :] = gch_ref[...].astype(F32) * vh_ref[...].astype(F32) * keep
    xa_ref[pl.ds(HALO, gc.shape[0]), :] = gc * v
    xb_ref[pl.ds(0, HALO), :] = u0h_ref[...].astype(F32) * _sig(ugh_ref[...].astype(F32)) * keep
    xb_ref[pl.ds(HALO, gc.shape[0]), :] = u0 * _sig(ug)


def _softmax_rows(s):
    e = jnp.exp(s - jnp.max(s, axis=-1, keepdims=True))
    return e / jnp.sum(e, axis=-1, keepdims=True)


def _mixer_fwd(proj, x, kv, conv_a, conv_b, cbias, ln_g, ln_b, b_gate, wsq, l, name, comm=None):
    t, d = x.shape
    m = kv.shape[0]
    tm = min(TM_ROW, t)
    hd = d // N_HEADS
    scale = 1.0 / math.sqrt(hd)

    def body(proj_ref, gch_ref, vh_ref, u0h_ref, ugh_ref, x_ref, kv_ref, ca_w, cb_w, cbias_ref, lng_ref, lnb_ref,
             bg_ref, wa_ref, wb_ref, wc_ref, wo_ref,
             x1_ref, za_ref, zb_ref, o_ref, ya_ref, yb_ref, yc_ref, mg_ref, cb_ref, xa_ref, xb_ref, sb_ref):
        i = pl.program_id(0)
        _load_branch_inputs(i, proj_ref, gch_ref, vh_ref, u0h_ref, ugh_ref, xa_ref, xb_ref, d)
        xb = _Rows(xb_ref, sb_ref, ALL_RESIDUES)
        xb.shift()
        def put_za(r0, rows, c0, ca):
            gb = proj_ref[pl.ds(r0, rows), pl.ds(c0, LANES)].astype(F32)
            za_ref[pl.ds(r0, rows), pl.ds(c0, LANES)] = (gb * ca).astype(BF)

        _conv_taps(_Rows(xa_ref), ca_w, _causal_offsets(K_A), tm, put_za)
        ya = _nn(za_ref[...], wa_ref[...])
        ya_ref[...] = ya.astype(BF)

        def put_cb(r0, rows, c0, conv):
            cb_ref[pl.ds(r0, rows), pl.ds(c0, LANES)] = conv + cbias_ref[:, pl.ds(c0, LANES)]

        _conv_taps(xb, cb_w, _causal_offsets(K_B), tm, put_cb)
        cb = cb_ref[...]
        mu = jnp.mean(cb, axis=-1, keepdims=True)
        dlt = cb - mu
        rstd = lax.rsqrt(jnp.mean(dlt * dlt, axis=-1, keepdims=True) + EPS)
        lnb = (dlt * rstd) * lng_ref[...] + lnb_ref[...]
        zb = (lnb * _sig(lnb)).astype(BF)
        zb_ref[...] = zb
        yb = _nn(zb, wb_ref[...])
        yb_ref[...] = yb.astype(BF)
        for h in range(N_HEADS):
            qh = proj_ref[:, 5 * d + h * hd:5 * d + (h + 1) * hd]
            kh = kv_ref[:, h * hd:(h + 1) * hd]
            vh = kv_ref[:, d + h * hd:d + (h + 1) * hd]
            p = _softmax_rows(_nt(qh, kh) * scale)
            o_ref[:, h * hd:(h + 1) * hd] = _nn(p.astype(BF), vh).astype(BF)
        yc = _nn(o_ref[...], wc_ref[...])
        yc_ref[...] = yc.astype(BF)
        g0 = _sig(proj_ref[:, 6 * d:7 * d].astype(F32) + bg_ref[:, 0:d])
        g1 = _sig(proj_ref[:, 7 * d:8 * d].astype(F32) + bg_ref[:, d:2 * d])
        g2 = _sig(proj_ref[:, 8 * d:9 * d].astype(F32) + bg_ref[:, 2 * d:3 * d])
        mg = (g0 * ya + g1 * yb + g2 * yc).astype(BF)
        mg_ref[...] = mg
        x1_ref[...] = x_ref[...] + _nn(mg, wo_ref[...])

    row = lambda w_: pl.BlockSpec((tm, w_), lambda i: (i, 0))
    halo = lambda col: pl.BlockSpec((HALO, d), _prev_halo(tm, col))
    sq = lambda which: _resident((None, None, d, d), (l, which, 0, 0))
    act = jax.ShapeDtypeStruct((t, d), BF)
    return _run(
        body, comm, name=name, grid=(t // tm,),
        in_specs=[row(9 * d), halo(1), halo(2), halo(3), halo(4), row(d),
                  _resident((m, 2 * d), (0, 0)),
                  _resident((None, K_A, d), (l, 0, 0)), _resident((None, K_B, d), (l, 0, 0)),
                  _resident((1, d), (0, 0)), _resident((1, d), (0, 0)), _resident((1, d), (0, 0)),
                  _resident((1, 3 * d), (0, 0)), sq(0), sq(1), sq(2), sq(3)],
        out_specs=[row(d)] * 9,
        out_shape=[jax.ShapeDtypeStruct((t, d), F32)] + [act] * 7 + [jax.ShapeDtypeStruct((t, d), F32)],
        scratch_shapes=[pltpu.VMEM((HALO + tm, d), F32), pltpu.VMEM((HALO + tm, d), F32),
                        _shifted_scratch(ALL_RESIDUES, tm, d)],
        args=(proj, proj, proj, proj, proj, x, kv, conv_a, conv_b, cbias, ln_g, ln_b, b_gate, wsq, wsq, wsq, wsq))


def _ffn_down_fwd(up, x1, conv_f, w_down, l, name, comm=None):
    t, d = x1.shape
    f2 = up.shape[1]
    f = f2 // 2
    tm = min(TM_ROW, t)

    def body(up_ref, uph_ref, x1_ref, cw_ref, wd_ref, x2_ref, zf_ref, uc_ref, xx_ref):
        i = pl.program_id(0)
        xx_ref[pl.ds(0, HALO), :] = uph_ref[...].astype(F32) * (i > 0).astype(F32)
        xx_ref[pl.ds(HALO, tm), :] = up_ref[...].astype(F32)
        uc = _conv_whole(xx_ref, cw_ref, _causal_offsets(K_F), tm)
        uc_ref[...] = uc.astype(BF)
        gt = uc[:, 0:f]
        zf = (gt * _sig(gt) * uc[:, f:f2]).astype(BF)
        zf_ref[...] = zf
        x2_ref[...] = x1_ref[...] + _nn(zf, wd_ref[...])

    return _run(
        body, comm, name=name, grid=(t // tm,),
        in_specs=[pl.BlockSpec((tm, f2), lambda i: (i, 0)),
                  pl.BlockSpec((HALO, f2), _prev_halo(tm, 0)),
                  pl.BlockSpec((tm, d), lambda i: (i, 0)),
                  _resident((None, K_F, f2), (l, 0, 0)),
                  _resident((None, f, d), (l, 0, 0))],
        out_specs=[pl.BlockSpec((tm, d), lambda i: (i, 0)), pl.BlockSpec((tm, f), lambda i: (i, 0)),
                   pl.BlockSpec((tm, f2), lambda i: (i, 0))],
        out_shape=[jax.ShapeDtypeStruct((t, d), F32), jax.ShapeDtypeStruct((t, f), BF),
                   jax.ShapeDtypeStruct((t, f2), BF)],
        scratch_shapes=[pltpu.VMEM((HALO + tm, f2), F32)],
        args=(up, up, x1, conv_f, w_down))


def _final_loss(x, g, target, name):
    t, d = x.shape
    tm = min(2 * TM_ROW, t)

    def body(x_ref, g_ref, t_ref, dx_ref, loss_ref, dg_ref):
        @pl.when(pl.program_id(0) == 0)
        def _():
            loss_ref[...] = jnp.zeros_like(loss_ref)
            dg_ref[...] = jnp.zeros_like(dg_ref)

        xf = x_ref[...]
        r = lax.rsqrt(jnp.mean(xf * xf, axis=-1, keepdims=True) + EPS)
        xhat = xf * r
        err = xhat * g_ref[...] - t_ref[...]
        loss_ref[...] += (0.5 / d) * jnp.sum(err * err)
        dy = err * (1.0 / d)
        dg_ref[...] += jnp.sum(dy * xhat, axis=0, keepdims=True)
        dxh = dy * g_ref[...]
        dx_ref[...] = r * (dxh - xhat * jnp.mean(dxh * xhat, axis=-1, keepdims=True))

    return _pallas_call(
        body, name=name, grid=(t // tm,),
        in_specs=[pl.BlockSpec((tm, d), lambda i: (i, 0)), pl.BlockSpec((1, d), lambda i: (0, 0)),
                  pl.BlockSpec((tm, d), lambda i: (i, 0))],
        out_specs=[pl.BlockSpec((tm, d), lambda i: (i, 0)), pl.BlockSpec((8, 128), lambda i: (0, 0)),
                   pl.BlockSpec((1, d), lambda i: (0, 0))],
        out_shape=[jax.ShapeDtypeStruct((t, d), F32), jax.ShapeDtypeStruct((8, 128), F32),
                   jax.ShapeDtypeStruct((1, d), F32)],
        compiler_params=_params(1),
    )(x, g, target)


def _ffn_down_bwd(dx2, up, uc, w_down, l, name, comm=None):
    t, d = dx2.shape
    f2 = up.shape[1]
    f = f2 // 2
    tm = min(TM_ROW, t)

    def body(dx2_ref, up_ref, uph_ref, uc_ref, wd_ref, duc_ref, dx2b_ref, dcw_ref, xx_ref):
        i = pl.program_id(0)

        @pl.when(i == 0)
        def _():
            dcw_ref[...] = jnp.zeros_like(dcw_ref)

        xx_ref[pl.ds(0, HALO), :] = uph_ref[...].astype(F32) * (i > 0).astype(F32)
        xx_ref[pl.ds(HALO, tm), :] = up_ref[...].astype(F32)
        gt = uc_ref[:, 0:f].astype(F32)
        sg = _sig(gt)
        dx2b = dx2_ref[...].astype(BF)
        dx2b_ref[...] = dx2b
        dzf = _nt(dx2b, wd_ref[...])
        duc_ref[:, 0:f] = (dzf * uc_ref[:, f:f2].astype(F32) * (sg * (1.0 + gt * (1.0 - sg)))).astype(BF)
        duc_ref[:, f:f2] = (dzf * (gt * sg)).astype(BF)
        _tap_grads_whole(dcw_ref, duc_ref[...].astype(F32), xx_ref, _causal_offsets(K_F), tm)

    return _run(
        body, comm, name=name, grid=(t // tm,),
        in_specs=[pl.BlockSpec((tm, d), lambda i: (i, 0)),
                  pl.BlockSpec((tm, f2), lambda i: (i, 0)),
                  pl.BlockSpec((HALO, f2), _prev_halo(tm, 0)),
                  pl.BlockSpec((tm, f2), lambda i: (i, 0)),
                  _resident((None, f, d), (l, 0, 0))],
        out_specs=[pl.BlockSpec((tm, f2), lambda i: (i, 0)), pl.BlockSpec((tm, d), lambda i: (i, 0)),
                   pl.BlockSpec((K_F, f2), lambda i: (0, 0))],
        out_shape=[jax.ShapeDtypeStruct((t, f2), BF), jax.ShapeDtypeStruct((t, d), BF),
                   jax.ShapeDtypeStruct((K_F, f2), F32)],
        scratch_shapes=[pltpu.VMEM((HALO + tm, f2), F32)],
        args=(dx2, up, up, uc, w_down))


def _ffn_conv_bwd(duc, conv_f, l, name, comm=None):
    t, f2 = duc.shape
    tm = min(TM_ROW, t)
    n_t = t // tm

    def body(duc_ref, nxt_ref, cw_ref, dup_ref, yy_ref):
        i = pl.program_id(0)
        yy_ref[pl.ds(0, tm), :] = duc_ref[...].astype(F32)
        yy_ref[pl.ds(tm, HALO), :] = nxt_ref[...].astype(F32) * (i < n_t - 1).astype(F32)

        def put(r0, rows, c0, conv):
            dup_ref[pl.ds(r0, rows), pl.ds(c0, LANES)] = conv.astype(BF)

        _conv_taps(_Rows(yy_ref), cw_ref, _anticausal_offsets(K_F), tm, put)

    return _run(
        body, comm, name=name, grid=(n_t,),
        in_specs=[pl.BlockSpec((tm, f2), lambda i: (i, 0)),
                  pl.BlockSpec((HALO, f2), _next_halo(tm, t, 0)),
                  _resident((None, K_F, f2), (l, 0, 0))],
        out_specs=[pl.BlockSpec((tm, f2), lambda i: (i, 0))],
        out_shape=[jax.ShapeDtypeStruct((t, f2), BF)],
        scratch_shapes=[pltpu.VMEM((tm + HALO, f2), F32)],
        args=(duc, duc, conv_f))


def _nt_matmul_norm_bwd(dy, w, gs, x, g, dres, name, comm=None):
    t, n = dy.shape
    d = x.shape[1]
    tm = min(TM_MM // 2, t)
    n_s, _, ns = w.shape
    n_k, tk = n_s // gs, gs * ns

    def body(dy_ref, w_ref, x_ref, g_ref, dres_ref, dx_ref, dxb_ref, dg_ref, acc_ref):
        i, k = pl.program_id(0), pl.program_id(1)

        @pl.when((i == 0) & (k == 0))
        def _():
            dg_ref[...] = jnp.zeros_like(dg_ref)

        @pl.when(k == 0)
        def _():
            acc_ref[...] = jnp.zeros_like(acc_ref)

        part = _nt(dy_ref[:, 0:ns], w_ref[0])
        for s in range(1, gs):
            part = part + _nt(dy_ref[:, s * ns:(s + 1) * ns], w_ref[s])
        acc_ref[...] += part

        @pl.when(k == n_k - 1)
        def _():
            xf = x_ref[...]
            r = lax.rsqrt(jnp.mean(xf * xf, axis=-1, keepdims=True) + EPS)
            xhat = xf * r
            dh = acc_ref[...]
            dg_ref[...] += jnp.sum(dh * xhat, axis=0, keepdims=True)
            dxh = dh * g_ref[...]
            dx = dres_ref[...] + r * (dxh - xhat * jnp.mean(dxh * xhat, axis=-1, keepdims=True))
            dx_ref[...] = dx
            dxb_ref[...] = dx.astype(BF)

    return _run(
        body, comm, name=name, grid=(t // tm, n_k),
        in_specs=[pl.BlockSpec((tm, tk), lambda i, k: (i, k)),
                  pl.BlockSpec((gs, d, ns), lambda i, k: (k, 0, 0)),
                  pl.BlockSpec((tm, d), lambda i, k: (i, 0)),
                  pl.BlockSpec((1, d), lambda i, k: (0, 0)),
                  pl.BlockSpec((tm, d), lambda i, k: (i, 0))],
        out_specs=[pl.BlockSpec((tm, d), lambda i, k: (i, 0)), pl.BlockSpec((tm, d), lambda i, k: (i, 0)),
                   pl.BlockSpec((1, d), lambda i, k: (0, 0))],
        out_shape=[jax.ShapeDtypeStruct((t, d), F32), jax.ShapeDtypeStruct((t, d), BF),
                   jax.ShapeDtypeStruct((1, d), F32)],
        scratch_shapes=[pltpu.VMEM((tm, d), F32)],
        args=(dy, w, x, g, dres))


def _mixer_bwd(dx1b, proj, ya, yb, yc, cb, kv, conv_a, conv_b, ln_g, ln_b, b_gate, wsq, l, name, comm=None):
    t, d = cb.shape
    m = kv.shape[0]
    tm = min(TM_ROW, t)
    hd = d // N_HEADS
    scale = 1.0 / math.sqrt(hd)

    def body(dx1b_ref, proj_ref, gch_ref, vh_ref, u0h_ref, ugh_ref, ya_ref, yb_ref, yc_ref, cb_ref, kv_ref,
             ca_w, cb_w, lng_ref, lnb_ref, bg_ref, wa_ref, wb_ref, wc_ref, wo_ref,
             dpre_ref, dya_ref, dyb_ref, dyc_ref, dkv_ref, dbg_ref, dlng_ref, dlnb_ref, dcbias_ref, dcaw_ref,
             dcbw_ref, xa_ref, xb_ref, sa_ref, sb_ref):
        i = pl.program_id(0)

        @pl.when(i == 0)
        def _():
            for ref in (dkv_ref, dbg_ref, dlng_ref, dlnb_ref, dcbias_ref, dcaw_ref, dcbw_ref):
                ref[...] = jnp.zeros_like(ref)

        _load_branch_inputs(i, proj_ref, gch_ref, vh_ref, u0h_ref, ugh_ref, xa_ref, xb_ref, d)
        xa, xb = _Rows(xa_ref, sa_ref, SHORT_RESIDUES), _Rows(xb_ref, sb_ref, ALL_RESIDUES)
        xa.shift()
        xb.shift()
        dmg = _nt(dx1b_ref[...], wo_ref[...])
        ys = (ya_ref, yb_ref, yc_ref)
        dys = (dya_ref, dyb_ref, dyc_ref)
        for b in range(3):
            gate = _sig(proj_ref[:, (6 + b) * d:(7 + b) * d].astype(F32) + bg_ref[:, b * d:(b + 1) * d])
            dys[b][...] = (gate * dmg).astype(BF)
            dpg = dmg * ys[b][...].astype(F32) * gate * (1.0 - gate)
            dpre_ref[:, (6 + b) * d:(7 + b) * d] = dpg.astype(BF)
            dbg_ref[:, b * d:(b + 1) * d] += jnp.sum(dpg, axis=0, keepdims=True)
        gb = proj_ref[:, 0:d].astype(F32)
        dza = _nt(dya_ref[...], wa_ref[...])

        def put_dgb(r0, rows, c0, ca):
            dpre_ref[pl.ds(r0, rows), pl.ds(c0, LANES)] = (dza[r0:r0 + rows, c0:c0 + LANES] * ca).astype(BF)

        _conv_taps(xa, ca_w, _causal_offsets(K_A), tm, put_dgb)
        dpre_ref[:, d:2 * d] = (dza * gb).astype(BF)
        _tap_grads(dcaw_ref, lambda r0, rows, c0: dpre_ref[pl.ds(r0, rows), pl.ds(d + c0, LANES)].astype(F32),
                   xa, _causal_offsets(K_A), tm)
        dpre_ref[:, 2 * d:3 * d] = jnp.zeros((tm, d), BF)
        cbv = cb_ref[...]
        mu = jnp.mean(cbv, axis=-1, keepdims=True)
        dlt = cbv - mu
        rstd = lax.rsqrt(jnp.mean(dlt * dlt, axis=-1, keepdims=True) + EPS)
        xhat = dlt * rstd
        lnb = xhat * lng_ref[...] + lnb_ref[...]
        sg = _sig(lnb)
        dzb = _nt(dyb_ref[...], wb_ref[...])
        dl = dzb * (sg * (1.0 + lnb * (1.0 - sg)))
        dlng_ref[...] += jnp.sum(dl * xhat, axis=0, keepdims=True)
        dlnb_ref[...] += jnp.sum(dl, axis=0, keepdims=True)
        dxh = dl * lng_ref[...]
        dcb = rstd * (dxh - jnp.mean(dxh, axis=-1, keepdims=True)
                      - xhat * jnp.mean(dxh * xhat, axis=-1, keepdims=True))
        dcbias_ref[...] += jnp.sum(dcb, axis=0, keepdims=True)
        dpre_ref[:, 3 * d:4 * d] = dcb.astype(BF)
        _tap_grads(dcbw_ref, lambda r0, rows, c0: dpre_ref[pl.ds(r0, rows), pl.ds(3 * d + c0, LANES)].astype(F32),
                   xb, _causal_offsets(K_B), tm)
        dpre_ref[:, 4 * d:5 * d] = jnp.zeros((tm, d), BF)
        do = _nt(dyc_ref[...], wc_ref[...]).astype(BF)
        for h in range(N_HEADS):
            qh = proj_ref[:, 5 * d + h * hd:5 * d + (h + 1) * hd]
            kh = kv_ref[:, h * hd:(h + 1) * hd]
            vh = kv_ref[:, d + h * hd:d + (h + 1) * hd]
            doh = do[:, h * hd:(h + 1) * hd]
            p = _softmax_rows(_nt(qh, kh) * scale)
            dp = _nt(doh, vh)
            ds = (p * (dp - jnp.sum(dp * p, axis=-1, keepdims=True)) * scale).astype(BF)
            dpre_ref[:, 5 * d + h * hd:5 * d + (h + 1) * hd] = _nn(ds, kh).astype(BF)
            dkv_ref[:, h * hd:(h + 1) * hd] += _tn(ds, qh)
            dkv_ref[:, d + h * hd:d + (h + 1) * hd] += _tn(p.astype(BF), doh)

    row = lambda w_: pl.BlockSpec((tm, w_), lambda i: (i, 0))
    halo = lambda col: pl.BlockSpec((HALO, d), _prev_halo(tm, col))
    sq = lambda which: _resident((None, None, d, d), (l, which, 0, 0))
    acc = lambda r, c: pl.BlockSpec((r, c), lambda i: (0, 0))
    act = jax.ShapeDtypeStruct((t, d), BF)
    vec = lambda r, c: jax.ShapeDtypeStruct((r, c), F32)
    return _run(
        body, comm, name=name, grid=(t // tm,),
        in_specs=[row(d), row(9 * d), halo(1), halo(2), halo(3), halo(4), row(d), row(d), row(d), row(d),
                  _resident((m, 2 * d), (0, 0)),
                  _resident((None, K_A, d), (l, 0, 0)), _resident((None, K_B, d), (l, 0, 0)),
                  _resident((1, d), (0, 0)), _resident((1, d), (0, 0)), _resident((1, 3 * d), (0, 0)),
                  sq(0), sq(1), sq(2), sq(3)],
        out_specs=[row(9 * d), row(d), row(d), row(d), acc(m, 2 * d), acc(1, 3 * d), acc(1, d), acc(1, d),
                   acc(1, d), acc(K_A, d), acc(K_B, d)],
        out_shape=[jax.ShapeDtypeStruct((t, 9 * d), BF), act, act, act, vec(m, 2 * d), vec(1, 3 * d), vec(1, d),
                   vec(1, d), vec(1, d), vec(K_A, d), vec(K_B, d)],
        scratch_shapes=[pltpu.VMEM((HALO + tm, d), F32), pltpu.VMEM((HALO + tm, d), F32),
                        _shifted_scratch(SHORT_RESIDUES, tm, d), _shifted_scratch(ALL_RESIDUES, tm, d)],
        args=(dx1b, proj, proj, proj, proj, proj, ya, yb, yc, cb, kv, conv_a, conv_b, ln_g, ln_b, b_gate,
              wsq, wsq, wsq, wsq))


def _inproj_conv_bwd(dpre, proj, conv_a, conv_b, l, name, comm=None):
    t, d9 = dpre.shape
    d = d9 // 9
    tm = min(TM_ROW, t)
    n_t = t // tm

    def body(dpre_ref, nxa_ref, nxb_ref, proj_ref, ca_w, cb_w, dproj_ref, ya_ref, yb_ref, sb_ref):
        i = pl.program_id(0)
        keep = (i < n_t - 1).astype(F32)
        ya_ref[pl.ds(0, tm), :] = dpre_ref[:, d:2 * d].astype(F32)
        ya_ref[pl.ds(tm, HALO), :] = nxa_ref[...].astype(F32) * keep
        yb_ref[pl.ds(0, tm), :] = dpre_ref[:, 3 * d:4 * d].astype(F32)
        yb_ref[pl.ds(tm, HALO), :] = nxb_ref[...].astype(F32) * keep
        yb = _Rows(yb_ref, sb_ref, ALL_RESIDUES)
        yb.shift()
        dproj_ref[:, 0:d] = dpre_ref[:, 0:d]
        dproj_ref[:, 5 * d:9 * d] = dpre_ref[:, 5 * d:9 * d]
        def chunk(ref, block, r0, rows, c0):
            return ref.at[pl.ds(r0, rows), pl.ds(block * d + c0, LANES)]

        def put_a(r0, rows, c0, dcv):
            chunk(dproj_ref, 1, r0, rows, c0)[...] = (dcv * chunk(proj_ref, 2, r0, rows, c0)[...].astype(F32)).astype(BF)
            chunk(dproj_ref, 2, r0, rows, c0)[...] = (dcv * chunk(proj_ref, 1, r0, rows, c0)[...].astype(F32)).astype(BF)

        def put_b(r0, rows, c0, dub):
            sg = _sig(chunk(proj_ref, 4, r0, rows, c0)[...].astype(F32))
            u0 = chunk(proj_ref, 3, r0, rows, c0)[...].astype(F32)
            chunk(dproj_ref, 3, r0, rows, c0)[...] = (dub * sg).astype(BF)
            chunk(dproj_ref, 4, r0, rows, c0)[...] = (dub * u0 * sg * (1.0 - sg)).astype(BF)

        _conv_taps(_Rows(ya_ref), ca_w, _anticausal_offsets(K_A), tm, put_a)
        _conv_taps(yb, cb_w, _anticausal_offsets(K_B), tm, put_b)

    return _run(
        body, comm, name=name, grid=(n_t,),
        in_specs=[pl.BlockSpec((tm, d9), lambda i: (i, 0)),
                  pl.BlockSpec((HALO, d), _next_halo(tm, t, 1)),
                  pl.BlockSpec((HALO, d), _next_halo(tm, t, 3)),
                  pl.BlockSpec((tm, d9), lambda i: (i, 0)),
                  _resident((None, K_A, d), (l, 0, 0)), _resident((None, K_B, d), (l, 0, 0))],
        out_specs=[pl.BlockSpec((tm, d9), lambda i: (i, 0))],
        out_shape=[jax.ShapeDtypeStruct((t, d9), BF)],
        scratch_shapes=[pltpu.VMEM((tm + HALO, d), F32), pltpu.VMEM((tm + HALO, d), F32),
                        _shifted_scratch(ALL_RESIDUES, tm, d)],
        args=(dpre, dpre, dpre, proj, conv_a, conv_b))


def _mem_kv_bwd(dkv, memn, mem, g, w_kv, name):
    m, d = mem.shape
    n_s, _, ns = w_kv.shape

    def body(dkv_ref, memn_ref, mem_ref, g_ref, w_ref, dw_ref, dg_ref):
        dkvb = dkv_ref[...].astype(BF)
        dw_ref[...] = _tn(memn_ref[...], dkvb).astype(BF)
        dmemn = _nt(dkvb[:, 0:ns], w_ref[0])
        for s in range(1, n_s):
            dmemn = dmemn + _nt(dkvb[:, s * ns:(s + 1) * ns], w_ref[s])
        xf = mem_ref[...]
        r = lax.rsqrt(jnp.mean(xf * xf, axis=-1, keepdims=True) + EPS)
        dg_ref[...] = jnp.sum(dmemn * (xf * r), axis=0, keepdims=True)

    return _pallas_call(
        body, name=name, grid=(1,),
        in_specs=[pl.BlockSpec((m, 2 * d), lambda i: (0, 0)), pl.BlockSpec((m, d), lambda i: (0, 0)),
                  pl.BlockSpec((m, d), lambda i: (0, 0)), pl.BlockSpec((1, d), lambda i: (0, 0)),
                  pl.BlockSpec((n_s, d, ns), lambda i: (0, 0, 0))],
        out_specs=[pl.BlockSpec((d, 2 * d), lambda i: (0, 0)), pl.BlockSpec((1, d), lambda i: (0, 0))],
        out_shape=[jax.ShapeDtypeStruct((d, 2 * d), BF), jax.ShapeDtypeStruct((1, d), F32)],
        compiler_params=_params(1),
    )(dkv, memn, mem, g, w_kv)


def _dw_matmul(a, b, tn, name, comm=None):
    t, k = a.shape
    n = b.shape[1]
    tt = min(TT_DW, t)
    while tt * k > DW_LHS_ELEMS and tt % 2 == 0:
        tt //= 2
    n_s = t // tt

    def body(a_ref, b_ref, o_ref, acc_ref):
        s = pl.program_id(1)
        part = _tn(a_ref[...], b_ref[...])
        if n_s == 1:
            o_ref[...] = part.astype(BF)
            return

        @pl.when(s == 0)
        def _():
            acc_ref[...] = part

        @pl.when(s > 0)
        def _():
            acc_ref[...] += part

        @pl.when(s == n_s - 1)
        def _():
            o_ref[...] = acc_ref[...].astype(BF)

    return _run(
        body, comm, name=name, grid=(n // tn, n_s),
        in_specs=[pl.BlockSpec((tt, k), lambda j, s: (s, 0)), pl.BlockSpec((tt, tn), lambda j, s: (s, j))],
        out_specs=[pl.BlockSpec((k, tn), lambda j, s: (0, j))],
        out_shape=[jax.ShapeDtypeStruct((k, n), BF)],
        scratch_shapes=[pltpu.VMEM((k, tn) if n_s > 1 else (8, 128), F32)],
        args=(a, b))


class _GradReduce:
    def __init__(self, l, grads, core):
        self.l, self.core, self.names = l, core, tuple(grads)
        self.views = {n: _halves_view(n, g) for n, g in grads.items()}
        self.got, self.sums, self.landing, self.pieces = {}, {}, {}, {}

    def swap_program(self):
        return _swap_program([self.views[n] for n in self.names])

    def swapped(self, outs):
        self.got = dict(zip(self.names, outs))
        for n in self.names:
            self.sums[n], self.landing[n] = _add_halves(self.views[n], self.got[n], self.core, n in COL,
                                                        f"add_halves_{n}_{self.l}")

    def scatter_program(self, names=None):
        names = self.names if names is None else names
        return _scatter_program([self.sums[n] for n in names], [self.landing[n] for n in names], names)

    def scattered(self, outs, names=None):
        self.pieces.update(zip(self.names if names is None else names, outs))


def _step(x, mem, target, sh, conv_sh, small, core):
    depth = len(sh["w_in"])
    d = x.shape[1]
    f2 = sh["w_up"][0].shape[2] * N_CHIPS
    tn_dw_in = min(1024, d)
    tn_dw_up = f2 // 11 if f2 % (11 * 128) == 0 and f2 // 11 >= 128 else f2
    row = lambda v: v.reshape(1, -1)
    one = lambda a: a[None]

    w_in, cab, cf = _run_comm(_gather_program([sh["w_in"][0]] + conv_sh, ("col", "small", "small")), "gather_first")
    saved = []
    for l in range(depth):
        conv = dict(a=cab[l:l + 1, :K_A], b=cab[l:l + 1, K_A:], f=cf[l:l + 1])
        (h, proj), (wsq, w_kv) = _norm_matmul(
            x, row(small["norm_mix_g"][l]), w_in, 1, f"in_proj_{l}",
            _gather_program([sh["wsq"][l], sh["w_kv"][l]], ("row", "col")))
        memn, kv = _mem_kv(mem, row(small["norm_mem_g"][l]), w_kv, f"mem_kv_{l}")
        (x1, za, zb, o, ya, yb, yc, mg, cb), (w_up, w_down) = _mixer_fwd(
            proj, x, kv, conv["a"], conv["b"], row(small["conv_b_bias"][l]), row(small["ln_b_g"][l]),
            row(small["ln_b_b"][l]), row(small["b_gate"][l]), one(wsq), 0, f"mixer_fwd_{l}",
            _gather_program([sh["w_up"][l], sh["w_down"][l]], ("col", "row")))
        more = l + 1 < depth
        nxt = _gather_program([sh["w_in"][l + 1]], ("col",), (0, 2)) if more else None
        (h2, up), w_in_next = _norm_matmul(x1, row(small["norm_ffn_g"][l]), w_up, 2, f"up_proj_{l}", nxt)
        nxt = _gather_program(w_in_next, ("col",), (1, 2)) if more else None
        (x2, zf, uc), w_in_next = _ffn_down_fwd(up, x1, conv["f"], one(w_down), 0, f"ffn_down_fwd_{l}", nxt)
        saved.append(dict(x=x, x1=x1, memn=memn, kv=kv, h=h, proj=proj, za=za, zb=zb, o=o, ya=ya, yb=yb, yc=yc,
                          mg=mg, cb=cb, h2=h2, up=up, zf=zf, uc=uc, w_in=w_in, wsq=one(wsq), w_kv=w_kv,
                          w_up=w_up, w_down=one(w_down), conv=conv))
        x = x2
        w_in = w_in_next[0] if w_in_next else None
    dx, loss, dg_final = _final_loss(x, row(small["norm_final_g"]), target, "final_loss")
    sgrads, pieces = [None] * depth, [None] * depth
    above = None
    first, second, rest = ("w_down",), SQUARES, ("w_in", "w_kv", "w_up")
    for l in reversed(range(depth)):
        s = saved[l]
        conv = s["conv"]
        bottom = l == 0
        (duc, dx2b, dconv_f), got = _ffn_down_bwd(dx, s["up"], s["uc"], s["w_down"], 0, f"ffn_down_bwd_{l}",
                                                  above and above.swap_program())
        if above:
            above.swapped(got)
        (dup,), got = _ffn_conv_bwd(duc, conv["f"], 0, f"ffn_conv_bwd_{l}", above and above.scatter_program(first))
        if above:
            above.scattered(got, first)
        (dx1, dx1b, dg_ffn), got = _nt_matmul_norm_bwd(
            dup, s["w_up"], 2, s["x1"], row(small["norm_ffn_g"][l]), dx, f"up_proj_bwd_{l}",
            above and above.scatter_program(second))
        if above:
            above.scattered(got, second)
        grads = dict(w_up=_dw_matmul(s["h2"], dup, tn_dw_up, f"dw_up_{l}")[0][0],
                     w_down=_dw_matmul(s["zf"], dx2b, d, f"dw_down_{l}")[0][0])
        ffn = _GradReduce(l, grads, core) if bottom else None
        (dpre, dya, dyb, dyc, dkv, dbg, dlng, dlnb, dcbias, dconv_a, dconv_b), got = _mixer_bwd(
            dx1b, s["proj"], s["ya"], s["yb"], s["yc"], s["cb"], s["kv"], conv["a"], conv["b"],
            row(small["ln_b_g"][l]), row(small["ln_b_b"][l]), row(small["b_gate"][l]), s["wsq"], 0,
            f"mixer_bwd_{l}", _join([above and above.scatter_program(rest), ffn and ffn.swap_program()]))
        if above:
            above.scattered(got[:len(rest)], rest)
            pieces[above.l] = above.pieces
            got = got[len(rest):]
        if ffn:
            ffn.swapped(got)
        dw_kv, dg_mem = _mem_kv_bwd(dkv, s["memn"], mem, row(small["norm_mem_g"][l]), s["w_kv"], f"mem_kv_bwd_{l}")
        mix_grads = dict(w_a_out=_dw_matmul(s["za"], dya, d, f"dw_a_out_{l}")[0][0],
                         w_b_out=_dw_matmul(s["zb"], dyb, d, f"dw_b_out_{l}")[0][0],
                         w_att_out=_dw_matmul(s["o"], dyc, d, f"dw_att_out_{l}")[0][0],
                         w_o=_dw_matmul(s["mg"], dx1b, d, f"dw_o_{l}")[0][0], w_kv=dw_kv)
        mix = _GradReduce(l, mix_grads, core) if bottom else None
        (dproj,), got = _inproj_conv_bwd(dpre, s["proj"], conv["a"], conv["b"], 0, f"inproj_conv_bwd_{l}",
                                         _join([ffn and ffn.scatter_program(), mix and mix.swap_program()]))
        if bottom:
            ffn.scattered(got[:len(ffn.names)])
            mix.swapped(got[len(ffn.names):])
        in_grads = dict(w_in=_dw_matmul(s["h"], dproj, tn_dw_in, f"dw_in_{l}")[0][0])
        inp = _GradReduce(l, in_grads, core) if bottom else None
        (dx0, _, dg_mix), got = _nt_matmul_norm_bwd(
            dproj, s["w_in"], 2, s["x"], row(small["norm_mix_g"][l]), dx1, f"in_proj_bwd_{l}",
            _join([mix and mix.scatter_program(), inp and inp.swap_program()]))
        if bottom:
            mix.scattered(got[:len(mix.names)])
            inp.swapped(got[len(mix.names):])
            inp.scattered(_run_comm(inp.scatter_program(), f"scatter_w_in_{l}"))
            pieces[l] = {**ffn.pieces, **mix.pieces, **inp.pieces}
        else:
            above = _GradReduce(l, {**grads, **mix_grads, **in_grads}, core)
        sgrads[l] = dict(norm_mix_g=dg_mix, norm_mem_g=dg_mem, b_gate=dbg, conv_b_bias=dcbias, ln_b_g=dlng,
                         ln_b_b=dlnb, norm_ffn_g=dg_ffn, conv_a_w=dconv_a, conv_b_w=dconv_b, conv_ffn_w=dconv_f)
        dx = dx0
    return loss, dx, sgrads, dg_final, pieces


BIG = ("w_in", "w_a_out", "w_b_out", "w_att_out", "w_o", "w_kv", "w_up", "w_down")
COL = ("w_in", "w_kv", "w_up")
SQUARES = ("w_a_out", "w_b_out", "w_att_out", "w_o")
ANY = pl.BlockSpec(memory_space=pl.ANY)


def _place():
    x, y, c = lax.axis_index("x"), lax.axis_index("y"), lax.axis_index("c")
    chips = [(1 - x, y), (x, 1 - y), (1 - x, 1 - y)]
    return x, y, c, 2 * x + y, chips


def _remote(src, dst, send_sem, recv_sem, dev):
    return pltpu.make_async_remote_copy(src_ref=src, dst_ref=dst, send_sem=send_sem, recv_sem=recv_sem,
                                        device_id=dev, device_id_type=MESH)


class _Striped:
    def __init__(self, src, dst, make):
        rows = src.shape[-2]
        unit = 8 * (4 // jnp.dtype(src.dtype).itemsize)
        n = max(k for k in range(1, MAX_STRIPES + 1) if rows % (unit * k) == 0) if rows % unit == 0 else 1
        q = rows // n
        self.parts = [make(_window(src, pl.ds(i * q, q), slice(None)), _window(dst, pl.ds(i * q, q), slice(None)))
                      for i in range(n)]
        self.whole = make(src, dst)

    def start(self):
        for p in self.parts:
            p.start()

    def wait(self):
        self.whole.wait()

    def wait_send(self):
        self.whole.wait_send()

    def wait_recv(self):
        self.whole.wait_recv()


def _far(src, dst, send_sem, recv_sem, dev):
    return _Striped(src, dst, lambda s, d: _remote(s, d, send_sem, recv_sem, dev))


def _near(src, dst, sem):
    return _Striped(src, dst, lambda s, d: pltpu.make_async_copy(s, d, sem))


def _window(ref, rows, cols):
    return ref.at[(slice(None),) * (len(ref.shape) - 2) + (rows, cols)]


def _row_tile(rows, cols, unit=16, limit=1 << 20):
    best = unit
    for tr in range(unit, rows + 1, unit):
        if rows % tr == 0 and tr * cols <= limit:
            best = tr
    return best


def _cast_place(ws, l, kind, chip, name):
    _, k, n = ws[0].shape
    if kind == "col":
        shape, spec = (N_CHIPS, k, n), pl.BlockSpec((None, k, n), lambda i, c: (c[0], 0, 0))
    elif len(ws) == 1:
        shape, spec = (N_CHIPS * k, n), pl.BlockSpec((k, n), lambda i, c: (c[0], 0))
    else:
        shape, spec = (len(ws), N_CHIPS * k, n), pl.BlockSpec((len(ws), k, n), lambda i, c: (0, c[0], 0))

    def body(c_ref, *refs):
        o_ref = refs[-1]
        if len(ws) == 1 or kind == "col":
            o_ref[...] = refs[0][...].astype(BF)
        else:
            for i in range(len(ws)):
                o_ref[i] = refs[i][...].astype(BF)

    return _pallas_call(
        body, name=name,
        grid_spec=pltpu.PrefetchScalarGridSpec(
            num_scalar_prefetch=1, grid=(1,),
            in_specs=[pl.BlockSpec((None, k, n), lambda i, c: (l, 0, 0))] * len(ws), out_specs=spec),
        out_shape=jax.ShapeDtypeStruct(shape, BF),
        compiler_params=_params(1),
    )(chip, *ws)


def _gather_program(arrays, kinds, part=(0, 1)):
    n_t = len(arrays)
    index, count = part

    def shard_rows(f, kind):
        return f.shape[-2] if kind == "col" else f.shape[-2] // N_CHIPS

    def run(phase, ins, full, sems):
        ici_send, ici_recv, sib_send, sib_recv, loc_sem = sems
        x, y, c, me, chips = _place()
        sibling = (x, y, 1 - c)

        def part_of(i, chip, half):
            f, kind = full[i], kinds[i]
            if kind == "small":
                cols = ins[i].shape[-1]
                return _window(f, slice(None), pl.ds(pl.multiple_of(chip * cols, 128), cols))
            rows = shard_rows(f, kind)
            r = rows // (2 * count)
            at = (half * count + index) * r
            if kind == "col":
                return f.at[chip, pl.ds(pl.multiple_of(at, 16), r), :]
            return _window(f, pl.ds(pl.multiple_of(chip * rows + at, 16), r), slice(None))

        src_part = lambda i, half: ins[i] if kinds[i] == "small" else part_of(i, me, half)
        dst_part = part_of
        local = [_near(ins[i], part_of(i, me, c), loc_sem.at[i]) for i in range(n_t) if kinds[i] == "small"]
        sends = []
        for i in range(n_t):
            for j, chip in enumerate(chips):
                sends.append(_far(src_part(i, c), dst_part(i, me, c), ici_send.at[3 * i + j],
                                  ici_recv.at[3 * i + j], (*chip, c)))
        if phase == "start":
            for cp in local + sends:
                cp.start()
            return
        passed = []
        for i in range(n_t):
            for j, chip in enumerate(chips):
                k = 2 * chip[0] + chip[1]
                landed = dst_part(i, k, c)
                if phase == "forward":
                    _remote(landed, landed, ici_send.at[3 * i + j], ici_recv.at[3 * i + j], (*chip, c)).wait_recv()
                if kinds[i] != "small":
                    passed.append(_far(landed, landed, sib_send.at[3 * i + j], sib_recv.at[3 * i + j], sibling))
                    if phase == "forward":
                        passed[-1].start()
        if phase == "forward":
            return
        for i in range(n_t):
            if kinds[i] == "small":
                continue
            for j, chip in enumerate(chips):
                k = 2 * chip[0] + chip[1]
                other = dst_part(i, k, 1 - c)
                _remote(other, other, sib_send.at[3 * i + j], sib_recv.at[3 * i + j], sibling).wait_recv()
        for cp in sends + passed:
            cp.wait_send()
        for cp in local:
            cp.wait()

    def out_shape(a, kind):
        shp = a.shape[:-1] + (a.shape[-1] * N_CHIPS,) if kind == "small" else a.shape
        return jax.ShapeDtypeStruct(shp, a.dtype)

    outs = [out_shape(a, k) for a, k in zip(arrays, kinds)]
    sems = [pltpu.SemaphoreType.DMA((3 * n_t,))] * 4 + [pltpu.SemaphoreType.DMA((n_t,))]
    return _Comm(arrays, outs, sems, run, {i: i for i in range(n_t) if kinds[i] != "small"})


def _all_reduce_small(part, name):
    r, n = part.shape

    def body(in_ref, out_ref, gather_ref, send_sems, recv_sems):
        x, y, c, _, _ = _place()
        me = 4 * x + 2 * y + c
        gather_ref[me] = in_ref[...]
        sends = []
        for k in range(1, N_DEV):
            peer = (me + k) % N_DEV
            sends.append(_remote(in_ref, gather_ref.at[me], send_sems.at[k - 1], recv_sems.at[k - 1],
                                 (peer // 4, (peer // 2) % 2, peer % 2)))
        for cp in sends:
            cp.start()
        for k in range(1, N_DEV):
            origin = (me + N_DEV - k) % N_DEV
            _remote(in_ref, gather_ref.at[origin], send_sems.at[k - 1], recv_sems.at[k - 1],
                    (x, y, c)).wait_recv()
        for cp in sends:
            cp.wait_send()
        total = gather_ref[0]
        for dev in range(1, N_DEV):
            total = total + gather_ref[dev]
        out_ref[...] = total

    vm = pl.BlockSpec(memory_space=pltpu.VMEM)
    return _pallas_call(
        body, name=name, in_specs=[vm], out_specs=vm, out_shape=jax.ShapeDtypeStruct((r, n), F32),
        scratch_shapes=[pltpu.VMEM((N_DEV, r, n), F32), pltpu.SemaphoreType.DMA((N_DEV - 1,)),
                        pltpu.SemaphoreType.DMA((N_DEV - 1,))],
        compiler_params=pltpu.CompilerParams(vmem_limit_bytes=VMEM_LIMIT),
    )(part)


def _halves_view(name, dw):
    k, n = dw.shape
    s = 1 if name in COL else N_CHIPS
    return dw.reshape(s, 2, k // (2 * s), n)


def _swap_program(views):
    n_t = len(views)

    def run(phase, src, dst, sems):
        send_sems, recv_sems = sems
        x, y, c, _, _ = _place()
        copies = [_far(src[i].at[:, 1 - c], dst[i], send_sems.at[i], recv_sems.at[i], (x, y, 1 - c))
                  for i in range(n_t)]
        for cp in copies:
            if phase == "start":
                cp.start()
            elif phase == "finish":
                cp.wait()

    outs = [jax.ShapeDtypeStruct((v.shape[0],) + v.shape[2:], v.dtype) for v in views]
    sems = [pltpu.SemaphoreType.DMA((n_t,)), pltpu.SemaphoreType.DMA((n_t,))]
    return _Comm(views, outs, sems, run)


def _add_halves(view, got, place, col, name):
    s, _, r, n = view.shape
    if col:
        cw = n // N_CHIPS
        tr = _row_tile(r, cw)
        grid = (r // tr, N_CHIPS)
        in_specs = [pl.BlockSpec((None, None, tr, cw), lambda j, q, p: (0, p[0], j, q)),
                    pl.BlockSpec((None, tr, cw), lambda j, q, p: (0, j, q))]
        out_specs = [pl.BlockSpec((None, tr, cw), lambda j, q, p: (0, j, q)),
                     pl.BlockSpec((None, None, tr, cw), lambda j, q, p: (p[0], p[1], j, 0))]
    else:
        cw = n
        tr = _row_tile(r, n)
        grid = (s, r // tr)
        in_specs = [pl.BlockSpec((None, None, tr, n), lambda i, j, p: (i, p[0], j, 0)),
                    pl.BlockSpec((None, tr, n), lambda i, j, p: (i, j, 0))]
        out_specs = [pl.BlockSpec((None, tr, n), lambda i, j, p: (i, j, 0)),
                     pl.BlockSpec((None, None, tr, n), lambda i, j, p: (p[0], i, j, 0))]

    def body(p_ref, a_ref, b_ref, o_ref, z_ref):
        total = (a_ref[...].astype(F32) + b_ref[...].astype(F32)).astype(BF)
        o_ref[...] = total
        if col:
            @pl.when(pl.program_id(1) == p_ref[1])
            def _():
                z_ref[...] = total
        else:
            z_ref[...] = total

    return _pallas_call(
        body, name=name,
        grid_spec=pltpu.PrefetchScalarGridSpec(num_scalar_prefetch=1, grid=grid, in_specs=in_specs,
                                               out_specs=out_specs),
        out_shape=[jax.ShapeDtypeStruct((s, r, n), BF), jax.ShapeDtypeStruct((2, N_CHIPS, r, cw), BF)],
        compiler_params=_params(2),
    )(place, view, got)


def _scatter_program(sums, landing, names):
    n_t = len(sums)

    def run(phase, src, dst, sems):
        ici_send, ici_recv, sib_send, sib_recv = sems
        x, y, c, me, chips = _place()
        sibling = (x, y, 1 - c)

        def piece(i, chip):
            if names[i] in COL:
                cw = src[i].shape[2] // N_CHIPS
                return src[i].at[0, :, pl.ds(pl.multiple_of(chip * cw, 128), cw)]
            return src[i].at[chip]

        local = []
        sends = []
        for i in range(n_t):
            sends.append(_far(piece(i, me), dst[i].at[c, me], sib_send.at[4 * i + 3], sib_recv.at[4 * i + 3],
                              sibling))
            for j, chip in enumerate(chips):
                k = 2 * chip[0] + chip[1]
                sends.append(_far(piece(i, k), dst[i].at[c, me], ici_send.at[3 * i + j], ici_recv.at[3 * i + j],
                                  (*chip, c)))
        if phase == "start":
            for cp in local + sends:
                cp.start()
            return
        passed = []
        for i in range(n_t):
            for j, chip in enumerate(chips):
                k = 2 * chip[0] + chip[1]
                landed = dst[i].at[c, k]
                if phase == "forward":
                    _remote(landed, landed, ici_send.at[3 * i + j], ici_recv.at[3 * i + j], (*chip, c)).wait_recv()
                passed.append(_far(landed, landed, sib_send.at[4 * i + j], sib_recv.at[4 * i + j], sibling))
                if phase == "forward":
                    passed[-1].start()
        if phase == "forward":
            return
        for i in range(n_t):
            other = dst[i].at[1 - c, me]
            _remote(other, other, sib_send.at[4 * i + 3], sib_recv.at[4 * i + 3], sibling).wait_recv()
            for j, chip in enumerate(chips):
                k = 2 * chip[0] + chip[1]
                other = dst[i].at[1 - c, k]
                _remote(other, other, sib_send.at[4 * i + j], sib_recv.at[4 * i + j], sibling).wait_recv()
        for cp in sends + passed:
            cp.wait_send()
        for cp in local:
            cp.wait()

    outs = [jax.ShapeDtypeStruct(z.shape, z.dtype) for z in landing]
    sems = [pltpu.SemaphoreType.DMA((3 * n_t,))] * 2 + [pltpu.SemaphoreType.DMA((4 * n_t,))] * 2
    return _Comm(list(sums) + list(landing), outs, sems, run, {n_t + i: i for i in range(n_t)})


def _adamw(w, g, m, v):
    m = ADAM_B1 * m + (1.0 - ADAM_B1) * g
    v = ADAM_B2 * v + (1.0 - ADAM_B2) * (g * g)
    m_hat = m / (1.0 - ADAM_B1 ** ADAM_STEP)
    v_hat = v / (1.0 - ADAM_B2 ** ADAM_STEP)
    return -ADAM_LR * (m_hat / (jnp.sqrt(v_hat) + ADAM_EPS) + ADAM_WD * w), m, v


def _adam_shard(pieces, w, m, v, l, prev, name):
    depth, rows, cw = w.shape
    hr = rows // 2
    tr = _row_tile(hr, cw, limit=1 << 18)
    n_i = hr // tr

    def body(*refs):
        z_ref, w_ref, m_ref, v_ref = refs[:4]
        g_ref, d_ref, nm_ref, nv_ref = refs[-4:]
        g = z_ref[0].astype(F32)
        for k in range(1, N_CHIPS):
            g = g + z_ref[k].astype(F32)
        g_ref[...] = g
        d_ref[...], nm_ref[...], nv_ref[...] = _adamw(w_ref[...], g, m_ref[...], v_ref[...])

    par = pl.BlockSpec((None, tr, cw), lambda h, i: (l, h * n_i + i, 0))
    out = jax.ShapeDtypeStruct((depth, rows, cw), F32)
    extra = [] if prev is None else list(prev)
    return _pallas_call(
        body, name=name, grid=(2, n_i),
        in_specs=[pl.BlockSpec((None, N_CHIPS, tr, cw), lambda h, i: (h, 0, i, 0)), par, par, par] + [ANY] * len(extra),
        out_specs=[par] * 4, out_shape=[out] * 4,
        input_output_aliases={4 + k: k for k in range(len(extra))},
        compiler_params=_params(2),
    )(pieces, w, m, v, *extra)


def _adam_small(g, w, m, v, name):
    def body(g_ref, w_ref, m_ref, v_ref, d_ref, nm_ref, nv_ref):
        d_ref[...], nm_ref[...], nv_ref[...] = _adamw(w_ref[...], g_ref[...], m_ref[...], v_ref[...])

    vm = pl.BlockSpec(memory_space=pltpu.VMEM)
    out = jax.ShapeDtypeStruct(g.shape, F32)
    return _pallas_call(body, name=name, in_specs=[vm] * 4, out_specs=[vm] * 3, out_shape=[out] * 3)(g, w, m, v)


WEIGHTS = ("norm_mix_g", "norm_mem_g", "w_in", "b_gate", "conv_a_w", "w_a_out", "conv_b_w", "conv_b_bias", "ln_b_g",
           "ln_b_b", "w_b_out", "w_kv", "w_att_out", "w_o", "norm_ffn_g", "w_up", "conv_ffn_w", "w_down",
           "norm_final_g")
REPLICATED = ("norm_mix_g", "norm_mem_g", "b_gate", "conv_b_bias", "ln_b_g", "ln_b_b", "norm_ffn_g")
CONVS = ("conv_a_w", "conv_b_w", "conv_ffn_w")
PACK_WIDTH = 1024


def _pack(arrays):
    flat = jnp.concatenate([a.reshape(-1) for a in arrays])
    size = -(-flat.shape[0] // (8 * PACK_WIDTH)) * (8 * PACK_WIDTH)
    return jnp.pad(flat, (0, size - flat.shape[0])).reshape(-1, PACK_WIDTH)


def _unpack(packed, shapes):
    flat = packed.reshape(-1)
    out, at = [], 0
    for shp in shapes:
        n = math.prod(shp)
        out.append(flat[at:at + n].reshape(shp))
        at += n
    return out


def kernel(x, mem, norm_mix_g, norm_mem_g, w_in, b_gate, conv_a_w, w_a_out, conv_b_w, conv_b_bias, ln_b_g, ln_b_b, w_b_out, w_kv, w_att_out, w_o, norm_ffn_g, w_up, conv_ffn_w, w_down, norm_final_g, loss_target, m_norm_mix_g, m_norm_mem_g, m_w_in, m_b_gate, m_conv_a_w, m_w_a_out, m_conv_b_w, m_conv_b_bias, m_ln_b_g, m_ln_b_b, m_w_b_out, m_w_kv, m_w_att_out, m_w_o, m_norm_ffn_g, m_w_up, m_conv_ffn_w, m_w_down, m_norm_final_g, v_norm_mix_g, v_norm_mem_g, v_w_in, v_b_gate, v_conv_a_w, v_w_a_out, v_conv_b_w, v_conv_b_bias, v_ln_b_g, v_ln_b_b, v_w_b_out, v_w_kv, v_w_att_out, v_w_o, v_norm_ffn_g, v_w_up, v_conv_ffn_w, v_w_down, v_norm_final_g):
    w = dict(norm_mix_g=norm_mix_g, norm_mem_g=norm_mem_g, w_in=w_in, b_gate=b_gate, conv_a_w=conv_a_w,
             w_a_out=w_a_out, conv_b_w=conv_b_w, conv_b_bias=conv_b_bias, ln_b_g=ln_b_g, ln_b_b=ln_b_b,
             w_b_out=w_b_out, w_kv=w_kv, w_att_out=w_att_out, w_o=w_o, norm_ffn_g=norm_ffn_g, w_up=w_up,
             conv_ffn_w=conv_ffn_w, w_down=w_down, norm_final_g=norm_final_g)
    mom = dict(norm_mix_g=m_norm_mix_g, norm_mem_g=m_norm_mem_g, w_in=m_w_in, b_gate=m_b_gate, conv_a_w=m_conv_a_w,
               w_a_out=m_w_a_out, conv_b_w=m_conv_b_w, conv_b_bias=m_conv_b_bias, ln_b_g=m_ln_b_g, ln_b_b=m_ln_b_b,
               w_b_out=m_w_b_out, w_kv=m_w_kv, w_att_out=m_w_att_out, w_o=m_w_o, norm_ffn_g=m_norm_ffn_g,
               w_up=m_w_up, conv_ffn_w=m_conv_ffn_w, w_down=m_w_down, norm_final_g=m_norm_final_g)
    var = dict(norm_mix_g=v_norm_mix_g, norm_mem_g=v_norm_mem_g, w_in=v_w_in, b_gate=v_b_gate, conv_a_w=v_conv_a_w,
               w_a_out=v_w_a_out, conv_b_w=v_conv_b_w, conv_b_bias=v_conv_b_bias, ln_b_g=v_ln_b_g, ln_b_b=v_ln_b_b,
               w_b_out=v_w_b_out, w_kv=v_w_kv, w_att_out=v_w_att_out, w_o=v_w_o, norm_ffn_g=v_norm_ffn_g,
               w_up=v_w_up, conv_ffn_w=v_conv_ffn_w, w_down=v_w_down, norm_final_g=v_norm_final_g)
    depth = w_in.shape[0]
    chip = 2 * lax.axis_index("x") + lax.axis_index("y")
    core = jnp.stack([lax.axis_index("c"), chip]).astype(jnp.int32)

    chip1 = chip.astype(jnp.int32).reshape(1)
    layers = range(depth)
    sh = dict(w_in=[_cast_place([w_in], l, "col", chip1, f"cast_w_in_{l}") for l in layers],
              wsq=[_cast_place([w[n] for n in SQUARES], l, "row", chip1, f"cast_squares_{l}") for l in layers],
              w_kv=[_cast_place([w_kv], l, "col", chip1, f"cast_w_kv_{l}") for l in layers],
              w_up=[_cast_place([w_up], l, "col", chip1, f"cast_w_up_{l}") for l in layers],
              w_down=[_cast_place([w_down], l, "row", chip1, f"cast_w_down_{l}") for l in layers])
    conv_sh = [jnp.concatenate([conv_a_w, conv_b_w], axis=1), conv_ffn_w]
    small = {n: w[n] for n in REPLICATED + ("norm_final_g",)}

    loss, dx, sgrads, dg_final, pieces = _step(x[0], mem[0], loss_target[0], sh, conv_sh, small, core)

    res = {n: None for n in BIG}
    for l in reversed(range(depth)):
        for n in BIG:
            res[n] = _adam_shard(pieces[l][n], w[n], mom[n], var[n], l, res[n], f"adam_{n}_{l}")

    per_layer = REPLICATED + CONVS
    parts = [sgrads[l][n] for l in range(depth) for n in per_layer] + [dg_final]
    total = _all_reduce_small(_pack(parts), "all_reduce_small")
    shapes = [sgrads[l][n].shape for l in range(depth) for n in per_layer] + [dg_final.shape]
    summed = _unpack(total, shapes)
    g_small = {}
    for k, n in enumerate(per_layer):
        full = jnp.stack([summed[l * len(per_layer) + k] for l in range(depth)])
        if n in CONVS:
            cols = w[n].shape[-1]
            full = lax.dynamic_slice_in_dim(full, chip * cols, cols, axis=2)
        g_small[n] = full.reshape(w[n].shape)
    g_small["norm_final_g"] = summed[-1].reshape(norm_final_g.shape)
    names = per_layer + ("norm_final_g",)
    d_p, m_p, v_p = _adam_small(_pack([g_small[n] for n in names]), _pack([w[n] for n in names]),
                                _pack([mom[n] for n in names]), _pack([var[n] for n in names]), "adam_small")
    shp = [w[n].shape for n in names]
    for n, dl, nm, nv in zip(names, _unpack(d_p, shp), _unpack(m_p, shp), _unpack(v_p, shp)):
        res[n] = (g_small[n], dl, nm, nv)

    loss = lax.psum(loss[0, 0], ("x", "y", "c"))
    return (loss, dx.reshape(x.shape), *[res[n][0] for n in WEIGHTS], *[res[n][1] for n in WEIGHTS],
            *[res[n][2] for n in WEIGHTS], *[res[n][3] for n in WEIGHTS])
```

```python
import functools
import math

import jax
import jax.numpy as jnp
from jax import lax
from jax.experimental import pallas as pl
from jax.experimental.pallas import tpu as pltpu

F32 = jnp.float32
BF = jnp.bfloat16
EPS = 1e-6
N_HEADS = 4
K_A, K_B, K_F = 3, 31, 3
ADAM_LR, ADAM_B1, ADAM_B2, ADAM_EPS, ADAM_WD, ADAM_STEP = 0.001, 0.9, 0.999, 1e-08, 0.01, 10
N_CHIPS = 4
N_DEV = 8
HALO = 32
MAX_STRIPES = 8
TM_ROW = 256
TM_MM = 1024
TT_DW = 4096
DW_LHS_ELEMS = 4 * 1024 * 1024
TR_EW = 128
VMEM_LIMIT = 56 * 1024 * 1024
MESH = pl.DeviceIdType.MESH
_pallas_call = pl.pallas_call


def _params(n_axes):
    return pltpu.CompilerParams(dimension_semantics=("arbitrary",) * n_axes, vmem_limit_bytes=VMEM_LIMIT)


def _resident(shape, index):
    return pl.BlockSpec(shape, lambda *_: index, pipeline_mode=pl.Buffered(1))


def _sig(x):
    return 1.0 / (1.0 + jnp.exp(-x))


def _nt(a, b):
    return lax.dot_general(a, b, (((1,), (1,)), ((), ())), preferred_element_type=F32)


def _tn(a, b):
    return lax.dot_general(a, b, (((0,), (0,)), ((), ())), preferred_element_type=F32)


def _nn(a, b):
    return jnp.dot(a, b, preferred_element_type=F32)


class _Comm:
    def __init__(self, inputs, out_shape, scratch, run, aliases=None):
        self.inputs, self.out_shape, self.scratch, self.run = list(inputs), list(out_shape), list(scratch), run
        self.aliases = dict(aliases or {})


def _join(programs):
    programs = [p for p in programs if p is not None]
    if not programs:
        return None

    def split(seq, counts):
        parts, at = [], 0
        for n in counts:
            parts.append(seq[at:at + n])
            at += n
        return parts

    n_in = [len(p.inputs) for p in programs]
    n_out = [len(p.out_shape) for p in programs]
    n_s = [len(p.scratch) for p in programs]

    def run(phase, ins, outs, sems):
        for p, i, o, s in zip(programs, split(ins, n_in), split(outs, n_out), split(sems, n_s)):
            p.run(phase, i, o, s)

    aliases, in_at, out_at = {}, 0, 0
    for p, i, o in zip(programs, n_in, n_out):
        aliases.update({in_at + a: out_at + b for a, b in p.aliases.items()})
        in_at, out_at = in_at + i, out_at + o
    return _Comm([a for p in programs for a in p.inputs], [a for p in programs for a in p.out_shape],
                 [a for p in programs for a in p.scratch], run, aliases)


def _run(body, comm, *, name, grid, in_specs, out_specs, out_shape, args, scratch_shapes=()):
    n_axes = len(grid)
    if comm is None:
        outs = _pallas_call(body, name=name, grid=grid, in_specs=list(in_specs), out_specs=list(out_specs),
                            out_shape=list(out_shape), scratch_shapes=list(scratch_shapes),
                            compiler_params=_params(n_axes))(*args)
        return list(outs), []
    counts = (len(in_specs), len(comm.inputs), len(out_specs), len(comm.out_shape), len(scratch_shapes),
              len(comm.scratch))

    def hosted(*refs):
        parts, at = [], 0
        for n in counts:
            parts.append(refs[at:at + n])
            at += n
        ins, c_ins, outs, c_outs, scr, c_sems = parts
        step = pl.program_id(0)
        for a in range(1, n_axes):
            step = step * grid[a] + pl.program_id(a)
        total = math.prod(grid)
        late = max(0, total - 1 - max(1, total // 4))

        @pl.when(step == 0)
        def _():
            comm.run("start", c_ins, c_outs, c_sems)

        body(*ins, *outs, *scr)

        @pl.when(step == late)
        def _():
            comm.run("forward", c_ins, c_outs, c_sems)

        @pl.when(step == total - 1)
        def _():
            comm.run("finish", c_ins, c_outs, c_sems)

    any_spec = pl.BlockSpec(memory_space=pl.ANY)
    res = _pallas_call(
        hosted, name=name, grid=grid, in_specs=list(in_specs) + [any_spec] * counts[1],
        out_specs=list(out_specs) + [any_spec] * counts[3], out_shape=list(out_shape) + comm.out_shape,
        scratch_shapes=list(scratch_shapes) + comm.scratch, compiler_params=_params(n_axes),
        input_output_aliases={counts[0] + a: counts[2] + b for a, b in comm.aliases.items()},
    )(*args, *comm.inputs)
    return list(res[:counts[2]]), list(res[counts[2]:])


def _run_comm(comm, name):
    n_in, n_out = len(comm.inputs), len(comm.out_shape)

    def body(*refs):
        ins, outs, sems = refs[:n_in], refs[n_in:n_in + n_out], refs[n_in + n_out:]
        for phase in ("start", "forward", "finish"):
            comm.run(phase, ins, outs, sems)

    any_spec = pl.BlockSpec(memory_space=pl.ANY)
    return list(_pallas_call(body, name=name, in_specs=[any_spec] * n_in, out_specs=[any_spec] * n_out,
                             out_shape=comm.out_shape, scratch_shapes=comm.scratch,
                             input_output_aliases=comm.aliases)(*comm.inputs))


SUBLANES = 8
LANES = 128
ROW_CHUNK = 128
ALL_RESIDUES = tuple(range(1, SUBLANES))
SHORT_RESIDUES = tuple(sorted({(HALO - K_A + 1 + k) % SUBLANES for k in range(K_A)} - {0}))


class _Rows:
    def __init__(self, ref, shifted_ref=None, residues=()):
        self.ref, self.shifted_ref, self.residues = ref, shifted_ref, tuple(residues)

    def shift(self):
        n = self.shifted_ref.shape[1]
        for j, b in enumerate(self.residues):
            self.shifted_ref[j] = self.ref[pl.ds(b, n), :]

    def at(self, offset, r0, c0, rows):
        b = offset % SUBLANES
        if b in self.residues:
            return self.shifted_ref[self.residues.index(b), pl.ds(offset - b + r0, rows), pl.ds(c0, LANES)]
        return self.ref[pl.ds(offset + r0, rows), pl.ds(c0, LANES)]


def _shifted_scratch(residues, tm, c):
    return pltpu.VMEM((len(residues), tm + HALO - SUBLANES, c), F32)


def _causal_offsets(k_taps):
    return [HALO - k_taps + 1 + k for k in range(k_taps)]


def _anticausal_offsets(k_taps):
    return [k_taps - 1 - k for k in range(k_taps)]


def _tap_chunk(src, w_ref, offsets, r0, c0, rows):
    acc = w_ref[0:1, pl.ds(c0, LANES)] * src.at(offsets[0], r0, c0, rows)
    for k in range(1, len(offsets)):
        acc = acc + w_ref[k:k + 1, pl.ds(c0, LANES)] * src.at(offsets[k], r0, c0, rows)
    return acc


def _conv_whole(ref, w_ref, offsets, tm):
    acc = w_ref[0:1, :] * ref[pl.ds(offsets[0], tm), :]
    for k in range(1, len(offsets)):
        acc = acc + w_ref[k:k + 1, :] * ref[pl.ds(offsets[k], tm), :]
    return acc


def _tap_grads_whole(dw_ref, dy, ref, offsets, tm):
    for k, off in enumerate(offsets):
        dw_ref[k:k + 1, :] += jnp.sum(dy * ref[pl.ds(off, tm), :], axis=0, keepdims=True)


def _conv_taps(src, w_ref, offsets, tm, emit):
    rows = min(ROW_CHUNK, tm)
    for c0 in range(0, w_ref.shape[1], LANES):
        for r0 in range(0, tm, rows):
            emit(r0, rows, c0, _tap_chunk(src, w_ref, offsets, r0, c0, rows))


def _tap_grads(dw_ref, dy_at, src, offsets, tm):
    rows = min(ROW_CHUNK // 2, tm)
    for c0 in range(0, dw_ref.shape[1], LANES):
        acc = [None] * len(offsets)
        for r0 in range(0, tm, rows):
            dy = dy_at(r0, rows, c0)
            for k, off in enumerate(offsets):
                part = (dy * src.at(off, r0, c0, rows)).reshape(rows // SUBLANES, SUBLANES, LANES).sum(axis=0)
                acc[k] = part if acc[k] is None else acc[k] + part
        for k in range(len(offsets)):
            dw_ref[k:k + 1, pl.ds(c0, LANES)] += jnp.sum(acc[k], axis=0, keepdims=True)


def _prev_halo(tm, col):
    return lambda i: (jnp.maximum(i * (tm // HALO) - 1, 0), col)


def _next_halo(tm, n_rows, col):
    return lambda i: (jnp.minimum((i + 1) * (tm // HALO), n_rows // HALO - 1), col)


def _norm_matmul(x, g, w, gs, name, comm=None):
    t, d = x.shape
    n_s, _, ns = w.shape
    n = n_s * ns
    tm = min(TM_MM, t)

    def body(x_ref, g_ref, w_ref, h_ref, y_ref):
        @pl.when(pl.program_id(1) == 0)
        def _():
            xf = x_ref[...]
            r = lax.rsqrt(jnp.mean(xf * xf, axis=-1, keepdims=True) + EPS)
            h_ref[...] = ((xf * r) * g_ref[...]).astype(BF)

        for s in range(gs):
            y_ref[:, s * ns:(s + 1) * ns] = _nn(h_ref[...], w_ref[s]).astype(BF)

    return _run(
        body, comm, name=name, grid=(t // tm, n_s // gs),
        in_specs=[pl.BlockSpec((tm, d), lambda i, j: (i, 0)),
                  pl.BlockSpec((1, d), lambda i, j: (0, 0)),
                  pl.BlockSpec((gs, d, ns), lambda i, j: (j, 0, 0))],
        out_specs=[pl.BlockSpec((tm, d), lambda i, j: (i, 0)),
                   pl.BlockSpec((tm, gs * ns), lambda i, j: (i, j))],
        out_shape=[jax.ShapeDtypeStruct((t, d), BF), jax.ShapeDtypeStruct((t, n), BF)],
        args=(x, g, w))


def _mem_kv(mem, g, w_kv, name):
    m, d = mem.shape
    n_s, _, ns = w_kv.shape

    def body(mem_ref, g_ref, w_ref, memn_ref, kv_ref):
        xf = mem_ref[...]
        r = lax.rsqrt(jnp.mean(xf * xf, axis=-1, keepdims=True) + EPS)
        memn = ((xf * r) * g_ref[...]).astype(BF)
        memn_ref[...] = memn
        for s in range(n_s):
            kv_ref[:, s * ns:(s + 1) * ns] = _nn(memn, w_ref[s]).astype(BF)

    return _pallas_call(
        body, name=name, grid=(1,),
        in_specs=[pl.BlockSpec((m, d), lambda i: (0, 0)),
                  pl.BlockSpec((1, d), lambda i: (0, 0)),
                  pl.BlockSpec((n_s, d, ns), lambda i: (0, 0, 0))],
        out_specs=[pl.BlockSpec((m, d), lambda i: (0, 0)),
                   pl.BlockSpec((m, 2 * d), lambda i: (0, 0))],
        out_shape=[jax.ShapeDtypeStruct((m, d), BF), jax.ShapeDtypeStruct((m, 2 * d), BF)],
        compiler_params=_params(1),
    )(mem, g, w_kv)


def _load_branch_inputs(i, proj_ref, gch_ref, vh_ref, u0h_ref, ugh_ref, xa_ref, xb_ref, d):
    gc = proj_ref[:, d:2 * d].astype(F32)
    v = proj_ref[:, 2 * d:3 * d].astype(F32)
    u0 = proj_ref[:, 3 * d:4 * d].astype(F32)
    ug = proj_ref[:, 4 * d:5 * d].astype(F32)
    keep = (i > 0).astype(F32)
    xa_ref[pl.ds(0, HALO), :] = gch_ref[...].astype(F32) * vh_ref[...].astype(F32) * keep
    xa_ref[pl.ds(HALO, gc.shape[0]), :] = gc * v
    xb_ref[pl.ds(0, HALO), :] = u0h_ref[...].astype(F32) * _sig(ugh_ref[...].astype(F32)) * keep
    xb_ref[pl.ds(HALO, gc.shape[0]), :] = u0 * _sig(ug)


def _softmax_rows(s):
    e = jnp.exp(s - jnp.max(s, axis=-1, keepdims=True))
    return e / jnp.sum(e, axis=-1, keepdims=True)


def _mixer_fwd(proj, x, kv, conv_a, conv_b, cbias, ln_g, ln_b, b_gate, wsq, l, name, comm=None):
    t, d = x.shape
    m = kv.shape[0]
    tm = min(TM_ROW, t)
    hd = d // N_HEADS
    scale = 1.0 / math.sqrt(hd)

    def body(proj_ref, gch_ref, vh_ref, u0h_ref, ugh_ref, x_ref, kv_ref, ca_w, cb_w, cbias_ref, lng_ref, lnb_ref,
             bg_ref, wa_ref, wb_ref, wc_ref, wo_ref,
             x1_ref, za_ref, zb_ref, o_ref, ya_ref, yb_ref, yc_ref, mg_ref, cb_ref, xa_ref, xb_ref, sb_ref):
        i = pl.program_id(0)
        _load_branch_inputs(i, proj_ref, gch_ref, vh_ref, u0h_ref, ugh_ref, xa_ref, xb_ref, d)
        xb = _Rows(xb_ref, sb_ref, ALL_RESIDUES)
        xb.shift()
        def put_za(r0, rows, c0, ca):
            gb = proj_ref[pl.ds(r0, rows), pl.ds(c0, LANES)].astype(F32)
            za_ref[pl.ds(r0, rows), pl.ds(c0, LANES)] = (gb * ca).astype(BF)

        _conv_taps(_Rows(xa_ref), ca_w, _causal_offsets(K_A), tm, put_za)
        ya = _nn(za_ref[...], wa_ref[...])
        ya_ref[...] = ya.astype(BF)

        def put_cb(r0, rows, c0, conv):
            cb_ref[pl.ds(r0, rows), pl.ds(c0, LANES)] = conv + cbias_ref[:, pl.ds(c0, LANES)]

        _conv_taps(xb, cb_w, _causal_offsets(K_B), tm, put_cb)
        cb = cb_ref[...]
        mu = jnp.mean(cb, axis=-1, keepdims=True)
        dlt = cb - mu
        rstd = lax.rsqrt(jnp.mean(dlt * dlt, axis=-1, keepdims=True) + EPS)
        lnb = (dlt * rstd) * lng_ref[...] + lnb_ref[...]
        zb = (lnb * _sig(lnb)).astype(BF)
        zb_ref[...] = zb
        yb = _nn(zb, wb_ref[...])
        yb_ref[...] = yb.astype(BF)
        for h in range(N_HEADS):
            qh = proj_ref[:, 5 * d + h * hd:5 * d + (h + 1) * hd]
            kh = kv_ref[:, h * hd:(h + 1) * hd]
            vh = kv_ref[:, d + h * hd:d + (h + 1) * hd]
            p = _softmax_rows(_nt(qh, kh) * scale)
            o_ref[:, h * hd:(h + 1) * hd] = _nn(p.astype(BF), vh).astype(BF)
        yc = _nn(o_ref[...], wc_ref[...])
        yc_ref[...] = yc.astype(BF)
        g0 = _sig(proj_ref[:, 6 * d:7 * d].astype(F32) + bg_ref[:, 0:d])
        g1 = _sig(proj_ref[:, 7 * d:8 * d].astype(F32) + bg_ref[:, d:2 * d])
        g2 = _sig(proj_ref[:, 8 * d:9 * d].astype(F32) + bg_ref[:, 2 * d:3 * d])
        mg = (g0 * ya + g1 * yb + g2 * yc).astype(BF)
        mg_ref[...] = mg
        x1_ref[...] = x_ref[...] + _nn(mg, wo_ref[...])

    row = lambda w_: pl.BlockSpec((tm, w_), lambda i: (i, 0))
    halo = lambda col: pl.BlockSpec((HALO, d), _prev_halo(tm, col))
    sq = lambda which: _resident((None, None, d, d), (l, which, 0, 0))
    act = jax.ShapeDtypeStruct((t, d), BF)
    return _run(
        body, comm, name=name, grid=(t // tm,),
        in_specs=[row(9 * d), halo(1), halo(2), halo(3), halo(4), row(d),
                  _resident((m, 2 * d), (0, 0)),
                  _resident((None, K_A, d), (l, 0, 0)), _resident((None, K_B, d), (l, 0, 0)),
                  _resident((1, d), (0, 0)), _resident((1, d), (0, 0)), _resident((1, d), (0, 0)),
                  _resident((1, 3 * d), (0, 0)), sq(0), sq(1), sq(2), sq(3)],
        out_specs=[row(d)] * 9,
        out_shape=[jax.ShapeDtypeStruct((t, d), F32)] + [act] * 7 + [jax.ShapeDtypeStruct((t, d), F32)],
        scratch_shapes=[pltpu.VMEM((HALO + tm, d), F32), pltpu.VMEM((HALO + tm, d), F32),
                        _shifted_scratch(ALL_RESIDUES, tm, d)],
        args=(proj, proj, proj, proj, proj, x, kv, conv_a, conv_b, cbias, ln_g, ln_b, b_gate, wsq, wsq, wsq, wsq))


def _ffn_down_fwd(up, x1, conv_f, w_down, l, name, comm=None):
    t, d = x1.shape
    f2 = up.shape[1]
    f = f2 // 2
    tm = min(TM_ROW, t)

    def body(up_ref, uph_ref, x1_ref, cw_ref, wd_ref, x2_ref, zf_ref, uc_ref, xx_ref):
        i = pl.program_id(0)
        xx_ref[pl.ds(0, HALO), :] = uph_ref[...].astype(F32) * (i > 0).astype(F32)
        xx_ref[pl.ds(HALO, tm), :] = up_ref[...].astype(F32)
        uc = _conv_whole(xx_ref, cw_ref, _causal_offsets(K_F), tm)
        uc_ref[...] = uc.astype(BF)
        gt = uc[:, 0:f]
        zf = (gt * _sig(gt) * uc[:, f:f2]).astype(BF)
        zf_ref[...] = zf
        x2_ref[...] = x1_ref[...] + _nn(zf, wd_ref[...])

    return _run(
        body, comm, name=name, grid=(t // tm,),
        in_specs=[pl.BlockSpec((tm, f2), lambda i: (i, 0)),
                  pl.BlockSpec((HALO, f2), _prev_halo(tm, 0)),
                  pl.BlockSpec((tm, d), lambda i: (i, 0)),
                  _resident((None, K_F, f2), (l, 0, 0)),
                  _resident((None, f, d), (l, 0, 0))],
        out_specs=[pl.BlockSpec((tm, d), lambda i: (i, 0)), pl.BlockSpec((tm, f), lambda i: (i, 0)),
                   pl.BlockSpec((tm, f2), lambda i: (i, 0))],
        out_shape=[jax.ShapeDtypeStruct((t, d), F32), jax.ShapeDtypeStruct((t, f), BF),
                   jax.ShapeDtypeStruct((t, f2), BF)],
        scratch_shapes=[pltpu.VMEM((HALO + tm, f2), F32)],
        args=(up, up, x1, conv_f, w_down))


def _final_loss(x, g, target, name):
    t, d = x.shape
    tm = min(2 * TM_ROW, t)

    def body(x_ref, g_ref, t_ref, dx_ref, loss_ref, dg_ref):
        @pl.when(pl.program_id(0) == 0)
        def _():
            loss_ref[...] = jnp.zeros_like(loss_ref)
            dg_ref[...] = jnp.zeros_like(dg_ref)

        xf = x_ref[...]
        r = lax.rsqrt(jnp.mean(xf * xf, axis=-1, keepdims=True) + EPS)
        xhat = xf * r
        err = xhat * g_ref[...] - t_ref[...]
        loss_ref[...] += (0.5 / d) * jnp.sum(err * err)
        dy = err * (1.0 / d)
        dg_ref[...] += jnp.sum(dy * xhat, axis=0, keepdims=True)
        dxh = dy * g_ref[...]
        dx_ref[...] = r * (dxh - xhat * jnp.mean(dxh * xhat, axis=-1, keepdims=True))

    return _pallas_call(
        body, name=name, grid=(t // tm,),
        in_specs=[pl.BlockSpec((tm, d), lambda i: (i, 0)), pl.BlockSpec((1, d), lambda i: (0, 0)),
                  pl.BlockSpec((tm, d), lambda i: (i, 0))],
        out_specs=[pl.BlockSpec((tm, d), lambda i: (i, 0)), pl.BlockSpec((8, 128), lambda i: (0, 0)),
                   pl.BlockSpec((1, d), lambda i: (0, 0))],
        out_shape=[jax.ShapeDtypeStruct((t, d), F32), jax.ShapeDtypeStruct((8, 128), F32),
                   jax.ShapeDtypeStruct((1, d), F32)],
        compiler_params=_params(1),
    )(x, g, target)


def _ffn_down_bwd(dx2, up, uc, w_down, l, name, comm=None):
    t, d = dx2.shape
    f2 = up.shape[1]
    f = f2 // 2
    tm = min(TM_ROW, t)

    def body(dx2_ref, up_ref, uph_ref, uc_ref, wd_ref, duc_ref, dx2b_ref, dcw_ref, xx_ref):
        i = pl.program_id(0)

        @pl.when(i == 0)
        def _():
            dcw_ref[...] = jnp.zeros_like(dcw_ref)

        xx_ref[pl.ds(0, HALO), :] = uph_ref[...].astype(F32) * (i > 0).astype(F32)
        xx_ref[pl.ds(HALO, tm), :] = up_ref[...].astype(F32)
        gt = uc_ref[:, 0:f].astype(F32)
        sg = _sig(gt)
        dx2b = dx2_ref[...].astype(BF)
        dx2b_ref[...] = dx2b
        dzf = _nt(dx2b, wd_ref[...])
        duc_ref[:, 0:f] = (dzf * uc_ref[:, f:f2].astype(F32) * (sg * (1.0 + gt * (1.0 - sg)))).astype(BF)
        duc_ref[:, f:f2] = (dzf * (gt * sg)).astype(BF)
        _tap_grads_whole(dcw_ref, duc_ref[...].astype(F32), xx_ref, _causal_offsets(K_F), tm)

    return _run(
        body, comm, name=name, grid=(t // tm,),
        in_specs=[pl.BlockSpec((tm, d), lambda i: (i, 0)),
                  pl.BlockSpec((tm, f2), lambda i: (i, 0)),
                  pl.BlockSpec((HALO, f2), _prev_halo(tm, 0)),
                  pl.BlockSpec((tm, f2), lambda i: (i, 0)),
                  _resident((None, f, d), (l, 0, 0))],
        out_specs=[pl.BlockSpec((tm, f2), lambda i: (i, 0)), pl.BlockSpec((tm, d), lambda i: (i, 0)),
                   pl.BlockSpec((K_F, f2), lambda i: (0, 0))],
        out_shape=[jax.ShapeDtypeStruct((t, f2), BF), jax.ShapeDtypeStruct((t, d), BF),
                   jax.ShapeDtypeStruct((K_F, f2), F32)],
        scratch_shapes=[pltpu.VMEM((HALO + tm, f2), F32)],
        args=(dx2, up, up, uc, w_down))


def _ffn_conv_bwd(duc, conv_f, l, name, comm=None):
    t, f2 = duc.shape
    tm = min(TM_ROW, t)
    n_t = t // tm

    def body(duc_ref, nxt_ref, cw_ref, dup_ref, yy_ref):
        i = pl.program_id(0)
        yy_ref[pl.ds(0, tm), :] = duc_ref[...].astype(F32)
        yy_ref[pl.ds(tm, HALO), :] = nxt_ref[...].astype(F32) * (i < n_t - 1).astype(F32)

        def put(r0, rows, c0, conv):
            dup_ref[pl.ds(r0, rows), pl.ds(c0, LANES)] = conv.astype(BF)

        _conv_taps(_Rows(yy_ref), cw_ref, _anticausal_offsets(K_F), tm, put)

    return _run(
        body, comm, name=name, grid=(n_t,),
        in_specs=[pl.BlockSpec((tm, f2), lambda i: (i, 0)),
                  pl.BlockSpec((HALO, f2), _next_halo(tm, t, 0)),
                  _resident((None, K_F, f2), (l, 0, 0))],
        out_specs=[pl.BlockSpec((tm, f2), lambda i: (i, 0))],
        out_shape=[jax.ShapeDtypeStruct((t, f2), BF)],
        scratch_shapes=[pltpu.VMEM((tm + HALO, f2), F32)],
        args=(duc, duc, conv_f))


def _nt_matmul_norm_bwd(dy, w, gs, x, g, dres, name, comm=None):
    t, n = dy.shape
    d = x.shape[1]
    tm = min(TM_MM // 2, t)
    n_s, _, ns = w.shape
    n_k, tk = n_s // gs, gs * ns

    def body(dy_ref, w_ref, x_ref, g_ref, dres_ref, dx_ref, dxb_ref, dg_ref, acc_ref):
        i, k = pl.program_id(0), pl.program_id(1)

        @pl.when((i == 0) & (k == 0))
        def _():
            dg_ref[...] = jnp.zeros_like(dg_ref)

        @pl.when(k == 0)
        def _():
            acc_ref[...] = jnp.zeros_like(acc_ref)

        part = _nt(dy_ref[:, 0:ns], w_ref[0])
        for s in range(1, gs):
            part = part + _nt(dy_ref[:, s * ns:(s + 1) * ns], w_ref[s])
        acc_ref[...] += part

        @pl.when(k == n_k - 1)
        def _():
            xf = x_ref[...]
            r = lax.rsqrt(jnp.mean(xf * xf, axis=-1, keepdims=True) + EPS)
            xhat = xf * r
            dh = acc_ref[...]
            dg_ref[...] += jnp.sum(dh * xhat, axis=0, keepdims=True)
            dxh = dh * g_ref[...]
            dx = dres_ref[...] + r * (dxh - xhat * jnp.mean(dxh * xhat, axis=-1, keepdims=True))
            dx_ref[...] = dx
            dxb_ref[...] = dx.astype(BF)

    return _run(
        body, comm, name=name, grid=(t // tm, n_k),
        in_specs=[pl.BlockSpec((tm, tk), lambda i, k: (i, k)),
                  pl.BlockSpec((gs, d, ns), lambda i, k: (k, 0, 0)),
                  pl.BlockSpec((tm, d), lambda i, k: (i, 0)),
                  pl.BlockSpec((1, d), lambda i, k: (0, 0)),
                  pl.BlockSpec((tm, d), lambda i, k: (i, 0))],
        out_specs=[pl.BlockSpec((tm, d), lambda i, k: (i, 0)), pl.BlockSpec((tm, d), lambda i, k: (i, 0)),
                   pl.BlockSpec((1, d), lambda i, k: (0, 0))],
        out_shape=[jax.ShapeDtypeStruct((t, d), F32), jax.ShapeDtypeStruct((t, d), BF),
                   jax.ShapeDtypeStruct((1, d), F32)],
        scratch_shapes=[pltpu.VMEM((tm, d), F32)],
        args=(dy, w, x, g, dres))


def _mixer_bwd(dx1b, proj, ya, yb, yc, cb, kv, conv_a, conv_b, ln_g, ln_b, b_gate, wsq, l, name, comm=None):
    t, d = cb.shape
    m = kv.shape[0]
    tm = min(TM_ROW, t)
    hd = d // N_HEADS
    scale = 1.0 / math.sqrt(hd)

    def body(dx1b_ref, proj_ref, gch_ref, vh_ref, u0h_ref, ugh_ref, ya_ref, yb_ref, yc_ref, cb_ref, kv_ref,
             ca_w, cb_w, lng_ref, lnb_ref, bg_ref, wa_ref, wb_ref, wc_ref, wo_ref,
             dpre_ref, dya_ref, dyb_ref, dyc_ref, dkv_ref, dbg_ref, dlng_ref, dlnb_ref, dcbias_ref, dcaw_ref,
             dcbw_ref, xa_ref, xb_ref, sa_ref, sb_ref):
        i = pl.program_id(0)

        @pl.when(i == 0)
        def _():
            for ref in (dkv_ref, dbg_ref, dlng_ref, dlnb_ref, dcbias_ref, dcaw_ref, dcbw_ref):
                ref[...] = jnp.zeros_like(ref)

        _load_branch_inputs(i, proj_ref, gch_ref, vh_ref, u0h_ref, ugh_ref, xa_ref, xb_ref, d)
        xa, xb = _Rows(xa_ref, sa_ref, SHORT_RESIDUES), _Rows(xb_ref, sb_ref, ALL_RESIDUES)
        xa.shift()
        xb.shift()
        dmg = _nt(dx1b_ref[...], wo_ref[...])
        ys = (ya_ref, yb_ref, yc_ref)
        dys = (dya_ref, dyb_ref, dyc_ref)
        for b in range(3):
            gate = _sig(proj_ref[:, (6 + b) * d:(7 + b) * d].astype(F32) + bg_ref[:, b * d:(b + 1) * d])
            dys[b][...] = (gate * dmg).astype(BF)
            dpg = dmg * ys[b][...].astype(F32) * gate * (1.0 - gate)
            dpre_ref[:, (6 + b) * d:(7 + b) * d] = dpg.astype(BF)
            dbg_ref[:, b * d:(b + 1) * d] += jnp.sum(dpg, axis=0, keepdims=True)
        gb = proj_ref[:, 0:d].astype(F32)
        dza = _nt(dya_ref[...], wa_ref[...])

        def put_dgb(r0, rows, c0, ca):
            dpre_ref[pl.ds(r0, rows), pl.ds(c0, LANES)] = (dza[r0:r0 + rows, c0:c0 + LANES] * ca).astype(BF)

        _conv_taps(xa, ca_w, _causal_offsets(K_A), tm, put_dgb)
        dpre_ref[:, d:2 * d] = (dza * gb).astype(BF)
        _tap_grads(dcaw_ref, lambda r0, rows, c0: dpre_ref[pl.ds(r0, rows), pl.ds(d + c0, LANES)].astype(F32),
                   xa, _causal_offsets(K_A), tm)
        dpre_ref[:, 2 * d:3 * d] = jnp.zeros((tm, d), BF)
        cbv = cb_ref[...]
        mu = jnp.mean(cbv, axis=-1, keepdims=True)
        dlt = cbv - mu
        rstd = lax.rsqrt(jnp.mean(dlt * dlt, axis=-1, keepdims=True) + EPS)
        xhat = dlt * rstd
        lnb = xhat * lng_ref[...] + lnb_ref[...]
        sg = _sig(lnb)
        dzb = _nt(dyb_ref[...], wb_ref[...])
        dl = dzb * (sg * (1.0 + lnb * (1.0 - sg)))
        dlng_ref[...] += jnp.sum(dl * xhat, axis=0, keepdims=True)
        dlnb_ref[...] += jnp.sum(dl, axis=0, keepdims=True)
        dxh = dl * lng_ref[...]
        dcb = rstd * (dxh - jnp.mean(dxh, axis=-1, keepdims=True)
                      - xhat * jnp.mean(dxh * xhat, axis=-1, keepdims=True))
        dcbias_ref[...] += jnp.sum(dcb, axis=0, keepdims=True)
        dpre_ref[:, 3 * d:4 * d] = dcb.astype(BF)
        _tap_grads(dcbw_ref, lambda r0, rows, c0: dpre_ref[pl.ds(r0, rows), pl.ds(3 * d + c0, LANES)].astype(F32),
                   xb, _causal_offsets(K_B), tm)
        dpre_ref[:, 4 * d:5 * d] = jnp.zeros((tm, d), BF)
        do = _nt(dyc_ref[...], wc_ref[...]).astype(BF)
        for h in range(N_HEADS):
            qh = proj_ref[:, 5 * d + h * hd:5 * d + (h + 1) * hd]
            kh = kv_ref[:, h * hd:(h + 1) * hd]
            vh = kv_ref[:, d + h * hd:d + (h + 1) * hd]
            doh = do[:, h * hd:(h + 1) * hd]
            p = _softmax_rows(_nt(qh, kh) * scale)
            dp = _nt(doh, vh)
            ds = (p * (dp - jnp.sum(dp * p, axis=-1, keepdims=True)) * scale).astype(BF)
            dpre_ref[:, 5 * d + h * hd:5 * d + (h + 1) * hd] = _nn(ds, kh).astype(BF)
            dkv_ref[:, h * hd:(h + 1) * hd] += _tn(ds, qh)
            dkv_ref[:, d + h * hd:d + (h + 1) * hd] += _tn(p.astype(BF), doh)

    row = lambda w_: pl.BlockSpec((tm, w_), lambda i: (i, 0))
    halo = lambda col: pl.BlockSpec((HALO, d), _prev_halo(tm, col))
    sq = lambda which: _resident((None, None, d, d), (l, which, 0, 0))
    acc = lambda r, c: pl.BlockSpec((r, c), lambda i: (0, 0))
    act = jax.ShapeDtypeStruct((t, d), BF)
    vec = lambda r, c: jax.ShapeDtypeStruct((r, c), F32)
    return _run(
        body, comm, name=name, grid=(t // tm,),
        in_specs=[row(d), row(9 * d), halo(1), halo(2), halo(3), halo(4), row(d), row(d), row(d), row(d),
                  _resident((m, 2 * d), (0, 0)),
                  _resident((None, K_A, d), (l, 0, 0)), _resident((None, K_B, d), (l, 0, 0)),
                  _resident((1, d), (0, 0)), _resident((1, d), (0, 0)), _resident((1, 3 * d), (0, 0)),
                  sq(0), sq(1), sq(2), sq(3)],
        out_specs=[row(9 * d), row(d), row(d), row(d), acc(m, 2 * d), acc(1, 3 * d), acc(1, d), acc(1, d),
                   acc(1, d), acc(K_A, d), acc(K_B, d)],
        out_shape=[jax.ShapeDtypeStruct((t, 9 * d), BF), act, act, act, vec(m, 2 * d), vec(1, 3 * d), vec(1, d),
                   vec(1, d), vec(1, d), vec(K_A, d), vec(K_B, d)],
        scratch_shapes=[pltpu.VMEM((HALO + tm, d), F32), pltpu.VMEM((HALO + tm, d), F32),
                        _shifted_scratch(SHORT_RESIDUES, tm, d), _shifted_scratch(ALL_RESIDUES, tm, d)],
        args=(dx1b, proj, proj, proj, proj, proj, ya, yb, yc, cb, kv, conv_a, conv_b, ln_g, ln_b, b_gate,
              wsq, wsq, wsq, wsq))


def _inproj_conv_bwd(dpre, proj, conv_a, conv_b, l, name, comm=None):
    t, d9 = dpre.shape
    d = d9 // 9
    tm = min(TM_ROW, t)
    n_t = t // tm

    def body(dpre_ref, nxa_ref, nxb_ref, proj_ref, ca_w, cb_w, dproj_ref, ya_ref, yb_ref, sb_ref):
        i = pl.program_id(0)
        keep = (i < n_t - 1).astype(F32)
        ya_ref[pl.ds(0, tm), :] = dpre_ref[:, d:2 * d].astype(F32)
        ya_ref[pl.ds(tm, HALO), :] = nxa_ref[...].astype(F32) * keep
        yb_ref[pl.ds(0, tm), :] = dpre_ref[:, 3 * d:4 * d].astype(F32)
        yb_ref[pl.ds(tm, HALO), :] = nxb_ref[...].astype(F32) * keep
        yb = _Rows(yb_ref, sb_ref, ALL_RESIDUES)
        yb.shift()
        dproj_ref[:, 0:d] = dpre_ref[:, 0:d]
        dproj_ref[:, 5 * d:9 * d] = dpre_ref[:, 5 * d:9 * d]
        def chunk(ref, block, r0, rows, c0):
            return ref.at[pl.ds(r0, rows), pl.ds(block * d + c0, LANES)]

        def put_a(r0, rows, c0, dcv):
            chunk(dproj_ref, 1, r0, rows, c0)[...] = (dcv * chunk(proj_ref, 2, r0, rows, c0)[...].astype(F32)).astype(BF)
            chunk(dproj_ref, 2, r0, rows, c0)[...] = (dcv * chunk(proj_ref, 1, r0, rows, c0)[...].astype(F32)).astype(BF)

        def put_b(r0, rows, c0, dub):
            sg = _sig(chunk(proj_ref, 4, r0, rows, c0)[...].astype(F32))
            u0 = chunk(proj_ref, 3, r0, rows, c0)[...].astype(F32)
            chunk(dproj_ref, 3, r0, rows, c0)[...] = (dub * sg).astype(BF)
            chunk(dproj_ref, 4, r0, rows, c0)[...] = (dub * u0 * sg * (1.0 - sg)).astype(BF)

        _conv_taps(_Rows(ya_ref), ca_w, _anticausal_offsets(K_A), tm, put_a)
        _conv_taps(yb, cb_w, _anticausal_offsets(K_B), tm, put_b)

    return _run(
        body, comm, name=name, grid=(n_t,),
        in_specs=[pl.BlockSpec((tm, d9), lambda i: (i, 0)),
                  pl.BlockSpec((HALO, d), _next_halo(tm, t, 1)),
                  pl.BlockSpec((HALO, d), _next_halo(tm, t, 3)),
                  pl.BlockSpec((tm, d9), lambda i: (i, 0)),
                  _resident((None, K_A, d), (l, 0, 0)), _resident((None, K_B, d), (l, 0, 0))],
        out_specs=[pl.BlockSpec((tm, d9), lambda i: (i, 0))],
        out_shape=[jax.ShapeDtypeStruct((t, d9), BF)],
        scratch_shapes=[pltpu.VMEM((tm + HALO, d), F32), pltpu.VMEM((tm + HALO, d), F32),
                        _shifted_scratch(ALL_RESIDUES, tm, d)],
        args=(dpre, dpre, dpre, proj, conv_a, conv_b))


def _mem_kv_bwd(dkv, memn, mem, g, w_kv, name):
    m, d = mem.shape
    n_s, _, ns = w_kv.shape

    def body(dkv_ref, memn_ref, mem_ref, g_ref, w_ref, dw_ref, dg_ref):
        dkvb = dkv_ref[...].astype(BF)
        dw_ref[...] = _tn(memn_ref[...], dkvb).astype(BF)
        dmemn = _nt(dkvb[:, 0:ns], w_ref[0])
        for s in range(1, n_s):
            dmemn = dmemn + _nt(dkvb[:, s * ns:(s + 1) * ns], w_ref[s])
        xf = mem_ref[...]
        r = lax.rsqrt(jnp.mean(xf * xf, axis=-1, keepdims=True) + EPS)
        dg_ref[...] = jnp.sum(dmemn * (xf * r), axis=0, keepdims=True)

    return _pallas_call(
        body, name=name, grid=(1,),
        in_specs=[pl.BlockSpec((m, 2 * d), lambda i: (0, 0)), pl.BlockSpec((m, d), lambda i: (0, 0)),
                  pl.BlockSpec((m, d), lambda i: (0, 0)), pl.BlockSpec((1, d), lambda i: (0, 0)),
                  pl.BlockSpec((n_s, d, ns), lambda i: (0, 0, 0))],
        out_specs=[pl.BlockSpec((d, 2 * d), lambda i: (0, 0)), pl.BlockSpec((1, d), lambda i: (0, 0))],
        out_shape=[jax.ShapeDtypeStruct((d, 2 * d), BF), jax.ShapeDtypeStruct((1, d), F32)],
        compiler_params=_params(1),
    )(dkv, memn, mem, g, w_kv)


def _dw_matmul(a, b, tn, name, comm=None):
    t, k = a.shape
    n = b.shape[1]
    tt = min(TT_DW, t)
    while tt * k > DW_LHS_ELEMS and tt % 2 == 0:
        tt //= 2
    n_s = t // tt

    def body(a_ref, b_ref, o_ref, acc_ref):
        s = pl.program_id(1)
        part = _tn(a_ref[...], b_ref[...])
        if n_s == 1:
            o_ref[...] = part.astype(BF)
            return

        @pl.when(s == 0)
        def _():
            acc_ref[...] = part

        @pl.when(s > 0)
        def _():
            acc_ref[...] += part

        @pl.when(s == n_s - 1)
        def _():
            o_ref[...] = acc_ref[...].astype(BF)

    return _run(
        body, comm, name=name, grid=(n // tn, n_s),
        in_specs=[pl.BlockSpec((tt, k), lambda j, s: (s, 0)), pl.BlockSpec((tt, tn), lambda j, s: (s, j))],
        out_specs=[pl.BlockSpec((k, tn), lambda j, s: (0, j))],
        out_shape=[jax.ShapeDtypeStruct((k, n), BF)],
        scratch_shapes=[pltpu.VMEM((k, tn) if n_s > 1 else (8, 128), F32)],
        args=(a, b))


class _GradReduce:
    def __init__(self, l, grads, core):
        self.l, self.core, self.names = l, core, tuple(grads)
        self.views = {n: _halves_view(n, g) for n, g in grads.items()}
        self.got, self.sums, self.landing, self.pieces = {}, {}, {}, {}

    def swap_program(self):
        return _swap_program([self.views[n] for n in self.names])

    def swapped(self, outs):
        self.got = dict(zip(self.names, outs))
        for n in self.names:
            self.sums[n], self.landing[n] = _add_halves(self.views[n], self.got[n], self.core, n in COL,
                                                        f"add_halves_{n}_{self.l}")

    def scatter_program(self, names=None):
        names = self.names if names is None else names
        return _scatter_program([self.sums[n] for n in names], [self.landing[n] for n in names], names)

    def scattered(self, outs, names=None):
        self.pieces.update(zip(self.names if names is None else names, outs))


def _step(x, mem, target, sh, conv_sh, small, core):
    depth = len(sh["w_in"])
    d = x.shape[1]
    f2 = sh["w_up"][0].shape[2] * N_CHIPS
    tn_dw_in = min(1024, d)
    tn_dw_sq = max(LANES, d // 4)
    tn_dw_up = f2 // 11 if f2 % (11 * 128) == 0 and f2 // 11 >= 128 else f2
    row = lambda v: v.reshape(1, -1)
    one = lambda a: a[None]

    w_in, cab, cf = _run_comm(_gather_program([sh["w_in"][0]] + conv_sh, ("col", "small", "small")), "gather_first")
    saved = []
    for l in range(depth):
        conv = dict(a=cab[l:l + 1, :K_A], b=cab[l:l + 1, K_A:], f=cf[l:l + 1])
        (h, proj), (wsq, w_kv) = _norm_matmul(
            x, row(small["norm_mix_g"][l]), w_in, 1, f"in_proj_{l}",
            _gather_program([sh["wsq"][l], sh["w_kv"][l]], ("row", "col")))
        memn, kv = _mem_kv(mem, row(small["norm_mem_g"][l]), w_kv, f"mem_kv_{l}")
        (x1, za, zb, o, ya, yb, yc, mg, cb), (w_up, w_down) = _mixer_fwd(
            proj, x, kv, conv["a"], conv["b"], row(small["conv_b_bias"][l]), row(small["ln_b_g"][l]),
            row(small["ln_b_b"][l]), row(small["b_gate"][l]), one(wsq), 0, f"mixer_fwd_{l}",
            _gather_program([sh["w_up"][l], sh["w_down"][l]], ("col", "row")))
        more = l + 1 < depth
        nxt = _gather_program([sh["w_in"][l + 1]], ("col",), (0, 2)) if more else None
        (h2, up), w_in_next = _norm_matmul(x1, row(small["norm_ffn_g"][l]), w_up, 2, f"up_proj_{l}", nxt)
        nxt = _gather_program(w_in_next, ("col",), (1, 2)) if more else None
        (x2, zf, uc), w_in_next = _ffn_down_fwd(up, x1, conv["f"], one(w_down), 0, f"ffn_down_fwd_{l}", nxt)
        saved.append(dict(x=x, x1=x1, memn=memn, kv=kv, h=h, proj=proj, za=za, zb=zb, o=o, ya=ya, yb=yb, yc=yc,
                          mg=mg, cb=cb, h2=h2, up=up, zf=zf, uc=uc, w_in=w_in, wsq=one(wsq), w_kv=w_kv,
                          w_up=w_up, w_down=one(w_down), conv=conv))
        x = x2
        w_in = w_in_next[0] if w_in_next else None
    dx, loss, dg_final = _final_loss(x, row(small["norm_final_g"]), target, "final_loss")
    sgrads, pieces = [None] * depth, [None] * depth
    above = None
    first, second, rest = SQUARES, ("w_up",), ("w_in", "w_kv", "w_down")
    for l in reversed(range(depth)):
        s = saved[l]
        conv = s["conv"]
        bottom = l == 0
        (duc, dx2b, dconv_f), got = _ffn_down_bwd(dx, s["up"], s["uc"], s["w_down"], 0, f"ffn_down_bwd_{l}",
                                                  above and above.swap_program())
        if above:
            above.swapped(got)
        (dup,), got = _ffn_conv_bwd(duc, conv["f"], 0, f"ffn_conv_bwd_{l}", above and above.scatter_program(first))
        if above:
            above.scattered(got, first)
        (dx1, dx1b, dg_ffn), got = _nt_matmul_norm_bwd(
            dup, s["w_up"], 2, s["x1"], row(small["norm_ffn_g"][l]), dx, f"up_proj_bwd_{l}",
            above and above.scatter_program(second))
        if above:
            above.scattered(got, second)
        grads = dict(w_up=_dw_matmul(s["h2"], dup, tn_dw_up, f"dw_up_{l}")[0][0],
                     w_down=_dw_matmul(s["zf"], dx2b, d, f"dw_down_{l}")[0][0])
        ffn = _GradReduce(l, grads, core) if bottom else None
        (dpre, dya, dyb, dyc, dkv, dbg, dlng, dlnb, dcbias, dconv_a, dconv_b), got = _mixer_bwd(
            dx1b, s["proj"], s["ya"], s["yb"], s["yc"], s["cb"], s["kv"], conv["a"], conv["b"],
            row(small["ln_b_g"][l]), row(small["ln_b_b"][l]), row(small["b_gate"][l]), s["wsq"], 0,
            f"mixer_bwd_{l}", _join([above and above.scatter_program(rest), ffn and ffn.swap_program()]))
        if above:
            above.scattered(got[:len(rest)], rest)
            pieces[above.l] = above.pieces
            got = got[len(rest):]
        if ffn:
            ffn.swapped(got)
        dw_kv, dg_mem = _mem_kv_bwd(dkv, s["memn"], mem, row(small["norm_mem_g"][l]), s["w_kv"], f"mem_kv_bwd_{l}")
        mix_grads = dict(w_a_out=_dw_matmul(s["za"], dya, tn_dw_sq, f"dw_a_out_{l}")[0][0],
                         w_b_out=_dw_matmul(s["zb"], dyb, tn_dw_sq, f"dw_b_out_{l}")[0][0],
                         w_att_out=_dw_matmul(s["o"], dyc, tn_dw_sq, f"dw_att_out_{l}")[0][0],
                         w_o=_dw_matmul(s["mg"], dx1b, tn_dw_sq, f"dw_o_{l}")[0][0], w_kv=dw_kv)
        mix = _GradReduce(l, mix_grads, core) if bottom else None
        (dproj,), got = _inproj_conv_bwd(dpre, s["proj"], conv["a"], conv["b"], 0, f"inproj_conv_bwd_{l}",
                                         _join([ffn and ffn.scatter_program(), mix and mix.swap_program()]))
        if bottom:
            ffn.scattered(got[:len(ffn.names)])
            mix.swapped(got[len(ffn.names):])
        in_grads = dict(w_in=_dw_matmul(s["h"], dproj, tn_dw_in, f"dw_in_{l}")[0][0])
        inp = _GradReduce(l, in_grads, core) if bottom else None
        (dx0, _, dg_mix), got = _nt_matmul_norm_bwd(
            dproj, s["w_in"], 2, s["x"], row(small["norm_mix_g"][l]), dx1, f"in_proj_bwd_{l}",
            _join([mix and mix.scatter_program(), inp and inp.swap_program()]))
        if bottom:
            mix.scattered(got[:len(mix.names)])
            inp.swapped(got[len(mix.names):])
            inp.scattered(_run_comm(inp.scatter_program(), f"scatter_w_in_{l}"))
            pieces[l] = {**ffn.pieces, **mix.pieces, **inp.pieces}
        else:
            above = _GradReduce(l, {**grads, **mix_grads, **in_grads}, core)
        sgrads[l] = dict(norm_mix_g=dg_mix, norm_mem_g=dg_mem, b_gate=dbg, conv_b_bias=dcbias, ln_b_g=dlng,
                         ln_b_b=dlnb, norm_ffn_g=dg_ffn, conv_a_w=dconv_a, conv_b_w=dconv_b, conv_ffn_w=dconv_f)
        dx = dx0
    return loss, dx, sgrads, dg_final, pieces


BIG = ("w_in", "w_a_out", "w_b_out", "w_att_out", "w_o", "w_kv", "w_up", "w_down")
COL = ("w_in", "w_kv", "w_up")
SQUARES = ("w_a_out", "w_b_out", "w_att_out", "w_o")
ANY = pl.BlockSpec(memory_space=pl.ANY)


def _place():
    x, y, c = lax.axis_index("x"), lax.axis_index("y"), lax.axis_index("c")
    chips = [(1 - x, y), (x, 1 - y), (1 - x, 1 - y)]
    return x, y, c, 2 * x + y, chips


def _remote(src, dst, send_sem, recv_sem, dev):
    return pltpu.make_async_remote_copy(src_ref=src, dst_ref=dst, send_sem=send_sem, recv_sem=recv_sem,
                                        device_id=dev, device_id_type=MESH)


class _Striped:
    def __init__(self, src, dst, make):
        rows = src.shape[-2]
        unit = 8 * (4 // jnp.dtype(src.dtype).itemsize)
        n = max(k for k in range(1, MAX_STRIPES + 1) if rows % (unit * k) == 0) if rows % unit == 0 else 1
        q = rows // n
        self.parts = [make(_window(src, pl.ds(i * q, q), slice(None)), _window(dst, pl.ds(i * q, q), slice(None)))
                      for i in range(n)]
        self.whole = make(src, dst)

    def start(self):
        for p in self.parts:
            p.start()

    def wait(self):
        self.whole.wait()

    def wait_send(self):
        self.whole.wait_send()

    def wait_recv(self):
        self.whole.wait_recv()


def _far(src, dst, send_sem, recv_sem, dev):
    return _Striped(src, dst, lambda s, d: _remote(s, d, send_sem, recv_sem, dev))


def _near(src, dst, sem):
    return _Striped(src, dst, lambda s, d: pltpu.make_async_copy(s, d, sem))


def _window(ref, rows, cols):
    return ref.at[(slice(None),) * (len(ref.shape) - 2) + (rows, cols)]


def _row_tile(rows, cols, unit=16, limit=1 << 20):
    best = unit
    for tr in range(unit, rows + 1, unit):
        if rows % tr == 0 and tr * cols <= limit:
            best = tr
    return best


def _cast_place(ws, l, kind, chip, name):
    _, k, n = ws[0].shape
    if kind == "col":
        shape, spec = (N_CHIPS, k, n), pl.BlockSpec((None, k, n), lambda i, c: (c[0], 0, 0))
    elif len(ws) == 1:
        shape, spec = (N_CHIPS * k, n), pl.BlockSpec((k, n), lambda i, c: (c[0], 0))
    else:
        shape, spec = (len(ws), N_CHIPS * k, n), pl.BlockSpec((len(ws), k, n), lambda i, c: (0, c[0], 0))

    def body(c_ref, *refs):
        o_ref = refs[-1]
        if len(ws) == 1 or kind == "col":
            o_ref[...] = refs[0][...].astype(BF)
        else:
            for i in range(len(ws)):
                o_ref[i] = refs[i][...].astype(BF)

    return _pallas_call(
        body, name=name,
        grid_spec=pltpu.PrefetchScalarGridSpec(
            num_scalar_prefetch=1, grid=(1,),
            in_specs=[pl.BlockSpec((None, k, n), lambda i, c: (l, 0, 0))] * len(ws), out_specs=spec),
        out_shape=jax.ShapeDtypeStruct(shape, BF),
        compiler_params=_params(1),
    )(chip, *ws)


def _gather_program(arrays, kinds, part=(0, 1)):
    n_t = len(arrays)
    index, count = part

    def shard_rows(f, kind):
        return f.shape[-2] if kind == "col" else f.shape[-2] // N_CHIPS

    def run(phase, ins, full, sems):
        ici_send, ici_recv, sib_send, sib_recv, loc_sem = sems
        x, y, c, me, chips = _place()
        sibling = (x, y, 1 - c)

        def part_of(i, chip, half):
            f, kind = full[i], kinds[i]
            if kind == "small":
                cols = ins[i].shape[-1]
                return _window(f, slice(None), pl.ds(pl.multiple_of(chip * cols, 128), cols))
            rows = shard_rows(f, kind)
            r = rows // (2 * count)
            at = (half * count + index) * r
            if kind == "col":
                return f.at[chip, pl.ds(pl.multiple_of(at, 16), r), :]
            return _window(f, pl.ds(pl.multiple_of(chip * rows + at, 16), r), slice(None))

        src_part = lambda i, half: ins[i] if kinds[i] == "small" else part_of(i, me, half)
        dst_part = part_of
        local = [_near(ins[i], part_of(i, me, c), loc_sem.at[i]) for i in range(n_t) if kinds[i] == "small"]
        sends = []
        for i in range(n_t):
            for j, chip in enumerate(chips):
                sends.append(_far(src_part(i, c), dst_part(i, me, c), ici_send.at[3 * i + j],
                                  ici_recv.at[3 * i + j], (*chip, c)))
        if phase == "start":
            for cp in local + sends:
                cp.start()
            return
        passed = []
        for i in range(n_t):
            for j, chip in enumerate(chips):
                k = 2 * chip[0] + chip[1]
                landed = dst_part(i, k, c)
                if phase == "forward":
                    _remote(landed, landed, ici_send.at[3 * i + j], ici_recv.at[3 * i + j], (*chip, c)).wait_recv()
                if kinds[i] != "small":
                    passed.append(_far(landed, landed, sib_send.at[3 * i + j], sib_recv.at[3 * i + j], sibling))
                    if phase == "forward":
                        passed[-1].start()
        if phase == "forward":
            return
        for i in range(n_t):
            if kinds[i] == "small":
                continue
            for j, chip in enumerate(chips):
                k = 2 * chip[0] + chip[1]
                other = dst_part(i, k, 1 - c)
                _remote(other, other, sib_send.at[3 * i + j], sib_recv.at[3 * i + j], sibling).wait_recv()
        for cp in sends + passed:
            cp.wait_send()
        for cp in local:
            cp.wait()

    def out_shape(a, kind):
        shp = a.shape[:-1] + (a.shape[-1] * N_CHIPS,) if kind == "small" else a.shape
        return jax.ShapeDtypeStruct(shp, a.dtype)

    outs = [out_shape(a, k) for a, k in zip(arrays, kinds)]
    sems = [pltpu.SemaphoreType.DMA((3 * n_t,))] * 4 + [pltpu.SemaphoreType.DMA((n_t,))]
    return _Comm(arrays, outs, sems, run, {i: i for i in range(n_t) if kinds[i] != "small"})


def _all_reduce_small(part, name):
    r, n = part.shape

    def body(in_ref, out_ref, pair_ref, chips_ref, sib_sems, send_sems, recv_sems):
        x, y, c, me, chips = _place()
        pair_ref[c] = in_ref[...]
        swap = _remote(in_ref, pair_ref.at[c], sib_sems.at[0], sib_sems.at[1], (x, y, 1 - c))
        swap.start()
        _remote(in_ref, pair_ref.at[1 - c], sib_sems.at[0], sib_sems.at[1], (x, y, c)).wait_recv()
        chips_ref[me] = pair_ref[0] + pair_ref[1]
        sends = [_remote(chips_ref.at[me], chips_ref.at[me], send_sems.at[j], recv_sems.at[j], (*chip, c))
                 for j, chip in enumerate(chips)]
        for cp in sends:
            cp.start()
        for j, chip in enumerate(chips):
            landed = chips_ref.at[2 * chip[0] + chip[1]]
            _remote(landed, landed, send_sems.at[j], recv_sems.at[j], (x, y, c)).wait_recv()
        for cp in sends:
            cp.wait_send()
        swap.wait_send()
        total = chips_ref[0]
        for k in range(1, N_CHIPS):
            total = total + chips_ref[k]
        out_ref[...] = total

    vm = pl.BlockSpec(memory_space=pltpu.VMEM)
    return _pallas_call(
        body, name=name, in_specs=[vm], out_specs=vm, out_shape=jax.ShapeDtypeStruct((r, n), F32),
        scratch_shapes=[pltpu.VMEM((2, r, n), F32), pltpu.VMEM((N_CHIPS, r, n), F32),
                        pltpu.SemaphoreType.DMA((2,)), pltpu.SemaphoreType.DMA((N_CHIPS - 1,)),
                        pltpu.SemaphoreType.DMA((N_CHIPS - 1,))],
        compiler_params=pltpu.CompilerParams(vmem_limit_bytes=VMEM_LIMIT),
    )(part)


def _halves_view(name, dw):
    k, n = dw.shape
    s = 1 if name in COL else N_CHIPS
    return dw.reshape(s, 2, k // (2 * s), n)


def _swap_program(views):
    n_t = len(views)

    def run(phase, src, dst, sems):
        send_sems, recv_sems = sems
        x, y, c, _, _ = _place()
        copies = [_far(src[i].at[:, 1 - c], dst[i], send_sems.at[i], recv_sems.at[i], (x, y, 1 - c))
                  for i in range(n_t)]
        for cp in copies:
            if phase == "start":
                cp.start()
            elif phase == "finish":
                cp.wait()

    outs = [jax.ShapeDtypeStruct((v.shape[0],) + v.shape[2:], v.dtype) for v in views]
    sems = [pltpu.SemaphoreType.DMA((n_t,)), pltpu.SemaphoreType.DMA((n_t,))]
    return _Comm(views, outs, sems, run)


def _add_halves(view, got, place, col, name):
    s, _, r, n = view.shape
    if col:
        cw = n // N_CHIPS
        tr = _row_tile(r, cw)
        grid = (r // tr, N_CHIPS)
        in_specs = [pl.BlockSpec((None, None, tr, cw), lambda j, q, p: (0, p[0], j, q)),
                    pl.BlockSpec((None, tr, cw), lambda j, q, p: (0, j, q))]
        out_specs = [pl.BlockSpec((None, tr, cw), lambda j, q, p: (0, j, q)),
                     pl.BlockSpec((None, None, tr, cw), lambda j, q, p: (p[0], p[1], j, 0))]
    else:
        cw = n
        tr = _row_tile(r, n)
        grid = (s, r // tr)
        in_specs = [pl.BlockSpec((None, None, tr, n), lambda i, j, p: (i, p[0], j, 0)),
                    pl.BlockSpec((None, tr, n), lambda i, j, p: (i, j, 0))]
        out_specs = [pl.BlockSpec((None, tr, n), lambda i, j, p: (i, j, 0)),
                     pl.BlockSpec((None, None, tr, n), lambda i, j, p: (p[0], i, j, 0))]

    def body(p_ref, a_ref, b_ref, o_ref, z_ref):
        total = (a_ref[...].astype(F32) + b_ref[...].astype(F32)).astype(BF)
        o_ref[...] = total
        if col:
            @pl.when(pl.program_id(1) == p_ref[1])
            def _():
                z_ref[...] = total
        else:
            z_ref[...] = total

    return _pallas_call(
        body, name=name,
        grid_spec=pltpu.PrefetchScalarGridSpec(num_scalar_prefetch=1, grid=grid, in_specs=in_specs,
                                               out_specs=out_specs),
        out_shape=[jax.ShapeDtypeStruct((s, r, n), BF), jax.ShapeDtypeStruct((2, N_CHIPS, r, cw), BF)],
        compiler_params=_params(2),
    )(place, view, got)


def _scatter_program(sums, landing, names):
    n_t = len(sums)

    def run(phase, src, dst, sems):
        ici_send, ici_recv, sib_send, sib_recv = sems
        x, y, c, me, chips = _place()
        sibling = (x, y, 1 - c)

        def piece(i, chip):
            if names[i] in COL:
                cw = src[i].shape[2] // N_CHIPS
                return src[i].at[0, :, pl.ds(pl.multiple_of(chip * cw, 128), cw)]
            return src[i].at[chip]

        local = []
        sends = []
        for i in range(n_t):
            sends.append(_far(piece(i, me), dst[i].at[c, me], sib_send.at[4 * i + 3], sib_recv.at[4 * i + 3],
                              sibling))
            for j, chip in enumerate(chips):
                k = 2 * chip[0] + chip[1]
                sends.append(_far(piece(i, k), dst[i].at[c, me], ici_send.at[3 * i + j], ici_recv.at[3 * i + j],
                                  (*chip, c)))
        if phase == "start":
            for cp in local + sends:
                cp.start()
            return
        passed = []
        for i in range(n_t):
            for j, chip in enumerate(chips):
                k = 2 * chip[0] + chip[1]
                landed = dst[i].at[c, k]
                if phase == "forward":
                    _remote(landed, landed, ici_send.at[3 * i + j], ici_recv.at[3 * i + j], (*chip, c)).wait_recv()
                passed.append(_far(landed, landed, sib_send.at[4 * i + j], sib_recv.at[4 * i + j], sibling))
                if phase == "forward":
                    passed[-1].start()
        if phase == "forward":
            return
        for i in range(n_t):
            other = dst[i].at[1 - c, me]
            _remote(other, other, sib_send.at[4 * i + 3], sib_recv.at[4 * i + 3], sibling).wait_recv()
            for j, chip in enumerate(chips):
                k = 2 * chip[0] + chip[1]
                other = dst[i].at[1 - c, k]
                _remote(other, other, sib_send.at[4 * i + j], sib_recv.at[4 * i + j], sibling).wait_recv()
        for cp in sends + passed:
            cp.wait_send()
        for cp in local:
            cp.wait()

    outs = [jax.ShapeDtypeStruct(z.shape, z.dtype) for z in landing]
    sems = [pltpu.SemaphoreType.DMA((3 * n_t,))] * 2 + [pltpu.SemaphoreType.DMA((4 * n_t,))] * 2
    return _Comm(list(sums) + list(landing), outs, sems, run, {n_t + i: i for i in range(n_t)})


def _adamw(w, g, m, v):
    m = ADAM_B1 * m + (1.0 - ADAM_B1) * g
    v = ADAM_B2 * v + (1.0 - ADAM_B2) * (g * g)
    m_hat = m / (1.0 - ADAM_B1 ** ADAM_STEP)
    v_hat = v / (1.0 - ADAM_B2 ** ADAM_STEP)
    return -ADAM_LR * (m_hat / (jnp.sqrt(v_hat) + ADAM_EPS) + ADAM_WD * w), m, v


def _adam_shard(pieces, w, m, v, l, prev, name):
    depth, rows, cw = w.shape
    hr = rows // 2
    tr = _row_tile(hr, cw, limit=1 << 18)
    n_i = hr // tr

    def body(*refs):
        z_ref, w_ref, m_ref, v_ref = refs[:4]
        g_ref, d_ref, nm_ref, nv_ref = refs[-4:]
        g = z_ref[0].astype(F32)
        for k in range(1, N_CHIPS):
            g = g + z_ref[k].astype(F32)
        g_ref[...] = g
        d_ref[...], nm_ref[...], nv_ref[...] = _adamw(w_ref[...], g, m_ref[...], v_ref[...])

    par = pl.BlockSpec((None, tr, cw), lambda h, i: (l, h * n_i + i, 0))
    out = jax.ShapeDtypeStruct((depth, rows, cw), F32)
    extra = [] if prev is None else list(prev)
    return _pallas_call(
        body, name=name, grid=(2, n_i),
        in_specs=[pl.BlockSpec((None, N_CHIPS, tr, cw), lambda h, i: (h, 0, i, 0)), par, par, par] + [ANY] * len(extra),
        out_specs=[par] * 4, out_shape=[out] * 4,
        input_output_aliases={4 + k: k for k in range(len(extra))},
        compiler_params=_params(2),
    )(pieces, w, m, v, *extra)


def _adam_small(g, w, m, v, name):
    def body(g_ref, w_ref, m_ref, v_ref, d_ref, nm_ref, nv_ref):
        d_ref[...], nm_ref[...], nv_ref[...] = _adamw(w_ref[...], g_ref[...], m_ref[...], v_ref[...])

    vm = pl.BlockSpec(memory_space=pltpu.VMEM)
    out = jax.ShapeDtypeStruct(g.shape, F32)
    return _pallas_call(body, name=name, in_specs=[vm] * 4, out_specs=[vm] * 3, out_shape=[out] * 3)(g, w, m, v)


WEIGHTS = ("norm_mix_g", "norm_mem_g", "w_in", "b_gate", "conv_a_w", "w_a_out", "conv_b_w", "conv_b_bias", "ln_b_g",
           "ln_b_b", "w_b_out", "w_kv", "w_att_out", "w_o", "norm_ffn_g", "w_up", "conv_ffn_w", "w_down",
           "norm_final_g")
REPLICATED = ("norm_mix_g", "norm_mem_g", "b_gate", "conv_b_bias", "ln_b_g", "ln_b_b", "norm_ffn_g")
CONVS = ("conv_a_w", "conv_b_w", "conv_ffn_w")
PACK_WIDTH = 1024


def _pack(arrays):
    flat = jnp.concatenate([a.reshape(-1) for a in arrays])
    size = -(-flat.shape[0] // (8 * PACK_WIDTH)) * (8 * PACK_WIDTH)
    return jnp.pad(flat, (0, size - flat.shape[0])).reshape(-1, PACK_WIDTH)


def _unpack(packed, shapes):
    flat = packed.reshape(-1)
    out, at = [], 0
    for shp in shapes:
        n = math.prod(shp)
        out.append(flat[at:at + n].reshape(shp))
        at += n
    return out


def kernel(x, mem, norm_mix_g, norm_mem_g, w_in, b_gate, conv_a_w, w_a_out, conv_b_w, conv_b_bias, ln_b_g, ln_b_b, w_b_out, w_kv, w_att_out, w_o, norm_ffn_g, w_up, conv_ffn_w, w_down, norm_final_g, loss_target, m_norm_mix_g, m_norm_mem_g, m_w_in, m_b_gate, m_conv_a_w, m_w_a_out, m_conv_b_w, m_conv_b_bias, m_ln_b_g, m_ln_b_b, m_w_b_out, m_w_kv, m_w_att_out, m_w_o, m_norm_ffn_g, m_w_up, m_conv_ffn_w, m_w_down, m_norm_final_g, v_norm_mix_g, v_norm_mem_g, v_w_in, v_b_gate, v_conv_a_w, v_w_a_out, v_conv_b_w, v_conv_b_bias, v_ln_b_g, v_ln_b_b, v_w_b_out, v_w_kv, v_w_att_out, v_w_o, v_norm_ffn_g, v_w_up, v_conv_ffn_w, v_w_down, v_norm_final_g):
    w = dict(norm_mix_g=norm_mix_g, norm_mem_g=norm_mem_g, w_in=w_in, b_gate=b_gate, conv_a_w=conv_a_w,
             w_a_out=w_a_out, conv_b_w=conv_b_w, conv_b_bias=conv_b_bias, ln_b_g=ln_b_g, ln_b_b=ln_b_b,
             w_b_out=w_b_out, w_kv=w_kv, w_att_out=w_att_out, w_o=w_o, norm_ffn_g=norm_ffn_g, w_up=w_up,
             conv_ffn_w=conv_ffn_w, w_down=w_down, norm_final_g=norm_final_g)
    mom = dict(norm_mix_g=m_norm_mix_g, norm_mem_g=m_norm_mem_g, w_in=m_w_in, b_gate=m_b_gate, conv_a_w=m_conv_a_w,
               w_a_out=m_w_a_out, conv_b_w=m_conv_b_w, conv_b_bias=m_conv_b_bias, ln_b_g=m_ln_b_g, ln_b_b=m_ln_b_b,
               w_b_out=m_w_b_out, w_kv=m_w_kv, w_att_out=m_w_att_out, w_o=m_w_o, norm_ffn_g=m_norm_ffn_g,
               w_up=m_w_up, conv_ffn_w=m_conv_ffn_w, w_down=m_w_down, norm_final_g=m_norm_final_g)
    var = dict(norm_mix_g=v_norm_mix_g, norm_mem_g=v_norm_mem_g, w_in=v_w_in, b_gate=v_b_gate, conv_a_w=v_conv_a_w,
               w_a_out=v_w_a_out, conv_b_w=v_conv_b_w, conv_b_bias=v_conv_b_bias, ln_b_g=v_ln_b_g, ln_b_b=v_ln_b_b,
               w_b_out=v_w_b_out, w_kv=v_w_kv, w_att_out=v_w_att_out, w_o=v_w_o, norm_ffn_g=v_norm_ffn_g,
               w_up=v_w_up, conv_ffn_w=v_conv_ffn_w, w_down=v_w_down, norm_final_g=v_norm_final_g)
    depth = w_in.shape[0]
    chip = 2 * lax.axis_index("x") + lax.axis_index("y")
    core = jnp.stack([lax.axis_index("c"), chip]).astype(jnp.int32)

    chip1 = chip.astype(jnp.int32).reshape(1)
    layers = range(depth)
    sh = dict(w_in=[_cast_place([w_in], l, "col", chip1, f"cast_w_in_{l}") for l in layers],
              wsq=[_cast_place([w[n] for n in SQUARES], l, "row", chip1, f"cast_squares_{l}") for l in layers],
              w_kv=[_cast_place([w_kv], l, "col", chip1, f"cast_w_kv_{l}") for l in layers],
              w_up=[_cast_place([w_up], l, "col", chip1, f"cast_w_up_{l}") for l in layers],
              w_down=[_cast_place([w_down], l, "row", chip1, f"cast_w_down_{l}") for l in layers])
    conv_sh = [jnp.concatenate([conv_a_w, conv_b_w], axis=1), conv_ffn_w]
    small = {n: w[n] for n in REPLICATED + ("norm_final_g",)}

    loss, dx, sgrads, dg_final, pieces = _step(x[0], mem[0], loss_target[0], sh, conv_sh, small, core)

    res = {n: None for n in BIG}
    for l in reversed(range(depth)):
        for n in BIG:
            res[n] = _adam_shard(pieces[l][n], w[n], mom[n], var[n], l, res[n], f"adam_{n}_{l}")

    per_layer = REPLICATED + CONVS
    parts = [sgrads[l][n] for l in range(depth) for n in per_layer] + [dg_final]
    total = _all_reduce_small(_pack(parts), "all_reduce_small")
    shapes = [sgrads[l][n].shape for l in range(depth) for n in per_layer] + [dg_final.shape]
    summed = _unpack(total, shapes)
    g_small = {}
    for k, n in enumerate(per_layer):
        full = jnp.stack([summed[l * len(per_layer) + k] for l in range(depth)])
        if n in CONVS:
            cols = w[n].shape[-1]
            full = lax.dynamic_slice_in_dim(full, chip * cols, cols, axis=2)
        g_small[n] = full.reshape(w[n].shape)
    g_small["norm_final_g"] = summed[-1].reshape(norm_final_g.shape)
    names = per_layer + ("norm_final_g",)
    d_p, m_p, v_p = _adam_small(_pack([g_small[n] for n in names]), _pack([w[n] for n in names]),
                                _pack([mom[n] for n in names]), _pack([var[n] for n in names]), "adam_small")
    shp = [w[n].shape for n in names]
    for n, dl, nm, nv in zip(names, _unpack(d_p, shp), _unpack(m_p, shp), _unpack(v_p, shp)):
        res[n] = (g_small[n], dl, nm, nv)

    loss = lax.psum(loss[0, 0], ("x", "y", "c"))
    return (loss, dx.reshape(x.shape), *[res[n][0] for n in WEIGHTS], *[res[n][1] for n in WEIGHTS],
            *[res[n][2] for n in WEIGHTS], *[res[n][3] for n in WEIGHTS])
```

```python
import functools
import math

import jax
import jax.numpy as jnp
from jax import lax
from jax.experimental import pallas as pl
from jax.experimental.pallas import tpu as pltpu

F32 = jnp.float32
BF = jnp.bfloat16
EPS = 1e-6
N_HEADS = 4
K_A, K_B, K_F = 3, 31, 3
ADAM_LR, ADAM_B1, ADAM_B2, ADAM_EPS, ADAM_WD, ADAM_STEP = 0.001, 0.9, 0.999, 1e-08, 0.01, 10
N_CHIPS = 4
N_DEV = 8
HALO = 32
MAX_STRIPES = 8
TM_ROW = 256
TM_MM = 1024
TT_DW = 4096
DW_LHS_ELEMS = 4 * 1024 * 1024
TR_EW = 128
VMEM_LIMIT = 56 * 1024 * 1024
MESH = pl.DeviceIdType.MESH
_pallas_call = pl.pallas_call


def _params(n_axes):
    return pltpu.CompilerParams(dimension_semantics=("arbitrary",) * n_axes, vmem_limit_bytes=VMEM_LIMIT)


def _resident(shape, index):
    return pl.BlockSpec(shape, lambda *_: index, pipeline_mode=pl.Buffered(1))


def _sig(x):
    return 1.0 / (1.0 + jnp.exp(-x))


def _nt(a, b):
    return lax.dot_general(a, b, (((1,), (1,)), ((), ())), preferred_element_type=F32)


def _tn(a, b):
    return lax.dot_general(a, b, (((0,), (0,)), ((), ())), preferred_element_type=F32)


def _nn(a, b):
    return jnp.dot(a, b, preferred_element_type=F32)


class _Comm:
    def __init__(self, inputs, out_shape, scratch, run, aliases=None):
        self.inputs, self.out_shape, self.scratch, self.run = list(inputs), list(out_shape), list(scratch), run
        self.aliases = dict(aliases or {})


def _join(programs):
    programs = [p for p in programs if p is not None]
    if not programs:
        return None

    def split(seq, counts):
        parts, at = [], 0
        for n in counts:
            parts.append(seq[at:at + n])
            at += n
        return parts

    n_in = [len(p.inputs) for p in programs]
    n_out = [len(p.out_shape) for p in programs]
    n_s = [len(p.scratch) for p in programs]

    def run(phase, ins, outs, sems):
        for p, i, o, s in zip(programs, split(ins, n_in), split(outs, n_out), split(sems, n_s)):
            p.run(phase, i, o, s)

    aliases, in_at, out_at = {}, 0, 0
    for p, i, o in zip(programs, n_in, n_out):
        aliases.update({in_at + a: out_at + b for a, b in p.aliases.items()})
        in_at, out_at = in_at + i, out_at + o
    return _Comm([a for p in programs for a in p.inputs], [a for p in programs for a in p.out_shape],
                 [a for p in programs for a in p.scratch], run, aliases)


def _run(body, comm, *, name, grid, in_specs, out_specs, out_shape, args, scratch_shapes=()):
    n_axes = len(grid)
    if comm is None:
        outs = _pallas_call(body, name=name, grid=grid, in_specs=list(in_specs), out_specs=list(out_specs),
                            out_shape=list(out_shape), scratch_shapes=list(scratch_shapes),
                            compiler_params=_params(n_axes))(*args)
        return list(outs), []
    counts = (len(in_specs), len(comm.inputs), len(out_specs), len(comm.out_shape), len(scratch_shapes),
              len(comm.scratch))

    def hosted(*refs):
        parts, at = [], 0
        for n in counts:
            parts.append(refs[at:at + n])
            at += n
        ins, c_ins, outs, c_outs, scr, c_sems = parts
        step = pl.program_id(0)
        for a in range(1, n_axes):
            step = step * grid[a] + pl.program_id(a)
        total = math.prod(grid)
        late = max(0, total - 1 - max(1, total // 4))

        @pl.when(step == 0)
        def _():
            comm.run("start", c_ins, c_outs, c_sems)

        body(*ins, *outs, *scr)

        @pl.when(step == late)
        def _():
            comm.run("forward", c_ins, c_outs, c_sems)

        @pl.when(step == total - 1)
        def _():
            comm.run("finish", c_ins, c_outs, c_sems)

    any_spec = pl.BlockSpec(memory_space=pl.ANY)
    res = _pallas_call(
        hosted, name=name, grid=grid, in_specs=list(in_specs) + [any_spec] * counts[1],
        out_specs=list(out_specs) + [any_spec] * counts[3], out_shape=list(out_shape) + comm.out_shape,
        scratch_shapes=list(scratch_shapes) + comm.scratch, compiler_params=_params(n_axes),
        input_output_aliases={counts[0] + a: counts[2] + b for a, b in comm.aliases.items()},
    )(*args, *comm.inputs)
    return list(res[:counts[2]]), list(res[counts[2]:])


def _run_comm(comm, name):
    n_in, n_out = len(comm.inputs), len(comm.out_shape)

    def body(*refs):
        ins, outs, sems = refs[:n_in], refs[n_in:n_in + n_out], refs[n_in + n_out:]
        for phase in ("start", "forward", "finish"):
            comm.run(phase, ins, outs, sems)

    any_spec = pl.BlockSpec(memory_space=pl.ANY)
    return list(_pallas_call(body, name=name, in_specs=[any_spec] * n_in, out_specs=[any_spec] * n_out,
                             out_shape=comm.out_shape, scratch_shapes=comm.scratch,
                             input_output_aliases=comm.aliases)(*comm.inputs))


SUBLANES = 8
LANES = 128
ROW_CHUNK = 128
ALL_RESIDUES = tuple(range(1, SUBLANES))
SHORT_RESIDUES = tuple(sorted({(HALO - K_A + 1 + k) % SUBLANES for k in range(K_A)} - {0}))


class _Rows:
    def __init__(self, ref, shifted_ref=None, residues=()):
        self.ref, self.shifted_ref, self.residues = ref, shifted_ref, tuple(residues)

    def shift(self):
        n = self.shifted_ref.shape[1]
        for j, b in enumerate(self.residues):
            self.shifted_ref[j] = self.ref[pl.ds(b, n), :]

    def at(self, offset, r0, c0, rows):
        b = offset % SUBLANES
        if b in self.residues:
            return self.shifted_ref[self.residues.index(b), pl.ds(offset - b + r0, rows), pl.ds(c0, LANES)]
        return self.ref[pl.ds(offset + r0, rows), pl.ds(c0, LANES)]


def _shifted_scratch(residues, tm, c):
    return pltpu.VMEM((len(residues), tm + HALO - SUBLANES, c), F32)


def _causal_offsets(k_taps):
    return [HALO - k_taps + 1 + k for k in range(k_taps)]


def _anticausal_offsets(k_taps):
    return [k_taps - 1 - k for k in range(k_taps)]


def _tap_chunk(src, w_ref, offsets, r0, c0, rows):
    acc = w_ref[0:1, pl.ds(c0, LANES)] * src.at(offsets[0], r0, c0, rows)
    for k in range(1, len(offsets)):
        acc = acc + w_ref[k:k + 1, pl.ds(c0, LANES)] * src.at(offsets[k], r0, c0, rows)
    return acc


def _conv_whole(ref, w_ref, offsets, tm):
    acc = w_ref[0:1, :] * ref[pl.ds(offsets[0], tm), :]
    for k in range(1, len(offsets)):
        acc = acc + w_ref[k:k + 1, :] * ref[pl.ds(offsets[k], tm), :]
    return acc


def _tap_grads_whole(dw_ref, dy, ref, offsets, tm):
    for k, off in enumerate(offsets):
        dw_ref[k:k + 1, :] += jnp.sum(dy * ref[pl.ds(off, tm), :], axis=0, keepdims=True)


def _conv_taps(src, w_ref, offsets, tm, emit):
    rows = min(ROW_CHUNK, tm)
    for c0 in range(0, w_ref.shape[1], LANES):
        for r0 in range(0, tm, rows):
            emit(r0, rows, c0, _tap_chunk(src, w_ref, offsets, r0, c0, rows))


def _tap_grads(dw_ref, dy_at, src, offsets, tm):
    rows = min(ROW_CHUNK // 2, tm)
    for c0 in range(0, dw_ref.shape[1], LANES):
        acc = [None] * len(offsets)
        for r0 in range(0, tm, rows):
            dy = dy_at(r0, rows, c0)
            for k, off in enumerate(offsets):
                part = (dy * src.at(off, r0, c0, rows)).reshape(rows // SUBLANES, SUBLANES, LANES).sum(axis=0)
                acc[k] = part if acc[k] is None else acc[k] + part
        for k in range(len(offsets)):
            dw_ref[k:k + 1, pl.ds(c0, LANES)] += jnp.sum(acc[k], axis=0, keepdims=True)


def _prev_halo(tm, col):
    return lambda i: (jnp.maximum(i * (tm // HALO) - 1, 0), col)


def _next_halo(tm, n_rows, col):
    return lambda i: (jnp.minimum((i + 1) * (tm // HALO), n_rows // HALO - 1), col)


def _norm_matmul(x, g, w, gs, name, comm=None):
    t, d = x.shape
    n_s, _, ns = w.shape
    n = n_s * ns
    tm = min(TM_MM, t)

    def body(x_ref, g_ref, w_ref, h_ref, y_ref):
        @pl.when(pl.program_id(1) == 0)
        def _():
            xf = x_ref[...]
            r = lax.rsqrt(jnp.mean(xf * xf, axis=-1, keepdims=True) + EPS)
            h_ref[...] = ((xf * r) * g_ref[...]).astype(BF)

        for s in range(gs):
            y_ref[:, s * ns:(s + 1) * ns] = _nn(h_ref[...], w_ref[s]).astype(BF)

    return _run(
        body, comm, name=name, grid=(t // tm, n_s // gs),
        in_specs=[pl.BlockSpec((tm, d), lambda i, j: (i, 0)),
                  pl.BlockSpec((1, d), lambda i, j: (0, 0)),
                  pl.BlockSpec((gs, d, ns), lambda i, j: (j, 0, 0))],
        out_specs=[pl.BlockSpec((tm, d), lambda i, j: (i, 0)),
                   pl.BlockSpec((tm, gs * ns), lambda i, j: (i, j))],
        out_shape=[jax.ShapeDtypeStruct((t, d), BF), jax.ShapeDtypeStruct((t, n), BF)],
        args=(x, g, w))


def _mem_kv(mem, g, w_kv, name):
    m, d = mem.shape
    n_s, _, ns = w_kv.shape

    def body(mem_ref, g_ref, w_ref, memn_ref, kv_ref):
        xf = mem_ref[...]
        r = lax.rsqrt(jnp.mean(xf * xf, axis=-1, keepdims=True) + EPS)
        memn = ((xf * r) * g_ref[...]).astype(BF)
        memn_ref[...] = memn
        for s in range(n_s):
            kv_ref[:, s * ns:(s + 1) * ns] = _nn(memn, w_ref[s]).astype(BF)

    return _pallas_call(
        body, name=name, grid=(1,),
        in_specs=[pl.BlockSpec((m, d), lambda i: (0, 0)),
                  pl.BlockSpec((1, d), lambda i: (0, 0)),
                  pl.BlockSpec((n_s, d, ns), lambda i: (0, 0, 0))],
        out_specs=[pl.BlockSpec((m, d), lambda i: (0, 0)),
                   pl.BlockSpec((m, 2 * d), lambda i: (0, 0))],
        out_shape=[jax.ShapeDtypeStruct((m, d), BF), jax.ShapeDtypeStruct((m, 2 * d), BF)],
        compiler_params=_params(1),
    )(mem, g, w_kv)


def _load_branch_inputs(i, proj_ref, gch_ref, vh_ref, u0h_ref, ugh_ref, xa_ref, xb_ref, d):
    gc = proj_ref[:, d:2 * d].astype(F32)
    v = proj_ref[:, 2 * d:3 * d].astype(F32)
    u0 = proj_ref[:, 3 * d:4 * d].astype(F32)
    ug = proj_ref[:, 4 * d:5 * d].astype(F32)
    keep = (i > 0).astype(F32)
    xa_ref[pl.ds(0, HALO), :] = gch_ref[...].astype(F32) * vh_ref[...].astype(F32) * keep
    xa_ref[pl.ds(HALO, gc.shape[0]), :] = gc * v
    xb_ref[pl.ds(0, HALO), :] = u0h_ref[...].astype(F32) * _sig(ugh_ref[...].astype(F32)) * keep
    xb_ref[pl.ds(HALO, gc.shape[0]), :] = u0 * _sig(ug)


def _softmax_rows(s):
    e = jnp.exp(s - jnp.max(s, axis=-1, keepdims=True))
    return e / jnp.sum(e, axis=-1, keepdims=True)


def _mixer_fwd(proj, x, kv, conv_a, conv_b, cbias, ln_g, ln_b, b_gate, wsq, l, name, comm=None):
    t, d = x.shape
    m = kv.shape[0]
    tm = min(TM_ROW, t)
    hd = d // N_HEADS
    scale = 1.0 / math.sqrt(hd)

    def body(proj_ref, gch_ref, vh_ref, u0h_ref, ugh_ref, x_ref, kv_ref, ca_w, cb_w, cbias_ref, lng_ref, lnb_ref,
             bg_ref, wa_ref, wb_ref, wc_ref, wo_ref,
             x1_ref, za_ref, zb_ref, o_ref, ya_ref, yb_ref, yc_ref, mg_ref, cb_ref, xa_ref, xb_ref, sb_ref):
        i = pl.program_id(0)
        _load_branch_inputs(i, proj_ref, gch_ref, vh_ref, u0h_ref, ugh_ref, xa_ref, xb_ref, d)
        xb = _Rows(xb_ref, sb_ref, ALL_RESIDUES)
        xb.shift()
        def put_za(r0, rows, c0, ca):
            gb = proj_ref[pl.ds(r0, rows), pl.ds(c0, LANES)].astype(F32)
            za_ref[pl.ds(r0, rows), pl.ds(c0, LANES)] = (gb * ca).astype(BF)

        _conv_taps(_Rows(xa_ref), ca_w, _causal_offsets(K_A), tm, put_za)
        ya = _nn(za_ref[...], wa_ref[...])
        ya_ref[...] = ya.astype(BF)

        def put_cb(r0, rows, c0, conv):
            cb_ref[pl.ds(r0, rows), pl.ds(c0, LANES)] = conv + cbias_ref[:, pl.ds(c0, LANES)]

        _conv_taps(xb, cb_w, _causal_offsets(K_B), tm, put_cb)
        cb = cb_ref[...]
        mu = jnp.mean(cb, axis=-1, keepdims=True)
        dlt = cb - mu
        rstd = lax.rsqrt(jnp.mean(dlt * dlt, axis=-1, keepdims=True) + EPS)
        lnb = (dlt * rstd) * lng_ref[...] + lnb_ref[...]
        zb = (lnb * _sig(lnb)).astype(BF)
        zb_ref[...] = zb
        yb = _nn(zb, wb_ref[...])
        yb_ref[...] = yb.astype(BF)
        for h in range(N_HEADS):
            qh = proj_ref[:, 5 * d + h * hd:5 * d + (h + 1) * hd]
            kh = kv_ref[:, h * hd:(h + 1) * hd]
            vh = kv_ref[:, d + h * hd:d + (h + 1) * hd]
            p = _softmax_rows(_nt(qh, kh) * scale)
            o_ref[:, h * hd:(h + 1) * hd] = _nn(p.astype(BF), vh).astype(BF)
        yc = _nn(o_ref[...], wc_ref[...])
        yc_ref[...] = yc.astype(BF)
        g0 = _sig(proj_ref[:, 6 * d:7 * d].astype(F32) + bg_ref[:, 0:d])
        g1 = _sig(proj_ref[:, 7 * d:8 * d].astype(F32) + bg_ref[:, d:2 * d])
        g2 = _sig(proj_ref[:, 8 * d:9 * d].astype(F32) + bg_ref[:, 2 * d:3 * d])
        mg = (g0 * ya + g1 * yb + g2 * yc).astype(BF)
        mg_ref[...] = mg
        x1_ref[...] = x_ref[...] + _nn(mg, wo_ref[...])

    row = lambda w_: pl.BlockSpec((tm, w_), lambda i: (i, 0))
    halo = lambda col: pl.BlockSpec((HALO, d), _prev_halo(tm, col))
    sq = lambda which: _resident((None, None, d, d), (l, which, 0, 0))
    act = jax.ShapeDtypeStruct((t, d), BF)
    return _run(
        body, comm, name=name, grid=(t // tm,),
        in_specs=[row(9 * d), halo(1), halo(2), halo(3), halo(4), row(d),
                  _resident((m, 2 * d), (0, 0)),
                  _resident((None, K_A, d), (l, 0, 0)), _resident((None, K_B, d), (l, 0, 0)),
                  _resident((1, d), (0, 0)), _resident((1, d), (0, 0)), _resident((1, d), (0, 0)),
                  _resident((1, 3 * d), (0, 0)), sq(0), sq(1), sq(2), sq(3)],
        out_specs=[row(d)] * 9,
        out_shape=[jax.ShapeDtypeStruct((t, d), F32)] + [act] * 7 + [jax.ShapeDtypeStruct((t, d), F32)],
        scratch_shapes=[pltpu.VMEM((HALO + tm, d), F32), pltpu.VMEM((HALO + tm, d), F32),
                        _shifted_scratch(ALL_RESIDUES, tm, d)],
        args=(proj, proj, proj, proj, proj, x, kv, conv_a, conv_b, cbias, ln_g, ln_b, b_gate, wsq, wsq, wsq, wsq))


def _ffn_down_fwd(up, x1, conv_f, w_down, l, name, comm=None):
    t, d = x1.shape
    f2 = up.shape[1]
    f = f2 // 2
    tm = min(TM_ROW, t)

    def body(up_ref, uph_ref, x1_ref, cw_ref, wd_ref, x2_ref, zf_ref, uc_ref, xx_ref):
        i = pl.program_id(0)
        xx_ref[pl.ds(0, HALO), :] = uph_ref[...].astype(F32) * (i > 0).astype(F32)
        xx_ref[pl.ds(HALO, tm), :] = up_ref[...].astype(F32)
        uc = _conv_whole(xx_ref, cw_ref, _causal_offsets(K_F), tm)
        uc_ref[...] = uc.astype(BF)
        gt = uc[:, 0:f]
        zf = (gt * _sig(gt) * uc[:, f:f2]).astype(BF)
        zf_ref[...] = zf
        x2_ref[...] = x1_ref[...] + _nn(zf, wd_ref[...])

    return _run(
        body, comm, name=name, grid=(t // tm,),
        in_specs=[pl.BlockSpec((tm, f2), lambda i: (i, 0)),
                  pl.BlockSpec((HALO, f2), _prev_halo(tm, 0)),
                  pl.BlockSpec((tm, d), lambda i: (i, 0)),
                  _resident((None, K_F, f2), (l, 0, 0)),
                  _resident((None, f, d), (l, 0, 0))],
        out_specs=[pl.BlockSpec((tm, d), lambda i: (i, 0)), pl.BlockSpec((tm, f), lambda i: (i, 0)),
                   pl.BlockSpec((tm, f2), lambda i: (i, 0))],
        out_shape=[jax.ShapeDtypeStruct((t, d), F32), jax.ShapeDtypeStruct((t, f), BF),
                   jax.ShapeDtypeStruct((t, f2), BF)],
        scratch_shapes=[pltpu.VMEM((HALO + tm, f2), F32)],
        args=(up, up, x1, conv_f, w_down))


def _final_loss(x, g, target, name):
    t, d = x.shape
    tm = min(2 * TM_ROW, t)

    def body(x_ref, g_ref, t_ref, dx_ref, loss_ref, dg_ref):
        @pl.when(pl.program_id(0) == 0)
        def _():
            loss_ref[...] = jnp.zeros_like(loss_ref)
            dg_ref[...] = jnp.zeros_like(dg_ref)

        xf = x_ref[...]
        r = lax.rsqrt(jnp.mean(xf * xf, axis=-1, keepdims=True) + EPS)
        xhat = xf * r
        err = xhat * g_ref[...] - t_ref[...]
        loss_ref[...] += (0.5 / d) * jnp.sum(err * err)
        dy = err * (1.0 / d)
        dg_ref[...] += jnp.sum(dy * xhat, axis=0, keepdims=True)
        dxh = dy * g_ref[...]
        dx_ref[...] = r * (dxh - xhat * jnp.mean(dxh * xhat, axis=-1, keepdims=True))

    return _pallas_call(
        body, name=name, grid=(t // tm,),
        in_specs=[pl.BlockSpec((tm, d), lambda i: (i, 0)), pl.BlockSpec((1, d), lambda i: (0, 0)),
                  pl.BlockSpec((tm, d), lambda i: (i, 0))],
        out_specs=[pl.BlockSpec((tm, d), lambda i: (i, 0)), pl.BlockSpec((8, 128), lambda i: (0, 0)),
                   pl.BlockSpec((1, d), lambda i: (0, 0))],
        out_shape=[jax.ShapeDtypeStruct((t, d), F32), jax.ShapeDtypeStruct((8, 128), F32),
                   jax.ShapeDtypeStruct((1, d), F32)],
        compiler_params=_params(1),
    )(x, g, target)


def _ffn_down_bwd(dx2, up, uc, w_down, l, name, comm=None):
    t, d = dx2.shape
    f2 = up.shape[1]
    f = f2 // 2
    tm = min(TM_ROW, t)

    def body(dx2_ref, up_ref, uph_ref, uc_ref, wd_ref, duc_ref, dx2b_ref, dcw_ref, xx_ref):
        i = pl.program_id(0)

        @pl.when(i == 0)
        def _():
            dcw_ref[...] = jnp.zeros_like(dcw_ref)

        xx_ref[pl.ds(0, HALO), :] = uph_ref[...].astype(F32) * (i > 0).astype(F32)
        xx_ref[pl.ds(HALO, tm), :] = up_ref[...].astype(F32)
        gt = uc_ref[:, 0:f].astype(F32)
        sg = _sig(gt)
        dx2b = dx2_ref[...].astype(BF)
        dx2b_ref[...] = dx2b
        dzf = _nt(dx2b, wd_ref[...])
        duc_ref[:, 0:f] = (dzf * uc_ref[:, f:f2].astype(F32) * (sg * (1.0 + gt * (1.0 - sg)))).astype(BF)
        duc_ref[:, f:f2] = (dzf * (gt * sg)).astype(BF)
        _tap_grads_whole(dcw_ref, duc_ref[...].astype(F32), xx_ref, _causal_offsets(K_F), tm)

    return _run(
        body, comm, name=name, grid=(t // tm,),
        in_specs=[pl.BlockSpec((tm, d), lambda i: (i, 0)),
                  pl.BlockSpec((tm, f2), lambda i: (i, 0)),
                  pl.BlockSpec((HALO, f2), _prev_halo(tm, 0)),
                  pl.BlockSpec((tm, f2), lambda i: (i, 0)),
                  _resident((None, f, d), (l, 0, 0))],
        out_specs=[pl.BlockSpec((tm, f2), lambda i: (i, 0)), pl.BlockSpec((tm, d), lambda i: (i, 0)),
                   pl.BlockSpec((K_F, f2), lambda i: (0, 0))],
        out_shape=[jax.ShapeDtypeStruct((t, f2), BF), jax.ShapeDtypeStruct((t, d), BF),
                   jax.ShapeDtypeStruct((K_F, f2), F32)],
        scratch_shapes=[pltpu.VMEM((HALO + tm, f2), F32)],
        args=(dx2, up, up, uc, w_down))


def _ffn_conv_bwd(duc, conv_f, l, name, comm=None):
    t, f2 = duc.shape
    tm = min(TM_ROW, t)
    n_t = t // tm

    def body(duc_ref, nxt_ref, cw_ref, dup_ref, yy_ref):
        i = pl.program_id(0)
        yy_ref[pl.ds(0, tm), :] = duc_ref[...].astype(F32)
        yy_ref[pl.ds(tm, HALO), :] = nxt_ref[...].astype(F32) * (i < n_t - 1).astype(F32)

        def put(r0, rows, c0, conv):
            dup_ref[pl.ds(r0, rows), pl.ds(c0, LANES)] = conv.astype(BF)

        _conv_taps(_Rows(yy_ref), cw_ref, _anticausal_offsets(K_F), tm, put)

    return _run(
        body, comm, name=name, grid=(n_t,),
        in_specs=[pl.BlockSpec((tm, f2), lambda i: (i, 0)),
                  pl.BlockSpec((HALO, f2), _next_halo(tm, t, 0)),
                  _resident((None, K_F, f2), (l, 0, 0))],
        out_specs=[pl.BlockSpec((tm, f2), lambda i: (i, 0))],
        out_shape=[jax.ShapeDtypeStruct((t, f2), BF)],
        scratch_shapes=[pltpu.VMEM((tm + HALO, f2), F32)],
        args=(duc, duc, conv_f))


def _nt_matmul_norm_bwd(dy, w, gs, x, g, dres, name, comm=None):
    t, n = dy.shape
    d = x.shape[1]
    tm = min(TM_MM // 2, t)
    n_s, _, ns = w.shape
    n_k, tk = n_s // gs, gs * ns

    def body(dy_ref, w_ref, x_ref, g_ref, dres_ref, dx_ref, dxb_ref, dg_ref, acc_ref):
        i, k = pl.program_id(0), pl.program_id(1)

        @pl.when((i == 0) & (k == 0))
        def _():
            dg_ref[...] = jnp.zeros_like(dg_ref)

        @pl.when(k == 0)
        def _():
            acc_ref[...] = jnp.zeros_like(acc_ref)

        part = _nt(dy_ref[:, 0:ns], w_ref[0])
        for s in range(1, gs):
            part = part + _nt(dy_ref[:, s * ns:(s + 1) * ns], w_ref[s])
        acc_ref[...] += part

        @pl.when(k == n_k - 1)
        def _():
            xf = x_ref[...]
            r = lax.rsqrt(jnp.mean(xf * xf, axis=-1, keepdims=True) + EPS)
            xhat = xf * r
            dh = acc_ref[...]
            dg_ref[...] += jnp.sum(dh * xhat, axis=0, keepdims=True)
            dxh = dh * g_ref[...]
            dx = dres_ref[...] + r * (dxh - xhat * jnp.mean(dxh * xhat, axis=-1, keepdims=True))
            dx_ref[...] = dx
            dxb_ref[...] = dx.astype(BF)

    return _run(
        body, comm, name=name, grid=(t // tm, n_k),
        in_specs=[pl.BlockSpec((tm, tk), lambda i, k: (i, k)),
                  pl.BlockSpec((gs, d, ns), lambda i, k: (k, 0, 0)),
                  pl.BlockSpec((tm, d), lambda i, k: (i, 0)),
                  pl.BlockSpec((1, d), lambda i, k: (0, 0)),
                  pl.BlockSpec((tm, d), lambda i, k: (i, 0))],
        out_specs=[pl.BlockSpec((tm, d), lambda i, k: (i, 0)), pl.BlockSpec((tm, d), lambda i, k: (i, 0)),
                   pl.BlockSpec((1, d), lambda i, k: (0, 0))],
        out_shape=[jax.ShapeDtypeStruct((t, d), F32), jax.ShapeDtypeStruct((t, d), BF),
                   jax.ShapeDtypeStruct((1, d), F32)],
        scratch_shapes=[pltpu.VMEM((tm, d), F32)],
        args=(dy, w, x, g, dres))


def _mixer_bwd(dx1b, proj, ya, yb, yc, cb, kv, conv_a, conv_b, ln_g, ln_b, b_gate, wsq, l, name, comm=None):
    t, d = cb.shape
    m = kv.shape[0]
    tm = min(TM_ROW, t)
    hd = d // N_HEADS
    scale = 1.0 / math.sqrt(hd)

    def body(dx1b_ref, proj_ref, gch_ref, vh_ref, u0h_ref, ugh_ref, ya_ref, yb_ref, yc_ref, cb_ref, kv_ref,
             ca_w, cb_w, lng_ref, lnb_ref, bg_ref, wa_ref, wb_ref, wc_ref, wo_ref,
             dpre_ref, dya_ref, dyb_ref, dyc_ref, dkv_ref, dbg_ref, dlng_ref, dlnb_ref, dcbias_ref, dcaw_ref,
             dcbw_ref, xa_ref, xb_ref, sa_ref, sb_ref):
        i = pl.program_id(0)

        @pl.when(i == 0)
        def _():
            for ref in (dkv_ref, dbg_ref, dlng_ref, dlnb_ref, dcbias_ref, dcaw_ref, dcbw_ref):
                ref[...] = jnp.zeros_like(ref)

        _load_branch_inputs(i, proj_ref, gch_ref, vh_ref, u0h_ref, ugh_ref, xa_ref, xb_ref, d)
        xa, xb = _Rows(xa_ref, sa_ref, SHORT_RESIDUES), _Rows(xb_ref, sb_ref, ALL_RESIDUES)
        xa.shift()
        xb.shift()
        dmg = _nt(dx1b_ref[...], wo_ref[...])
        ys = (ya_ref, yb_ref, yc_ref)
        dys = (dya_ref, dyb_ref, dyc_ref)
        for b in range(3):
            gate = _sig(proj_ref[:, (6 + b) * d:(7 + b) * d].astype(F32) + bg_ref[:, b * d:(b + 1) * d])
            dys[b][...] = (gate * dmg).astype(BF)
            dpg = dmg * ys[b][...].astype(F32) * gate * (1.0 - gate)
            dpre_ref[:, (6 + b) * d:(7 + b) * d] = dpg.astype(BF)
            dbg_ref[:, b * d:(b + 1) * d] += jnp.sum(dpg, axis=0, keepdims=True)
        gb = proj_ref[:, 0:d].astype(F32)
        dza = _nt(dya_ref[...], wa_ref[...])

        def put_dgb(r0, rows, c0, ca):
            dpre_ref[pl.ds(r0, rows), pl.ds(c0, LANES)] = (dza[r0:r0 + rows, c0:c0 + LANES] * ca).astype(BF)

        _conv_taps(xa, ca_w, _causal_offsets(K_A), tm, put_dgb)
        dpre_ref[:, d:2 * d] = (dza * gb).astype(BF)
        _tap_grads(dcaw_ref, lambda r0, rows, c0: dpre_ref[pl.ds(r0, rows), pl.ds(d + c0, LANES)].astype(F32),
                   xa, _causal_offsets(K_A), tm)
        dpre_ref[:, 2 * d:3 * d] = jnp.zeros((tm, d), BF)
        cbv = cb_ref[...]
        mu = jnp.mean(cbv, axis=-1, keepdims=True)
        dlt = cbv - mu
        rstd = lax.rsqrt(jnp.mean(dlt * dlt, axis=-1, keepdims=True) + EPS)
        xhat = dlt * rstd
        lnb = xhat * lng_ref[...] + lnb_ref[...]
        sg = _sig(lnb)
        dzb = _nt(dyb_ref[...], wb_ref[...])
        dl = dzb * (sg * (1.0 + lnb * (1.0 - sg)))
        dlng_ref[...] += jnp.sum(dl * xhat, axis=0, keepdims=True)
        dlnb_ref[...] += jnp.sum(dl, axis=0, keepdims=True)
        dxh = dl * lng_ref[...]
        dcb = rstd * (dxh - jnp.mean(dxh, axis=-1, keepdims=True)
                      - xhat * jnp.mean(dxh * xhat, axis=-1, keepdims=True))
        dcbias_ref[...] += jnp.sum(dcb, axis=0, keepdims=True)
        dpre_ref[:, 3 * d:4 * d] = dcb.astype(BF)
        _tap_grads(dcbw_ref, lambda r0, rows, c0: dpre_ref[pl.ds(r0, rows), pl.ds(3 * d + c0, LANES)].astype(F32),
                   xb, _causal_offsets(K_B), tm)
        dpre_ref[:, 4 * d:5 * d] = jnp.zeros((tm, d), BF)
        do = _nt(dyc_ref[...], wc_ref[...]).astype(BF)
        for h in range(N_HEADS):
            qh = proj_ref[:, 5 * d + h * hd:5 * d + (h + 1) * hd]
            kh = kv_ref[:, h * hd:(h + 1) * hd]
            vh = kv_ref[:, d + h * hd:d + (h + 1) * hd]
            doh = do[:, h * hd:(h + 1) * hd]
            p = _softmax_rows(_nt(qh, kh) * scale)
            dp = _nt(doh, vh)
            ds = (p * (dp - jnp.sum(dp * p, axis=-1, keepdims=True)) * scale).astype(BF)
            dpre_ref[:, 5 * d + h * hd:5 * d + (h + 1) * hd] = _nn(ds, kh).astype(BF)
            dkv_ref[:, h * hd:(h + 1) * hd] += _tn(ds, qh)
            dkv_ref[:, d + h * hd:d + (h + 1) * hd] += _tn(p.astype(BF), doh)

    row = lambda w_: pl.BlockSpec((tm, w_), lambda i: (i, 0))
    halo = lambda col: pl.BlockSpec((HALO, d), _prev_halo(tm, col))
    sq = lambda which: _resident((None, None, d, d), (l, which, 0, 0))
    acc = lambda r, c: pl.BlockSpec((r, c), lambda i: (0, 0))
    act = jax.ShapeDtypeStruct((t, d), BF)
    vec = lambda r, c: jax.ShapeDtypeStruct((r, c), F32)
    return _run(
        body, comm, name=name, grid=(t // tm,),
        in_specs=[row(d), row(9 * d), halo(1), halo(2), halo(3), halo(4), row(d), row(d), row(d), row(d),
                  _resident((m, 2 * d), (0, 0)),
                  _resident((None, K_A, d), (l, 0, 0)), _resident((None, K_B, d), (l, 0, 0)),
                  _resident((1, d), (0, 0)), _resident((1, d), (0, 0)), _resident((1, 3 * d), (0, 0)),
                  sq(0), sq(1), sq(2), sq(3)],
        out_specs=[row(9 * d), row(d), row(d), row(d), acc(m, 2 * d), acc(1, 3 * d), acc(1, d), acc(1, d),
                   acc(1, d), acc(K_A, d), acc(K_B, d)],
        out_shape=[jax.ShapeDtypeStruct((t, 9 * d), BF), act, act, act, vec(m, 2 * d), vec(1, 3 * d), vec(1, d),
                   vec(1, d), vec(1, d), vec(K_A, d), vec(K_B, d)],
        scratch_shapes=[pltpu.VMEM((HALO + tm, d), F32), pltpu.VMEM((HALO + tm, d), F32),
                        _shifted_scratch(SHORT_RESIDUES, tm, d), _shifted_scratch(ALL_RESIDUES, tm, d)],
        args=(dx1b, proj, proj, proj, proj, proj, ya, yb, yc, cb, kv, conv_a, conv_b, ln_g, ln_b, b_gate,
              wsq, wsq, wsq, wsq))


def _inproj_conv_bwd(dpre, proj, conv_a, conv_b, l, name, comm=None):
    t, d9 = dpre.shape
    d = d9 // 9
    tm = min(TM_ROW, t)
    n_t = t // tm

    def body(dpre_ref, nxa_ref, nxb_ref, proj_ref, ca_w, cb_w, dproj_ref, ya_ref, yb_ref, sb_ref):
        i = pl.program_id(0)
        keep = (i < n_t - 1).astype(F32)
        ya_ref[pl.ds(0, tm), :] = dpre_ref[:, d:2 * d].astype(F32)
        ya_ref[pl.ds(tm, HALO), :] = nxa_ref[...].astype(F32) * keep
        yb_ref[pl.ds(0, tm), :] = dpre_ref[:, 3 * d:4 * d].astype(F32)
        yb_ref[pl.ds(tm, HALO), :] = nxb_ref[...].astype(F32) * keep
        yb = _Rows(yb_ref, sb_ref, ALL_RESIDUES)
        yb.shift()
        dproj_ref[:, 0:d] = dpre_ref[:, 0:d]
        dproj_ref[:, 5 * d:9 * d] = dpre_ref[:, 5 * d:9 * d]
        def chunk(ref, block, r0, rows, c0):
            return ref.at[pl.ds(r0, rows), pl.ds(block * d + c0, LANES)]

        def put_a(r0, rows, c0, dcv):
            chunk(dproj_ref, 1, r0, rows, c0)[...] = (dcv * chunk(proj_ref, 2, r0, rows, c0)[...].astype(F32)).astype(BF)
            chunk(dproj_ref, 2, r0, rows, c0)[...] = (dcv * chunk(proj_ref, 1, r0, rows, c0)[...].astype(F32)).astype(BF)

        def put_b(r0, rows, c0, dub):
            sg = _sig(chunk(proj_ref, 4, r0, rows, c0)[...].astype(F32))
            u0 = chunk(proj_ref, 3, r0, rows, c0)[...].astype(F32)
            chunk(dproj_ref, 3, r0, rows, c0)[...] = (dub * sg).astype(BF)
            chunk(dproj_ref, 4, r0, rows, c0)[...] = (dub * u0 * sg * (1.0 - sg)).astype(BF)

        _conv_taps(_Rows(ya_ref), ca_w, _anticausal_offsets(K_A), tm, put_a)
        _conv_taps(yb, cb_w, _anticausal_offsets(K_B), tm, put_b)

    return _run(
        body, comm, name=name, grid=(n_t,),
        in_specs=[pl.BlockSpec((tm, d9), lambda i: (i, 0)),
                  pl.BlockSpec((HALO, d), _next_halo(tm, t, 1)),
                  pl.BlockSpec((HALO, d), _next_halo(tm, t, 3)),
                  pl.BlockSpec((tm, d9), lambda i: (i, 0)),
                  _resident((None, K_A, d), (l, 0, 0)), _resident((None, K_B, d), (l, 0, 0))],
        out_specs=[pl.BlockSpec((tm, d9), lambda i: (i, 0))],
        out_shape=[jax.ShapeDtypeStruct((t, d9), BF)],
        scratch_shapes=[pltpu.VMEM((tm + HALO, d), F32), pltpu.VMEM((tm + HALO, d), F32),
                        _shifted_scratch(ALL_RESIDUES, tm, d)],
        args=(dpre, dpre, dpre, proj, conv_a, conv_b))


def _mem_kv_bwd(dkv, memn, mem, g, w_kv, name):
    m, d = mem.shape
    n_s, _, ns = w_kv.shape

    def body(dkv_ref, memn_ref, mem_ref, g_ref, w_ref, dw_ref, dg_ref):
        dkvb = dkv_ref[...].astype(BF)
        dw_ref[...] = _tn(memn_ref[...], dkvb).astype(BF)
        dmemn = _nt(dkvb[:, 0:ns], w_ref[0])
        for s in range(1, n_s):
            dmemn = dmemn + _nt(dkvb[:, s * ns:(s + 1) * ns], w_ref[s])
        xf = mem_ref[...]
        r = lax.rsqrt(jnp.mean(xf * xf, axis=-1, keepdims=True) + EPS)
        dg_ref[...] = jnp.sum(dmemn * (xf * r), axis=0, keepdims=True)

    return _pallas_call(
        body, name=name, grid=(1,),
        in_specs=[pl.BlockSpec((m, 2 * d), lambda i: (0, 0)), pl.BlockSpec((m, d), lambda i: (0, 0)),
                  pl.BlockSpec((m, d), lambda i: (0, 0)), pl.BlockSpec((1, d), lambda i: (0, 0)),
                  pl.BlockSpec((n_s, d, ns), lambda i: (0, 0, 0))],
        out_specs=[pl.BlockSpec((d, 2 * d), lambda i: (0, 0)), pl.BlockSpec((1, d), lambda i: (0, 0))],
        out_shape=[jax.ShapeDtypeStruct((d, 2 * d), BF), jax.ShapeDtypeStruct((1, d), F32)],
        compiler_params=_params(1),
    )(dkv, memn, mem, g, w_kv)


def _dw_matmul(a, b, tn, name, comm=None):
    t, k = a.shape
    n = b.shape[1]
    tt = min(TT_DW, t)
    while tt * k > DW_LHS_ELEMS and tt % 2 == 0:
        tt //= 2
    n_s = t // tt

    def body(a_ref, b_ref, o_ref, acc_ref):
        s = pl.program_id(1)
        part = _tn(a_ref[...], b_ref[...])
        if n_s == 1:
            o_ref[...] = part.astype(BF)
            return

        @pl.when(s == 0)
        def _():
            acc_ref[...] = part

        @pl.when(s > 0)
        def _():
            acc_ref[...] += part

        @pl.when(s == n_s - 1)
        def _():
            o_ref[...] = acc_ref[...].astype(BF)

    return _run(
        body, comm, name=name, grid=(n // tn, n_s),
        in_specs=[pl.BlockSpec((tt, k), lambda j, s: (s, 0)), pl.BlockSpec((tt, tn), lambda j, s: (s, j))],
        out_specs=[pl.BlockSpec((k, tn), lambda j, s: (0, j))],
        out_shape=[jax.ShapeDtypeStruct((k, n), BF)],
        scratch_shapes=[pltpu.VMEM((k, tn) if n_s > 1 else (8, 128), F32)],
        args=(a, b))


class _GradReduce:
    def __init__(self, l, grads, core):
        self.l, self.core, self.names = l, core, tuple(grads)
        self.views = {n: _halves_view(n, g) for n, g in grads.items()}
        self.got, self.sums, self.landing, self.pieces = {}, {}, {}, {}

    def swap_program(self):
        return _swap_program([self.views[n] for n in self.names])

    def swapped(self, outs):
        self.got = dict(zip(self.names, outs))
        for n in self.names:
            self.sums[n], self.landing[n] = _add_halves(self.views[n], self.got[n], self.core, n in COL,
                                                        f"add_halves_{n}_{self.l}")

    def scatter_program(self, names=None):
        names = self.names if names is None else names
        return _scatter_program([self.sums[n] for n in names], [self.landing[n] for n in names], names)

    def scattered(self, outs, names=None):
        self.pieces.update(zip(self.names if names is None else names, outs))


def _step(x, mem, target, sh, conv_sh, small, core):
    depth = len(sh["w_in"])
    d = x.shape[1]
    f2 = sh["w_up"][0].shape[2] * N_CHIPS
    tn_dw_in = min(1024, d)
    tn_dw_sq = max(LANES, d // 4)
    tn_dw_up = f2 // 11 if f2 % (11 * 128) == 0 and f2 // 11 >= 128 else f2
    row = lambda v: v.reshape(1, -1)
    one = lambda a: a[None]

    w_in, cab, cf = _run_comm(_gather_program([sh["w_in"][0]] + conv_sh, ("col", "small", "small")), "gather_first")
    saved = []
    for l in range(depth):
        conv = dict(a=cab[l:l + 1, :K_A], b=cab[l:l + 1, K_A:], f=cf[l:l + 1])
        (h, proj), (wsq, w_kv) = _norm_matmul(
            x, row(small["norm_mix_g"][l]), w_in, 1, f"in_proj_{l}",
            _gather_program([sh["wsq"][l], sh["w_kv"][l]], ("row", "col")))
        memn, kv = _mem_kv(mem, row(small["norm_mem_g"][l]), w_kv, f"mem_kv_{l}")
        (x1, za, zb, o, ya, yb, yc, mg, cb), (w_up, w_down) = _mixer_fwd(
            proj, x, kv, conv["a"], conv["b"], row(small["conv_b_bias"][l]), row(small["ln_b_g"][l]),
            row(small["ln_b_b"][l]), row(small["b_gate"][l]), one(wsq), 0, f"mixer_fwd_{l}",
            _gather_program([sh["w_up"][l], sh["w_down"][l]], ("col", "row")))
        more = l + 1 < depth
        nxt = _gather_program([sh["w_in"][l + 1]], ("col",), (0, 2)) if more else None
        (h2, up), w_in_next = _norm_matmul(x1, row(small["norm_ffn_g"][l]), w_up, 2, f"up_proj_{l}", nxt)
        nxt = _gather_program(w_in_next, ("col",), (1, 2)) if more else None
        (x2, zf, uc), w_in_next = _ffn_down_fwd(up, x1, conv["f"], one(w_down), 0, f"ffn_down_fwd_{l}", nxt)
        saved.append(dict(x=x, x1=x1, memn=memn, kv=kv, h=h, proj=proj, za=za, zb=zb, o=o, ya=ya, yb=yb, yc=yc,
                          mg=mg, cb=cb, h2=h2, up=up, zf=zf, uc=uc, w_in=w_in, wsq=one(wsq), w_kv=w_kv,
                          w_up=w_up, w_down=one(w_down), conv=conv))
        x = x2
        w_in = w_in_next[0] if w_in_next else None
    dx, loss, dg_final = _final_loss(x, row(small["norm_final_g"]), target, "final_loss")
    sgrads, pieces = [None] * depth, [None] * depth
    above = None
    first, second, rest = SQUARES, ("w_up",), ("w_in", "w_kv", "w_down")
    for l in reversed(range(depth)):
        s = saved[l]
        conv = s["conv"]
        bottom = l == 0
        (duc, dx2b, dconv_f), got = _ffn_down_bwd(dx, s["up"], s["uc"], s["w_down"], 0, f"ffn_down_bwd_{l}",
                                                  above and above.swap_program())
        if above:
            above.swapped(got)
        (dup,), got = _ffn_conv_bwd(duc, conv["f"], 0, f"ffn_conv_bwd_{l}", above and above.scatter_program(first))
        if above:
            above.scattered(got, first)
        (dx1, dx1b, dg_ffn), got = _nt_matmul_norm_bwd(
            dup, s["w_up"], 2, s["x1"], row(small["norm_ffn_g"][l]), dx, f"up_proj_bwd_{l}",
            above and above.scatter_program(second))
        if above:
            above.scattered(got, second)
        grads = dict(w_up=_dw_matmul(s["h2"], dup, tn_dw_up, f"dw_up_{l}")[0][0],
                     w_down=_dw_matmul(s["zf"], dx2b, d, f"dw_down_{l}")[0][0])
        ffn = _GradReduce(l, grads, core) if bottom else None
        (dpre, dya, dyb, dyc, dkv, dbg, dlng, dlnb, dcbias, dconv_a, dconv_b), got = _mixer_bwd(
            dx1b, s["proj"], s["ya"], s["yb"], s["yc"], s["cb"], s["kv"], conv["a"], conv["b"],
            row(small["ln_b_g"][l]), row(small["ln_b_b"][l]), row(small["b_gate"][l]), s["wsq"], 0,
            f"mixer_bwd_{l}", _join([above and above.scatter_program(rest), ffn and ffn.swap_program()]))
        if above:
            above.scattered(got[:len(rest)], rest)
            pieces[above.l] = above.pieces
            got = got[len(rest):]
        if ffn:
            ffn.swapped(got)
        dw_kv, dg_mem = _mem_kv_bwd(dkv, s["memn"], mem, row(small["norm_mem_g"][l]), s["w_kv"], f"mem_kv_bwd_{l}")
        mix_grads = dict(w_a_out=_dw_matmul(s["za"], dya, tn_dw_sq, f"dw_a_out_{l}")[0][0],
                         w_b_out=_dw_matmul(s["zb"], dyb, tn_dw_sq, f"dw_b_out_{l}")[0][0],
                         w_att_out=_dw_matmul(s["o"], dyc, tn_dw_sq, f"dw_att_out_{l}")[0][0],
                         w_o=_dw_matmul(s["mg"], dx1b, tn_dw_sq, f"dw_o_{l}")[0][0], w_kv=dw_kv)
        mix = _GradReduce(l, mix_grads, core) if bottom else None
        (dproj,), got = _inproj_conv_bwd(dpre, s["proj"], conv["a"], conv["b"], 0, f"inproj_conv_bwd_{l}",
                                         _join([ffn and ffn.scatter_program(), mix and mix.swap_program()]))
        if bottom:
            ffn.scattered(got[:len(ffn.names)])
            mix.swapped(got[len(ffn.names):])
        in_grads = dict(w_in=_dw_matmul(s["h"], dproj, tn_dw_in, f"dw_in_{l}")[0][0])
        inp = _GradReduce(l, in_grads, core) if bottom else None
        (dx0, _, dg_mix), got = _nt_matmul_norm_bwd(
            dproj, s["w_in"], 2, s["x"], row(small["norm_mix_g"][l]), dx1, f"in_proj_bwd_{l}",
            _join([mix and mix.scatter_program(), inp and inp.swap_program()]))
        if bottom:
            mix.scattered(got[:len(mix.names)])
            inp.swapped(got[len(mix.names):])
            inp.scattered(_run_comm(inp.scatter_program(), f"scatter_w_in_{l}"))
            pieces[l] = {**ffn.pieces, **mix.pieces, **inp.pieces}
        else:
            above = _GradReduce(l, {**grads, **mix_grads, **in_grads}, core)
        sgrads[l] = dict(norm_mix_g=dg_mix, norm_mem_g=dg_mem, b_gate=dbg, conv_b_bias=dcbias, ln_b_g=dlng,
                         ln_b_b=dlnb, norm_ffn_g=dg_ffn, conv_a_w=dconv_a, conv_b_w=dconv_b, conv_ffn_w=dconv_f)
        dx = dx0
    return loss, dx, sgrads, dg_final, pieces


BIG = ("w_in", "w_a_out", "w_b_out", "w_att_out", "w_o", "w_kv", "w_up", "w_down")
COL = ("w_in", "w_kv", "w_up")
SQUARES = ("w_a_out", "w_b_out", "w_att_out", "w_o")
ANY = pl.BlockSpec(memory_space=pl.ANY)


def _place():
    x, y, c = lax.axis_index("x"), lax.axis_index("y"), lax.axis_index("c")
    chips = [(1 - x, y), (x, 1 - y), (1 - x, 1 - y)]
    return x, y, c, 2 * x + y, chips


def _remote(src, dst, send_sem, recv_sem, dev):
    return pltpu.make_async_remote_copy(src_ref=src, dst_ref=dst, send_sem=send_sem, recv_sem=recv_sem,
                                        device_id=dev, device_id_type=MESH)


class _Striped:
    def __init__(self, src, dst, make):
        rows = src.shape[-2]
        unit = 8 * (4 // jnp.dtype(src.dtype).itemsize)
        n = max(k for k in range(1, MAX_STRIPES + 1) if rows % (unit * k) == 0) if rows % unit == 0 else 1
        q = rows // n
        self.parts = [make(_window(src, pl.ds(i * q, q), slice(None)), _window(dst, pl.ds(i * q, q), slice(None)))
                      for i in range(n)]
        self.whole = make(src, dst)

    def start(self):
        for p in self.parts:
            p.start()

    def wait(self):
        self.whole.wait()

    def wait_send(self):
        self.whole.wait_send()

    def wait_recv(self):
        self.whole.wait_recv()


def _far(src, dst, send_sem, recv_sem, dev):
    return _Striped(src, dst, lambda s, d: _remote(s, d, send_sem, recv_sem, dev))


def _near(src, dst, sem):
    return _Striped(src, dst, lambda s, d: pltpu.make_async_copy(s, d, sem))


def _window(ref, rows, cols):
    return ref.at[(slice(None),) * (len(ref.shape) - 2) + (rows, cols)]


def _row_tile(rows, cols, unit=16, limit=1 << 20):
    best = unit
    for tr in range(unit, rows + 1, unit):
        if rows % tr == 0 and tr * cols <= limit:
            best = tr
    return best


def _cast_place(ws, l, kind, chip, name):
    _, k, n = ws[0].shape
    if kind == "col":
        shape, spec = (N_CHIPS, k, n), pl.BlockSpec((None, k, n), lambda i, c: (c[0], 0, 0))
    elif len(ws) == 1:
        shape, spec = (N_CHIPS * k, n), pl.BlockSpec((k, n), lambda i, c: (c[0], 0))
    else:
        shape, spec = (len(ws), N_CHIPS * k, n), pl.BlockSpec((len(ws), k, n), lambda i, c: (0, c[0], 0))

    def body(c_ref, *refs):
        o_ref = refs[-1]
        if len(ws) == 1 or kind == "col":
            o_ref[...] = refs[0][...].astype(BF)
        else:
            for i in range(len(ws)):
                o_ref[i] = refs[i][...].astype(BF)

    return _pallas_call(
        body, name=name,
        grid_spec=pltpu.PrefetchScalarGridSpec(
            num_scalar_prefetch=1, grid=(1,),
            in_specs=[pl.BlockSpec((None, k, n), lambda i, c: (l, 0, 0))] * len(ws), out_specs=spec),
        out_shape=jax.ShapeDtypeStruct(shape, BF),
        compiler_params=_params(1),
    )(chip, *ws)


def _gather_program(arrays, kinds, part=(0, 1)):
    n_t = len(arrays)
    index, count = part

    def shard_rows(f, kind):
        return f.shape[-2] if kind == "col" else f.shape[-2] // N_CHIPS

    def run(phase, ins, full, sems):
        ici_send, ici_recv, sib_send, sib_recv, loc_sem = sems
        x, y, c, me, chips = _place()
        sibling = (x, y, 1 - c)

        def part_of(i, chip, half):
            f, kind = full[i], kinds[i]
            if kind == "small":
                cols = ins[i].shape[-1]
                return _window(f, slice(None), pl.ds(pl.multiple_of(chip * cols, 128), cols))
            rows = shard_rows(f, kind)
            r = rows // (2 * count)
            at = (half * count + index) * r
            if kind == "col":
                return f.at[chip, pl.ds(pl.multiple_of(at, 16), r), :]
            return _window(f, pl.ds(pl.multiple_of(chip * rows + at, 16), r), slice(None))

        src_part = lambda i, half: ins[i] if kinds[i] == "small" else part_of(i, me, half)
        dst_part = part_of
        local = [_near(ins[i], part_of(i, me, c), loc_sem.at[i]) for i in range(n_t) if kinds[i] == "small"]
        sends = []
        for i in range(n_t):
            for j, chip in enumerate(chips):
                sends.append(_far(src_part(i, c), dst_part(i, me, c), ici_send.at[3 * i + j],
                                  ici_recv.at[3 * i + j], (*chip, c)))
        if phase == "start":
            for cp in local + sends:
                cp.start()
            return
        passed = []
        for i in range(n_t):
            for j, chip in enumerate(chips):
                k = 2 * chip[0] + chip[1]
                landed = dst_part(i, k, c)
                if phase == "forward":
                    _remote(landed, landed, ici_send.at[3 * i + j], ici_recv.at[3 * i + j], (*chip, c)).wait_recv()
                if kinds[i] != "small":
                    passed.append(_far(landed, landed, sib_send.at[3 * i + j], sib_recv.at[3 * i + j], sibling))
                    if phase == "forward":
                        passed[-1].start()
        if phase == "forward":
            return
        for i in range(n_t):
            if kinds[i] == "small":
                continue
            for j, chip in enumerate(chips):
                k = 2 * chip[0] + chip[1]
                other = dst_part(i, k, 1 - c)
                _remote(other, other, sib_send.at[3 * i + j], sib_recv.at[3 * i + j], sibling).wait_recv()
        for cp in sends + passed:
            cp.wait_send()
        for cp in local:
            cp.wait()

    def out_shape(a, kind):
        shp = a.shape[:-1] + (a.shape[-1] * N_CHIPS,) if kind == "small" else a.shape
        return jax.ShapeDtypeStruct(shp, a.dtype)

    outs = [out_shape(a, k) for a, k in zip(arrays, kinds)]
    sems = [pltpu.SemaphoreType.DMA((3 * n_t,))] * 4 + [pltpu.SemaphoreType.DMA((n_t,))]
    return _Comm(arrays, outs, sems, run, {i: i for i in range(n_t) if kinds[i] != "small"})


def _all_reduce_small(part, name):
    r, n = part.shape

    def body(in_ref, out_ref, pair_ref, chips_ref, sib_sems, send_sems, recv_sems):
        x, y, c, me, chips = _place()
        pair_ref[c] = in_ref[...]
        swap = _remote(in_ref, pair_ref.at[c], sib_sems.at[0], sib_sems.at[1], (x, y, 1 - c))
        swap.start()
        _remote(in_ref, pair_ref.at[1 - c], sib_sems.at[0], sib_sems.at[1], (x, y, c)).wait_recv()
        chips_ref[me] = pair_ref[0] + pair_ref[1]
        sends = [_remote(chips_ref.at[me], chips_ref.at[me], send_sems.at[j], recv_sems.at[j], (*chip, c))
                 for j, chip in enumerate(chips)]
        for cp in sends:
            cp.start()
        for j, chip in enumerate(chips):
            landed = chips_ref.at[2 * chip[0] + chip[1]]
            _remote(landed, landed, send_sems.at[j], recv_sems.at[j], (x, y, c)).wait_recv()
        for cp in sends:
            cp.wait_send()
        swap.wait_send()
        total = chips_ref[0]
        for k in range(1, N_CHIPS):
            total = total + chips_ref[k]
        out_ref[...] = total

    vm = pl.BlockSpec(memory_space=pltpu.VMEM)
    return _pallas_call(
        body, name=name, in_specs=[vm], out_specs=vm, out_shape=jax.ShapeDtypeStruct((r, n), F32),
        scratch_shapes=[pltpu.VMEM((2, r, n), F32), pltpu.VMEM((N_CHIPS, r, n), F32),
                        pltpu.SemaphoreType.DMA((2,)), pltpu.SemaphoreType.DMA((N_CHIPS - 1,)),
                        pltpu.SemaphoreType.DMA((N_CHIPS - 1,))],
        compiler_params=pltpu.CompilerParams(vmem_limit_bytes=VMEM_LIMIT),
    )(part)


def _halves_view(name, dw):
    k, n = dw.shape
    s = 1 if name in COL else N_CHIPS
    return dw.reshape(s, 2, k // (2 * s), n)


def _swap_program(views):
    n_t = len(views)

    def run(phase, src, dst, sems):
        send_sems, recv_sems = sems
        x, y, c, _, _ = _place()
        copies = [_far(src[i].at[:, 1 - c], dst[i], send_sems.at[i], recv_sems.at[i], (x, y, 1 - c))
                  for i in range(n_t)]
        for cp in copies:
            if phase == "start":
                cp.start()
            elif phase == "finish":
                cp.wait()

    outs = [jax.ShapeDtypeStruct((v.shape[0],) + v.shape[2:], v.dtype) for v in views]
    sems = [pltpu.SemaphoreType.DMA((n_t,)), pltpu.SemaphoreType.DMA((n_t,))]
    return _Comm(views, outs, sems, run)


def _add_halves(view, got, place, col, name):
    s, _, r, n = view.shape
    if col:
        cw = n // N_CHIPS
        tr = _row_tile(r, cw)
        grid = (r // tr, N_CHIPS)
        in_specs = [pl.BlockSpec((None, None, tr, cw), lambda j, q, p: (0, p[0], j, q)),
                    pl.BlockSpec((None, tr, cw), lambda j, q, p: (0, j, q))]
        out_specs = [pl.BlockSpec((None, tr, cw), lambda j, q, p: (0, j, q)),
                     pl.BlockSpec((None, None, tr, cw), lambda j, q, p: (p[0], p[1], j, 0))]
    else:
        cw = n
        tr = _row_tile(r, n)
        grid = (s, r // tr)
        in_specs = [pl.BlockSpec((None, None, tr, n), lambda i, j, p: (i, p[0], j, 0)),
                    pl.BlockSpec((None, tr, n), lambda i, j, p: (i, j, 0))]
        out_specs = [pl.BlockSpec((None, tr, n), lambda i, j, p: (i, j, 0)),
                     pl.BlockSpec((None, None, tr, n), lambda i, j, p: (p[0], i, j, 0))]

    def body(p_ref, a_ref, b_ref, o_ref, z_ref):
        total = (a_ref[...].astype(F32) + b_ref[...].astype(F32)).astype(BF)
        o_ref[...] = total
        if col:
            @pl.when(pl.program_id(1) == p_ref[1])
            def _():
                z_ref[...] = total
        else:
            z_ref[...] = total

    return _pallas_call(
        body, name=name,
        grid_spec=pltpu.PrefetchScalarGridSpec(num_scalar_prefetch=1, grid=grid, in_specs=in_specs,
                                               out_specs=out_specs),
        out_shape=[jax.ShapeDtypeStruct((s, r, n), BF), jax.ShapeDtypeStruct((2, N_CHIPS, r, cw), BF)],
        compiler_params=_params(2),
    )(place, view, got)


def _scatter_program(sums, landing, names):
    n_t = len(sums)

    def run(phase, src, dst, sems):
        ici_send, ici_recv, sib_send, sib_recv = sems
        x, y, c, me, chips = _place()
        sibling = (x, y, 1 - c)

        def piece(i, chip):
            if names[i] in COL:
                cw = src[i].shape[2] // N_CHIPS
                return src[i].at[0, :, pl.ds(pl.multiple_of(chip * cw, 128), cw)]
            return src[i].at[chip]

        local = []
        sends = []
        for i in range(n_t):
            sends.append(_far(piece(i, me), dst[i].at[c, me], sib_send.at[4 * i + 3], sib_recv.at[4 * i + 3],
                              sibling))
            for j, chip in enumerate(chips):
                k = 2 * chip[0] + chip[1]
                sends.append(_far(piece(i, k), dst[i].at[c, me], ici_send.at[3 * i + j], ici_recv.at[3 * i + j],
                                  (*chip, c)))
        if phase == "start":
            for cp in local + sends:
                cp.start()
            return
        passed = []
        for i in range(n_t):
            for j, chip in enumerate(chips):
                k = 2 * chip[0] + chip[1]
                landed = dst[i].at[c, k]
                if phase == "forward":
                    _remote(landed, landed, ici_send.at[3 * i + j], ici_recv.at[3 * i + j], (*chip, c)).wait_recv()
                passed.append(_far(landed, landed, sib_send.at[4 * i + j], sib_recv.at[4 * i + j], sibling))
                if phase == "forward":
                    passed[-1].start()
        if phase == "forward":
            return
        for i in range(n_t):
            other = dst[i].at[1 - c, me]
            _remote(other, other, sib_send.at[4 * i + 3], sib_recv.at[4 * i + 3], sibling).wait_recv()
            for j, chip in enumerate(chips):
                k = 2 * chip[0] + chip[1]
                other = dst[i].at[1 - c, k]
                _remote(other, other, sib_send.at[4 * i + j], sib_recv.at[4 * i + j], sibling).wait_recv()
        for cp in sends + passed:
            cp.wait_send()
        for cp in local:
            cp.wait()

    outs = [jax.ShapeDtypeStruct(z.shape, z.dtype) for z in landing]
    sems = [pltpu.SemaphoreType.DMA((3 * n_t,))] * 2 + [pltpu.SemaphoreType.DMA((4 * n_t,))] * 2
    return _Comm(list(sums) + list(landing), outs, sems, run, {n_t + i: i for i in range(n_t)})


def _adamw(w, g, m, v):
    m = ADAM_B1 * m + (1.0 - ADAM_B1) * g
    v = ADAM_B2 * v + (1.0 - ADAM_B2) * (g * g)
    m_hat = m / (1.0 - ADAM_B1 ** ADAM_STEP)
    v_hat = v / (1.0 - ADAM_B2 ** ADAM_STEP)
    return -ADAM_LR * (m_hat / (jnp.sqrt(v_hat) + ADAM_EPS) + ADAM_WD * w), m, v


def _adam_shard(pieces, w, m, v, l, prev, name):
    depth, rows, cw = w.shape
    hr = rows // 2
    tr = _row_tile(hr, cw, limit=1 << 18)
    n_i = hr // tr

    def body(*refs):
        z_ref, w_ref, m_ref, v_ref = refs[:4]
        g_ref, d_ref, nm_ref, nv_ref = refs[-4:]
        g = z_ref[0].astype(F32)
        for k in range(1, N_CHIPS):
            g = g + z_ref[k].astype(F32)
        g_ref[...] = g
        d_ref[...], nm_ref[...], nv_ref[...] = _adamw(w_ref[...], g, m_ref[...], v_ref[...])

    par = pl.BlockSpec((None, tr, cw), lambda h, i: (l, h * n_i + i, 0))
    out = jax.ShapeDtypeStruct((depth, rows, cw), F32)
    extra = [] if prev is None else list(prev)
    return _pallas_call(
        body, name=name, grid=(2, n_i),
        in_specs=[pl.BlockSpec((None, N_CHIPS, tr, cw), lambda h, i: (h, 0, i, 0)), par, par, par] + [ANY] * len(extra),
        out_specs=[par] * 4, out_shape=[out] * 4,
        input_output_aliases={4 + k: k for k in range(len(extra))},
        compiler_params=_params(2),
    )(pieces, w, m, v, *extra)


def _adam_small(gs, ws, ms, vs, name):
    n = len(gs)

    def body(*refs):
        g, w, m, v, d, nm, nv = (refs[k * n:(k + 1) * n] for k in range(7))
        for i in range(n):
            d[i][...], nm[i][...], nv[i][...] = _adamw(w[i][...], g[i][...], m[i][...], v[i][...])

    vm = pl.BlockSpec(memory_space=pltpu.VMEM)
    outs = [jax.ShapeDtypeStruct(a.shape, F32) for a in ws] * 3
    res = _pallas_call(body, name=name, in_specs=[vm] * (4 * n), out_specs=[vm] * (3 * n), out_shape=outs)(
        *gs, *ws, *ms, *vs)
    return res[:n], res[n:2 * n], res[2 * n:]


WEIGHTS = ("norm_mix_g", "norm_mem_g", "w_in", "b_gate", "conv_a_w", "w_a_out", "conv_b_w", "conv_b_bias", "ln_b_g",
           "ln_b_b", "w_b_out", "w_kv", "w_att_out", "w_o", "norm_ffn_g", "w_up", "conv_ffn_w", "w_down",
           "norm_final_g")
REPLICATED = ("norm_mix_g", "norm_mem_g", "b_gate", "conv_b_bias", "ln_b_g", "ln_b_b", "norm_ffn_g")
CONVS = ("conv_a_w", "conv_b_w", "conv_ffn_w")
PACK_WIDTH = 1024


def _pack(arrays):
    flat = jnp.concatenate([a.reshape(-1) for a in arrays])
    size = -(-flat.shape[0] // (8 * PACK_WIDTH)) * (8 * PACK_WIDTH)
    return jnp.pad(flat, (0, size - flat.shape[0])).reshape(-1, PACK_WIDTH)


def _unpack(packed, shapes):
    flat = packed.reshape(-1)
    out, at = [], 0
    for shp in shapes:
        n = math.prod(shp)
        out.append(flat[at:at + n].reshape(shp))
        at += n
    return out


def kernel(x, mem, norm_mix_g, norm_mem_g, w_in, b_gate, conv_a_w, w_a_out, conv_b_w, conv_b_bias, ln_b_g, ln_b_b, w_b_out, w_kv, w_att_out, w_o, norm_ffn_g, w_up, conv_ffn_w, w_down, norm_final_g, loss_target, m_norm_mix_g, m_norm_mem_g, m_w_in, m_b_gate, m_conv_a_w, m_w_a_out, m_conv_b_w, m_conv_b_bias, m_ln_b_g, m_ln_b_b, m_w_b_out, m_w_kv, m_w_att_out, m_w_o, m_norm_ffn_g, m_w_up, m_conv_ffn_w, m_w_down, m_norm_final_g, v_norm_mix_g, v_norm_mem_g, v_w_in, v_b_gate, v_conv_a_w, v_w_a_out, v_conv_b_w, v_conv_b_bias, v_ln_b_g, v_ln_b_b, v_w_b_out, v_w_kv, v_w_att_out, v_w_o, v_norm_ffn_g, v_w_up, v_conv_ffn_w, v_w_down, v_norm_final_g):
    w = dict(norm_mix_g=norm_mix_g, norm_mem_g=norm_mem_g, w_in=w_in, b_gate=b_gate, conv_a_w=conv_a_w,
             w_a_out=w_a_out, conv_b_w=conv_b_w, conv_b_bias=conv_b_bias, ln_b_g=ln_b_g, ln_b_b=ln_b_b,
             w_b_out=w_b_out, w_kv=w_kv, w_att_out=w_att_out, w_o=w_o, norm_ffn_g=norm_ffn_g, w_up=w_up,
             conv_ffn_w=conv_ffn_w, w_down=w_down, norm_final_g=norm_final_g)
    mom = dict(norm_mix_g=m_norm_mix_g, norm_mem_g=m_norm_mem_g, w_in=m_w_in, b_gate=m_b_gate, conv_a_w=m_conv_a_w,
               w_a_out=m_w_a_out, conv_b_w=m_conv_b_w, conv_b_bias=m_conv_b_bias, ln_b_g=m_ln_b_g, ln_b_b=m_ln_b_b,
               w_b_out=m_w_b_out, w_kv=m_w_kv, w_att_out=m_w_att_out, w_o=m_w_o, norm_ffn_g=m_norm_ffn_g,
               w_up=m_w_up, conv_ffn_w=m_conv_ffn_w, w_down=m_w_down, norm_final_g=m_norm_final_g)
    var = dict(norm_mix_g=v_norm_mix_g, norm_mem_g=v_norm_mem_g, w_in=v_w_in, b_gate=v_b_gate, conv_a_w=v_conv_a_w,
               w_a_out=v_w_a_out, conv_b_w=v_conv_b_w, conv_b_bias=v_conv_b_bias, ln_b_g=v_ln_b_g, ln_b_b=v_ln_b_b,
               w_b_out=v_w_b_out, w_kv=v_w_kv, w_att_out=v_w_att_out, w_o=v_w_o, norm_ffn_g=v_norm_ffn_g,
               w_up=v_w_up, conv_ffn_w=v_conv_ffn_w, w_down=v_w_down, norm_final_g=v_norm_final_g)
    depth = w_in.shape[0]
    chip = 2 * lax.axis_index("x") + lax.axis_index("y")
    core = jnp.stack([lax.axis_index("c"), chip]).astype(jnp.int32)

    chip1 = chip.astype(jnp.int32).reshape(1)
    layers = range(depth)
    sh = dict(w_in=[_cast_place([w_in], l, "col", chip1, f"cast_w_in_{l}") for l in layers],
              wsq=[_cast_place([w[n] for n in SQUARES], l, "row", chip1, f"cast_squares_{l}") for l in layers],
              w_kv=[_cast_place([w_kv], l, "col", chip1, f"cast_w_kv_{l}") for l in layers],
              w_up=[_cast_place([w_up], l, "col", chip1, f"cast_w_up_{l}") for l in layers],
              w_down=[_cast_place([w_down], l, "row", chip1, f"cast_w_down_{l}") for l in layers])
    conv_sh = [jnp.concatenate([conv_a_w, conv_b_w], axis=1), conv_ffn_w]
    small = {n: w[n] for n in REPLICATED + ("norm_final_g",)}

    loss, dx, sgrads, dg_final, pieces = _step(x[0], mem[0], loss_target[0], sh, conv_sh, small, core)

    res = {n: None for n in BIG}
    for l in reversed(range(depth)):
        for n in BIG:
            res[n] = _adam_shard(pieces[l][n], w[n], mom[n], var[n], l, res[n], f"adam_{n}_{l}")

    per_layer = REPLICATED + CONVS
    parts = [sgrads[l][n] for l in range(depth) for n in per_layer] + [dg_final, loss[0:1, 0:1]]
    total = _all_reduce_small(_pack(parts), "all_reduce_small")
    summed = _unpack(total, [p.shape for p in parts])
    g_small = {}
    for k, n in enumerate(per_layer):
        full = jnp.stack([summed[l * len(per_layer) + k] for l in range(depth)])
        if n in CONVS:
            cols = w[n].shape[-1]
            full = lax.dynamic_slice_in_dim(full, chip * cols, cols, axis=2)
        g_small[n] = full.reshape(w[n].shape)
    g_small["norm_final_g"] = summed[-2].reshape(norm_final_g.shape)
    names = per_layer + ("norm_final_g",)
    two_d = lambda a: a.reshape(1, -1) if a.ndim == 1 else a
    deltas, new_ms, new_vs = _adam_small(*[[two_d(t[n]) for n in names] for t in (g_small, w, mom, var)],
                                         "adam_small")
    for n, dl, nm, nv in zip(names, deltas, new_ms, new_vs):
        res[n] = (g_small[n], dl.reshape(w[n].shape), nm.reshape(w[n].shape), nv.reshape(w[n].shape))

    loss = summed[-1].reshape(())
    return (loss, dx.reshape(x.shape), *[res[n][0] for n in WEIGHTS], *[res[n][1] for n in WEIGHTS],
            *[res[n][2] for n in WEIGHTS], *[res[n][3] for n in WEIGHTS])
```

```python
import functools
import math

import jax
import jax.numpy as jnp
from jax import lax
from jax.experimental import pallas as pl
from jax.experimental.pallas import tpu as pltpu

F32 = jnp.float32
BF = jnp.bfloat16
EPS = 1e-6
N_HEADS = 4
K_A, K_B, K_F = 3, 31, 3
ADAM_LR, ADAM_B1, ADAM_B2, ADAM_EPS, ADAM_WD, ADAM_STEP = 0.001, 0.9, 0.999, 1e-08, 0.01, 10
N_CHIPS = 4
N_DEV = 8
HALO = 32
MAX_STRIPES = 8
TM_ROW = 256
TM_MM = 1024
TT_DW = 4096
DW_LHS_ELEMS = 4 * 1024 * 1024
TR_EW = 128
VMEM_LIMIT = 56 * 1024 * 1024
MESH = pl.DeviceIdType.MESH
_pallas_call = pl.pallas_call


def _params(n_axes):
    return pltpu.CompilerParams(dimension_semantics=("arbitrary",) * n_axes, vmem_limit_bytes=VMEM_LIMIT)


def _resident(shape, index):
    return pl.BlockSpec(shape, lambda *_: index, pipeline_mode=pl.Buffered(1))


def _sig(x):
    return 1.0 / (1.0 + jnp.exp(-x))


def _nt(a, b):
    return lax.dot_general(a, b, (((1,), (1,)), ((), ())), preferred_element_type=F32)


def _tn(a, b):
    return lax.dot_general(a, b, (((0,), (0,)), ((), ())), preferred_element_type=F32)


def _nn(a, b):
    return jnp.dot(a, b, preferred_element_type=F32)


class _Comm:
    def __init__(self, inputs, out_shape, scratch, run, aliases=None):
        self.inputs, self.out_shape, self.scratch, self.run = list(inputs), list(out_shape), list(scratch), run
        self.aliases = dict(aliases or {})


def _join(programs):
    programs = [p for p in programs if p is not None]
    if not programs:
        return None

    def split(seq, counts):
        parts, at = [], 0
        for n in counts:
            parts.append(seq[at:at + n])
            at += n
        return parts

    n_in = [len(p.inputs) for p in programs]
    n_out = [len(p.out_shape) for p in programs]
    n_s = [len(p.scratch) for p in programs]

    def run(phase, ins, outs, sems):
        for p, i, o, s in zip(programs, split(ins, n_in), split(outs, n_out), split(sems, n_s)):
            p.run(phase, i, o, s)

    aliases, in_at, out_at = {}, 0, 0
    for p, i, o in zip(programs, n_in, n_out):
        aliases.update({in_at + a: out_at + b for a, b in p.aliases.items()})
        in_at, out_at = in_at + i, out_at + o
    return _Comm([a for p in programs for a in p.inputs], [a for p in programs for a in p.out_shape],
                 [a for p in programs for a in p.scratch], run, aliases)


def _run(body, comm, *, name, grid, in_specs, out_specs, out_shape, args, scratch_shapes=(), aliases=None):
    n_axes = len(grid)
    aliases = dict(aliases or {})
    if comm is None:
        outs = _pallas_call(body, name=name, grid=grid, in_specs=list(in_specs), out_specs=list(out_specs),
                            out_shape=list(out_shape), scratch_shapes=list(scratch_shapes),
                            input_output_aliases=aliases, compiler_params=_params(n_axes))(*args)
        return list(outs), []
    counts = (len(in_specs), len(comm.inputs), len(out_specs), len(comm.out_shape), len(scratch_shapes),
              len(comm.scratch))

    def hosted(*refs):
        parts, at = [], 0
        for n in counts:
            parts.append(refs[at:at + n])
            at += n
        ins, c_ins, outs, c_outs, scr, c_sems = parts
        step = pl.program_id(0)
        for a in range(1, n_axes):
            step = step * grid[a] + pl.program_id(a)
        total = math.prod(grid)
        late = max(0, total - 1 - max(1, total // 4))

        @pl.when(step == 0)
        def _():
            comm.run("start", c_ins, c_outs, c_sems)

        body(*ins, *outs, *scr)

        @pl.when(step == late)
        def _():
            comm.run("forward", c_ins, c_outs, c_sems)

        @pl.when(step == total - 1)
        def _():
            comm.run("finish", c_ins, c_outs, c_sems)

    any_spec = pl.BlockSpec(memory_space=pl.ANY)
    res = _pallas_call(
        hosted, name=name, grid=grid, in_specs=list(in_specs) + [any_spec] * counts[1],
        out_specs=list(out_specs) + [any_spec] * counts[3], out_shape=list(out_shape) + comm.out_shape,
        scratch_shapes=list(scratch_shapes) + comm.scratch, compiler_params=_params(n_axes),
        input_output_aliases={**aliases, **{counts[0] + a: counts[2] + b for a, b in comm.aliases.items()}},
    )(*args, *comm.inputs)
    return list(res[:counts[2]]), list(res[counts[2]:])


def _run_comm(comm, name):
    n_in, n_out = len(comm.inputs), len(comm.out_shape)

    def body(*refs):
        ins, outs, sems = refs[:n_in], refs[n_in:n_in + n_out], refs[n_in + n_out:]
        for phase in ("start", "forward", "finish"):
            comm.run(phase, ins, outs, sems)

    any_spec = pl.BlockSpec(memory_space=pl.ANY)
    return list(_pallas_call(body, name=name, in_specs=[any_spec] * n_in, out_specs=[any_spec] * n_out,
                             out_shape=comm.out_shape, scratch_shapes=comm.scratch,
                             input_output_aliases=comm.aliases)(*comm.inputs))


SUBLANES = 8
LANES = 128
ROW_CHUNK = 128
ALL_RESIDUES = tuple(range(1, SUBLANES))
SHORT_RESIDUES = tuple(sorted({(HALO - K_A + 1 + k) % SUBLANES for k in range(K_A)} - {0}))


class _Rows:
    def __init__(self, ref, shifted_ref=None, residues=()):
        self.ref, self.shifted_ref, self.residues = ref, shifted_ref, tuple(residues)

    def shift(self):
        n = self.shifted_ref.shape[1]
        for j, b in enumerate(self.residues):
            self.shifted_ref[j] = self.ref[pl.ds(b, n), :]

    def at(self, offset, r0, c0, rows):
        b = offset % SUBLANES
        if b in self.residues:
            return self.shifted_ref[self.residues.index(b), pl.ds(offset - b + r0, rows), pl.ds(c0, LANES)]
        return self.ref[pl.ds(offset + r0, rows), pl.ds(c0, LANES)]


def _shifted_scratch(residues, tm, c):
    return pltpu.VMEM((len(residues), tm + HALO - SUBLANES, c), F32)


def _causal_offsets(k_taps):
    return [HALO - k_taps + 1 + k for k in range(k_taps)]


def _anticausal_offsets(k_taps):
    return [k_taps - 1 - k for k in range(k_taps)]


def _tap_chunk(src, w_ref, offsets, r0, c0, rows):
    acc = w_ref[0:1, pl.ds(c0, LANES)] * src.at(offsets[0], r0, c0, rows)
    for k in range(1, len(offsets)):
        acc = acc + w_ref[k:k + 1, pl.ds(c0, LANES)] * src.at(offsets[k], r0, c0, rows)
    return acc


def _conv_whole(ref, w_ref, offsets, tm):
    acc = w_ref[0:1, :] * ref[pl.ds(offsets[0], tm), :]
    for k in range(1, len(offsets)):
        acc = acc + w_ref[k:k + 1, :] * ref[pl.ds(offsets[k], tm), :]
    return acc


def _tap_grads_whole(dw_ref, dy, ref, offsets, tm):
    for k, off in enumerate(offsets):
        dw_ref[k:k + 1, :] += jnp.sum(dy * ref[pl.ds(off, tm), :], axis=0, keepdims=True)


def _conv_taps(src, w_ref, offsets, tm, emit):
    rows = min(ROW_CHUNK, tm)
    for c0 in range(0, w_ref.shape[1], LANES):
        for r0 in range(0, tm, rows):
            emit(r0, rows, c0, _tap_chunk(src, w_ref, offsets, r0, c0, rows))


def _tap_grads(dw_ref, dy_at, src, offsets, tm):
    rows = min(ROW_CHUNK // 2, tm)
    for c0 in range(0, dw_ref.shape[1], LANES):
        acc = [None] * len(offsets)
        for r0 in range(0, tm, rows):
            dy = dy_at(r0, rows, c0)
            for k, off in enumerate(offsets):
                part = (dy * src.at(off, r0, c0, rows)).reshape(rows // SUBLANES, SUBLANES, LANES).sum(axis=0)
                acc[k] = part if acc[k] is None else acc[k] + part
        for k in range(len(offsets)):
            dw_ref[k:k + 1, pl.ds(c0, LANES)] += jnp.sum(acc[k], axis=0, keepdims=True)


def _prev_halo(tm, col):
    return lambda i: (jnp.maximum(i * (tm // HALO) - 1, 0), col)


def _next_halo(tm, n_rows, col):
    return lambda i: (jnp.minimum((i + 1) * (tm // HALO), n_rows // HALO - 1), col)


def _norm_matmul(x, g, w, gs, name, comm=None):
    t, d = x.shape
    n_s, _, ns = w.shape
    n = n_s * ns
    tm = min(TM_MM, t)

    def body(x_ref, g_ref, w_ref, h_ref, y_ref):
        @pl.when(pl.program_id(1) == 0)
        def _():
            xf = x_ref[...]
            r = lax.rsqrt(jnp.mean(xf * xf, axis=-1, keepdims=True) + EPS)
            h_ref[...] = ((xf * r) * g_ref[...]).astype(BF)

        for s in range(gs):
            y_ref[:, s * ns:(s + 1) * ns] = _nn(h_ref[...], w_ref[s]).astype(BF)

    return _run(
        body, comm, name=name, grid=(t // tm, n_s // gs),
        in_specs=[pl.BlockSpec((tm, d), lambda i, j: (i, 0)),
                  pl.BlockSpec((1, d), lambda i, j: (0, 0)),
                  pl.BlockSpec((gs, d, ns), lambda i, j: (j, 0, 0))],
        out_specs=[pl.BlockSpec((tm, d), lambda i, j: (i, 0)),
                   pl.BlockSpec((tm, gs * ns), lambda i, j: (i, j))],
        out_shape=[jax.ShapeDtypeStruct((t, d), BF), jax.ShapeDtypeStruct((t, n), BF)],
        args=(x, g, w))


def _mem_kv(mem, g, w_kv, name):
    m, d = mem.shape
    n_s, _, ns = w_kv.shape

    def body(mem_ref, g_ref, w_ref, memn_ref, kv_ref):
        xf = mem_ref[...]
        r = lax.rsqrt(jnp.mean(xf * xf, axis=-1, keepdims=True) + EPS)
        memn = ((xf * r) * g_ref[...]).astype(BF)
        memn_ref[...] = memn
        for s in range(n_s):
            kv_ref[:, s * ns:(s + 1) * ns] = _nn(memn, w_ref[s]).astype(BF)

    return _pallas_call(
        body, name=name, grid=(1,),
        in_specs=[pl.BlockSpec((m, d), lambda i: (0, 0)),
                  pl.BlockSpec((1, d), lambda i: (0, 0)),
                  pl.BlockSpec((n_s, d, ns), lambda i: (0, 0, 0))],
        out_specs=[pl.BlockSpec((m, d), lambda i: (0, 0)),
                   pl.BlockSpec((m, 2 * d), lambda i: (0, 0))],
        out_shape=[jax.ShapeDtypeStruct((m, d), BF), jax.ShapeDtypeStruct((m, 2 * d), BF)],
        compiler_params=_params(1),
    )(mem, g, w_kv)


def _load_branch_inputs(i, proj_ref, gch_ref, vh_ref, u0h_ref, ugh_ref, xa_ref, xb_ref, d):
    gc = proj_ref[:, d:2 * d].astype(F32)
    v = proj_ref[:, 2 * d:3 * d].astype(F32)
    u0 = proj_ref[:, 3 * d:4 * d].astype(F32)
    ug = proj_ref[:, 4 * d:5 * d].astype(F32)
    keep = (i > 0).astype(F32)
    xa_ref[pl.ds(0, HALO), :] = gch_ref[...].astype(F32) * vh_ref[...].astype(F32) * keep
    xa_ref[pl.ds(HALO, gc.shape[0]), :] = gc * v
    xb_ref[pl.ds(0, HALO), :] = u0h_ref[...].astype(F32) * _sig(ugh_ref[...].astype(F32)) * keep
    xb_ref[pl.ds(HALO, gc.shape[0]), :] = u0 * _sig(ug)


def _softmax_rows(s):
    e = jnp.exp(s - jnp.max(s, axis=-1, keepdims=True))
    return e / jnp.sum(e, axis=-1, keepdims=True)


def _mixer_fwd(proj, x, kv, conv_a, conv_b, cbias, ln_g, ln_b, b_gate, wsq, l, name, comm=None):
    t, d = x.shape
    m = kv.shape[0]
    tm = min(TM_ROW, t)
    hd = d // N_HEADS
    scale = 1.0 / math.sqrt(hd)

    def body(proj_ref, gch_ref, vh_ref, u0h_ref, ugh_ref, x_ref, kv_ref, ca_w, cb_w, cbias_ref, lng_ref, lnb_ref,
             bg_ref, wa_ref, wb_ref, wc_ref, wo_ref,
             x1_ref, za_ref, zb_ref, o_ref, ya_ref, yb_ref, yc_ref, mg_ref, cb_ref, xa_ref, xb_ref, sb_ref):
        i = pl.program_id(0)
        _load_branch_inputs(i, proj_ref, gch_ref, vh_ref, u0h_ref, ugh_ref, xa_ref, xb_ref, d)
        xb = _Rows(xb_ref, sb_ref, ALL_RESIDUES)
        xb.shift()
        def put_za(r0, rows, c0, ca):
            gb = proj_ref[pl.ds(r0, rows), pl.ds(c0, LANES)].astype(F32)
            za_ref[pl.ds(r0, rows), pl.ds(c0, LANES)] = (gb * ca).astype(BF)

        _conv_taps(_Rows(xa_ref), ca_w, _causal_offsets(K_A), tm, put_za)
        ya = _nn(za_ref[...], wa_ref[...])
        ya_ref[...] = ya.astype(BF)

        def put_cb(r0, rows, c0, conv):
            cb_ref[pl.ds(r0, rows), pl.ds(c0, LANES)] = conv + cbias_ref[:, pl.ds(c0, LANES)]

        _conv_taps(xb, cb_w, _causal_offsets(K_B), tm, put_cb)
        cb = cb_ref[...]
        mu = jnp.mean(cb, axis=-1, keepdims=True)
        dlt = cb - mu
        rstd = lax.rsqrt(jnp.mean(dlt * dlt, axis=-1, keepdims=True) + EPS)
        lnb = (dlt * rstd) * lng_ref[...] + lnb_ref[...]
        zb = (lnb * _sig(lnb)).astype(BF)
        zb_ref[...] = zb
        yb = _nn(zb, wb_ref[...])
        yb_ref[...] = yb.astype(BF)
        for h in range(N_HEADS):
            qh = proj_ref[:, 5 * d + h * hd:5 * d + (h + 1) * hd]
            kh = kv_ref[:, h * hd:(h + 1) * hd]
            vh = kv_ref[:, d + h * hd:d + (h + 1) * hd]
            p = _softmax_rows(_nt(qh, kh) * scale)
            o_ref[:, h * hd:(h + 1) * hd] = _nn(p.astype(BF), vh).astype(BF)
        yc = _nn(o_ref[...], wc_ref[...])
        yc_ref[...] = yc.astype(BF)
        g0 = _sig(proj_ref[:, 6 * d:7 * d].astype(F32) + bg_ref[:, 0:d])
        g1 = _sig(proj_ref[:, 7 * d:8 * d].astype(F32) + bg_ref[:, d:2 * d])
        g2 = _sig(proj_ref[:, 8 * d:9 * d].astype(F32) + bg_ref[:, 2 * d:3 * d])
        mg = (g0 * ya + g1 * yb + g2 * yc).astype(BF)
        mg_ref[...] = mg
        x1_ref[...] = x_ref[...] + _nn(mg, wo_ref[...])

    row = lambda w_: pl.BlockSpec((tm, w_), lambda i: (i, 0))
    halo = lambda col: pl.BlockSpec((HALO, d), _prev_halo(tm, col))
    sq = lambda which: _resident((None, None, d, d), (l, which, 0, 0))
    act = jax.ShapeDtypeStruct((t, d), BF)
    return _run(
        body, comm, name=name, grid=(t // tm,),
        in_specs=[row(9 * d), halo(1), halo(2), halo(3), halo(4), row(d),
                  _resident((m, 2 * d), (0, 0)),
                  _resident((None, K_A, d), (l, 0, 0)), _resident((None, K_B, d), (l, 0, 0)),
                  _resident((1, d), (0, 0)), _resident((1, d), (0, 0)), _resident((1, d), (0, 0)),
                  _resident((1, 3 * d), (0, 0)), sq(0), sq(1), sq(2), sq(3)],
        out_specs=[row(d)] * 9,
        out_shape=[jax.ShapeDtypeStruct((t, d), F32)] + [act] * 7 + [jax.ShapeDtypeStruct((t, d), F32)],
        scratch_shapes=[pltpu.VMEM((HALO + tm, d), F32), pltpu.VMEM((HALO + tm, d), F32),
                        _shifted_scratch(ALL_RESIDUES, tm, d)],
        args=(proj, proj, proj, proj, proj, x, kv, conv_a, conv_b, cbias, ln_g, ln_b, b_gate, wsq, wsq, wsq, wsq))


def _ffn_down_fwd(up, x1, conv_f, w_down, l, name, comm=None):
    t, d = x1.shape
    f2 = up.shape[1]
    f = f2 // 2
    tm = min(TM_ROW, t)

    def body(up_ref, uph_ref, x1_ref, cw_ref, wd_ref, x2_ref, zf_ref, uc_ref, xx_ref):
        i = pl.program_id(0)
        xx_ref[pl.ds(0, HALO), :] = uph_ref[...].astype(F32) * (i > 0).astype(F32)
        xx_ref[pl.ds(HALO, tm), :] = up_ref[...].astype(F32)
        uc = _conv_whole(xx_ref, cw_ref, _causal_offsets(K_F), tm)
        uc_ref[...] = uc.astype(BF)
        gt = uc[:, 0:f]
        zf = (gt * _sig(gt) * uc[:, f:f2]).astype(BF)
        zf_ref[...] = zf
        x2_ref[...] = x1_ref[...] + _nn(zf, wd_ref[...])

    return _run(
        body, comm, name=name, grid=(t // tm,),
        in_specs=[pl.BlockSpec((tm, f2), lambda i: (i, 0)),
                  pl.BlockSpec((HALO, f2), _prev_halo(tm, 0)),
                  pl.BlockSpec((tm, d), lambda i: (i, 0)),
                  _resident((None, K_F, f2), (l, 0, 0)),
                  _resident((None, f, d), (l, 0, 0))],
        out_specs=[pl.BlockSpec((tm, d), lambda i: (i, 0)), pl.BlockSpec((tm, f), lambda i: (i, 0)),
                   pl.BlockSpec((tm, f2), lambda i: (i, 0))],
        out_shape=[jax.ShapeDtypeStruct((t, d), F32), jax.ShapeDtypeStruct((t, f), BF),
                   jax.ShapeDtypeStruct((t, f2), BF)],
        scratch_shapes=[pltpu.VMEM((HALO + tm, f2), F32)],
        args=(up, up, x1, conv_f, w_down))


def _final_loss(x, g, target, name):
    t, d = x.shape
    tm = min(2 * TM_ROW, t)

    def body(x_ref, g_ref, t_ref, dx_ref, loss_ref, dg_ref):
        @pl.when(pl.program_id(0) == 0)
        def _():
            loss_ref[...] = jnp.zeros_like(loss_ref)
            dg_ref[...] = jnp.zeros_like(dg_ref)

        xf = x_ref[...]
        r = lax.rsqrt(jnp.mean(xf * xf, axis=-1, keepdims=True) + EPS)
        xhat = xf * r
        err = xhat * g_ref[...] - t_ref[...]
        loss_ref[...] += (0.5 / d) * jnp.sum(err * err)
        dy = err * (1.0 / d)
        dg_ref[...] += jnp.sum(dy * xhat, axis=0, keepdims=True)
        dxh = dy * g_ref[...]
        dx_ref[...] = r * (dxh - xhat * jnp.mean(dxh * xhat, axis=-1, keepdims=True))

    return _pallas_call(
        body, name=name, grid=(t // tm,),
        in_specs=[pl.BlockSpec((tm, d), lambda i: (i, 0)), pl.BlockSpec((1, d), lambda i: (0, 0)),
                  pl.BlockSpec((tm, d), lambda i: (i, 0))],
        out_specs=[pl.BlockSpec((tm, d), lambda i: (i, 0)), pl.BlockSpec((8, 128), lambda i: (0, 0)),
                   pl.BlockSpec((1, d), lambda i: (0, 0))],
        out_shape=[jax.ShapeDtypeStruct((t, d), F32), jax.ShapeDtypeStruct((8, 128), F32),
                   jax.ShapeDtypeStruct((1, d), F32)],
        compiler_params=_params(1),
    )(x, g, target)


def _ffn_down_bwd(dx2, up, uc, w_down, l, name, comm=None):
    t, d = dx2.shape
    f2 = up.shape[1]
    f = f2 // 2
    tm = min(TM_ROW, t)

    def body(dx2_ref, up_ref, uph_ref, uc_ref, wd_ref, duc_ref, dx2b_ref, dcw_ref, xx_ref):
        i = pl.program_id(0)

        @pl.when(i == 0)
        def _():
            dcw_ref[...] = jnp.zeros_like(dcw_ref)

        xx_ref[pl.ds(0, HALO), :] = uph_ref[...].astype(F32) * (i > 0).astype(F32)
        xx_ref[pl.ds(HALO, tm), :] = up_ref[...].astype(F32)
        gt = uc_ref[:, 0:f].astype(F32)
        sg = _sig(gt)
        dx2b = dx2_ref[...].astype(BF)
        dx2b_ref[...] = dx2b
        dzf = _nt(dx2b, wd_ref[...])
        duc_ref[:, 0:f] = (dzf * uc_ref[:, f:f2].astype(F32) * (sg * (1.0 + gt * (1.0 - sg)))).astype(BF)
        duc_ref[:, f:f2] = (dzf * (gt * sg)).astype(BF)
        _tap_grads_whole(dcw_ref, duc_ref[...].astype(F32), xx_ref, _causal_offsets(K_F), tm)

    return _run(
        body, comm, name=name, grid=(t // tm,),
        in_specs=[pl.BlockSpec((tm, d), lambda i: (i, 0)),
                  pl.BlockSpec((tm, f2), lambda i: (i, 0)),
                  pl.BlockSpec((HALO, f2), _prev_halo(tm, 0)),
                  pl.BlockSpec((tm, f2), lambda i: (i, 0)),
                  _resident((None, f, d), (l, 0, 0))],
        out_specs=[pl.BlockSpec((tm, f2), lambda i: (i, 0)), pl.BlockSpec((tm, d), lambda i: (i, 0)),
                   pl.BlockSpec((K_F, f2), lambda i: (0, 0))],
        out_shape=[jax.ShapeDtypeStruct((t, f2), BF), jax.ShapeDtypeStruct((t, d), BF),
                   jax.ShapeDtypeStruct((K_F, f2), F32)],
        scratch_shapes=[pltpu.VMEM((HALO + tm, f2), F32)],
        args=(dx2, up, up, uc, w_down))


def _ffn_conv_bwd(duc, conv_f, l, name, comm=None):
    t, f2 = duc.shape
    tm = min(TM_ROW, t)
    n_t = t // tm

    def body(duc_ref, nxt_ref, cw_ref, dup_ref, yy_ref):
        i = pl.program_id(0)
        yy_ref[pl.ds(0, tm), :] = duc_ref[...].astype(F32)
        yy_ref[pl.ds(tm, HALO), :] = nxt_ref[...].astype(F32) * (i < n_t - 1).astype(F32)

        def put(r0, rows, c0, conv):
            dup_ref[pl.ds(r0, rows), pl.ds(c0, LANES)] = conv.astype(BF)

        _conv_taps(_Rows(yy_ref), cw_ref, _anticausal_offsets(K_F), tm, put)

    return _run(
        body, comm, name=name, grid=(n_t,),
        in_specs=[pl.BlockSpec((tm, f2), lambda i: (i, 0)),
                  pl.BlockSpec((HALO, f2), _next_halo(tm, t, 0)),
                  _resident((None, K_F, f2), (l, 0, 0))],
        out_specs=[pl.BlockSpec((tm, f2), lambda i: (i, 0))],
        out_shape=[jax.ShapeDtypeStruct((t, f2), BF)],
        scratch_shapes=[pltpu.VMEM((tm + HALO, f2), F32)],
        args=(duc, duc, conv_f))


def _nt_matmul_norm_bwd(dy, w, gs, x, g, dres, name, comm=None):
    t, n = dy.shape
    d = x.shape[1]
    tm = min(TM_MM // 2, t)
    n_s, _, ns = w.shape
    n_k, tk = n_s // gs, gs * ns

    def body(dy_ref, w_ref, x_ref, g_ref, dres_ref, dx_ref, dxb_ref, dg_ref, acc_ref):
        i, k = pl.program_id(0), pl.program_id(1)

        @pl.when((i == 0) & (k == 0))
        def _():
            dg_ref[...] = jnp.zeros_like(dg_ref)

        @pl.when(k == 0)
        def _():
            acc_ref[...] = jnp.zeros_like(acc_ref)

        part = _nt(dy_ref[:, 0:ns], w_ref[0])
        for s in range(1, gs):
            part = part + _nt(dy_ref[:, s * ns:(s + 1) * ns], w_ref[s])
        acc_ref[...] += part

        @pl.when(k == n_k - 1)
        def _():
            xf = x_ref[...]
            r = lax.rsqrt(jnp.mean(xf * xf, axis=-1, keepdims=True) + EPS)
            xhat = xf * r
            dh = acc_ref[...]
            dg_ref[...] += jnp.sum(dh * xhat, axis=0, keepdims=True)
            dxh = dh * g_ref[...]
            dx = dres_ref[...] + r * (dxh - xhat * jnp.mean(dxh * xhat, axis=-1, keepdims=True))
            dx_ref[...] = dx
            dxb_ref[...] = dx.astype(BF)

    return _run(
        body, comm, name=name, grid=(t // tm, n_k),
        in_specs=[pl.BlockSpec((tm, tk), lambda i, k: (i, k)),
                  pl.BlockSpec((gs, d, ns), lambda i, k: (k, 0, 0)),
                  pl.BlockSpec((tm, d), lambda i, k: (i, 0)),
                  pl.BlockSpec((1, d), lambda i, k: (0, 0)),
                  pl.BlockSpec((tm, d), lambda i, k: (i, 0))],
        out_specs=[pl.BlockSpec((tm, d), lambda i, k: (i, 0)), pl.BlockSpec((tm, d), lambda i, k: (i, 0)),
                   pl.BlockSpec((1, d), lambda i, k: (0, 0))],
        out_shape=[jax.ShapeDtypeStruct((t, d), F32), jax.ShapeDtypeStruct((t, d), BF),
                   jax.ShapeDtypeStruct((1, d), F32)],
        scratch_shapes=[pltpu.VMEM((tm, d), F32)],
        args=(dy, w, x, g, dres))


def _mixer_bwd(dx1b, proj, ya, yb, yc, cb, kv, conv_a, conv_b, ln_g, ln_b, b_gate, wsq, l, name, comm=None):
    t, d = cb.shape
    m = kv.shape[0]
    tm = min(TM_ROW, t)
    hd = d // N_HEADS
    scale = 1.0 / math.sqrt(hd)

    def body(dx1b_ref, proj_ref, gch_ref, vh_ref, u0h_ref, ugh_ref, ya_ref, yb_ref, yc_ref, cb_ref, kv_ref,
             ca_w, cb_w, lng_ref, lnb_ref, bg_ref, wa_ref, wb_ref, wc_ref, wo_ref,
             dpre_ref, dya_ref, dyb_ref, dyc_ref, dkv_ref, dbg_ref, dlng_ref, dlnb_ref, dcbias_ref, dcaw_ref,
             dcbw_ref, xa_ref, xb_ref, sa_ref, sb_ref):
        i = pl.program_id(0)

        @pl.when(i == 0)
        def _():
            for ref in (dkv_ref, dbg_ref, dlng_ref, dlnb_ref, dcbias_ref, dcaw_ref, dcbw_ref):
                ref[...] = jnp.zeros_like(ref)

        _load_branch_inputs(i, proj_ref, gch_ref, vh_ref, u0h_ref, ugh_ref, xa_ref, xb_ref, d)
        xa, xb = _Rows(xa_ref, sa_ref, SHORT_RESIDUES), _Rows(xb_ref, sb_ref, ALL_RESIDUES)
        xa.shift()
        xb.shift()
        dmg = _nt(dx1b_ref[...], wo_ref[...])
        ys = (ya_ref, yb_ref, yc_ref)
        dys = (dya_ref, dyb_ref, dyc_ref)
        for b in range(3):
            gate = _sig(proj_ref[:, (6 + b) * d:(7 + b) * d].astype(F32) + bg_ref[:, b * d:(b + 1) * d])
            dys[b][...] = (gate * dmg).astype(BF)
            dpg = dmg * ys[b][...].astype(F32) * gate * (1.0 - gate)
            dpre_ref[:, (6 + b) * d:(7 + b) * d] = dpg.astype(BF)
            dbg_ref[:, b * d:(b + 1) * d] += jnp.sum(dpg, axis=0, keepdims=True)
        gb = proj_ref[:, 0:d].astype(F32)
        dza = _nt(dya_ref[...], wa_ref[...])

        def put_dgb(r0, rows, c0, ca):
            dpre_ref[pl.ds(r0, rows), pl.ds(c0, LANES)] = (dza[r0:r0 + rows, c0:c0 + LANES] * ca).astype(BF)

        _conv_taps(xa, ca_w, _causal_offsets(K_A), tm, put_dgb)
        dpre_ref[:, d:2 * d] = (dza * gb).astype(BF)
        _tap_grads(dcaw_ref, lambda r0, rows, c0: dpre_ref[pl.ds(r0, rows), pl.ds(d + c0, LANES)].astype(F32),
                   xa, _causal_offsets(K_A), tm)
        dpre_ref[:, 2 * d:3 * d] = jnp.zeros((tm, d), BF)
        cbv = cb_ref[...]
        mu = jnp.mean(cbv, axis=-1, keepdims=True)
        dlt = cbv - mu
        rstd = lax.rsqrt(jnp.mean(dlt * dlt, axis=-1, keepdims=True) + EPS)
        xhat = dlt * rstd
        lnb = xhat * lng_ref[...] + lnb_ref[...]
        sg = _sig(lnb)
        dzb = _nt(dyb_ref[...], wb_ref[...])
        dl = dzb * (sg * (1.0 + lnb * (1.0 - sg)))
        dlng_ref[...] += jnp.sum(dl * xhat, axis=0, keepdims=True)
        dlnb_ref[...] += jnp.sum(dl, axis=0, keepdims=True)
        dxh = dl * lng_ref[...]
        dcb = rstd * (dxh - jnp.mean(dxh, axis=-1, keepdims=True)
                      - xhat * jnp.mean(dxh * xhat, axis=-1, keepdims=True))
        dcbias_ref[...] += jnp.sum(dcb, axis=0, keepdims=True)
        dpre_ref[:, 3 * d:4 * d] = dcb.astype(BF)
        _tap_grads(dcbw_ref, lambda r0, rows, c0: dpre_ref[pl.ds(r0, rows), pl.ds(3 * d + c0, LANES)].astype(F32),
                   xb, _causal_offsets(K_B), tm)
        dpre_ref[:, 4 * d:5 * d] = jnp.zeros((tm, d), BF)
        do = _nt(dyc_ref[...], wc_ref[...]).astype(BF)
        for h in range(N_HEADS):
            qh = proj_ref[:, 5 * d + h * hd:5 * d + (h + 1) * hd]
            kh = kv_ref[:, h * hd:(h + 1) * hd]
            vh = kv_ref[:, d + h * hd:d + (h + 1) * hd]
            doh = do[:, h * hd:(h + 1) * hd]
            p = _softmax_rows(_nt(qh, kh) * scale)
            dp = _nt(doh, vh)
            ds = (p * (dp - jnp.sum(dp * p, axis=-1, keepdims=True)) * scale).astype(BF)
            dpre_ref[:, 5 * d + h * hd:5 * d + (h + 1) * hd] = _nn(ds, kh).astype(BF)
            dkv_ref[:, h * hd:(h + 1) * hd] += _tn(ds, qh)
            dkv_ref[:, d + h * hd:d + (h + 1) * hd] += _tn(p.astype(BF), doh)

    row = lambda w_: pl.BlockSpec((tm, w_), lambda i: (i, 0))
    halo = lambda col: pl.BlockSpec((HALO, d), _prev_halo(tm, col))
    sq = lambda which: _resident((None, None, d, d), (l, which, 0, 0))
    acc = lambda r, c: pl.BlockSpec((r, c), lambda i: (0, 0))
    act = jax.ShapeDtypeStruct((t, d), BF)
    vec = lambda r, c: jax.ShapeDtypeStruct((r, c), F32)
    return _run(
        body, comm, name=name, grid=(t // tm,),
        in_specs=[row(d), row(9 * d), halo(1), halo(2), halo(3), halo(4), row(d), row(d), row(d), row(d),
                  _resident((m, 2 * d), (0, 0)),
                  _resident((None, K_A, d), (l, 0, 0)), _resident((None, K_B, d), (l, 0, 0)),
                  _resident((1, d), (0, 0)), _resident((1, d), (0, 0)), _resident((1, 3 * d), (0, 0)),
                  sq(0), sq(1), sq(2), sq(3)],
        out_specs=[row(9 * d), row(d), row(d), row(d), acc(m, 2 * d), acc(1, 3 * d), acc(1, d), acc(1, d),
                   acc(1, d), acc(K_A, d), acc(K_B, d)],
        out_shape=[jax.ShapeDtypeStruct((t, 9 * d), BF), act, act, act, vec(m, 2 * d), vec(1, 3 * d), vec(1, d),
                   vec(1, d), vec(1, d), vec(K_A, d), vec(K_B, d)],
        scratch_shapes=[pltpu.VMEM((HALO + tm, d), F32), pltpu.VMEM((HALO + tm, d), F32),
                        _shifted_scratch(SHORT_RESIDUES, tm, d), _shifted_scratch(ALL_RESIDUES, tm, d)],
        args=(dx1b, proj, proj, proj, proj, proj, ya, yb, yc, cb, kv, conv_a, conv_b, ln_g, ln_b, b_gate,
              wsq, wsq, wsq, wsq))


def _inproj_conv_bwd(dpre, proj, conv_a, conv_b, l, name, comm=None):
    t, d9 = dpre.shape
    d = d9 // 9
    tm = min(TM_ROW, t)
    n_t = t // tm

    def body(dpre_ref, nxa_ref, nxb_ref, proj_ref, ca_w, cb_w, dproj_ref, ya_ref, yb_ref, sb_ref):
        i = pl.program_id(0)
        keep = (i < n_t - 1).astype(F32)
        ya_ref[pl.ds(0, tm), :] = dpre_ref[:, d:2 * d].astype(F32)
        ya_ref[pl.ds(tm, HALO), :] = nxa_ref[...].astype(F32) * keep
        yb_ref[pl.ds(0, tm), :] = dpre_ref[:, 3 * d:4 * d].astype(F32)
        yb_ref[pl.ds(tm, HALO), :] = nxb_ref[...].astype(F32) * keep
        yb = _Rows(yb_ref, sb_ref, ALL_RESIDUES)
        yb.shift()
        dproj_ref[:, 0:d] = dpre_ref[:, 0:d]
        dproj_ref[:, 5 * d:9 * d] = dpre_ref[:, 5 * d:9 * d]
        def chunk(ref, block, r0, rows, c0):
            return ref.at[pl.ds(r0, rows), pl.ds(block * d + c0, LANES)]

        def put_a(r0, rows, c0, dcv):
            chunk(dproj_ref, 1, r0, rows, c0)[...] = (dcv * chunk(proj_ref, 2, r0, rows, c0)[...].astype(F32)).astype(BF)
            chunk(dproj_ref, 2, r0, rows, c0)[...] = (dcv * chunk(proj_ref, 1, r0, rows, c0)[...].astype(F32)).astype(BF)

        def put_b(r0, rows, c0, dub):
            sg = _sig(chunk(proj_ref, 4, r0, rows, c0)[...].astype(F32))
            u0 = chunk(proj_ref, 3, r0, rows, c0)[...].astype(F32)
            chunk(dproj_ref, 3, r0, rows, c0)[...] = (dub * sg).astype(BF)
            chunk(dproj_ref, 4, r0, rows, c0)[...] = (dub * u0 * sg * (1.0 - sg)).astype(BF)

        _conv_taps(_Rows(ya_ref), ca_w, _anticausal_offsets(K_A), tm, put_a)
        _conv_taps(yb, cb_w, _anticausal_offsets(K_B), tm, put_b)

    return _run(
        body, comm, name=name, grid=(n_t,),
        in_specs=[pl.BlockSpec((tm, d9), lambda i: (i, 0)),
                  pl.BlockSpec((HALO, d), _next_halo(tm, t, 1)),
                  pl.BlockSpec((HALO, d), _next_halo(tm, t, 3)),
                  pl.BlockSpec((tm, d9), lambda i: (i, 0)),
                  _resident((None, K_A, d), (l, 0, 0)), _resident((None, K_B, d), (l, 0, 0))],
        out_specs=[pl.BlockSpec((tm, d9), lambda i: (i, 0))],
        out_shape=[jax.ShapeDtypeStruct((t, d9), BF)],
        scratch_shapes=[pltpu.VMEM((tm + HALO, d), F32), pltpu.VMEM((tm + HALO, d), F32),
                        _shifted_scratch(ALL_RESIDUES, tm, d)],
        args=(dpre, dpre, dpre, proj, conv_a, conv_b))


def _mem_kv_bwd(dkv, memn, mem, g, w_kv, name):
    m, d = mem.shape
    n_s, _, ns = w_kv.shape

    def body(dkv_ref, memn_ref, mem_ref, g_ref, w_ref, dw_ref, dg_ref):
        dkvb = dkv_ref[...].astype(BF)
        dw_ref[...] = _tn(memn_ref[...], dkvb).astype(BF)
        dmemn = _nt(dkvb[:, 0:ns], w_ref[0])
        for s in range(1, n_s):
            dmemn = dmemn + _nt(dkvb[:, s * ns:(s + 1) * ns], w_ref[s])
        xf = mem_ref[...]
        r = lax.rsqrt(jnp.mean(xf * xf, axis=-1, keepdims=True) + EPS)
        dg_ref[...] = jnp.sum(dmemn * (xf * r), axis=0, keepdims=True)

    return _pallas_call(
        body, name=name, grid=(1,),
        in_specs=[pl.BlockSpec((m, 2 * d), lambda i: (0, 0)), pl.BlockSpec((m, d), lambda i: (0, 0)),
                  pl.BlockSpec((m, d), lambda i: (0, 0)), pl.BlockSpec((1, d), lambda i: (0, 0)),
                  pl.BlockSpec((n_s, d, ns), lambda i: (0, 0, 0))],
        out_specs=[pl.BlockSpec((d, 2 * d), lambda i: (0, 0)), pl.BlockSpec((1, d), lambda i: (0, 0))],
        out_shape=[jax.ShapeDtypeStruct((d, 2 * d), BF), jax.ShapeDtypeStruct((1, d), F32)],
        compiler_params=_params(1),
    )(dkv, memn, mem, g, w_kv)


def _dw_matmul(a, b, tn, name, comm=None):
    t, k = a.shape
    n = b.shape[1]
    tt = min(TT_DW, t)
    while tt * k > DW_LHS_ELEMS and tt % 2 == 0:
        tt //= 2
    n_s = t // tt

    def body(a_ref, b_ref, o_ref, acc_ref):
        s = pl.program_id(1)
        part = _tn(a_ref[...], b_ref[...])
        if n_s == 1:
            o_ref[...] = part.astype(BF)
            return

        @pl.when(s == 0)
        def _():
            acc_ref[...] = part

        @pl.when(s > 0)
        def _():
            acc_ref[...] += part

        @pl.when(s == n_s - 1)
        def _():
            o_ref[...] = acc_ref[...].astype(BF)

    return _run(
        body, comm, name=name, grid=(n // tn, n_s),
        in_specs=[pl.BlockSpec((tt, k), lambda j, s: (s, 0)), pl.BlockSpec((tt, tn), lambda j, s: (s, j))],
        out_specs=[pl.BlockSpec((k, tn), lambda j, s: (0, j))],
        out_shape=[jax.ShapeDtypeStruct((k, n), BF)],
        scratch_shapes=[pltpu.VMEM((k, tn) if n_s > 1 else (8, 128), F32)],
        args=(a, b))


class _GradReduce:
    def __init__(self, l, grads, core):
        self.l, self.core, self.names = l, core, tuple(grads)
        self.views = {n: _halves_view(n, g) for n, g in grads.items()}
        self.got, self.sums, self.landing, self.pieces = {}, {}, {}, {}

    def swap_program(self):
        return _swap_program([self.views[n] for n in self.names])

    def swapped(self, outs):
        self.got = dict(zip(self.names, outs))
        for n in self.names:
            self.sums[n], self.landing[n] = _add_halves(self.views[n], self.got[n], self.core, n in COL,
                                                        f"add_halves_{n}_{self.l}")

    def scatter_program(self, names=None, part=(0, 1)):
        names = self.names if names is None else names
        return _scatter_program([self.sums[n] for n in names], [self.landing[n] for n in names], names, part)

    def scattered(self, outs, names=None):
        names = self.names if names is None else names
        self.landing.update(zip(names, outs))
        self.pieces.update(zip(names, outs))


def _step(x, mem, target, sh, conv_sh, small, core):
    depth = len(sh["w_in"])
    d = x.shape[1]
    f2 = sh["w_up"][0].shape[2] * N_CHIPS
    tn_dw_in = min(1024, d)
    tn_dw_sq = max(LANES, d // 4)
    tn_dw_up = f2 // 11 if f2 % (11 * 128) == 0 and f2 // 11 >= 128 else f2
    row = lambda v: v.reshape(1, -1)
    one = lambda a: a[None]

    w_in, cab, cf = _run_comm(_gather_program([sh["w_in"][0]] + conv_sh, ("col", "small", "small")), "gather_first")
    saved = []
    for l in range(depth):
        conv = dict(a=cab[l:l + 1, :K_A], b=cab[l:l + 1, K_A:], f=cf[l:l + 1])
        (h, proj), (wsq, w_kv) = _norm_matmul(
            x, row(small["norm_mix_g"][l]), w_in, 1, f"in_proj_{l}",
            _gather_program([sh["wsq"][l], sh["w_kv"][l]], ("row", "col")))
        memn, kv = _mem_kv(mem, row(small["norm_mem_g"][l]), w_kv, f"mem_kv_{l}")
        (x1, za, zb, o, ya, yb, yc, mg, cb), (w_up, w_down) = _mixer_fwd(
            proj, x, kv, conv["a"], conv["b"], row(small["conv_b_bias"][l]), row(small["ln_b_g"][l]),
            row(small["ln_b_b"][l]), row(small["b_gate"][l]), one(wsq), 0, f"mixer_fwd_{l}",
            _gather_program([sh["w_up"][l], sh["w_down"][l]], ("col", "row")))
        more = l + 1 < depth
        nxt = _gather_program([sh["w_in"][l + 1]], ("col",), (0, 2)) if more else None
        (h2, up), w_in_next = _norm_matmul(x1, row(small["norm_ffn_g"][l]), w_up, 2, f"up_proj_{l}", nxt)
        nxt = _gather_program(w_in_next, ("col",), (1, 2)) if more else None
        (x2, zf, uc), w_in_next = _ffn_down_fwd(up, x1, conv["f"], one(w_down), 0, f"ffn_down_fwd_{l}", nxt)
        saved.append(dict(x=x, x1=x1, memn=memn, kv=kv, h=h, proj=proj, za=za, zb=zb, o=o, ya=ya, yb=yb, yc=yc,
                          mg=mg, cb=cb, h2=h2, up=up, zf=zf, uc=uc, w_in=w_in, wsq=one(wsq), w_kv=w_kv,
                          w_up=w_up, w_down=one(w_down), conv=conv))
        x = x2
        w_in = w_in_next[0] if w_in_next else None
    dx, loss, dg_final = _final_loss(x, row(small["norm_final_g"]), target, "final_loss")
    sgrads, pieces = [None] * depth, [None] * depth
    above = None
    first, second, rest = SQUARES, ("w_up",), ("w_in", "w_kv", "w_down")
    for l in reversed(range(depth)):
        s = saved[l]
        conv = s["conv"]
        bottom = l == 0
        (duc, dx2b, dconv_f), got = _ffn_down_bwd(dx, s["up"], s["uc"], s["w_down"], 0, f"ffn_down_bwd_{l}",
                                                  above and above.swap_program())
        if above:
            above.swapped(got)
        (dup,), got = _ffn_conv_bwd(duc, conv["f"], 0, f"ffn_conv_bwd_{l}", above and above.scatter_program(first))
        if above:
            above.scattered(got, first)
        (dx1, dx1b, dg_ffn), got = _nt_matmul_norm_bwd(
            dup, s["w_up"], 2, s["x1"], row(small["norm_ffn_g"][l]), dx, f"up_proj_bwd_{l}",
            above and above.scatter_program(second))
        if above:
            above.scattered(got, second)
        grads = dict(w_up=_dw_matmul(s["h2"], dup, tn_dw_up, f"dw_up_{l}")[0][0],
                     w_down=_dw_matmul(s["zf"], dx2b, d, f"dw_down_{l}")[0][0])
        ffn = _GradReduce(l, grads, core) if bottom else None
        (dpre, dya, dyb, dyc, dkv, dbg, dlng, dlnb, dcbias, dconv_a, dconv_b), got = _mixer_bwd(
            dx1b, s["proj"], s["ya"], s["yb"], s["yc"], s["cb"], s["kv"], conv["a"], conv["b"],
            row(small["ln_b_g"][l]), row(small["ln_b_b"][l]), row(small["b_gate"][l]), s["wsq"], 0,
            f"mixer_bwd_{l}", _join([above and above.scatter_program(rest), ffn and ffn.swap_program()]))
        if above:
            above.scattered(got[:len(rest)], rest)
            pieces[above.l] = above.pieces
            got = got[len(rest):]
        if ffn:
            ffn.swapped(got)
        dw_kv, dg_mem = _mem_kv_bwd(dkv, s["memn"], mem, row(small["norm_mem_g"][l]), s["w_kv"], f"mem_kv_bwd_{l}")
        mix_grads = dict(w_a_out=_dw_matmul(s["za"], dya, tn_dw_sq, f"dw_a_out_{l}")[0][0],
                         w_b_out=_dw_matmul(s["zb"], dyb, tn_dw_sq, f"dw_b_out_{l}")[0][0],
                         w_att_out=_dw_matmul(s["o"], dyc, tn_dw_sq, f"dw_att_out_{l}")[0][0],
                         w_o=_dw_matmul(s["mg"], dx1b, tn_dw_sq, f"dw_o_{l}")[0][0], w_kv=dw_kv)
        mix = _GradReduce(l, mix_grads, core) if bottom else None
        (dproj,), got = _inproj_conv_bwd(dpre, s["proj"], conv["a"], conv["b"], 0, f"inproj_conv_bwd_{l}",
                                         _join([ffn and ffn.scatter_program(), mix and mix.swap_program()]))
        if bottom:
            ffn.scattered(got[:len(ffn.names)])
            mix.swapped(got[len(ffn.names):])
        in_grads = dict(w_in=_dw_matmul(s["h"], dproj, tn_dw_in, f"dw_in_{l}")[0][0])
        inp = _GradReduce(l, in_grads, core) if bottom else None
        (dx0, _, dg_mix), got = _nt_matmul_norm_bwd(
            dproj, s["w_in"], 2, s["x"], row(small["norm_mix_g"][l]), dx1, f"in_proj_bwd_{l}",
            _join([mix and mix.scatter_program(), inp and inp.swap_program()]))
        if bottom:
            mix.scattered(got[:len(mix.names)])
            inp.swapped(got[len(mix.names):])
            pieces[l] = {**ffn.pieces, **mix.pieces}
            last = inp
        else:
            above = _GradReduce(l, {**grads, **mix_grads, **in_grads}, core)
        sgrads[l] = dict(norm_mix_g=dg_mix, norm_mem_g=dg_mem, b_gate=dbg, conv_b_bias=dcbias, ln_b_g=dlng,
                         ln_b_b=dlnb, norm_ffn_g=dg_ffn, conv_a_w=dconv_a, conv_b_w=dconv_b, conv_ffn_w=dconv_f)
        dx = dx0
    return loss, dx, sgrads, dg_final, pieces, last


BIG = ("w_in", "w_a_out", "w_b_out", "w_att_out", "w_o", "w_kv", "w_up", "w_down")
COL = ("w_in", "w_kv", "w_up")
SQUARES = ("w_a_out", "w_b_out", "w_att_out", "w_o")
ANY = pl.BlockSpec(memory_space=pl.ANY)


def _place():
    x, y, c = lax.axis_index("x"), lax.axis_index("y"), lax.axis_index("c")
    chips = [(1 - x, y), (x, 1 - y), (1 - x, 1 - y)]
    return x, y, c, 2 * x + y, chips


def _remote(src, dst, send_sem, recv_sem, dev):
    return pltpu.make_async_remote_copy(src_ref=src, dst_ref=dst, send_sem=send_sem, recv_sem=recv_sem,
                                        device_id=dev, device_id_type=MESH)


class _Striped:
    def __init__(self, src, dst, make):
        rows = src.shape[-2]
        unit = 8 * (4 // jnp.dtype(src.dtype).itemsize)
        n = max(k for k in range(1, MAX_STRIPES + 1) if rows % (unit * k) == 0) if rows % unit == 0 else 1
        q = rows // n
        self.parts = [make(_window(src, pl.ds(i * q, q), slice(None)), _window(dst, pl.ds(i * q, q), slice(None)))
                      for i in range(n)]
        self.whole = make(src, dst)

    def start(self):
        for p in self.parts:
            p.start()

    def wait(self):
        self.whole.wait()

    def wait_send(self):
        self.whole.wait_send()

    def wait_recv(self):
        self.whole.wait_recv()


def _far(src, dst, send_sem, recv_sem, dev):
    return _Striped(src, dst, lambda s, d: _remote(s, d, send_sem, recv_sem, dev))


def _near(src, dst, sem):
    return _Striped(src, dst, lambda s, d: pltpu.make_async_copy(s, d, sem))


def _window(ref, rows, cols):
    return ref.at[(slice(None),) * (len(ref.shape) - 2) + (rows, cols)]


def _row_tile(rows, cols, unit=16, limit=1 << 20):
    best = unit
    for tr in range(unit, rows + 1, unit):
        if rows % tr == 0 and tr * cols <= limit:
            best = tr
    return best


def _cast_place(ws, l, kind, chip, name):
    _, k, n = ws[0].shape
    if kind == "col":
        shape, spec = (N_CHIPS, k, n), pl.BlockSpec((None, k, n), lambda i, c: (c[0], 0, 0))
    elif len(ws) == 1:
        shape, spec = (N_CHIPS * k, n), pl.BlockSpec((k, n), lambda i, c: (c[0], 0))
    else:
        shape, spec = (len(ws), N_CHIPS * k, n), pl.BlockSpec((len(ws), k, n), lambda i, c: (0, c[0], 0))

    def body(c_ref, *refs):
        o_ref = refs[-1]
        if len(ws) == 1 or kind == "col":
            o_ref[...] = refs[0][...].astype(BF)
        else:
            for i in range(len(ws)):
                o_ref[i] = refs[i][...].astype(BF)

    return _pallas_call(
        body, name=name,
        grid_spec=pltpu.PrefetchScalarGridSpec(
            num_scalar_prefetch=1, grid=(1,),
            in_specs=[pl.BlockSpec((None, k, n), lambda i, c: (l, 0, 0))] * len(ws), out_specs=spec),
        out_shape=jax.ShapeDtypeStruct(shape, BF),
        compiler_params=_params(1),
    )(chip, *ws)


def _gather_program(arrays, kinds, part=(0, 1)):
    n_t = len(arrays)
    index, count = part

    def shard_rows(f, kind):
        return f.shape[-2] if kind == "col" else f.shape[-2] // N_CHIPS

    def run(phase, ins, full, sems):
        ici_send, ici_recv, sib_send, sib_recv, loc_sem = sems
        x, y, c, me, chips = _place()
        sibling = (x, y, 1 - c)

        def part_of(i, chip, half):
            f, kind = full[i], kinds[i]
            if kind == "small":
                cols = ins[i].shape[-1]
                return _window(f, slice(None), pl.ds(pl.multiple_of(chip * cols, 128), cols))
            rows = shard_rows(f, kind)
            r = rows // (2 * count)
            at = (half * count + index) * r
            if kind == "col":
                return f.at[chip, pl.ds(pl.multiple_of(at, 16), r), :]
            return _window(f, pl.ds(pl.multiple_of(chip * rows + at, 16), r), slice(None))

        src_part = lambda i, half: ins[i] if kinds[i] == "small" else part_of(i, me, half)
        dst_part = part_of
        local = [_near(ins[i], part_of(i, me, c), loc_sem.at[i]) for i in range(n_t) if kinds[i] == "small"]
        sends = []
        for i in range(n_t):
            for j, chip in enumerate(chips):
                sends.append(_far(src_part(i, c), dst_part(i, me, c), ici_send.at[3 * i + j],
                                  ici_recv.at[3 * i + j], (*chip, c)))
        if phase == "start":
            for cp in local + sends:
                cp.start()
            return
        passed = []
        for i in range(n_t):
            for j, chip in enumerate(chips):
                k = 2 * chip[0] + chip[1]
                landed = dst_part(i, k, c)
                if phase == "forward":
                    _remote(landed, landed, ici_send.at[3 * i + j], ici_recv.at[3 * i + j], (*chip, c)).wait_recv()
                if kinds[i] != "small":
                    passed.append(_far(landed, landed, sib_send.at[3 * i + j], sib_recv.at[3 * i + j], sibling))
                    if phase == "forward":
                        passed[-1].start()
        if phase == "forward":
            return
        for i in range(n_t):
            if kinds[i] == "small":
                continue
            for j, chip in enumerate(chips):
                k = 2 * chip[0] + chip[1]
                other = dst_part(i, k, 1 - c)
                _remote(other, other, sib_send.at[3 * i + j], sib_recv.at[3 * i + j], sibling).wait_recv()
        for cp in sends + passed:
            cp.wait_send()
        for cp in local:
            cp.wait()

    def out_shape(a, kind):
        shp = a.shape[:-1] + (a.shape[-1] * N_CHIPS,) if kind == "small" else a.shape
        return jax.ShapeDtypeStruct(shp, a.dtype)

    outs = [out_shape(a, k) for a, k in zip(arrays, kinds)]
    sems = [pltpu.SemaphoreType.DMA((3 * n_t,))] * 4 + [pltpu.SemaphoreType.DMA((n_t,))]
    return _Comm(arrays, outs, sems, run, {i: i for i in range(n_t) if kinds[i] != "small"})


def _all_reduce_small(part, name):
    r, n = part.shape

    def body(in_ref, out_ref, pair_ref, chips_ref, sib_sems, send_sems, recv_sems):
        x, y, c, me, chips = _place()
        pair_ref[c] = in_ref[...]
        swap = _remote(in_ref, pair_ref.at[c], sib_sems.at[0], sib_sems.at[1], (x, y, 1 - c))
        swap.start()
        _remote(in_ref, pair_ref.at[1 - c], sib_sems.at[0], sib_sems.at[1], (x, y, c)).wait_recv()
        chips_ref[me] = pair_ref[0] + pair_ref[1]
        sends = [_remote(chips_ref.at[me], chips_ref.at[me], send_sems.at[j], recv_sems.at[j], (*chip, c))
                 for j, chip in enumerate(chips)]
        for cp in sends:
            cp.start()
        for j, chip in enumerate(chips):
            landed = chips_ref.at[2 * chip[0] + chip[1]]
            _remote(landed, landed, send_sems.at[j], recv_sems.at[j], (x, y, c)).wait_recv()
        for cp in sends:
            cp.wait_send()
        swap.wait_send()
        total = chips_ref[0]
        for k in range(1, N_CHIPS):
            total = total + chips_ref[k]
        out_ref[...] = total

    vm = pl.BlockSpec(memory_space=pltpu.VMEM)
    return _pallas_call(
        body, name=name, in_specs=[vm], out_specs=vm, out_shape=jax.ShapeDtypeStruct((r, n), F32),
        scratch_shapes=[pltpu.VMEM((2, r, n), F32), pltpu.VMEM((N_CHIPS, r, n), F32),
                        pltpu.SemaphoreType.DMA((2,)), pltpu.SemaphoreType.DMA((N_CHIPS - 1,)),
                        pltpu.SemaphoreType.DMA((N_CHIPS - 1,))],
        compiler_params=pltpu.CompilerParams(vmem_limit_bytes=VMEM_LIMIT),
    )(part)


def _halves_view(name, dw):
    k, n = dw.shape
    s = 1 if name in COL else N_CHIPS
    return dw.reshape(s, 2, k // (2 * s), n)


def _swap_program(views):
    n_t = len(views)

    def run(phase, src, dst, sems):
        send_sems, recv_sems = sems
        x, y, c, _, _ = _place()
        copies = [_far(src[i].at[:, 1 - c], dst[i], send_sems.at[i], recv_sems.at[i], (x, y, 1 - c))
                  for i in range(n_t)]
        for cp in copies:
            if phase == "start":
                cp.start()
            elif phase == "finish":
                cp.wait()

    outs = [jax.ShapeDtypeStruct((v.shape[0],) + v.shape[2:], v.dtype) for v in views]
    sems = [pltpu.SemaphoreType.DMA((n_t,)), pltpu.SemaphoreType.DMA((n_t,))]
    return _Comm(views, outs, sems, run)


def _add_halves(view, got, place, col, name):
    s, _, r, n = view.shape
    if col:
        cw = n // N_CHIPS
        tr = _row_tile(r, cw)
        grid = (r // tr, N_CHIPS)
        in_specs = [pl.BlockSpec((None, None, tr, cw), lambda j, q, p: (0, p[0], j, q)),
                    pl.BlockSpec((None, tr, cw), lambda j, q, p: (0, j, q))]
        out_specs = [pl.BlockSpec((None, tr, cw), lambda j, q, p: (0, j, q)),
                     pl.BlockSpec((None, None, tr, cw), lambda j, q, p: (p[0], p[1], j, 0))]
    else:
        cw = n
        tr = _row_tile(r, n)
        grid = (s, r // tr)
        in_specs = [pl.BlockSpec((None, None, tr, n), lambda i, j, p: (i, p[0], j, 0)),
                    pl.BlockSpec((None, tr, n), lambda i, j, p: (i, j, 0))]
        out_specs = [pl.BlockSpec((None, tr, n), lambda i, j, p: (i, j, 0)),
                     pl.BlockSpec((None, None, tr, n), lambda i, j, p: (p[0], i, j, 0))]

    def body(p_ref, a_ref, b_ref, o_ref, z_ref):
        total = (a_ref[...].astype(F32) + b_ref[...].astype(F32)).astype(BF)
        o_ref[...] = total
        if col:
            @pl.when(pl.program_id(1) == p_ref[1])
            def _():
                z_ref[...] = total
        else:
            z_ref[...] = total

    return _pallas_call(
        body, name=name,
        grid_spec=pltpu.PrefetchScalarGridSpec(num_scalar_prefetch=1, grid=grid, in_specs=in_specs,
                                               out_specs=out_specs),
        out_shape=[jax.ShapeDtypeStruct((s, r, n), BF), jax.ShapeDtypeStruct((2, N_CHIPS, r, cw), BF)],
        compiler_params=_params(2),
    )(place, view, got)


def _scatter_program(sums, landing, names, part=(0, 1)):
    n_t = len(sums)
    index, count = part

    def rows(ref):
        q = ref.shape[-2] // count
        return ref if count == 1 else _window(ref, pl.ds(index * q, q), slice(None))

    def run(phase, src, dst, sems):
        ici_send, ici_recv, sib_send, sib_recv = sems
        x, y, c, me, chips = _place()
        sibling = (x, y, 1 - c)

        def piece(i, chip):
            if names[i] in COL:
                cw = src[i].shape[2] // N_CHIPS
                return src[i].at[0, :, pl.ds(pl.multiple_of(chip * cw, 128), cw)]
            return src[i].at[chip]

        local = []
        sends = []
        for i in range(n_t):
            sends.append(_far(rows(piece(i, me)), rows(dst[i].at[c, me]), sib_send.at[4 * i + 3],
                              sib_recv.at[4 * i + 3], sibling))
            for j, chip in enumerate(chips):
                k = 2 * chip[0] + chip[1]
                sends.append(_far(rows(piece(i, k)), rows(dst[i].at[c, me]), ici_send.at[3 * i + j],
                                  ici_recv.at[3 * i + j], (*chip, c)))
        if phase == "start":
            for cp in local + sends:
                cp.start()
            return
        passed = []
        for i in range(n_t):
            for j, chip in enumerate(chips):
                k = 2 * chip[0] + chip[1]
                landed = rows(dst[i].at[c, k])
                if phase == "forward":
                    _remote(landed, landed, ici_send.at[3 * i + j], ici_recv.at[3 * i + j], (*chip, c)).wait_recv()
                passed.append(_far(landed, landed, sib_send.at[4 * i + j], sib_recv.at[4 * i + j], sibling))
                if phase == "forward":
                    passed[-1].start()
        if phase == "forward":
            return
        for i in range(n_t):
            other = rows(dst[i].at[1 - c, me])
            _remote(other, other, sib_send.at[4 * i + 3], sib_recv.at[4 * i + 3], sibling).wait_recv()
            for j, chip in enumerate(chips):
                k = 2 * chip[0] + chip[1]
                other = rows(dst[i].at[1 - c, k])
                _remote(other, other, sib_send.at[4 * i + j], sib_recv.at[4 * i + j], sibling).wait_recv()
        for cp in sends + passed:
            cp.wait_send()
        for cp in local:
            cp.wait()

    outs = [jax.ShapeDtypeStruct(z.shape, z.dtype) for z in landing]
    sems = [pltpu.SemaphoreType.DMA((3 * n_t,))] * 2 + [pltpu.SemaphoreType.DMA((4 * n_t,))] * 2
    return _Comm(list(sums) + list(landing), outs, sems, run, {n_t + i: i for i in range(n_t)})


def _adamw(w, g, m, v):
    m = ADAM_B1 * m + (1.0 - ADAM_B1) * g
    v = ADAM_B2 * v + (1.0 - ADAM_B2) * (g * g)
    m_hat = m / (1.0 - ADAM_B1 ** ADAM_STEP)
    v_hat = v / (1.0 - ADAM_B2 ** ADAM_STEP)
    return -ADAM_LR * (m_hat / (jnp.sqrt(v_hat) + ADAM_EPS) + ADAM_WD * w), m, v


def _adam_shard(pieces, w, m, v, l, prev, name, comm=None):
    depth, rows, cw = w.shape
    hr = rows // 2
    tr = _row_tile(hr, cw, limit=1 << 18)
    n_i = hr // tr

    def body(*refs):
        z_ref, w_ref, m_ref, v_ref = refs[:4]
        g_ref, d_ref, nm_ref, nv_ref = refs[-4:]
        g = z_ref[0].astype(F32)
        for k in range(1, N_CHIPS):
            g = g + z_ref[k].astype(F32)
        g_ref[...] = g
        d_ref[...], nm_ref[...], nv_ref[...] = _adamw(w_ref[...], g, m_ref[...], v_ref[...])

    par = pl.BlockSpec((None, tr, cw), lambda h, i: (l, h * n_i + i, 0))
    out = jax.ShapeDtypeStruct((depth, rows, cw), F32)
    extra = [] if prev is None else list(prev)
    return _run(
        body, comm, name=name, grid=(2, n_i),
        in_specs=[pl.BlockSpec((None, N_CHIPS, tr, cw), lambda h, i: (h, 0, i, 0)), par, par, par] + [ANY] * len(extra),
        out_specs=[par] * 4, out_shape=[out] * 4, args=(pieces, w, m, v, *extra),
        aliases={4 + k: k for k in range(len(extra))})


def _adam_small(gs, ws, ms, vs, name):
    n = len(gs)

    def body(*refs):
        g, w, m, v, d, nm, nv = (refs[k * n:(k + 1) * n] for k in range(7))
        for i in range(n):
            d[i][...], nm[i][...], nv[i][...] = _adamw(w[i][...], g[i][...], m[i][...], v[i][...])

    vm = pl.BlockSpec(memory_space=pltpu.VMEM)
    outs = [jax.ShapeDtypeStruct(a.shape, F32) for a in ws] * 3
    res = _pallas_call(body, name=name, in_specs=[vm] * (4 * n), out_specs=[vm] * (3 * n), out_shape=outs)(
        *gs, *ws, *ms, *vs)
    return res[:n], res[n:2 * n], res[2 * n:]


WEIGHTS = ("norm_mix_g", "norm_mem_g", "w_in", "b_gate", "conv_a_w", "w_a_out", "conv_b_w", "conv_b_bias", "ln_b_g",
           "ln_b_b", "w_b_out", "w_kv", "w_att_out", "w_o", "norm_ffn_g", "w_up", "conv_ffn_w", "w_down",
           "norm_final_g")
REPLICATED = ("norm_mix_g", "norm_mem_g", "b_gate", "conv_b_bias", "ln_b_g", "ln_b_b", "norm_ffn_g")
CONVS = ("conv_a_w", "conv_b_w", "conv_ffn_w")
PACK_WIDTH = 1024


def _pack(arrays):
    flat = jnp.concatenate([a.reshape(-1) for a in arrays])
    size = -(-flat.shape[0] // (8 * PACK_WIDTH)) * (8 * PACK_WIDTH)
    return jnp.pad(flat, (0, size - flat.shape[0])).reshape(-1, PACK_WIDTH)


def _unpack(packed, shapes):
    flat = packed.reshape(-1)
    out, at = [], 0
    for shp in shapes:
        n = math.prod(shp)
        out.append(flat[at:at + n].reshape(shp))
        at += n
    return out


def kernel(x, mem, norm_mix_g, norm_mem_g, w_in, b_gate, conv_a_w, w_a_out, conv_b_w, conv_b_bias, ln_b_g, ln_b_b, w_b_out, w_kv, w_att_out, w_o, norm_ffn_g, w_up, conv_ffn_w, w_down, norm_final_g, loss_target, m_norm_mix_g, m_norm_mem_g, m_w_in, m_b_gate, m_conv_a_w, m_w_a_out, m_conv_b_w, m_conv_b_bias, m_ln_b_g, m_ln_b_b, m_w_b_out, m_w_kv, m_w_att_out, m_w_o, m_norm_ffn_g, m_w_up, m_conv_ffn_w, m_w_down, m_norm_final_g, v_norm_mix_g, v_norm_mem_g, v_w_in, v_b_gate, v_conv_a_w, v_w_a_out, v_conv_b_w, v_conv_b_bias, v_ln_b_g, v_ln_b_b, v_w_b_out, v_w_kv, v_w_att_out, v_w_o, v_norm_ffn_g, v_w_up, v_conv_ffn_w, v_w_down, v_norm_final_g):
    w = dict(norm_mix_g=norm_mix_g, norm_mem_g=norm_mem_g, w_in=w_in, b_gate=b_gate, conv_a_w=conv_a_w,
             w_a_out=w_a_out, conv_b_w=conv_b_w, conv_b_bias=conv_b_bias, ln_b_g=ln_b_g, ln_b_b=ln_b_b,
             w_b_out=w_b_out, w_kv=w_kv, w_att_out=w_att_out, w_o=w_o, norm_ffn_g=norm_ffn_g, w_up=w_up,
             conv_ffn_w=conv_ffn_w, w_down=w_down, norm_final_g=norm_final_g)
    mom = dict(norm_mix_g=m_norm_mix_g, norm_mem_g=m_norm_mem_g, w_in=m_w_in, b_gate=m_b_gate, conv_a_w=m_conv_a_w,
               w_a_out=m_w_a_out, conv_b_w=m_conv_b_w, conv_b_bias=m_conv_b_bias, ln_b_g=m_ln_b_g, ln_b_b=m_ln_b_b,
               w_b_out=m_w_b_out, w_kv=m_w_kv, w_att_out=m_w_att_out, w_o=m_w_o, norm_ffn_g=m_norm_ffn_g,
               w_up=m_w_up, conv_ffn_w=m_conv_ffn_w, w_down=m_w_down, norm_final_g=m_norm_final_g)
    var = dict(norm_mix_g=v_norm_mix_g, norm_mem_g=v_norm_mem_g, w_in=v_w_in, b_gate=v_b_gate, conv_a_w=v_conv_a_w,
               w_a_out=v_w_a_out, conv_b_w=v_conv_b_w, conv_b_bias=v_conv_b_bias, ln_b_g=v_ln_b_g, ln_b_b=v_ln_b_b,
               w_b_out=v_w_b_out, w_kv=v_w_kv, w_att_out=v_w_att_out, w_o=v_w_o, norm_ffn_g=v_norm_ffn_g,
               w_up=v_w_up, conv_ffn_w=v_conv_ffn_w, w_down=v_w_down, norm_final_g=v_norm_final_g)
    depth = w_in.shape[0]
    chip = 2 * lax.axis_index("x") + lax.axis_index("y")
    core = jnp.stack([lax.axis_index("c"), chip]).astype(jnp.int32)

    chip1 = chip.astype(jnp.int32).reshape(1)
    layers = range(depth)
    sh = dict(w_in=[_cast_place([w_in], l, "col", chip1, f"cast_w_in_{l}") for l in layers],
              wsq=[_cast_place([w[n] for n in SQUARES], l, "row", chip1, f"cast_squares_{l}") for l in layers],
              w_kv=[_cast_place([w_kv], l, "col", chip1, f"cast_w_kv_{l}") for l in layers],
              w_up=[_cast_place([w_up], l, "col", chip1, f"cast_w_up_{l}") for l in layers],
              w_down=[_cast_place([w_down], l, "row", chip1, f"cast_w_down_{l}") for l in layers])
    conv_sh = [jnp.concatenate([conv_a_w, conv_b_w], axis=1), conv_ffn_w]
    small = {n: w[n] for n in REPLICATED + ("norm_final_g",)}

    loss, dx, sgrads, dg_final, pieces, last = _step(x[0], mem[0], loss_target[0], sh, conv_sh, small, core)

    hosts = [("w_in", depth - 1), ("w_up", depth - 1), ("w_up", 0), ("w_down", depth - 1)] if depth > 1 else []
    if not hosts:
        last.scattered(_run_comm(last.scatter_program(), "scatter_w_in_0"))
    res = {n: None for n in BIG}
    for l in reversed(range(depth)):
        for n in [n for n in BIG if n != "w_in"] + ["w_in"]:
            comm = last.scatter_program(part=(hosts.index((n, l)), len(hosts))) if (n, l) in hosts else None
            piece = last.pieces["w_in"] if (n, l) == ("w_in", 0) else pieces[l][n]
            res[n], got = _adam_shard(piece, w[n], mom[n], var[n], l, res[n], f"adam_{n}_{l}", comm)
            if comm is not None:
                last.scattered(got)

    per_layer = REPLICATED + CONVS
    parts = [sgrads[l][n] for l in range(depth) for n in per_layer] + [dg_final, loss[0:1, 0:1]]
    total = _all_reduce_small(_pack(parts), "all_reduce_small")
    summed = _unpack(total, [p.shape for p in parts])
    g_small = {}
    for k, n in enumerate(per_layer):
        full = jnp.stack([summed[l * len(per_layer) + k] for l in range(depth)])
        if n in CONVS:
            cols = w[n].shape[-1]
            full = lax.dynamic_slice_in_dim(full, chip * cols, cols, axis=2)
        g_small[n] = full.reshape(w[n].shape)
    g_small["norm_final_g"] = summed[-2].reshape(norm_final_g.shape)
    names = per_layer + ("norm_final_g",)
    two_d = lambda a: a.reshape(1, -1) if a.ndim == 1 else a
    deltas, new_ms, new_vs = _adam_small(*[[two_d(t[n]) for n in names] for t in (g_small, w, mom, var)],
                                         "adam_small")
    for n, dl, nm, nv in zip(names, deltas, new_ms, new_vs):
        res[n] = (g_small[n], dl.reshape(w[n].shape), nm.reshape(w[n].shape), nv.reshape(w[n].shape))

    loss = summed[-1].reshape(())
    return (loss, dx.reshape(x.shape), *[res[n][0] for n in WEIGHTS], *[res[n][1] for n in WEIGHTS],
            *[res[n][2] for n in WEIGHTS], *[res[n][3] for n in WEIGHTS])
```

```python
import functools
import math

import jax
import jax.numpy as jnp
from jax import lax
from jax.experimental import pallas as pl
from jax.experimental.pallas import tpu as pltpu

F32 = jnp.float32
BF = jnp.bfloat16
EPS = 1e-6
N_HEADS = 4
K_A, K_B, K_F = 3, 31, 3
ADAM_LR, ADAM_B1, ADAM_B2, ADAM_EPS, ADAM_WD, ADAM_STEP = 0.001, 0.9, 0.999, 1e-08, 0.01, 10
N_CHIPS = 4
N_DEV = 8
HALO = 32
MAX_STRIPES = 8
TM_ROW = 256
TM_MM = 1024
TT_DW = 4096
DW_LHS_ELEMS = 4 * 1024 * 1024
TR_EW = 128
VMEM_LIMIT = 56 * 1024 * 1024
MESH = pl.DeviceIdType.MESH
_pallas_call = pl.pallas_call


def _params(n_axes):
    return pltpu.CompilerParams(dimension_semantics=("arbitrary",) * n_axes, vmem_limit_bytes=VMEM_LIMIT)


def _resident(shape, index):
    return pl.BlockSpec(shape, lambda *_: index, pipeline_mode=pl.Buffered(1))


def _sig(x):
    return 1.0 / (1.0 + jnp.exp(-x))


def _nt(a, b):
    return lax.dot_general(a, b, (((1,), (1,)), ((), ())), preferred_element_type=F32)


def _tn(a, b):
    return lax.dot_general(a, b, (((0,), (0,)), ((), ())), preferred_element_type=F32)


def _nn(a, b):
    return jnp.dot(a, b, preferred_element_type=F32)


class _Comm:
    def __init__(self, inputs, out_shape, scratch, run, aliases=None):
        self.inputs, self.out_shape, self.scratch, self.run = list(inputs), list(out_shape), list(scratch), run
        self.aliases = dict(aliases or {})


def _join(programs):
    programs = [p for p in programs if p is not None]
    if not programs:
        return None

    def split(seq, counts):
        parts, at = [], 0
        for n in counts:
            parts.append(seq[at:at + n])
            at += n
        return parts

    n_in = [len(p.inputs) for p in programs]
    n_out = [len(p.out_shape) for p in programs]
    n_s = [len(p.scratch) for p in programs]

    def run(phase, ins, outs, sems):
        for p, i, o, s in zip(programs, split(ins, n_in), split(outs, n_out), split(sems, n_s)):
            p.run(phase, i, o, s)

    aliases, in_at, out_at = {}, 0, 0
    for p, i, o in zip(programs, n_in, n_out):
        aliases.update({in_at + a: out_at + b for a, b in p.aliases.items()})
        in_at, out_at = in_at + i, out_at + o
    return _Comm([a for p in programs for a in p.inputs], [a for p in programs for a in p.out_shape],
                 [a for p in programs for a in p.scratch], run, aliases)


def _run(body, comm, *, name, grid, in_specs, out_specs, out_shape, args, scratch_shapes=(), aliases=None):
    n_axes = len(grid)
    aliases = dict(aliases or {})
    if comm is None:
        outs = _pallas_call(body, name=name, grid=grid, in_specs=list(in_specs), out_specs=list(out_specs),
                            out_shape=list(out_shape), scratch_shapes=list(scratch_shapes),
                            input_output_aliases=aliases, compiler_params=_params(n_axes))(*args)
        return list(outs), []
    counts = (len(in_specs), len(comm.inputs), len(out_specs), len(comm.out_shape), len(scratch_shapes),
              len(comm.scratch))

    def hosted(*refs):
        parts, at = [], 0
        for n in counts:
            parts.append(refs[at:at + n])
            at += n
        ins, c_ins, outs, c_outs, scr, c_sems = parts
        step = pl.program_id(0)
        for a in range(1, n_axes):
            step = step * grid[a] + pl.program_id(a)
        total = math.prod(grid)
        late = max(0, total - 1 - max(1, total // 4))

        @pl.when(step == 0)
        def _():
            comm.run("start", c_ins, c_outs, c_sems)

        body(*ins, *outs, *scr)

        @pl.when(step == late)
        def _():
            comm.run("forward", c_ins, c_outs, c_sems)

        @pl.when(step == total - 1)
        def _():
            comm.run("finish", c_ins, c_outs, c_sems)

    any_spec = pl.BlockSpec(memory_space=pl.ANY)
    res = _pallas_call(
        hosted, name=name, grid=grid, in_specs=list(in_specs) + [any_spec] * counts[1],
        out_specs=list(out_specs) + [any_spec] * counts[3], out_shape=list(out_shape) + comm.out_shape,
        scratch_shapes=list(scratch_shapes) + comm.scratch, compiler_params=_params(n_axes),
        input_output_aliases={**aliases, **{counts[0] + a: counts[2] + b for a, b in comm.aliases.items()}},
    )(*args, *comm.inputs)
    return list(res[:counts[2]]), list(res[counts[2]:])


def _run_comm(comm, name):
    n_in, n_out = len(comm.inputs), len(comm.out_shape)

    def body(*refs):
        ins, outs, sems = refs[:n_in], refs[n_in:n_in + n_out], refs[n_in + n_out:]
        for phase in ("start", "forward", "finish"):
            comm.run(phase, ins, outs, sems)

    any_spec = pl.BlockSpec(memory_space=pl.ANY)
    return list(_pallas_call(body, name=name, in_specs=[any_spec] * n_in, out_specs=[any_spec] * n_out,
                             out_shape=comm.out_shape, scratch_shapes=comm.scratch,
                             input_output_aliases=comm.aliases)(*comm.inputs))


SUBLANES = 8
LANES = 128
ROW_CHUNK = 128
ALL_RESIDUES = tuple(range(1, SUBLANES))
SHORT_RESIDUES = tuple(sorted({(HALO - K_A + 1 + k) % SUBLANES for k in range(K_A)} - {0}))


class _Rows:
    def __init__(self, ref, shifted_ref=None, residues=()):
        self.ref, self.shifted_ref, self.residues = ref, shifted_ref, tuple(residues)

    def shift(self):
        n = self.shifted_ref.shape[1]
        for j, b in enumerate(self.residues):
            self.shifted_ref[j] = self.ref[pl.ds(b, n), :]

    def at(self, offset, r0, c0, rows):
        b = offset % SUBLANES
        if b in self.residues:
            return self.shifted_ref[self.residues.index(b), pl.ds(offset - b + r0, rows), pl.ds(c0, LANES)]
        return self.ref[pl.ds(offset + r0, rows), pl.ds(c0, LANES)]


def _shifted_scratch(residues, tm, c):
    return pltpu.VMEM((len(residues), tm + HALO - SUBLANES, c), F32)


def _causal_offsets(k_taps):
    return [HALO - k_taps + 1 + k for k in range(k_taps)]


def _anticausal_offsets(k_taps):
    return [k_taps - 1 - k for k in range(k_taps)]


def _tap_chunk(src, w_ref, offsets, r0, c0, rows):
    acc = w_ref[0:1, pl.ds(c0, LANES)] * src.at(offsets[0], r0, c0, rows)
    for k in range(1, len(offsets)):
        acc = acc + w_ref[k:k + 1, pl.ds(c0, LANES)] * src.at(offsets[k], r0, c0, rows)
    return acc


def _conv_whole(ref, w_ref, offsets, tm):
    acc = w_ref[0:1, :] * ref[pl.ds(offsets[0], tm), :]
    for k in range(1, len(offsets)):
        acc = acc + w_ref[k:k + 1, :] * ref[pl.ds(offsets[k], tm), :]
    return acc


def _tap_grads_whole(dw_ref, dy, ref, offsets, tm):
    for k, off in enumerate(offsets):
        dw_ref[k:k + 1, :] += jnp.sum(dy * ref[pl.ds(off, tm), :], axis=0, keepdims=True)


def _conv_taps(src, w_ref, offsets, tm, emit):
    rows = min(ROW_CHUNK, tm)
    for c0 in range(0, w_ref.shape[1], LANES):
        for r0 in range(0, tm, rows):
            emit(r0, rows, c0, _tap_chunk(src, w_ref, offsets, r0, c0, rows))


def _tap_grads(dw_ref, dy_at, src, offsets, tm):
    rows = min(ROW_CHUNK // 2, tm)
    for c0 in range(0, dw_ref.shape[1], LANES):
        acc = [None] * len(offsets)
        for r0 in range(0, tm, rows):
            dy = dy_at(r0, rows, c0)
            for k, off in enumerate(offsets):
                part = (dy * src.at(off, r0, c0, rows)).reshape(rows // SUBLANES, SUBLANES, LANES).sum(axis=0)
                acc[k] = part if acc[k] is None else acc[k] + part
        for k in range(len(offsets)):
            dw_ref[k:k + 1, pl.ds(c0, LANES)] += jnp.sum(acc[k], axis=0, keepdims=True)


def _prev_halo(tm, col):
    return lambda i: (jnp.maximum(i * (tm // HALO) - 1, 0), col)


def _next_halo(tm, n_rows, col):
    return lambda i: (jnp.minimum((i + 1) * (tm // HALO), n_rows // HALO - 1), col)


def _norm_matmul(x, g, w, gs, name, comm=None):
    t, d = x.shape
    n_s, _, ns = w.shape
    n = n_s * ns
    tm = min(TM_MM, t)

    def body(x_ref, g_ref, w_ref, h_ref, y_ref):
        @pl.when(pl.program_id(1) == 0)
        def _():
            xf = x_ref[...]
            r = lax.rsqrt(jnp.mean(xf * xf, axis=-1, keepdims=True) + EPS)
            h_ref[...] = ((xf * r) * g_ref[...]).astype(BF)

        for s in range(gs):
            y_ref[:, s * ns:(s + 1) * ns] = _nn(h_ref[...], w_ref[s]).astype(BF)

    return _run(
        body, comm, name=name, grid=(t // tm, n_s // gs),
        in_specs=[pl.BlockSpec((tm, d), lambda i, j: (i, 0)),
                  pl.BlockSpec((1, d), lambda i, j: (0, 0)),
                  pl.BlockSpec((gs, d, ns), lambda i, j: (j, 0, 0))],
        out_specs=[pl.BlockSpec((tm, d), lambda i, j: (i, 0)),
                   pl.BlockSpec((tm, gs * ns), lambda i, j: (i, j))],
        out_shape=[jax.ShapeDtypeStruct((t, d), BF), jax.ShapeDtypeStruct((t, n), BF)],
        args=(x, g, w))


def _mem_kv(mem, g, w_kv, name):
    m, d = mem.shape
    n_s, _, ns = w_kv.shape

    def body(mem_ref, g_ref, w_ref, memn_ref, kv_ref):
        xf = mem_ref[...]
        r = lax.rsqrt(jnp.mean(xf * xf, axis=-1, keepdims=True) + EPS)
        memn = ((xf * r) * g_ref[...]).astype(BF)
        memn_ref[...] = memn
        for s in range(n_s):
            kv_ref[:, s * ns:(s + 1) * ns] = _nn(memn, w_ref[s]).astype(BF)

    return _pallas_call(
        body, name=name, grid=(1,),
        in_specs=[pl.BlockSpec((m, d), lambda i: (0, 0)),
                  pl.BlockSpec((1, d), lambda i: (0, 0)),
                  pl.BlockSpec((n_s, d, ns), lambda i: (0, 0, 0))],
        out_specs=[pl.BlockSpec((m, d), lambda i: (0, 0)),
                   pl.BlockSpec((m, 2 * d), lambda i: (0, 0))],
        out_shape=[jax.ShapeDtypeStruct((m, d), BF), jax.ShapeDtypeStruct((m, 2 * d), BF)],
        compiler_params=_params(1),
    )(mem, g, w_kv)


def _load_branch_inputs(i, proj_ref, gch_ref, vh_ref, u0h_ref, ugh_ref, xa_ref, xb_ref, d):
    gc = proj_ref[:, d:2 * d].astype(F32)
    v = proj_ref[:, 2 * d:3 * d].astype(F32)
    u0 = proj_ref[:, 3 * d:4 * d].astype(F32)
    ug = proj_ref[:, 4 * d:5 * d].astype(F32)
    keep = (i > 0).astype(F32)
    xa_ref[pl.ds(0, HALO), :] = gch_ref[...].astype(F32) * vh_ref[...].astype(F32) * keep
    xa_ref[pl.ds(HALO, gc.shape[0]), :] = gc * v
    xb_ref[pl.ds(0, HALO), :] = u0h_ref[...].astype(F32) * _sig(ugh_ref[...].astype(F32)) * keep
    xb_ref[pl.ds(HALO, gc.shape[0]), :] = u0 * _sig(ug)


def _softmax_rows(s):
    e = jnp.exp(s - jnp.max(s, axis=-1, keepdims=True))
    return e / jnp.sum(e, axis=-1, keepdims=True)


def _mixer_fwd(proj, x, kv, conv_a, conv_b, cbias, ln_g, ln_b, b_gate, wsq, l, name, comm=None):
    t, d = x.shape
    m = kv.shape[0]
    tm = min(TM_ROW, t)
    hd = d // N_HEADS
    scale = 1.0 / math.sqrt(hd)

    def body(proj_ref, gch_ref, vh_ref, u0h_ref, ugh_ref, x_ref, kv_ref, ca_w, cb_w, cbias_ref, lng_ref, lnb_ref,
             bg_ref, wa_ref, wb_ref, wc_ref, wo_ref,
             x1_ref, za_ref, zb_ref, o_ref, ya_ref, yb_ref, yc_ref, mg_ref, cb_ref, xa_ref, xb_ref, sb_ref):
        i = pl.program_id(0)
        _load_branch_inputs(i, proj_ref, gch_ref, vh_ref, u0h_ref, ugh_ref, xa_ref, xb_ref, d)
        xb = _Rows(xb_ref, sb_ref, ALL_RESIDUES)
        xb.shift()
        def put_za(r0, rows, c0, ca):
            gb = proj_ref[pl.ds(r0, rows), pl.ds(c0, LANES)].astype(F32)
            za_ref[pl.ds(r0, rows), pl.ds(c0, LANES)] = (gb * ca).astype(BF)

        _conv_taps(_Rows(xa_ref), ca_w, _causal_offsets(K_A), tm, put_za)
        ya = _nn(za_ref[...], wa_ref[...])
        ya_ref[...] = ya.astype(BF)

        def put_cb(r0, rows, c0, conv):
            cb_ref[pl.ds(r0, rows), pl.ds(c0, LANES)] = conv + cbias_ref[:, pl.ds(c0, LANES)]

        _conv_taps(xb, cb_w, _causal_offsets(K_B), tm, put_cb)
        cb = cb_ref[...]
        mu = jnp.mean(cb, axis=-1, keepdims=True)
        dlt = cb - mu
        rstd = lax.rsqrt(jnp.mean(dlt * dlt, axis=-1, keepdims=True) + EPS)
        lnb = (dlt * rstd) * lng_ref[...] + lnb_ref[...]
        zb = (lnb * _sig(lnb)).astype(BF)
        zb_ref[...] = zb
        yb = _nn(zb, wb_ref[...])
        yb_ref[...] = yb.astype(BF)
        for h in range(N_HEADS):
            qh = proj_ref[:, 5 * d + h * hd:5 * d + (h + 1) * hd]
            kh = kv_ref[:, h * hd:(h + 1) * hd]
            vh = kv_ref[:, d + h * hd:d + (h + 1) * hd]
            p = _softmax_rows(_nt(qh, kh) * scale)
            o_ref[:, h * hd:(h + 1) * hd] = _nn(p.astype(BF), vh).astype(BF)
        yc = _nn(o_ref[...], wc_ref[...])
        yc_ref[...] = yc.astype(BF)
        g0 = _sig(proj_ref[:, 6 * d:7 * d].astype(F32) + bg_ref[:, 0:d])
        g1 = _sig(proj_ref[:, 7 * d:8 * d].astype(F32) + bg_ref[:, d:2 * d])
        g2 = _sig(proj_ref[:, 8 * d:9 * d].astype(F32) + bg_ref[:, 2 * d:3 * d])
        mg = (g0 * ya + g1 * yb + g2 * yc).astype(BF)
        mg_ref[...] = mg
        x1_ref[...] = x_ref[...] + _nn(mg, wo_ref[...])

    row = lambda w_: pl.BlockSpec((tm, w_), lambda i: (i, 0))
    halo = lambda col: pl.BlockSpec((HALO, d), _prev_halo(tm, col))
    sq = lambda which: _resident((None, None, d, d), (l, which, 0, 0))
    act = jax.ShapeDtypeStruct((t, d), BF)
    return _run(
        body, comm, name=name, grid=(t // tm,),
        in_specs=[row(9 * d), halo(1), halo(2), halo(3), halo(4), row(d),
                  _resident((m, 2 * d), (0, 0)),
                  _resident((None, K_A, d), (l, 0, 0)), _resident((None, K_B, d), (l, 0, 0)),
                  _resident((1, d), (0, 0)), _resident((1, d), (0, 0)), _resident((1, d), (0, 0)),
                  _resident((1, 3 * d), (0, 0)), sq(0), sq(1), sq(2), sq(3)],
        out_specs=[row(d)] * 9,
        out_shape=[jax.ShapeDtypeStruct((t, d), F32)] + [act] * 7 + [jax.ShapeDtypeStruct((t, d), F32)],
        scratch_shapes=[pltpu.VMEM((HALO + tm, d), F32), pltpu.VMEM((HALO + tm, d), F32),
                        _shifted_scratch(ALL_RESIDUES, tm, d)],
        args=(proj, proj, proj, proj, proj, x, kv, conv_a, conv_b, cbias, ln_g, ln_b, b_gate, wsq, wsq, wsq, wsq))


def _ffn_down_fwd(up, x1, conv_f, w_down, l, name, comm=None):
    t, d = x1.shape
    f2 = up.shape[1]
    f = f2 // 2
    tm = min(TM_ROW, t)

    def body(up_ref, uph_ref, x1_ref, cw_ref, wd_ref, x2_ref, zf_ref, uc_ref, xx_ref):
        i = pl.program_id(0)
        xx_ref[pl.ds(0, HALO), :] = uph_ref[...].astype(F32) * (i > 0).astype(F32)
        xx_ref[pl.ds(HALO, tm), :] = up_ref[...].astype(F32)
        uc = _conv_whole(xx_ref, cw_ref, _causal_offsets(K_F), tm)
        uc_ref[...] = uc.astype(BF)
        gt = uc[:, 0:f]
        zf = (gt * _sig(gt) * uc[:, f:f2]).astype(BF)
        zf_ref[...] = zf
        x2_ref[...] = x1_ref[...] + _nn(zf, wd_ref[...])

    return _run(
        body, comm, name=name, grid=(t // tm,),
        in_specs=[pl.BlockSpec((tm, f2), lambda i: (i, 0)),
                  pl.BlockSpec((HALO, f2), _prev_halo(tm, 0)),
                  pl.BlockSpec((tm, d), lambda i: (i, 0)),
                  _resident((None, K_F, f2), (l, 0, 0)),
                  _resident((None, f, d), (l, 0, 0))],
        out_specs=[pl.BlockSpec((tm, d), lambda i: (i, 0)), pl.BlockSpec((tm, f), lambda i: (i, 0)),
                   pl.BlockSpec((tm, f2), lambda i: (i, 0))],
        out_shape=[jax.ShapeDtypeStruct((t, d), F32), jax.ShapeDtypeStruct((t, f), BF),
                   jax.ShapeDtypeStruct((t, f2), BF)],
        scratch_shapes=[pltpu.VMEM((HALO + tm, f2), F32)],
        args=(up, up, x1, conv_f, w_down))


def _final_loss(x, g, target, name):
    t, d = x.shape
    tm = min(2 * TM_ROW, t)

    def body(x_ref, g_ref, t_ref, dx_ref, loss_ref, dg_ref):
        @pl.when(pl.program_id(0) == 0)
        def _():
            loss_ref[...] = jnp.zeros_like(loss_ref)
            dg_ref[...] = jnp.zeros_like(dg_ref)

        xf = x_ref[...]
        r = lax.rsqrt(jnp.mean(xf * xf, axis=-1, keepdims=True) + EPS)
        xhat = xf * r
        err = xhat * g_ref[...] - t_ref[...]
        loss_ref[...] += (0.5 / d) * jnp.sum(err * err)
        dy = err * (1.0 / d)
        dg_ref[...] += jnp.sum(dy * xhat, axis=0, keepdims=True)
        dxh = dy * g_ref[...]
        dx_ref[...] = r * (dxh - xhat * jnp.mean(dxh * xhat, axis=-1, keepdims=True))

    return _pallas_call(
        body, name=name, grid=(t // tm,),
        in_specs=[pl.BlockSpec((tm, d), lambda i: (i, 0)), pl.BlockSpec((1, d), lambda i: (0, 0)),
                  pl.BlockSpec((tm, d), lambda i: (i, 0))],
        out_specs=[pl.BlockSpec((tm, d), lambda i: (i, 0)), pl.BlockSpec((8, 128), lambda i: (0, 0)),
                   pl.BlockSpec((1, d), lambda i: (0, 0))],
        out_shape=[jax.ShapeDtypeStruct((t, d), F32), jax.ShapeDtypeStruct((8, 128), F32),
                   jax.ShapeDtypeStruct((1, d), F32)],
        compiler_params=_params(1),
    )(x, g, target)


def _ffn_down_bwd(dx2, up, uc, w_down, l, name, comm=None):
    t, d = dx2.shape
    f2 = up.shape[1]
    f = f2 // 2
    tm = min(TM_ROW, t)

    def body(dx2_ref, up_ref, uph_ref, uc_ref, wd_ref, duc_ref, dx2b_ref, dcw_ref, xx_ref):
        i = pl.program_id(0)

        @pl.when(i == 0)
        def _():
            dcw_ref[...] = jnp.zeros_like(dcw_ref)

        xx_ref[pl.ds(0, HALO), :] = uph_ref[...].astype(F32) * (i > 0).astype(F32)
        xx_ref[pl.ds(HALO, tm), :] = up_ref[...].astype(F32)
        gt = uc_ref[:, 0:f].astype(F32)
        sg = _sig(gt)
        dx2b = dx2_ref[...].astype(BF)
        dx2b_ref[...] = dx2b
        dzf = _nt(dx2b, wd_ref[...])
        duc_ref[:, 0:f] = (dzf * uc_ref[:, f:f2].astype(F32) * (sg * (1.0 + gt * (1.0 - sg)))).astype(BF)
        duc_ref[:, f:f2] = (dzf * (gt * sg)).astype(BF)
        _tap_grads_whole(dcw_ref, duc_ref[...].astype(F32), xx_ref, _causal_offsets(K_F), tm)

    return _run(
        body, comm, name=name, grid=(t // tm,),
        in_specs=[pl.BlockSpec((tm, d), lambda i: (i, 0)),
                  pl.BlockSpec((tm, f2), lambda i: (i, 0)),
                  pl.BlockSpec((HALO, f2), _prev_halo(tm, 0)),
                  pl.BlockSpec((tm, f2), lambda i: (i, 0)),
                  _resident((None, f, d), (l, 0, 0))],
        out_specs=[pl.BlockSpec((tm, f2), lambda i: (i, 0)), pl.BlockSpec((tm, d), lambda i: (i, 0)),
                   pl.BlockSpec((K_F, f2), lambda i: (0, 0))],
        out_shape=[jax.ShapeDtypeStruct((t, f2), BF), jax.ShapeDtypeStruct((t, d), BF),
                   jax.ShapeDtypeStruct((K_F, f2), F32)],
        scratch_shapes=[pltpu.VMEM((HALO + tm, f2), F32)],
        args=(dx2, up, up, uc, w_down))


def _ffn_conv_bwd(duc, conv_f, l, name, comm=None):
    t, f2 = duc.shape
    tm = min(TM_ROW, t)
    n_t = t // tm

    def body(duc_ref, nxt_ref, cw_ref, dup_ref, yy_ref):
        i = pl.program_id(0)
        yy_ref[pl.ds(0, tm), :] = duc_ref[...].astype(F32)
        yy_ref[pl.ds(tm, HALO), :] = nxt_ref[...].astype(F32) * (i < n_t - 1).astype(F32)

        def put(r0, rows, c0, conv):
            dup_ref[pl.ds(r0, rows), pl.ds(c0, LANES)] = conv.astype(BF)

        _conv_taps(_Rows(yy_ref), cw_ref, _anticausal_offsets(K_F), tm, put)

    return _run(
        body, comm, name=name, grid=(n_t,),
        in_specs=[pl.BlockSpec((tm, f2), lambda i: (i, 0)),
                  pl.BlockSpec((HALO, f2), _next_halo(tm, t, 0)),
                  _resident((None, K_F, f2), (l, 0, 0))],
        out_specs=[pl.BlockSpec((tm, f2), lambda i: (i, 0))],
        out_shape=[jax.ShapeDtypeStruct((t, f2), BF)],
        scratch_shapes=[pltpu.VMEM((tm + HALO, f2), F32)],
        args=(duc, duc, conv_f))


def _nt_matmul_norm_bwd(dy, w, gs, x, g, dres, name, comm=None):
    t, n = dy.shape
    d = x.shape[1]
    tm = min(TM_MM // 2, t)
    n_s, _, ns = w.shape
    n_k, tk = n_s // gs, gs * ns

    def body(dy_ref, w_ref, x_ref, g_ref, dres_ref, dx_ref, dxb_ref, dg_ref, acc_ref):
        i, k = pl.program_id(0), pl.program_id(1)

        @pl.when((i == 0) & (k == 0))
        def _():
            dg_ref[...] = jnp.zeros_like(dg_ref)

        @pl.when(k == 0)
        def _():
            acc_ref[...] = jnp.zeros_like(acc_ref)

        part = _nt(dy_ref[:, 0:ns], w_ref[0])
        for s in range(1, gs):
            part = part + _nt(dy_ref[:, s * ns:(s + 1) * ns], w_ref[s])
        acc_ref[...] += part

        @pl.when(k == n_k - 1)
        def _():
            xf = x_ref[...]
            r = lax.rsqrt(jnp.mean(xf * xf, axis=-1, keepdims=True) + EPS)
            xhat = xf * r
            dh = acc_ref[...]
            dg_ref[...] += jnp.sum(dh * xhat, axis=0, keepdims=True)
            dxh = dh * g_ref[...]
            dx = dres_ref[...] + r * (dxh - xhat * jnp.mean(dxh * xhat, axis=-1, keepdims=True))
            dx_ref[...] = dx
            dxb_ref[...] = dx.astype(BF)

    return _run(
        body, comm, name=name, grid=(t // tm, n_k),
        in_specs=[pl.BlockSpec((tm, tk), lambda i, k: (i, k)),
                  pl.BlockSpec((gs, d, ns), lambda i, k: (k, 0, 0)),
                  pl.BlockSpec((tm, d), lambda i, k: (i, 0)),
                  pl.BlockSpec((1, d), lambda i, k: (0, 0)),
                  pl.BlockSpec((tm, d), lambda i, k: (i, 0))],
        out_specs=[pl.BlockSpec((tm, d), lambda i, k: (i, 0)), pl.BlockSpec((tm, d), lambda i, k: (i, 0)),
                   pl.BlockSpec((1, d), lambda i, k: (0, 0))],
        out_shape=[jax.ShapeDtypeStruct((t, d), F32), jax.ShapeDtypeStruct((t, d), BF),
                   jax.ShapeDtypeStruct((1, d), F32)],
        scratch_shapes=[pltpu.VMEM((tm, d), F32)],
        args=(dy, w, x, g, dres))


def _mixer_bwd(dx1b, proj, ya, yb, yc, cb, kv, conv_a, conv_b, ln_g, ln_b, b_gate, wsq, l, name, comm=None):
    t, d = cb.shape
    m = kv.shape[0]
    tm = min(TM_ROW, t)
    hd = d // N_HEADS
    scale = 1.0 / math.sqrt(hd)

    def body(dx1b_ref, proj_ref, gch_ref, vh_ref, u0h_ref, ugh_ref, ya_ref, yb_ref, yc_ref, cb_ref, kv_ref,
             ca_w, cb_w, lng_ref, lnb_ref, bg_ref, wa_ref, wb_ref, wc_ref, wo_ref,
             dpre_ref, dya_ref, dyb_ref, dyc_ref, dkv_ref, dbg_ref, dlng_ref, dlnb_ref, dcbias_ref, dcaw_ref,
             dcbw_ref, xa_ref, xb_ref, sa_ref, sb_ref):
        i = pl.program_id(0)

        @pl.when(i == 0)
        def _():
            for ref in (dkv_ref, dbg_ref, dlng_ref, dlnb_ref, dcbias_ref, dcaw_ref, dcbw_ref):
                ref[...] = jnp.zeros_like(ref)

        _load_branch_inputs(i, proj_ref, gch_ref, vh_ref, u0h_ref, ugh_ref, xa_ref, xb_ref, d)
        xa, xb = _Rows(xa_ref, sa_ref, SHORT_RESIDUES), _Rows(xb_ref, sb_ref, ALL_RESIDUES)
        xa.shift()
        xb.shift()
        dmg = _nt(dx1b_ref[...], wo_ref[...])
        ys = (ya_ref, yb_ref, yc_ref)
        dys = (dya_ref, dyb_ref, dyc_ref)
        for b in range(3):
            gate = _sig(proj_ref[:, (6 + b) * d:(7 + b) * d].astype(F32) + bg_ref[:, b * d:(b + 1) * d])
            dys[b][...] = (gate * dmg).astype(BF)
            dpg = dmg * ys[b][...].astype(F32) * gate * (1.0 - gate)
            dpre_ref[:, (6 + b) * d:(7 + b) * d] = dpg.astype(BF)
            dbg_ref[:, b * d:(b + 1) * d] += jnp.sum(dpg, axis=0, keepdims=True)
        gb = proj_ref[:, 0:d].astype(F32)
        dza = _nt(dya_ref[...], wa_ref[...])

        def put_dgb(r0, rows, c0, ca):
            dpre_ref[pl.ds(r0, rows), pl.ds(c0, LANES)] = (dza[r0:r0 + rows, c0:c0 + LANES] * ca).astype(BF)

        _conv_taps(xa, ca_w, _causal_offsets(K_A), tm, put_dgb)
        dpre_ref[:, d:2 * d] = (dza * gb).astype(BF)
        _tap_grads(dcaw_ref, lambda r0, rows, c0: dpre_ref[pl.ds(r0, rows), pl.ds(d + c0, LANES)].astype(F32),
                   xa, _causal_offsets(K_A), tm)
        dpre_ref[:, 2 * d:3 * d] = jnp.zeros((tm, d), BF)
        cbv = cb_ref[...]
        mu = jnp.mean(cbv, axis=-1, keepdims=True)
        dlt = cbv - mu
        rstd = lax.rsqrt(jnp.mean(dlt * dlt, axis=-1, keepdims=True) + EPS)
        xhat = dlt * rstd
        lnb = xhat * lng_ref[...] + lnb_ref[...]
        sg = _sig(lnb)
        dzb = _nt(dyb_ref[...], wb_ref[...])
        dl = dzb * (sg * (1.0 + lnb * (1.0 - sg)))
        dlng_ref[...] += jnp.sum(dl * xhat, axis=0, keepdims=True)
        dlnb_ref[...] += jnp.sum(dl, axis=0, keepdims=True)
        dxh = dl * lng_ref[...]
        dcb = rstd * (dxh - jnp.mean(dxh, axis=-1, keepdims=True)
                      - xhat * jnp.mean(dxh * xhat, axis=-1, keepdims=True))
        dcbias_ref[...] += jnp.sum(dcb, axis=0, keepdims=True)
        dpre_ref[:, 3 * d:4 * d] = dcb.astype(BF)
        _tap_grads(dcbw_ref, lambda r0, rows, c0: dpre_ref[pl.ds(r0, rows), pl.ds(3 * d + c0, LANES)].astype(F32),
                   xb, _causal_offsets(K_B), tm)
        dpre_ref[:, 4 * d:5 * d] = jnp.zeros((tm, d), BF)
        do = _nt(dyc_ref[...], wc_ref[...]).astype(BF)
        for h in range(N_HEADS):
            qh = proj_ref[:, 5 * d + h * hd:5 * d + (h + 1) * hd]
            kh = kv_ref[:, h * hd:(h + 1) * hd]
            vh = kv_ref[:, d + h * hd:d + (h + 1) * hd]
            doh = do[:, h * hd:(h + 1) * hd]
            p = _softmax_rows(_nt(qh, kh) * scale)
            dp = _nt(doh, vh)
            ds = (p * (dp - jnp.sum(dp * p, axis=-1, keepdims=True)) * scale).astype(BF)
            dpre_ref[:, 5 * d + h * hd:5 * d + (h + 1) * hd] = _nn(ds, kh).astype(BF)
            dkv_ref[:, h * hd:(h + 1) * hd] += _tn(ds, qh)
            dkv_ref[:, d + h * hd:d + (h + 1) * hd] += _tn(p.astype(BF), doh)

    row = lambda w_: pl.BlockSpec((tm, w_), lambda i: (i, 0))
    halo = lambda col: pl.BlockSpec((HALO, d), _prev_halo(tm, col))
    sq = lambda which: _resident((None, None, d, d), (l, which, 0, 0))
    acc = lambda r, c: pl.BlockSpec((r, c), lambda i: (0, 0))
    act = jax.ShapeDtypeStruct((t, d), BF)
    vec = lambda r, c: jax.ShapeDtypeStruct((r, c), F32)
    return _run(
        body, comm, name=name, grid=(t // tm,),
        in_specs=[row(d), row(9 * d), halo(1), halo(2), halo(3), halo(4), row(d), row(d), row(d), row(d),
                  _resident((m, 2 * d), (0, 0)),
                  _resident((None, K_A, d), (l, 0, 0)), _resident((None, K_B, d), (l, 0, 0)),
                  _resident((1, d), (0, 0)), _resident((1, d), (0, 0)), _resident((1, 3 * d), (0, 0)),
                  sq(0), sq(1), sq(2), sq(3)],
        out_specs=[row(9 * d), row(d), row(d), row(d), acc(m, 2 * d), acc(1, 3 * d), acc(1, d), acc(1, d),
                   acc(1, d), acc(K_A, d), acc(K_B, d)],
        out_shape=[jax.ShapeDtypeStruct((t, 9 * d), BF), act, act, act, vec(m, 2 * d), vec(1, 3 * d), vec(1, d),
                   vec(1, d), vec(1, d), vec(K_A, d), vec(K_B, d)],
        scratch_shapes=[pltpu.VMEM((HALO + tm, d), F32), pltpu.VMEM((HALO + tm, d), F32),
                        _shifted_scratch(SHORT_RESIDUES, tm, d), _shifted_scratch(ALL_RESIDUES, tm, d)],
        args=(dx1b, proj, proj, proj, proj, proj, ya, yb, yc, cb, kv, conv_a, conv_b, ln_g, ln_b, b_gate,
              wsq, wsq, wsq, wsq))


def _inproj_conv_bwd(dpre, proj, conv_a, conv_b, l, name, comm=None):
    t, d9 = dpre.shape
    d = d9 // 9
    tm = min(TM_ROW, t)
    n_t = t // tm

    def body(dpre_ref, nxa_ref, nxb_ref, proj_ref, ca_w, cb_w, dproj_ref, ya_ref, yb_ref, sb_ref):
        i = pl.program_id(0)
        keep = (i < n_t - 1).astype(F32)
        ya_ref[pl.ds(0, tm), :] = dpre_ref[:, d:2 * d].astype(F32)
        ya_ref[pl.ds(tm, HALO), :] = nxa_ref[...].astype(F32) * keep
        yb_ref[pl.ds(0, tm), :] = dpre_ref[:, 3 * d:4 * d].astype(F32)
        yb_ref[pl.ds(tm, HALO), :] = nxb_ref[...].astype(F32) * keep
        yb = _Rows(yb_ref, sb_ref, ALL_RESIDUES)
        yb.shift()
        dproj_ref[:, 0:d] = dpre_ref[:, 0:d]
        dproj_ref[:, 5 * d:9 * d] = dpre_ref[:, 5 * d:9 * d]
        def chunk(ref, block, r0, rows, c0):
            return ref.at[pl.ds(r0, rows), pl.ds(block * d + c0, LANES)]

        def put_a(r0, rows, c0, dcv):
            chunk(dproj_ref, 1, r0, rows, c0)[...] = (dcv * chunk(proj_ref, 2, r0, rows, c0)[...].astype(F32)).astype(BF)
            chunk(dproj_ref, 2, r0, rows, c0)[...] = (dcv * chunk(proj_ref, 1, r0, rows, c0)[...].astype(F32)).astype(BF)

        def put_b(r0, rows, c0, dub):
            sg = _sig(chunk(proj_ref, 4, r0, rows, c0)[...].astype(F32))
            u0 = chunk(proj_ref, 3, r0, rows, c0)[...].astype(F32)
            chunk(dproj_ref, 3, r0, rows, c0)[...] = (dub * sg).astype(BF)
            chunk(dproj_ref, 4, r0, rows, c0)[...] = (dub * u0 * sg * (1.0 - sg)).astype(BF)

        _conv_taps(_Rows(ya_ref), ca_w, _anticausal_offsets(K_A), tm, put_a)
        _conv_taps(yb, cb_w, _anticausal_offsets(K_B), tm, put_b)

    return _run(
        body, comm, name=name, grid=(n_t,),
        in_specs=[pl.BlockSpec((tm, d9), lambda i: (i, 0)),
                  pl.BlockSpec((HALO, d), _next_halo(tm, t, 1)),
                  pl.BlockSpec((HALO, d), _next_halo(tm, t, 3)),
                  pl.BlockSpec((tm, d9), lambda i: (i, 0)),
                  _resident((None, K_A, d), (l, 0, 0)), _resident((None, K_B, d), (l, 0, 0))],
        out_specs=[pl.BlockSpec((tm, d9), lambda i: (i, 0))],
        out_shape=[jax.ShapeDtypeStruct((t, d9), BF)],
        scratch_shapes=[pltpu.VMEM((tm + HALO, d), F32), pltpu.VMEM((tm + HALO, d), F32),
                        _shifted_scratch(ALL_RESIDUES, tm, d)],
        args=(dpre, dpre, dpre, proj, conv_a, conv_b))


def _mem_kv_bwd(dkv, memn, mem, g, w_kv, name):
    m, d = mem.shape
    n_s, _, ns = w_kv.shape

    def body(dkv_ref, memn_ref, mem_ref, g_ref, w_ref, dw_ref, dg_ref):
        dkvb = dkv_ref[...].astype(BF)
        dw_ref[...] = _tn(memn_ref[...], dkvb).astype(BF)
        dmemn = _nt(dkvb[:, 0:ns], w_ref[0])
        for s in range(1, n_s):
            dmemn = dmemn + _nt(dkvb[:, s * ns:(s + 1) * ns], w_ref[s])
        xf = mem_ref[...]
        r = lax.rsqrt(jnp.mean(xf * xf, axis=-1, keepdims=True) + EPS)
        dg_ref[...] = jnp.sum(dmemn * (xf * r), axis=0, keepdims=True)

    return _pallas_call(
        body, name=name, grid=(1,),
        in_specs=[pl.BlockSpec((m, 2 * d), lambda i: (0, 0)), pl.BlockSpec((m, d), lambda i: (0, 0)),
                  pl.BlockSpec((m, d), lambda i: (0, 0)), pl.BlockSpec((1, d), lambda i: (0, 0)),
                  pl.BlockSpec((n_s, d, ns), lambda i: (0, 0, 0))],
        out_specs=[pl.BlockSpec((d, 2 * d), lambda i: (0, 0)), pl.BlockSpec((1, d), lambda i: (0, 0))],
        out_shape=[jax.ShapeDtypeStruct((d, 2 * d), BF), jax.ShapeDtypeStruct((1, d), F32)],
        compiler_params=_params(1),
    )(dkv, memn, mem, g, w_kv)


def _dw_matmul(a, b, tn, name, comm=None):
    t, k = a.shape
    n = b.shape[1]
    tt = min(TT_DW, t)
    while tt * k > DW_LHS_ELEMS and tt % 2 == 0:
        tt //= 2
    n_s = t // tt

    def body(a_ref, b_ref, o_ref, acc_ref):
        s = pl.program_id(1)
        part = _tn(a_ref[...], b_ref[...])
        if n_s == 1:
            o_ref[...] = part.astype(BF)
            return

        @pl.when(s == 0)
        def _():
            acc_ref[...] = part

        @pl.when(s > 0)
        def _():
            acc_ref[...] += part

        @pl.when(s == n_s - 1)
        def _():
            o_ref[...] = acc_ref[...].astype(BF)

    return _run(
        body, comm, name=name, grid=(n // tn, n_s),
        in_specs=[pl.BlockSpec((tt, k), lambda j, s: (s, 0)), pl.BlockSpec((tt, tn), lambda j, s: (s, j))],
        out_specs=[pl.BlockSpec((k, tn), lambda j, s: (0, j))],
        out_shape=[jax.ShapeDtypeStruct((k, n), BF)],
        scratch_shapes=[pltpu.VMEM((k, tn) if n_s > 1 else (8, 128), F32)],
        args=(a, b))


class _GradReduce:
    def __init__(self, l, grads, core):
        self.l, self.core, self.names = l, core, tuple(grads)
        self.views = {n: _halves_view(n, g) for n, g in grads.items()}
        self.got, self.sums, self.landing, self.pieces = {}, {}, {}, {}

    def swap_program(self):
        return _swap_program([self.views[n] for n in self.names])

    def swapped(self, outs):
        self.got = dict(zip(self.names, outs))
        for n in self.names:
            self.sums[n], self.landing[n] = _add_halves(self.views[n], self.got[n], self.core, n in COL,
                                                        f"add_halves_{n}_{self.l}")

    def scatter_program(self, names=None, part=(0, 1)):
        names = self.names if names is None else names
        return _scatter_program([self.sums[n] for n in names], [self.landing[n] for n in names], names, part)

    def scattered(self, outs, names=None):
        names = self.names if names is None else names
        self.landing.update(zip(names, outs))
        self.pieces.update(zip(names, outs))


def _step(x, mem, target, sh, conv_sh, small, core):
    depth = len(sh["w_in"])
    d = x.shape[1]
    f2 = sh["w_up"][0].shape[2] * N_CHIPS
    tn_dw_in = min(1024, d)
    tn_dw_sq = max(LANES, d // 4)
    tn_dw_up = f2 // 11 if f2 % (11 * 128) == 0 and f2 // 11 >= 128 else f2
    row = lambda v: v.reshape(1, -1)
    one = lambda a: a[None]

    w_in, cab, cf = _run_comm(_gather_program([sh["w_in"][0]] + conv_sh, ("col", "small", "small")), "gather_first")
    saved = []
    for l in range(depth):
        conv = dict(a=cab[l:l + 1, :K_A], b=cab[l:l + 1, K_A:], f=cf[l:l + 1])
        (h, proj), (wsq, w_kv) = _norm_matmul(
            x, row(small["norm_mix_g"][l]), w_in, 1, f"in_proj_{l}",
            _gather_program([sh["wsq"][l], sh["w_kv"][l]], ("row", "col")))
        memn, kv = _mem_kv(mem, row(small["norm_mem_g"][l]), w_kv, f"mem_kv_{l}")
        (x1, za, zb, o, ya, yb, yc, mg, cb), (w_up, w_down) = _mixer_fwd(
            proj, x, kv, conv["a"], conv["b"], row(small["conv_b_bias"][l]), row(small["ln_b_g"][l]),
            row(small["ln_b_b"][l]), row(small["b_gate"][l]), one(wsq), 0, f"mixer_fwd_{l}",
            _gather_program([sh["w_up"][l], sh["w_down"][l]], ("col", "row")))
        more = l + 1 < depth
        nxt = _gather_program([sh["w_in"][l + 1]], ("col",), (0, 2)) if more else None
        (h2, up), w_in_next = _norm_matmul(x1, row(small["norm_ffn_g"][l]), w_up, 2, f"up_proj_{l}", nxt)
        nxt = _gather_program(w_in_next, ("col",), (1, 2)) if more else None
        (x2, zf, uc), w_in_next = _ffn_down_fwd(up, x1, conv["f"], one(w_down), 0, f"ffn_down_fwd_{l}", nxt)
        saved.append(dict(x=x, x1=x1, memn=memn, kv=kv, h=h, proj=proj, za=za, zb=zb, o=o, ya=ya, yb=yb, yc=yc,
                          mg=mg, cb=cb, h2=h2, up=up, zf=zf, uc=uc, w_in=w_in, wsq=one(wsq), w_kv=w_kv,
                          w_up=w_up, w_down=one(w_down), conv=conv))
        x = x2
        w_in = w_in_next[0] if w_in_next else None
    dx, loss, dg_final = _final_loss(x, row(small["norm_final_g"]), target, "final_loss")
    sgrads, pieces = [None] * depth, [None] * depth
    above = None
    first, second, rest = SQUARES, ("w_up",), ("w_in", "w_kv", "w_down")
    for l in reversed(range(depth)):
        s = saved[l]
        conv = s["conv"]
        bottom = l == 0
        (duc, dx2b, dconv_f), got = _ffn_down_bwd(dx, s["up"], s["uc"], s["w_down"], 0, f"ffn_down_bwd_{l}",
                                                  above and above.swap_program())
        if above:
            above.swapped(got)
        (dup,), got = _ffn_conv_bwd(duc, conv["f"], 0, f"ffn_conv_bwd_{l}", above and above.scatter_program(first))
        if above:
            above.scattered(got, first)
        (dx1, dx1b, dg_ffn), got = _nt_matmul_norm_bwd(
            dup, s["w_up"], 2, s["x1"], row(small["norm_ffn_g"][l]), dx, f"up_proj_bwd_{l}",
            above and above.scatter_program(second))
        if above:
            above.scattered(got, second)
        grads = dict(w_up=_dw_matmul(s["h2"], dup, tn_dw_up, f"dw_up_{l}")[0][0],
                     w_down=_dw_matmul(s["zf"], dx2b, d, f"dw_down_{l}")[0][0])
        ffn = _GradReduce(l, grads, core) if bottom else None
        (dpre, dya, dyb, dyc, dkv, dbg, dlng, dlnb, dcbias, dconv_a, dconv_b), got = _mixer_bwd(
            dx1b, s["proj"], s["ya"], s["yb"], s["yc"], s["cb"], s["kv"], conv["a"], conv["b"],
            row(small["ln_b_g"][l]), row(small["ln_b_b"][l]), row(small["b_gate"][l]), s["wsq"], 0,
            f"mixer_bwd_{l}", _join([above and above.scatter_program(rest), ffn and ffn.swap_program()]))
        if above:
            above.scattered(got[:len(rest)], rest)
            pieces[above.l] = above.pieces
            got = got[len(rest):]
        if ffn:
            ffn.swapped(got)
        dw_kv, dg_mem = _mem_kv_bwd(dkv, s["memn"], mem, row(small["norm_mem_g"][l]), s["w_kv"], f"mem_kv_bwd_{l}")
        mix_grads = dict(w_a_out=_dw_matmul(s["za"], dya, tn_dw_sq, f"dw_a_out_{l}")[0][0],
                         w_b_out=_dw_matmul(s["zb"], dyb, tn_dw_sq, f"dw_b_out_{l}")[0][0],
                         w_att_out=_dw_matmul(s["o"], dyc, tn_dw_sq, f"dw_att_out_{l}")[0][0],
                         w_o=_dw_matmul(s["mg"], dx1b, tn_dw_sq, f"dw_o_{l}")[0][0], w_kv=dw_kv)
        mix = _GradReduce(l, mix_grads, core) if bottom else None
        (dproj,), got = _inproj_conv_bwd(dpre, s["proj"], conv["a"], conv["b"], 0, f"inproj_conv_bwd_{l}",
                                         _join([ffn and ffn.scatter_program(), mix and mix.swap_program()]))
        if bottom:
            ffn.scattered(got[:len(ffn.names)])
            mix.swapped(got[len(ffn.names):])
        (dw_in,), got = _dw_matmul(s["h"], dproj, tn_dw_in, f"dw_in_{l}", mix and mix.scatter_program())
        in_grads = dict(w_in=dw_in)
        inp = _GradReduce(l, in_grads, core) if bottom else None
        if bottom:
            mix.scattered(got)
            inp.swapped(_run_comm(inp.swap_program(), f"swap_halves_w_in_{l}"))
        (dx0, _, dg_mix), got = _nt_matmul_norm_bwd(
            dproj, s["w_in"], 2, s["x"], row(small["norm_mix_g"][l]), dx1, f"in_proj_bwd_{l}",
            inp and inp.scatter_program())
        if bottom:
            inp.scattered(got)
            pieces[l] = {**ffn.pieces, **mix.pieces, **inp.pieces}
        else:
            above = _GradReduce(l, {**grads, **mix_grads, **in_grads}, core)
        sgrads[l] = dict(norm_mix_g=dg_mix, norm_mem_g=dg_mem, b_gate=dbg, conv_b_bias=dcbias, ln_b_g=dlng,
                         ln_b_b=dlnb, norm_ffn_g=dg_ffn, conv_a_w=dconv_a, conv_b_w=dconv_b, conv_ffn_w=dconv_f)
        dx = dx0
    return loss, dx, sgrads, dg_final, pieces


BIG = ("w_in", "w_a_out", "w_b_out", "w_att_out", "w_o", "w_kv", "w_up", "w_down")
COL = ("w_in", "w_kv", "w_up")
SQUARES = ("w_a_out", "w_b_out", "w_att_out", "w_o")
ANY = pl.BlockSpec(memory_space=pl.ANY)


def _place():
    x, y, c = lax.axis_index("x"), lax.axis_index("y"), lax.axis_index("c")
    chips = [(1 - x, y), (x, 1 - y), (1 - x, 1 - y)]
    return x, y, c, 2 * x + y, chips


def _remote(src, dst, send_sem, recv_sem, dev):
    return pltpu.make_async_remote_copy(src_ref=src, dst_ref=dst, send_sem=send_sem, recv_sem=recv_sem,
                                        device_id=dev, device_id_type=MESH)


class _Striped:
    def __init__(self, src, dst, make):
        rows = src.shape[-2]
        unit = 8 * (4 // jnp.dtype(src.dtype).itemsize)
        n = max(k for k in range(1, MAX_STRIPES + 1) if rows % (unit * k) == 0) if rows % unit == 0 else 1
        q = rows // n
        self.parts = [make(_window(src, pl.ds(i * q, q), slice(None)), _window(dst, pl.ds(i * q, q), slice(None)))
                      for i in range(n)]
        self.whole = make(src, dst)

    def start(self):
        for p in self.parts:
            p.start()

    def wait(self):
        self.whole.wait()

    def wait_send(self):
        self.whole.wait_send()

    def wait_recv(self):
        self.whole.wait_recv()


def _far(src, dst, send_sem, recv_sem, dev):
    return _Striped(src, dst, lambda s, d: _remote(s, d, send_sem, recv_sem, dev))


def _near(src, dst, sem):
    return _Striped(src, dst, lambda s, d: pltpu.make_async_copy(s, d, sem))


def _window(ref, rows, cols):
    return ref.at[(slice(None),) * (len(ref.shape) - 2) + (rows, cols)]


def _row_tile(rows, cols, unit=16, limit=1 << 20):
    best = unit
    for tr in range(unit, rows + 1, unit):
        if rows % tr == 0 and tr * cols <= limit:
            best = tr
    return best


def _cast_place(ws, l, kind, chip, name):
    _, k, n = ws[0].shape
    if kind == "col":
        shape, spec = (N_CHIPS, k, n), pl.BlockSpec((None, k, n), lambda i, c: (c[0], 0, 0))
    elif len(ws) == 1:
        shape, spec = (N_CHIPS * k, n), pl.BlockSpec((k, n), lambda i, c: (c[0], 0))
    else:
        shape, spec = (len(ws), N_CHIPS * k, n), pl.BlockSpec((len(ws), k, n), lambda i, c: (0, c[0], 0))

    def body(c_ref, *refs):
        o_ref = refs[-1]
        if len(ws) == 1 or kind == "col":
            o_ref[...] = refs[0][...].astype(BF)
        else:
            for i in range(len(ws)):
                o_ref[i] = refs[i][...].astype(BF)

    return _pallas_call(
        body, name=name,
        grid_spec=pltpu.PrefetchScalarGridSpec(
            num_scalar_prefetch=1, grid=(1,),
            in_specs=[pl.BlockSpec((None, k, n), lambda i, c: (l, 0, 0))] * len(ws), out_specs=spec),
        out_shape=jax.ShapeDtypeStruct(shape, BF),
        compiler_params=_params(1),
    )(chip, *ws)


def _gather_program(arrays, kinds, part=(0, 1)):
    n_t = len(arrays)
    index, count = part

    def shard_rows(f, kind):
        return f.shape[-2] if kind == "col" else f.shape[-2] // N_CHIPS

    def run(phase, ins, full, sems):
        ici_send, ici_recv, sib_send, sib_recv, loc_sem = sems
        x, y, c, me, chips = _place()
        sibling = (x, y, 1 - c)

        def part_of(i, chip, half):
            f, kind = full[i], kinds[i]
            if kind == "small":
                cols = ins[i].shape[-1]
                return _window(f, slice(None), pl.ds(pl.multiple_of(chip * cols, 128), cols))
            rows = shard_rows(f, kind)
            r = rows // (2 * count)
            at = (half * count + index) * r
            if kind == "col":
                return f.at[chip, pl.ds(pl.multiple_of(at, 16), r), :]
            return _window(f, pl.ds(pl.multiple_of(chip * rows + at, 16), r), slice(None))

        src_part = lambda i, half: ins[i] if kinds[i] == "small" else part_of(i, me, half)
        dst_part = part_of
        local = [_near(ins[i], part_of(i, me, c), loc_sem.at[i]) for i in range(n_t) if kinds[i] == "small"]
        sends = []
        for i in range(n_t):
            for j, chip in enumerate(chips):
                sends.append(_far(src_part(i, c), dst_part(i, me, c), ici_send.at[3 * i + j],
                                  ici_recv.at[3 * i + j], (*chip, c)))
        if phase == "start":
            for cp in local + sends:
                cp.start()
            return
        passed = []
        for i in range(n_t):
            for j, chip in enumerate(chips):
                k = 2 * chip[0] + chip[1]
                landed = dst_part(i, k, c)
                if phase == "forward":
                    _remote(landed, landed, ici_send.at[3 * i + j], ici_recv.at[3 * i + j], (*chip, c)).wait_recv()
                if kinds[i] != "small":
                    passed.append(_far(landed, landed, sib_send.at[3 * i + j], sib_recv.at[3 * i + j], sibling))
                    if phase == "forward":
                        passed[-1].start()
        if phase == "forward":
            return
        for i in range(n_t):
            if kinds[i] == "small":
                continue
            for j, chip in enumerate(chips):
                k = 2 * chip[0] + chip[1]
                other = dst_part(i, k, 1 - c)
                _remote(other, other, sib_send.at[3 * i + j], sib_recv.at[3 * i + j], sibling).wait_recv()
        for cp in sends + passed:
            cp.wait_send()
        for cp in local:
            cp.wait()

    def out_shape(a, kind):
        shp = a.shape[:-1] + (a.shape[-1] * N_CHIPS,) if kind == "small" else a.shape
        return jax.ShapeDtypeStruct(shp, a.dtype)

    outs = [out_shape(a, k) for a, k in zip(arrays, kinds)]
    sems = [pltpu.SemaphoreType.DMA((3 * n_t,))] * 4 + [pltpu.SemaphoreType.DMA((n_t,))]
    return _Comm(arrays, outs, sems, run, {i: i for i in range(n_t) if kinds[i] != "small"})


def _all_reduce_small(part, name):
    r, n = part.shape

    def body(in_ref, out_ref, pair_ref, chips_ref, sib_sems, send_sems, recv_sems):
        x, y, c, me, chips = _place()
        pair_ref[c] = in_ref[...]
        swap = _remote(in_ref, pair_ref.at[c], sib_sems.at[0], sib_sems.at[1], (x, y, 1 - c))
        swap.start()
        _remote(in_ref, pair_ref.at[1 - c], sib_sems.at[0], sib_sems.at[1], (x, y, c)).wait_recv()
        chips_ref[me] = pair_ref[0] + pair_ref[1]
        sends = [_remote(chips_ref.at[me], chips_ref.at[me], send_sems.at[j], recv_sems.at[j], (*chip, c))
                 for j, chip in enumerate(chips)]
        for cp in sends:
            cp.start()
        for j, chip in enumerate(chips):
            landed = chips_ref.at[2 * chip[0] + chip[1]]
            _remote(landed, landed, send_sems.at[j], recv_sems.at[j], (x, y, c)).wait_recv()
        for cp in sends:
            cp.wait_send()
        swap.wait_send()
        total = chips_ref[0]
        for k in range(1, N_CHIPS):
            total = total + chips_ref[k]
        out_ref[...] = total

    vm = pl.BlockSpec(memory_space=pltpu.VMEM)
    return _pallas_call(
        body, name=name, in_specs=[vm], out_specs=vm, out_shape=jax.ShapeDtypeStruct((r, n), F32),
        scratch_shapes=[pltpu.VMEM((2, r, n), F32), pltpu.VMEM((N_CHIPS, r, n), F32),
                        pltpu.SemaphoreType.DMA((2,)), pltpu.SemaphoreType.DMA((N_CHIPS - 1,)),
                        pltpu.SemaphoreType.DMA((N_CHIPS - 1,))],
        compiler_params=pltpu.CompilerParams(vmem_limit_bytes=VMEM_LIMIT),
    )(part)


def _halves_view(name, dw):
    k, n = dw.shape
    s = 1 if name in COL else N_CHIPS
    return dw.reshape(s, 2, k // (2 * s), n)


def _swap_program(views):
    n_t = len(views)

    def run(phase, src, dst, sems):
        send_sems, recv_sems = sems
        x, y, c, _, _ = _place()
        copies = [_far(src[i].at[:, 1 - c], dst[i], send_sems.at[i], recv_sems.at[i], (x, y, 1 - c))
                  for i in range(n_t)]
        for cp in copies:
            if phase == "start":
                cp.start()
            elif phase == "finish":
                cp.wait()

    outs = [jax.ShapeDtypeStruct((v.shape[0],) + v.shape[2:], v.dtype) for v in views]
    sems = [pltpu.SemaphoreType.DMA((n_t,)), pltpu.SemaphoreType.DMA((n_t,))]
    return _Comm(views, outs, sems, run)


def _add_halves(view, got, place, col, name):
    s, _, r, n = view.shape
    if col:
        cw = n // N_CHIPS
        tr = _row_tile(r, cw)
        grid = (r // tr, N_CHIPS)
        in_specs = [pl.BlockSpec((None, None, tr, cw), lambda j, q, p: (0, p[0], j, q)),
                    pl.BlockSpec((None, tr, cw), lambda j, q, p: (0, j, q))]
        out_specs = [pl.BlockSpec((None, tr, cw), lambda j, q, p: (0, j, q)),
                     pl.BlockSpec((None, None, tr, cw), lambda j, q, p: (p[0], p[1], j, 0))]
    else:
        cw = n
        tr = _row_tile(r, n)
        grid = (s, r // tr)
        in_specs = [pl.BlockSpec((None, None, tr, n), lambda i, j, p: (i, p[0], j, 0)),
                    pl.BlockSpec((None, tr, n), lambda i, j, p: (i, j, 0))]
        out_specs = [pl.BlockSpec((None, tr, n), lambda i, j, p: (i, j, 0)),
                     pl.BlockSpec((None, None, tr, n), lambda i, j, p: (p[0], i, j, 0))]

    def body(p_ref, a_ref, b_ref, o_ref, z_ref):
        total = (a_ref[...].astype(F32) + b_ref[...].astype(F32)).astype(BF)
        o_ref[...] = total
        if col:
            @pl.when(pl.program_id(1) == p_ref[1])
            def _():
                z_ref[...] = total
        else:
            z_ref[...] = total

    return _pallas_call(
        body, name=name,
        grid_spec=pltpu.PrefetchScalarGridSpec(num_scalar_prefetch=1, grid=grid, in_specs=in_specs,
                                               out_specs=out_specs),
        out_shape=[jax.ShapeDtypeStruct((s, r, n), BF), jax.ShapeDtypeStruct((2, N_CHIPS, r, cw), BF)],
        compiler_params=_params(2),
    )(place, view, got)


def _scatter_program(sums, landing, names, part=(0, 1)):
    n_t = len(sums)
    index, count = part

    def rows(ref):
        q = ref.shape[-2] // count
        return ref if count == 1 else _window(ref, pl.ds(index * q, q), slice(None))

    def run(phase, src, dst, sems):
        ici_send, ici_recv, sib_send, sib_recv = sems
        x, y, c, me, chips = _place()
        sibling = (x, y, 1 - c)

        def piece(i, chip):
            if names[i] in COL:
                cw = src[i].shape[2] // N_CHIPS
                return src[i].at[0, :, pl.ds(pl.multiple_of(chip * cw, 128), cw)]
            return src[i].at[chip]

        local = []
        sends = []
        for i in range(n_t):
            sends.append(_far(rows(piece(i, me)), rows(dst[i].at[c, me]), sib_send.at[4 * i + 3],
                              sib_recv.at[4 * i + 3], sibling))
            for j, chip in enumerate(chips):
                k = 2 * chip[0] + chip[1]
                sends.append(_far(rows(piece(i, k)), rows(dst[i].at[c, me]), ici_send.at[3 * i + j],
                                  ici_recv.at[3 * i + j], (*chip, c)))
        if phase == "start":
            for cp in local + sends:
                cp.start()
            return
        passed = []
        for i in range(n_t):
            for j, chip in enumerate(chips):
                k = 2 * chip[0] + chip[1]
                landed = rows(dst[i].at[c, k])
                if phase == "forward":
                    _remote(landed, landed, ici_send.at[3 * i + j], ici_recv.at[3 * i + j], (*chip, c)).wait_recv()
                passed.append(_far(landed, landed, sib_send.at[4 * i + j], sib_recv.at[4 * i + j], sibling))
                if phase == "forward":
                    passed[-1].start()
        if phase == "forward":
            return
        for i in range(n_t):
            other = rows(dst[i].at[1 - c, me])
            _remote(other, other, sib_send.at[4 * i + 3], sib_recv.at[4 * i + 3], sibling).wait_recv()
            for j, chip in enumerate(chips):
                k = 2 * chip[0] + chip[1]
                other = rows(dst[i].at[1 - c, k])
                _remote(other, other, sib_send.at[4 * i + j], sib_recv.at[4 * i + j], sibling).wait_recv()
        for cp in sends + passed:
            cp.wait_send()
        for cp in local:
            cp.wait()

    outs = [jax.ShapeDtypeStruct(z.shape, z.dtype) for z in landing]
    sems = [pltpu.SemaphoreType.DMA((3 * n_t,))] * 2 + [pltpu.SemaphoreType.DMA((4 * n_t,))] * 2
    return _Comm(list(sums) + list(landing), outs, sems, run, {n_t + i: i for i in range(n_t)})


def _adamw(w, g, m, v):
    m = ADAM_B1 * m + (1.0 - ADAM_B1) * g
    v = ADAM_B2 * v + (1.0 - ADAM_B2) * (g * g)
    m_hat = m / (1.0 - ADAM_B1 ** ADAM_STEP)
    v_hat = v / (1.0 - ADAM_B2 ** ADAM_STEP)
    return -ADAM_LR * (m_hat / (jnp.sqrt(v_hat) + ADAM_EPS) + ADAM_WD * w), m, v


def _adam_shard(pieces, w, m, v, l, prev, name, comm=None):
    depth, rows, cw = w.shape
    hr = rows // 2
    tr = _row_tile(hr, cw, limit=1 << 18)
    n_i = hr // tr

    def body(*refs):
        z_ref, w_ref, m_ref, v_ref = refs[:4]
        g_ref, d_ref, nm_ref, nv_ref = refs[-4:]
        g = z_ref[0].astype(F32)
        for k in range(1, N_CHIPS):
            g = g + z_ref[k].astype(F32)
        g_ref[...] = g
        d_ref[...], nm_ref[...], nv_ref[...] = _adamw(w_ref[...], g, m_ref[...], v_ref[...])

    par = pl.BlockSpec((None, tr, cw), lambda h, i: (l, h * n_i + i, 0))
    out = jax.ShapeDtypeStruct((depth, rows, cw), F32)
    extra = [] if prev is None else list(prev)
    return _run(
        body, comm, name=name, grid=(2, n_i),
        in_specs=[pl.BlockSpec((None, N_CHIPS, tr, cw), lambda h, i: (h, 0, i, 0)), par, par, par] + [ANY] * len(extra),
        out_specs=[par] * 4, out_shape=[out] * 4, args=(pieces, w, m, v, *extra),
        aliases={4 + k: k for k in range(len(extra))})


def _adam_small(gs, ws, ms, vs, name):
    n = len(gs)

    def body(*refs):
        g, w, m, v, d, nm, nv = (refs[k * n:(k + 1) * n] for k in range(7))
        for i in range(n):
            d[i][...], nm[i][...], nv[i][...] = _adamw(w[i][...], g[i][...], m[i][...], v[i][...])

    vm = pl.BlockSpec(memory_space=pltpu.VMEM)
    outs = [jax.ShapeDtypeStruct(a.shape, F32) for a in ws] * 3
    res = _pallas_call(body, name=name, in_specs=[vm] * (4 * n), out_specs=[vm] * (3 * n), out_shape=outs)(
        *gs, *ws, *ms, *vs)
    return res[:n], res[n:2 * n], res[2 * n:]


WEIGHTS = ("norm_mix_g", "norm_mem_g", "w_in", "b_gate", "conv_a_w", "w_a_out", "conv_b_w", "conv_b_bias", "ln_b_g",
           "ln_b_b", "w_b_out", "w_kv", "w_att_out", "w_o", "norm_ffn_g", "w_up", "conv_ffn_w", "w_down",
           "norm_final_g")
REPLICATED = ("norm_mix_g", "norm_mem_g", "b_gate", "conv_b_bias", "ln_b_g", "ln_b_b", "norm_ffn_g")
CONVS = ("conv_a_w", "conv_b_w", "conv_ffn_w")
PACK_WIDTH = 1024


def _pack(arrays):
    flat = jnp.concatenate([a.reshape(-1) for a in arrays])
    size = -(-flat.shape[0] // (8 * PACK_WIDTH)) * (8 * PACK_WIDTH)
    return jnp.pad(flat, (0, size - flat.shape[0])).reshape(-1, PACK_WIDTH)


def _unpack(packed, shapes):
    flat = packed.reshape(-1)
    out, at = [], 0
    for shp in shapes:
        n = math.prod(shp)
        out.append(flat[at:at + n].reshape(shp))
        at += n
    return out


def kernel(x, mem, norm_mix_g, norm_mem_g, w_in, b_gate, conv_a_w, w_a_out, conv_b_w, conv_b_bias, ln_b_g, ln_b_b, w_b_out, w_kv, w_att_out, w_o, norm_ffn_g, w_up, conv_ffn_w, w_down, norm_final_g, loss_target, m_norm_mix_g, m_norm_mem_g, m_w_in, m_b_gate, m_conv_a_w, m_w_a_out, m_conv_b_w, m_conv_b_bias, m_ln_b_g, m_ln_b_b, m_w_b_out, m_w_kv, m_w_att_out, m_w_o, m_norm_ffn_g, m_w_up, m_conv_ffn_w, m_w_down, m_norm_final_g, v_norm_mix_g, v_norm_mem_g, v_w_in, v_b_gate, v_conv_a_w, v_w_a_out, v_conv_b_w, v_conv_b_bias, v_ln_b_g, v_ln_b_b, v_w_b_out, v_w_kv, v_w_att_out, v_w_o, v_norm_ffn_g, v_w_up, v_conv_ffn_w, v_w_down, v_norm_final_g):
    w = dict(norm_mix_g=norm_mix_g, norm_mem_g=norm_mem_g, w_in=w_in, b_gate=b_gate, conv_a_w=conv_a_w,
             w_a_out=w_a_out, conv_b_w=conv_b_w, conv_b_bias=conv_b_bias, ln_b_g=ln_b_g, ln_b_b=ln_b_b,
             w_b_out=w_b_out, w_kv=w_kv, w_att_out=w_att_out, w_o=w_o, norm_ffn_g=norm_ffn_g, w_up=w_up,
             conv_ffn_w=conv_ffn_w, w_down=w_down, norm_final_g=norm_final_g)
    mom = dict(norm_mix_g=m_norm_mix_g, norm_mem_g=m_norm_mem_g, w_in=m_w_in, b_gate=m_b_gate, conv_a_w=m_conv_a_w,
               w_a_out=m_w_a_out, conv_b_w=m_conv_b_w, conv_b_bias=m_conv_b_bias, ln_b_g=m_ln_b_g, ln_b_b=m_ln_b_b,
               w_b_out=m_w_b_out, w_kv=m_w_kv, w_att_out=m_w_att_out, w_o=m_w_o, norm_ffn_g=m_norm_ffn_g,
               w_up=m_w_up, conv_ffn_w=m_conv_ffn_w, w_down=m_w_down, norm_final_g=m_norm_final_g)
    var = dict(norm_mix_g=v_norm_mix_g, norm_mem_g=v_norm_mem_g, w_in=v_w_in, b_gate=v_b_gate, conv_a_w=v_conv_a_w,
               w_a_out=v_w_a_out, conv_b_w=v_conv_b_w, conv_b_bias=v_conv_b_bias, ln_b_g=v_ln_b_g, ln_b_b=v_ln_b_b,
               w_b_out=v_w_b_out, w_kv=v_w_kv, w_att_out=v_w_att_out, w_o=v_w_o, norm_ffn_g=v_norm_ffn_g,
               w_up=v_w_up, conv_ffn_w=v_conv_ffn_w, w_down=v_w_down, norm_final_g=v_norm_final_g)
    depth = w_in.shape[0]
    chip = 2 * lax.axis_index("x") + lax.axis_index("y")
    core = jnp.stack([lax.axis_index("c"), chip]).astype(jnp.int32)

    chip1 = chip.astype(jnp.int32).reshape(1)
    layers = range(depth)
    sh = dict(w_in=[_cast_place([w_in], l, "col", chip1, f"cast_w_in_{l}") for l in layers],
              wsq=[_cast_place([w[n] for n in SQUARES], l, "row", chip1, f"cast_squares_{l}") for l in layers],
              w_kv=[_cast_place([w_kv], l, "col", chip1, f"cast_w_kv_{l}") for l in layers],
              w_up=[_cast_place([w_up], l, "col", chip1, f"cast_w_up_{l}") for l in layers],
              w_down=[_cast_place([w_down], l, "row", chip1, f"cast_w_down_{l}") for l in layers])
    conv_sh = [jnp.concatenate([conv_a_w, conv_b_w], axis=1), conv_ffn_w]
    small = {n: w[n] for n in REPLICATED + ("norm_final_g",)}

    loss, dx, sgrads, dg_final, pieces = _step(x[0], mem[0], loss_target[0], sh, conv_sh, small, core)

    res = {n: None for n in BIG}
    for l in reversed(range(depth)):
        for n in BIG:
            res[n], _ = _adam_shard(pieces[l][n], w[n], mom[n], var[n], l, res[n], f"adam_{n}_{l}")

    per_layer = REPLICATED + CONVS
    parts = [sgrads[l][n] for l in range(depth) for n in per_layer] + [dg_final, loss[0:1, 0:1]]
    total = _all_reduce_small(_pack(parts), "all_reduce_small")
    summed = _unpack(total, [p.shape for p in parts])
    g_small = {}
    for k, n in enumerate(per_layer):
        full = jnp.stack([summed[l * len(per_layer) + k] for l in range(depth)])
        if n in CONVS:
            cols = w[n].shape[-1]
            full = lax.dynamic_slice_in_dim(full, chip * cols, cols, axis=2)
        g_small[n] = full.reshape(w[n].shape)
    g_small["norm_final_g"] = summed[-2].reshape(norm_final_g.shape)
    names = per_layer + ("norm_final_g",)
    two_d = lambda a: a.reshape(1, -1) if a.ndim == 1 else a
    deltas, new_ms, new_vs = _adam_small(*[[two_d(t[n]) for n in names] for t in (g_small, w, mom, var)],
                                         "adam_small")
    for n, dl, nm, nv in zip(names, deltas, new_ms, new_vs):
        res[n] = (g_small[n], dl.reshape(w[n].shape), nm.reshape(w[n].shape), nv.reshape(w[n].shape))

    loss = summed[-1].reshape(())
    return (loss, dx.reshape(x.shape), *[res[n][0] for n in WEIGHTS], *[res[n][1] for n in WEIGHTS],
            *[res[n][2] for n in WEIGHTS], *[res[n][3] for n in WEIGHTS])
```

```python
import functools
import math

import jax
import jax.numpy as jnp
from jax import lax
from jax.experimental import pallas as pl
from jax.experimental.pallas import tpu as pltpu

F32 = jnp.float32
BF = jnp.bfloat16
EPS = 1e-6
N_HEADS = 4
K_A, K_B, K_F = 3, 31, 3
ADAM_LR, ADAM_B1, ADAM_B2, ADAM_EPS, ADAM_WD, ADAM_STEP = 0.001, 0.9, 0.999, 1e-08, 0.01, 10
N_CHIPS = 4
N_DEV = 8
HALO = 32
MAX_STRIPES = 8
TM_ROW = 256
TM_MM = 1024
TT_DW = 4096
DW_LHS_ELEMS = 4 * 1024 * 1024
TR_EW = 128
VMEM_LIMIT = 56 * 1024 * 1024
MESH = pl.DeviceIdType.MESH
_pallas_call = pl.pallas_call


def _params(n_axes):
    return pltpu.CompilerParams(dimension_semantics=("arbitrary",) * n_axes, vmem_limit_bytes=VMEM_LIMIT)


def _resident(shape, index):
    return pl.BlockSpec(shape, lambda *_: index, pipeline_mode=pl.Buffered(1))


def _sig(x):
    return 1.0 / (1.0 + jnp.exp(-x))


def _nt(a, b):
    return lax.dot_general(a, b, (((1,), (1,)), ((), ())), preferred_element_type=F32)


def _tn(a, b):
    return lax.dot_general(a, b, (((0,), (0,)), ((), ())), preferred_element_type=F32)


def _nn(a, b):
    return jnp.dot(a, b, preferred_element_type=F32)


class _Comm:
    def __init__(self, inputs, out_shape, scratch, run, aliases=None):
        self.inputs, self.out_shape, self.scratch, self.run = list(inputs), list(out_shape), list(scratch), run
        self.aliases = dict(aliases or {})


def _join(programs):
    programs = [p for p in programs if p is not None]
    if not programs:
        return None

    def split(seq, counts):
        parts, at = [], 0
        for n in counts:
            parts.append(seq[at:at + n])
            at += n
        return parts

    n_in = [len(p.inputs) for p in programs]
    n_out = [len(p.out_shape) for p in programs]
    n_s = [len(p.scratch) for p in programs]

    def run(phase, ins, outs, sems):
        for p, i, o, s in zip(programs, split(ins, n_in), split(outs, n_out), split(sems, n_s)):
            p.run(phase, i, o, s)

    aliases, in_at, out_at = {}, 0, 0
    for p, i, o in zip(programs, n_in, n_out):
        aliases.update({in_at + a: out_at + b for a, b in p.aliases.items()})
        in_at, out_at = in_at + i, out_at + o
    return _Comm([a for p in programs for a in p.inputs], [a for p in programs for a in p.out_shape],
                 [a for p in programs for a in p.scratch], run, aliases)


def _run(body, comm, *, name, grid, in_specs, out_specs, out_shape, args, scratch_shapes=(), aliases=None):
    n_axes = len(grid)
    aliases = dict(aliases or {})
    if comm is None:
        outs = _pallas_call(body, name=name, grid=grid, in_specs=list(in_specs), out_specs=list(out_specs),
                            out_shape=list(out_shape), scratch_shapes=list(scratch_shapes),
                            input_output_aliases=aliases, compiler_params=_params(n_axes))(*args)
        return list(outs), []
    counts = (len(in_specs), len(comm.inputs), len(out_specs), len(comm.out_shape), len(scratch_shapes),
              len(comm.scratch))

    def hosted(*refs):
        parts, at = [], 0
        for n in counts:
            parts.append(refs[at:at + n])
            at += n
        ins, c_ins, outs, c_outs, scr, c_sems = parts
        step = pl.program_id(0)
        for a in range(1, n_axes):
            step = step * grid[a] + pl.program_id(a)
        total = math.prod(grid)
        late = max(0, total - 1 - max(1, total // 4))

        @pl.when(step == 0)
        def _():
            comm.run("start", c_ins, c_outs, c_sems)

        body(*ins, *outs, *scr)

        @pl.when(step == late)
        def _():
            comm.run("forward", c_ins, c_outs, c_sems)

        @pl.when(step == total - 1)
        def _():
            comm.run("finish", c_ins, c_outs, c_sems)

    any_spec = pl.BlockSpec(memory_space=pl.ANY)
    res = _pallas_call(
        hosted, name=name, grid=grid, in_specs=list(in_specs) + [any_spec] * counts[1],
        out_specs=list(out_specs) + [any_spec] * counts[3], out_shape=list(out_shape) + comm.out_shape,
        scratch_shapes=list(scratch_shapes) + comm.scratch, compiler_params=_params(n_axes),
        input_output_aliases={**aliases, **{counts[0] + a: counts[2] + b for a, b in comm.aliases.items()}},
    )(*args, *comm.inputs)
    return list(res[:counts[2]]), list(res[counts[2]:])


def _run_comm(comm, name):
    n_in, n_out = len(comm.inputs), len(comm.out_shape)

    def body(*refs):
        ins, outs, sems = refs[:n_in], refs[n_in:n_in + n_out], refs[n_in + n_out:]
        for phase in ("start", "forward", "finish"):
            comm.run(phase, ins, outs, sems)

    any_spec = pl.BlockSpec(memory_space=pl.ANY)
    return list(_pallas_call(body, name=name, in_specs=[any_spec] * n_in, out_specs=[any_spec] * n_out,
                             out_shape=comm.out_shape, scratch_shapes=comm.scratch,
                             input_output_aliases=comm.aliases)(*comm.inputs))


SUBLANES = 8
LANES = 128
ROW_CHUNK = 128
ALL_RESIDUES = tuple(range(1, SUBLANES))
SHORT_RESIDUES = tuple(sorted({(HALO - K_A + 1 + k) % SUBLANES for k in range(K_A)} - {0}))


class _Rows:
    def __init__(self, ref, shifted_ref=None, residues=()):
        self.ref, self.shifted_ref, self.residues = ref, shifted_ref, tuple(residues)

    def shift(self):
        n = self.shifted_ref.shape[1]
        for j, b in enumerate(self.residues):
            self.shifted_ref[j] = self.ref[pl.ds(b, n), :]

    def at(self, offset, r0, c0, rows):
        b = offset % SUBLANES
        if b in self.residues:
            return self.shifted_ref[self.residues.index(b), pl.ds(offset - b + r0, rows), pl.ds(c0, LANES)]
        return self.ref[pl.ds(offset + r0, rows), pl.ds(c0, LANES)]


def _shifted_scratch(residues, tm, c):
    return pltpu.VMEM((len(residues), tm + HALO - SUBLANES, c), F32)


def _causal_offsets(k_taps):
    return [HALO - k_taps + 1 + k for k in range(k_taps)]


def _anticausal_offsets(k_taps):
    return [k_taps - 1 - k for k in range(k_taps)]


def _tap_chunk(src, w_ref, offsets, r0, c0, rows):
    acc = w_ref[0:1, pl.ds(c0, LANES)] * src.at(offsets[0], r0, c0, rows)
    for k in range(1, len(offsets)):
        acc = acc + w_ref[k:k + 1, pl.ds(c0, LANES)] * src.at(offsets[k], r0, c0, rows)
    return acc


def _conv_whole(ref, w_ref, offsets, tm):
    acc = w_ref[0:1, :] * ref[pl.ds(offsets[0], tm), :]
    for k in range(1, len(offsets)):
        acc = acc + w_ref[k:k + 1, :] * ref[pl.ds(offsets[k], tm), :]
    return acc


def _tap_grads_whole(dw_ref, dy, ref, offsets, tm):
    for k, off in enumerate(offsets):
        dw_ref[k:k + 1, :] += jnp.sum(dy * ref[pl.ds(off, tm), :], axis=0, keepdims=True)


def _conv_taps(src, w_ref, offsets, tm, emit):
    rows = min(ROW_CHUNK, tm)
    for c0 in range(0, w_ref.shape[1], LANES):
        for r0 in range(0, tm, rows):
            emit(r0, rows, c0, _tap_chunk(src, w_ref, offsets, r0, c0, rows))


def _tap_grads(dw_ref, dy_at, src, offsets, tm):
    rows = min(ROW_CHUNK // 2, tm)
    for c0 in range(0, dw_ref.shape[1], LANES):
        acc = [None] * len(offsets)
        for r0 in range(0, tm, rows):
            dy = dy_at(r0, rows, c0)
            for k, off in enumerate(offsets):
                part = (dy * src.at(off, r0, c0, rows)).reshape(rows // SUBLANES, SUBLANES, LANES).sum(axis=0)
                acc[k] = part if acc[k] is None else acc[k] + part
        for k in range(len(offsets)):
            dw_ref[k:k + 1, pl.ds(c0, LANES)] += jnp.sum(acc[k], axis=0, keepdims=True)


def _prev_halo(tm, col):
    return lambda i: (jnp.maximum(i * (tm // HALO) - 1, 0), col)


def _next_halo(tm, n_rows, col):
    return lambda i: (jnp.minimum((i + 1) * (tm // HALO), n_rows // HALO - 1), col)


def _norm_matmul(x, g, w, gs, name, comm=None):
    t, d = x.shape
    n_s, _, ns = w.shape
    n = n_s * ns
    tm = min(TM_MM, t)

    def body(x_ref, g_ref, w_ref, h_ref, y_ref):
        @pl.when(pl.program_id(1) == 0)
        def _():
            xf = x_ref[...]
            r = lax.rsqrt(jnp.mean(xf * xf, axis=-1, keepdims=True) + EPS)
            h_ref[...] = ((xf * r) * g_ref[...]).astype(BF)

        for s in range(gs):
            y_ref[:, s * ns:(s + 1) * ns] = _nn(h_ref[...], w_ref[s]).astype(BF)

    return _run(
        body, comm, name=name, grid=(t // tm, n_s // gs),
        in_specs=[pl.BlockSpec((tm, d), lambda i, j: (i, 0)),
                  pl.BlockSpec((1, d), lambda i, j: (0, 0)),
                  pl.BlockSpec((gs, d, ns), lambda i, j: (j, 0, 0))],
        out_specs=[pl.BlockSpec((tm, d), lambda i, j: (i, 0)),
                   pl.BlockSpec((tm, gs * ns), lambda i, j: (i, j))],
        out_shape=[jax.ShapeDtypeStruct((t, d), BF), jax.ShapeDtypeStruct((t, n), BF)],
        args=(x, g, w))


def _mem_kv(mem, g, w_kv, name):
    m, d = mem.shape
    n_s, _, ns = w_kv.shape

    def body(mem_ref, g_ref, w_ref, memn_ref, kv_ref):
        xf = mem_ref[...]
        r = lax.rsqrt(jnp.mean(xf * xf, axis=-1, keepdims=True) + EPS)
        memn = ((xf * r) * g_ref[...]).astype(BF)
        memn_ref[...] = memn
        for s in range(n_s):
            kv_ref[:, s * ns:(s + 1) * ns] = _nn(memn, w_ref[s]).astype(BF)

    return _pallas_call(
        body, name=name, grid=(1,),
        in_specs=[pl.BlockSpec((m, d), lambda i: (0, 0)),
                  pl.BlockSpec((1, d), lambda i: (0, 0)),
                  pl.BlockSpec((n_s, d, ns), lambda i: (0, 0, 0))],
        out_specs=[pl.BlockSpec((m, d), lambda i: (0, 0)),
                   pl.BlockSpec((m, 2 * d), lambda i: (0, 0))],
        out_shape=[jax.ShapeDtypeStruct((m, d), BF), jax.ShapeDtypeStruct((m, 2 * d), BF)],
        compiler_params=_params(1),
    )(mem, g, w_kv)


def _load_branch_inputs(i, proj_ref, gch_ref, vh_ref, u0h_ref, ugh_ref, xa_ref, xb_ref, d):
    gc = proj_ref[:, d:2 * d].astype(F32)
    v = proj_ref[:, 2 * d:3 * d].astype(F32)
    u0 = proj_ref[:, 3 * d:4 * d].astype(F32)
    ug = proj_ref[:, 4 * d:5 * d].astype(F32)
    keep = (i > 0).astype(F32)
    xa_ref[pl.ds(0, HALO), :] = gch_ref[...].astype(F32) * vh_ref[...].astype(F32) * keep
    xa_ref[pl.ds(HALO, gc.shape[0]), :] = gc * v
    xb_ref[pl.ds(0, HALO), :] = u0h_ref[...].astype(F32) * _sig(ugh_ref[...].astype(F32)) * keep
    xb_ref[pl.ds(HALO, gc.shape[0]), :] = u0 * _sig(ug)


def _softmax_rows(s):
    e = jnp.exp(s - jnp.max(s, axis=-1, keepdims=True))
    return e / jnp.sum(e, axis=-1, keepdims=True)


def _mixer_fwd(proj, x, kv, conv_a, conv_b, cbias, ln_g, ln_b, b_gate, wsq, l, name, comm=None):
    t, d = x.shape
    m = kv.shape[0]
    tm = min(TM_ROW, t)
    hd = d // N_HEADS
    scale = 1.0 / math.sqrt(hd)

    def body(proj_ref, gch_ref, vh_ref, u0h_ref, ugh_ref, x_ref, kv_ref, ca_w, cb_w, cbias_ref, lng_ref, lnb_ref,
             bg_ref, wa_ref, wb_ref, wc_ref, wo_ref,
             x1_ref, za_ref, zb_ref, o_ref, ya_ref, yb_ref, yc_ref, mg_ref, cb_ref, xa_ref, xb_ref, sb_ref):
        i = pl.program_id(0)
        _load_branch_inputs(i, proj_ref, gch_ref, vh_ref, u0h_ref, ugh_ref, xa_ref, xb_ref, d)
        xb = _Rows(xb_ref, sb_ref, ALL_RESIDUES)
        xb.shift()
        def put_za(r0, rows, c0, ca):
            gb = proj_ref[pl.ds(r0, rows), pl.ds(c0, LANES)].astype(F32)
            za_ref[pl.ds(r0, rows), pl.ds(c0, LANES)] = (gb * ca).astype(BF)

        _conv_taps(_Rows(xa_ref), ca_w, _causal_offsets(K_A), tm, put_za)
        ya = _nn(za_ref[...], wa_ref[...])
        ya_ref[...] = ya.astype(BF)

        def put_cb(r0, rows, c0, conv):
            cb_ref[pl.ds(r0, rows), pl.ds(c0, LANES)] = conv + cbias_ref[:, pl.ds(c0, LANES)]

        _conv_taps(xb, cb_w, _causal_offsets(K_B), tm, put_cb)
        cb = cb_ref[...]
        mu = jnp.mean(cb, axis=-1, keepdims=True)
        dlt = cb - mu
        rstd = lax.rsqrt(jnp.mean(dlt * dlt, axis=-1, keepdims=True) + EPS)
        lnb = (dlt * rstd) * lng_ref[...] + lnb_ref[...]
        zb = (lnb * _sig(lnb)).astype(BF)
        zb_ref[...] = zb
        yb = _nn(zb, wb_ref[...])
        yb_ref[...] = yb.astype(BF)
        for h in range(N_HEADS):
            qh = proj_ref[:, 5 * d + h * hd:5 * d + (h + 1) * hd]
            kh = kv_ref[:, h * hd:(h + 1) * hd]
            vh = kv_ref[:, d + h * hd:d + (h + 1) * hd]
            p = _softmax_rows(_nt(qh, kh) * scale)
            o_ref[:, h * hd:(h + 1) * hd] = _nn(p.astype(BF), vh).astype(BF)
        yc = _nn(o_ref[...], wc_ref[...])
        yc_ref[...] = yc.astype(BF)
        g0 = _sig(proj_ref[:, 6 * d:7 * d].astype(F32) + bg_ref[:, 0:d])
        g1 = _sig(proj_ref[:, 7 * d:8 * d].astype(F32) + bg_ref[:, d:2 * d])
        g2 = _sig(proj_ref[:, 8 * d:9 * d].astype(F32) + bg_ref[:, 2 * d:3 * d])
        mg = (g0 * ya + g1 * yb + g2 * yc).astype(BF)
        mg_ref[...] = mg
        x1_ref[...] = x_ref[...] + _nn(mg, wo_ref[...])

    row = lambda w_: pl.BlockSpec((tm, w_), lambda i: (i, 0))
    halo = lambda col: pl.BlockSpec((HALO, d), _prev_halo(tm, col))
    sq = lambda which: _resident((None, None, d, d), (l, which, 0, 0))
    act = jax.ShapeDtypeStruct((t, d), BF)
    return _run(
        body, comm, name=name, grid=(t // tm,),
        in_specs=[row(9 * d), halo(1), halo(2), halo(3), halo(4), row(d),
                  _resident((m, 2 * d), (0, 0)),
                  _resident((None, K_A, d), (l, 0, 0)), _resident((None, K_B, d), (l, 0, 0)),
                  _resident((1, d), (0, 0)), _resident((1, d), (0, 0)), _resident((1, d), (0, 0)),
                  _resident((1, 3 * d), (0, 0)), sq(0), sq(1), sq(2), sq(3)],
        out_specs=[row(d)] * 9,
        out_shape=[jax.ShapeDtypeStruct((t, d), F32)] + [act] * 7 + [jax.ShapeDtypeStruct((t, d), F32)],
        scratch_shapes=[pltpu.VMEM((HALO + tm, d), F32), pltpu.VMEM((HALO + tm, d), F32),
                        _shifted_scratch(ALL_RESIDUES, tm, d)],
        args=(proj, proj, proj, proj, proj, x, kv, conv_a, conv_b, cbias, ln_g, ln_b, b_gate, wsq, wsq, wsq, wsq))


def _ffn_down_fwd(up, x1, conv_f, w_down, l, name, comm=None):
    t, d = x1.shape
    f2 = up.shape[1]
    f = f2 // 2
    tm = min(TM_ROW, t)

    def body(up_ref, uph_ref, x1_ref, cw_ref, wd_ref, x2_ref, zf_ref, uc_ref, xx_ref):
        i = pl.program_id(0)
        xx_ref[pl.ds(0, HALO), :] = uph_ref[...].astype(F32) * (i > 0).astype(F32)
        xx_ref[pl.ds(HALO, tm), :] = up_ref[...].astype(F32)
        uc = _conv_whole(xx_ref, cw_ref, _causal_offsets(K_F), tm)
        uc_ref[...] = uc.astype(BF)
        gt = uc[:, 0:f]
        zf = (gt * _sig(gt) * uc[:, f:f2]).astype(BF)
        zf_ref[...] = zf
        x2_ref[...] = x1_ref[...] + _nn(zf, wd_ref[...])

    return _run(
        body, comm, name=name, grid=(t // tm,),
        in_specs=[pl.BlockSpec((tm, f2), lambda i: (i, 0)),
                  pl.BlockSpec((HALO, f2), _prev_halo(tm, 0)),
                  pl.BlockSpec((tm, d), lambda i: (i, 0)),
                  _resident((None, K_F, f2), (l, 0, 0)),
                  _resident((None, f, d), (l, 0, 0))],
        out_specs=[pl.BlockSpec((tm, d), lambda i: (i, 0)), pl.BlockSpec((tm, f), lambda i: (i, 0)),
                   pl.BlockSpec((tm, f2), lambda i: (i, 0))],
        out_shape=[jax.ShapeDtypeStruct((t, d), F32), jax.ShapeDtypeStruct((t, f), BF),
                   jax.ShapeDtypeStruct((t, f2), BF)],
        scratch_shapes=[pltpu.VMEM((HALO + tm, f2), F32)],
        args=(up, up, x1, conv_f, w_down))


def _final_loss(x, g, target, name):
    t, d = x.shape
    tm = min(2 * TM_ROW, t)

    def body(x_ref, g_ref, t_ref, dx_ref, loss_ref, dg_ref):
        @pl.when(pl.program_id(0) == 0)
        def _():
            loss_ref[...] = jnp.zeros_like(loss_ref)
            dg_ref[...] = jnp.zeros_like(dg_ref)

        xf = x_ref[...]
        r = lax.rsqrt(jnp.mean(xf * xf, axis=-1, keepdims=True) + EPS)
        xhat = xf * r
        err = xhat * g_ref[...] - t_ref[...]
        loss_ref[...] += (0.5 / d) * jnp.sum(err * err)
        dy = err * (1.0 / d)
        dg_ref[...] += jnp.sum(dy * xhat, axis=0, keepdims=True)
        dxh = dy * g_ref[...]
        dx_ref[...] = r * (dxh - xhat * jnp.mean(dxh * xhat, axis=-1, keepdims=True))

    return _pallas_call(
        body, name=name, grid=(t // tm,),
        in_specs=[pl.BlockSpec((tm, d), lambda i: (i, 0)), pl.BlockSpec((1, d), lambda i: (0, 0)),
                  pl.BlockSpec((tm, d), lambda i: (i, 0))],
        out_specs=[pl.BlockSpec((tm, d), lambda i: (i, 0)), pl.BlockSpec((8, 128), lambda i: (0, 0)),
                   pl.BlockSpec((1, d), lambda i: (0, 0))],
        out_shape=[jax.ShapeDtypeStruct((t, d), F32), jax.ShapeDtypeStruct((8, 128), F32),
                   jax.ShapeDtypeStruct((1, d), F32)],
        compiler_params=_params(1),
    )(x, g, target)


def _ffn_down_bwd(dx2, up, uc, w_down, l, name, comm=None):
    t, d = dx2.shape
    f2 = up.shape[1]
    f = f2 // 2
    tm = min(TM_ROW, t)

    def body(dx2_ref, up_ref, uph_ref, uc_ref, wd_ref, duc_ref, dx2b_ref, dcw_ref, xx_ref):
        i = pl.program_id(0)

        @pl.when(i == 0)
        def _():
            dcw_ref[...] = jnp.zeros_like(dcw_ref)

        xx_ref[pl.ds(0, HALO), :] = uph_ref[...].astype(F32) * (i > 0).astype(F32)
        xx_ref[pl.ds(HALO, tm), :] = up_ref[...].astype(F32)
        gt = uc_ref[:, 0:f].astype(F32)
        sg = _sig(gt)
        dx2b = dx2_ref[...].astype(BF)
        dx2b_ref[...] = dx2b
        dzf = _nt(dx2b, wd_ref[...])
        duc_ref[:, 0:f] = (dzf * uc_ref[:, f:f2].astype(F32) * (sg * (1.0 + gt * (1.0 - sg)))).astype(BF)
        duc_ref[:, f:f2] = (dzf * (gt * sg)).astype(BF)
        _tap_grads_whole(dcw_ref, duc_ref[...].astype(F32), xx_ref, _causal_offsets(K_F), tm)

    return _run(
        body, comm, name=name, grid=(t // tm,),
        in_specs=[pl.BlockSpec((tm, d), lambda i: (i, 0)),
                  pl.BlockSpec((tm, f2), lambda i: (i, 0)),
                  pl.BlockSpec((HALO, f2), _prev_halo(tm, 0)),
                  pl.BlockSpec((tm, f2), lambda i: (i, 0)),
                  _resident((None, f, d), (l, 0, 0))],
        out_specs=[pl.BlockSpec((tm, f2), lambda i: (i, 0)), pl.BlockSpec((tm, d), lambda i: (i, 0)),
                   pl.BlockSpec((K_F, f2), lambda i: (0, 0))],
        out_shape=[jax.ShapeDtypeStruct((t, f2), BF), jax.ShapeDtypeStruct((t, d), BF),
                   jax.ShapeDtypeStruct((K_F, f2), F32)],
        scratch_shapes=[pltpu.VMEM((HALO + tm, f2), F32)],
        args=(dx2, up, up, uc, w_down))


def _ffn_conv_bwd(duc, conv_f, l, name, comm=None):
    t, f2 = duc.shape
    tm = min(TM_ROW, t)
    n_t = t // tm

    def body(duc_ref, nxt_ref, cw_ref, dup_ref, yy_ref):
        i = pl.program_id(0)
        yy_ref[pl.ds(0, tm), :] = duc_ref[...].astype(F32)
        yy_ref[pl.ds(tm, HALO), :] = nxt_ref[...].astype(F32) * (i < n_t - 1).astype(F32)

        def put(r0, rows, c0, conv):
            dup_ref[pl.ds(r0, rows), pl.ds(c0, LANES)] = conv.astype(BF)

        _conv_taps(_Rows(yy_ref), cw_ref, _anticausal_offsets(K_F), tm, put)

    return _run(
        body, comm, name=name, grid=(n_t,),
        in_specs=[pl.BlockSpec((tm, f2), lambda i: (i, 0)),
                  pl.BlockSpec((HALO, f2), _next_halo(tm, t, 0)),
                  _resident((None, K_F, f2), (l, 0, 0))],
        out_specs=[pl.BlockSpec((tm, f2), lambda i: (i, 0))],
        out_shape=[jax.ShapeDtypeStruct((t, f2), BF)],
        scratch_shapes=[pltpu.VMEM((tm + HALO, f2), F32)],
        args=(duc, duc, conv_f))


def _nt_matmul_norm_bwd(dy, w, gs, x, g, dres, name, comm=None):
    t, n = dy.shape
    d = x.shape[1]
    tm = min(TM_MM // 2, t)
    n_s, _, ns = w.shape
    n_k, tk = n_s // gs, gs * ns

    def body(dy_ref, w_ref, x_ref, g_ref, dres_ref, dx_ref, dxb_ref, dg_ref, acc_ref):
        i, k = pl.program_id(0), pl.program_id(1)

        @pl.when((i == 0) & (k == 0))
        def _():
            dg_ref[...] = jnp.zeros_like(dg_ref)

        @pl.when(k == 0)
        def _():
            acc_ref[...] = jnp.zeros_like(acc_ref)

        part = _nt(dy_ref[:, 0:ns], w_ref[0])
        for s in range(1, gs):
            part = part + _nt(dy_ref[:, s * ns:(s + 1) * ns], w_ref[s])
        acc_ref[...] += part

        @pl.when(k == n_k - 1)
        def _():
            xf = x_ref[...]
            r = lax.rsqrt(jnp.mean(xf * xf, axis=-1, keepdims=True) + EPS)
            xhat = xf * r
            dh = acc_ref[...]
            dg_ref[...] += jnp.sum(dh * xhat, axis=0, keepdims=True)
            dxh = dh * g_ref[...]
            dx = dres_ref[...] + r * (dxh - xhat * jnp.mean(dxh * xhat, axis=-1, keepdims=True))
            dx_ref[...] = dx
            dxb_ref[...] = dx.astype(BF)

    return _run(
        body, comm, name=name, grid=(t // tm, n_k),
        in_specs=[pl.BlockSpec((tm, tk), lambda i, k: (i, k)),
                  pl.BlockSpec((gs, d, ns), lambda i, k: (k, 0, 0)),
                  pl.BlockSpec((tm, d), lambda i, k: (i, 0)),
                  pl.BlockSpec((1, d), lambda i, k: (0, 0)),
                  pl.BlockSpec((tm, d), lambda i, k: (i, 0))],
        out_specs=[pl.BlockSpec((tm, d), lambda i, k: (i, 0)), pl.BlockSpec((tm, d), lambda i, k: (i, 0)),
                   pl.BlockSpec((1, d), lambda i, k: (0, 0))],
        out_shape=[jax.ShapeDtypeStruct((t, d), F32), jax.ShapeDtypeStruct((t, d), BF),
                   jax.ShapeDtypeStruct((1, d), F32)],
        scratch_shapes=[pltpu.VMEM((tm, d), F32)],
        args=(dy, w, x, g, dres))


def _mixer_bwd(dx1b, proj, ya, yb, yc, cb, kv, conv_a, conv_b, ln_g, ln_b, b_gate, wsq, l, name, comm=None):
    t, d = cb.shape
    m = kv.shape[0]
    tm = min(TM_ROW, t)
    hd = d // N_HEADS
    scale = 1.0 / math.sqrt(hd)

    def body(dx1b_ref, proj_ref, gch_ref, vh_ref, u0h_ref, ugh_ref, ya_ref, yb_ref, yc_ref, cb_ref, kv_ref,
             ca_w, cb_w, lng_ref, lnb_ref, bg_ref, wa_ref, wb_ref, wc_ref, wo_ref,
             dpre_ref, dya_ref, dyb_ref, dyc_ref, dkv_ref, dbg_ref, dlng_ref, dlnb_ref, dcbias_ref, dcaw_ref,
             dcbw_ref, xa_ref, xb_ref, sa_ref, sb_ref):
        i = pl.program_id(0)

        @pl.when(i == 0)
        def _():
            for ref in (dkv_ref, dbg_ref, dlng_ref, dlnb_ref, dcbias_ref, dcaw_ref, dcbw_ref):
                ref[...] = jnp.zeros_like(ref)

        _load_branch_inputs(i, proj_ref, gch_ref, vh_ref, u0h_ref, ugh_ref, xa_ref, xb_ref, d)
        xa, xb = _Rows(xa_ref, sa_ref, SHORT_RESIDUES), _Rows(xb_ref, sb_ref, ALL_RESIDUES)
        xa.shift()
        xb.shift()
        dmg = _nt(dx1b_ref[...], wo_ref[...])
        ys = (ya_ref, yb_ref, yc_ref)
        dys = (dya_ref, dyb_ref, dyc_ref)
        for b in range(3):
            gate = _sig(proj_ref[:, (6 + b) * d:(7 + b) * d].astype(F32) + bg_ref[:, b * d:(b + 1) * d])
            dys[b][...] = (gate * dmg).astype(BF)
            dpg = dmg * ys[b][...].astype(F32) * gate * (1.0 - gate)
            dpre_ref[:, (6 + b) * d:(7 + b) * d] = dpg.astype(BF)
            dbg_ref[:, b * d:(b + 1) * d] += jnp.sum(dpg, axis=0, keepdims=True)
        gb = proj_ref[:, 0:d].astype(F32)
        dza = _nt(dya_ref[...], wa_ref[...])

        def put_dgb(r0, rows, c0, ca):
            dpre_ref[pl.ds(r0, rows), pl.ds(c0, LANES)] = (dza[r0:r0 + rows, c0:c0 + LANES] * ca).astype(BF)

        _conv_taps(xa, ca_w, _causal_offsets(K_A), tm, put_dgb)
        dpre_ref[:, d:2 * d] = (dza * gb).astype(BF)
        _tap_grads(dcaw_ref, lambda r0, rows, c0: dpre_ref[pl.ds(r0, rows), pl.ds(d + c0, LANES)].astype(F32),
                   xa, _causal_offsets(K_A), tm)
        dpre_ref[:, 2 * d:3 * d] = jnp.zeros((tm, d), BF)
        cbv = cb_ref[...]
        mu = jnp.mean(cbv, axis=-1, keepdims=True)
        dlt = cbv - mu
        rstd = lax.rsqrt(jnp.mean(dlt * dlt, axis=-1, keepdims=True) + EPS)
        xhat = dlt * rstd
        lnb = xhat * lng_ref[...] + lnb_ref[...]
        sg = _sig(lnb)
        dzb = _nt(dyb_ref[...], wb_ref[...])
        dl = dzb * (sg * (1.0 + lnb * (1.0 - sg)))
        dlng_ref[...] += jnp.sum(dl * xhat, axis=0, keepdims=True)
        dlnb_ref[...] += jnp.sum(dl, axis=0, keepdims=True)
        dxh = dl * lng_ref[...]
        dcb = rstd * (dxh - jnp.mean(dxh, axis=-1, keepdims=True)
                      - xhat * jnp.mean(dxh * xhat, axis=-1, keepdims=True))
        dcbias_ref[...] += jnp.sum(dcb, axis=0, keepdims=True)
        dpre_ref[:, 3 * d:4 * d] = dcb.astype(BF)
        _tap_grads(dcbw_ref, lambda r0, rows, c0: dpre_ref[pl.ds(r0, rows), pl.ds(3 * d + c0, LANES)].astype(F32),
                   xb, _causal_offsets(K_B), tm)
        dpre_ref[:, 4 * d:5 * d] = jnp.zeros((tm, d), BF)
        do = _nt(dyc_ref[...], wc_ref[...]).astype(BF)
        for h in range(N_HEADS):
            qh = proj_ref[:, 5 * d + h * hd:5 * d + (h + 1) * hd]
            kh = kv_ref[:, h * hd:(h + 1) * hd]
            vh = kv_ref[:, d + h * hd:d + (h + 1) * hd]
            doh = do[:, h * hd:(h + 1) * hd]
            p = _softmax_rows(_nt(qh, kh) * scale)
            dp = _nt(doh, vh)
            ds = (p * (dp - jnp.sum(dp * p, axis=-1, keepdims=True)) * scale).astype(BF)
            dpre_ref[:, 5 * d + h * hd:5 * d + (h + 1) * hd] = _nn(ds, kh).astype(BF)
            dkv_ref[:, h * hd:(h + 1) * hd] += _tn(ds, qh)
            dkv_ref[:, d + h * hd:d + (h + 1) * hd] += _tn(p.astype(BF), doh)

    row = lambda w_: pl.BlockSpec((tm, w_), lambda i: (i, 0))
    halo = lambda col: pl.BlockSpec((HALO, d), _prev_halo(tm, col))
    sq = lambda which: _resident((None, None, d, d), (l, which, 0, 0))
    acc = lambda r, c: pl.BlockSpec((r, c), lambda i: (0, 0))
    act = jax.ShapeDtypeStruct((t, d), BF)
    vec = lambda r, c: jax.ShapeDtypeStruct((r, c), F32)
    return _run(
        body, comm, name=name, grid=(t // tm,),
        in_specs=[row(d), row(9 * d), halo(1), halo(2), halo(3), halo(4), row(d), row(d), row(d), row(d),
                  _resident((m, 2 * d), (0, 0)),
                  _resident((None, K_A, d), (l, 0, 0)), _resident((None, K_B, d), (l, 0, 0)),
                  _resident((1, d), (0, 0)), _resident((1, d), (0, 0)), _resident((1, 3 * d), (0, 0)),
                  sq(0), sq(1), sq(2), sq(3)],
        out_specs=[row(9 * d), row(d), row(d), row(d), acc(m, 2 * d), acc(1, 3 * d), acc(1, d), acc(1, d),
                   acc(1, d), acc(K_A, d), acc(K_B, d)],
        out_shape=[jax.ShapeDtypeStruct((t, 9 * d), BF), act, act, act, vec(m, 2 * d), vec(1, 3 * d), vec(1, d),
                   vec(1, d), vec(1, d), vec(K_A, d), vec(K_B, d)],
        scratch_shapes=[pltpu.VMEM((HALO + tm, d), F32), pltpu.VMEM((HALO + tm, d), F32),
                        _shifted_scratch(SHORT_RESIDUES, tm, d), _shifted_scratch(ALL_RESIDUES, tm, d)],
        args=(dx1b, proj, proj, proj, proj, proj, ya, yb, yc, cb, kv, conv_a, conv_b, ln_g, ln_b, b_gate,
              wsq, wsq, wsq, wsq))


def _inproj_conv_bwd(dpre, proj, conv_a, conv_b, l, name, comm=None):
    t, d9 = dpre.shape
    d = d9 // 9
    tm = min(TM_ROW, t)
    n_t = t // tm

    def body(dpre_ref, nxa_ref, nxb_ref, proj_ref, ca_w, cb_w, dproj_ref, ya_ref, yb_ref, sb_ref):
        i = pl.program_id(0)
        keep = (i < n_t - 1).astype(F32)
        ya_ref[pl.ds(0, tm), :] = dpre_ref[:, d:2 * d].astype(F32)
        ya_ref[pl.ds(tm, HALO), :] = nxa_ref[...].astype(F32) * keep
        yb_ref[pl.ds(0, tm), :] = dpre_ref[:, 3 * d:4 * d].astype(F32)
        yb_ref[pl.ds(tm, HALO), :] = nxb_ref[...].astype(F32) * keep
        yb = _Rows(yb_ref, sb_ref, ALL_RESIDUES)
        yb.shift()
        dproj_ref[:, 0:d] = dpre_ref[:, 0:d]
        dproj_ref[:, 5 * d:9 * d] = dpre_ref[:, 5 * d:9 * d]
        def chunk(ref, block, r0, rows, c0):
            return ref.at[pl.ds(r0, rows), pl.ds(block * d + c0, LANES)]

        def put_a(r0, rows, c0, dcv):
            chunk(dproj_ref, 1, r0, rows, c0)[...] = (dcv * chunk(proj_ref, 2, r0, rows, c0)[...].astype(F32)).astype(BF)
            chunk(dproj_ref, 2, r0, rows, c0)[...] = (dcv * chunk(proj_ref, 1, r0, rows, c0)[...].astype(F32)).astype(BF)

        def put_b(r0, rows, c0, dub):
            sg = _sig(chunk(proj_ref, 4, r0, rows, c0)[...].astype(F32))
            u0 = chunk(proj_ref, 3, r0, rows, c0)[...].astype(F32)
            chunk(dproj_ref, 3, r0, rows, c0)[...] = (dub * sg).astype(BF)
            chunk(dproj_ref, 4, r0, rows, c0)[...] = (dub * u0 * sg * (1.0 - sg)).astype(BF)

        _conv_taps(_Rows(ya_ref), ca_w, _anticausal_offsets(K_A), tm, put_a)
        _conv_taps(yb, cb_w, _anticausal_offsets(K_B), tm, put_b)

    return _run(
        body, comm, name=name, grid=(n_t,),
        in_specs=[pl.BlockSpec((tm, d9), lambda i: (i, 0)),
                  pl.BlockSpec((HALO, d), _next_halo(tm, t, 1)),
                  pl.BlockSpec((HALO, d), _next_halo(tm, t, 3)),
                  pl.BlockSpec((tm, d9), lambda i: (i, 0)),
                  _resident((None, K_A, d), (l, 0, 0)), _resident((None, K_B, d), (l, 0, 0))],
        out_specs=[pl.BlockSpec((tm, d9), lambda i: (i, 0))],
        out_shape=[jax.ShapeDtypeStruct((t, d9), BF)],
        scratch_shapes=[pltpu.VMEM((tm + HALO, d), F32), pltpu.VMEM((tm + HALO, d), F32),
                        _shifted_scratch(ALL_RESIDUES, tm, d)],
        args=(dpre, dpre, dpre, proj, conv_a, conv_b))


def _mem_kv_bwd(dkv, memn, mem, g, w_kv, name):
    m, d = mem.shape
    n_s, _, ns = w_kv.shape

    def body(dkv_ref, memn_ref, mem_ref, g_ref, w_ref, dw_ref, dg_ref):
        dkvb = dkv_ref[...].astype(BF)
        dw_ref[...] = _tn(memn_ref[...], dkvb).astype(BF)
        dmemn = _nt(dkvb[:, 0:ns], w_ref[0])
        for s in range(1, n_s):
            dmemn = dmemn + _nt(dkvb[:, s * ns:(s + 1) * ns], w_ref[s])
        xf = mem_ref[...]
        r = lax.rsqrt(jnp.mean(xf * xf, axis=-1, keepdims=True) + EPS)
        dg_ref[...] = jnp.sum(dmemn * (xf * r), axis=0, keepdims=True)

    return _pallas_call(
        body, name=name, grid=(1,),
        in_specs=[pl.BlockSpec((m, 2 * d), lambda i: (0, 0)), pl.BlockSpec((m, d), lambda i: (0, 0)),
                  pl.BlockSpec((m, d), lambda i: (0, 0)), pl.BlockSpec((1, d), lambda i: (0, 0)),
                  pl.BlockSpec((n_s, d, ns), lambda i: (0, 0, 0))],
        out_specs=[pl.BlockSpec((d, 2 * d), lambda i: (0, 0)), pl.BlockSpec((1, d), lambda i: (0, 0))],
        out_shape=[jax.ShapeDtypeStruct((d, 2 * d), BF), jax.ShapeDtypeStruct((1, d), F32)],
        compiler_params=_params(1),
    )(dkv, memn, mem, g, w_kv)


def _dw_matmul(a, b, tn, name, comm=None):
    t, k = a.shape
    n = b.shape[1]
    tt = min(TT_DW, t)
    while tt * k > DW_LHS_ELEMS and tt % 2 == 0:
        tt //= 2
    n_s = t // tt

    def body(a_ref, b_ref, o_ref, acc_ref):
        s = pl.program_id(1)
        part = _tn(a_ref[...], b_ref[...])
        if n_s == 1:
            o_ref[...] = part.astype(BF)
            return

        @pl.when(s == 0)
        def _():
            acc_ref[...] = part

        @pl.when(s > 0)
        def _():
            acc_ref[...] += part

        @pl.when(s == n_s - 1)
        def _():
            o_ref[...] = acc_ref[...].astype(BF)

    return _run(
        body, comm, name=name, grid=(n // tn, n_s),
        in_specs=[pl.BlockSpec((tt, k), lambda j, s: (s, 0)), pl.BlockSpec((tt, tn), lambda j, s: (s, j))],
        out_specs=[pl.BlockSpec((k, tn), lambda j, s: (0, j))],
        out_shape=[jax.ShapeDtypeStruct((k, n), BF)],
        scratch_shapes=[pltpu.VMEM((k, tn) if n_s > 1 else (8, 128), F32)],
        args=(a, b))


class _GradReduce:
    def __init__(self, l, grads, core):
        self.l, self.core, self.names = l, core, tuple(grads)
        self.views = {n: _halves_view(n, g) for n, g in grads.items()}
        self.got, self.sums, self.landing, self.pieces = {}, {}, {}, {}

    def swap_program(self):
        return _swap_program([self.views[n] for n in self.names])

    def swapped(self, outs):
        self.got = dict(zip(self.names, outs))
        for n in self.names:
            self.sums[n], self.landing[n] = _add_halves(self.views[n], self.got[n], self.core, n in COL,
                                                        f"add_halves_{n}_{self.l}")

    def scatter_program(self, names=None, part=(0, 1)):
        names = self.names if names is None else names
        return _scatter_program([self.sums[n] for n in names], [self.landing[n] for n in names], names, part)

    def scattered(self, outs, names=None):
        names = self.names if names is None else names
        self.landing.update(zip(names, outs))
        self.pieces.update(zip(names, outs))


def _step(x, mem, target, sh, conv_sh, small, core):
    depth = len(sh["w_in"])
    d = x.shape[1]
    f2 = sh["w_up"][0].shape[2] * N_CHIPS
    tn_dw_in = min(1024, d)
    tn_dw_sq = max(LANES, d // 4)
    tn_dw_up = f2 // 11 if f2 % (11 * 128) == 0 and f2 // 11 >= 128 else f2
    row = lambda v: v.reshape(1, -1)
    one = lambda a: a[None]

    w_in, cab, cf = _run_comm(_gather_program([sh["w_in"][0]] + conv_sh, ("col", "small", "small")), "gather_first")
    saved = []
    for l in range(depth):
        conv = dict(a=cab[l:l + 1, :K_A], b=cab[l:l + 1, K_A:], f=cf[l:l + 1])
        (h, proj), (wsq, w_kv) = _norm_matmul(
            x, row(small["norm_mix_g"][l]), w_in, 1, f"in_proj_{l}",
            _gather_program([sh["wsq"][l], sh["w_kv"][l]], ("row", "col")))
        memn, kv = _mem_kv(mem, row(small["norm_mem_g"][l]), w_kv, f"mem_kv_{l}")
        (x1, za, zb, o, ya, yb, yc, mg, cb), (w_up, w_down) = _mixer_fwd(
            proj, x, kv, conv["a"], conv["b"], row(small["conv_b_bias"][l]), row(small["ln_b_g"][l]),
            row(small["ln_b_b"][l]), row(small["b_gate"][l]), one(wsq), 0, f"mixer_fwd_{l}",
            _gather_program([sh["w_up"][l], sh["w_down"][l]], ("col", "row")))
        more = l + 1 < depth
        nxt = _gather_program([sh["w_in"][l + 1]], ("col",), (0, 2)) if more else None
        (h2, up), w_in_next = _norm_matmul(x1, row(small["norm_ffn_g"][l]), w_up, 2, f"up_proj_{l}", nxt)
        nxt = _gather_program(w_in_next, ("col",), (1, 2)) if more else None
        (x2, zf, uc), w_in_next = _ffn_down_fwd(up, x1, conv["f"], one(w_down), 0, f"ffn_down_fwd_{l}", nxt)
        saved.append(dict(x=x, x1=x1, memn=memn, kv=kv, h=h, proj=proj, za=za, zb=zb, o=o, ya=ya, yb=yb, yc=yc,
                          mg=mg, cb=cb, h2=h2, up=up, zf=zf, uc=uc, w_in=w_in, wsq=one(wsq), w_kv=w_kv,
                          w_up=w_up, w_down=one(w_down), conv=conv))
        x = x2
        w_in = w_in_next[0] if w_in_next else None
    dx, loss, dg_final = _final_loss(x, row(small["norm_final_g"]), target, "final_loss")
    sgrads, pieces = [None] * depth, [None] * depth
    above = None
    first, second, rest = SQUARES, ("w_up",), ("w_in", "w_kv", "w_down")
    for l in reversed(range(depth)):
        s = saved[l]
        conv = s["conv"]
        bottom = l == 0
        (duc, dx2b, dconv_f), got = _ffn_down_bwd(dx, s["up"], s["uc"], s["w_down"], 0, f"ffn_down_bwd_{l}",
                                                  above and above.swap_program())
        if above:
            above.swapped(got)
        (dup,), got = _ffn_conv_bwd(duc, conv["f"], 0, f"ffn_conv_bwd_{l}", above and above.scatter_program(first))
        if above:
            above.scattered(got, first)
        (dx1, dx1b, dg_ffn), got = _nt_matmul_norm_bwd(
            dup, s["w_up"], 2, s["x1"], row(small["norm_ffn_g"][l]), dx, f"up_proj_bwd_{l}",
            above and above.scatter_program(second))
        if above:
            above.scattered(got, second)
        grads = dict(w_up=_dw_matmul(s["h2"], dup, tn_dw_up, f"dw_up_{l}")[0][0],
                     w_down=_dw_matmul(s["zf"], dx2b, d, f"dw_down_{l}")[0][0])
        ffn = _GradReduce(l, grads, core) if bottom else None
        (dpre, dya, dyb, dyc, dkv, dbg, dlng, dlnb, dcbias, dconv_a, dconv_b), got = _mixer_bwd(
            dx1b, s["proj"], s["ya"], s["yb"], s["yc"], s["cb"], s["kv"], conv["a"], conv["b"],
            row(small["ln_b_g"][l]), row(small["ln_b_b"][l]), row(small["b_gate"][l]), s["wsq"], 0,
            f"mixer_bwd_{l}", _join([above and above.scatter_program(rest), ffn and ffn.swap_program()]))
        if above:
            above.scattered(got[:len(rest)], rest)
            pieces[above.l] = above.pieces
            got = got[len(rest):]
        if ffn:
            ffn.swapped(got)
        dw_kv, dg_mem = _mem_kv_bwd(dkv, s["memn"], mem, row(small["norm_mem_g"][l]), s["w_kv"], f"mem_kv_bwd_{l}")
        mix_grads = dict(w_a_out=_dw_matmul(s["za"], dya, tn_dw_sq, f"dw_a_out_{l}")[0][0],
                         w_b_out=_dw_matmul(s["zb"], dyb, tn_dw_sq, f"dw_b_out_{l}")[0][0],
                         w_att_out=_dw_matmul(s["o"], dyc, tn_dw_sq, f"dw_att_out_{l}")[0][0],
                         w_o=_dw_matmul(s["mg"], dx1b, tn_dw_sq, f"dw_o_{l}")[0][0], w_kv=dw_kv)
        mix = _GradReduce(l, mix_grads, core) if bottom else None
        (dproj,), got = _inproj_conv_bwd(dpre, s["proj"], conv["a"], conv["b"], 0, f"inproj_conv_bwd_{l}",
                                         _join([ffn and ffn.scatter_program(), mix and mix.swap_program()]))
        if bottom:
            ffn.scattered(got[:len(ffn.names)])
            mix.swapped(got[len(ffn.names):])
        (dw_in,), got = _dw_matmul(s["h"], dproj, tn_dw_in, f"dw_in_{l}", mix and mix.scatter_program())
        in_grads = dict(w_in=dw_in)
        inp = _GradReduce(l, in_grads, core) if bottom else None
        if bottom:
            mix.scattered(got)
            inp.swapped(_run_comm(inp.swap_program(), f"swap_halves_w_in_{l}"))
        (dx0, _, dg_mix), got = _nt_matmul_norm_bwd(
            dproj, s["w_in"], 2, s["x"], row(small["norm_mix_g"][l]), dx1, f"in_proj_bwd_{l}",
            inp and inp.scatter_program())
        if bottom:
            inp.scattered(got)
            pieces[l] = {**ffn.pieces, **mix.pieces, **inp.pieces}
        else:
            above = _GradReduce(l, {**grads, **mix_grads, **in_grads}, core)
        sgrads[l] = dict(norm_mix_g=dg_mix, norm_mem_g=dg_mem, b_gate=dbg, conv_b_bias=dcbias, ln_b_g=dlng,
                         ln_b_b=dlnb, norm_ffn_g=dg_ffn, conv_a_w=dconv_a, conv_b_w=dconv_b, conv_ffn_w=dconv_f)
        dx = dx0
    return loss, dx, sgrads, dg_final, pieces


BIG = ("w_in", "w_a_out", "w_b_out", "w_att_out", "w_o", "w_kv", "w_up", "w_down")
COL = ("w_in", "w_kv", "w_up")
SQUARES = ("w_a_out", "w_b_out", "w_att_out", "w_o")
ANY = pl.BlockSpec(memory_space=pl.ANY)


def _place():
    x, y, c = lax.axis_index("x"), lax.axis_index("y"), lax.axis_index("c")
    chips = [(1 - x, y), (x, 1 - y), (1 - x, 1 - y)]
    return x, y, c, 2 * x + y, chips


def _remote(src, dst, send_sem, recv_sem, dev):
    return pltpu.make_async_remote_copy(src_ref=src, dst_ref=dst, send_sem=send_sem, recv_sem=recv_sem,
                                        device_id=dev, device_id_type=MESH)


class _Striped:
    def __init__(self, src, dst, make):
        rows = src.shape[-2]
        unit = 8 * (4 // jnp.dtype(src.dtype).itemsize)
        n = max(k for k in range(1, MAX_STRIPES + 1) if rows % (unit * k) == 0) if rows % unit == 0 else 1
        q = rows // n
        self.parts = [make(_window(src, pl.ds(i * q, q), slice(None)), _window(dst, pl.ds(i * q, q), slice(None)))
                      for i in range(n)]
        self.whole = make(src, dst)

    def start(self):
        for p in self.parts:
            p.start()

    def wait(self):
        self.whole.wait()

    def wait_send(self):
        self.whole.wait_send()

    def wait_recv(self):
        self.whole.wait_recv()


def _far(src, dst, send_sem, recv_sem, dev):
    return _Striped(src, dst, lambda s, d: _remote(s, d, send_sem, recv_sem, dev))


def _near(src, dst, sem):
    return _Striped(src, dst, lambda s, d: pltpu.make_async_copy(s, d, sem))


def _window(ref, rows, cols):
    return ref.at[(slice(None),) * (len(ref.shape) - 2) + (rows, cols)]


def _row_tile(rows, cols, unit=16, limit=1 << 20):
    best = unit
    for tr in range(unit, rows + 1, unit):
        if rows % tr == 0 and tr * cols <= limit:
            best = tr
    return best


def _cast_place(ws, l, kind, chip, name):
    _, k, n = ws[0].shape
    if kind == "col":
        shape, spec = (N_CHIPS, k, n), pl.BlockSpec((None, k, n), lambda i, c: (c[0], 0, 0))
    elif len(ws) == 1:
        shape, spec = (N_CHIPS * k, n), pl.BlockSpec((k, n), lambda i, c: (c[0], 0))
    else:
        shape, spec = (len(ws), N_CHIPS * k, n), pl.BlockSpec((len(ws), k, n), lambda i, c: (0, c[0], 0))

    def body(c_ref, *refs):
        o_ref = refs[-1]
        if len(ws) == 1 or kind == "col":
            o_ref[...] = refs[0][...].astype(BF)
        else:
            for i in range(len(ws)):
                o_ref[i] = refs[i][...].astype(BF)

    return _pallas_call(
        body, name=name,
        grid_spec=pltpu.PrefetchScalarGridSpec(
            num_scalar_prefetch=1, grid=(1,),
            in_specs=[pl.BlockSpec((None, k, n), lambda i, c: (l, 0, 0))] * len(ws), out_specs=spec),
        out_shape=jax.ShapeDtypeStruct(shape, BF),
        compiler_params=_params(1),
    )(chip, *ws)


def _gather_program(arrays, kinds, part=(0, 1)):
    n_t = len(arrays)
    index, count = part

    def shard_rows(f, kind):
        return f.shape[-2] if kind == "col" else f.shape[-2] // N_CHIPS

    def run(phase, ins, full, sems):
        ici_send, ici_recv, sib_send, sib_recv, loc_sem = sems
        x, y, c, me, chips = _place()
        sibling = (x, y, 1 - c)

        def part_of(i, chip, half):
            f, kind = full[i], kinds[i]
            if kind == "small":
                cols = ins[i].shape[-1]
                return _window(f, slice(None), pl.ds(pl.multiple_of(chip * cols, 128), cols))
            rows = shard_rows(f, kind)
            r = rows // (2 * count)
            at = (half * count + index) * r
            if kind == "col":
                return f.at[chip, pl.ds(pl.multiple_of(at, 16), r), :]
            return _window(f, pl.ds(pl.multiple_of(chip * rows + at, 16), r), slice(None))

        src_part = lambda i, half: ins[i] if kinds[i] == "small" else part_of(i, me, half)
        dst_part = part_of
        local = [_near(ins[i], part_of(i, me, c), loc_sem.at[i]) for i in range(n_t) if kinds[i] == "small"]
        sends = []
        for i in range(n_t):
            for j, chip in enumerate(chips):
                sends.append(_far(src_part(i, c), dst_part(i, me, c), ici_send.at[3 * i + j],
                                  ici_recv.at[3 * i + j], (*chip, c)))
        if phase == "start":
            for cp in local + sends:
                cp.start()
            return
        passed = []
        for i in range(n_t):
            for j, chip in enumerate(chips):
                k = 2 * chip[0] + chip[1]
                landed = dst_part(i, k, c)
                if phase == "forward":
                    _remote(landed, landed, ici_send.at[3 * i + j], ici_recv.at[3 * i + j], (*chip, c)).wait_recv()
                if kinds[i] != "small":
                    passed.append(_far(landed, landed, sib_send.at[3 * i + j], sib_recv.at[3 * i + j], sibling))
                    if phase == "forward":
                        passed[-1].start()
        if phase == "forward":
            return
        for i in range(n_t):
            if kinds[i] == "small":
                continue
            for j, chip in enumerate(chips):
                k = 2 * chip[0] + chip[1]
                other = dst_part(i, k, 1 - c)
                _remote(other, other, sib_send.at[3 * i + j], sib_recv.at[3 * i + j], sibling).wait_recv()
        for cp in sends + passed:
            cp.wait_send()
        for cp in local:
            cp.wait()

    def out_shape(a, kind):
        shp = a.shape[:-1] + (a.shape[-1] * N_CHIPS,) if kind == "small" else a.shape
        return jax.ShapeDtypeStruct(shp, a.dtype)

    outs = [out_shape(a, k) for a, k in zip(arrays, kinds)]
    sems = [pltpu.SemaphoreType.DMA((3 * n_t,))] * 4 + [pltpu.SemaphoreType.DMA((n_t,))]
    return _Comm(arrays, outs, sems, run, {i: i for i in range(n_t) if kinds[i] != "small"})


def _all_reduce_small(part, name):
    r, n = part.shape

    def body(in_ref, out_ref, pair_ref, chips_ref, sib_sems, send_sems, recv_sems):
        x, y, c, me, chips = _place()
        pair_ref[c] = in_ref[...]
        swap = _remote(in_ref, pair_ref.at[c], sib_sems.at[0], sib_sems.at[1], (x, y, 1 - c))
        swap.start()
        _remote(in_ref, pair_ref.at[1 - c], sib_sems.at[0], sib_sems.at[1], (x, y, c)).wait_recv()
        chips_ref[me] = pair_ref[0] + pair_ref[1]
        sends = [_remote(chips_ref.at[me], chips_ref.at[me], send_sems.at[j], recv_sems.at[j], (*chip, c))
                 for j, chip in enumerate(chips)]
        for cp in sends:
            cp.start()
        for j, chip in enumerate(chips):
            landed = chips_ref.at[2 * chip[0] + chip[1]]
            _remote(landed, landed, send_sems.at[j], recv_sems.at[j], (x, y, c)).wait_recv()
        for cp in sends:
            cp.wait_send()
        swap.wait_send()
        total = chips_ref[0]
        for k in range(1, N_CHIPS):
            total = total + chips_ref[k]
        out_ref[...] = total

    vm = pl.BlockSpec(memory_space=pltpu.VMEM)
    return _pallas_call(
        body, name=name, in_specs=[vm], out_specs=vm, out_shape=jax.ShapeDtypeStruct((r, n), F32),
        scratch_shapes=[pltpu.VMEM((2, r, n), F32), pltpu.VMEM((N_CHIPS, r, n), F32),
                        pltpu.SemaphoreType.DMA((2,)), pltpu.SemaphoreType.DMA((N_CHIPS - 1,)),
                        pltpu.SemaphoreType.DMA((N_CHIPS - 1,))],
        compiler_params=pltpu.CompilerParams(vmem_limit_bytes=VMEM_LIMIT),
    )(part)


def _halves_view(name, dw):
    k, n = dw.shape
    s = 1 if name in COL else N_CHIPS
    return dw.reshape(s, 2, k // (2 * s), n)


def _swap_program(views):
    n_t = len(views)

    def run(phase, src, dst, sems):
        send_sems, recv_sems = sems
        x, y, c, _, _ = _place()
        copies = [_far(src[i].at[:, 1 - c], dst[i], send_sems.at[i], recv_sems.at[i], (x, y, 1 - c))
                  for i in range(n_t)]
        for cp in copies:
            if phase == "start":
                cp.start()
            elif phase == "finish":
                cp.wait()

    outs = [jax.ShapeDtypeStruct((v.shape[0],) + v.shape[2:], v.dtype) for v in views]
    sems = [pltpu.SemaphoreType.DMA((n_t,)), pltpu.SemaphoreType.DMA((n_t,))]
    return _Comm(views, outs, sems, run)


def _add_halves(view, got, place, col, name):
    s, _, r, n = view.shape
    if col:
        cw = n // N_CHIPS
        tr = _row_tile(r, cw)
        grid = (r // tr, N_CHIPS)
        in_specs = [pl.BlockSpec((None, None, tr, cw), lambda j, q, p: (0, p[0], j, q)),
                    pl.BlockSpec((None, tr, cw), lambda j, q, p: (0, j, q))]
        out_specs = [pl.BlockSpec((None, tr, cw), lambda j, q, p: (0, j, q)),
                     pl.BlockSpec((None, None, tr, cw), lambda j, q, p: (p[0], p[1], j, 0))]
    else:
        cw = n
        tr = _row_tile(r, n)
        grid = (s, r // tr)
        in_specs = [pl.BlockSpec((None, None, tr, n), lambda i, j, p: (i, p[0], j, 0)),
                    pl.BlockSpec((None, tr, n), lambda i, j, p: (i, j, 0))]
        out_specs = [pl.BlockSpec((None, tr, n), lambda i, j, p: (i, j, 0)),
                     pl.BlockSpec((None, None, tr, n), lambda i, j, p: (p[0], i, j, 0))]

    def body(p_ref, a_ref, b_ref, o_ref, z_ref):
        total = (a_ref[...].astype(F32) + b_ref[...].astype(F32)).astype(BF)
        o_ref[...] = total
        if col:
            @pl.when(pl.program_id(1) == p_ref[1])
            def _():
                z_ref[...] = total
        else:
            z_ref[...] = total

    return _pallas_call(
        body, name=name,
        grid_spec=pltpu.PrefetchScalarGridSpec(num_scalar_prefetch=1, grid=grid, in_specs=in_specs,
                                               out_specs=out_specs),
        out_shape=[jax.ShapeDtypeStruct((s, r, n), BF), jax.ShapeDtypeStruct((2, N_CHIPS, r, cw), BF)],
        compiler_params=_params(2),
    )(place, view, got)


def _scatter_program(sums, landing, names, part=(0, 1)):
    n_t = len(sums)
    index, count = part

    def rows(ref):
        q = ref.shape[-2] // count
        return ref if count == 1 else _window(ref, pl.ds(index * q, q), slice(None))

    def run(phase, src, dst, sems):
        ici_send, ici_recv, sib_send, sib_recv = sems
        x, y, c, me, chips = _place()
        sibling = (x, y, 1 - c)

        def piece(i, chip):
            if names[i] in COL:
                cw = src[i].shape[2] // N_CHIPS
                return src[i].at[0, :, pl.ds(pl.multiple_of(chip * cw, 128), cw)]
            return src[i].at[chip]

        local = []
        sends = []
        for i in range(n_t):
            sends.append(_far(rows(piece(i, me)), rows(dst[i].at[c, me]), sib_send.at[4 * i + 3],
                              sib_recv.at[4 * i + 3], sibling))
            for j, chip in enumerate(chips):
                k = 2 * chip[0] + chip[1]
                sends.append(_far(rows(piece(i, k)), rows(dst[i].at[c, me]), ici_send.at[3 * i + j],
                                  ici_recv.at[3 * i + j], (*chip, c)))
        if phase == "start":
            for cp in local + sends:
                cp.start()
            return
        passed = []
        for i in range(n_t):
            for j, chip in enumerate(chips):
                k = 2 * chip[0] + chip[1]
                landed = rows(dst[i].at[c, k])
                if phase == "forward":
                    _remote(landed, landed, ici_send.at[3 * i + j], ici_recv.at[3 * i + j], (*chip, c)).wait_recv()
                passed.append(_far(landed, landed, sib_send.at[4 * i + j], sib_recv.at[4 * i + j], sibling))
                if phase == "forward":
                    passed[-1].start()
        if phase == "forward":
            return
        for i in range(n_t):
            other = rows(dst[i].at[1 - c, me])
            _remote(other, other, sib_send.at[4 * i + 3], sib_recv.at[4 * i + 3], sibling).wait_recv()
            for j, chip in enumerate(chips):
                k = 2 * chip[0] + chip[1]
                other = rows(dst[i].at[1 - c, k])
                _remote(other, other, sib_send.at[4 * i + j], sib_recv.at[4 * i + j], sibling).wait_recv()
        for cp in sends + passed:
            cp.wait_send()
        for cp in local:
            cp.wait()

    outs = [jax.ShapeDtypeStruct(z.shape, z.dtype) for z in landing]
    sems = [pltpu.SemaphoreType.DMA((3 * n_t,))] * 2 + [pltpu.SemaphoreType.DMA((4 * n_t,))] * 2
    return _Comm(list(sums) + list(landing), outs, sems, run, {n_t + i: i for i in range(n_t)})


def _adamw(w, g, m, v):
    m = ADAM_B1 * m + (1.0 - ADAM_B1) * g
    v = ADAM_B2 * v + (1.0 - ADAM_B2) * (g * g)
    m_hat = m / (1.0 - ADAM_B1 ** ADAM_STEP)
    v_hat = v / (1.0 - ADAM_B2 ** ADAM_STEP)
    return -ADAM_LR * (m_hat / (jnp.sqrt(v_hat) + ADAM_EPS) + ADAM_WD * w), m, v


def _adam_shard(pieces, w, m, v, l, prev, name, comm=None):
    depth, rows, cw = w.shape
    hr = rows // 2
    tr = _row_tile(hr, cw, limit=1 << 19)
    n_i = hr // tr

    def body(*refs):
        z_ref, w_ref, m_ref, v_ref = refs[:4]
        g_ref, d_ref, nm_ref, nv_ref = refs[-4:]
        g = z_ref[0].astype(F32)
        for k in range(1, N_CHIPS):
            g = g + z_ref[k].astype(F32)
        g_ref[...] = g
        d_ref[...], nm_ref[...], nv_ref[...] = _adamw(w_ref[...], g, m_ref[...], v_ref[...])

    par = pl.BlockSpec((None, tr, cw), lambda h, i: (l, h * n_i + i, 0))
    out = jax.ShapeDtypeStruct((depth, rows, cw), F32)
    extra = [] if prev is None else list(prev)
    return _run(
        body, comm, name=name, grid=(2, n_i),
        in_specs=[pl.BlockSpec((None, N_CHIPS, tr, cw), lambda h, i: (h, 0, i, 0)), par, par, par] + [ANY] * len(extra),
        out_specs=[par] * 4, out_shape=[out] * 4, args=(pieces, w, m, v, *extra),
        aliases={4 + k: k for k in range(len(extra))})


def _adam_small(gs, ws, ms, vs, name):
    n = len(gs)

    def body(*refs):
        g, w, m, v, d, nm, nv = (refs[k * n:(k + 1) * n] for k in range(7))
        for i in range(n):
            d[i][...], nm[i][...], nv[i][...] = _adamw(w[i][...], g[i][...], m[i][...], v[i][...])

    vm = pl.BlockSpec(memory_space=pltpu.VMEM)
    outs = [jax.ShapeDtypeStruct(a.shape, F32) for a in ws] * 3
    res = _pallas_call(body, name=name, in_specs=[vm] * (4 * n), out_specs=[vm] * (3 * n), out_shape=outs)(
        *gs, *ws, *ms, *vs)
    return res[:n], res[n:2 * n], res[2 * n:]


WEIGHTS = ("norm_mix_g", "norm_mem_g", "w_in", "b_gate", "conv_a_w", "w_a_out", "conv_b_w", "conv_b_bias", "ln_b_g",
           "ln_b_b", "w_b_out", "w_kv", "w_att_out", "w_o", "norm_ffn_g", "w_up", "conv_ffn_w", "w_down",
           "norm_final_g")
REPLICATED = ("norm_mix_g", "norm_mem_g", "b_gate", "conv_b_bias", "ln_b_g", "ln_b_b", "norm_ffn_g")
CONVS = ("conv_a_w", "conv_b_w", "conv_ffn_w")
PACK_WIDTH = 1024


def _pack(arrays):
    flat = jnp.concatenate([a.reshape(-1) for a in arrays])
    size = -(-flat.shape[0] // (8 * PACK_WIDTH)) * (8 * PACK_WIDTH)
    return jnp.pad(flat, (0, size - flat.shape[0])).reshape(-1, PACK_WIDTH)


def _unpack(packed, shapes):
    flat = packed.reshape(-1)
    out, at = [], 0
    for shp in shapes:
        n = math.prod(shp)
        out.append(flat[at:at + n].reshape(shp))
        at += n
    return out


def kernel(x, mem, norm_mix_g, norm_mem_g, w_in, b_gate, conv_a_w, w_a_out, conv_b_w, conv_b_bias, ln_b_g, ln_b_b, w_b_out, w_kv, w_att_out, w_o, norm_ffn_g, w_up, conv_ffn_w, w_down, norm_final_g, loss_target, m_norm_mix_g, m_norm_mem_g, m_w_in, m_b_gate, m_conv_a_w, m_w_a_out, m_conv_b_w, m_conv_b_bias, m_ln_b_g, m_ln_b_b, m_w_b_out, m_w_kv, m_w_att_out, m_w_o, m_norm_ffn_g, m_w_up, m_conv_ffn_w, m_w_down, m_norm_final_g, v_norm_mix_g, v_norm_mem_g, v_w_in, v_b_gate, v_conv_a_w, v_w_a_out, v_conv_b_w, v_conv_b_bias, v_ln_b_g, v_ln_b_b, v_w_b_out, v_w_kv, v_w_att_out, v_w_o, v_norm_ffn_g, v_w_up, v_conv_ffn_w, v_w_down, v_norm_final_g):
    w = dict(norm_mix_g=norm_mix_g, norm_mem_g=norm_mem_g, w_in=w_in, b_gate=b_gate, conv_a_w=conv_a_w,
             w_a_out=w_a_out, conv_b_w=conv_b_w, conv_b_bias=conv_b_bias, ln_b_g=ln_b_g, ln_b_b=ln_b_b,
             w_b_out=w_b_out, w_kv=w_kv, w_att_out=w_att_out, w_o=w_o, norm_ffn_g=norm_ffn_g, w_up=w_up,
             conv_ffn_w=conv_ffn_w, w_down=w_down, norm_final_g=norm_final_g)
    mom = dict(norm_mix_g=m_norm_mix_g, norm_mem_g=m_norm_mem_g, w_in=m_w_in, b_gate=m_b_gate, conv_a_w=m_conv_a_w,
               w_a_out=m_w_a_out, conv_b_w=m_conv_b_w, conv_b_bias=m_conv_b_bias, ln_b_g=m_ln_b_g, ln_b_b=m_ln_b_b,
               w_b_out=m_w_b_out, w_kv=m_w_kv, w_att_out=m_w_att_out, w_o=m_w_o, norm_ffn_g=m_norm_ffn_g,
               w_up=m_w_up, conv_ffn_w=m_conv_ffn_w, w_down=m_w_down, norm_final_g=m_norm_final_g)
    var = dict(norm_mix_g=v_norm_mix_g, norm_mem_g=v_norm_mem_g, w_in=v_w_in, b_gate=v_b_gate, conv_a_w=v_conv_a_w,
               w_a_out=v_w_a_out, conv_b_w=v_conv_b_w, conv_b_bias=v_conv_b_bias, ln_b_g=v_ln_b_g, ln_b_b=v_ln_b_b,
               w_b_out=v_w_b_out, w_kv=v_w_kv, w_att_out=v_w_att_out, w_o=v_w_o, norm_ffn_g=v_norm_ffn_g,
               w_up=v_w_up, conv_ffn_w=v_conv_ffn_w, w_down=v_w_down, norm_final_g=v_norm_final_g)
    depth = w_in.shape[0]
    chip = 2 * lax.axis_index("x") + lax.axis_index("y")
    core = jnp.stack([lax.axis_index("c"), chip]).astype(jnp.int32)

    chip1 = chip.astype(jnp.int32).reshape(1)
    layers = range(depth)
    sh = dict(w_in=[_cast_place([w_in], l, "col", chip1, f"cast_w_in_{l}") for l in layers],
              wsq=[_cast_place([w[n] for n in SQUARES], l, "row", chip1, f"cast_squares_{l}") for l in layers],
              w_kv=[_cast_place([w_kv], l, "col", chip1, f"cast_w_kv_{l}") for l in layers],
              w_up=[_cast_place([w_up], l, "col", chip1, f"cast_w_up_{l}") for l in layers],
              w_down=[_cast_place([w_down], l, "row", chip1, f"cast_w_down_{l}") for l in layers])
    conv_sh = [jnp.concatenate([conv_a_w, conv_b_w], axis=1), conv_ffn_w]
    small = {n: w[n] for n in REPLICATED + ("norm_final_g",)}

    loss, dx, sgrads, dg_final, pieces = _step(x[0], mem[0], loss_target[0], sh, conv_sh, small, core)

    res = {n: None for n in BIG}
    for l in reversed(range(depth)):
        for n in BIG:
            res[n], _ = _adam_shard(pieces[l][n], w[n], mom[n], var[n], l, res[n], f"adam_{n}_{l}")

    per_layer = REPLICATED + CONVS
    parts = [sgrads[l][n] for l in range(depth) for n in per_layer] + [dg_final, loss[0:1, 0:1]]
    total = _all_reduce_small(_pack(parts), "all_reduce_small")
    summed = _unpack(total, [p.shape for p in parts])
    g_small = {}
    for k, n in enumerate(per_layer):
        full = jnp.stack([summed[l * len(per_layer) + k] for l in range(depth)])
        if n in CONVS:
            cols = w[n].shape[-1]
            full = lax.dynamic_slice_in_dim(full, chip * cols, cols, axis=2)
        g_small[n] = full.reshape(w[n].shape)
    g_small["norm_final_g"] = summed[-2].reshape(norm_final_g.shape)
    names = per_layer + ("norm_final_g",)
    two_d = lambda a: a.reshape(1, -1) if a.ndim == 1 else a
    deltas, new_ms, new_vs = _adam_small(*[[two_d(t[n]) for n in names] for t in (g_small, w, mom, var)],
                                         "adam_small")
    for n, dl, nm, nv in zip(names, deltas, new_ms, new_vs):
        res[n] = (g_small[n], dl.reshape(w[n].shape), nm.reshape(w[n].shape), nv.reshape(w[n].shape))

    loss = summed[-1].reshape(())
    return (loss, dx.reshape(x.shape), *[res[n][0] for n in WEIGHTS], *[res[n][1] for n in WEIGHTS],
            *[res[n][2] for n in WEIGHTS], *[res[n][3] for n in WEIGHTS])
```

```python
import functools
import math

import jax
import jax.numpy as jnp
from jax import lax
from jax.experimental import pallas as pl
from jax.experimental.pallas import tpu as pltpu

F32 = jnp.float32
BF = jnp.bfloat16
EPS = 1e-6
N_HEADS = 4
K_A, K_B, K_F = 3, 31, 3
ADAM_LR, ADAM_B1, ADAM_B2, ADAM_EPS, ADAM_WD, ADAM_STEP = 0.001, 0.9, 0.999, 1e-08, 0.01, 10
N_CHIPS = 4
N_DEV = 8
HALO = 32
MAX_STRIPES = 8
TM_ROW = 256
TM_MM = 1024
TT_DW = 4096
DW_LHS_ELEMS = 4 * 1024 * 1024
TR_EW = 128
VMEM_LIMIT = 56 * 1024 * 1024
MESH = pl.DeviceIdType.MESH
_pallas_call = pl.pallas_call


def _params(n_axes):
    return pltpu.CompilerParams(dimension_semantics=("arbitrary",) * n_axes, vmem_limit_bytes=VMEM_LIMIT)


def _resident(shape, index):
    return pl.BlockSpec(shape, lambda *_: index, pipeline_mode=pl.Buffered(1))


def _sig(x):
    return 1.0 / (1.0 + jnp.exp(-x))


def _nt(a, b):
    return lax.dot_general(a, b, (((1,), (1,)), ((), ())), preferred_element_type=F32)


def _tn(a, b):
    return lax.dot_general(a, b, (((0,), (0,)), ((), ())), preferred_element_type=F32)


def _nn(a, b):
    return jnp.dot(a, b, preferred_element_type=F32)


class _Comm:
    def __init__(self, inputs, out_shape, scratch, run, aliases=None):
        self.inputs, self.out_shape, self.scratch, self.run = list(inputs), list(out_shape), list(scratch), run
        self.aliases = dict(aliases or {})


def _join(programs):
    programs = [p for p in programs if p is not None]
    if not programs:
        return None

    def split(seq, counts):
        parts, at = [], 0
        for n in counts:
            parts.append(seq[at:at + n])
            at += n
        return parts

    n_in = [len(p.inputs) for p in programs]
    n_out = [len(p.out_shape) for p in programs]
    n_s = [len(p.scratch) for p in programs]

    def run(phase, ins, outs, sems):
        for p, i, o, s in zip(programs, split(ins, n_in), split(outs, n_out), split(sems, n_s)):
            p.run(phase, i, o, s)

    aliases, in_at, out_at = {}, 0, 0
    for p, i, o in zip(programs, n_in, n_out):
        aliases.update({in_at + a: out_at + b for a, b in p.aliases.items()})
        in_at, out_at = in_at + i, out_at + o
    return _Comm([a for p in programs for a in p.inputs], [a for p in programs for a in p.out_shape],
                 [a for p in programs for a in p.scratch], run, aliases)


def _run(body, comm, *, name, grid, in_specs, out_specs, out_shape, args, scratch_shapes=(), aliases=None):
    n_axes = len(grid)
    aliases = dict(aliases or {})
    if comm is None:
        outs = _pallas_call(body, name=name, grid=grid, in_specs=list(in_specs), out_specs=list(out_specs),
                            out_shape=list(out_shape), scratch_shapes=list(scratch_shapes),
                            input_output_aliases=aliases, compiler_params=_params(n_axes))(*args)
        return list(outs), []
    counts = (len(in_specs), len(comm.inputs), len(out_specs), len(comm.out_shape), len(scratch_shapes),
              len(comm.scratch))

    def hosted(*refs):
        parts, at = [], 0
        for n in counts:
            parts.append(refs[at:at + n])
            at += n
        ins, c_ins, outs, c_outs, scr, c_sems = parts
        step = pl.program_id(0)
        for a in range(1, n_axes):
            step = step * grid[a] + pl.program_id(a)
        total = math.prod(grid)
        late = max(0, total - 1 - max(1, total // 4))

        @pl.when(step == 0)
        def _():
            comm.run("start", c_ins, c_outs, c_sems)

        body(*ins, *outs, *scr)

        @pl.when(step == late)
        def _():
            comm.run("forward", c_ins, c_outs, c_sems)

        @pl.when(step == total - 1)
        def _():
            comm.run("finish", c_ins, c_outs, c_sems)

    any_spec = pl.BlockSpec(memory_space=pl.ANY)
    res = _pallas_call(
        hosted, name=name, grid=grid, in_specs=list(in_specs) + [any_spec] * counts[1],
        out_specs=list(out_specs) + [any_spec] * counts[3], out_shape=list(out_shape) + comm.out_shape,
        scratch_shapes=list(scratch_shapes) + comm.scratch, compiler_params=_params(n_axes),
        input_output_aliases={**aliases, **{counts[0] + a: counts[2] + b for a, b in comm.aliases.items()}},
    )(*args, *comm.inputs)
    return list(res[:counts[2]]), list(res[counts[2]:])


def _run_comm(comm, name):
    n_in, n_out = len(comm.inputs), len(comm.out_shape)

    def body(*refs):
        ins, outs, sems = refs[:n_in], refs[n_in:n_in + n_out], refs[n_in + n_out:]
        for phase in ("start", "forward", "finish"):
            comm.run(phase, ins, outs, sems)

    any_spec = pl.BlockSpec(memory_space=pl.ANY)
    return list(_pallas_call(body, name=name, in_specs=[any_spec] * n_in, out_specs=[any_spec] * n_out,
                             out_shape=comm.out_shape, scratch_shapes=comm.scratch,
                             input_output_aliases=comm.aliases)(*comm.inputs))


SUBLANES = 8
LANES = 128
ROW_CHUNK = 128
ALL_RESIDUES = tuple(range(1, SUBLANES))
SHORT_RESIDUES = tuple(sorted({(HALO - K_A + 1 + k) % SUBLANES for k in range(K_A)} - {0}))


class _Rows:
    def __init__(self, ref, shifted_ref=None, residues=()):
        self.ref, self.shifted_ref, self.residues = ref, shifted_ref, tuple(residues)

    def shift(self):
        n = self.shifted_ref.shape[1]
        for j, b in enumerate(self.residues):
            self.shifted_ref[j] = self.ref[pl.ds(b, n), :]

    def at(self, offset, r0, c0, rows):
        b = offset % SUBLANES
        if b in self.residues:
            return self.shifted_ref[self.residues.index(b), pl.ds(offset - b + r0, rows), pl.ds(c0, LANES)]
        return self.ref[pl.ds(offset + r0, rows), pl.ds(c0, LANES)]


def _shifted_scratch(residues, tm, c):
    return pltpu.VMEM((len(residues), tm + HALO - SUBLANES, c), F32)


def _causal_offsets(k_taps):
    return [HALO - k_taps + 1 + k for k in range(k_taps)]


def _anticausal_offsets(k_taps):
    return [k_taps - 1 - k for k in range(k_taps)]


def _tap_chunk(src, w_ref, offsets, r0, c0, rows):
    acc = w_ref[0:1, pl.ds(c0, LANES)] * src.at(offsets[0], r0, c0, rows)
    for k in range(1, len(offsets)):
        acc = acc + w_ref[k:k + 1, pl.ds(c0, LANES)] * src.at(offsets[k], r0, c0, rows)
    return acc


def _conv_whole(ref, w_ref, offsets, tm):
    acc = w_ref[0:1, :] * ref[pl.ds(offsets[0], tm), :]
    for k in range(1, len(offsets)):
        acc = acc + w_ref[k:k + 1, :] * ref[pl.ds(offsets[k], tm), :]
    return acc


def _tap_grads_whole(dw_ref, dy, ref, offsets, tm):
    for k, off in enumerate(offsets):
        dw_ref[k:k + 1, :] += jnp.sum(dy * ref[pl.ds(off, tm), :], axis=0, keepdims=True)


def _conv_taps(src, w_ref, offsets, tm, emit):
    rows = min(ROW_CHUNK, tm)
    for c0 in range(0, w_ref.shape[1], LANES):
        for r0 in range(0, tm, rows):
            emit(r0, rows, c0, _tap_chunk(src, w_ref, offsets, r0, c0, rows))


def _tap_grads(dw_ref, dy_at, src, offsets, tm):
    rows = min(ROW_CHUNK // 2, tm)
    for c0 in range(0, dw_ref.shape[1], LANES):
        acc = [None] * len(offsets)
        for r0 in range(0, tm, rows):
            dy = dy_at(r0, rows, c0)
            for k, off in enumerate(offsets):
                part = (dy * src.at(off, r0, c0, rows)).reshape(rows // SUBLANES, SUBLANES, LANES).sum(axis=0)
                acc[k] = part if acc[k] is None else acc[k] + part
        for k in range(len(offsets)):
            dw_ref[k:k + 1, pl.ds(c0, LANES)] += jnp.sum(acc[k], axis=0, keepdims=True)


def _prev_halo(tm, col):
    return lambda i: (jnp.maximum(i * (tm // HALO) - 1, 0), col)


def _next_halo(tm, n_rows, col):
    return lambda i: (jnp.minimum((i + 1) * (tm // HALO), n_rows // HALO - 1), col)


def _norm_matmul(x, g, w, gs, name, comm=None):
    t, d = x.shape
    n_s, _, ns = w.shape
    n = n_s * ns
    tm = min(TM_MM, t)

    def body(x_ref, g_ref, w_ref, h_ref, y_ref):
        @pl.when(pl.program_id(1) == 0)
        def _():
            xf = x_ref[...]
            r = lax.rsqrt(jnp.mean(xf * xf, axis=-1, keepdims=True) + EPS)
            h_ref[...] = ((xf * r) * g_ref[...]).astype(BF)

        for s in range(gs):
            y_ref[:, s * ns:(s + 1) * ns] = _nn(h_ref[...], w_ref[s]).astype(BF)

    return _run(
        body, comm, name=name, grid=(t // tm, n_s // gs),
        in_specs=[pl.BlockSpec((tm, d), lambda i, j: (i, 0)),
                  pl.BlockSpec((1, d), lambda i, j: (0, 0)),
                  pl.BlockSpec((gs, d, ns), lambda i, j: (j, 0, 0))],
        out_specs=[pl.BlockSpec((tm, d), lambda i, j: (i, 0)),
                   pl.BlockSpec((tm, gs * ns), lambda i, j: (i, j))],
        out_shape=[jax.ShapeDtypeStruct((t, d), BF), jax.ShapeDtypeStruct((t, n), BF)],
        args=(x, g, w))


def _mem_kv(mem, g, w_kv, name):
    m, d = mem.shape
    n_s, _, ns = w_kv.shape

    def body(mem_ref, g_ref, w_ref, memn_ref, kv_ref):
        xf = mem_ref[...]
        r = lax.rsqrt(jnp.mean(xf * xf, axis=-1, keepdims=True) + EPS)
        memn = ((xf * r) * g_ref[...]).astype(BF)
        memn_ref[...] = memn
        for s in range(n_s):
            kv_ref[:, s * ns:(s + 1) * ns] = _nn(memn, w_ref[s]).astype(BF)

    return _pallas_call(
        body, name=name, grid=(1,),
        in_specs=[pl.BlockSpec((m, d), lambda i: (0, 0)),
                  pl.BlockSpec((1, d), lambda i: (0, 0)),
                  pl.BlockSpec((n_s, d, ns), lambda i: (0, 0, 0))],
        out_specs=[pl.BlockSpec((m, d), lambda i: (0, 0)),
                   pl.BlockSpec((m, 2 * d), lambda i: (0, 0))],
        out_shape=[jax.ShapeDtypeStruct((m, d), BF), jax.ShapeDtypeStruct((m, 2 * d), BF)],
        compiler_params=_params(1),
    )(mem, g, w_kv)


def _load_branch_inputs(i, proj_ref, gch_ref, vh_ref, u0h_ref, ugh_ref, xa_ref, xb_ref, d):
    gc = proj_ref[:, d:2 * d].astype(F32)
    v = proj_ref[:, 2 * d:3 * d].astype(F32)
    u0 = proj_ref[:, 3 * d:4 * d].astype(F32)
    ug = proj_ref[:, 4 * d:5 * d].astype(F32)
    keep = (i > 0).astype(F32)
    xa_ref[pl.ds(0, HALO), :] = gch_ref[...].astype(F32) * vh_ref[...].astype(F32) * keep
    xa_ref[pl.ds(HALO, gc.shape[0]), :] = gc * v
    xb_ref[pl.ds(0, HALO), :] = u0h_ref[...].astype(F32) * _sig(ugh_ref[...].astype(F32)) * keep
    xb_ref[pl.ds(HALO, gc.shape[0]), :] = u0 * _sig(ug)


def _softmax_rows(s):
    e = jnp.exp(s - jnp.max(s, axis=-1, keepdims=True))
    return e / jnp.sum(e, axis=-1, keepdims=True)


def _mixer_fwd(proj, x, kv, conv_a, conv_b, cbias, ln_g, ln_b, b_gate, wsq, l, name, comm=None):
    t, d = x.shape
    m = kv.shape[0]
    tm = min(TM_ROW, t)
    hd = d // N_HEADS
    scale = 1.0 / math.sqrt(hd)

    def body(proj_ref, gch_ref, vh_ref, u0h_ref, ugh_ref, x_ref, kv_ref, ca_w, cb_w, cbias_ref, lng_ref, lnb_ref,
             bg_ref, wa_ref, wb_ref, wc_ref, wo_ref,
             x1_ref, za_ref, zb_ref, o_ref, ya_ref, yb_ref, yc_ref, mg_ref, cb_ref, cv_ref, ub_ref,
             xa_ref, xb_ref, sb_ref):
        i = pl.program_id(0)
        _load_branch_inputs(i, proj_ref, gch_ref, vh_ref, u0h_ref, ugh_ref, xa_ref, xb_ref, d)
        cv_ref[...] = xa_ref[pl.ds(HALO, tm), :].astype(BF)
        ub_ref[...] = xb_ref[pl.ds(HALO, tm), :].astype(BF)
        xb = _Rows(xb_ref, sb_ref, ALL_RESIDUES)
        xb.shift()
        def put_za(r0, rows, c0, ca):
            gb = proj_ref[pl.ds(r0, rows), pl.ds(c0, LANES)].astype(F32)
            za_ref[pl.ds(r0, rows), pl.ds(c0, LANES)] = (gb * ca).astype(BF)

        _conv_taps(_Rows(xa_ref), ca_w, _causal_offsets(K_A), tm, put_za)
        ya = _nn(za_ref[...], wa_ref[...])
        ya_ref[...] = ya.astype(BF)

        def put_cb(r0, rows, c0, conv):
            cb_ref[pl.ds(r0, rows), pl.ds(c0, LANES)] = conv + cbias_ref[:, pl.ds(c0, LANES)]

        _conv_taps(xb, cb_w, _causal_offsets(K_B), tm, put_cb)
        cb = cb_ref[...]
        mu = jnp.mean(cb, axis=-1, keepdims=True)
        dlt = cb - mu
        rstd = lax.rsqrt(jnp.mean(dlt * dlt, axis=-1, keepdims=True) + EPS)
        lnb = (dlt * rstd) * lng_ref[...] + lnb_ref[...]
        zb = (lnb * _sig(lnb)).astype(BF)
        zb_ref[...] = zb
        yb = _nn(zb, wb_ref[...])
        yb_ref[...] = yb.astype(BF)
        for h in range(N_HEADS):
            qh = proj_ref[:, 5 * d + h * hd:5 * d + (h + 1) * hd]
            kh = kv_ref[:, h * hd:(h + 1) * hd]
            vh = kv_ref[:, d + h * hd:d + (h + 1) * hd]
            p = _softmax_rows(_nt(qh, kh) * scale)
            o_ref[:, h * hd:(h + 1) * hd] = _nn(p.astype(BF), vh).astype(BF)
        yc = _nn(o_ref[...], wc_ref[...])
        yc_ref[...] = yc.astype(BF)
        g0 = _sig(proj_ref[:, 6 * d:7 * d].astype(F32) + bg_ref[:, 0:d])
        g1 = _sig(proj_ref[:, 7 * d:8 * d].astype(F32) + bg_ref[:, d:2 * d])
        g2 = _sig(proj_ref[:, 8 * d:9 * d].astype(F32) + bg_ref[:, 2 * d:3 * d])
        mg = (g0 * ya + g1 * yb + g2 * yc).astype(BF)
        mg_ref[...] = mg
        x1_ref[...] = x_ref[...] + _nn(mg, wo_ref[...])

    row = lambda w_: pl.BlockSpec((tm, w_), lambda i: (i, 0))
    halo = lambda col: pl.BlockSpec((HALO, d), _prev_halo(tm, col))
    sq = lambda which: _resident((None, None, d, d), (l, which, 0, 0))
    act = jax.ShapeDtypeStruct((t, d), BF)
    return _run(
        body, comm, name=name, grid=(t // tm,),
        in_specs=[row(9 * d), halo(1), halo(2), halo(3), halo(4), row(d),
                  _resident((m, 2 * d), (0, 0)),
                  _resident((None, K_A, d), (l, 0, 0)), _resident((None, K_B, d), (l, 0, 0)),
                  _resident((1, d), (0, 0)), _resident((1, d), (0, 0)), _resident((1, d), (0, 0)),
                  _resident((1, 3 * d), (0, 0)), sq(0), sq(1), sq(2), sq(3)],
        out_specs=[row(d)] * 11,
        out_shape=[jax.ShapeDtypeStruct((t, d), F32)] + [act] * 7 + [jax.ShapeDtypeStruct((t, d), F32), act, act],
        scratch_shapes=[pltpu.VMEM((HALO + tm, d), F32), pltpu.VMEM((HALO + tm, d), F32),
                        _shifted_scratch(ALL_RESIDUES, tm, d)],
        args=(proj, proj, proj, proj, proj, x, kv, conv_a, conv_b, cbias, ln_g, ln_b, b_gate, wsq, wsq, wsq, wsq))


def _ffn_down_fwd(up, x1, conv_f, w_down, l, name, comm=None):
    t, d = x1.shape
    f2 = up.shape[1]
    f = f2 // 2
    tm = min(TM_ROW, t)

    def body(up_ref, uph_ref, x1_ref, cw_ref, wd_ref, x2_ref, zf_ref, uc_ref, xx_ref):
        i = pl.program_id(0)
        xx_ref[pl.ds(0, HALO), :] = uph_ref[...].astype(F32) * (i > 0).astype(F32)
        xx_ref[pl.ds(HALO, tm), :] = up_ref[...].astype(F32)
        uc = _conv_whole(xx_ref, cw_ref, _causal_offsets(K_F), tm)
        uc_ref[...] = uc.astype(BF)
        gt = uc[:, 0:f]
        zf = (gt * _sig(gt) * uc[:, f:f2]).astype(BF)
        zf_ref[...] = zf
        x2_ref[...] = x1_ref[...] + _nn(zf, wd_ref[...])

    return _run(
        body, comm, name=name, grid=(t // tm,),
        in_specs=[pl.BlockSpec((tm, f2), lambda i: (i, 0)),
                  pl.BlockSpec((HALO, f2), _prev_halo(tm, 0)),
                  pl.BlockSpec((tm, d), lambda i: (i, 0)),
                  _resident((None, K_F, f2), (l, 0, 0)),
                  _resident((None, f, d), (l, 0, 0))],
        out_specs=[pl.BlockSpec((tm, d), lambda i: (i, 0)), pl.BlockSpec((tm, f), lambda i: (i, 0)),
                   pl.BlockSpec((tm, f2), lambda i: (i, 0))],
        out_shape=[jax.ShapeDtypeStruct((t, d), F32), jax.ShapeDtypeStruct((t, f), BF),
                   jax.ShapeDtypeStruct((t, f2), BF)],
        scratch_shapes=[pltpu.VMEM((HALO + tm, f2), F32)],
        args=(up, up, x1, conv_f, w_down))


def _final_loss(x, g, target, name):
    t, d = x.shape
    tm = min(2 * TM_ROW, t)

    def body(x_ref, g_ref, t_ref, dx_ref, loss_ref, dg_ref):
        @pl.when(pl.program_id(0) == 0)
        def _():
            loss_ref[...] = jnp.zeros_like(loss_ref)
            dg_ref[...] = jnp.zeros_like(dg_ref)

        xf = x_ref[...]
        r = lax.rsqrt(jnp.mean(xf * xf, axis=-1, keepdims=True) + EPS)
        xhat = xf * r
        err = xhat * g_ref[...] - t_ref[...]
        loss_ref[...] += (0.5 / d) * jnp.sum(err * err)
        dy = err * (1.0 / d)
        dg_ref[...] += jnp.sum(dy * xhat, axis=0, keepdims=True)
        dxh = dy * g_ref[...]
        dx_ref[...] = r * (dxh - xhat * jnp.mean(dxh * xhat, axis=-1, keepdims=True))

    return _pallas_call(
        body, name=name, grid=(t // tm,),
        in_specs=[pl.BlockSpec((tm, d), lambda i: (i, 0)), pl.BlockSpec((1, d), lambda i: (0, 0)),
                  pl.BlockSpec((tm, d), lambda i: (i, 0))],
        out_specs=[pl.BlockSpec((tm, d), lambda i: (i, 0)), pl.BlockSpec((8, 128), lambda i: (0, 0)),
                   pl.BlockSpec((1, d), lambda i: (0, 0))],
        out_shape=[jax.ShapeDtypeStruct((t, d), F32), jax.ShapeDtypeStruct((8, 128), F32),
                   jax.ShapeDtypeStruct((1, d), F32)],
        compiler_params=_params(1),
    )(x, g, target)


def _ffn_down_bwd(dx2, up, uc, w_down, l, name, comm=None):
    t, d = dx2.shape
    f2 = up.shape[1]
    f = f2 // 2
    tm = min(TM_ROW, t)

    def body(dx2_ref, up_ref, uph_ref, uc_ref, wd_ref, duc_ref, dx2b_ref, dcw_ref, xx_ref):
        i = pl.program_id(0)

        @pl.when(i == 0)
        def _():
            dcw_ref[...] = jnp.zeros_like(dcw_ref)

        xx_ref[pl.ds(0, HALO), :] = uph_ref[...].astype(F32) * (i > 0).astype(F32)
        xx_ref[pl.ds(HALO, tm), :] = up_ref[...].astype(F32)
        gt = uc_ref[:, 0:f].astype(F32)
        sg = _sig(gt)
        dx2b = dx2_ref[...].astype(BF)
        dx2b_ref[...] = dx2b
        dzf = _nt(dx2b, wd_ref[...])
        duc_ref[:, 0:f] = (dzf * uc_ref[:, f:f2].astype(F32) * (sg * (1.0 + gt * (1.0 - sg)))).astype(BF)
        duc_ref[:, f:f2] = (dzf * (gt * sg)).astype(BF)
        _tap_grads_whole(dcw_ref, duc_ref[...].astype(F32), xx_ref, _causal_offsets(K_F), tm)

    return _run(
        body, comm, name=name, grid=(t // tm,),
        in_specs=[pl.BlockSpec((tm, d), lambda i: (i, 0)),
                  pl.BlockSpec((tm, f2), lambda i: (i, 0)),
                  pl.BlockSpec((HALO, f2), _prev_halo(tm, 0)),
                  pl.BlockSpec((tm, f2), lambda i: (i, 0)),
                  _resident((None, f, d), (l, 0, 0))],
        out_specs=[pl.BlockSpec((tm, f2), lambda i: (i, 0)), pl.BlockSpec((tm, d), lambda i: (i, 0)),
                   pl.BlockSpec((K_F, f2), lambda i: (0, 0))],
        out_shape=[jax.ShapeDtypeStruct((t, f2), BF), jax.ShapeDtypeStruct((t, d), BF),
                   jax.ShapeDtypeStruct((K_F, f2), F32)],
        scratch_shapes=[pltpu.VMEM((HALO + tm, f2), F32)],
        args=(dx2, up, up, uc, w_down))


def _ffn_conv_bwd(duc, conv_f, l, name, comm=None):
    t, f2 = duc.shape
    tm = min(TM_ROW, t)
    n_t = t // tm

    def body(duc_ref, nxt_ref, cw_ref, dup_ref, yy_ref):
        i = pl.program_id(0)
        yy_ref[pl.ds(0, tm), :] = duc_ref[...].astype(F32)
        yy_ref[pl.ds(tm, HALO), :] = nxt_ref[...].astype(F32) * (i < n_t - 1).astype(F32)

        def put(r0, rows, c0, conv):
            dup_ref[pl.ds(r0, rows), pl.ds(c0, LANES)] = conv.astype(BF)

        _conv_taps(_Rows(yy_ref), cw_ref, _anticausal_offsets(K_F), tm, put)

    return _run(
        body, comm, name=name, grid=(n_t,),
        in_specs=[pl.BlockSpec((tm, f2), lambda i: (i, 0)),
                  pl.BlockSpec((HALO, f2), _next_halo(tm, t, 0)),
                  _resident((None, K_F, f2), (l, 0, 0))],
        out_specs=[pl.BlockSpec((tm, f2), lambda i: (i, 0))],
        out_shape=[jax.ShapeDtypeStruct((t, f2), BF)],
        scratch_shapes=[pltpu.VMEM((tm + HALO, f2), F32)],
        args=(duc, duc, conv_f))


def _nt_matmul_norm_bwd(dy, w, gs, x, g, dres, name, comm=None):
    t, n = dy.shape
    d = x.shape[1]
    tm = min(TM_MM // 2, t)
    n_s, _, ns = w.shape
    n_k, tk = n_s // gs, gs * ns

    def body(dy_ref, w_ref, x_ref, g_ref, dres_ref, dx_ref, dxb_ref, dg_ref, acc_ref):
        i, k = pl.program_id(0), pl.program_id(1)

        @pl.when((i == 0) & (k == 0))
        def _():
            dg_ref[...] = jnp.zeros_like(dg_ref)

        @pl.when(k == 0)
        def _():
            acc_ref[...] = jnp.zeros_like(acc_ref)

        part = _nt(dy_ref[:, 0:ns], w_ref[0])
        for s in range(1, gs):
            part = part + _nt(dy_ref[:, s * ns:(s + 1) * ns], w_ref[s])
        acc_ref[...] += part

        @pl.when(k == n_k - 1)
        def _():
            xf = x_ref[...]
            r = lax.rsqrt(jnp.mean(xf * xf, axis=-1, keepdims=True) + EPS)
            xhat = xf * r
            dh = acc_ref[...]
            dg_ref[...] += jnp.sum(dh * xhat, axis=0, keepdims=True)
            dxh = dh * g_ref[...]
            dx = dres_ref[...] + r * (dxh - xhat * jnp.mean(dxh * xhat, axis=-1, keepdims=True))
            dx_ref[...] = dx
            dxb_ref[...] = dx.astype(BF)

    return _run(
        body, comm, name=name, grid=(t // tm, n_k),
        in_specs=[pl.BlockSpec((tm, tk), lambda i, k: (i, k)),
                  pl.BlockSpec((gs, d, ns), lambda i, k: (k, 0, 0)),
                  pl.BlockSpec((tm, d), lambda i, k: (i, 0)),
                  pl.BlockSpec((1, d), lambda i, k: (0, 0)),
                  pl.BlockSpec((tm, d), lambda i, k: (i, 0))],
        out_specs=[pl.BlockSpec((tm, d), lambda i, k: (i, 0)), pl.BlockSpec((tm, d), lambda i, k: (i, 0)),
                   pl.BlockSpec((1, d), lambda i, k: (0, 0))],
        out_shape=[jax.ShapeDtypeStruct((t, d), F32), jax.ShapeDtypeStruct((t, d), BF),
                   jax.ShapeDtypeStruct((1, d), F32)],
        scratch_shapes=[pltpu.VMEM((tm, d), F32)],
        args=(dy, w, x, g, dres))


def _mixer_bwd(dx1b, proj, cv, ub, ya, yb, yc, cb, kv, conv_a, conv_b, ln_g, ln_b, b_gate, wsq, l, name, comm=None):
    t, d = cb.shape
    m = kv.shape[0]
    tm = min(TM_ROW, t)
    hd = d // N_HEADS
    scale = 1.0 / math.sqrt(hd)

    def body(dx1b_ref, proj_ref, cv_ref, cvh_ref, ub_ref, ubh_ref, ya_ref, yb_ref, yc_ref, cb_ref, kv_ref,
             ca_w, cb_w, lng_ref, lnb_ref, bg_ref, wa_ref, wb_ref, wc_ref, wo_ref,
             dpre_ref, dya_ref, dyb_ref, dyc_ref, dkv_ref, dbg_ref, dlng_ref, dlnb_ref, dcbias_ref, dcaw_ref,
             dcbw_ref, xa_ref, xb_ref, sa_ref, sb_ref):
        i = pl.program_id(0)

        @pl.when(i == 0)
        def _():
            for ref in (dkv_ref, dbg_ref, dlng_ref, dlnb_ref, dcbias_ref, dcaw_ref, dcbw_ref):
                ref[...] = jnp.zeros_like(ref)

        keep = (i > 0).astype(F32)
        xa_ref[pl.ds(0, HALO), :] = cvh_ref[...].astype(F32) * keep
        xa_ref[pl.ds(HALO, tm), :] = cv_ref[...].astype(F32)
        xb_ref[pl.ds(0, HALO), :] = ubh_ref[...].astype(F32) * keep
        xb_ref[pl.ds(HALO, tm), :] = ub_ref[...].astype(F32)
        xa, xb = _Rows(xa_ref, sa_ref, SHORT_RESIDUES), _Rows(xb_ref, sb_ref, ALL_RESIDUES)
        xa.shift()
        xb.shift()
        dmg = _nt(dx1b_ref[...], wo_ref[...])
        ys = (ya_ref, yb_ref, yc_ref)
        dys = (dya_ref, dyb_ref, dyc_ref)
        for b in range(3):
            gate = _sig(proj_ref[:, (6 + b) * d:(7 + b) * d].astype(F32) + bg_ref[:, b * d:(b + 1) * d])
            dys[b][...] = (gate * dmg).astype(BF)
            dpg = dmg * ys[b][...].astype(F32) * gate * (1.0 - gate)
            dpre_ref[:, (6 + b) * d:(7 + b) * d] = dpg.astype(BF)
            dbg_ref[:, b * d:(b + 1) * d] += jnp.sum(dpg, axis=0, keepdims=True)
        gb = proj_ref[:, 0:d].astype(F32)
        dza = _nt(dya_ref[...], wa_ref[...])

        def put_dgb(r0, rows, c0, ca):
            dpre_ref[pl.ds(r0, rows), pl.ds(c0, LANES)] = (dza[r0:r0 + rows, c0:c0 + LANES] * ca).astype(BF)

        _conv_taps(xa, ca_w, _causal_offsets(K_A), tm, put_dgb)
        dpre_ref[:, d:2 * d] = (dza * gb).astype(BF)
        _tap_grads(dcaw_ref, lambda r0, rows, c0: dpre_ref[pl.ds(r0, rows), pl.ds(d + c0, LANES)].astype(F32),
                   xa, _causal_offsets(K_A), tm)
        dpre_ref[:, 2 * d:3 * d] = jnp.zeros((tm, d), BF)
        cbv = cb_ref[...]
        mu = jnp.mean(cbv, axis=-1, keepdims=True)
        dlt = cbv - mu
        rstd = lax.rsqrt(jnp.mean(dlt * dlt, axis=-1, keepdims=True) + EPS)
        xhat = dlt * rstd
        lnb = xhat * lng_ref[...] + lnb_ref[...]
        sg = _sig(lnb)
        dzb = _nt(dyb_ref[...], wb_ref[...])
        dl = dzb * (sg * (1.0 + lnb * (1.0 - sg)))
        dlng_ref[...] += jnp.sum(dl * xhat, axis=0, keepdims=True)
        dlnb_ref[...] += jnp.sum(dl, axis=0, keepdims=True)
        dxh = dl * lng_ref[...]
        dcb = rstd * (dxh - jnp.mean(dxh, axis=-1, keepdims=True)
                      - xhat * jnp.mean(dxh * xhat, axis=-1, keepdims=True))
        dcbias_ref[...] += jnp.sum(dcb, axis=0, keepdims=True)
        dpre_ref[:, 3 * d:4 * d] = dcb.astype(BF)
        _tap_grads(dcbw_ref, lambda r0, rows, c0: dpre_ref[pl.ds(r0, rows), pl.ds(3 * d + c0, LANES)].astype(F32),
                   xb, _causal_offsets(K_B), tm)
        dpre_ref[:, 4 * d:5 * d] = jnp.zeros((tm, d), BF)
        do = _nt(dyc_ref[...], wc_ref[...]).astype(BF)
        for h in range(N_HEADS):
            qh = proj_ref[:, 5 * d + h * hd:5 * d + (h + 1) * hd]
            kh = kv_ref[:, h * hd:(h + 1) * hd]
            vh = kv_ref[:, d + h * hd:d + (h + 1) * hd]
            doh = do[:, h * hd:(h + 1) * hd]
            p = _softmax_rows(_nt(qh, kh) * scale)
            dp = _nt(doh, vh)
            ds = (p * (dp - jnp.sum(dp * p, axis=-1, keepdims=True)) * scale).astype(BF)
            dpre_ref[:, 5 * d + h * hd:5 * d + (h + 1) * hd] = _nn(ds, kh).astype(BF)
            dkv_ref[:, h * hd:(h + 1) * hd] += _tn(ds, qh)
            dkv_ref[:, d + h * hd:d + (h + 1) * hd] += _tn(p.astype(BF), doh)

    row = lambda w_: pl.BlockSpec((tm, w_), lambda i: (i, 0))
    halo = lambda col: pl.BlockSpec((HALO, d), _prev_halo(tm, col))
    sq = lambda which: _resident((None, None, d, d), (l, which, 0, 0))
    acc = lambda r, c: pl.BlockSpec((r, c), lambda i: (0, 0))
    act = jax.ShapeDtypeStruct((t, d), BF)
    vec = lambda r, c: jax.ShapeDtypeStruct((r, c), F32)
    return _run(
        body, comm, name=name, grid=(t // tm,),
        in_specs=[row(d), row(9 * d), row(d), halo(0), row(d), halo(0), row(d), row(d), row(d), row(d),
                  _resident((m, 2 * d), (0, 0)),
                  _resident((None, K_A, d), (l, 0, 0)), _resident((None, K_B, d), (l, 0, 0)),
                  _resident((1, d), (0, 0)), _resident((1, d), (0, 0)), _resident((1, 3 * d), (0, 0)),
                  sq(0), sq(1), sq(2), sq(3)],
        out_specs=[row(9 * d), row(d), row(d), row(d), acc(m, 2 * d), acc(1, 3 * d), acc(1, d), acc(1, d),
                   acc(1, d), acc(K_A, d), acc(K_B, d)],
        out_shape=[jax.ShapeDtypeStruct((t, 9 * d), BF), act, act, act, vec(m, 2 * d), vec(1, 3 * d), vec(1, d),
                   vec(1, d), vec(1, d), vec(K_A, d), vec(K_B, d)],
        scratch_shapes=[pltpu.VMEM((HALO + tm, d), F32), pltpu.VMEM((HALO + tm, d), F32),
                        _shifted_scratch(SHORT_RESIDUES, tm, d), _shifted_scratch(ALL_RESIDUES, tm, d)],
        args=(dx1b, proj, cv, cv, ub, ub, ya, yb, yc, cb, kv, conv_a, conv_b, ln_g, ln_b, b_gate,
              wsq, wsq, wsq, wsq))


def _inproj_conv_bwd(dpre, proj, conv_a, conv_b, l, name, comm=None):
    t, d9 = dpre.shape
    d = d9 // 9
    tm = min(TM_ROW, t)
    n_t = t // tm

    def body(dpre_ref, nxa_ref, nxb_ref, proj_ref, ca_w, cb_w, dproj_ref, ya_ref, yb_ref, sb_ref):
        i = pl.program_id(0)
        keep = (i < n_t - 1).astype(F32)
        ya_ref[pl.ds(0, tm), :] = dpre_ref[:, d:2 * d].astype(F32)
        ya_ref[pl.ds(tm, HALO), :] = nxa_ref[...].astype(F32) * keep
        yb_ref[pl.ds(0, tm), :] = dpre_ref[:, 3 * d:4 * d].astype(F32)
        yb_ref[pl.ds(tm, HALO), :] = nxb_ref[...].astype(F32) * keep
        yb = _Rows(yb_ref, sb_ref, ALL_RESIDUES)
        yb.shift()
        dproj_ref[:, 0:d] = dpre_ref[:, 0:d]
        dproj_ref[:, 5 * d:9 * d] = dpre_ref[:, 5 * d:9 * d]
        def chunk(ref, block, r0, rows, c0):
            return ref.at[pl.ds(r0, rows), pl.ds(block * d + c0, LANES)]

        def put_a(r0, rows, c0, dcv):
            chunk(dproj_ref, 1, r0, rows, c0)[...] = (dcv * chunk(proj_ref, 2, r0, rows, c0)[...].astype(F32)).astype(BF)
            chunk(dproj_ref, 2, r0, rows, c0)[...] = (dcv * chunk(proj_ref, 1, r0, rows, c0)[...].astype(F32)).astype(BF)

        def put_b(r0, rows, c0, dub):
            sg = _sig(chunk(proj_ref, 4, r0, rows, c0)[...].astype(F32))
            u0 = chunk(proj_ref, 3, r0, rows, c0)[...].astype(F32)
            chunk(dproj_ref, 3, r0, rows, c0)[...] = (dub * sg).astype(BF)
            chunk(dproj_ref, 4, r0, rows, c0)[...] = (dub * u0 * sg * (1.0 - sg)).astype(BF)

        _conv_taps(_Rows(ya_ref), ca_w, _anticausal_offsets(K_A), tm, put_a)
        _conv_taps(yb, cb_w, _anticausal_offsets(K_B), tm, put_b)

    return _run(
        body, comm, name=name, grid=(n_t,),
        in_specs=[pl.BlockSpec((tm, d9), lambda i: (i, 0)),
                  pl.BlockSpec((HALO, d), _next_halo(tm, t, 1)),
                  pl.BlockSpec((HALO, d), _next_halo(tm, t, 3)),
                  pl.BlockSpec((tm, d9), lambda i: (i, 0)),
                  _resident((None, K_A, d), (l, 0, 0)), _resident((None, K_B, d), (l, 0, 0))],
        out_specs=[pl.BlockSpec((tm, d9), lambda i: (i, 0))],
        out_shape=[jax.ShapeDtypeStruct((t, d9), BF)],
        scratch_shapes=[pltpu.VMEM((tm + HALO, d), F32), pltpu.VMEM((tm + HALO, d), F32),
                        _shifted_scratch(ALL_RESIDUES, tm, d)],
        args=(dpre, dpre, dpre, proj, conv_a, conv_b))


def _mem_kv_bwd(dkv, memn, mem, g, w_kv, name):
    m, d = mem.shape
    n_s, _, ns = w_kv.shape

    def body(dkv_ref, memn_ref, mem_ref, g_ref, w_ref, dw_ref, dg_ref):
        dkvb = dkv_ref[...].astype(BF)
        dw_ref[...] = _tn(memn_ref[...], dkvb).astype(BF)
        dmemn = _nt(dkvb[:, 0:ns], w_ref[0])
        for s in range(1, n_s):
            dmemn = dmemn + _nt(dkvb[:, s * ns:(s + 1) * ns], w_ref[s])
        xf = mem_ref[...]
        r = lax.rsqrt(jnp.mean(xf * xf, axis=-1, keepdims=True) + EPS)
        dg_ref[...] = jnp.sum(dmemn * (xf * r), axis=0, keepdims=True)

    return _pallas_call(
        body, name=name, grid=(1,),
        in_specs=[pl.BlockSpec((m, 2 * d), lambda i: (0, 0)), pl.BlockSpec((m, d), lambda i: (0, 0)),
                  pl.BlockSpec((m, d), lambda i: (0, 0)), pl.BlockSpec((1, d), lambda i: (0, 0)),
                  pl.BlockSpec((n_s, d, ns), lambda i: (0, 0, 0))],
        out_specs=[pl.BlockSpec((d, 2 * d), lambda i: (0, 0)), pl.BlockSpec((1, d), lambda i: (0, 0))],
        out_shape=[jax.ShapeDtypeStruct((d, 2 * d), BF), jax.ShapeDtypeStruct((1, d), F32)],
        compiler_params=_params(1),
    )(dkv, memn, mem, g, w_kv)


def _dw_matmul(a, b, tn, name, comm=None):
    t, k = a.shape
    n = b.shape[1]
    tt = min(TT_DW, t)
    while tt * k > DW_LHS_ELEMS and tt % 2 == 0:
        tt //= 2
    n_s = t // tt

    def body(a_ref, b_ref, o_ref, acc_ref):
        s = pl.program_id(1)
        part = _tn(a_ref[...], b_ref[...])
        if n_s == 1:
            o_ref[...] = part.astype(BF)
            return

        @pl.when(s == 0)
        def _():
            acc_ref[...] = part

        @pl.when(s > 0)
        def _():
            acc_ref[...] += part

        @pl.when(s == n_s - 1)
        def _():
            o_ref[...] = acc_ref[...].astype(BF)

    return _run(
        body, comm, name=name, grid=(n // tn, n_s),
        in_specs=[pl.BlockSpec((tt, k), lambda j, s: (s, 0)), pl.BlockSpec((tt, tn), lambda j, s: (s, j))],
        out_specs=[pl.BlockSpec((k, tn), lambda j, s: (0, j))],
        out_shape=[jax.ShapeDtypeStruct((k, n), BF)],
        scratch_shapes=[pltpu.VMEM((k, tn) if n_s > 1 else (8, 128), F32)],
        args=(a, b))


class _GradReduce:
    def __init__(self, l, grads, core):
        self.l, self.core, self.names = l, core, tuple(grads)
        self.views = {n: _halves_view(n, g) for n, g in grads.items()}
        self.got, self.sums, self.landing, self.pieces = {}, {}, {}, {}

    def swap_program(self):
        return _swap_program([self.views[n] for n in self.names])

    def swapped(self, outs):
        self.got = dict(zip(self.names, outs))
        for n in self.names:
            self.sums[n], self.landing[n] = _add_halves(self.views[n], self.got[n], self.core, n in COL,
                                                        f"add_halves_{n}_{self.l}")

    def scatter_program(self, names=None, part=(0, 1)):
        names = self.names if names is None else names
        return _scatter_program([self.sums[n] for n in names], [self.landing[n] for n in names], names, part)

    def scattered(self, outs, names=None):
        names = self.names if names is None else names
        self.landing.update(zip(names, outs))
        self.pieces.update(zip(names, outs))


def _step(x, mem, target, sh, conv_sh, small, core):
    depth = len(sh["w_in"])
    d = x.shape[1]
    f2 = sh["w_up"][0].shape[2] * N_CHIPS
    tn_dw_in = min(1024, d)
    tn_dw_sq = max(LANES, d // 4)
    tn_dw_up = f2 // 11 if f2 % (11 * 128) == 0 and f2 // 11 >= 128 else f2
    row = lambda v: v.reshape(1, -1)
    one = lambda a: a[None]

    w_in, cab, cf = _run_comm(_gather_program([sh["w_in"][0]] + conv_sh, ("col", "small", "small")), "gather_first")
    saved = []
    for l in range(depth):
        conv = dict(a=cab[l:l + 1, :K_A], b=cab[l:l + 1, K_A:], f=cf[l:l + 1])
        (h, proj), (wsq, w_kv) = _norm_matmul(
            x, row(small["norm_mix_g"][l]), w_in, 1, f"in_proj_{l}",
            _gather_program([sh["wsq"][l], sh["w_kv"][l]], ("row", "col")))
        memn, kv = _mem_kv(mem, row(small["norm_mem_g"][l]), w_kv, f"mem_kv_{l}")
        (x1, za, zb, o, ya, yb, yc, mg, cb, cv, ub), (w_up, w_down) = _mixer_fwd(
            proj, x, kv, conv["a"], conv["b"], row(small["conv_b_bias"][l]), row(small["ln_b_g"][l]),
            row(small["ln_b_b"][l]), row(small["b_gate"][l]), one(wsq), 0, f"mixer_fwd_{l}",
            _gather_program([sh["w_up"][l], sh["w_down"][l]], ("col", "row")))
        more = l + 1 < depth
        nxt = _gather_program([sh["w_in"][l + 1]], ("col",), (0, 2)) if more else None
        (h2, up), w_in_next = _norm_matmul(x1, row(small["norm_ffn_g"][l]), w_up, 2, f"up_proj_{l}", nxt)
        nxt = _gather_program(w_in_next, ("col",), (1, 2)) if more else None
        (x2, zf, uc), w_in_next = _ffn_down_fwd(up, x1, conv["f"], one(w_down), 0, f"ffn_down_fwd_{l}", nxt)
        saved.append(dict(x=x, x1=x1, memn=memn, kv=kv, h=h, proj=proj, za=za, zb=zb, o=o, ya=ya, yb=yb, yc=yc,
                          mg=mg, cb=cb, cv=cv, ub=ub, h2=h2, up=up, zf=zf, uc=uc, w_in=w_in, wsq=one(wsq), w_kv=w_kv,
                          w_up=w_up, w_down=one(w_down), conv=conv))
        x = x2
        w_in = w_in_next[0] if w_in_next else None
    dx, loss, dg_final = _final_loss(x, row(small["norm_final_g"]), target, "final_loss")
    sgrads, pieces = [None] * depth, [None] * depth
    above = None
    first, second, rest = SQUARES, ("w_up",), ("w_in", "w_kv", "w_down")
    for l in reversed(range(depth)):
        s = saved[l]
        conv = s["conv"]
        bottom = l == 0
        (duc, dx2b, dconv_f), got = _ffn_down_bwd(dx, s["up"], s["uc"], s["w_down"], 0, f"ffn_down_bwd_{l}",
                                                  above and above.swap_program())
        if above:
            above.swapped(got)
        (dup,), got = _ffn_conv_bwd(duc, conv["f"], 0, f"ffn_conv_bwd_{l}", above and above.scatter_program(first))
        if above:
            above.scattered(got, first)
        (dx1, dx1b, dg_ffn), got = _nt_matmul_norm_bwd(
            dup, s["w_up"], 2, s["x1"], row(small["norm_ffn_g"][l]), dx, f"up_proj_bwd_{l}",
            above and above.scatter_program(second))
        if above:
            above.scattered(got, second)
        grads = dict(w_up=_dw_matmul(s["h2"], dup, tn_dw_up, f"dw_up_{l}")[0][0],
                     w_down=_dw_matmul(s["zf"], dx2b, d, f"dw_down_{l}")[0][0])
        ffn = _GradReduce(l, grads, core) if bottom else None
        (dpre, dya, dyb, dyc, dkv, dbg, dlng, dlnb, dcbias, dconv_a, dconv_b), got = _mixer_bwd(
            dx1b, s["proj"], s["cv"], s["ub"], s["ya"], s["yb"], s["yc"], s["cb"], s["kv"], conv["a"], conv["b"],
            row(small["ln_b_g"][l]), row(small["ln_b_b"][l]), row(small["b_gate"][l]), s["wsq"], 0,
            f"mixer_bwd_{l}", _join([above and above.scatter_program(rest), ffn and ffn.swap_program()]))
        if above:
            above.scattered(got[:len(rest)], rest)
            pieces[above.l] = above.pieces
            got = got[len(rest):]
        if ffn:
            ffn.swapped(got)
        dw_kv, dg_mem = _mem_kv_bwd(dkv, s["memn"], mem, row(small["norm_mem_g"][l]), s["w_kv"], f"mem_kv_bwd_{l}")
        mix_grads = dict(w_a_out=_dw_matmul(s["za"], dya, tn_dw_sq, f"dw_a_out_{l}")[0][0],
                         w_b_out=_dw_matmul(s["zb"], dyb, tn_dw_sq, f"dw_b_out_{l}")[0][0],
                         w_att_out=_dw_matmul(s["o"], dyc, tn_dw_sq, f"dw_att_out_{l}")[0][0],
                         w_o=_dw_matmul(s["mg"], dx1b, tn_dw_sq, f"dw_o_{l}")[0][0], w_kv=dw_kv)
        mix = _GradReduce(l, mix_grads, core) if bottom else None
        (dproj,), got = _inproj_conv_bwd(dpre, s["proj"], conv["a"], conv["b"], 0, f"inproj_conv_bwd_{l}",
                                         _join([ffn and ffn.scatter_program(), mix and mix.swap_program()]))
        if bottom:
            ffn.scattered(got[:len(ffn.names)])
            mix.swapped(got[len(ffn.names):])
        (dw_in,), got = _dw_matmul(s["h"], dproj, tn_dw_in, f"dw_in_{l}", mix and mix.scatter_program())
        in_grads = dict(w_in=dw_in)
        inp = _GradReduce(l, in_grads, core) if bottom else None
        if bottom:
            mix.scattered(got)
            inp.swapped(_run_comm(inp.swap_program(), f"swap_halves_w_in_{l}"))
        (dx0, _, dg_mix), got = _nt_matmul_norm_bwd(
            dproj, s["w_in"], 2, s["x"], row(small["norm_mix_g"][l]), dx1, f"in_proj_bwd_{l}",
            inp and inp.scatter_program())
        if bottom:
            inp.scattered(got)
            pieces[l] = {**ffn.pieces, **mix.pieces, **inp.pieces}
        else:
            above = _GradReduce(l, {**grads, **mix_grads, **in_grads}, core)
        sgrads[l] = dict(norm_mix_g=dg_mix, norm_mem_g=dg_mem, b_gate=dbg, conv_b_bias=dcbias, ln_b_g=dlng,
                         ln_b_b=dlnb, norm_ffn_g=dg_ffn, conv_a_w=dconv_a, conv_b_w=dconv_b, conv_ffn_w=dconv_f)
        dx = dx0
    return loss, dx, sgrads, dg_final, pieces


BIG = ("w_in", "w_a_out", "w_b_out", "w_att_out", "w_o", "w_kv", "w_up", "w_down")
COL = ("w_in", "w_kv", "w_up")
SQUARES = ("w_a_out", "w_b_out", "w_att_out", "w_o")
ANY = pl.BlockSpec(memory_space=pl.ANY)


def _place():
    x, y, c = lax.axis_index("x"), lax.axis_index("y"), lax.axis_index("c")
    chips = [(1 - x, y), (x, 1 - y), (1 - x, 1 - y)]
    return x, y, c, 2 * x + y, chips


def _remote(src, dst, send_sem, recv_sem, dev):
    return pltpu.make_async_remote_copy(src_ref=src, dst_ref=dst, send_sem=send_sem, recv_sem=recv_sem,
                                        device_id=dev, device_id_type=MESH)


class _Striped:
    def __init__(self, src, dst, make):
        rows = src.shape[-2]
        unit = 8 * (4 // jnp.dtype(src.dtype).itemsize)
        n = max(k for k in range(1, MAX_STRIPES + 1) if rows % (unit * k) == 0) if rows % unit == 0 else 1
        q = rows // n
        self.parts = [make(_window(src, pl.ds(i * q, q), slice(None)), _window(dst, pl.ds(i * q, q), slice(None)))
                      for i in range(n)]
        self.whole = make(src, dst)

    def start(self):
        for p in self.parts:
            p.start()

    def wait(self):
        self.whole.wait()

    def wait_send(self):
        self.whole.wait_send()

    def wait_recv(self):
        self.whole.wait_recv()


def _far(src, dst, send_sem, recv_sem, dev):
    return _Striped(src, dst, lambda s, d: _remote(s, d, send_sem, recv_sem, dev))


def _near(src, dst, sem):
    return _Striped(src, dst, lambda s, d: pltpu.make_async_copy(s, d, sem))


def _window(ref, rows, cols):
    return ref.at[(slice(None),) * (len(ref.shape) - 2) + (rows, cols)]


def _row_tile(rows, cols, unit=16, limit=1 << 20):
    best = unit
    for tr in range(unit, rows + 1, unit):
        if rows % tr == 0 and tr * cols <= limit:
            best = tr
    return best


def _cast_place(ws, l, kind, chip, name):
    _, k, n = ws[0].shape
    if kind == "col":
        shape, spec = (N_CHIPS, k, n), pl.BlockSpec((None, k, n), lambda i, c: (c[0], 0, 0))
    elif len(ws) == 1:
        shape, spec = (N_CHIPS * k, n), pl.BlockSpec((k, n), lambda i, c: (c[0], 0))
    else:
        shape, spec = (len(ws), N_CHIPS * k, n), pl.BlockSpec((len(ws), k, n), lambda i, c: (0, c[0], 0))

    def body(c_ref, *refs):
        o_ref = refs[-1]
        if len(ws) == 1 or kind == "col":
            o_ref[...] = refs[0][...].astype(BF)
        else:
            for i in range(len(ws)):
                o_ref[i] = refs[i][...].astype(BF)

    return _pallas_call(
        body, name=name,
        grid_spec=pltpu.PrefetchScalarGridSpec(
            num_scalar_prefetch=1, grid=(1,),
            in_specs=[pl.BlockSpec((None, k, n), lambda i, c: (l, 0, 0))] * len(ws), out_specs=spec),
        out_shape=jax.ShapeDtypeStruct(shape, BF),
        compiler_params=_params(1),
    )(chip, *ws)


def _gather_program(arrays, kinds, part=(0, 1)):
    n_t = len(arrays)
    index, count = part

    def shard_rows(f, kind):
        return f.shape[-2] if kind == "col" else f.shape[-2] // N_CHIPS

    def run(phase, ins, full, sems):
        ici_send, ici_recv, sib_send, sib_recv, loc_sem = sems
        x, y, c, me, chips = _place()
        sibling = (x, y, 1 - c)

        def part_of(i, chip, half):
            f, kind = full[i], kinds[i]
            if kind == "small":
                cols = ins[i].shape[-1]
                return _window(f, slice(None), pl.ds(pl.multiple_of(chip * cols, 128), cols))
            rows = shard_rows(f, kind)
            r = rows // (2 * count)
            at = (half * count + index) * r
            if kind == "col":
                return f.at[chip, pl.ds(pl.multiple_of(at, 16), r), :]
            return _window(f, pl.ds(pl.multiple_of(chip * rows + at, 16), r), slice(None))

        src_part = lambda i, half: ins[i] if kinds[i] == "small" else part_of(i, me, half)
        dst_part = part_of
        local = [_near(ins[i], part_of(i, me, c), loc_sem.at[i]) for i in range(n_t) if kinds[i] == "small"]
        sends = []
        for i in range(n_t):
            for j, chip in enumerate(chips):
                sends.append(_far(src_part(i, c), dst_part(i, me, c), ici_send.at[3 * i + j],
                                  ici_recv.at[3 * i + j], (*chip, c)))
        if phase == "start":
            for cp in local + sends:
                cp.start()
            return
        passed = []
        for i in range(n_t):
            for j, chip in enumerate(chips):
                k = 2 * chip[0] + chip[1]
                landed = dst_part(i, k, c)
                if phase == "forward":
                    _remote(landed, landed, ici_send.at[3 * i + j], ici_recv.at[3 * i + j], (*chip, c)).wait_recv()
                if kinds[i] != "small":
                    passed.append(_far(landed, landed, sib_send.at[3 * i + j], sib_recv.at[3 * i + j], sibling))
                    if phase == "forward":
                        passed[-1].start()
        if phase == "forward":
            return
        for i in range(n_t):
            if kinds[i] == "small":
                continue
            for j, chip in enumerate(chips):
                k = 2 * chip[0] + chip[1]
                other = dst_part(i, k, 1 - c)
                _remote(other, other, sib_send.at[3 * i + j], sib_recv.at[3 * i + j], sibling).wait_recv()
        for cp in sends + passed:
            cp.wait_send()
        for cp in local:
            cp.wait()

    def out_shape(a, kind):
        shp = a.shape[:-1] + (a.shape[-1] * N_CHIPS,) if kind == "small" else a.shape
        return jax.ShapeDtypeStruct(shp, a.dtype)

    outs = [out_shape(a, k) for a, k in zip(arrays, kinds)]
    sems = [pltpu.SemaphoreType.DMA((3 * n_t,))] * 4 + [pltpu.SemaphoreType.DMA((n_t,))]
    return _Comm(arrays, outs, sems, run, {i: i for i in range(n_t) if kinds[i] != "small"})


def _all_reduce_small(part, name):
    r, n = part.shape

    def body(in_ref, out_ref, pair_ref, chips_ref, sib_sems, send_sems, recv_sems):
        x, y, c, me, chips = _place()
        pair_ref[c] = in_ref[...]
        swap = _remote(in_ref, pair_ref.at[c], sib_sems.at[0], sib_sems.at[1], (x, y, 1 - c))
        swap.start()
        _remote(in_ref, pair_ref.at[1 - c], sib_sems.at[0], sib_sems.at[1], (x, y, c)).wait_recv()
        chips_ref[me] = pair_ref[0] + pair_ref[1]
        sends = [_remote(chips_ref.at[me], chips_ref.at[me], send_sems.at[j], recv_sems.at[j], (*chip, c))
                 for j, chip in enumerate(chips)]
        for cp in sends:
            cp.start()
        for j, chip in enumerate(chips):
            landed = chips_ref.at[2 * chip[0] + chip[1]]
            _remote(landed, landed, send_sems.at[j], recv_sems.at[j], (x, y, c)).wait_recv()
        for cp in sends:
            cp.wait_send()
        swap.wait_send()
        total = chips_ref[0]
        for k in range(1, N_CHIPS):
            total = total + chips_ref[k]
        out_ref[...] = total

    vm = pl.BlockSpec(memory_space=pltpu.VMEM)
    return _pallas_call(
        body, name=name, in_specs=[vm], out_specs=vm, out_shape=jax.ShapeDtypeStruct((r, n), F32),
        scratch_shapes=[pltpu.VMEM((2, r, n), F32), pltpu.VMEM((N_CHIPS, r, n), F32),
                        pltpu.SemaphoreType.DMA((2,)), pltpu.SemaphoreType.DMA((N_CHIPS - 1,)),
                        pltpu.SemaphoreType.DMA((N_CHIPS - 1,))],
        compiler_params=pltpu.CompilerParams(vmem_limit_bytes=VMEM_LIMIT),
    )(part)


def _halves_view(name, dw):
    k, n = dw.shape
    s = 1 if name in COL else N_CHIPS
    return dw.reshape(s, 2, k // (2 * s), n)


def _swap_program(views):
    n_t = len(views)

    def run(phase, src, dst, sems):
        send_sems, recv_sems = sems
        x, y, c, _, _ = _place()
        copies = [_far(src[i].at[:, 1 - c], dst[i], send_sems.at[i], recv_sems.at[i], (x, y, 1 - c))
                  for i in range(n_t)]
        for cp in copies:
            if phase == "start":
                cp.start()
            elif phase == "finish":
                cp.wait()

    outs = [jax.ShapeDtypeStruct((v.shape[0],) + v.shape[2:], v.dtype) for v in views]
    sems = [pltpu.SemaphoreType.DMA((n_t,)), pltpu.SemaphoreType.DMA((n_t,))]
    return _Comm(views, outs, sems, run)


def _add_halves(view, got, place, col, name):
    s, _, r, n = view.shape
    if col:
        cw = n // N_CHIPS
        tr = _row_tile(r, cw)
        grid = (r // tr, N_CHIPS)
        in_specs = [pl.BlockSpec((None, None, tr, cw), lambda j, q, p: (0, p[0], j, q)),
                    pl.BlockSpec((None, tr, cw), lambda j, q, p: (0, j, q))]
        out_specs = [pl.BlockSpec((None, tr, cw), lambda j, q, p: (0, j, q)),
                     pl.BlockSpec((None, None, tr, cw), lambda j, q, p: (p[0], p[1], j, 0))]
    else:
        cw = n
        tr = _row_tile(r, n)
        grid = (s, r // tr)
        in_specs = [pl.BlockSpec((None, None, tr, n), lambda i, j, p: (i, p[0], j, 0)),
                    pl.BlockSpec((None, tr, n), lambda i, j, p: (i, j, 0))]
        out_specs = [pl.BlockSpec((None, tr, n), lambda i, j, p: (i, j, 0)),
                     pl.BlockSpec((None, None, tr, n), lambda i, j, p: (p[0], i, j, 0))]

    def body(p_ref, a_ref, b_ref, o_ref, z_ref):
        total = (a_ref[...].astype(F32) + b_ref[...].astype(F32)).astype(BF)
        o_ref[...] = total
        if col:
            @pl.when(pl.program_id(1) == p_ref[1])
            def _():
                z_ref[...] = total
        else:
            z_ref[...] = total

    return _pallas_call(
        body, name=name,
        grid_spec=pltpu.PrefetchScalarGridSpec(num_scalar_prefetch=1, grid=grid, in_specs=in_specs,
                                               out_specs=out_specs),
        out_shape=[jax.ShapeDtypeStruct((s, r, n), BF), jax.ShapeDtypeStruct((2, N_CHIPS, r, cw), BF)],
        compiler_params=_params(2),
    )(place, view, got)


def _scatter_program(sums, landing, names, part=(0, 1)):
    n_t = len(sums)
    index, count = part

    def rows(ref):
        q = ref.shape[-2] // count
        return ref if count == 1 else _window(ref, pl.ds(index * q, q), slice(None))

    def run(phase, src, dst, sems):
        ici_send, ici_recv, sib_send, sib_recv = sems
        x, y, c, me, chips = _place()
        sibling = (x, y, 1 - c)

        def piece(i, chip):
            if names[i] in COL:
                cw = src[i].shape[2] // N_CHIPS
                return src[i].at[0, :, pl.ds(pl.multiple_of(chip * cw, 128), cw)]
            return src[i].at[chip]

        local = []
        sends = []
        for i in range(n_t):
            sends.append(_far(rows(piece(i, me)), rows(dst[i].at[c, me]), sib_send.at[4 * i + 3],
                              sib_recv.at[4 * i + 3], sibling))
            for j, chip in enumerate(chips):
                k = 2 * chip[0] + chip[1]
                sends.append(_far(rows(piece(i, k)), rows(dst[i].at[c, me]), ici_send.at[3 * i + j],
                                  ici_recv.at[3 * i + j], (*chip, c)))
        if phase == "start":
            for cp in local + sends:
                cp.start()
            return
        passed = []
        for i in range(n_t):
            for j, chip in enumerate(chips):
                k = 2 * chip[0] + chip[1]
                landed = rows(dst[i].at[c, k])
                if phase == "forward":
                    _remote(landed, landed, ici_send.at[3 * i + j], ici_recv.at[3 * i + j], (*chip, c)).wait_recv()
                passed.append(_far(landed, landed, sib_send.at[4 * i + j], sib_recv.at[4 * i + j], sibling))
                if phase == "forward":
                    passed[-1].start()
        if phase == "forward":
            return
        for i in range(n_t):
            other = rows(dst[i].at[1 - c, me])
            _remote(other, other, sib_send.at[4 * i + 3], sib_recv.at[4 * i + 3], sibling).wait_recv()
            for j, chip in enumerate(chips):
                k = 2 * chip[0] + chip[1]
                other = rows(dst[i].at[1 - c, k])
                _remote(other, other, sib_send.at[4 * i + j], sib_recv.at[4 * i + j], sibling).wait_recv()
        for cp in sends + passed:
            cp.wait_send()
        for cp in local:
            cp.wait()

    outs = [jax.ShapeDtypeStruct(z.shape, z.dtype) for z in landing]
    sems = [pltpu.SemaphoreType.DMA((3 * n_t,))] * 2 + [pltpu.SemaphoreType.DMA((4 * n_t,))] * 2
    return _Comm(list(sums) + list(landing), outs, sems, run, {n_t + i: i for i in range(n_t)})


def _adamw(w, g, m, v):
    m = ADAM_B1 * m + (1.0 - ADAM_B1) * g
    v = ADAM_B2 * v + (1.0 - ADAM_B2) * (g * g)
    m_hat = m / (1.0 - ADAM_B1 ** ADAM_STEP)
    v_hat = v / (1.0 - ADAM_B2 ** ADAM_STEP)
    return -ADAM_LR * (m_hat / (jnp.sqrt(v_hat) + ADAM_EPS) + ADAM_WD * w), m, v


def _adam_shard(pieces, w, m, v, l, prev, name, comm=None):
    depth, rows, cw = w.shape
    hr = rows // 2
    tr = _row_tile(hr, cw, limit=1 << 19)
    n_i = hr // tr

    def body(*refs):
        z_ref, w_ref, m_ref, v_ref = refs[:4]
        g_ref, d_ref, nm_ref, nv_ref = refs[-4:]
        g = z_ref[0].astype(F32)
        for k in range(1, N_CHIPS):
            g = g + z_ref[k].astype(F32)
        g_ref[...] = g
        d_ref[...], nm_ref[...], nv_ref[...] = _adamw(w_ref[...], g, m_ref[...], v_ref[...])

    par = pl.BlockSpec((None, tr, cw), lambda h, i: (l, h * n_i + i, 0))
    out = jax.ShapeDtypeStruct((depth, rows, cw), F32)
    extra = [] if prev is None else list(prev)
    return _run(
        body, comm, name=name, grid=(2, n_i),
        in_specs=[pl.BlockSpec((None, N_CHIPS, tr, cw), lambda h, i: (h, 0, i, 0)), par, par, par] + [ANY] * len(extra),
        out_specs=[par] * 4, out_shape=[out] * 4, args=(pieces, w, m, v, *extra),
        aliases={4 + k: k for k in range(len(extra))})


def _adam_small(gs, ws, ms, vs, name):
    n = len(gs)

    def body(*refs):
        g, w, m, v, d, nm, nv = (refs[k * n:(k + 1) * n] for k in range(7))
        for i in range(n):
            d[i][...], nm[i][...], nv[i][...] = _adamw(w[i][...], g[i][...], m[i][...], v[i][...])

    vm = pl.BlockSpec(memory_space=pltpu.VMEM)
    outs = [jax.ShapeDtypeStruct(a.shape, F32) for a in ws] * 3
    res = _pallas_call(body, name=name, in_specs=[vm] * (4 * n), out_specs=[vm] * (3 * n), out_shape=outs)(
        *gs, *ws, *ms, *vs)
    return res[:n], res[n:2 * n], res[2 * n:]


WEIGHTS = ("norm_mix_g", "norm_mem_g", "w_in", "b_gate", "conv_a_w", "w_a_out", "conv_b_w", "conv_b_bias", "ln_b_g",
           "ln_b_b", "w_b_out", "w_kv", "w_att_out", "w_o", "norm_ffn_g", "w_up", "conv_ffn_w", "w_down",
           "norm_final_g")
REPLICATED = ("norm_mix_g", "norm_mem_g", "b_gate", "conv_b_bias", "ln_b_g", "ln_b_b", "norm_ffn_g")
CONVS = ("conv_a_w", "conv_b_w", "conv_ffn_w")
PACK_WIDTH = 1024


def _pack(arrays):
    flat = jnp.concatenate([a.reshape(-1) for a in arrays])
    size = -(-flat.shape[0] // (8 * PACK_WIDTH)) * (8 * PACK_WIDTH)
    return jnp.pad(flat, (0, size - flat.shape[0])).reshape(-1, PACK_WIDTH)


def _unpack(packed, shapes):
    flat = packed.reshape(-1)
    out, at = [], 0
    for shp in shapes:
        n = math.prod(shp)
        out.append(flat[at:at + n].reshape(shp))
        at += n
    return out


def kernel(x, mem, norm_mix_g, norm_mem_g, w_in, b_gate, conv_a_w, w_a_out, conv_b_w, conv_b_bias, ln_b_g, ln_b_b, w_b_out, w_kv, w_att_out, w_o, norm_ffn_g, w_up, conv_ffn_w, w_down, norm_final_g, loss_target, m_norm_mix_g, m_norm_mem_g, m_w_in, m_b_gate, m_conv_a_w, m_w_a_out, m_conv_b_w, m_conv_b_bias, m_ln_b_g, m_ln_b_b, m_w_b_out, m_w_kv, m_w_att_out, m_w_o, m_norm_ffn_g, m_w_up, m_conv_ffn_w, m_w_down, m_norm_final_g, v_norm_mix_g, v_norm_mem_g, v_w_in, v_b_gate, v_conv_a_w, v_w_a_out, v_conv_b_w, v_conv_b_bias, v_ln_b_g, v_ln_b_b, v_w_b_out, v_w_kv, v_w_att_out, v_w_o, v_norm_ffn_g, v_w_up, v_conv_ffn_w, v_w_down, v_norm_final_g):
    w = dict(norm_mix_g=norm_mix_g, norm_mem_g=norm_mem_g, w_in=w_in, b_gate=b_gate, conv_a_w=conv_a_w,
             w_a_out=w_a_out, conv_b_w=conv_b_w, conv_b_bias=conv_b_bias, ln_b_g=ln_b_g, ln_b_b=ln_b_b,
             w_b_out=w_b_out, w_kv=w_kv, w_att_out=w_att_out, w_o=w_o, norm_ffn_g=norm_ffn_g, w_up=w_up,
             conv_ffn_w=conv_ffn_w, w_down=w_down, norm_final_g=norm_final_g)
    mom = dict(norm_mix_g=m_norm_mix_g, norm_mem_g=m_norm_mem_g, w_in=m_w_in, b_gate=m_b_gate, conv_a_w=m_conv_a_w,
               w_a_out=m_w_a_out, conv_b_w=m_conv_b_w, conv_b_bias=m_conv_b_bias, ln_b_g=m_ln_b_g, ln_b_b=m_ln_b_b,
               w_b_out=m_w_b_out, w_kv=m_w_kv, w_att_out=m_w_att_out, w_o=m_w_o, norm_ffn_g=m_norm_ffn_g,
               w_up=m_w_up, conv_ffn_w=m_conv_ffn_w, w_down=m_w_down, norm_final_g=m_norm_final_g)
    var = dict(norm_mix_g=v_norm_mix_g, norm_mem_g=v_norm_mem_g, w_in=v_w_in, b_gate=v_b_gate, conv_a_w=v_conv_a_w,
               w_a_out=v_w_a_out, conv_b_w=v_conv_b_w, conv_b_bias=v_conv_b_bias, ln_b_g=v_ln_b_g, ln_b_b=v_ln_b_b,
               w_b_out=v_w_b_out, w_kv=v_w_kv, w_att_out=v_w_att_out, w_o=v_w_o, norm_ffn_g=v_norm_ffn_g,
               w_up=v_w_up, conv_ffn_w=v_conv_ffn_w, w_down=v_w_down, norm_final_g=v_norm_final_g)
    depth = w_in.shape[0]
    chip = 2 * lax.axis_index("x") + lax.axis_index("y")
    core = jnp.stack([lax.axis_index("c"), chip]).astype(jnp.int32)

    chip1 = chip.astype(jnp.int32).reshape(1)
    layers = range(depth)
    sh = dict(w_in=[_cast_place([w_in], l, "col", chip1, f"cast_w_in_{l}") for l in layers],
              wsq=[_cast_place([w[n] for n in SQUARES], l, "row", chip1, f"cast_squares_{l}") for l in layers],
              w_kv=[_cast_place([w_kv], l, "col", chip1, f"cast_w_kv_{l}") for l in layers],
              w_up=[_cast_place([w_up], l, "col", chip1, f"cast_w_up_{l}") for l in layers],
              w_down=[_cast_place([w_down], l, "row", chip1, f"cast_w_down_{l}") for l in layers])
    conv_sh = [jnp.concatenate([conv_a_w, conv_b_w], axis=1), conv_ffn_w]
    small = {n: w[n] for n in REPLICATED + ("norm_final_g",)}

    loss, dx, sgrads, dg_final, pieces = _step(x[0], mem[0], loss_target[0], sh, conv_sh, small, core)

    res = {n: None for n in BIG}
    for l in reversed(range(depth)):
        for n in BIG:
            res[n], _ = _adam_shard(pieces[l][n], w[n], mom[n], var[n], l, res[n], f"adam_{n}_{l}")

    per_layer = REPLICATED + CONVS
    parts = [sgrads[l][n] for l in range(depth) for n in per_layer] + [dg_final, loss[0:1, 0:1]]
    total = _all_reduce_small(_pack(parts), "all_reduce_small")
    summed = _unpack(total, [p.shape for p in parts])
    g_small = {}
    for k, n in enumerate(per_layer):
        full = jnp.stack([summed[l * len(per_layer) + k] for l in range(depth)])
        if n in CONVS:
            cols = w[n].shape[-1]
            full = lax.dynamic_slice_in_dim(full, chip * cols, cols, axis=2)
        g_small[n] = full.reshape(w[n].shape)
    g_small["norm_final_g"] = summed[-2].reshape(norm_final_g.shape)
    names = per_layer + ("norm_final_g",)
    two_d = lambda a: a.reshape(1, -1) if a.ndim == 1 else a
    deltas, new_ms, new_vs = _adam_small(*[[two_d(t[n]) for n in names] for t in (g_small, w, mom, var)],
                                         "adam_small")
    for n, dl, nm, nv in zip(names, deltas, new_ms, new_vs):
        res[n] = (g_small[n], dl.reshape(w[n].shape), nm.reshape(w[n].shape), nv.reshape(w[n].shape))

    loss = summed[-1].reshape(())
    return (loss, dx.reshape(x.shape), *[res[n][0] for n in WEIGHTS], *[res[n][1] for n in WEIGHTS],
            *[res[n][2] for n in WEIGHTS], *[res[n][3] for n in WEIGHTS])
```
